```python
import math
import numpy as np
import jax
import jax.numpy as jnp
from jax import lax

D_MODEL = 2048
BATCH = 4
SEQ = 2048
DEPTH = 1

HEAD_DIM = 128
N_FOX_HEADS = 8
N_NSA_HEADS = 8
N_NSA_KV_HEADS = 2
NSA_GROUP = N_NSA_HEADS // N_NSA_KV_HEADS
FOX_WIDTH = N_FOX_HEADS * HEAD_DIM
NSA_WIDTH = N_NSA_HEADS * HEAD_DIM
NSA_KV_WIDTH = N_NSA_KV_HEADS * HEAD_DIM
MIX_WIDTH = FOX_WIDTH + NSA_WIDTH
N_BRANCH = 3
CMP_BLOCK = 32
CMP_STRIDE = 16
SEL_BLOCK = 64
N_SEL = 8
WINDOW = 512
Q_BLOCK = 128
N_BUCKETS = 32
MAX_DISTANCE = 128
D_FF = -(-(8 * D_MODEL) // (3 * 256)) * 256
N_MOD = 6
IN_SPLITS = (FOX_WIDTH, FOX_WIDTH, FOX_WIDTH, N_FOX_HEADS, NSA_WIDTH,
             N_BRANCH * NSA_KV_WIDTH, N_BRANCH * NSA_KV_WIDTH, N_BRANCH * N_NSA_HEADS)
IN_WIDTH = 3 * FOX_WIDTH + N_FOX_HEADS + NSA_WIDTH + 2 * N_BRANCH * NSA_KV_WIDTH + N_BRANCH * N_NSA_HEADS
SCALE = HEAD_DIM ** -0.5
EPS = 1e-6
NEG = -1e30
FORCE = 1e6

kernel_name = 'hymba_fox_nsa_adaln_block'


def rms_norm(x, gain):
    xf = x.astype(jnp.float32)
    y = xf * lax.rsqrt(jnp.mean(xf * xf, axis=-1, keepdims=True) + EPS)
    return (y * gain.astype(jnp.float32)).astype(x.dtype)


def masked_softmax(logits, valid):
    p = jax.nn.softmax(jnp.where(valid, logits, NEG), axis=-1)
    return jnp.where(valid, p, 0.0)


def t5_bucket(dist):
    n = jnp.maximum(dist, 0)
    max_exact = N_BUCKETS // 2
    nf = jnp.maximum(n, 1).astype(jnp.float32)
    large = max_exact + (jnp.log(nf / max_exact) / math.log(MAX_DISTANCE / max_exact)
                         * (N_BUCKETS - max_exact)).astype(jnp.int32)
    large = jnp.minimum(large, N_BUCKETS - 1)
    return jnp.where(n < max_exact, n, large)


def to_blocks(a):
    b, s = a.shape[:2]
    return jnp.moveaxis(a.reshape(b, s // Q_BLOCK, Q_BLOCK, *a.shape[2:]), 1, 0)


def from_blocks(a):
    nb, b, qb = a.shape[:3]
    return jnp.moveaxis(a, 0, 1).reshape(b, nb * qb, *a.shape[3:])


def forgetting_attention(q, k, v, f_logit, f_bias):
    b, s = q.shape[:2]
    log_f = jax.nn.log_sigmoid(f_logit.astype(jnp.float32) + f_bias.astype(jnp.float32))
    cum = jnp.cumsum(log_f, axis=1)
    cum_k = jnp.transpose(cum, (0, 2, 1))
    kpos = jnp.arange(s)

    def one_block(args):
        qb, cq, q0 = args
        qpos = q0 + jnp.arange(Q_BLOCK)
        logits = jnp.einsum('bqhd,bkhd->bhqk', qb, k).astype(jnp.float32) * SCALE
        logits = logits + jnp.transpose(cq, (0, 2, 1))[..., None] - cum_k[:, :, None, :]
        p = masked_softmax(logits, kpos[None, :] <= qpos[:, None])
        return jnp.einsum('bhqk,bkhd->bqhd', p.astype(v.dtype), v)

    starts = jnp.arange(s // Q_BLOCK, dtype=jnp.int32) * Q_BLOCK
    return from_blocks(lax.map(one_block, (to_blocks(q), to_blocks(cum), starts)))


def compress_blocks(kv, pos_emb, w1, w2):
    b, s = kv.shape[:2]
    n_cmp = (s - CMP_BLOCK) // CMP_STRIDE + 1
    idx = np.arange(n_cmp)[:, None] * CMP_STRIDE + np.arange(CMP_BLOCK)[None, :]
    blocks = kv[:, idx] + pos_emb[None, None, :, None, :]
    flat = jnp.transpose(blocks, (0, 1, 3, 2, 4)).reshape(b, n_cmp, N_NSA_KV_HEADS, CMP_BLOCK * HEAD_DIM)
    return jax.nn.silu(flat @ w1) @ w2


def selection_matrix(n_cmp, n_slc):
    r, q = SEL_BLOCK // CMP_STRIDE, CMP_BLOCK // CMP_STRIDE
    tgt = (r * np.arange(n_slc)[:, None, None] + np.arange(r)[None, :, None]
           - np.arange(q)[None, None, :]).reshape(n_slc, -1)
    return (np.arange(n_cmp)[:, None, None] == tgt[None]).sum(-1).astype(np.float32)


def native_sparse_attention(q, k_br, v_br, gates, k_gain, cmp_pos, cmp_w1, cmp_w2, rel_bias):
    b, s = q.shape[:2]
    hkv, grp = N_NSA_KV_HEADS, NSA_GROUP
    kc = rms_norm(compress_blocks(k_br[:, :, 0], cmp_pos[0], cmp_w1[0], cmp_w2[0]), k_gain[0])
    vc = compress_blocks(v_br[:, :, 0], cmp_pos[1], cmp_w1[1], cmp_w2[1])
    k_slc = rms_norm(k_br[:, :, 1], k_gain[1])
    k_win = rms_norm(k_br[:, :, 2], k_gain[2])
    n_cmp = kc.shape[1]
    n_slc = s // SEL_BLOCK
    top_n = min(N_SEL, n_slc)
    cmp_end = jnp.arange(n_cmp) * CMP_STRIDE + CMP_BLOCK - 1
    sel_m = jnp.asarray(selection_matrix(n_cmp, n_slc))
    ksb = k_slc.reshape(b, n_slc, SEL_BLOCK, hkv, HEAD_DIM).transpose(0, 3, 1, 2, 4)
    vsb = v_br[:, :, 1].reshape(b, n_slc, SEL_BLOCK, hkv, HEAD_DIM).transpose(0, 3, 1, 2, 4)
    pad = ((0, 0), (WINDOW, 0), (0, 0), (0, 0))
    kw = jnp.pad(k_win, pad)
    vw = jnp.pad(v_br[:, :, 2], pad)
    tb = rel_bias.reshape(N_BUCKETS, hkv, grp)
    b_ix = jnp.arange(b)[:, None, None, None]
    h_ix = jnp.arange(hkv)[None, None, :, None]
    blk_ids = jnp.arange(n_slc)

    def one_block(args):
        qb, gb, q0 = args
        qpos = q0 + jnp.arange(Q_BLOCK)
        qg = qb.reshape(b, Q_BLOCK, hkv, grp, HEAD_DIM)
        d_c = qpos[:, None] - cmp_end[None, :]
        s_c = jnp.einsum('bqhgd,bchd->bqhgc', qg, kc).astype(jnp.float32) * SCALE
        s_c = s_c + tb[t5_bucket(d_c)].transpose(0, 2, 3, 1)
        p_c = masked_softmax(s_c, (d_c >= 0)[:, None, None, :])
        o_c = jnp.einsum('bqhgc,bchd->bqhgd', p_c.astype(vc.dtype), vc)
        imp = jnp.einsum('bqhc,cj->bqhj', p_c.sum(axis=3), sel_m)
        cur = (qpos // SEL_BLOCK)[:, None]
        forced = (blk_ids[None, :] == 0) | (blk_ids[None, :] == cur) | (blk_ids[None, :] == cur - 1)
        imp = jnp.where(forced[None, :, None, :], FORCE, imp)
        imp = jnp.where((blk_ids[None, :] <= cur)[None, :, None, :], imp, -jnp.inf)
        top_s, top_i = lax.top_k(imp, top_n)
        ks = ksb[b_ix, h_ix, top_i].reshape(b, Q_BLOCK, hkv, top_n * SEL_BLOCK, HEAD_DIM)
        vs = vsb[b_ix, h_ix, top_i].reshape(b, Q_BLOCK, hkv, top_n * SEL_BLOCK, HEAD_DIM)
        spos = (top_i[..., None] * SEL_BLOCK + jnp.arange(SEL_BLOCK)).reshape(b, Q_BLOCK, hkv, -1)
        d_s = qpos[None, :, None, None] - spos
        valid_s = (d_s >= 0) & jnp.repeat(jnp.isfinite(top_s), SEL_BLOCK, axis=-1)
        s_s = jnp.einsum('bqhgd,bqhkd->bqhgk', qg, ks).astype(jnp.float32) * SCALE
        s_s = s_s + tb[t5_bucket(d_s), h_ix].transpose(0, 1, 2, 4, 3)
        p_s = masked_softmax(s_s, valid_s[:, :, :, None, :])
        o_s = jnp.einsum('bqhgk,bqhkd->bqhgd', p_s.astype(vs.dtype), vs)
        kwb = lax.dynamic_slice_in_dim(kw, q0, WINDOW + Q_BLOCK, axis=1)
        vwb = lax.dynamic_slice_in_dim(vw, q0, WINDOW + Q_BLOCK, axis=1)
        wpos = q0 - WINDOW + jnp.arange(WINDOW + Q_BLOCK)
        d_w = qpos[:, None] - wpos[None, :]
        valid_w = (d_w >= 0) & (d_w < WINDOW) & (wpos[None, :] >= 0)
        s_w = jnp.einsum('bqhgd,bkhd->bqhgk', qg, kwb).astype(jnp.float32) * SCALE
        s_w = s_w + tb[t5_bucket(d_w)].transpose(0, 2, 3, 1)
        p_w = masked_softmax(s_w, valid_w[:, None, None, :])
        o_w = jnp.einsum('bqhgk,bkhd->bqhgd', p_w.astype(vwb.dtype), vwb)
        g = gb.reshape(b, Q_BLOCK, hkv, grp, N_BRANCH)
        o = g[..., 0:1] * o_c + g[..., 1:2] * o_s + g[..., 2:3] * o_w
        return o.reshape(b, Q_BLOCK, N_NSA_HEADS, HEAD_DIM)

    starts = jnp.arange(s // Q_BLOCK, dtype=jnp.int32) * Q_BLOCK
    return from_blocks(lax.map(one_block, (to_blocks(q), to_blocks(gates), starts)))


def setup_inputs(seed: int = 0) -> dict:
    key = jax.random.key(seed)
    ks = jax.random.split(key, 20)
    nrm = lambda k, shape, sc: jax.random.normal(k, shape, jnp.float32) * sc
    gain = lambda k, shape: 1.0 + 0.02 * jax.random.normal(k, shape, jnp.float32)
    return {
        'x': nrm(ks[0], (BATCH, SEQ, D_MODEL), 1.0),
        'c': nrm(ks[1], (BATCH, D_MODEL), 1.0),
        'ada_w': nrm(ks[2], (DEPTH, D_MODEL, N_MOD * D_MODEL), 0.02),
        'ada_b': nrm(ks[3], (DEPTH, N_MOD * D_MODEL), 0.02),
        'norm1_gain': gain(ks[4], (DEPTH, D_MODEL)),
        'norm2_gain': gain(ks[5], (DEPTH, D_MODEL)),
        'w_in': nrm(ks[6], (DEPTH, D_MODEL, IN_WIDTH), D_MODEL ** -0.5),
        'fox_f_bias': jax.random.uniform(ks[7], (DEPTH, N_FOX_HEADS), jnp.float32, 1.0, 4.0),
        'fox_q_gain': gain(ks[8], (DEPTH, HEAD_DIM)),
        'fox_k_gain': gain(ks[9], (DEPTH, HEAD_DIM)),
        'nsa_q_gain': gain(ks[10], (DEPTH, HEAD_DIM)),
        'nsa_k_gain': gain(ks[11], (DEPTH, N_BRANCH, HEAD_DIM)),
        'nsa_cmp_pos': nrm(ks[12], (DEPTH, 2, CMP_BLOCK, HEAD_DIM), 0.1),
        'nsa_cmp_w1': nrm(ks[13], (DEPTH, 2, CMP_BLOCK * HEAD_DIM, HEAD_DIM), (CMP_BLOCK * HEAD_DIM) ** -0.5),
        'nsa_cmp_w2': nrm(ks[14], (DEPTH, 2, HEAD_DIM, HEAD_DIM), HEAD_DIM ** -0.5),
        'rel_bias': nrm(ks[15], (N_BUCKETS, N_NSA_HEADS), 0.1),
        'mix_out_gain': gain(ks[16], (DEPTH, MIX_WIDTH)),
        'w_out': nrm(ks[17], (DEPTH, MIX_WIDTH, D_MODEL), MIX_WIDTH ** -0.5),
        'ffn_w_gate': nrm(ks[18], (DEPTH, D_MODEL, D_FF), D_MODEL ** -0.5),
        'ffn_w_up': nrm(jax.random.fold_in(ks[18], 1), (DEPTH, D_MODEL, D_FF), D_MODEL ** -0.5),
        'ffn_w_down': nrm(ks[19], (DEPTH, D_FF, D_MODEL), D_FF ** -0.5),
    }


def reference(x, c, ada_w, ada_b, norm1_gain, norm2_gain, w_in, fox_f_bias, fox_q_gain, fox_k_gain,
              nsa_q_gain, nsa_k_gain, nsa_cmp_pos, nsa_cmp_w1, nsa_cmp_w2, rel_bias, mix_out_gain,
              w_out, ffn_w_gate, ffn_w_up, ffn_w_down):
    b, s = x.shape[:2]
    split_points = np.cumsum(IN_SPLITS)[:-1].tolist()
    for layer in range(DEPTH):
        mod = (jax.nn.silu(c) @ ada_w[layer] + ada_b[layer])[:, None, :]
        sh1, sc1, g1, sh2, sc2, g2 = jnp.split(mod, N_MOD, axis=-1)
        h = rms_norm(x, norm1_gain[layer]) * (1 + sc1) + sh1
        proj = h @ w_in[layer]
        fq, fk, fv, ff, nq, nk, nv, ng = jnp.split(proj, split_points, axis=-1)
        fq = rms_norm(fq.reshape(b, s, N_FOX_HEADS, HEAD_DIM), fox_q_gain[layer])
        fk = rms_norm(fk.reshape(b, s, N_FOX_HEADS, HEAD_DIM), fox_k_gain[layer])
        fv = fv.reshape(b, s, N_FOX_HEADS, HEAD_DIM)
        o_fox = forgetting_attention(fq, fk, fv, ff, fox_f_bias[layer])
        nq = rms_norm(nq.reshape(b, s, N_NSA_HEADS, HEAD_DIM), nsa_q_gain[layer])
        nk = nk.reshape(b, s, N_BRANCH, N_NSA_KV_HEADS, HEAD_DIM)
        nv = nv.reshape(b, s, N_BRANCH, N_NSA_KV_HEADS, HEAD_DIM)
        ng = jax.nn.sigmoid(ng.reshape(b, s, N_NSA_HEADS, N_BRANCH))
        o_nsa = native_sparse_attention(nq, nk, nv, ng, nsa_k_gain[layer], nsa_cmp_pos[layer],
                                        nsa_cmp_w1[layer], nsa_cmp_w2[layer], rel_bias)
        y = jnp.concatenate([
            rms_norm(o_fox.reshape(b, s, FOX_WIDTH), mix_out_gain[layer, :FOX_WIDTH]),
            rms_norm(o_nsa.reshape(b, s, NSA_WIDTH), mix_out_gain[layer, FOX_WIDTH:]),
        ], axis=-1)
        x = x + g1 * (y @ w_out[layer])
        h2 = rms_norm(x, norm2_gain[layer]) * (1 + sc2) + sh2
        ffn = (jax.nn.silu(h2 @ ffn_w_gate[layer]) * (h2 @ ffn_w_up[layer])) @ ffn_w_down[layer]
        x = x + g2 * ffn
    return x
```

```python
import functools
import math

import numpy as np
import jax
import jax.numpy as jnp
from jax import lax
from jax.experimental import pallas as pl
from jax.experimental.pallas import tpu as pltpu

HEAD_DIM = 128
N_FOX_HEADS = 8
N_NSA_HEADS = 8
N_NSA_KV_HEADS = 2
NSA_GROUP = N_NSA_HEADS // N_NSA_KV_HEADS
FOX_WIDTH = N_FOX_HEADS * HEAD_DIM
NSA_WIDTH = N_NSA_HEADS * HEAD_DIM
NSA_KV_WIDTH = N_NSA_KV_HEADS * HEAD_DIM
MIX_WIDTH = FOX_WIDTH + NSA_WIDTH
N_BRANCH = 3
CMP_BLOCK = 32
CMP_STRIDE = 16
SEL_BLOCK = 64
N_SEL = 8
WINDOW = 512
N_BUCKETS = 32
MAX_DISTANCE = 128
N_MOD = 6
SCALE = HEAD_DIM ** -0.5
EPS = 1e-6
NEG = -1e30
FORCE = 1e6

LANES = 128
GROUP_ROWS = NSA_GROUP * LANES
VMEM_LIMIT = 56 * 1024 * 1024

W16 = 3 * FOX_WIDTH + NSA_WIDTH + 4 * NSA_KV_WIDTH
W32 = 1024
COL_FF = 2 * NSA_KV_WIDTH
COL_NG = COL_FF + N_FOX_HEADS

F32 = jnp.float32
BF16 = jnp.bfloat16
NT_DIMS = (((1,), (1,)), ((), ()))


def _params(n_axes):
    return pltpu.CompilerParams(dimension_semantics=("arbitrary",) * n_axes,
                                vmem_limit_bytes=VMEM_LIMIT)


def _sigmoid(x):
    return 1.0 / (1.0 + jnp.exp(-x))


def _rms(x):
    return x * lax.rsqrt(jnp.mean(x * x, axis=-1, keepdims=True) + EPS)


def _ada_kernel(c_ref, w_ref, b_ref, o_ref):
    c = c_ref[...]
    s = (c * _sigmoid(c)).astype(BF16)
    o_ref[...] = jnp.dot(s, w_ref[...].astype(BF16), preferred_element_type=F32) + b_ref[...]


def _ada_call(c_pad, w, b):
    rows, d = c_pad.shape
    n = w.shape[1]
    tn = next(t for t in (1024, 768, 512, 384, 256, 128) if n % t == 0)
    return pl.pallas_call(
        _ada_kernel,
        grid=(n // tn,),
        in_specs=[pl.BlockSpec((rows, d), lambda j: (0, 0)),
                  pl.BlockSpec((d, tn), lambda j: (0, j)),
                  pl.BlockSpec((1, tn), lambda j: (0, j))],
        out_specs=pl.BlockSpec((rows, tn), lambda j: (0, j)),
        out_shape=jax.ShapeDtypeStruct((rows, n), F32),
        compiler_params=_params(1),
        name="adaln",
    )(c_pad, w, b)


def _proj_kernel(x_ref, sc_ref, sh_ref, g_ref, w_ref, gain_ref, flag_ref, o16_ref, o32_ref, h_scr, *, n16, tn):
    j = pl.program_id(1)

    @pl.when(j == 0)
    def _():
        h = _rms(x_ref[...]) * g_ref[...]
        h = h * (1.0 + sc_ref[0]) + sh_ref[0]
        h_scr[...] = h.astype(BF16)

    acc = jnp.dot(h_scr[...], w_ref[...], preferred_element_type=F32)

    @pl.when(j < n16)
    def _():
        for g in range(tn // LANES):
            cols = slice(g * LANES, (g + 1) * LANES)
            a = acc[:, cols]
            r = lax.rsqrt(jnp.mean(a * a, axis=-1, keepdims=True) + EPS)
            scale = jnp.where(flag_ref[:, cols] > 0.5, r, 1.0)
            o16_ref[:, cols] = (a * scale * gain_ref[:, cols]).astype(BF16)

    @pl.when(j >= n16)
    def _():
        o32_ref[...] = acc


def _proj_call(x2, sc, sh, gain1, w_perm, col_gain, col_flag, seq):
    t, d = x2.shape
    tm = min(512, seq)
    tn = W32
    n16 = W16 // tn
    rows_per_batch = seq // tm
    return pl.pallas_call(
        functools.partial(_proj_kernel, n16=n16, tn=tn),
        grid=(t // tm, n16 + 1),
        in_specs=[pl.BlockSpec((tm, d), lambda i, j: (i, 0)),
                  pl.BlockSpec((1, 1, d), lambda i, j: (i // rows_per_batch, 0, 0)),
                  pl.BlockSpec((1, 1, d), lambda i, j: (i // rows_per_batch, 0, 0)),
                  pl.BlockSpec((1, d), lambda i, j: (0, 0)),
                  pl.BlockSpec((d, tn), lambda i, j: (0, j)),
                  pl.BlockSpec((1, tn), lambda i, j: (0, jnp.minimum(j, n16 - 1))),
                  pl.BlockSpec((1, tn), lambda i, j: (0, jnp.minimum(j, n16 - 1)))],
        out_specs=[pl.BlockSpec((tm, tn), lambda i, j: (i, jnp.minimum(j, n16 - 1))),
                   pl.BlockSpec((tm, tn), lambda i, j: (i, 0))],
        out_shape=[jax.ShapeDtypeStruct((t, W16), BF16),
                   jax.ShapeDtypeStruct((t, W32), F32)],
        scratch_shapes=[pltpu.VMEM((tm, d), BF16)],
        compiler_params=_params(2),
        name="in_proj",
    )(x2, sc, sh, gain1, w_perm, col_gain, col_flag)


def _cum_kernel(ff_ref, fb_ref, o_ref, *, seq):
    ri = lax.broadcasted_iota(jnp.int32, (LANES, LANES), 0)
    ci = lax.broadcasted_iota(jnp.int32, (LANES, LANES), 1)
    tri = jnp.where(ri >= ci, 1.0, 0.0).astype(BF16)
    carry = jnp.zeros((1, LANES), F32)
    for blk in range(seq // LANES):
        rows = slice(blk * LANES, (blk + 1) * LANES)
        x = ff_ref[0, rows, :] + fb_ref[...]
        lf = jnp.minimum(x, 0.0) - jnp.log(1.0 + jnp.exp(-jnp.abs(x)))
        hi = lf.astype(BF16)
        r1 = lf - hi.astype(F32)
        mid = r1.astype(BF16)
        lo = (r1 - mid.astype(F32)).astype(BF16)
        c = (jnp.dot(tri, hi, preferred_element_type=F32)
             + jnp.dot(tri, mid, preferred_element_type=F32)
             + jnp.dot(tri, lo, preferred_element_type=F32)) + carry
        o_ref[0, rows, :] = c
        carry = c[LANES - 1:LANES, :]


def _cum_call(o32, fb_pad, batch, seq):
    o32v = o32.reshape(batch, seq, W32)
    return pl.pallas_call(
        functools.partial(_cum_kernel, seq=seq),
        grid=(batch,),
        in_specs=[pl.BlockSpec((1, seq, LANES), lambda b: (b, 0, COL_FF // LANES)),
                  pl.BlockSpec((1, LANES), lambda b: (0, 0))],
        out_specs=pl.BlockSpec((1, seq, LANES), lambda b: (b, 0, 0)),
        out_shape=jax.ShapeDtypeStruct((batch, seq, LANES), F32),
        compiler_params=_params(1),
        name="fox_cumsum",
    )(o32v, fb_pad)


def _fox_kernel(q_ref, k_ref, v_ref, cum_ref, cumt_ref, o_ref, m_scr, l_scr, acc_scr, *, seq, tq):
    h = pl.program_id(1)
    lane = lax.broadcasted_iota(jnp.int32, (tq, LANES), 1)
    ri = lax.broadcasted_iota(jnp.int32, (tq, tq), 0)
    ci = lax.broadcasted_iota(jnp.int32, (tq, tq), 1)

    def q_body(qi, carry):
        q0 = pl.multiple_of(qi * tq, tq)
        q = q_ref[0, pl.ds(q0, tq), :]
        cq = jnp.sum(jnp.where(lane == h, cum_ref[0, pl.ds(q0, tq), :], 0.0), axis=1, keepdims=True)
        m_scr[...] = jnp.full((tq, 1), NEG, F32)
        l_scr[...] = jnp.zeros((tq, 1), F32)
        acc_scr[...] = jnp.zeros((tq, HEAD_DIM), F32)

        def k_body(ki, c2):
            k0 = pl.multiple_of(ki * tq, tq)
            k = k_ref[0, pl.ds(k0, tq), :]
            v = v_ref[0, pl.ds(k0, tq), :]
            ck = cumt_ref[0, 0, pl.ds(ki, 1), :]
            s = lax.dot_general(q, k, NT_DIMS, preferred_element_type=F32) * SCALE
            s = (s + cq) - ck
            valid = (k0 + ci) <= (q0 + ri)
            s = jnp.where(valid, s, NEG)
            m_prev = m_scr[...]
            m_new = jnp.maximum(m_prev, jnp.max(s, axis=1, keepdims=True))
            alpha = jnp.exp(m_prev - m_new)
            p = jnp.where(valid, jnp.exp(s - m_new), 0.0)
            l_scr[...] = alpha * l_scr[...] + jnp.sum(p, axis=1, keepdims=True)
            acc_scr[...] = alpha * acc_scr[...] + jnp.dot(p.astype(BF16), v, preferred_element_type=F32)
            m_scr[...] = m_new
            return c2

        lax.fori_loop(0, qi + 1, k_body, 0)
        o_ref[0, pl.ds(q0, tq), :] = acc_scr[...] / l_scr[...]
        return carry

    lax.fori_loop(0, seq // tq, q_body, 0)


def _fox_call(o16v, cum, cumt, batch, seq, tq):
    nh = N_FOX_HEADS
    return pl.pallas_call(
        functools.partial(_fox_kernel, seq=seq, tq=tq),
        grid=(batch, nh),
        in_specs=[pl.BlockSpec((1, seq, HEAD_DIM), lambda b, h: (b, 0, h)),
                  pl.BlockSpec((1, seq, HEAD_DIM), lambda b, h: (b, 0, nh + h)),
                  pl.BlockSpec((1, seq, HEAD_DIM), lambda b, h: (b, 0, 2 * nh + h)),
                  pl.BlockSpec((1, seq, LANES), lambda b, h: (b, 0, 0)),
                  pl.BlockSpec((1, 1, seq // tq, tq), lambda b, h: (b, h, 0, 0))],
        out_specs=pl.BlockSpec((1, seq, HEAD_DIM), lambda b, h: (b, 0, h)),
        out_shape=jax.ShapeDtypeStruct((batch, seq, FOX_WIDTH), F32),
        scratch_shapes=[pltpu.VMEM((tq, 1), F32), pltpu.VMEM((tq, 1), F32), pltpu.VMEM((tq, HEAD_DIM), F32)],
        compiler_params=_params(2),
        name="fox_attention",
    )(o16v, o16v, o16v, cum, cumt)


def _cmp_kernel(x_ref, pos_ref, w1_ref, w2_ref, gain_ref, o_ref, xs_scr, *, seq, ncp):
    kv = pl.program_id(1)
    rows = xs_scr.shape[0]
    xs_scr[0:seq, :] = x_ref[0]
    xs_scr[seq:rows, :] = jnp.zeros((rows - seq, LANES), F32)
    acc = jnp.zeros((ncp, HEAD_DIM), F32)
    for l in range(CMP_BLOCK):
        xl = xs_scr[pl.ds(l, ncp, stride=CMP_STRIDE), :] + pos_ref[0, l:l + 1, :]
        acc = acc + jnp.dot(xl.astype(BF16), w1_ref[0, l], preferred_element_type=F32)
    hmid = acc * _sigmoid(acc)
    y = jnp.dot(hmid.astype(BF16), w2_ref[0], preferred_element_type=F32)
    yk = _rms(y) * gain_ref[...]
    o_ref[0, 0, 0] = jnp.where(kv == 0, yk, y).astype(BF16)


def _cmp_call(o32, pos, w1, w2, gain, batch, seq, ncp):
    o32v = o32.reshape(batch, seq, W32)
    hkv = N_NSA_KV_HEADS
    return pl.pallas_call(
        functools.partial(_cmp_kernel, seq=seq, ncp=ncp),
        grid=(batch, 2, hkv),
        in_specs=[pl.BlockSpec((1, seq, HEAD_DIM), lambda b, kv, h: (b, 0, kv * hkv + h)),
                  pl.BlockSpec((1, CMP_BLOCK, HEAD_DIM), lambda b, kv, h: (kv, 0, 0)),
                  pl.BlockSpec((1, CMP_BLOCK, HEAD_DIM, HEAD_DIM), lambda b, kv, h: (kv, 0, 0, 0)),
                  pl.BlockSpec((1, HEAD_DIM, HEAD_DIM), lambda b, kv, h: (kv, 0, 0)),
                  pl.BlockSpec((1, HEAD_DIM), lambda b, kv, h: (0, 0))],
        out_specs=pl.BlockSpec((1, 1, 1, ncp, HEAD_DIM), lambda b, kv, h: (b, kv, h, 0, 0)),
        out_shape=jax.ShapeDtypeStruct((batch, 2, hkv, ncp, HEAD_DIM), BF16),
        scratch_shapes=[pltpu.VMEM((CMP_STRIDE * ncp + CMP_BLOCK, LANES), F32)],
        compiler_params=_params(3),
        name="nsa_compress",
    )(o32v, pos, w1, w2, gain)


def _bias_kernel(rb_ref, o_ref, *, width, key_stride, key_offset):
    v = pl.program_id(0)
    i = lax.broadcasted_iota(jnp.int32, (LANES, width), 0)
    j = lax.broadcasted_iota(jnp.int32, (LANES, width), 1)
    d = v * LANES + i - (key_stride * j + key_offset)
    n = jnp.maximum(d, 0)
    max_exact = N_BUCKETS // 2
    nf = jnp.maximum(n, 1).astype(F32)
    large = max_exact + (jnp.log(nf / max_exact) / math.log(MAX_DISTANCE / max_exact)
                         * (N_BUCKETS - max_exact)).astype(jnp.int32)
    large = jnp.minimum(large, N_BUCKETS - 1)
    bkt = jnp.where(n < max_exact, n, large)
    vals = [jnp.zeros((LANES, width), F32) for _ in range(N_NSA_HEADS)]
    for bk in range(N_BUCKETS):
        hit = bkt == bk
        for h in range(N_NSA_HEADS):
            vals[h] = jnp.where(hit, rb_ref[bk * N_NSA_HEADS + h], vals[h])
    for h in range(N_NSA_HEADS):
        g = h % NSA_GROUP
        o_ref[0, h // NSA_GROUP, g * LANES:(g + 1) * LANES, :] = vals[h]


def _bias_call(rb_flat, n_variants, width, key_stride, key_offset, name):
    return pl.pallas_call(
        functools.partial(_bias_kernel, width=width, key_stride=key_stride, key_offset=key_offset),
        grid=(n_variants,),
        in_specs=[pl.BlockSpec(memory_space=pltpu.SMEM)],
        out_specs=pl.BlockSpec((1, N_NSA_KV_HEADS, GROUP_ROWS, width), lambda v: (v, 0, 0, 0)),
        out_shape=jax.ShapeDtypeStruct((n_variants, N_NSA_KV_HEADS, GROUP_ROWS, width), F32),
        compiler_params=_params(1),
        name=name,
    )(rb_flat)


def _tile4(a):
    return jnp.concatenate([a] * NSA_GROUP, axis=0)


def _nsa_kernel(q_ref, ks_ref, kw_ref, vs_ref, vw_ref, g_ref, kcvc_ref, bc_ref, tz_ref, selmt_ref, e_ref,
                o_ref, qg_scr, m_scr, l_scr, acc_scr, o_scr, *, seq, ncp):
    qt = pl.program_id(1)
    q0 = qt * LANES
    n_slc = seq // SEL_BLOCK
    top_n = min(N_SEL, n_slc)
    win_tiles = WINDOW // LANES
    ri = lax.broadcasted_iota(jnp.int32, (LANES, LANES), 0)
    ci = lax.broadcasted_iota(jnp.int32, (LANES, LANES), 1)
    eye = jnp.where(ri == ci, 1.0, 0.0).astype(BF16)
    gates = _sigmoid(g_ref[0])

    def gate_col(hk, br):
        cols = []
        for g in range(NSA_GROUP):
            c = COL_NG % LANES + (hk * NSA_GROUP + g) * N_BRANCH + br
            cols.append(gates[:, c:c + 1])
        return jnp.concatenate(cols, axis=0)

    def reset():
        m_scr[...] = jnp.full((GROUP_ROWS, 1), NEG, F32)
        l_scr[...] = jnp.zeros((GROUP_ROWS, 1), F32)
        acc_scr[...] = jnp.zeros((GROUP_ROWS, HEAD_DIM), F32)

    def online_update(s, valid, v):
        s = jnp.where(valid, s, NEG)
        m_prev = m_scr[...]
        m_new = jnp.maximum(m_prev, jnp.max(s, axis=1, keepdims=True))
        alpha = jnp.exp(m_prev - m_new)
        p = jnp.where(valid, jnp.exp(s - m_new), 0.0)
        l_scr[...] = alpha * l_scr[...] + jnp.sum(p, axis=1, keepdims=True)
        acc_scr[...] = alpha * acc_scr[...] + jnp.dot(p.astype(BF16), v, preferred_element_type=F32)
        m_scr[...] = m_new

    def finish():
        l = l_scr[...]
        return acc_scr[...] / jnp.where(l > 0.0, l, 1.0)

    for hk in range(N_NSA_KV_HEADS):
        hcols = slice(hk * HEAD_DIM, (hk + 1) * HEAD_DIM)
        for g in range(NSA_GROUP):
            h = hk * NSA_GROUP + g
            qg_scr[g * LANES:(g + 1) * LANES, :] = q_ref[0, :, h * HEAD_DIM:(h + 1) * HEAD_DIM]
        qg = qg_scr[...]

        kc = kcvc_ref[0, 0, hk]
        vc = kcvc_ref[0, 1, hk]
        s = lax.dot_general(qg, kc, NT_DIMS, preferred_element_type=F32) * SCALE + bc_ref[0, hk]
        rc = lax.broadcasted_iota(jnp.int32, (LANES, ncp), 0)
        cc = lax.broadcasted_iota(jnp.int32, (LANES, ncp), 1)
        valid_c = _tile4(jnp.where(q0 + rc - (CMP_STRIDE * cc + CMP_BLOCK - 1) >= 0, 1.0, 0.0)) > 0.5
        s = jnp.where(valid_c, s, NEG)
        p = jnp.where(valid_c, jnp.exp(s - jnp.max(s, axis=1, keepdims=True)), 0.0)
        l = jnp.sum(p, axis=1, keepdims=True)
        p = p / jnp.where(l > 0.0, l, 1.0)
        o_scr[...] = gate_col(hk, 0) * jnp.dot(p.astype(BF16), vc, preferred_element_type=F32)

        psum = p[0:LANES]
        for g in range(1, NSA_GROUP):
            psum = psum + p[g * LANES:(g + 1) * LANES]
        p_hi = psum.astype(BF16)
        p_lo = (psum - p_hi.astype(F32)).astype(BF16)
        selmt = selmt_ref[...]
        imp = (lax.dot_general(selmt, p_hi, NT_DIMS, preferred_element_type=F32)
               + lax.dot_general(selmt, p_lo, NT_DIMS, preferred_element_type=F32))
        imp = imp[0:n_slc]
        blk = lax.broadcasted_iota(jnp.int32, (n_slc, LANES), 0)
        cur = (q0 + lax.broadcasted_iota(jnp.int32, (n_slc, LANES), 1)) // SEL_BLOCK
        forced = (blk == 0) | (blk == cur) | (blk == cur - 1)
        imp = jnp.where(forced, FORCE, imp)
        imp = jnp.where(blk <= cur, imp, -jnp.inf)
        rank = jnp.zeros((n_slc, LANES), F32)
        for j in range(n_slc):
            row = imp[j:j + 1, :]
            beats = jnp.where(row > imp, 1.0, jnp.where(row == imp, jnp.where(blk > j, 1.0, 0.0), 0.0))
            rank = rank + beats
        sel_t = jnp.where(rank < top_n, jnp.where(imp > -jnp.inf, 1.0, 0.0), 0.0)
        if n_slc < LANES:
            sel_t = jnp.concatenate([sel_t, jnp.zeros((LANES - n_slc, LANES), F32)], axis=0)
        sel_q = lax.dot_general(eye, sel_t.astype(BF16), NT_DIMS, preferred_element_type=F32).astype(BF16)

        reset()

        def sel_body(kt, carry):
            k0 = pl.multiple_of(kt * LANES, LANES)
            k = ks_ref[0, pl.ds(k0, LANES), hcols]
            v = vs_ref[0, pl.ds(k0, LANES), hcols]
            member = jnp.dot(sel_q, e_ref[kt], preferred_element_type=F32)
            ok = jnp.where((k0 + ci) <= (q0 + ri), member, 0.0)
            valid = _tile4(ok) > 0.5
            s = (lax.dot_general(qg, k, NT_DIMS, preferred_element_type=F32) * SCALE
                 + tz_ref[jnp.minimum(qt - kt, 2), hk])
            online_update(s, valid, v)
            return carry

        lax.fori_loop(0, qt + 1, sel_body, 0)
        o_scr[...] = o_scr[...] + gate_col(hk, 1) * finish()

        reset()

        def win_body(kt, carry):
            k0 = pl.multiple_of(kt * LANES, LANES)
            k = kw_ref[0, pl.ds(k0, LANES), hcols]
            v = vw_ref[0, pl.ds(k0, LANES), hcols]
            d = (q0 + ri) - (k0 + ci)
            ok = jnp.where(d >= 0, jnp.where(d < WINDOW, 1.0, 0.0), 0.0)
            valid = _tile4(ok) > 0.5
            s = (lax.dot_general(qg, k, NT_DIMS, preferred_element_type=F32) * SCALE
                 + tz_ref[jnp.minimum(qt - kt, 2), hk])
            online_update(s, valid, v)
            return carry

        lax.fori_loop(jnp.maximum(qt - win_tiles, 0), qt + 1, win_body, 0)
        o = o_scr[...] + gate_col(hk, 2) * finish()
        for g in range(NSA_GROUP):
            h = hk * NSA_GROUP + g
            o_ref[0, :, h * HEAD_DIM:(h + 1) * HEAD_DIM] = o[g * LANES:(g + 1) * LANES]


def _nsa_call(o16v, o32v, kcvc, bias_c, tz, selmt, e3, batch, seq, ncp):
    nqt = seq // LANES
    kvw = NSA_KV_WIDTH
    base = (3 * FOX_WIDTH + NSA_WIDTH) // kvw
    return pl.pallas_call(
        functools.partial(_nsa_kernel, seq=seq, ncp=ncp),
        grid=(batch, nqt),
        in_specs=[pl.BlockSpec((1, LANES, NSA_WIDTH), lambda b, t: (b, t, 3 * FOX_WIDTH // NSA_WIDTH)),
                  pl.BlockSpec((1, seq, kvw), lambda b, t: (b, 0, base)),
                  pl.BlockSpec((1, seq, kvw), lambda b, t: (b, 0, base + 1)),
                  pl.BlockSpec((1, seq, kvw), lambda b, t: (b, 0, base + 2)),
                  pl.BlockSpec((1, seq, kvw), lambda b, t: (b, 0, base + 3)),
                  pl.BlockSpec((1, LANES, LANES), lambda b, t: (b, t, COL_NG // LANES)),
                  pl.BlockSpec((1, 2, N_NSA_KV_HEADS, ncp, HEAD_DIM), lambda b, t: (b, 0, 0, 0, 0)),
                  pl.BlockSpec((1, N_NSA_KV_HEADS, GROUP_ROWS, ncp), lambda b, t: (t, 0, 0, 0)),
                  pl.BlockSpec((3, N_NSA_KV_HEADS, GROUP_ROWS, LANES), lambda b, t: (0, 0, 0, 0)),
                  pl.BlockSpec((LANES, ncp), lambda b, t: (0, 0)),
                  pl.BlockSpec((nqt, LANES, LANES), lambda b, t: (0, 0, 0))],
        out_specs=pl.BlockSpec((1, LANES, NSA_WIDTH), lambda b, t: (b, t, 0)),
        out_shape=jax.ShapeDtypeStruct((batch, seq, NSA_WIDTH), F32),
        scratch_shapes=[pltpu.VMEM((GROUP_ROWS, HEAD_DIM), BF16),
                        pltpu.VMEM((GROUP_ROWS, 1), F32),
                        pltpu.VMEM((GROUP_ROWS, 1), F32),
                        pltpu.VMEM((GROUP_ROWS, HEAD_DIM), F32),
                        pltpu.VMEM((GROUP_ROWS, HEAD_DIM), F32)],
        compiler_params=_params(2),
        name="nsa_attention",
    )(o16v, o16v, o16v, o16v, o16v, o32v, kcvc, bias_c, tz, selmt, e3)


def _out_kernel(of_ref, on_ref, gain_ref, w_ref, x_ref, g_ref, o_ref, y_scr):
    j = pl.program_id(1)

    @pl.when(j == 0)
    def _():
        y_scr[:, 0:FOX_WIDTH] = (_rms(of_ref[...]) * gain_ref[:, 0:FOX_WIDTH]).astype(BF16)
        y_scr[:, FOX_WIDTH:MIX_WIDTH] = (_rms(on_ref[...]) * gain_ref[:, FOX_WIDTH:MIX_WIDTH]).astype(BF16)

    acc = jnp.dot(y_scr[...], w_ref[...], preferred_element_type=F32)
    o_ref[...] = x_ref[...] + g_ref[0] * acc


def _out_call(o_fox, o_nsa, gain, w_out, x2, g1, seq):
    t, d = x2.shape
    tm = min(512, seq)
    tn = min(1024, d)
    rows_per_batch = seq // tm
    return pl.pallas_call(
        _out_kernel,
        grid=(t // tm, d // tn),
        in_specs=[pl.BlockSpec((tm, FOX_WIDTH), lambda i, j: (i, 0)),
                  pl.BlockSpec((tm, NSA_WIDTH), lambda i, j: (i, 0)),
                  pl.BlockSpec((1, MIX_WIDTH), lambda i, j: (0, 0)),
                  pl.BlockSpec((MIX_WIDTH, tn), lambda i, j: (0, j)),
                  pl.BlockSpec((tm, tn), lambda i, j: (i, j)),
                  pl.BlockSpec((1, 1, tn), lambda i, j: (i // rows_per_batch, 0, j))],
        out_specs=pl.BlockSpec((tm, tn), lambda i, j: (i, j)),
        out_shape=jax.ShapeDtypeStruct((t, d), F32),
        scratch_shapes=[pltpu.VMEM((tm, MIX_WIDTH), BF16)],
        compiler_params=_params(2),
        name="out_proj",
    )(o_fox, o_nsa, gain, w_out, x2, g1)


def _ffn_kernel(x_ref, gain_ref, sc_ref, sh_ref, g_ref, wg_ref, wu_ref, wd_ref, o_ref, h_scr, acc_scr):
    f = pl.program_id(1)

    @pl.when(f == 0)
    def _():
        h = _rms(x_ref[...]) * gain_ref[...]
        h = h * (1.0 + sc_ref[0]) + sh_ref[0]
        h_scr[...] = h.astype(BF16)
        acc_scr[...] = jnp.zeros(acc_scr.shape, F32)

    h = h_scr[...]
    a = jnp.dot(h, wg_ref[...], preferred_element_type=F32)
    u = jnp.dot(h, wu_ref[...], preferred_element_type=F32)
    t = (a * _sigmoid(a)) * u
    acc_scr[...] += jnp.dot(t.astype(BF16), wd_ref[...], preferred_element_type=F32)

    @pl.when(f == pl.num_programs(1) - 1)
    def _():
        o_ref[...] = x_ref[...] + g_ref[0] * acc_scr[...]


def _ffn_call(x1, gain2, sc, sh, g2, wg, wu, wd, seq):
    t, d = x1.shape
    dff = wg.shape[1]
    tm = min(512, seq)
    tf = 512 if dff % 512 == 0 else dff
    rows_per_batch = seq // tm
    return pl.pallas_call(
        _ffn_kernel,
        grid=(t // tm, dff // tf),
        in_specs=[pl.BlockSpec((tm, d), lambda i, f: (i, 0)),
                  pl.BlockSpec((1, d), lambda i, f: (0, 0)),
                  pl.BlockSpec((1, 1, d), lambda i, f: (i // rows_per_batch, 0, 0)),
                  pl.BlockSpec((1, 1, d), lambda i, f: (i // rows_per_batch, 0, 0)),
                  pl.BlockSpec((1, 1, d), lambda i, f: (i // rows_per_batch, 0, 0)),
                  pl.BlockSpec((d, tf), lambda i, f: (0, f)),
                  pl.BlockSpec((d, tf), lambda i, f: (0, f)),
                  pl.BlockSpec((tf, d), lambda i, f: (f, 0))],
        out_specs=pl.BlockSpec((tm, d), lambda i, f: (i, 0)),
        out_shape=jax.ShapeDtypeStruct((t, d), F32),
        scratch_shapes=[pltpu.VMEM((tm, d), BF16), pltpu.VMEM((tm, d), F32)],
        compiler_params=_params(2),
        name="swiglu_ffn",
    )(x1, gain2, sc, sh, g2, wg, wu, wd)


def _selection_matrix_t(ncp, n_slc):
    r, q = SEL_BLOCK // CMP_STRIDE, CMP_BLOCK // CMP_STRIDE
    m = np.zeros((LANES, ncp), np.float32)
    for j in range(n_slc):
        for a in range(r):
            for b in range(q):
                c = r * j + a - b
                if 0 <= c < ncp:
                    m[j, c] += 1.0
    return m


def _block_expander(seq):
    keys = np.arange(seq).reshape(seq // LANES, 1, LANES)
    blocks = np.arange(LANES).reshape(1, LANES, 1)
    return (keys // SEL_BLOCK == blocks).astype(np.float32)


def _permute_w_in(w):
    d = w.shape[0]
    o = 0
    parts = {}
    for name, width in (("fq", FOX_WIDTH), ("fk", FOX_WIDTH), ("fv", FOX_WIDTH), ("ff", N_FOX_HEADS),
                        ("nq", NSA_WIDTH), ("nk", N_BRANCH * NSA_KV_WIDTH), ("nv", N_BRANCH * NSA_KV_WIDTH),
                        ("ng", N_BRANCH * N_NSA_HEADS)):
        parts[name] = w[:, o:o + width]
        o += width
    kvw = NSA_KV_WIDTH
    pad = jnp.zeros((d, W32 - (2 * kvw + N_FOX_HEADS + N_BRANCH * N_NSA_HEADS)), w.dtype)
    return jnp.concatenate([
        parts["fq"], parts["fk"], parts["fv"], parts["nq"],
        parts["nk"][:, kvw:2 * kvw], parts["nk"][:, 2 * kvw:3 * kvw],
        parts["nv"][:, kvw:2 * kvw], parts["nv"][:, 2 * kvw:3 * kvw],
        parts["nk"][:, 0:kvw], parts["nv"][:, 0:kvw], parts["ff"], parts["ng"], pad], axis=1).astype(BF16)


def kernel(x, c, ada_w, ada_b, norm1_gain, norm2_gain, w_in, fox_f_bias, fox_q_gain, fox_k_gain, nsa_q_gain,
           nsa_k_gain, nsa_cmp_pos, nsa_cmp_w1, nsa_cmp_w2, rel_bias, mix_out_gain, w_out, ffn_w_gate, ffn_w_up,
           ffn_w_down):
    batch, seq, d = x.shape
    assert seq % LANES == 0 and seq >= WINDOW and d % LANES == 0
    depth = ada_w.shape[0]
    nqt = seq // LANES
    ncp = -(-(seq // CMP_STRIDE) // LANES) * LANES
    tq_fox = min(256, seq)

    selmt = jnp.asarray(_selection_matrix_t(ncp, seq // SEL_BLOCK), BF16)
    e3 = jnp.asarray(_block_expander(seq), BF16)
    rb_flat = rel_bias.reshape(-1)
    bias_c = _bias_call(rb_flat, nqt, ncp, CMP_STRIDE, CMP_BLOCK - 1, "t5_bias_compressed")
    tz = _bias_call(rb_flat, 3, LANES, 1, 0, "t5_bias_toeplitz")

    ones_h = jnp.ones((HEAD_DIM,), F32)
    c_pad = jnp.pad(c, ((0, 8 - batch % 8 if batch % 8 else 0), (0, 0)))
    x2 = x.reshape(batch * seq, d)
    for layer in range(depth):
        mod = _ada_call(c_pad, ada_w[layer], ada_b[layer][None, :])[:batch]
        sh1, sc1, g1, sh2, sc2, g2 = [mod[:, i * d:(i + 1) * d][:, None, :] for i in range(N_MOD)]

        kg = nsa_k_gain[layer]
        col_gain = jnp.concatenate([
            jnp.tile(fox_q_gain[layer], N_FOX_HEADS), jnp.tile(fox_k_gain[layer], N_FOX_HEADS),
            jnp.tile(ones_h, N_FOX_HEADS), jnp.tile(nsa_q_gain[layer], N_NSA_HEADS),
            jnp.tile(kg[1], N_NSA_KV_HEADS), jnp.tile(kg[2], N_NSA_KV_HEADS),
            jnp.tile(ones_h, 2 * N_NSA_KV_HEADS)])[None, :]
        col_flag = jnp.concatenate([
            jnp.ones((2 * FOX_WIDTH,), F32), jnp.zeros((FOX_WIDTH,), F32), jnp.ones((NSA_WIDTH,), F32),
            jnp.ones((2 * NSA_KV_WIDTH,), F32), jnp.zeros((2 * NSA_KV_WIDTH,), F32)])[None, :]
        o16, o32 = _proj_call(x2, sc1, sh1, norm1_gain[layer][None, :], _permute_w_in(w_in[layer]),
                              col_gain, col_flag, seq)
        o16v = o16.reshape(batch, seq, W16)
        o32v = o32.reshape(batch, seq, W32)

        fb_pad = jnp.pad(fox_f_bias[layer], (0, LANES - N_FOX_HEADS))[None, :]
        cum = _cum_call(o32, fb_pad, batch, seq)
        cumt = jnp.transpose(cum[:, :, :N_FOX_HEADS], (0, 2, 1)).reshape(batch, N_FOX_HEADS, seq // tq_fox, tq_fox)
        o_fox = _fox_call(o16v, cum, cumt, batch, seq, tq_fox)

        w1 = nsa_cmp_w1[layer].reshape(2, CMP_BLOCK, HEAD_DIM, HEAD_DIM).astype(BF16)
        kcvc = _cmp_call(o32, nsa_cmp_pos[layer], w1, nsa_cmp_w2[layer].astype(BF16), kg[0][None, :],
                         batch, seq, ncp)
        o_nsa = _nsa_call(o16v, o32v, kcvc, bias_c, tz, selmt, e3, batch, seq, ncp)

        x1 = _out_call(o_fox.reshape(batch * seq, FOX_WIDTH), o_nsa.reshape(batch * seq, NSA_WIDTH),
                       mix_out_gain[layer][None, :], w_out[layer].astype(BF16), x2, g1, seq)
        x2 = _ffn_call(x1, norm2_gain[layer][None, :], sc2, sh2, g2, ffn_w_gate[layer].astype(BF16),
                       ffn_w_up[layer].astype(BF16), ffn_w_down[layer].astype(BF16), seq)
    return x2.reshape(batch, seq, d)
```

```python
import functools
import math

import numpy as np
import jax
import jax.numpy as jnp
from jax import lax
from jax.experimental import pallas as pl
from jax.experimental.pallas import tpu as pltpu

HEAD_DIM = 128
N_FOX_HEADS = 8
N_NSA_HEADS = 8
N_NSA_KV_HEADS = 2
NSA_GROUP = N_NSA_HEADS // N_NSA_KV_HEADS
FOX_WIDTH = N_FOX_HEADS * HEAD_DIM
NSA_WIDTH = N_NSA_HEADS * HEAD_DIM
NSA_KV_WIDTH = N_NSA_KV_HEADS * HEAD_DIM
MIX_WIDTH = FOX_WIDTH + NSA_WIDTH
N_BRANCH = 3
CMP_BLOCK = 32
CMP_STRIDE = 16
SEL_BLOCK = 64
N_SEL = 8
WINDOW = 512
N_BUCKETS = 32
MAX_DISTANCE = 128
N_MOD = 6
SCALE = HEAD_DIM ** -0.5
EPS = 1e-6
NEG = -1e30
FORCE = 1e6

LANES = 128
GROUP_ROWS = NSA_GROUP * LANES
VMEM_LIMIT = 56 * 1024 * 1024
KEY_CHUNK = 2 * LANES
WIN_PAD = WINDOW + LANES
MASK_BIG = 2.0 ** 100

W16 = 3 * FOX_WIDTH + NSA_WIDTH + 4 * NSA_KV_WIDTH
W32 = 1024
COL_FF = 2 * NSA_KV_WIDTH
COL_NG = COL_FF + N_FOX_HEADS

F32 = jnp.float32
BF16 = jnp.bfloat16
NT_DIMS = (((1,), (1,)), ((), ()))


def _params(n_axes):
    return pltpu.CompilerParams(dimension_semantics=("arbitrary",) * n_axes,
                                vmem_limit_bytes=VMEM_LIMIT)


def _sigmoid(x):
    return 1.0 / (1.0 + jnp.exp(-x))


def _rms(x):
    return x * lax.rsqrt(jnp.mean(x * x, axis=-1, keepdims=True) + EPS)


def _ada_kernel(c_ref, w_ref, b_ref, o_ref):
    c = c_ref[...]
    s = (c * _sigmoid(c)).astype(BF16)
    o_ref[...] = jnp.dot(s, w_ref[...].astype(BF16), preferred_element_type=F32) + b_ref[...]


def _ada_call(c_pad, w, b):
    rows, d = c_pad.shape
    n = w.shape[1]
    tn = next(t for t in (1024, 768, 512, 384, 256, 128) if n % t == 0)
    return pl.pallas_call(
        _ada_kernel,
        grid=(n // tn,),
        in_specs=[pl.BlockSpec((rows, d), lambda j: (0, 0)),
                  pl.BlockSpec((d, tn), lambda j: (0, j)),
                  pl.BlockSpec((1, tn), lambda j: (0, j))],
        out_specs=pl.BlockSpec((rows, tn), lambda j: (0, j)),
        out_shape=jax.ShapeDtypeStruct((rows, n), F32),
        compiler_params=_params(1),
        name="adaln",
    )(c_pad, w, b)


def _proj_kernel(x_ref, sc_ref, sh_ref, g_ref, w_ref, gain_ref, flag_ref, o16_ref, o32_ref, h_scr, *, n16, tn):
    j = pl.program_id(1)

    @pl.when(j == 0)
    def _():
        h = _rms(x_ref[...]) * g_ref[...]
        h = h * (1.0 + sc_ref[0]) + sh_ref[0]
        h_scr[...] = h.astype(BF16)

    acc = jnp.dot(h_scr[...], w_ref[...], preferred_element_type=F32)

    @pl.when(j < n16)
    def _():
        for g in range(tn // LANES):
            cols = slice(g * LANES, (g + 1) * LANES)
            a = acc[:, cols]
            r = lax.rsqrt(jnp.mean(a * a, axis=-1, keepdims=True) + EPS)
            scale = jnp.where(flag_ref[:, cols] > 0.5, r, 1.0)
            o16_ref[:, cols] = (a * scale * gain_ref[:, cols]).astype(BF16)

    @pl.when(j >= n16)
    def _():
        o32_ref[...] = acc


def _proj_call(x2, sc, sh, gain1, w_perm, col_gain, col_flag, seq):
    t, d = x2.shape
    tm = min(512, seq)
    tn = W32
    n16 = W16 // tn
    rows_per_batch = seq // tm
    return pl.pallas_call(
        functools.partial(_proj_kernel, n16=n16, tn=tn),
        grid=(t // tm, n16 + 1),
        in_specs=[pl.BlockSpec((tm, d), lambda i, j: (i, 0)),
                  pl.BlockSpec((1, 1, d), lambda i, j: (i // rows_per_batch, 0, 0)),
                  pl.BlockSpec((1, 1, d), lambda i, j: (i // rows_per_batch, 0, 0)),
                  pl.BlockSpec((1, d), lambda i, j: (0, 0)),
                  pl.BlockSpec((d, tn), lambda i, j: (0, j)),
                  pl.BlockSpec((1, tn), lambda i, j: (0, jnp.minimum(j, n16 - 1))),
                  pl.BlockSpec((1, tn), lambda i, j: (0, jnp.minimum(j, n16 - 1)))],
        out_specs=[pl.BlockSpec((tm, tn), lambda i, j: (i, jnp.minimum(j, n16 - 1))),
                   pl.BlockSpec((tm, tn), lambda i, j: (i, 0))],
        out_shape=[jax.ShapeDtypeStruct((t, W16), BF16),
                   jax.ShapeDtypeStruct((t, W32), F32)],
        scratch_shapes=[pltpu.VMEM((tm, d), BF16)],
        compiler_params=_params(2),
        name="in_proj",
    )(x2, sc, sh, gain1, w_perm, col_gain, col_flag)


def _cum_kernel(ff_ref, fb_ref, o_ref, *, seq):
    ri = lax.broadcasted_iota(jnp.int32, (LANES, LANES), 0)
    ci = lax.broadcasted_iota(jnp.int32, (LANES, LANES), 1)
    tri = jnp.where(ri >= ci, 1.0, 0.0).astype(BF16)
    carry = jnp.zeros((1, LANES), F32)
    for blk in range(seq // LANES):
        rows = slice(blk * LANES, (blk + 1) * LANES)
        x = ff_ref[0, rows, :] + fb_ref[...]
        lf = jnp.minimum(x, 0.0) - jnp.log(1.0 + jnp.exp(-jnp.abs(x)))
        hi = lf.astype(BF16)
        r1 = lf - hi.astype(F32)
        mid = r1.astype(BF16)
        lo = (r1 - mid.astype(F32)).astype(BF16)
        c = (jnp.dot(tri, hi, preferred_element_type=F32)
             + jnp.dot(tri, mid, preferred_element_type=F32)
             + jnp.dot(tri, lo, preferred_element_type=F32)) + carry
        o_ref[0, rows, :] = c
        carry = c[LANES - 1:LANES, :]


def _cum_call(o32, fb_pad, batch, seq):
    o32v = o32.reshape(batch, seq, W32)
    return pl.pallas_call(
        functools.partial(_cum_kernel, seq=seq),
        grid=(batch,),
        in_specs=[pl.BlockSpec((1, seq, LANES), lambda b: (b, 0, COL_FF // LANES)),
                  pl.BlockSpec((1, LANES), lambda b: (0, 0))],
        out_specs=pl.BlockSpec((1, seq, LANES), lambda b: (b, 0, 0)),
        out_shape=jax.ShapeDtypeStruct((batch, seq, LANES), F32),
        compiler_params=_params(1),
        name="fox_cumsum",
    )(o32v, fb_pad)


def _fox_kernel(q_ref, k_ref, v_ref, cum_ref, cumt_ref, o_ref, m_scr, acc_scr, vaug_scr, *, seq, tq, nh):
    hp = pl.program_id(1)
    lane = lax.broadcasted_iota(jnp.int32, (tq, LANES), 1)
    causal = lax.broadcasted_iota(jnp.int32, (tq, tq), 1) <= lax.broadcasted_iota(jnp.int32, (tq, tq), 0)
    for j in range(nh):
        vaug_scr[j, :, 0:HEAD_DIM] = v_ref[0, :, j * HEAD_DIM:(j + 1) * HEAD_DIM]
        vaug_scr[j, :, HEAD_DIM:2 * HEAD_DIM] = jnp.ones((seq, HEAD_DIM), BF16)

    def q_body(qi, carry):
        q0 = pl.multiple_of(qi * tq, tq)
        cum_t = cum_ref[0, pl.ds(q0, tq), :]
        qs, cqs = [], []
        for j in range(nh):
            qs.append(q_ref[0, pl.ds(q0, tq), j * HEAD_DIM:(j + 1) * HEAD_DIM])
            cqs.append(jnp.sum(jnp.where(lane == hp * nh + j, cum_t, 0.0), axis=1, keepdims=True))
            m_scr[j] = jnp.full((tq, LANES), NEG, F32)
            acc_scr[j] = jnp.zeros((tq, 2 * HEAD_DIM), F32)

        def tile(ki, diagonal):
            k0 = pl.multiple_of(ki * tq, tq)
            for j in range(nh):
                k = k_ref[0, pl.ds(k0, tq), j * HEAD_DIM:(j + 1) * HEAD_DIM]
                ck = cumt_ref[0, j, pl.ds(ki, 1), :]
                s = lax.dot_general(qs[j], k, NT_DIMS, preferred_element_type=F32) * SCALE
                s = (s + cqs[j]) - ck
                if diagonal:
                    s = jnp.where(causal, s, NEG)
                m_prev = m_scr[j]
                m_new = jnp.maximum(m_prev, jnp.max(s, axis=1, keepdims=True))
                alpha = jnp.exp(m_prev - m_new)
                p = jnp.exp(s - pltpu.repeat(m_new, tq // LANES, axis=1))
                pv = jnp.dot(p.astype(BF16), vaug_scr[j, pl.ds(k0, tq), :], preferred_element_type=F32)
                acc_scr[j] = pltpu.repeat(alpha, 2, axis=1) * acc_scr[j] + pv
                m_scr[j] = m_new

        def k_body(ki, c2):
            tile(ki, False)
            return c2

        lax.fori_loop(0, qi, k_body, 0)
        tile(qi, True)
        for j in range(nh):
            acc = acc_scr[j]
            o_ref[0, pl.ds(q0, tq), j * HEAD_DIM:(j + 1) * HEAD_DIM] = acc[:, 0:HEAD_DIM] / acc[:, HEAD_DIM:]
        return carry

    lax.fori_loop(0, seq // tq, q_body, 0)


def _fox_call(o16v, cum, cumt, batch, seq, tq):
    nh = 4
    groups = N_FOX_HEADS // nh
    w = nh * HEAD_DIM
    return pl.pallas_call(
        functools.partial(_fox_kernel, seq=seq, tq=tq, nh=nh),
        grid=(batch, groups),
        in_specs=[pl.BlockSpec((1, seq, w), lambda b, h: (b, 0, h)),
                  pl.BlockSpec((1, seq, w), lambda b, h: (b, 0, groups + h)),
                  pl.BlockSpec((1, seq, w), lambda b, h: (b, 0, 2 * groups + h)),
                  pl.BlockSpec((1, seq, LANES), lambda b, h: (b, 0, 0)),
                  pl.BlockSpec((1, nh, seq // tq, tq), lambda b, h: (b, h, 0, 0))],
        out_specs=pl.BlockSpec((1, seq, w), lambda b, h: (b, 0, h)),
        out_shape=jax.ShapeDtypeStruct((batch, seq, FOX_WIDTH), F32),
        scratch_shapes=[pltpu.VMEM((nh, tq, LANES), F32),
                        pltpu.VMEM((nh, tq, 2 * HEAD_DIM), F32),
                        pltpu.VMEM((nh, seq, 2 * HEAD_DIM), BF16)],
        compiler_params=_params(2),
        name="fox_attention",
    )(o16v, o16v, o16v, cum, cumt)


def _cmp_kernel(x_ref, pos_ref, w1_ref, w2_ref, gain_ref, o_ref, xs_scr, *, seq, ncp):
    kv = pl.program_id(1)
    rows = xs_scr.shape[0]
    xs_scr[0:seq, :] = x_ref[0]
    xs_scr[seq:rows, :] = jnp.zeros((rows - seq, LANES), F32)
    acc = jnp.zeros((ncp, HEAD_DIM), F32)
    for l in range(CMP_BLOCK):
        xl = xs_scr[pl.ds(l, ncp, stride=CMP_STRIDE), :] + pos_ref[0, l:l + 1, :]
        acc = acc + jnp.dot(xl.astype(BF16), w1_ref[0, l], preferred_element_type=F32)
    hmid = acc * _sigmoid(acc)
    y = jnp.dot(hmid.astype(BF16), w2_ref[0], preferred_element_type=F32)
    yk = _rms(y) * gain_ref[...]
    o_ref[0, 0, 0] = jnp.where(kv == 0, yk, y).astype(BF16)


def _cmp_call(o32, pos, w1, w2, gain, batch, seq, ncp):
    o32v = o32.reshape(batch, seq, W32)
    hkv = N_NSA_KV_HEADS
    return pl.pallas_call(
        functools.partial(_cmp_kernel, seq=seq, ncp=ncp),
        grid=(batch, 2, hkv),
        in_specs=[pl.BlockSpec((1, seq, HEAD_DIM), lambda b, kv, h: (b, 0, kv * hkv + h)),
                  pl.BlockSpec((1, CMP_BLOCK, HEAD_DIM), lambda b, kv, h: (kv, 0, 0)),
                  pl.BlockSpec((1, CMP_BLOCK, HEAD_DIM, HEAD_DIM), lambda b, kv, h: (kv, 0, 0, 0)),
                  pl.BlockSpec((1, HEAD_DIM, HEAD_DIM), lambda b, kv, h: (kv, 0, 0)),
                  pl.BlockSpec((1, HEAD_DIM), lambda b, kv, h: (0, 0))],
        out_specs=pl.BlockSpec((1, 1, 1, ncp, HEAD_DIM), lambda b, kv, h: (b, kv, h, 0, 0)),
        out_shape=jax.ShapeDtypeStruct((batch, 2, hkv, ncp, HEAD_DIM), BF16),
        scratch_shapes=[pltpu.VMEM((CMP_STRIDE * ncp + CMP_BLOCK, LANES), F32)],
        compiler_params=_params(3),
        name="nsa_compress",
    )(o32v, pos, w1, w2, gain)


def _bias_kernel(rb_ref, o_ref, *, width, key_stride, key_offset):
    v = pl.program_id(0)
    i = lax.broadcasted_iota(jnp.int32, (LANES, width), 0)
    j = lax.broadcasted_iota(jnp.int32, (LANES, width), 1)
    d = v * LANES + i - (key_stride * j + key_offset)
    n = jnp.maximum(d, 0)
    max_exact = N_BUCKETS // 2
    nf = jnp.maximum(n, 1).astype(F32)
    large = max_exact + (jnp.log(nf / max_exact) / math.log(MAX_DISTANCE / max_exact)
                         * (N_BUCKETS - max_exact)).astype(jnp.int32)
    large = jnp.minimum(large, N_BUCKETS - 1)
    bkt = jnp.where(n < max_exact, n, large)
    vals = [jnp.zeros((LANES, width), F32) for _ in range(N_NSA_HEADS)]
    for bk in range(N_BUCKETS):
        hit = bkt == bk
        for h in range(N_NSA_HEADS):
            vals[h] = jnp.where(hit, rb_ref[bk * N_NSA_HEADS + h], vals[h])
    for h in range(N_NSA_HEADS):
        g = h % NSA_GROUP
        o_ref[0, h // NSA_GROUP, g * LANES:(g + 1) * LANES, :] = vals[h]


def _bias_call(rb_flat, n_variants, width, key_stride, key_offset, name):
    return pl.pallas_call(
        functools.partial(_bias_kernel, width=width, key_stride=key_stride, key_offset=key_offset),
        grid=(n_variants,),
        in_specs=[pl.BlockSpec(memory_space=pltpu.SMEM)],
        out_specs=pl.BlockSpec((1, N_NSA_KV_HEADS, GROUP_ROWS, width), lambda v: (v, 0, 0, 0)),
        out_shape=jax.ShapeDtypeStruct((n_variants, N_NSA_KV_HEADS, GROUP_ROWS, width), F32),
        compiler_params=_params(1),
        name=name,
    )(rb_flat)


def _tile4(a):
    return jnp.concatenate([a] * NSA_GROUP, axis=0)


def _nsa_kernel(q_ref, ks_ref, kw_ref, vs_ref, vw_ref, g_ref, kcvc_ref, bc_ref, tz_ref, selmt_ref, e_ref,
                o_ref, qg_scr, m_scr, acc_scr, o_scr, selq_scr, vsa_scr, kwp_scr, vwa_scr, *, seq, ncp):
    qt = pl.program_id(1)
    q0 = qt * LANES
    n_slc = seq // SEL_BLOCK
    top_n = min(N_SEL, n_slc)
    nkv = N_NSA_KV_HEADS
    ri = lax.broadcasted_iota(jnp.int32, (LANES, LANES), 0)
    ci = lax.broadcasted_iota(jnp.int32, (LANES, LANES), 1)
    rk = lax.broadcasted_iota(jnp.int32, (LANES, KEY_CHUNK), 0)
    ck = lax.broadcasted_iota(jnp.int32, (LANES, KEY_CHUNK), 1)
    eye = jnp.where(ri == ci, 1.0, 0.0).astype(BF16)
    gates = _sigmoid(g_ref[0])

    def gate_col(hk, br):
        cols = []
        for g in range(NSA_GROUP):
            c = COL_NG % LANES + (hk * NSA_GROUP + g) * N_BRANCH + br
            cols.append(gates[:, c:c + 1])
        return jnp.concatenate(cols, axis=0)

    @pl.when(qt == 0)
    def _():
        ones = jnp.ones((seq, HEAD_DIM), BF16)
        for hk in range(nkv):
            hc = slice(hk * HEAD_DIM, (hk + 1) * HEAD_DIM)
            vsa_scr[hk, :, 0:HEAD_DIM] = vs_ref[0, :, hc]
            vsa_scr[hk, :, HEAD_DIM:] = ones
            kwp_scr[hk, 0:WIN_PAD, :] = jnp.zeros((WIN_PAD, HEAD_DIM), BF16)
            kwp_scr[hk, WIN_PAD:, :] = kw_ref[0, :, hc]
            vwa_scr[hk, 0:WIN_PAD, :] = jnp.zeros((WIN_PAD, 2 * HEAD_DIM), BF16)
            vwa_scr[hk, WIN_PAD:, 0:HEAD_DIM] = vw_ref[0, :, hc]
            vwa_scr[hk, WIN_PAD:, HEAD_DIM:] = ones

    def reset():
        m_scr[...] = jnp.full(m_scr.shape, NEG, F32)
        acc_scr[...] = jnp.zeros(acc_scr.shape, F32)

    def online_update(hk, s, vaug):
        m_prev = m_scr[hk]
        m_new = jnp.maximum(m_prev, jnp.max(s, axis=1, keepdims=True))
        alpha = jnp.exp(m_prev - m_new)
        p = jnp.exp(s - pltpu.repeat(m_new, KEY_CHUNK // LANES, axis=1))
        pv = jnp.dot(p.astype(BF16), vaug, preferred_element_type=F32)
        acc_scr[hk] = pltpu.repeat(alpha, 2, axis=1) * acc_scr[hk] + pv
        m_scr[hk] = m_new

    def finish(hk):
        acc = acc_scr[hk]
        return acc[:, 0:HEAD_DIM] / acc[:, HEAD_DIM:]

    def chunk_bias(hk, ta, tb):
        return jnp.concatenate([tz_ref[jnp.clip(ta, 0, 2), hk], tz_ref[jnp.clip(tb, 0, 2), hk]], axis=1)

    for hk in range(nkv):
        for g in range(NSA_GROUP):
            h = hk * NSA_GROUP + g
            qg_scr[hk, g * LANES:(g + 1) * LANES, :] = q_ref[0, :, h * HEAD_DIM:(h + 1) * HEAD_DIM]

    for hk in range(nkv):
        qg = qg_scr[hk]

        kc = kcvc_ref[0, 0, hk]
        vc = kcvc_ref[0, 1, hk]
        s = lax.dot_general(qg, kc, NT_DIMS, preferred_element_type=F32) * SCALE + bc_ref[0, hk]
        rc = lax.broadcasted_iota(jnp.int32, (LANES, ncp), 0)
        cc = lax.broadcasted_iota(jnp.int32, (LANES, ncp), 1)
        valid_c = _tile4(jnp.where(q0 + rc - (CMP_STRIDE * cc + CMP_BLOCK - 1) >= 0, 1.0, 0.0)) > 0.5
        s = jnp.where(valid_c, s, NEG)
        p = jnp.where(valid_c, jnp.exp(s - jnp.max(s, axis=1, keepdims=True)), 0.0)
        l = jnp.sum(p, axis=1, keepdims=True)
        p = p / jnp.where(l > 0.0, l, 1.0)
        o_scr[hk] = gate_col(hk, 0) * jnp.dot(p.astype(BF16), vc, preferred_element_type=F32)

        psum = p[0:LANES]
        for g in range(1, NSA_GROUP):
            psum = psum + p[g * LANES:(g + 1) * LANES]
        p_hi = psum.astype(BF16)
        p_lo = (psum - p_hi.astype(F32)).astype(BF16)
        selmt = selmt_ref[...]
        imp = (lax.dot_general(selmt, p_hi, NT_DIMS, preferred_element_type=F32)
               + lax.dot_general(selmt, p_lo, NT_DIMS, preferred_element_type=F32))
        imp = imp[0:n_slc]
        blk = lax.broadcasted_iota(jnp.int32, (n_slc, LANES), 0)
        cur = (q0 + lax.broadcasted_iota(jnp.int32, (n_slc, LANES), 1)) // SEL_BLOCK
        forced = (blk == 0) | (blk == cur) | (blk == cur - 1)
        imp = jnp.where(forced, FORCE, imp)
        imp = jnp.where(blk <= cur, imp, -jnp.inf)
        rank = jnp.zeros((n_slc, LANES), F32)
        for j in range(n_slc):
            row = imp[j:j + 1, :]
            beats = jnp.where(row > imp, 1.0, jnp.where(row == imp, jnp.where(blk > j, 1.0, 0.0), 0.0))
            rank = rank + beats
        sel_t = jnp.where(rank < top_n, jnp.where(imp > -jnp.inf, 1.0, 0.0), 0.0)
        if n_slc < LANES:
            sel_t = jnp.concatenate([sel_t, jnp.zeros((LANES - n_slc, LANES), F32)], axis=0)
        sel_q = lax.dot_general(eye, sel_t.astype(BF16), NT_DIMS, preferred_element_type=F32)
        selq_scr[hk] = jnp.where(ci == LANES - 1, 1.0, sel_q).astype(BF16)

    reset()

    def sel_chunk(c, last):
        k0 = pl.multiple_of(c * KEY_CHUNK, KEY_CHUNK)
        ta = qt - 2 * c
        for hk in range(nkv):
            hc = slice(hk * HEAD_DIM, (hk + 1) * HEAD_DIM)
            k = ks_ref[0, pl.ds(k0, KEY_CHUNK), hc]
            madd = jnp.dot(selq_scr[hk], e_ref[c], preferred_element_type=F32)
            if last:
                madd = jnp.where((k0 + ck) <= (q0 + rk), madd, -MASK_BIG)
            s = (lax.dot_general(qg_scr[hk], k, NT_DIMS, preferred_element_type=F32) * SCALE
                 + chunk_bias(hk, ta, ta - 1) + _tile4(madd))
            online_update(hk, s, vsa_scr[hk, pl.ds(k0, KEY_CHUNK), :])

    def sel_body(c, carry):
        sel_chunk(c, False)
        return carry

    n_chunks = (qt + 2) // 2
    lax.fori_loop(0, n_chunks - 1, sel_body, 0)
    sel_chunk(n_chunks - 1, True)
    for hk in range(nkv):
        o_scr[hk] = o_scr[hk] + gate_col(hk, 1) * finish(hk)

    reset()

    def win_chunk(off):
        kstart = (qt - off) * LANES
        pos = kstart + ck
        d = (q0 + rk) - pos
        ok = (d >= 0) & (d < WINDOW) & (pos >= 0)
        madd = _tile4(jnp.where(ok, 0.0, -MASK_BIG))
        p0 = pl.multiple_of(kstart + WIN_PAD, LANES)
        for hk in range(nkv):
            k = kwp_scr[hk, pl.ds(p0, KEY_CHUNK), :]
            s = (lax.dot_general(qg_scr[hk], k, NT_DIMS, preferred_element_type=F32) * SCALE
                 + chunk_bias(hk, off, off - 1) + madd)
            online_update(hk, s, vwa_scr[hk, pl.ds(p0, KEY_CHUNK), :])

    win_chunk(1)
    for off in (3, 5):
        pl.when(qt >= off - 1)(functools.partial(win_chunk, off))
    for hk in range(nkv):
        o = o_scr[hk] + gate_col(hk, 2) * finish(hk)
        for g in range(NSA_GROUP):
            h = hk * NSA_GROUP + g
            o_ref[0, :, h * HEAD_DIM:(h + 1) * HEAD_DIM] = o[g * LANES:(g + 1) * LANES]


def _nsa_call(o16v, o32v, kcvc, bias_c, tz, selmt, e3, batch, seq, ncp):
    nqt = seq // LANES
    kvw = NSA_KV_WIDTH
    nkv = N_NSA_KV_HEADS
    base = (3 * FOX_WIDTH + NSA_WIDTH) // kvw
    return pl.pallas_call(
        functools.partial(_nsa_kernel, seq=seq, ncp=ncp),
        grid=(batch, nqt),
        in_specs=[pl.BlockSpec((1, LANES, NSA_WIDTH), lambda b, t: (b, t, 3 * FOX_WIDTH // NSA_WIDTH)),
                  pl.BlockSpec((1, seq, kvw), lambda b, t: (b, 0, base)),
                  pl.BlockSpec((1, seq, kvw), lambda b, t: (b, 0, base + 1)),
                  pl.BlockSpec((1, seq, kvw), lambda b, t: (b, 0, base + 2)),
                  pl.BlockSpec((1, seq, kvw), lambda b, t: (b, 0, base + 3)),
                  pl.BlockSpec((1, LANES, LANES), lambda b, t: (b, t, COL_NG // LANES)),
                  pl.BlockSpec((1, 2, N_NSA_KV_HEADS, ncp, HEAD_DIM), lambda b, t: (b, 0, 0, 0, 0)),
                  pl.BlockSpec((1, N_NSA_KV_HEADS, GROUP_ROWS, ncp), lambda b, t: (t, 0, 0, 0)),
                  pl.BlockSpec((3, N_NSA_KV_HEADS, GROUP_ROWS, LANES), lambda b, t: (0, 0, 0, 0)),
                  pl.BlockSpec((LANES, ncp), lambda b, t: (0, 0)),
                  pl.BlockSpec((seq // KEY_CHUNK, LANES, KEY_CHUNK), lambda b, t: (0, 0, 0))],
        out_specs=pl.BlockSpec((1, LANES, NSA_WIDTH), lambda b, t: (b, t, 0)),
        out_shape=jax.ShapeDtypeStruct((batch, seq, NSA_WIDTH), F32),
        scratch_shapes=[pltpu.VMEM((nkv, GROUP_ROWS, HEAD_DIM), BF16),
                        pltpu.VMEM((nkv, GROUP_ROWS, LANES), F32),
                        pltpu.VMEM((nkv, GROUP_ROWS, 2 * HEAD_DIM), F32),
                        pltpu.VMEM((nkv, GROUP_ROWS, HEAD_DIM), F32),
                        pltpu.VMEM((nkv, LANES, LANES), BF16),
                        pltpu.VMEM((nkv, seq, 2 * HEAD_DIM), BF16),
                        pltpu.VMEM((nkv, seq + WIN_PAD, HEAD_DIM), BF16),
                        pltpu.VMEM((nkv, seq + WIN_PAD, 2 * HEAD_DIM), BF16)],
        compiler_params=_params(2),
        name="nsa_attention",
    )(o16v, o16v, o16v, o16v, o16v, o32v, kcvc, bias_c, tz, selmt, e3)


def _out_kernel(of_ref, on_ref, gain_ref, w_ref, x_ref, g_ref, o_ref, y_scr):
    j = pl.program_id(1)

    @pl.when(j == 0)
    def _():
        y_scr[:, 0:FOX_WIDTH] = (_rms(of_ref[...]) * gain_ref[:, 0:FOX_WIDTH]).astype(BF16)
        y_scr[:, FOX_WIDTH:MIX_WIDTH] = (_rms(on_ref[...]) * gain_ref[:, FOX_WIDTH:MIX_WIDTH]).astype(BF16)

    acc = jnp.dot(y_scr[...], w_ref[...], preferred_element_type=F32)
    o_ref[...] = x_ref[...] + g_ref[0] * acc


def _out_call(o_fox, o_nsa, gain, w_out, x2, g1, seq):
    t, d = x2.shape
    tm = min(512, seq)
    tn = min(1024, d)
    rows_per_batch = seq // tm
    return pl.pallas_call(
        _out_kernel,
        grid=(t // tm, d // tn),
        in_specs=[pl.BlockSpec((tm, FOX_WIDTH), lambda i, j: (i, 0)),
                  pl.BlockSpec((tm, NSA_WIDTH), lambda i, j: (i, 0)),
                  pl.BlockSpec((1, MIX_WIDTH), lambda i, j: (0, 0)),
                  pl.BlockSpec((MIX_WIDTH, tn), lambda i, j: (0, j)),
                  pl.BlockSpec((tm, tn), lambda i, j: (i, j)),
                  pl.BlockSpec((1, 1, tn), lambda i, j: (i // rows_per_batch, 0, j))],
        out_specs=pl.BlockSpec((tm, tn), lambda i, j: (i, j)),
        out_shape=jax.ShapeDtypeStruct((t, d), F32),
        scratch_shapes=[pltpu.VMEM((tm, MIX_WIDTH), BF16)],
        compiler_params=_params(2),
        name="out_proj",
    )(o_fox, o_nsa, gain, w_out, x2, g1)


def _ffn_kernel(x_ref, gain_ref, sc_ref, sh_ref, g_ref, wg_ref, wu_ref, wd_ref, o_ref, h_scr, acc_scr):
    f = pl.program_id(1)

    @pl.when(f == 0)
    def _():
        h = _rms(x_ref[...]) * gain_ref[...]
        h = h * (1.0 + sc_ref[0]) + sh_ref[0]
        h_scr[...] = h.astype(BF16)
        acc_scr[...] = jnp.zeros(acc_scr.shape, F32)

    h = h_scr[...]
    a = jnp.dot(h, wg_ref[...], preferred_element_type=F32)
    u = jnp.dot(h, wu_ref[...], preferred_element_type=F32)
    t = (a * _sigmoid(a)) * u
    acc_scr[...] += jnp.dot(t.astype(BF16), wd_ref[...], preferred_element_type=F32)

    @pl.when(f == pl.num_programs(1) - 1)
    def _():
        o_ref[...] = x_ref[...] + g_ref[0] * acc_scr[...]


def _ffn_call(x1, gain2, sc, sh, g2, wg, wu, wd, seq):
    t, d = x1.shape
    dff = wg.shape[1]
    tm = min(512, seq)
    tf = 512 if dff % 512 == 0 else dff
    rows_per_batch = seq // tm
    return pl.pallas_call(
        _ffn_kernel,
        grid=(t // tm, dff // tf),
        in_specs=[pl.BlockSpec((tm, d), lambda i, f: (i, 0)),
                  pl.BlockSpec((1, d), lambda i, f: (0, 0)),
                  pl.BlockSpec((1, 1, d), lambda i, f: (i // rows_per_batch, 0, 0)),
                  pl.BlockSpec((1, 1, d), lambda i, f: (i // rows_per_batch, 0, 0)),
                  pl.BlockSpec((1, 1, d), lambda i, f: (i // rows_per_batch, 0, 0)),
                  pl.BlockSpec((d, tf), lambda i, f: (0, f)),
                  pl.BlockSpec((d, tf), lambda i, f: (0, f)),
                  pl.BlockSpec((tf, d), lambda i, f: (f, 0))],
        out_specs=pl.BlockSpec((tm, d), lambda i, f: (i, 0)),
        out_shape=jax.ShapeDtypeStruct((t, d), F32),
        scratch_shapes=[pltpu.VMEM((tm, d), BF16), pltpu.VMEM((tm, d), F32)],
        compiler_params=_params(2),
        name="swiglu_ffn",
    )(x1, gain2, sc, sh, g2, wg, wu, wd)


def _selection_matrix_t(ncp, n_slc):
    r, q = SEL_BLOCK // CMP_STRIDE, CMP_BLOCK // CMP_STRIDE
    m = np.zeros((LANES, ncp), np.float32)
    for j in range(n_slc):
        for a in range(r):
            for b in range(q):
                c = r * j + a - b
                if 0 <= c < ncp:
                    m[j, c] += 1.0
    return m


def _block_expander(seq):
    assert seq // SEL_BLOCK < LANES
    keys = np.arange(seq).reshape(seq // KEY_CHUNK, 1, KEY_CHUNK)
    blocks = np.arange(LANES).reshape(1, LANES, 1)
    e = (keys // SEL_BLOCK == blocks).astype(np.float32) * MASK_BIG
    e[:, LANES - 1, :] = -MASK_BIG
    return e


def _permute_w_in(w):
    d = w.shape[0]
    o = 0
    parts = {}
    for name, width in (("fq", FOX_WIDTH), ("fk", FOX_WIDTH), ("fv", FOX_WIDTH), ("ff", N_FOX_HEADS),
                        ("nq", NSA_WIDTH), ("nk", N_BRANCH * NSA_KV_WIDTH), ("nv", N_BRANCH * NSA_KV_WIDTH),
                        ("ng", N_BRANCH * N_NSA_HEADS)):
        parts[name] = w[:, o:o + width]
        o += width
    kvw = NSA_KV_WIDTH
    pad = jnp.zeros((d, W32 - (2 * kvw + N_FOX_HEADS + N_BRANCH * N_NSA_HEADS)), w.dtype)
    return jnp.concatenate([
        parts["fq"], parts["fk"], parts["fv"], parts["nq"],
        parts["nk"][:, kvw:2 * kvw], parts["nk"][:, 2 * kvw:3 * kvw],
        parts["nv"][:, kvw:2 * kvw], parts["nv"][:, 2 * kvw:3 * kvw],
        parts["nk"][:, 0:kvw], parts["nv"][:, 0:kvw], parts["ff"], parts["ng"], pad], axis=1).astype(BF16)


def kernel(x, c, ada_w, ada_b, norm1_gain, norm2_gain, w_in, fox_f_bias, fox_q_gain, fox_k_gain, nsa_q_gain,
           nsa_k_gain, nsa_cmp_pos, nsa_cmp_w1, nsa_cmp_w2, rel_bias, mix_out_gain, w_out, ffn_w_gate, ffn_w_up,
           ffn_w_down):
    batch, seq, d = x.shape
    assert seq % LANES == 0 and seq >= WINDOW and d % LANES == 0
    depth = ada_w.shape[0]
    nqt = seq // LANES
    ncp = -(-(seq // CMP_STRIDE) // LANES) * LANES
    tq_fox = min(256, seq)

    selmt = jnp.asarray(_selection_matrix_t(ncp, seq // SEL_BLOCK), BF16)
    e3 = jnp.asarray(_block_expander(seq), BF16)
    rb_flat = rel_bias.reshape(-1)
    bias_c = _bias_call(rb_flat, nqt, ncp, CMP_STRIDE, CMP_BLOCK - 1, "t5_bias_compressed")
    tz = _bias_call(rb_flat, 3, LANES, 1, 0, "t5_bias_toeplitz")

    ones_h = jnp.ones((HEAD_DIM,), F32)
    c_pad = jnp.pad(c, ((0, 8 - batch % 8 if batch % 8 else 0), (0, 0)))
    x2 = x.reshape(batch * seq, d)
    for layer in range(depth):
        mod = _ada_call(c_pad, ada_w[layer], ada_b[layer][None, :])[:batch]
        sh1, sc1, g1, sh2, sc2, g2 = [mod[:, i * d:(i + 1) * d][:, None, :] for i in range(N_MOD)]

        kg = nsa_k_gain[layer]
        col_gain = jnp.concatenate([
            jnp.tile(fox_q_gain[layer], N_FOX_HEADS), jnp.tile(fox_k_gain[layer], N_FOX_HEADS),
            jnp.tile(ones_h, N_FOX_HEADS), jnp.tile(nsa_q_gain[layer], N_NSA_HEADS),
            jnp.tile(kg[1], N_NSA_KV_HEADS), jnp.tile(kg[2], N_NSA_KV_HEADS),
            jnp.tile(ones_h, 2 * N_NSA_KV_HEADS)])[None, :]
        col_flag = jnp.concatenate([
            jnp.ones((2 * FOX_WIDTH,), F32), jnp.zeros((FOX_WIDTH,), F32), jnp.ones((NSA_WIDTH,), F32),
            jnp.ones((2 * NSA_KV_WIDTH,), F32), jnp.zeros((2 * NSA_KV_WIDTH,), F32)])[None, :]
        o16, o32 = _proj_call(x2, sc1, sh1, norm1_gain[layer][None, :], _permute_w_in(w_in[layer]),
                              col_gain, col_flag, seq)
        o16v = o16.reshape(batch, seq, W16)
        o32v = o32.reshape(batch, seq, W32)

        fb_pad = jnp.pad(fox_f_bias[layer], (0, LANES - N_FOX_HEADS))[None, :]
        cum = _cum_call(o32, fb_pad, batch, seq)
        cumt = jnp.transpose(cum[:, :, :N_FOX_HEADS], (0, 2, 1)).reshape(batch, N_FOX_HEADS, seq // tq_fox, tq_fox)
        o_fox = _fox_call(o16v, cum, cumt, batch, seq, tq_fox)

        w1 = nsa_cmp_w1[layer].reshape(2, CMP_BLOCK, HEAD_DIM, HEAD_DIM).astype(BF16)
        kcvc = _cmp_call(o32, nsa_cmp_pos[layer], w1, nsa_cmp_w2[layer].astype(BF16), kg[0][None, :],
                         batch, seq, ncp)
        o_nsa = _nsa_call(o16v, o32v, kcvc, bias_c, tz, selmt, e3, batch, seq, ncp)

        x1 = _out_call(o_fox.reshape(batch * seq, FOX_WIDTH), o_nsa.reshape(batch * seq, NSA_WIDTH),
                       mix_out_gain[layer][None, :], w_out[layer].astype(BF16), x2, g1, seq)
        x2 = _ffn_call(x1, norm2_gain[layer][None, :], sc2, sh2, g2, ffn_w_gate[layer].astype(BF16),
                       ffn_w_up[layer].astype(BF16), ffn_w_down[layer].astype(BF16), seq)
    return x2.reshape(batch, seq, d)
```

```python
import functools
import math

import numpy as np
import jax
import jax.numpy as jnp
from jax import lax
from jax.experimental import pallas as pl
from jax.experimental.pallas import tpu as pltpu

HEAD_DIM = 128
N_FOX_HEADS = 8
N_NSA_HEADS = 8
N_NSA_KV_HEADS = 2
NSA_GROUP = N_NSA_HEADS // N_NSA_KV_HEADS
FOX_WIDTH = N_FOX_HEADS * HEAD_DIM
NSA_WIDTH = N_NSA_HEADS * HEAD_DIM
NSA_KV_WIDTH = N_NSA_KV_HEADS * HEAD_DIM
MIX_WIDTH = FOX_WIDTH + NSA_WIDTH
N_BRANCH = 3
CMP_BLOCK = 32
CMP_STRIDE = 16
SEL_BLOCK = 64
N_SEL = 8
WINDOW = 512
N_BUCKETS = 32
MAX_DISTANCE = 128
N_MOD = 6
SCALE = HEAD_DIM ** -0.5
EPS = 1e-6
NEG = -1e30
FORCE = 1e6

LANES = 128
GROUP_ROWS = NSA_GROUP * LANES
VMEM_LIMIT = 56 * 1024 * 1024
MXU_COLS = 256
KEY_CHUNK = MXU_COLS
WIN_PAD = WINDOW + LANES
MASK_BIG = 2.0 ** 100

W16 = 3 * FOX_WIDTH + NSA_WIDTH + 4 * NSA_KV_WIDTH
W32 = 1024
COL_FF = 2 * NSA_KV_WIDTH
COL_NG = COL_FF + N_FOX_HEADS

F32 = jnp.float32
BF16 = jnp.bfloat16
NT_DIMS = (((1,), (1,)), ((), ()))


def _params(n_axes):
    return pltpu.CompilerParams(dimension_semantics=("arbitrary",) * n_axes,
                                vmem_limit_bytes=VMEM_LIMIT)


def _sigmoid(x):
    return 1.0 / (1.0 + jnp.exp(-x))


def _lane_tile(a, n):
    return jnp.concatenate([a] * n, axis=1)


def _rms(x):
    return x * lax.rsqrt(jnp.mean(x * x, axis=-1, keepdims=True) + EPS)


def _ada_kernel(c_ref, w_ref, b_ref, o_ref):
    c = c_ref[...]
    s = (c * _sigmoid(c)).astype(BF16)
    o_ref[...] = jnp.dot(s, w_ref[...].astype(BF16), preferred_element_type=F32) + b_ref[...]


def _ada_call(c_pad, w, b):
    rows, d = c_pad.shape
    n = w.shape[1]
    tn = next(t for t in (1024, 768, 512, 384, 256, 128) if n % t == 0)
    return pl.pallas_call(
        _ada_kernel,
        grid=(n // tn,),
        in_specs=[pl.BlockSpec((rows, d), lambda j: (0, 0)),
                  pl.BlockSpec((d, tn), lambda j: (0, j)),
                  pl.BlockSpec((1, tn), lambda j: (0, j))],
        out_specs=pl.BlockSpec((rows, tn), lambda j: (0, j)),
        out_shape=jax.ShapeDtypeStruct((rows, n), F32),
        compiler_params=_params(1),
        name="adaln",
    )(c_pad, w, b)


def _proj_kernel(x_ref, sc_ref, sh_ref, g_ref, w_ref, gain_ref, flag_ref, o16_ref, o32_ref, h_scr, *, n16, tn):
    j = pl.program_id(1)

    @pl.when(j == 0)
    def _():
        h = _rms(x_ref[...]) * g_ref[...]
        h = h * (1.0 + sc_ref[0]) + sh_ref[0]
        h_scr[...] = h.astype(BF16)

    @pl.when(j < n16)
    def _():
        h = h_scr[...]
        for c in range(tn // MXU_COLS):
            acc = jnp.dot(h, w_ref[:, c * MXU_COLS:(c + 1) * MXU_COLS], preferred_element_type=F32)
            for g in range(MXU_COLS // LANES):
                cols = slice(c * MXU_COLS + g * LANES, c * MXU_COLS + (g + 1) * LANES)
                a = acc[:, g * LANES:(g + 1) * LANES]
                r = lax.rsqrt(jnp.mean(a * a, axis=-1, keepdims=True) + EPS)
                scale = jnp.where(flag_ref[:, cols] > 0.5, r, 1.0)
                o16_ref[:, cols] = (a * scale * gain_ref[:, cols]).astype(BF16)

    @pl.when(j >= n16)
    def _():
        o32_ref[...] = jnp.dot(h_scr[...], w_ref[...], preferred_element_type=F32)


def _proj_call(x2, sc, sh, gain1, w_perm, col_gain, col_flag, seq):
    t, d = x2.shape
    tm = min(512, seq)
    tn = W32
    n16 = W16 // tn
    rows_per_batch = seq // tm
    return pl.pallas_call(
        functools.partial(_proj_kernel, n16=n16, tn=tn),
        grid=(t // tm, n16 + 1),
        in_specs=[pl.BlockSpec((tm, d), lambda i, j: (i, 0)),
                  pl.BlockSpec((1, 1, d), lambda i, j: (i // rows_per_batch, 0, 0)),
                  pl.BlockSpec((1, 1, d), lambda i, j: (i // rows_per_batch, 0, 0)),
                  pl.BlockSpec((1, d), lambda i, j: (0, 0)),
                  pl.BlockSpec((d, tn), lambda i, j: (0, j)),
                  pl.BlockSpec((1, tn), lambda i, j: (0, jnp.minimum(j, n16 - 1))),
                  pl.BlockSpec((1, tn), lambda i, j: (0, jnp.minimum(j, n16 - 1)))],
        out_specs=[pl.BlockSpec((tm, tn), lambda i, j: (i, jnp.minimum(j, n16 - 1))),
                   pl.BlockSpec((tm, tn), lambda i, j: (i, 0))],
        out_shape=[jax.ShapeDtypeStruct((t, W16), BF16),
                   jax.ShapeDtypeStruct((t, W32), F32)],
        scratch_shapes=[pltpu.VMEM((tm, d), BF16)],
        compiler_params=_params(2),
        name="in_proj",
    )(x2, sc, sh, gain1, w_perm, col_gain, col_flag)


def _cum_kernel(ff_ref, fb_ref, o_ref, *, seq):
    ri = lax.broadcasted_iota(jnp.int32, (LANES, LANES), 0)
    ci = lax.broadcasted_iota(jnp.int32, (LANES, LANES), 1)
    tri = jnp.where(ri >= ci, 1.0, 0.0).astype(BF16)
    carry = jnp.zeros((1, LANES), F32)
    for blk in range(seq // LANES):
        rows = slice(blk * LANES, (blk + 1) * LANES)
        x = ff_ref[0, rows, :] + fb_ref[...]
        lf = jnp.minimum(x, 0.0) - jnp.log(1.0 + jnp.exp(-jnp.abs(x)))
        hi = lf.astype(BF16)
        r1 = lf - hi.astype(F32)
        mid = r1.astype(BF16)
        lo = (r1 - mid.astype(F32)).astype(BF16)
        c = (jnp.dot(tri, hi, preferred_element_type=F32)
             + jnp.dot(tri, mid, preferred_element_type=F32)
             + jnp.dot(tri, lo, preferred_element_type=F32)) + carry
        o_ref[0, rows, :] = c
        carry = c[LANES - 1:LANES, :]


def _cum_call(o32, fb_pad, batch, seq):
    o32v = o32.reshape(batch, seq, W32)
    return pl.pallas_call(
        functools.partial(_cum_kernel, seq=seq),
        grid=(batch,),
        in_specs=[pl.BlockSpec((1, seq, LANES), lambda b: (b, 0, COL_FF // LANES)),
                  pl.BlockSpec((1, LANES), lambda b: (0, 0))],
        out_specs=pl.BlockSpec((1, seq, LANES), lambda b: (b, 0, 0)),
        out_shape=jax.ShapeDtypeStruct((batch, seq, LANES), F32),
        compiler_params=_params(1),
        name="fox_cumsum",
    )(o32v, fb_pad)


def _fox_kernel(q_ref, k_ref, v_ref, cum_ref, cumt_ref, o_ref, m_scr, acc_scr, vaug_scr, *, seq, tq, nh):
    hp = pl.program_id(1)
    lane = lax.broadcasted_iota(jnp.int32, (tq, LANES), 1)
    causal = lax.broadcasted_iota(jnp.int32, (tq, tq), 1) <= lax.broadcasted_iota(jnp.int32, (tq, tq), 0)
    for j in range(nh):
        vaug_scr[j, :, 0:HEAD_DIM] = v_ref[0, :, j * HEAD_DIM:(j + 1) * HEAD_DIM]
        vaug_scr[j, :, HEAD_DIM:2 * HEAD_DIM] = jnp.ones((seq, HEAD_DIM), BF16)

    def q_body(qi, carry):
        q0 = pl.multiple_of(qi * tq, tq)
        cum_t = cum_ref[0, pl.ds(q0, tq), :]
        qs, cqs = [], []
        for j in range(nh):
            qs.append(q_ref[0, pl.ds(q0, tq), j * HEAD_DIM:(j + 1) * HEAD_DIM])
            cqs.append(jnp.sum(jnp.where(lane == hp * nh + j, cum_t, 0.0), axis=1, keepdims=True))
            m_scr[j] = jnp.full((tq, LANES), NEG, F32)
            acc_scr[j] = jnp.zeros((tq, 2 * HEAD_DIM), F32)

        def tile(ki, diagonal):
            k0 = pl.multiple_of(ki * tq, tq)
            for j in range(nh):
                k = k_ref[0, pl.ds(k0, tq), j * HEAD_DIM:(j + 1) * HEAD_DIM]
                ck = cumt_ref[0, j, pl.ds(ki, 1), :]
                s = lax.dot_general(qs[j], k, NT_DIMS, preferred_element_type=F32) * SCALE
                s = (s + cqs[j]) - ck
                if diagonal:
                    s = jnp.where(causal, s, NEG)
                m_prev = m_scr[j]
                m_new = jnp.maximum(m_prev, jnp.max(s, axis=1, keepdims=True))
                alpha = jnp.exp(m_prev - m_new)
                p = jnp.exp(s - _lane_tile(m_new, tq // LANES))
                pv = jnp.dot(p.astype(BF16), vaug_scr[j, pl.ds(k0, tq), :], preferred_element_type=F32)
                acc_scr[j] = _lane_tile(alpha, 2) * acc_scr[j] + pv
                m_scr[j] = m_new

        def k_body(ki, c2):
            tile(ki, False)
            return c2

        lax.fori_loop(0, qi, k_body, 0)
        tile(qi, True)
        for j in range(nh):
            acc = acc_scr[j]
            o_ref[0, pl.ds(q0, tq), j * HEAD_DIM:(j + 1) * HEAD_DIM] = acc[:, 0:HEAD_DIM] / acc[:, HEAD_DIM:]
        return carry

    lax.fori_loop(0, seq // tq, q_body, 0)


def _fox_call(o16v, cum, cumt, batch, seq, tq):
    nh = 4
    groups = N_FOX_HEADS // nh
    w = nh * HEAD_DIM
    return pl.pallas_call(
        functools.partial(_fox_kernel, seq=seq, tq=tq, nh=nh),
        grid=(batch, groups),
        in_specs=[pl.BlockSpec((1, seq, w), lambda b, h: (b, 0, h)),
                  pl.BlockSpec((1, seq, w), lambda b, h: (b, 0, groups + h)),
                  pl.BlockSpec((1, seq, w), lambda b, h: (b, 0, 2 * groups + h)),
                  pl.BlockSpec((1, seq, LANES), lambda b, h: (b, 0, 0)),
                  pl.BlockSpec((1, nh, seq // tq, tq), lambda b, h: (b, h, 0, 0))],
        out_specs=pl.BlockSpec((1, seq, w), lambda b, h: (b, 0, h)),
        out_shape=jax.ShapeDtypeStruct((batch, seq, FOX_WIDTH), F32),
        scratch_shapes=[pltpu.VMEM((nh, tq, LANES), F32),
                        pltpu.VMEM((nh, tq, 2 * HEAD_DIM), F32),
                        pltpu.VMEM((nh, seq, 2 * HEAD_DIM), BF16)],
        compiler_params=_params(2),
        name="fox_attention",
    )(o16v, o16v, o16v, cum, cumt)


def _cmp_kernel(x_ref, pos_ref, w1_ref, w2_ref, gain_ref, o_ref, xs_scr, *, seq, ncp):
    kv = pl.program_id(1)
    rows = xs_scr.shape[0]
    xs_scr[0:seq, :] = x_ref[0]
    xs_scr[seq:rows, :] = jnp.zeros((rows - seq, LANES), F32)
    acc = jnp.zeros((ncp, HEAD_DIM), F32)
    for l in range(CMP_BLOCK):
        xl = xs_scr[pl.ds(l, ncp, stride=CMP_STRIDE), :] + pos_ref[0, l:l + 1, :]
        acc = acc + jnp.dot(xl.astype(BF16), w1_ref[0, l], preferred_element_type=F32)
    hmid = acc * _sigmoid(acc)
    y = jnp.dot(hmid.astype(BF16), w2_ref[0], preferred_element_type=F32)
    yk = _rms(y) * gain_ref[...]
    o_ref[0, 0, 0] = jnp.where(kv == 0, yk, y).astype(BF16)


def _cmp_call(o32, pos, w1, w2, gain, batch, seq, ncp):
    o32v = o32.reshape(batch, seq, W32)
    hkv = N_NSA_KV_HEADS
    return pl.pallas_call(
        functools.partial(_cmp_kernel, seq=seq, ncp=ncp),
        grid=(batch, 2, hkv),
        in_specs=[pl.BlockSpec((1, seq, HEAD_DIM), lambda b, kv, h: (b, 0, kv * hkv + h)),
                  pl.BlockSpec((1, CMP_BLOCK, HEAD_DIM), lambda b, kv, h: (kv, 0, 0)),
                  pl.BlockSpec((1, CMP_BLOCK, HEAD_DIM, HEAD_DIM), lambda b, kv, h: (kv, 0, 0, 0)),
                  pl.BlockSpec((1, HEAD_DIM, HEAD_DIM), lambda b, kv, h: (kv, 0, 0)),
                  pl.BlockSpec((1, HEAD_DIM), lambda b, kv, h: (0, 0))],
        out_specs=pl.BlockSpec((1, 1, 1, ncp, HEAD_DIM), lambda b, kv, h: (b, kv, h, 0, 0)),
        out_shape=jax.ShapeDtypeStruct((batch, 2, hkv, ncp, HEAD_DIM), BF16),
        scratch_shapes=[pltpu.VMEM((CMP_STRIDE * ncp + CMP_BLOCK, LANES), F32)],
        compiler_params=_params(3),
        name="nsa_compress",
    )(o32v, pos, w1, w2, gain)


def _bias_kernel(rb_ref, o_ref, *, width, key_stride, key_offset):
    v = pl.program_id(0)
    i = lax.broadcasted_iota(jnp.int32, (LANES, width), 0)
    j = lax.broadcasted_iota(jnp.int32, (LANES, width), 1)
    d = v * LANES + i - (key_stride * j + key_offset)
    n = jnp.maximum(d, 0)
    max_exact = N_BUCKETS // 2
    nf = jnp.maximum(n, 1).astype(F32)
    large = max_exact + (jnp.log(nf / max_exact) / math.log(MAX_DISTANCE / max_exact)
                         * (N_BUCKETS - max_exact)).astype(jnp.int32)
    large = jnp.minimum(large, N_BUCKETS - 1)
    bkt = jnp.where(n < max_exact, n, large)
    vals = [jnp.zeros((LANES, width), F32) for _ in range(N_NSA_HEADS)]
    for bk in range(N_BUCKETS):
        hit = bkt == bk
        for h in range(N_NSA_HEADS):
            vals[h] = jnp.where(hit, rb_ref[bk * N_NSA_HEADS + h], vals[h])
    for h in range(N_NSA_HEADS):
        g = h % NSA_GROUP
        o_ref[0, h // NSA_GROUP, g * LANES:(g + 1) * LANES, :] = vals[h]


def _bias_call(rb_flat, n_variants, width, key_stride, key_offset, name):
    return pl.pallas_call(
        functools.partial(_bias_kernel, width=width, key_stride=key_stride, key_offset=key_offset),
        grid=(n_variants,),
        in_specs=[pl.BlockSpec(memory_space=pltpu.SMEM)],
        out_specs=pl.BlockSpec((1, N_NSA_KV_HEADS, GROUP_ROWS, width), lambda v: (v, 0, 0, 0)),
        out_shape=jax.ShapeDtypeStruct((n_variants, N_NSA_KV_HEADS, GROUP_ROWS, width), F32),
        compiler_params=_params(1),
        name=name,
    )(rb_flat)


def _tile4(a):
    return jnp.concatenate([a] * NSA_GROUP, axis=0)


def _nsa_kernel(q_ref, ks_ref, kw_ref, vs_ref, vw_ref, g_ref, kcvc_ref, bc_ref, tz_ref, selmt_ref, e_ref,
                o_ref, qg_scr, m_scr, acc_scr, o_scr, selq_scr, vsa_scr, kwp_scr, vwa_scr, *, seq, ncp):
    qt = pl.program_id(1)
    q0 = qt * LANES
    n_slc = seq // SEL_BLOCK
    top_n = min(N_SEL, n_slc)
    nkv = N_NSA_KV_HEADS
    ri = lax.broadcasted_iota(jnp.int32, (LANES, LANES), 0)
    ci = lax.broadcasted_iota(jnp.int32, (LANES, LANES), 1)
    rk = lax.broadcasted_iota(jnp.int32, (LANES, KEY_CHUNK), 0)
    ck = lax.broadcasted_iota(jnp.int32, (LANES, KEY_CHUNK), 1)
    eye = jnp.where(ri == ci, 1.0, 0.0).astype(BF16)
    gates = _sigmoid(g_ref[0])

    def gate_col(hk, br):
        cols = []
        for g in range(NSA_GROUP):
            c = COL_NG % LANES + (hk * NSA_GROUP + g) * N_BRANCH + br
            cols.append(gates[:, c:c + 1])
        return jnp.concatenate(cols, axis=0)

    @pl.when(qt == 0)
    def _():
        ones = jnp.ones((seq, HEAD_DIM), BF16)
        for hk in range(nkv):
            hc = slice(hk * HEAD_DIM, (hk + 1) * HEAD_DIM)
            vsa_scr[hk, :, 0:HEAD_DIM] = vs_ref[0, :, hc]
            vsa_scr[hk, :, HEAD_DIM:] = ones
            kwp_scr[hk, 0:WIN_PAD, :] = jnp.zeros((WIN_PAD, HEAD_DIM), BF16)
            kwp_scr[hk, WIN_PAD:, :] = kw_ref[0, :, hc]
            vwa_scr[hk, 0:WIN_PAD, :] = jnp.zeros((WIN_PAD, 2 * HEAD_DIM), BF16)
            vwa_scr[hk, WIN_PAD:, 0:HEAD_DIM] = vw_ref[0, :, hc]
            vwa_scr[hk, WIN_PAD:, HEAD_DIM:] = ones

    def reset():
        m_scr[...] = jnp.full(m_scr.shape, NEG, F32)
        acc_scr[...] = jnp.zeros(acc_scr.shape, F32)

    def online_update(hk, s, vaug):
        m_prev = m_scr[hk]
        m_new = jnp.maximum(m_prev, jnp.max(s, axis=1, keepdims=True))
        alpha = jnp.exp(m_prev - m_new)
        p = jnp.exp(s - _lane_tile(m_new, KEY_CHUNK // LANES))
        pv = jnp.dot(p.astype(BF16), vaug, preferred_element_type=F32)
        acc_scr[hk] = _lane_tile(alpha, 2) * acc_scr[hk] + pv
        m_scr[hk] = m_new

    def finish(hk):
        acc = acc_scr[hk]
        return acc[:, 0:HEAD_DIM] / acc[:, HEAD_DIM:]

    def chunk_bias(hk, ta, tb):
        return jnp.concatenate([tz_ref[jnp.clip(ta, 0, 2), hk], tz_ref[jnp.clip(tb, 0, 2), hk]], axis=1)

    for hk in range(nkv):
        for g in range(NSA_GROUP):
            h = hk * NSA_GROUP + g
            qg_scr[hk, g * LANES:(g + 1) * LANES, :] = q_ref[0, :, h * HEAD_DIM:(h + 1) * HEAD_DIM]

    for hk in range(nkv):
        qg = qg_scr[hk]

        kc = kcvc_ref[0, 0, hk]
        vc = kcvc_ref[0, 1, hk]
        s = lax.dot_general(qg, kc, NT_DIMS, preferred_element_type=F32) * SCALE + bc_ref[0, hk]
        rc = lax.broadcasted_iota(jnp.int32, (LANES, ncp), 0)
        cc = lax.broadcasted_iota(jnp.int32, (LANES, ncp), 1)
        valid_c = _tile4(jnp.where(q0 + rc - (CMP_STRIDE * cc + CMP_BLOCK - 1) >= 0, 1.0, 0.0)) > 0.5
        s = jnp.where(valid_c, s, NEG)
        p = jnp.where(valid_c, jnp.exp(s - jnp.max(s, axis=1, keepdims=True)), 0.0)
        l = jnp.sum(p, axis=1, keepdims=True)
        p = p / jnp.where(l > 0.0, l, 1.0)
        o_scr[hk] = gate_col(hk, 0) * jnp.dot(p.astype(BF16), vc, preferred_element_type=F32)

        psum = p[0:LANES]
        for g in range(1, NSA_GROUP):
            psum = psum + p[g * LANES:(g + 1) * LANES]
        p_hi = psum.astype(BF16)
        p_lo = (psum - p_hi.astype(F32)).astype(BF16)
        selmt = selmt_ref[...]
        imp = (lax.dot_general(selmt, p_hi, NT_DIMS, preferred_element_type=F32)
               + lax.dot_general(selmt, p_lo, NT_DIMS, preferred_element_type=F32))
        imp = imp[0:n_slc]
        blk = lax.broadcasted_iota(jnp.int32, (n_slc, LANES), 0)
        cur = (q0 + lax.broadcasted_iota(jnp.int32, (n_slc, LANES), 1)) // SEL_BLOCK
        forced = (blk == 0) | (blk == cur) | (blk == cur - 1)
        imp = jnp.where(forced, FORCE, imp)
        imp = jnp.where(blk <= cur, imp, -jnp.inf)
        rank = jnp.zeros((n_slc, LANES), F32)
        for j in range(n_slc):
            row = imp[j:j + 1, :]
            beats = jnp.where(row > imp, 1.0, jnp.where(row == imp, jnp.where(blk > j, 1.0, 0.0), 0.0))
            rank = rank + beats
        sel_t = jnp.where(rank < top_n, jnp.where(imp > -jnp.inf, 1.0, 0.0), 0.0)
        if n_slc < LANES:
            sel_t = jnp.concatenate([sel_t, jnp.zeros((LANES - n_slc, LANES), F32)], axis=0)
        sel_q = lax.dot_general(eye, sel_t.astype(BF16), NT_DIMS, preferred_element_type=F32)
        selq_scr[hk] = jnp.where(ci == LANES - 1, 1.0, sel_q).astype(BF16)

    reset()

    def sel_chunk(c, last):
        k0 = pl.multiple_of(c * KEY_CHUNK, KEY_CHUNK)
        ta = qt - 2 * c
        for hk in range(nkv):
            hc = slice(hk * HEAD_DIM, (hk + 1) * HEAD_DIM)
            k = ks_ref[0, pl.ds(k0, KEY_CHUNK), hc]
            madd = jnp.dot(selq_scr[hk], e_ref[c], preferred_element_type=F32)
            if last:
                madd = jnp.where((k0 + ck) <= (q0 + rk), madd, -MASK_BIG)
            s = (lax.dot_general(qg_scr[hk], k, NT_DIMS, preferred_element_type=F32) * SCALE
                 + chunk_bias(hk, ta, ta - 1) + _tile4(madd))
            online_update(hk, s, vsa_scr[hk, pl.ds(k0, KEY_CHUNK), :])

    def sel_body(c, carry):
        sel_chunk(c, False)
        return carry

    n_chunks = (qt + 2) // 2
    lax.fori_loop(0, n_chunks - 1, sel_body, 0)
    sel_chunk(n_chunks - 1, True)
    for hk in range(nkv):
        o_scr[hk] = o_scr[hk] + gate_col(hk, 1) * finish(hk)

    reset()

    def win_chunk(off):
        kstart = (qt - off) * LANES
        pos = kstart + ck
        d = (q0 + rk) - pos
        ok = (d >= 0) & (d < WINDOW) & (pos >= 0)
        madd = _tile4(jnp.where(ok, 0.0, -MASK_BIG))
        p0 = pl.multiple_of(kstart + WIN_PAD, LANES)
        for hk in range(nkv):
            k = kwp_scr[hk, pl.ds(p0, KEY_CHUNK), :]
            s = (lax.dot_general(qg_scr[hk], k, NT_DIMS, preferred_element_type=F32) * SCALE
                 + chunk_bias(hk, off, off - 1) + madd)
            online_update(hk, s, vwa_scr[hk, pl.ds(p0, KEY_CHUNK), :])

    win_chunk(1)
    for off in (3, 5):
        pl.when(qt >= off - 1)(functools.partial(win_chunk, off))
    for hk in range(nkv):
        o = o_scr[hk] + gate_col(hk, 2) * finish(hk)
        for g in range(NSA_GROUP):
            h = hk * NSA_GROUP + g
            o_ref[0, :, h * HEAD_DIM:(h + 1) * HEAD_DIM] = o[g * LANES:(g + 1) * LANES]


def _nsa_call(o16v, o32v, kcvc, bias_c, tz, selmt, e3, batch, seq, ncp):
    nqt = seq // LANES
    kvw = NSA_KV_WIDTH
    nkv = N_NSA_KV_HEADS
    base = (3 * FOX_WIDTH + NSA_WIDTH) // kvw
    return pl.pallas_call(
        functools.partial(_nsa_kernel, seq=seq, ncp=ncp),
        grid=(batch, nqt),
        in_specs=[pl.BlockSpec((1, LANES, NSA_WIDTH), lambda b, t: (b, t, 3 * FOX_WIDTH // NSA_WIDTH)),
                  pl.BlockSpec((1, seq, kvw), lambda b, t: (b, 0, base)),
                  pl.BlockSpec((1, seq, kvw), lambda b, t: (b, 0, base + 1)),
                  pl.BlockSpec((1, seq, kvw), lambda b, t: (b, 0, base + 2)),
                  pl.BlockSpec((1, seq, kvw), lambda b, t: (b, 0, base + 3)),
                  pl.BlockSpec((1, LANES, LANES), lambda b, t: (b, t, COL_NG // LANES)),
                  pl.BlockSpec((1, 2, N_NSA_KV_HEADS, ncp, HEAD_DIM), lambda b, t: (b, 0, 0, 0, 0)),
                  pl.BlockSpec((1, N_NSA_KV_HEADS, GROUP_ROWS, ncp), lambda b, t: (t, 0, 0, 0)),
                  pl.BlockSpec((3, N_NSA_KV_HEADS, GROUP_ROWS, LANES), lambda b, t: (0, 0, 0, 0)),
                  pl.BlockSpec((LANES, ncp), lambda b, t: (0, 0)),
                  pl.BlockSpec((seq // KEY_CHUNK, LANES, KEY_CHUNK), lambda b, t: (0, 0, 0))],
        out_specs=pl.BlockSpec((1, LANES, NSA_WIDTH), lambda b, t: (b, t, 0)),
        out_shape=jax.ShapeDtypeStruct((batch, seq, NSA_WIDTH), F32),
        scratch_shapes=[pltpu.VMEM((nkv, GROUP_ROWS, HEAD_DIM), BF16),
                        pltpu.VMEM((nkv, GROUP_ROWS, LANES), F32),
                        pltpu.VMEM((nkv, GROUP_ROWS, 2 * HEAD_DIM), F32),
                        pltpu.VMEM((nkv, GROUP_ROWS, HEAD_DIM), F32),
                        pltpu.VMEM((nkv, LANES, LANES), BF16),
                        pltpu.VMEM((nkv, seq, 2 * HEAD_DIM), BF16),
                        pltpu.VMEM((nkv, seq + WIN_PAD, HEAD_DIM), BF16),
                        pltpu.VMEM((nkv, seq + WIN_PAD, 2 * HEAD_DIM), BF16)],
        compiler_params=_params(2),
        name="nsa_attention",
    )(o16v, o16v, o16v, o16v, o16v, o32v, kcvc, bias_c, tz, selmt, e3)


def _out_kernel(of_ref, on_ref, gain_ref, w_ref, x_ref, g_ref, o_ref, *, tm, n_chains):
    rows_per_chain = tm // n_chains
    for c in range(n_chains):
        rows = slice(c * rows_per_chain, (c + 1) * rows_per_chain)
        y = jnp.concatenate([(_rms(of_ref[rows, :]) * gain_ref[:, 0:FOX_WIDTH]).astype(BF16),
                             (_rms(on_ref[rows, :]) * gain_ref[:, FOX_WIDTH:MIX_WIDTH]).astype(BF16)], axis=1)
        acc = jnp.dot(y, w_ref[...], preferred_element_type=F32)
        o_ref[rows, :] = x_ref[rows, :] + g_ref[0] * acc


def _out_call(o_fox, o_nsa, gain, w_out, x2, g1, seq):
    t, d = x2.shape
    tm = min(512, seq)
    rows_per_batch = seq // tm
    return pl.pallas_call(
        functools.partial(_out_kernel, tm=tm, n_chains=2),
        grid=(t // tm,),
        in_specs=[pl.BlockSpec((tm, FOX_WIDTH), lambda i: (i, 0)),
                  pl.BlockSpec((tm, NSA_WIDTH), lambda i: (i, 0)),
                  pl.BlockSpec((1, MIX_WIDTH), lambda i: (0, 0)),
                  pl.BlockSpec((MIX_WIDTH, d), lambda i: (0, 0)),
                  pl.BlockSpec((tm, d), lambda i: (i, 0)),
                  pl.BlockSpec((1, 1, d), lambda i: (i // rows_per_batch, 0, 0))],
        out_specs=pl.BlockSpec((tm, d), lambda i: (i, 0)),
        out_shape=jax.ShapeDtypeStruct((t, d), F32),
        compiler_params=_params(1),
        name="out_proj",
    )(o_fox, o_nsa, gain, w_out, x2, g1)


def _ffn_kernel(x_ref, gain_ref, sc_ref, sh_ref, g_ref, wg_ref, wu_ref, wd_ref, o_ref, h_scr, acc_scr):
    f = pl.program_id(1)

    @pl.when(f == 0)
    def _():
        h = _rms(x_ref[...]) * gain_ref[...]
        h = h * (1.0 + sc_ref[0]) + sh_ref[0]
        h_scr[...] = h.astype(BF16)
        acc_scr[...] = jnp.zeros(acc_scr.shape, F32)

    h = h_scr[...]
    a = jnp.dot(h, wg_ref[...], preferred_element_type=F32)
    u = jnp.dot(h, wu_ref[...], preferred_element_type=F32)
    t = (a * _sigmoid(a)) * u
    acc_scr[...] += jnp.dot(t.astype(BF16), wd_ref[...], preferred_element_type=F32)

    @pl.when(f == pl.num_programs(1) - 1)
    def _():
        o_ref[...] = x_ref[...] + g_ref[0] * acc_scr[...]


def _ffn_call(x1, gain2, sc, sh, g2, wg, wu, wd, seq):
    t, d = x1.shape
    dff = wg.shape[1]
    tm = min(512, seq)
    tf = 512 if dff % 512 == 0 else dff
    rows_per_batch = seq // tm
    return pl.pallas_call(
        _ffn_kernel,
        grid=(t // tm, dff // tf),
        in_specs=[pl.BlockSpec((tm, d), lambda i, f: (i, 0)),
                  pl.BlockSpec((1, d), lambda i, f: (0, 0)),
                  pl.BlockSpec((1, 1, d), lambda i, f: (i // rows_per_batch, 0, 0)),
                  pl.BlockSpec((1, 1, d), lambda i, f: (i // rows_per_batch, 0, 0)),
                  pl.BlockSpec((1, 1, d), lambda i, f: (i // rows_per_batch, 0, 0)),
                  pl.BlockSpec((d, tf), lambda i, f: (0, f)),
                  pl.BlockSpec((d, tf), lambda i, f: (0, f)),
                  pl.BlockSpec((tf, d), lambda i, f: (f, 0))],
        out_specs=pl.BlockSpec((tm, d), lambda i, f: (i, 0)),
        out_shape=jax.ShapeDtypeStruct((t, d), F32),
        scratch_shapes=[pltpu.VMEM((tm, d), BF16), pltpu.VMEM((tm, d), F32)],
        compiler_params=_params(2),
        name="swiglu_ffn",
    )(x1, gain2, sc, sh, g2, wg, wu, wd)


def _selection_matrix_t(ncp, n_slc):
    r, q = SEL_BLOCK // CMP_STRIDE, CMP_BLOCK // CMP_STRIDE
    m = np.zeros((LANES, ncp), np.float32)
    for j in range(n_slc):
        for a in range(r):
            for b in range(q):
                c = r * j + a - b
                if 0 <= c < ncp:
                    m[j, c] += 1.0
    return m


def _block_expander(seq):
    assert seq // SEL_BLOCK < LANES
    keys = np.arange(seq).reshape(seq // KEY_CHUNK, 1, KEY_CHUNK)
    blocks = np.arange(LANES).reshape(1, LANES, 1)
    e = (keys // SEL_BLOCK == blocks).astype(np.float32) * MASK_BIG
    e[:, LANES - 1, :] = -MASK_BIG
    return e


def _permute_w_in(w):
    d = w.shape[0]
    o = 0
    parts = {}
    for name, width in (("fq", FOX_WIDTH), ("fk", FOX_WIDTH), ("fv", FOX_WIDTH), ("ff", N_FOX_HEADS),
                        ("nq", NSA_WIDTH), ("nk", N_BRANCH * NSA_KV_WIDTH), ("nv", N_BRANCH * NSA_KV_WIDTH),
                        ("ng", N_BRANCH * N_NSA_HEADS)):
        parts[name] = w[:, o:o + width]
        o += width
    kvw = NSA_KV_WIDTH
    pad = jnp.zeros((d, W32 - (2 * kvw + N_FOX_HEADS + N_BRANCH * N_NSA_HEADS)), w.dtype)
    return jnp.concatenate([
        parts["fq"], parts["fk"], parts["fv"], parts["nq"],
        parts["nk"][:, kvw:2 * kvw], parts["nk"][:, 2 * kvw:3 * kvw],
        parts["nv"][:, kvw:2 * kvw], parts["nv"][:, 2 * kvw:3 * kvw],
        parts["nk"][:, 0:kvw], parts["nv"][:, 0:kvw], parts["ff"], parts["ng"], pad], axis=1).astype(BF16)


def kernel(x, c, ada_w, ada_b, norm1_gain, norm2_gain, w_in, fox_f_bias, fox_q_gain, fox_k_gain, nsa_q_gain,
           nsa_k_gain, nsa_cmp_pos, nsa_cmp_w1, nsa_cmp_w2, rel_bias, mix_out_gain, w_out, ffn_w_gate, ffn_w_up,
           ffn_w_down):
    batch, seq, d = x.shape
    assert seq % LANES == 0 and seq >= WINDOW and d % LANES == 0
    depth = ada_w.shape[0]
    nqt = seq // LANES
    ncp = -(-(seq // CMP_STRIDE) // LANES) * LANES
    tq_fox = min(256, seq)

    selmt = jnp.asarray(_selection_matrix_t(ncp, seq // SEL_BLOCK), BF16)
    e3 = jnp.asarray(_block_expander(seq), BF16)
    rb_flat = rel_bias.reshape(-1)
    bias_c = _bias_call(rb_flat, nqt, ncp, CMP_STRIDE, CMP_BLOCK - 1, "t5_bias_compressed")
    tz = _bias_call(rb_flat, 3, LANES, 1, 0, "t5_bias_toeplitz")

    ones_h = jnp.ones((HEAD_DIM,), F32)
    c_pad = jnp.pad(c, ((0, 8 - batch % 8 if batch % 8 else 0), (0, 0)))
    x2 = x.reshape(batch * seq, d)
    for layer in range(depth):
        mod = _ada_call(c_pad, ada_w[layer], ada_b[layer][None, :])[:batch]
        sh1, sc1, g1, sh2, sc2, g2 = [mod[:, i * d:(i + 1) * d][:, None, :] for i in range(N_MOD)]

        kg = nsa_k_gain[layer]
        col_gain = jnp.concatenate([
            jnp.tile(fox_q_gain[layer], N_FOX_HEADS), jnp.tile(fox_k_gain[layer], N_FOX_HEADS),
            jnp.tile(ones_h, N_FOX_HEADS), jnp.tile(nsa_q_gain[layer], N_NSA_HEADS),
            jnp.tile(kg[1], N_NSA_KV_HEADS), jnp.tile(kg[2], N_NSA_KV_HEADS),
            jnp.tile(ones_h, 2 * N_NSA_KV_HEADS)])[None, :]
        col_flag = jnp.concatenate([
            jnp.ones((2 * FOX_WIDTH,), F32), jnp.zeros((FOX_WIDTH,), F32), jnp.ones((NSA_WIDTH,), F32),
            jnp.ones((2 * NSA_KV_WIDTH,), F32), jnp.zeros((2 * NSA_KV_WIDTH,), F32)])[None, :]
        o16, o32 = _proj_call(x2, sc1, sh1, norm1_gain[layer][None, :], _permute_w_in(w_in[layer]),
                              col_gain, col_flag, seq)
        o16v = o16.reshape(batch, seq, W16)
        o32v = o32.reshape(batch, seq, W32)

        fb_pad = jnp.pad(fox_f_bias[layer], (0, LANES - N_FOX_HEADS))[None, :]
        cum = _cum_call(o32, fb_pad, batch, seq)
        cumt = jnp.transpose(cum[:, :, :N_FOX_HEADS], (0, 2, 1)).reshape(batch, N_FOX_HEADS, seq // tq_fox, tq_fox)
        o_fox = _fox_call(o16v, cum, cumt, batch, seq, tq_fox)

        w1 = nsa_cmp_w1[layer].reshape(2, CMP_BLOCK, HEAD_DIM, HEAD_DIM).astype(BF16)
        kcvc = _cmp_call(o32, nsa_cmp_pos[layer], w1, nsa_cmp_w2[layer].astype(BF16), kg[0][None, :],
                         batch, seq, ncp)
        o_nsa = _nsa_call(o16v, o32v, kcvc, bias_c, tz, selmt, e3, batch, seq, ncp)

        x1 = _out_call(o_fox.reshape(batch * seq, FOX_WIDTH), o_nsa.reshape(batch * seq, NSA_WIDTH),
                       mix_out_gain[layer][None, :], w_out[layer].astype(BF16), x2, g1, seq)
        x2 = _ffn_call(x1, norm2_gain[layer][None, :], sc2, sh2, g2, ffn_w_gate[layer].astype(BF16),
                       ffn_w_up[layer].astype(BF16), ffn_w_down[layer].astype(BF16), seq)
    return x2.reshape(batch, seq, d)
```

```python
import functools
import math

import numpy as np
import jax
import jax.numpy as jnp
from jax import lax
from jax.experimental import pallas as pl
from jax.experimental.pallas import tpu as pltpu

HEAD_DIM = 128
N_FOX_HEADS = 8
N_NSA_HEADS = 8
N_NSA_KV_HEADS = 2
NSA_GROUP = N_NSA_HEADS // N_NSA_KV_HEADS
FOX_WIDTH = N_FOX_HEADS * HEAD_DIM
NSA_WIDTH = N_NSA_HEADS * HEAD_DIM
NSA_KV_WIDTH = N_NSA_KV_HEADS * HEAD_DIM
MIX_WIDTH = FOX_WIDTH + NSA_WIDTH
N_BRANCH = 3
CMP_BLOCK = 32
CMP_STRIDE = 16
SEL_BLOCK = 64
N_SEL = 8
WINDOW = 512
N_BUCKETS = 32
MAX_DISTANCE = 128
N_MOD = 6
SCALE = HEAD_DIM ** -0.5
LOG2E = math.log2(math.e)
QSCALE = SCALE * LOG2E
EPS = 1e-6
NEG = -1e30
FORCE = 1e6

LANES = 128
GROUP_ROWS = NSA_GROUP * LANES
VMEM_LIMIT = 56 * 1024 * 1024
MXU_COLS = 256
KEY_CHUNK = MXU_COLS
WIN_PAD = WINDOW + LANES
MASK_BIG = 2.0 ** 100
SEL, WIN = 0, 1

W16 = 3 * FOX_WIDTH + NSA_WIDTH + 4 * NSA_KV_WIDTH
W32 = 1024
COL_FF = 2 * NSA_KV_WIDTH
COL_NG = COL_FF + N_FOX_HEADS

F32 = jnp.float32
BF16 = jnp.bfloat16
NT_DIMS = (((1,), (1,)), ((), ()))


def _params(n_axes):
    return pltpu.CompilerParams(dimension_semantics=("arbitrary",) * n_axes,
                                vmem_limit_bytes=VMEM_LIMIT)


def _sigmoid(x):
    return 1.0 / (1.0 + jnp.exp(-x))


def _lane_tile(a, n):
    return jnp.concatenate([a] * n, axis=1)


def _rms(x):
    return x * lax.rsqrt(jnp.mean(x * x, axis=-1, keepdims=True) + EPS)


def _ada_kernel(c_ref, w_ref, b_ref, o_ref):
    c = c_ref[...]
    s = (c * _sigmoid(c)).astype(BF16)
    o_ref[...] = jnp.dot(s, w_ref[...].astype(BF16), preferred_element_type=F32) + b_ref[...]


def _ada_call(c_pad, w, b):
    rows, d = c_pad.shape
    n = w.shape[1]
    tn = next(t for t in (1024, 768, 512, 384, 256, 128) if n % t == 0)
    return pl.pallas_call(
        _ada_kernel,
        grid=(n // tn,),
        in_specs=[pl.BlockSpec((rows, d), lambda j: (0, 0)),
                  pl.BlockSpec((d, tn), lambda j: (0, j)),
                  pl.BlockSpec((1, tn), lambda j: (0, j))],
        out_specs=pl.BlockSpec((rows, tn), lambda j: (0, j)),
        out_shape=jax.ShapeDtypeStruct((rows, n), F32),
        compiler_params=_params(1),
        name="adaln",
    )(c_pad, w, b)


def _proj_kernel(x_ref, sc_ref, sh_ref, g_ref, w_ref, gain_ref, flag_ref, o16_ref, o32_ref, h_scr, *, n16, tn):
    j = pl.program_id(1)

    @pl.when(j == 0)
    def _():
        h = _rms(x_ref[...]) * g_ref[...]
        h = h * (1.0 + sc_ref[0]) + sh_ref[0]
        h_scr[...] = h.astype(BF16)

    @pl.when(j < n16)
    def _():
        h = h_scr[...]
        for c in range(tn // MXU_COLS):
            acc = jnp.dot(h, w_ref[:, c * MXU_COLS:(c + 1) * MXU_COLS], preferred_element_type=F32)
            for g in range(MXU_COLS // LANES):
                cols = slice(c * MXU_COLS + g * LANES, c * MXU_COLS + (g + 1) * LANES)
                a = acc[:, g * LANES:(g + 1) * LANES]
                r = lax.rsqrt(jnp.mean(a * a, axis=-1, keepdims=True) + EPS)
                scale = jnp.where(flag_ref[:, cols] > 0.5, r, 1.0)
                o16_ref[:, cols] = (a * scale * gain_ref[:, cols]).astype(BF16)

    @pl.when(j >= n16)
    def _():
        o32_ref[...] = jnp.dot(h_scr[...], w_ref[...], preferred_element_type=F32)


def _proj_call(x2, sc, sh, gain1, w_perm, col_gain, col_flag, seq):
    t, d = x2.shape
    tm = min(512, seq)
    tn = W32
    n16 = W16 // tn
    rows_per_batch = seq // tm
    return pl.pallas_call(
        functools.partial(_proj_kernel, n16=n16, tn=tn),
        grid=(t // tm, n16 + 1),
        in_specs=[pl.BlockSpec((tm, d), lambda i, j: (i, 0)),
                  pl.BlockSpec((1, 1, d), lambda i, j: (i // rows_per_batch, 0, 0)),
                  pl.BlockSpec((1, 1, d), lambda i, j: (i // rows_per_batch, 0, 0)),
                  pl.BlockSpec((1, d), lambda i, j: (0, 0)),
                  pl.BlockSpec((d, tn), lambda i, j: (0, j)),
                  pl.BlockSpec((1, tn), lambda i, j: (0, jnp.minimum(j, n16 - 1))),
                  pl.BlockSpec((1, tn), lambda i, j: (0, jnp.minimum(j, n16 - 1)))],
        out_specs=[pl.BlockSpec((tm, tn), lambda i, j: (i, jnp.minimum(j, n16 - 1))),
                   pl.BlockSpec((tm, tn), lambda i, j: (i, 0))],
        out_shape=[jax.ShapeDtypeStruct((t, W16), BF16),
                   jax.ShapeDtypeStruct((t, W32), F32)],
        scratch_shapes=[pltpu.VMEM((tm, d), BF16)],
        compiler_params=_params(2),
        name="in_proj",
    )(x2, sc, sh, gain1, w_perm, col_gain, col_flag)


def _cum_kernel(ff_ref, fb_ref, o_ref, *, seq):
    ri = lax.broadcasted_iota(jnp.int32, (LANES, LANES), 0)
    ci = lax.broadcasted_iota(jnp.int32, (LANES, LANES), 1)
    tri = jnp.where(ri >= ci, 1.0, 0.0).astype(BF16)
    carry = jnp.zeros((1, LANES), F32)
    for blk in range(seq // LANES):
        rows = slice(blk * LANES, (blk + 1) * LANES)
        x = ff_ref[0, rows, :] + fb_ref[...]
        lf = jnp.minimum(x, 0.0) - jnp.log(1.0 + jnp.exp(-jnp.abs(x)))
        hi = lf.astype(BF16)
        r1 = lf - hi.astype(F32)
        mid = r1.astype(BF16)
        lo = (r1 - mid.astype(F32)).astype(BF16)
        c = (jnp.dot(tri, hi, preferred_element_type=F32)
             + jnp.dot(tri, mid, preferred_element_type=F32)
             + jnp.dot(tri, lo, preferred_element_type=F32)) + carry
        o_ref[0, rows, :] = c
        carry = c[LANES - 1:LANES, :]


def _cum_call(o32, fb_pad, batch, seq):
    o32v = o32.reshape(batch, seq, W32)
    return pl.pallas_call(
        functools.partial(_cum_kernel, seq=seq),
        grid=(batch,),
        in_specs=[pl.BlockSpec((1, seq, LANES), lambda b: (b, 0, COL_FF // LANES)),
                  pl.BlockSpec((1, LANES), lambda b: (0, 0))],
        out_specs=pl.BlockSpec((1, seq, LANES), lambda b: (b, 0, 0)),
        out_shape=jax.ShapeDtypeStruct((batch, seq, LANES), F32),
        compiler_params=_params(1),
        name="fox_cumsum",
    )(o32v, fb_pad)


def _split3(c):
    hi = c.astype(BF16).astype(F32)
    r1 = c - hi
    mid = r1.astype(BF16).astype(F32)
    return hi, mid, (r1 - mid).astype(BF16).astype(F32)


def _fox_kernel(q_ref, k_ref, v_ref, cum_ref, o_ref, m_scr, acc_scr, s_scr, kaug_scr, vaug_scr, *, seq, tq, nh):
    hp = pl.program_id(1)
    lane = lax.broadcasted_iota(jnp.int32, (tq, LANES), 1)
    lane_s = lax.broadcasted_iota(jnp.int32, (seq, LANES), 1)
    causal = lax.broadcasted_iota(jnp.int32, (tq, tq), 1) <= lax.broadcasted_iota(jnp.int32, (tq, tq), 0)
    for j in range(nh):
        ck = jnp.sum(jnp.where(lane_s == hp * nh + j, cum_ref[0], 0.0), axis=1, keepdims=True) * LOG2E
        hi, mid, lo = _split3(ck)
        tail = jnp.where(lane_s < 3, 1.0, jnp.where(lane_s == 3, -hi, jnp.where(lane_s == 4, -mid,
                         jnp.where(lane_s == 5, -lo, 0.0))))
        for c in range(seq // tq):
            rows = slice(c * tq, (c + 1) * tq)
            kaug_scr[j, c, 0:HEAD_DIM, :] = k_ref[0, rows, j * HEAD_DIM:(j + 1) * HEAD_DIM].T
            kaug_scr[j, c, HEAD_DIM:, :] = tail[rows].T.astype(BF16)
        vaug_scr[j, :, 0:HEAD_DIM] = v_ref[0, :, j * HEAD_DIM:(j + 1) * HEAD_DIM]
        vaug_scr[j, :, HEAD_DIM:] = jnp.ones((seq, HEAD_DIM), BF16)

    def q_body(qi, carry):
        q0 = pl.multiple_of(qi * tq, tq)
        cum_t = cum_ref[0, pl.ds(q0, tq), :]
        qs = []
        for j in range(nh):
            cq = jnp.sum(jnp.where(lane == hp * nh + j, cum_t, 0.0), axis=1, keepdims=True) * LOG2E
            hi, mid, lo = _split3(cq)
            tail = jnp.where(lane == 0, hi, jnp.where(lane == 1, mid, jnp.where(lane == 2, lo,
                             jnp.where(lane < 6, 1.0, 0.0))))
            qs.append(jnp.concatenate([q_ref[0, pl.ds(q0, tq), j * HEAD_DIM:(j + 1) * HEAD_DIM],
                                       tail.astype(BF16)], axis=1))
            m_scr[j] = jnp.full((tq, LANES), NEG, F32)
            acc_scr[j] = jnp.zeros((tq, 2 * HEAD_DIM), F32)

        def scores(j, ki):
            return jnp.dot(qs[j], kaug_scr[j, ki], preferred_element_type=F32)

        def tile(ki, diagonal):
            k0 = pl.multiple_of(ki * tq, tq)
            for j in range(nh):
                s = s_scr[j]
                if diagonal:
                    s = jnp.where(causal, s, NEG)
                else:
                    s_scr[j] = scores(j, ki + 1)
                m_prev = m_scr[j]
                m_new = jnp.maximum(m_prev, jnp.max(s, axis=1, keepdims=True))
                alpha = jnp.exp2(m_prev - m_new)
                p = jnp.exp2(s - _lane_tile(m_new, tq // LANES))
                pv = jnp.dot(p.astype(BF16), vaug_scr[j, pl.ds(k0, tq), :], preferred_element_type=F32)
                acc_scr[j] = _lane_tile(alpha, 2) * acc_scr[j] + pv
                m_scr[j] = m_new

        def k_body(ki, c2):
            tile(ki, False)
            return c2

        for j in range(nh):
            s_scr[j] = scores(j, 0)
        lax.fori_loop(0, qi, k_body, 0)
        tile(qi, True)
        for j in range(nh):
            acc = acc_scr[j]
            o_ref[0, pl.ds(q0, tq), j * HEAD_DIM:(j + 1) * HEAD_DIM] = acc[:, 0:HEAD_DIM] / acc[:, HEAD_DIM:]
        return carry

    lax.fori_loop(0, seq // tq, q_body, 0)


def _fox_call(o16v, cum, batch, seq, tq):
    nh = 4
    groups = N_FOX_HEADS // nh
    w = nh * HEAD_DIM
    return pl.pallas_call(
        functools.partial(_fox_kernel, seq=seq, tq=tq, nh=nh),
        grid=(batch, groups),
        in_specs=[pl.BlockSpec((1, seq, w), lambda b, h: (b, 0, h)),
                  pl.BlockSpec((1, seq, w), lambda b, h: (b, 0, groups + h)),
                  pl.BlockSpec((1, seq, w), lambda b, h: (b, 0, 2 * groups + h)),
                  pl.BlockSpec((1, seq, LANES), lambda b, h: (b, 0, 0))],
        out_specs=pl.BlockSpec((1, seq, w), lambda b, h: (b, 0, h)),
        out_shape=jax.ShapeDtypeStruct((batch, seq, FOX_WIDTH), F32),
        scratch_shapes=[pltpu.VMEM((nh, tq, LANES), F32),
                        pltpu.VMEM((nh, tq, 2 * HEAD_DIM), F32),
                        pltpu.VMEM((nh, tq, tq), F32),
                        pltpu.VMEM((nh, seq // tq, 2 * HEAD_DIM, tq), BF16),
                        pltpu.VMEM((nh, seq, 2 * HEAD_DIM), BF16)],
        compiler_params=_params(2),
        name="fox_attention",
    )(o16v, o16v, o16v, cum)


def _cmp_kernel(x_ref, pos_ref, w1_ref, w2_ref, gain_ref, o_ref, xs_scr, *, seq, ncp):
    kv = pl.program_id(1)
    rows = xs_scr.shape[0]
    xs_scr[0:seq, :] = x_ref[0]
    xs_scr[seq:rows, :] = jnp.zeros((rows - seq, LANES), F32)
    acc = jnp.zeros((ncp, HEAD_DIM), F32)
    for l in range(CMP_BLOCK):
        xl = xs_scr[pl.ds(l, ncp, stride=CMP_STRIDE), :] + pos_ref[0, l:l + 1, :]
        acc = acc + jnp.dot(xl.astype(BF16), w1_ref[0, l], preferred_element_type=F32)
    hmid = acc * _sigmoid(acc)
    y = jnp.dot(hmid.astype(BF16), w2_ref[0], preferred_element_type=F32)
    yk = _rms(y) * gain_ref[...]
    o_ref[0, 0, 0] = jnp.where(kv == 0, yk, y).astype(BF16)


def _cmp_call(o32, pos, w1, w2, gain, batch, seq, ncp):
    o32v = o32.reshape(batch, seq, W32)
    hkv = N_NSA_KV_HEADS
    return pl.pallas_call(
        functools.partial(_cmp_kernel, seq=seq, ncp=ncp),
        grid=(batch, 2, hkv),
        in_specs=[pl.BlockSpec((1, seq, HEAD_DIM), lambda b, kv, h: (b, 0, kv * hkv + h)),
                  pl.BlockSpec((1, CMP_BLOCK, HEAD_DIM), lambda b, kv, h: (kv, 0, 0)),
                  pl.BlockSpec((1, CMP_BLOCK, HEAD_DIM, HEAD_DIM), lambda b, kv, h: (kv, 0, 0, 0)),
                  pl.BlockSpec((1, HEAD_DIM, HEAD_DIM), lambda b, kv, h: (kv, 0, 0)),
                  pl.BlockSpec((1, HEAD_DIM), lambda b, kv, h: (0, 0))],
        out_specs=pl.BlockSpec((1, 1, 1, ncp, HEAD_DIM), lambda b, kv, h: (b, kv, h, 0, 0)),
        out_shape=jax.ShapeDtypeStruct((batch, 2, hkv, ncp, HEAD_DIM), BF16),
        scratch_shapes=[pltpu.VMEM((CMP_STRIDE * ncp + CMP_BLOCK, LANES), F32)],
        compiler_params=_params(3),
        name="nsa_compress",
    )(o32v, pos, w1, w2, gain)


def _bias_kernel(rb_ref, o_ref, *, width, key_stride, key_offset):
    v = pl.program_id(0)
    i = lax.broadcasted_iota(jnp.int32, (LANES, width), 0)
    j = lax.broadcasted_iota(jnp.int32, (LANES, width), 1)
    d = v * LANES + i - (key_stride * j + key_offset)
    n = jnp.maximum(d, 0)
    max_exact = N_BUCKETS // 2
    nf = jnp.maximum(n, 1).astype(F32)
    large = max_exact + (jnp.log(nf / max_exact) / math.log(MAX_DISTANCE / max_exact)
                         * (N_BUCKETS - max_exact)).astype(jnp.int32)
    large = jnp.minimum(large, N_BUCKETS - 1)
    bkt = jnp.where(n < max_exact, n, large)
    vals = [jnp.zeros((LANES, width), F32) for _ in range(N_NSA_HEADS)]
    for bk in range(N_BUCKETS):
        hit = bkt == bk
        for h in range(N_NSA_HEADS):
            vals[h] = jnp.where(hit, rb_ref[bk * N_NSA_HEADS + h] * LOG2E, vals[h])
    for h in range(N_NSA_HEADS):
        g = h % NSA_GROUP
        o_ref[0, h // NSA_GROUP, g * LANES:(g + 1) * LANES, :] = vals[h]


def _bias_call(rb_flat, n_variants, width, key_stride, key_offset, name):
    return pl.pallas_call(
        functools.partial(_bias_kernel, width=width, key_stride=key_stride, key_offset=key_offset),
        grid=(n_variants,),
        in_specs=[pl.BlockSpec(memory_space=pltpu.SMEM)],
        out_specs=pl.BlockSpec((1, N_NSA_KV_HEADS, GROUP_ROWS, width), lambda v: (v, 0, 0, 0)),
        out_shape=jax.ShapeDtypeStruct((n_variants, N_NSA_KV_HEADS, GROUP_ROWS, width), F32),
        compiler_params=_params(1),
        name=name,
    )(rb_flat)


def _tile4(a):
    return jnp.concatenate([a] * NSA_GROUP, axis=0)


def _nsa_kernel(q_ref, ks_ref, kw_ref, vs_ref, vw_ref, g_ref, kcvc_ref, bc_ref, tz_ref, selmt_ref,
                o_ref, qg_scr, m_scr, acc_scr, o_scr, s_scr, ksa_scr, vsa_scr, kwp_scr, vwa_scr, *, seq, ncp):
    qt = pl.program_id(1)
    q0 = qt * LANES
    n_slc = seq // SEL_BLOCK
    top_n = min(N_SEL, n_slc)
    nkv = N_NSA_KV_HEADS
    ri = lax.broadcasted_iota(jnp.int32, (LANES, LANES), 0)
    ci = lax.broadcasted_iota(jnp.int32, (LANES, LANES), 1)
    rk = lax.broadcasted_iota(jnp.int32, (LANES, KEY_CHUNK), 0)
    ck = lax.broadcasted_iota(jnp.int32, (LANES, KEY_CHUNK), 1)
    eye = jnp.where(ri == ci, 1.0, 0.0).astype(BF16)
    gates = _sigmoid(g_ref[0])

    def gate_col(hk, br):
        cols = []
        for g in range(NSA_GROUP):
            c = COL_NG % LANES + (hk * NSA_GROUP + g) * N_BRANCH + br
            cols.append(gates[:, c:c + 1])
        return jnp.concatenate(cols, axis=0)

    @pl.when(qt == 0)
    def _():
        ones = jnp.ones((seq, HEAD_DIM), BF16)
        blk_row = lax.broadcasted_iota(jnp.int32, (LANES, KEY_CHUNK), 0)
        for hk in range(nkv):
            hc = slice(hk * HEAD_DIM, (hk + 1) * HEAD_DIM)
            for c in range(seq // KEY_CHUNK):
                rows = slice(c * KEY_CHUNK, (c + 1) * KEY_CHUNK)
                ksa_scr[hk, c, 0:HEAD_DIM, :] = ks_ref[0, rows, hc].T
                ksa_scr[hk, c, HEAD_DIM:, :] = jnp.where((c * KEY_CHUNK + ck) // SEL_BLOCK == blk_row,
                                                         -MASK_BIG, 0.0).astype(BF16)
            for t in range(WIN_PAD // LANES):
                kwp_scr[hk, t] = jnp.zeros((HEAD_DIM, LANES), BF16)
            for t in range(seq // LANES):
                kwp_scr[hk, WIN_PAD // LANES + t] = kw_ref[0, t * LANES:(t + 1) * LANES, hc].T
            vsa_scr[hk, :, 0:HEAD_DIM] = vs_ref[0, :, hc]
            vsa_scr[hk, :, HEAD_DIM:] = ones
            vwa_scr[hk, 0:WIN_PAD, :] = jnp.zeros((WIN_PAD, 2 * HEAD_DIM), BF16)
            vwa_scr[hk, WIN_PAD:, 0:HEAD_DIM] = vw_ref[0, :, hc]
            vwa_scr[hk, WIN_PAD:, HEAD_DIM:] = ones

    def reset(br):
        m_scr[br] = jnp.full(m_scr.shape[1:], NEG, F32)
        acc_scr[br] = jnp.zeros(acc_scr.shape[1:], F32)

    def online_update(br, hk, s, vaug):
        m_prev = m_scr[br, hk]
        m_new = jnp.maximum(m_prev, jnp.max(s, axis=1, keepdims=True))
        alpha = jnp.exp2(m_prev - m_new)
        p = jnp.exp2(s - _lane_tile(m_new, KEY_CHUNK // LANES))
        pv = jnp.dot(p.astype(BF16), vaug, preferred_element_type=F32)
        acc_scr[br, hk] = _lane_tile(alpha, 2) * acc_scr[br, hk] + pv
        m_scr[br, hk] = m_new

    def finish(br, hk):
        acc = acc_scr[br, hk]
        return acc[:, 0:HEAD_DIM] / acc[:, HEAD_DIM:]

    def chunk_bias(hk, ta, tb):
        return jnp.concatenate([tz_ref[jnp.clip(ta, 0, 2), hk], tz_ref[jnp.clip(tb, 0, 2), hk]], axis=1)

    for hk in range(nkv):
        for g in range(NSA_GROUP):
            h = hk * NSA_GROUP + g
            qg_scr[hk, g * LANES:(g + 1) * LANES, 0:HEAD_DIM] = q_ref[0, :, h * HEAD_DIM:(h + 1) * HEAD_DIM]

    for hk in range(nkv):
        qg = qg_scr[hk, :, 0:HEAD_DIM]

        kc = kcvc_ref[0, 0, hk]
        vc = kcvc_ref[0, 1, hk]
        s = lax.dot_general(qg, kc, NT_DIMS, preferred_element_type=F32) + bc_ref[0, hk]
        rc = lax.broadcasted_iota(jnp.int32, (LANES, ncp), 0)
        cc = lax.broadcasted_iota(jnp.int32, (LANES, ncp), 1)
        valid_c = _tile4(jnp.where(q0 + rc - (CMP_STRIDE * cc + CMP_BLOCK - 1) >= 0, 1.0, 0.0)) > 0.5
        s = jnp.where(valid_c, s, NEG)
        p = jnp.where(valid_c, jnp.exp2(s - jnp.max(s, axis=1, keepdims=True)), 0.0)
        l = jnp.sum(p, axis=1, keepdims=True)
        p = p / jnp.where(l > 0.0, l, 1.0)
        o_scr[hk] = gate_col(hk, 0) * jnp.dot(p.astype(BF16), vc, preferred_element_type=F32)

        psum = p[0:LANES]
        for g in range(1, NSA_GROUP):
            psum = psum + p[g * LANES:(g + 1) * LANES]
        p_hi = psum.astype(BF16)
        p_lo = (psum - p_hi.astype(F32)).astype(BF16)
        selmt = selmt_ref[...]
        imp = (lax.dot_general(selmt, p_hi, NT_DIMS, preferred_element_type=F32)
               + lax.dot_general(selmt, p_lo, NT_DIMS, preferred_element_type=F32))
        imp = imp[0:n_slc]
        blk = lax.broadcasted_iota(jnp.int32, (n_slc, LANES), 0)
        cur = (q0 + lax.broadcasted_iota(jnp.int32, (n_slc, LANES), 1)) // SEL_BLOCK
        forced = (blk == 0) | (blk == cur) | (blk == cur - 1)
        imp = jnp.where(forced, FORCE, imp)
        imp = jnp.where(blk <= cur, imp, -jnp.inf)
        rank = jnp.zeros((n_slc, LANES), F32)
        for j in range(n_slc):
            row = imp[j:j + 1, :]
            beats = jnp.where(row > imp, 1.0, jnp.where(row == imp, jnp.where(blk > j, 1.0, 0.0), 0.0))
            rank = rank + beats
        sel_t = jnp.where(rank < top_n, jnp.where(imp > -jnp.inf, 1.0, 0.0), 0.0)
        if n_slc < LANES:
            sel_t = jnp.concatenate([sel_t, jnp.zeros((LANES - n_slc, LANES), F32)], axis=0)
        sel_q = lax.dot_general(eye, sel_t.astype(BF16), NT_DIMS, preferred_element_type=F32)
        not_sel = jnp.where(ci < n_slc, 1.0 - sel_q, 0.0).astype(BF16)
        for g in range(NSA_GROUP):
            qg_scr[hk, g * LANES:(g + 1) * LANES, HEAD_DIM:] = not_sel

    reset(WIN)
    for off in (1, 3, 5):
        kstart = (qt - off) * LANES
        pos = kstart + ck
        d = (q0 + rk) - pos
        ok = (d >= 0) & (d < WINDOW) & (pos >= 0)
        madd = _tile4(jnp.where(ok, 0.0, -MASK_BIG))
        t0 = qt - off + WIN_PAD // LANES
        p0 = pl.multiple_of(kstart + WIN_PAD, LANES)
        for hk in range(nkv):
            k_t = jnp.concatenate([kwp_scr[hk, t0], kwp_scr[hk, t0 + 1]], axis=1)
            s = (jnp.dot(qg_scr[hk, :, 0:HEAD_DIM], k_t, preferred_element_type=F32)
                 + chunk_bias(hk, off, off - 1) + madd)
            online_update(WIN, hk, s, vwa_scr[hk, pl.ds(p0, KEY_CHUNK), :])

    reset(SEL)

    def sel_scores(hk, c):
        ta = qt - 2 * c
        return jnp.dot(qg_scr[hk], ksa_scr[hk, c], preferred_element_type=F32) + chunk_bias(hk, ta, ta - 1)

    def sel_chunk(c, last):
        k0 = pl.multiple_of(c * KEY_CHUNK, KEY_CHUNK)
        for hk in range(nkv):
            s = s_scr[hk]
            if last:
                s = jnp.where(_tile4(jnp.where((k0 + ck) <= (q0 + rk), 1.0, 0.0)) > 0.5, s, -MASK_BIG)
            else:
                s_scr[hk] = sel_scores(hk, c + 1)
            online_update(SEL, hk, s, vsa_scr[hk, pl.ds(k0, KEY_CHUNK), :])

    def sel_body(c, carry):
        sel_chunk(c, False)
        return carry

    n_chunks = (qt + 2) // 2
    for hk in range(nkv):
        s_scr[hk] = sel_scores(hk, 0)
    lax.fori_loop(0, n_chunks - 1, sel_body, 0)
    sel_chunk(n_chunks - 1, True)
    for hk in range(nkv):
        o = (o_scr[hk] + gate_col(hk, 1) * finish(SEL, hk)) + gate_col(hk, 2) * finish(WIN, hk)
        for g in range(NSA_GROUP):
            h = hk * NSA_GROUP + g
            o_ref[0, :, h * HEAD_DIM:(h + 1) * HEAD_DIM] = o[g * LANES:(g + 1) * LANES]


def _nsa_call(o16v, o32v, kcvc, bias_c, tz, selmt, batch, seq, ncp):
    nqt = seq // LANES
    kvw = NSA_KV_WIDTH
    nkv = N_NSA_KV_HEADS
    base = (3 * FOX_WIDTH + NSA_WIDTH) // kvw
    return pl.pallas_call(
        functools.partial(_nsa_kernel, seq=seq, ncp=ncp),
        grid=(batch, nqt),
        in_specs=[pl.BlockSpec((1, LANES, NSA_WIDTH), lambda b, t: (b, t, 3 * FOX_WIDTH // NSA_WIDTH)),
                  pl.BlockSpec((1, seq, kvw), lambda b, t: (b, 0, base)),
                  pl.BlockSpec((1, seq, kvw), lambda b, t: (b, 0, base + 1)),
                  pl.BlockSpec((1, seq, kvw), lambda b, t: (b, 0, base + 2)),
                  pl.BlockSpec((1, seq, kvw), lambda b, t: (b, 0, base + 3)),
                  pl.BlockSpec((1, LANES, LANES), lambda b, t: (b, t, COL_NG // LANES)),
                  pl.BlockSpec((1, 2, N_NSA_KV_HEADS, ncp, HEAD_DIM), lambda b, t: (b, 0, 0, 0, 0)),
                  pl.BlockSpec((1, N_NSA_KV_HEADS, GROUP_ROWS, ncp), lambda b, t: (t, 0, 0, 0)),
                  pl.BlockSpec((3, N_NSA_KV_HEADS, GROUP_ROWS, LANES), lambda b, t: (0, 0, 0, 0)),
                  pl.BlockSpec((LANES, ncp), lambda b, t: (0, 0))],
        out_specs=pl.BlockSpec((1, LANES, NSA_WIDTH), lambda b, t: (b, t, 0)),
        out_shape=jax.ShapeDtypeStruct((batch, seq, NSA_WIDTH), F32),
        scratch_shapes=[pltpu.VMEM((nkv, GROUP_ROWS, 2 * HEAD_DIM), BF16),
                        pltpu.VMEM((2, nkv, GROUP_ROWS, LANES), F32),
                        pltpu.VMEM((2, nkv, GROUP_ROWS, 2 * HEAD_DIM), F32),
                        pltpu.VMEM((nkv, GROUP_ROWS, HEAD_DIM), F32),
                        pltpu.VMEM((nkv, GROUP_ROWS, KEY_CHUNK), F32),
                        pltpu.VMEM((nkv, seq // KEY_CHUNK, 2 * HEAD_DIM, KEY_CHUNK), BF16),
                        pltpu.VMEM((nkv, seq, 2 * HEAD_DIM), BF16),
                        pltpu.VMEM((nkv, (seq + WIN_PAD) // LANES, HEAD_DIM, LANES), BF16),
                        pltpu.VMEM((nkv, seq + WIN_PAD, 2 * HEAD_DIM), BF16)],
        compiler_params=_params(2),
        name="nsa_attention",
    )(o16v, o16v, o16v, o16v, o16v, o32v, kcvc, bias_c, tz, selmt)


def _out_kernel(of_ref, on_ref, gain_ref, w_ref, x_ref, g_ref, o_ref, *, tm, n_chains):
    rows_per_chain = tm // n_chains
    for c in range(n_chains):
        rows = slice(c * rows_per_chain, (c + 1) * rows_per_chain)
        y = jnp.concatenate([(_rms(of_ref[rows, :]) * gain_ref[:, 0:FOX_WIDTH]).astype(BF16),
                             (_rms(on_ref[rows, :]) * gain_ref[:, FOX_WIDTH:MIX_WIDTH]).astype(BF16)], axis=1)
        acc = jnp.dot(y, w_ref[...], preferred_element_type=F32)
        o_ref[rows, :] = x_ref[rows, :] + g_ref[0] * acc


def _out_call(o_fox, o_nsa, gain, w_out, x2, g1, seq):
    t, d = x2.shape
    tm = min(512, seq)
    rows_per_batch = seq // tm
    return pl.pallas_call(
        functools.partial(_out_kernel, tm=tm, n_chains=2),
        grid=(t // tm,),
        in_specs=[pl.BlockSpec((tm, FOX_WIDTH), lambda i: (i, 0)),
                  pl.BlockSpec((tm, NSA_WIDTH), lambda i: (i, 0)),
                  pl.BlockSpec((1, MIX_WIDTH), lambda i: (0, 0)),
                  pl.BlockSpec((MIX_WIDTH, d), lambda i: (0, 0)),
                  pl.BlockSpec((tm, d), lambda i: (i, 0)),
                  pl.BlockSpec((1, 1, d), lambda i: (i // rows_per_batch, 0, 0))],
        out_specs=pl.BlockSpec((tm, d), lambda i: (i, 0)),
        out_shape=jax.ShapeDtypeStruct((t, d), F32),
        compiler_params=_params(1),
        name="out_proj",
    )(o_fox, o_nsa, gain, w_out, x2, g1)


def _ffn_kernel(x_ref, gain_ref, sc_ref, sh_ref, g_ref, wg_ref, wu_ref, wd_ref, o_ref, h_scr, acc_scr):
    f = pl.program_id(1)

    @pl.when(f == 0)
    def _():
        h = _rms(x_ref[...]) * gain_ref[...]
        h = h * (1.0 + sc_ref[0]) + sh_ref[0]
        h_scr[...] = h.astype(BF16)
        acc_scr[...] = jnp.zeros(acc_scr.shape, F32)

    h = h_scr[...]
    a = jnp.dot(h, wg_ref[...], preferred_element_type=F32)
    u = jnp.dot(h, wu_ref[...], preferred_element_type=F32)
    t = (a * _sigmoid(a)) * u
    acc_scr[...] += jnp.dot(t.astype(BF16), wd_ref[...], preferred_element_type=F32)

    @pl.when(f == pl.num_programs(1) - 1)
    def _():
        o_ref[...] = x_ref[...] + g_ref[0] * acc_scr[...]


def _ffn_call(x1, gain2, sc, sh, g2, wg, wu, wd, seq):
    t, d = x1.shape
    dff = wg.shape[1]
    tm = min(512, seq)
    tf = 512 if dff % 512 == 0 else dff
    rows_per_batch = seq // tm
    return pl.pallas_call(
        _ffn_kernel,
        grid=(t // tm, dff // tf),
        in_specs=[pl.BlockSpec((tm, d), lambda i, f: (i, 0)),
                  pl.BlockSpec((1, d), lambda i, f: (0, 0)),
                  pl.BlockSpec((1, 1, d), lambda i, f: (i // rows_per_batch, 0, 0)),
                  pl.BlockSpec((1, 1, d), lambda i, f: (i // rows_per_batch, 0, 0)),
                  pl.BlockSpec((1, 1, d), lambda i, f: (i // rows_per_batch, 0, 0)),
                  pl.BlockSpec((d, tf), lambda i, f: (0, f)),
                  pl.BlockSpec((d, tf), lambda i, f: (0, f)),
                  pl.BlockSpec((tf, d), lambda i, f: (f, 0))],
        out_specs=pl.BlockSpec((tm, d), lambda i, f: (i, 0)),
        out_shape=jax.ShapeDtypeStruct((t, d), F32),
        scratch_shapes=[pltpu.VMEM((tm, d), BF16), pltpu.VMEM((tm, d), F32)],
        compiler_params=_params(2),
        name="swiglu_ffn",
    )(x1, gain2, sc, sh, g2, wg, wu, wd)


def _selection_matrix_t(ncp, n_slc):
    r, q = SEL_BLOCK // CMP_STRIDE, CMP_BLOCK // CMP_STRIDE
    m = np.zeros((LANES, ncp), np.float32)
    for j in range(n_slc):
        for a in range(r):
            for b in range(q):
                c = r * j + a - b
                if 0 <= c < ncp:
                    m[j, c] += 1.0
    return m


def _permute_w_in(w):
    d = w.shape[0]
    o = 0
    parts = {}
    for name, width in (("fq", FOX_WIDTH), ("fk", FOX_WIDTH), ("fv", FOX_WIDTH), ("ff", N_FOX_HEADS),
                        ("nq", NSA_WIDTH), ("nk", N_BRANCH * NSA_KV_WIDTH), ("nv", N_BRANCH * NSA_KV_WIDTH),
                        ("ng", N_BRANCH * N_NSA_HEADS)):
        parts[name] = w[:, o:o + width]
        o += width
    kvw = NSA_KV_WIDTH
    pad = jnp.zeros((d, W32 - (2 * kvw + N_FOX_HEADS + N_BRANCH * N_NSA_HEADS)), w.dtype)
    return jnp.concatenate([
        parts["fq"], parts["fk"], parts["fv"], parts["nq"],
        parts["nk"][:, kvw:2 * kvw], parts["nk"][:, 2 * kvw:3 * kvw],
        parts["nv"][:, kvw:2 * kvw], parts["nv"][:, 2 * kvw:3 * kvw],
        parts["nk"][:, 0:kvw], parts["nv"][:, 0:kvw], parts["ff"], parts["ng"], pad], axis=1).astype(BF16)


def kernel(x, c, ada_w, ada_b, norm1_gain, norm2_gain, w_in, fox_f_bias, fox_q_gain, fox_k_gain, nsa_q_gain,
           nsa_k_gain, nsa_cmp_pos, nsa_cmp_w1, nsa_cmp_w2, rel_bias, mix_out_gain, w_out, ffn_w_gate, ffn_w_up,
           ffn_w_down):
    batch, seq, d = x.shape
    assert seq % KEY_CHUNK == 0 and seq >= WINDOW and d % LANES == 0 and seq // SEL_BLOCK <= LANES
    depth = ada_w.shape[0]
    nqt = seq // LANES
    ncp = -(-(seq // CMP_STRIDE) // LANES) * LANES
    tq_fox = min(256, seq)

    selmt = jnp.asarray(_selection_matrix_t(ncp, seq // SEL_BLOCK), BF16)
    rb_flat = rel_bias.reshape(-1)
    bias_c = _bias_call(rb_flat, nqt, ncp, CMP_STRIDE, CMP_BLOCK - 1, "t5_bias_compressed")
    tz = _bias_call(rb_flat, 3, LANES, 1, 0, "t5_bias_toeplitz")

    ones_h = jnp.ones((HEAD_DIM,), F32)
    c_pad = jnp.pad(c, ((0, 8 - batch % 8 if batch % 8 else 0), (0, 0)))
    x2 = x.reshape(batch * seq, d)
    for layer in range(depth):
        mod = _ada_call(c_pad, ada_w[layer], ada_b[layer][None, :])[:batch]
        sh1, sc1, g1, sh2, sc2, g2 = [mod[:, i * d:(i + 1) * d][:, None, :] for i in range(N_MOD)]

        kg = nsa_k_gain[layer]
        col_gain = jnp.concatenate([
            jnp.tile(fox_q_gain[layer] * QSCALE, N_FOX_HEADS), jnp.tile(fox_k_gain[layer], N_FOX_HEADS),
            jnp.tile(ones_h, N_FOX_HEADS), jnp.tile(nsa_q_gain[layer] * QSCALE, N_NSA_HEADS),
            jnp.tile(kg[1], N_NSA_KV_HEADS), jnp.tile(kg[2], N_NSA_KV_HEADS),
            jnp.tile(ones_h, 2 * N_NSA_KV_HEADS)])[None, :]
        col_flag = jnp.concatenate([
            jnp.ones((2 * FOX_WIDTH,), F32), jnp.zeros((FOX_WIDTH,), F32), jnp.ones((NSA_WIDTH,), F32),
            jnp.ones((2 * NSA_KV_WIDTH,), F32), jnp.zeros((2 * NSA_KV_WIDTH,), F32)])[None, :]
        o16, o32 = _proj_call(x2, sc1, sh1, norm1_gain[layer][None, :], _permute_w_in(w_in[layer]),
                              col_gain, col_flag, seq)
        o16v = o16.reshape(batch, seq, W16)
        o32v = o32.reshape(batch, seq, W32)

        fb_pad = jnp.pad(fox_f_bias[layer], (0, LANES - N_FOX_HEADS))[None, :]
        cum = _cum_call(o32, fb_pad, batch, seq)
        o_fox = _fox_call(o16v, cum, batch, seq, tq_fox)

        w1 = nsa_cmp_w1[layer].reshape(2, CMP_BLOCK, HEAD_DIM, HEAD_DIM).astype(BF16)
        kcvc = _cmp_call(o32, nsa_cmp_pos[layer], w1, nsa_cmp_w2[layer].astype(BF16), kg[0][None, :],
                         batch, seq, ncp)
        o_nsa = _nsa_call(o16v, o32v, kcvc, bias_c, tz, selmt, batch, seq, ncp)

        x1 = _out_call(o_fox.reshape(batch * seq, FOX_WIDTH), o_nsa.reshape(batch * seq, NSA_WIDTH),
                       mix_out_gain[layer][None, :], w_out[layer].astype(BF16), x2, g1, seq)
        x2 = _ffn_call(x1, norm2_gain[layer][None, :], sc2, sh2, g2, ffn_w_gate[layer].astype(BF16),
                       ffn_w_up[layer].astype(BF16), ffn_w_down[layer].astype(BF16), seq)
    return x2.reshape(batch, seq, d)
```

```python
import functools
import math

import numpy as np
import jax
import jax.numpy as jnp
from jax import lax
from jax.experimental import pallas as pl
from jax.experimental.pallas import tpu as pltpu

HEAD_DIM = 128
N_FOX_HEADS = 8
N_NSA_HEADS = 8
N_NSA_KV_HEADS = 2
NSA_GROUP = N_NSA_HEADS // N_NSA_KV_HEADS
FOX_WIDTH = N_FOX_HEADS * HEAD_DIM
NSA_WIDTH = N_NSA_HEADS * HEAD_DIM
NSA_KV_WIDTH = N_NSA_KV_HEADS * HEAD_DIM
MIX_WIDTH = FOX_WIDTH + NSA_WIDTH
N_BRANCH = 3
CMP_BLOCK = 32
CMP_STRIDE = 16
SEL_BLOCK = 64
N_SEL = 8
WINDOW = 512
N_BUCKETS = 32
MAX_DISTANCE = 128
N_MOD = 6
SCALE = HEAD_DIM ** -0.5
LOG2E = math.log2(math.e)
QSCALE = SCALE * LOG2E
EPS = 1e-6
NEG = -1e30
FORCE = 1e6

LANES = 128
GROUP_ROWS = NSA_GROUP * LANES
VMEM_LIMIT = 56 * 1024 * 1024
MXU_COLS = 256
KEY_CHUNK = MXU_COLS
WIN_PAD = WINDOW + LANES
MASK_BIG = 2.0 ** 100
SEL, WIN = 0, 1

W16 = 3 * FOX_WIDTH + NSA_WIDTH + 4 * NSA_KV_WIDTH
W32 = 1024
COL_FF = 2 * NSA_KV_WIDTH
COL_NG = COL_FF + N_FOX_HEADS

F32 = jnp.float32
BF16 = jnp.bfloat16
NT_DIMS = (((1,), (1,)), ((), ()))


def _params(n_axes):
    return pltpu.CompilerParams(dimension_semantics=("arbitrary",) * n_axes,
                                vmem_limit_bytes=VMEM_LIMIT)


def _sigmoid(x):
    return 1.0 / (1.0 + jnp.exp(-x))


def _lane_tile(a, n):
    return jnp.concatenate([a] * n, axis=1)


def _rms(x):
    return x * lax.rsqrt(jnp.mean(x * x, axis=-1, keepdims=True) + EPS)


def _ada_kernel(c_ref, w_ref, b_ref, o_ref):
    c = c_ref[...]
    s = (c * _sigmoid(c)).astype(BF16)
    o_ref[...] = jnp.dot(s, w_ref[...].astype(BF16), preferred_element_type=F32) + b_ref[...]


def _ada_call(c_pad, w, b):
    rows, d = c_pad.shape
    n = w.shape[1]
    tn = next(t for t in (1024, 768, 512, 384, 256, 128) if n % t == 0)
    return pl.pallas_call(
        _ada_kernel,
        grid=(n // tn,),
        in_specs=[pl.BlockSpec((rows, d), lambda j: (0, 0)),
                  pl.BlockSpec((d, tn), lambda j: (0, j)),
                  pl.BlockSpec((1, tn), lambda j: (0, j))],
        out_specs=pl.BlockSpec((rows, tn), lambda j: (0, j)),
        out_shape=jax.ShapeDtypeStruct((rows, n), F32),
        compiler_params=_params(1),
        name="adaln",
    )(c_pad, w, b)


def _proj_kernel(x_ref, sc_ref, sh_ref, g_ref, w_ref, gain_ref, flag_ref, o16_ref, o32_ref, h_scr, *, n16, tn):
    j = pl.program_id(1)

    @pl.when(j == 0)
    def _():
        h = _rms(x_ref[...]) * g_ref[...]
        h = h * (1.0 + sc_ref[0]) + sh_ref[0]
        h_scr[...] = h.astype(BF16)

    @pl.when(j < n16)
    def _():
        h = h_scr[...]
        for c in range(tn // MXU_COLS):
            acc = jnp.dot(h, w_ref[:, c * MXU_COLS:(c + 1) * MXU_COLS], preferred_element_type=F32)
            for g in range(MXU_COLS // LANES):
                cols = slice(c * MXU_COLS + g * LANES, c * MXU_COLS + (g + 1) * LANES)
                a = acc[:, g * LANES:(g + 1) * LANES]
                r = lax.rsqrt(jnp.mean(a * a, axis=-1, keepdims=True) + EPS)
                scale = jnp.where(flag_ref[:, cols] > 0.5, r, 1.0)
                o16_ref[:, cols] = (a * scale * gain_ref[:, cols]).astype(BF16)

    @pl.when(j >= n16)
    def _():
        o32_ref[...] = jnp.dot(h_scr[...], w_ref[...], preferred_element_type=F32)


def _proj_call(x2, sc, sh, gain1, w_perm, col_gain, col_flag, seq):
    t, d = x2.shape
    tm = min(1024, seq)
    tn = W32
    n16 = W16 // tn
    rows_per_batch = seq // tm
    return pl.pallas_call(
        functools.partial(_proj_kernel, n16=n16, tn=tn),
        grid=(t // tm, n16 + 1),
        in_specs=[pl.BlockSpec((tm, d), lambda i, j: (i, 0)),
                  pl.BlockSpec((1, 1, d), lambda i, j: (i // rows_per_batch, 0, 0)),
                  pl.BlockSpec((1, 1, d), lambda i, j: (i // rows_per_batch, 0, 0)),
                  pl.BlockSpec((1, d), lambda i, j: (0, 0)),
                  pl.BlockSpec((d, tn), lambda i, j: (0, j)),
                  pl.BlockSpec((1, tn), lambda i, j: (0, jnp.minimum(j, n16 - 1))),
                  pl.BlockSpec((1, tn), lambda i, j: (0, jnp.minimum(j, n16 - 1)))],
        out_specs=[pl.BlockSpec((tm, tn), lambda i, j: (i, jnp.minimum(j, n16 - 1))),
                   pl.BlockSpec((tm, tn), lambda i, j: (i, 0))],
        out_shape=[jax.ShapeDtypeStruct((t, W16), BF16),
                   jax.ShapeDtypeStruct((t, W32), F32)],
        scratch_shapes=[pltpu.VMEM((tm, d), BF16)],
        compiler_params=_params(2),
        name="in_proj",
    )(x2, sc, sh, gain1, w_perm, col_gain, col_flag)


def _cum_kernel(ff_ref, fb_ref, o_ref, *, seq):
    ri = lax.broadcasted_iota(jnp.int32, (LANES, LANES), 0)
    ci = lax.broadcasted_iota(jnp.int32, (LANES, LANES), 1)
    tri = jnp.where(ri >= ci, 1.0, 0.0).astype(BF16)
    carry = jnp.zeros((1, LANES), F32)
    for blk in range(seq // LANES):
        rows = slice(blk * LANES, (blk + 1) * LANES)
        x = ff_ref[0, rows, :] + fb_ref[...]
        lf = jnp.minimum(x, 0.0) - jnp.log(1.0 + jnp.exp(-jnp.abs(x)))
        hi = lf.astype(BF16)
        r1 = lf - hi.astype(F32)
        mid = r1.astype(BF16)
        lo = (r1 - mid.astype(F32)).astype(BF16)
        c = (jnp.dot(tri, hi, preferred_element_type=F32)
             + jnp.dot(tri, mid, preferred_element_type=F32)
             + jnp.dot(tri, lo, preferred_element_type=F32)) + carry
        o_ref[0, rows, :] = c
        carry = c[LANES - 1:LANES, :]


def _cum_call(o32, fb_pad, batch, seq):
    o32v = o32.reshape(batch, seq, W32)
    return pl.pallas_call(
        functools.partial(_cum_kernel, seq=seq),
        grid=(batch,),
        in_specs=[pl.BlockSpec((1, seq, LANES), lambda b: (b, 0, COL_FF // LANES)),
                  pl.BlockSpec((1, LANES), lambda b: (0, 0))],
        out_specs=pl.BlockSpec((1, seq, LANES), lambda b: (b, 0, 0)),
        out_shape=jax.ShapeDtypeStruct((batch, seq, LANES), F32),
        compiler_params=_params(1),
        name="fox_cumsum",
    )(o32v, fb_pad)


def _split3(c):
    hi = c.astype(BF16).astype(F32)
    r1 = c - hi
    mid = r1.astype(BF16).astype(F32)
    return hi, mid, (r1 - mid).astype(BF16).astype(F32)


def _fox_kernel(q_ref, k_ref, v_ref, cum_ref, o_ref, m_scr, acc_scr, s_scr, kaug_scr, vaug_scr, *, seq, tq, nh):
    hp = pl.program_id(1)
    lane = lax.broadcasted_iota(jnp.int32, (tq, LANES), 1)
    lane_s = lax.broadcasted_iota(jnp.int32, (seq, LANES), 1)
    causal = lax.broadcasted_iota(jnp.int32, (tq, tq), 1) <= lax.broadcasted_iota(jnp.int32, (tq, tq), 0)
    for j in range(nh):
        ck = jnp.sum(jnp.where(lane_s == hp * nh + j, cum_ref[0], 0.0), axis=1, keepdims=True) * LOG2E
        hi, mid, lo = _split3(ck)
        tail = jnp.where(lane_s < 3, 1.0, jnp.where(lane_s == 3, -hi, jnp.where(lane_s == 4, -mid,
                         jnp.where(lane_s == 5, -lo, 0.0))))
        for c in range(seq // tq):
            rows = slice(c * tq, (c + 1) * tq)
            kaug_scr[j, c, 0:HEAD_DIM, :] = k_ref[0, rows, j * HEAD_DIM:(j + 1) * HEAD_DIM].T
            kaug_scr[j, c, HEAD_DIM:, :] = tail[rows].T.astype(BF16)
        vaug_scr[j, :, 0:HEAD_DIM] = v_ref[0, :, j * HEAD_DIM:(j + 1) * HEAD_DIM]
        vaug_scr[j, :, HEAD_DIM:] = jnp.ones((seq, HEAD_DIM), BF16)

    def q_body(qi, carry):
        q0 = pl.multiple_of(qi * tq, tq)
        cum_t = cum_ref[0, pl.ds(q0, tq), :]
        qs = []
        for j in range(nh):
            cq = jnp.sum(jnp.where(lane == hp * nh + j, cum_t, 0.0), axis=1, keepdims=True) * LOG2E
            hi, mid, lo = _split3(cq)
            tail = jnp.where(lane == 0, hi, jnp.where(lane == 1, mid, jnp.where(lane == 2, lo,
                             jnp.where(lane < 6, 1.0, 0.0))))
            qs.append(jnp.concatenate([q_ref[0, pl.ds(q0, tq), j * HEAD_DIM:(j + 1) * HEAD_DIM],
                                       tail.astype(BF16)], axis=1))
            m_scr[j] = jnp.full((tq, LANES), NEG, F32)
            acc_scr[j] = jnp.zeros((tq, 2 * HEAD_DIM), F32)

        def scores(j, ki):
            return jnp.dot(qs[j], kaug_scr[j, ki], preferred_element_type=F32)

        def tile(ki, diagonal):
            k0 = pl.multiple_of(ki * tq, tq)
            for j in range(nh):
                s = s_scr[j]
                if diagonal:
                    s = jnp.where(causal, s, NEG)
                else:
                    s_scr[j] = scores(j, ki + 1)
                m_prev = m_scr[j]
                m_new = jnp.maximum(m_prev, jnp.max(s, axis=1, keepdims=True))
                alpha = jnp.exp2(m_prev - m_new)
                p = jnp.exp2(s - _lane_tile(m_new, tq // LANES))
                pv = jnp.dot(p.astype(BF16), vaug_scr[j, pl.ds(k0, tq), :], preferred_element_type=F32)
                acc_scr[j] = _lane_tile(alpha, 2) * acc_scr[j] + pv
                m_scr[j] = m_new

        def k_body(ki, c2):
            tile(ki, False)
            return c2

        for j in range(nh):
            s_scr[j] = scores(j, 0)
        lax.fori_loop(0, qi, k_body, 0)
        tile(qi, True)
        for j in range(nh):
            acc = acc_scr[j]
            o_ref[0, pl.ds(q0, tq), j * HEAD_DIM:(j + 1) * HEAD_DIM] = acc[:, 0:HEAD_DIM] / acc[:, HEAD_DIM:]
        return carry

    lax.fori_loop(0, seq // tq, q_body, 0)


def _fox_call(o16v, cum, batch, seq, tq):
    nh = 4
    groups = N_FOX_HEADS // nh
    w = nh * HEAD_DIM
    return pl.pallas_call(
        functools.partial(_fox_kernel, seq=seq, tq=tq, nh=nh),
        grid=(batch, groups),
        in_specs=[pl.BlockSpec((1, seq, w), lambda b, h: (b, 0, h)),
                  pl.BlockSpec((1, seq, w), lambda b, h: (b, 0, groups + h)),
                  pl.BlockSpec((1, seq, w), lambda b, h: (b, 0, 2 * groups + h)),
                  pl.BlockSpec((1, seq, LANES), lambda b, h: (b, 0, 0))],
        out_specs=pl.BlockSpec((1, seq, w), lambda b, h: (b, 0, h)),
        out_shape=jax.ShapeDtypeStruct((batch, seq, FOX_WIDTH), F32),
        scratch_shapes=[pltpu.VMEM((nh, tq, LANES), F32),
                        pltpu.VMEM((nh, tq, 2 * HEAD_DIM), F32),
                        pltpu.VMEM((nh, tq, tq), F32),
                        pltpu.VMEM((nh, seq // tq, 2 * HEAD_DIM, tq), BF16),
                        pltpu.VMEM((nh, seq, 2 * HEAD_DIM), BF16)],
        compiler_params=_params(2),
        name="fox_attention",
    )(o16v, o16v, o16v, cum)


def _cmp_kernel(x_ref, pos_ref, w1_ref, w2_ref, gain_ref, o_ref, xs_scr, *, seq, ncp):
    kv = pl.program_id(1)
    rows = xs_scr.shape[0]
    xs_scr[0:seq, :] = x_ref[0]
    xs_scr[seq:rows, :] = jnp.zeros((rows - seq, LANES), F32)
    acc = jnp.zeros((ncp, HEAD_DIM), F32)
    for l in range(CMP_BLOCK):
        xl = xs_scr[pl.ds(l, ncp, stride=CMP_STRIDE), :] + pos_ref[0, l:l + 1, :]
        acc = acc + jnp.dot(xl.astype(BF16), w1_ref[0, l], preferred_element_type=F32)
    hmid = acc * _sigmoid(acc)
    y = jnp.dot(hmid.astype(BF16), w2_ref[0], preferred_element_type=F32)
    yk = _rms(y) * gain_ref[...]
    o_ref[0, 0, 0] = jnp.where(kv == 0, yk, y).astype(BF16)


def _cmp_call(o32, pos, w1, w2, gain, batch, seq, ncp):
    o32v = o32.reshape(batch, seq, W32)
    hkv = N_NSA_KV_HEADS
    return pl.pallas_call(
        functools.partial(_cmp_kernel, seq=seq, ncp=ncp),
        grid=(batch, 2, hkv),
        in_specs=[pl.BlockSpec((1, seq, HEAD_DIM), lambda b, kv, h: (b, 0, kv * hkv + h)),
                  pl.BlockSpec((1, CMP_BLOCK, HEAD_DIM), lambda b, kv, h: (kv, 0, 0)),
                  pl.BlockSpec((1, CMP_BLOCK, HEAD_DIM, HEAD_DIM), lambda b, kv, h: (kv, 0, 0, 0)),
                  pl.BlockSpec((1, HEAD_DIM, HEAD_DIM), lambda b, kv, h: (kv, 0, 0)),
                  pl.BlockSpec((1, HEAD_DIM), lambda b, kv, h: (0, 0))],
        out_specs=pl.BlockSpec((1, 1, 1, ncp, HEAD_DIM), lambda b, kv, h: (b, kv, h, 0, 0)),
        out_shape=jax.ShapeDtypeStruct((batch, 2, hkv, ncp, HEAD_DIM), BF16),
        scratch_shapes=[pltpu.VMEM((CMP_STRIDE * ncp + CMP_BLOCK, LANES), F32)],
        compiler_params=_params(3),
        name="nsa_compress",
    )(o32v, pos, w1, w2, gain)


def _bias_kernel(rb_ref, o_ref, *, width, key_stride, key_offset):
    v = pl.program_id(0)
    i = lax.broadcasted_iota(jnp.int32, (LANES, width), 0)
    j = lax.broadcasted_iota(jnp.int32, (LANES, width), 1)
    d = v * LANES + i - (key_stride * j + key_offset)
    n = jnp.maximum(d, 0)
    max_exact = N_BUCKETS // 2
    nf = jnp.maximum(n, 1).astype(F32)
    large = max_exact + (jnp.log(nf / max_exact) / math.log(MAX_DISTANCE / max_exact)
                         * (N_BUCKETS - max_exact)).astype(jnp.int32)
    large = jnp.minimum(large, N_BUCKETS - 1)
    bkt = jnp.where(n < max_exact, n, large)
    vals = [jnp.zeros((LANES, width), F32) for _ in range(N_NSA_HEADS)]
    for bk in range(N_BUCKETS):
        hit = bkt == bk
        for h in range(N_NSA_HEADS):
            vals[h] = jnp.where(hit, rb_ref[bk * N_NSA_HEADS + h] * LOG2E, vals[h])
    for h in range(N_NSA_HEADS):
        g = h % NSA_GROUP
        o_ref[0, h // NSA_GROUP, g * LANES:(g + 1) * LANES, :] = vals[h]


def _bias_call(rb_flat, n_variants, width, key_stride, key_offset, name):
    return pl.pallas_call(
        functools.partial(_bias_kernel, width=width, key_stride=key_stride, key_offset=key_offset),
        grid=(n_variants,),
        in_specs=[pl.BlockSpec(memory_space=pltpu.SMEM)],
        out_specs=pl.BlockSpec((1, N_NSA_KV_HEADS, GROUP_ROWS, width), lambda v: (v, 0, 0, 0)),
        out_shape=jax.ShapeDtypeStruct((n_variants, N_NSA_KV_HEADS, GROUP_ROWS, width), F32),
        compiler_params=_params(1),
        name=name,
    )(rb_flat)


def _tile4(a):
    return jnp.concatenate([a] * NSA_GROUP, axis=0)


def _nsa_kernel(q_ref, ks_ref, kw_ref, vs_ref, vw_ref, g_ref, kcvc_ref, bc_ref, tz_ref, selmt_ref,
                o_ref, qg_scr, m_scr, acc_scr, o_scr, s_scr, ksa_scr, vsa_scr, kwp_scr, vwa_scr, *, seq, ncp):
    qt = pl.program_id(1)
    q0 = qt * LANES
    n_slc = seq // SEL_BLOCK
    top_n = min(N_SEL, n_slc)
    nkv = N_NSA_KV_HEADS
    ri = lax.broadcasted_iota(jnp.int32, (LANES, LANES), 0)
    ci = lax.broadcasted_iota(jnp.int32, (LANES, LANES), 1)
    rk = lax.broadcasted_iota(jnp.int32, (LANES, KEY_CHUNK), 0)
    ck = lax.broadcasted_iota(jnp.int32, (LANES, KEY_CHUNK), 1)
    eye = jnp.where(ri == ci, 1.0, 0.0).astype(BF16)
    gates = _sigmoid(g_ref[0])

    def gate_col(hk, br):
        cols = []
        for g in range(NSA_GROUP):
            c = COL_NG % LANES + (hk * NSA_GROUP + g) * N_BRANCH + br
            cols.append(gates[:, c:c + 1])
        return jnp.concatenate(cols, axis=0)

    @pl.when(qt == 0)
    def _():
        ones = jnp.ones((seq, HEAD_DIM), BF16)
        blk_row = lax.broadcasted_iota(jnp.int32, (LANES, KEY_CHUNK), 0)
        for hk in range(nkv):
            hc = slice(hk * HEAD_DIM, (hk + 1) * HEAD_DIM)
            for c in range(seq // KEY_CHUNK):
                rows = slice(c * KEY_CHUNK, (c + 1) * KEY_CHUNK)
                ksa_scr[hk, c, 0:HEAD_DIM, :] = ks_ref[0, rows, hc].T
                ksa_scr[hk, c, HEAD_DIM:, :] = jnp.where((c * KEY_CHUNK + ck) // SEL_BLOCK == blk_row,
                                                         -MASK_BIG, 0.0).astype(BF16)
            for t in range(WIN_PAD // LANES):
                kwp_scr[hk, t] = jnp.zeros((HEAD_DIM, LANES), BF16)
            for t in range(seq // LANES):
                kwp_scr[hk, WIN_PAD // LANES + t] = kw_ref[0, t * LANES:(t + 1) * LANES, hc].T
            vsa_scr[hk, :, 0:HEAD_DIM] = vs_ref[0, :, hc]
            vsa_scr[hk, :, HEAD_DIM:] = ones
            vwa_scr[hk, 0:WIN_PAD, :] = jnp.zeros((WIN_PAD, 2 * HEAD_DIM), BF16)
            vwa_scr[hk, WIN_PAD:, 0:HEAD_DIM] = vw_ref[0, :, hc]
            vwa_scr[hk, WIN_PAD:, HEAD_DIM:] = ones

    def reset(br):
        m_scr[br] = jnp.full(m_scr.shape[1:], NEG, F32)
        acc_scr[br] = jnp.zeros(acc_scr.shape[1:], F32)

    def online_update(br, hk, s, vaug):
        m_prev = m_scr[br, hk]
        m_new = jnp.maximum(m_prev, jnp.max(s, axis=1, keepdims=True))
        alpha = jnp.exp2(m_prev - m_new)
        p = jnp.exp2(s - _lane_tile(m_new, s.shape[1] // LANES))
        pv = jnp.dot(p.astype(BF16), vaug, preferred_element_type=F32)
        acc_scr[br, hk] = _lane_tile(alpha, 2) * acc_scr[br, hk] + pv
        m_scr[br, hk] = m_new

    def finish(br, hk):
        acc = acc_scr[br, hk]
        return acc[:, 0:HEAD_DIM] / acc[:, HEAD_DIM:]

    def chunk_bias(hk, ta, tb):
        return jnp.concatenate([tz_ref[jnp.clip(ta, 0, 2), hk], tz_ref[jnp.clip(tb, 0, 2), hk]], axis=1)

    for hk in range(nkv):
        for g in range(NSA_GROUP):
            h = hk * NSA_GROUP + g
            qg_scr[hk, g * LANES:(g + 1) * LANES, 0:HEAD_DIM] = q_ref[0, :, h * HEAD_DIM:(h + 1) * HEAD_DIM]

    for hk in range(nkv):
        qg = qg_scr[hk, :, 0:HEAD_DIM]

        kc = kcvc_ref[0, 0, hk]
        vc = kcvc_ref[0, 1, hk]
        s = lax.dot_general(qg, kc, NT_DIMS, preferred_element_type=F32) + bc_ref[0, hk]
        rc = lax.broadcasted_iota(jnp.int32, (LANES, ncp), 0)
        cc = lax.broadcasted_iota(jnp.int32, (LANES, ncp), 1)
        valid_c = _tile4(jnp.where(q0 + rc - (CMP_STRIDE * cc + CMP_BLOCK - 1) >= 0, 1.0, 0.0)) > 0.5
        s = jnp.where(valid_c, s, NEG)
        p = jnp.where(valid_c, jnp.exp2(s - jnp.max(s, axis=1, keepdims=True)), 0.0)
        l = jnp.sum(p, axis=1, keepdims=True)
        p = p / jnp.where(l > 0.0, l, 1.0)
        o_scr[hk] = gate_col(hk, 0) * jnp.dot(p.astype(BF16), vc, preferred_element_type=F32)

        psum = p[0:LANES]
        for g in range(1, NSA_GROUP):
            psum = psum + p[g * LANES:(g + 1) * LANES]
        p_hi = psum.astype(BF16)
        p_lo = (psum - p_hi.astype(F32)).astype(BF16)
        selmt = selmt_ref[...]
        imp = (lax.dot_general(selmt, p_hi, NT_DIMS, preferred_element_type=F32)
               + lax.dot_general(selmt, p_lo, NT_DIMS, preferred_element_type=F32))
        imp = imp[0:n_slc]
        blk = lax.broadcasted_iota(jnp.int32, (n_slc, LANES), 0)
        cur = (q0 + lax.broadcasted_iota(jnp.int32, (n_slc, LANES), 1)) // SEL_BLOCK
        forced = (blk == 0) | (blk == cur) | (blk == cur - 1)
        imp = jnp.where(forced, FORCE, imp)
        imp = jnp.where(blk <= cur, imp, -jnp.inf)
        rank = jnp.zeros((n_slc, LANES), F32)
        for j in range(n_slc):
            row = imp[j:j + 1, :]
            beats = jnp.where(row > imp, 1.0, jnp.where(row == imp, jnp.where(blk > j, 1.0, 0.0), 0.0))
            rank = rank + beats
        sel_t = jnp.where(rank < top_n, jnp.where(imp > -jnp.inf, 1.0, 0.0), 0.0)
        if n_slc < LANES:
            sel_t = jnp.concatenate([sel_t, jnp.zeros((LANES - n_slc, LANES), F32)], axis=0)
        sel_q = lax.dot_general(eye, sel_t.astype(BF16), NT_DIMS, preferred_element_type=F32)
        not_sel = jnp.where(ci < n_slc, 1.0 - sel_q, 0.0).astype(BF16)
        for g in range(NSA_GROUP):
            qg_scr[hk, g * LANES:(g + 1) * LANES, HEAD_DIM:] = not_sel

    reset(WIN)
    for off, n_tiles in ((1, 2), (3, 2), (5, 2)):
        width = n_tiles * LANES
        kstart = (qt - off) * LANES
        pos = kstart + lax.broadcasted_iota(jnp.int32, (LANES, width), 1)
        d = (q0 + lax.broadcasted_iota(jnp.int32, (LANES, width), 0)) - pos
        ok = (d >= 0) & (d < WINDOW) & (pos >= 0)
        madd = _tile4(jnp.where(ok, 0.0, -MASK_BIG))
        t0 = qt - off + WIN_PAD // LANES
        p0 = pl.multiple_of(kstart + WIN_PAD, LANES)
        for hk in range(nkv):
            k_t = jnp.concatenate([kwp_scr[hk, t0 + t] for t in range(n_tiles)], axis=1)
            bias = jnp.concatenate([tz_ref[min(off - t, 2), hk] for t in range(n_tiles)], axis=1)
            s = jnp.dot(qg_scr[hk, :, 0:HEAD_DIM], k_t, preferred_element_type=F32) + bias + madd
            online_update(WIN, hk, s, vwa_scr[hk, pl.ds(p0, width), :])

    reset(SEL)

    def sel_scores(hk, c):
        ta = qt - 2 * c
        return jnp.dot(qg_scr[hk], ksa_scr[hk, c], preferred_element_type=F32) + chunk_bias(hk, ta, ta - 1)

    def sel_chunk(c, last):
        k0 = pl.multiple_of(c * KEY_CHUNK, KEY_CHUNK)
        for hk in range(nkv):
            s = s_scr[hk]
            if last:
                s = jnp.where(_tile4(jnp.where((k0 + ck) <= (q0 + rk), 1.0, 0.0)) > 0.5, s, -MASK_BIG)
            else:
                s_scr[hk] = sel_scores(hk, c + 1)
            online_update(SEL, hk, s, vsa_scr[hk, pl.ds(k0, KEY_CHUNK), :])

    def sel_body(c, carry):
        sel_chunk(c, False)
        return carry

    n_chunks = (qt + 2) // 2
    for hk in range(nkv):
        s_scr[hk] = sel_scores(hk, 0)
    lax.fori_loop(0, n_chunks - 1, sel_body, 0)
    sel_chunk(n_chunks - 1, True)
    for hk in range(nkv):
        o = (o_scr[hk] + gate_col(hk, 1) * finish(SEL, hk)) + gate_col(hk, 2) * finish(WIN, hk)
        for g in range(NSA_GROUP):
            h = hk * NSA_GROUP + g
            o_ref[0, :, h * HEAD_DIM:(h + 1) * HEAD_DIM] = o[g * LANES:(g + 1) * LANES]


def _nsa_call(o16v, o32v, kcvc, bias_c, tz, selmt, batch, seq, ncp):
    nqt = seq // LANES
    kvw = NSA_KV_WIDTH
    nkv = N_NSA_KV_HEADS
    base = (3 * FOX_WIDTH + NSA_WIDTH) // kvw
    return pl.pallas_call(
        functools.partial(_nsa_kernel, seq=seq, ncp=ncp),
        grid=(batch, nqt),
        in_specs=[pl.BlockSpec((1, LANES, NSA_WIDTH), lambda b, t: (b, t, 3 * FOX_WIDTH // NSA_WIDTH)),
                  pl.BlockSpec((1, seq, kvw), lambda b, t: (b, 0, base)),
                  pl.BlockSpec((1, seq, kvw), lambda b, t: (b, 0, base + 1)),
                  pl.BlockSpec((1, seq, kvw), lambda b, t: (b, 0, base + 2)),
                  pl.BlockSpec((1, seq, kvw), lambda b, t: (b, 0, base + 3)),
                  pl.BlockSpec((1, LANES, LANES), lambda b, t: (b, t, COL_NG // LANES)),
                  pl.BlockSpec((1, 2, N_NSA_KV_HEADS, ncp, HEAD_DIM), lambda b, t: (b, 0, 0, 0, 0)),
                  pl.BlockSpec((1, N_NSA_KV_HEADS, GROUP_ROWS, ncp), lambda b, t: (t, 0, 0, 0)),
                  pl.BlockSpec((3, N_NSA_KV_HEADS, GROUP_ROWS, LANES), lambda b, t: (0, 0, 0, 0)),
                  pl.BlockSpec((LANES, ncp), lambda b, t: (0, 0))],
        out_specs=pl.BlockSpec((1, LANES, NSA_WIDTH), lambda b, t: (b, t, 0)),
        out_shape=jax.ShapeDtypeStruct((batch, seq, NSA_WIDTH), F32),
        scratch_shapes=[pltpu.VMEM((nkv, GROUP_ROWS, 2 * HEAD_DIM), BF16),
                        pltpu.VMEM((2, nkv, GROUP_ROWS, LANES), F32),
                        pltpu.VMEM((2, nkv, GROUP_ROWS, 2 * HEAD_DIM), F32),
                        pltpu.VMEM((nkv, GROUP_ROWS, HEAD_DIM), F32),
                        pltpu.VMEM((nkv, GROUP_ROWS, KEY_CHUNK), F32),
                        pltpu.VMEM((nkv, seq // KEY_CHUNK, 2 * HEAD_DIM, KEY_CHUNK), BF16),
                        pltpu.VMEM((nkv, seq, 2 * HEAD_DIM), BF16),
                        pltpu.VMEM((nkv, (seq + WIN_PAD) // LANES, HEAD_DIM, LANES), BF16),
                        pltpu.VMEM((nkv, seq + WIN_PAD, 2 * HEAD_DIM), BF16)],
        compiler_params=_params(2),
        name="nsa_attention",
    )(o16v, o16v, o16v, o16v, o16v, o32v, kcvc, bias_c, tz, selmt)


def _out_kernel(of_ref, on_ref, gain_ref, w_ref, x_ref, g_ref, o_ref, *, tm, n_chains):
    rows_per_chain = tm // n_chains
    for c in range(n_chains):
        rows = slice(c * rows_per_chain, (c + 1) * rows_per_chain)
        y = jnp.concatenate([(_rms(of_ref[rows, :]) * gain_ref[:, 0:FOX_WIDTH]).astype(BF16),
                             (_rms(on_ref[rows, :]) * gain_ref[:, FOX_WIDTH:MIX_WIDTH]).astype(BF16)], axis=1)
        acc = jnp.dot(y, w_ref[...], preferred_element_type=F32)
        o_ref[rows, :] = x_ref[rows, :] + g_ref[0] * acc


def _out_call(o_fox, o_nsa, gain, w_out, x2, g1, seq):
    t, d = x2.shape
    tm = min(512, seq)
    rows_per_batch = seq // tm
    return pl.pallas_call(
        functools.partial(_out_kernel, tm=tm, n_chains=2),
        grid=(t // tm,),
        in_specs=[pl.BlockSpec((tm, FOX_WIDTH), lambda i: (i, 0)),
                  pl.BlockSpec((tm, NSA_WIDTH), lambda i: (i, 0)),
                  pl.BlockSpec((1, MIX_WIDTH), lambda i: (0, 0)),
                  pl.BlockSpec((MIX_WIDTH, d), lambda i: (0, 0)),
                  pl.BlockSpec((tm, d), lambda i: (i, 0)),
                  pl.BlockSpec((1, 1, d), lambda i: (i // rows_per_batch, 0, 0))],
        out_specs=pl.BlockSpec((tm, d), lambda i: (i, 0)),
        out_shape=jax.ShapeDtypeStruct((t, d), F32),
        compiler_params=_params(1),
        name="out_proj",
    )(o_fox, o_nsa, gain, w_out, x2, g1)


def _ffn_kernel(x_ref, gain_ref, sc_ref, sh_ref, g_ref, wg_ref, wu_ref, wd_ref, o_ref, h_scr, acc_scr):
    f = pl.program_id(1)

    @pl.when(f == 0)
    def _():
        h = _rms(x_ref[...]) * gain_ref[...]
        h = h * (1.0 + sc_ref[0]) + sh_ref[0]
        h_scr[...] = h.astype(BF16)
        acc_scr[...] = jnp.zeros(acc_scr.shape, F32)

    h = h_scr[...]
    a = jnp.dot(h, wg_ref[...], preferred_element_type=F32)
    u = jnp.dot(h, wu_ref[...], preferred_element_type=F32)
    t = (a * _sigmoid(a)) * u
    acc_scr[...] += jnp.dot(t.astype(BF16), wd_ref[...], preferred_element_type=F32)

    @pl.when(f == pl.num_programs(1) - 1)
    def _():
        o_ref[...] = x_ref[...] + g_ref[0] * acc_scr[...]


def _ffn_call(x1, gain2, sc, sh, g2, wg, wu, wd, seq):
    t, d = x1.shape
    dff = wg.shape[1]
    tm = min(512, seq)
    tf = 512 if dff % 512 == 0 else dff
    rows_per_batch = seq // tm
    return pl.pallas_call(
        _ffn_kernel,
        grid=(t // tm, dff // tf),
        in_specs=[pl.BlockSpec((tm, d), lambda i, f: (i, 0)),
                  pl.BlockSpec((1, d), lambda i, f: (0, 0)),
                  pl.BlockSpec((1, 1, d), lambda i, f: (i // rows_per_batch, 0, 0)),
                  pl.BlockSpec((1, 1, d), lambda i, f: (i // rows_per_batch, 0, 0)),
                  pl.BlockSpec((1, 1, d), lambda i, f: (i // rows_per_batch, 0, 0)),
                  pl.BlockSpec((d, tf), lambda i, f: (0, f)),
                  pl.BlockSpec((d, tf), lambda i, f: (0, f)),
                  pl.BlockSpec((tf, d), lambda i, f: (f, 0))],
        out_specs=pl.BlockSpec((tm, d), lambda i, f: (i, 0)),
        out_shape=jax.ShapeDtypeStruct((t, d), F32),
        scratch_shapes=[pltpu.VMEM((tm, d), BF16), pltpu.VMEM((tm, d), F32)],
        compiler_params=_params(2),
        name="swiglu_ffn",
    )(x1, gain2, sc, sh, g2, wg, wu, wd)


def _selection_matrix_t(ncp, n_slc):
    r, q = SEL_BLOCK // CMP_STRIDE, CMP_BLOCK // CMP_STRIDE
    m = np.zeros((LANES, ncp), np.float32)
    for j in range(n_slc):
        for a in range(r):
            for b in range(q):
                c = r * j + a - b
                if 0 <= c < ncp:
                    m[j, c] += 1.0
    return m


def _w_in_column_moves():
    o = 0
    start = {}
    for name, width in (("fq", FOX_WIDTH), ("fk", FOX_WIDTH), ("fv", FOX_WIDTH), ("ff", N_FOX_HEADS),
                        ("nq", NSA_WIDTH), ("nk", N_BRANCH * NSA_KV_WIDTH), ("nv", N_BRANCH * NSA_KV_WIDTH),
                        ("ng", N_BRANCH * N_NSA_HEADS)):
        start[name] = o
        o += width
    kvw = NSA_KV_WIDTH
    return [(start["fq"], 3 * FOX_WIDTH), (start["nq"], NSA_WIDTH),
            (start["nk"] + kvw, 2 * kvw), (start["nv"] + kvw, 2 * kvw),
            (start["nk"], kvw), (start["nv"], kvw),
            (start["ff"], N_FOX_HEADS), (start["ng"], N_BRANCH * N_NSA_HEADS)]


def _repack_kernel(w_ref, o_ref):
    o = 0
    for src, width in _w_in_column_moves():
        o_ref[:, o:o + width] = w_ref[:, src:src + width].astype(BF16)
        o += width
    o_ref[:, o:] = jnp.zeros((o_ref.shape[0], o_ref.shape[1] - o), BF16)


def _permute_w_in(w):
    d, n = w.shape
    tr = 256 if d % 256 == 0 else d
    return pl.pallas_call(
        _repack_kernel,
        grid=(d // tr,),
        in_specs=[pl.BlockSpec((tr, n), lambda i: (i, 0))],
        out_specs=pl.BlockSpec((tr, W16 + W32), lambda i: (i, 0)),
        out_shape=jax.ShapeDtypeStruct((d, W16 + W32), BF16),
        compiler_params=_params(1),
        name="w_in_repack",
    )(w)


def kernel(x, c, ada_w, ada_b, norm1_gain, norm2_gain, w_in, fox_f_bias, fox_q_gain, fox_k_gain, nsa_q_gain,
           nsa_k_gain, nsa_cmp_pos, nsa_cmp_w1, nsa_cmp_w2, rel_bias, mix_out_gain, w_out, ffn_w_gate, ffn_w_up,
           ffn_w_down):
    batch, seq, d = x.shape
    assert seq % KEY_CHUNK == 0 and seq >= WINDOW and d % LANES == 0 and seq // SEL_BLOCK <= LANES
    depth = ada_w.shape[0]
    nqt = seq // LANES
    ncp = -(-(seq // CMP_STRIDE) // LANES) * LANES
    tq_fox = min(256, seq)

    selmt = jnp.asarray(_selection_matrix_t(ncp, seq // SEL_BLOCK), BF16)
    rb_flat = rel_bias.reshape(-1)
    bias_c = _bias_call(rb_flat, nqt, ncp, CMP_STRIDE, CMP_BLOCK - 1, "t5_bias_compressed")
    tz = _bias_call(rb_flat, 3, LANES, 1, 0, "t5_bias_toeplitz")

    ones_h = jnp.ones((HEAD_DIM,), F32)
    c_pad = jnp.pad(c, ((0, 8 - batch % 8 if batch % 8 else 0), (0, 0)))
    x2 = x.reshape(batch * seq, d)
    for layer in range(depth):
        mod = _ada_call(c_pad, ada_w[layer], ada_b[layer][None, :])[:batch]
        sh1, sc1, g1, sh2, sc2, g2 = [mod[:, i * d:(i + 1) * d][:, None, :] for i in range(N_MOD)]

        kg = nsa_k_gain[layer]
        col_gain = jnp.concatenate([
            jnp.tile(fox_q_gain[layer] * QSCALE, N_FOX_HEADS), jnp.tile(fox_k_gain[layer], N_FOX_HEADS),
            jnp.tile(ones_h, N_FOX_HEADS), jnp.tile(nsa_q_gain[layer] * QSCALE, N_NSA_HEADS),
            jnp.tile(kg[1], N_NSA_KV_HEADS), jnp.tile(kg[2], N_NSA_KV_HEADS),
            jnp.tile(ones_h, 2 * N_NSA_KV_HEADS)])[None, :]
        col_flag = jnp.concatenate([
            jnp.ones((2 * FOX_WIDTH,), F32), jnp.zeros((FOX_WIDTH,), F32), jnp.ones((NSA_WIDTH,), F32),
            jnp.ones((2 * NSA_KV_WIDTH,), F32), jnp.zeros((2 * NSA_KV_WIDTH,), F32)])[None, :]
        o16, o32 = _proj_call(x2, sc1, sh1, norm1_gain[layer][None, :], _permute_w_in(w_in[layer]),
                              col_gain, col_flag, seq)
        o16v = o16.reshape(batch, seq, W16)
        o32v = o32.reshape(batch, seq, W32)

        fb_pad = jnp.pad(fox_f_bias[layer], (0, LANES - N_FOX_HEADS))[None, :]
        cum = _cum_call(o32, fb_pad, batch, seq)
        o_fox = _fox_call(o16v, cum, batch, seq, tq_fox)

        w1 = nsa_cmp_w1[layer].reshape(2, CMP_BLOCK, HEAD_DIM, HEAD_DIM).astype(BF16)
        kcvc = _cmp_call(o32, nsa_cmp_pos[layer], w1, nsa_cmp_w2[layer].astype(BF16), kg[0][None, :],
                         batch, seq, ncp)
        o_nsa = _nsa_call(o16v, o32v, kcvc, bias_c, tz, selmt, batch, seq, ncp)

        x1 = _out_call(o_fox.reshape(batch * seq, FOX_WIDTH), o_nsa.reshape(batch * seq, NSA_WIDTH),
                       mix_out_gain[layer][None, :], w_out[layer].astype(BF16), x2, g1, seq)
        x2 = _ffn_call(x1, norm2_gain[layer][None, :], sc2, sh2, g2, ffn_w_gate[layer].astype(BF16),
                       ffn_w_up[layer].astype(BF16), ffn_w_down[layer].astype(BF16), seq)
    return x2.reshape(batch, seq, d)
```

```python
import functools
import math

import numpy as np
import jax
import jax.numpy as jnp
from jax import lax
from jax.experimental import pallas as pl
from jax.experimental.pallas import tpu as pltpu

HEAD_DIM = 128
N_FOX_HEADS = 8
N_NSA_HEADS = 8
N_NSA_KV_HEADS = 2
NSA_GROUP = N_NSA_HEADS // N_NSA_KV_HEADS
FOX_WIDTH = N_FOX_HEADS * HEAD_DIM
NSA_WIDTH = N_NSA_HEADS * HEAD_DIM
NSA_KV_WIDTH = N_NSA_KV_HEADS * HEAD_DIM
MIX_WIDTH = FOX_WIDTH + NSA_WIDTH
N_BRANCH = 3
CMP_BLOCK = 32
CMP_STRIDE = 16
SEL_BLOCK = 64
N_SEL = 8
WINDOW = 512
N_BUCKETS = 32
MAX_DISTANCE = 128
N_MOD = 6
SCALE = HEAD_DIM ** -0.5
LOG2E = math.log2(math.e)
QSCALE = SCALE * LOG2E
EPS = 1e-6
NEG = -1e30
FORCE = 1e6

LANES = 128
GROUP_ROWS = NSA_GROUP * LANES
VMEM_LIMIT = 56 * 1024 * 1024
MXU_COLS = 256
KEY_CHUNK = MXU_COLS
WIN_PAD = WINDOW + LANES
MASK_BIG = 2.0 ** 100
SEL, WIN = 0, 1

W16 = 3 * FOX_WIDTH + NSA_WIDTH + 4 * NSA_KV_WIDTH
W32 = 1024
COL_FF = 2 * NSA_KV_WIDTH
COL_NG = COL_FF + N_FOX_HEADS

F32 = jnp.float32
BF16 = jnp.bfloat16
NT_DIMS = (((1,), (1,)), ((), ()))


def _params(n_axes):
    return pltpu.CompilerParams(dimension_semantics=("arbitrary",) * n_axes,
                                vmem_limit_bytes=VMEM_LIMIT)


def _sigmoid(x):
    return 1.0 / (1.0 + jnp.exp(-x))


def _lane_tile(a, n):
    return jnp.concatenate([a] * n, axis=1)


BF16_SUBLANES = 16


def _cast_block(w, n_steps):
    rows, cols = w.shape
    assert rows % n_steps == 0, (w.shape, n_steps)
    per_step = rows // n_steps
    span = BF16_SUBLANES // math.gcd(BF16_SUBLANES, per_step)
    assert n_steps % span == 0, (w.shape, n_steps)
    return (per_step * span, cols), span


def _rms(x):
    return x * lax.rsqrt(jnp.mean(x * x, axis=-1, keepdims=True) + EPS)


def _ada_kernel(c_ref, w_ref, b_ref, o_ref):
    c = c_ref[...]
    s = (c * _sigmoid(c)).astype(BF16)
    o_ref[...] = jnp.dot(s, w_ref[...].astype(BF16), preferred_element_type=F32) + b_ref[...]


def _ada_call(c_pad, w, b):
    rows, d = c_pad.shape
    n = w.shape[1]
    tn = next(t for t in (1024, 768, 512, 384, 256, 128) if n % t == 0)
    return pl.pallas_call(
        _ada_kernel,
        grid=(n // tn,),
        in_specs=[pl.BlockSpec((rows, d), lambda j: (0, 0)),
                  pl.BlockSpec((d, tn), lambda j: (0, j)),
                  pl.BlockSpec((1, tn), lambda j: (0, j))],
        out_specs=pl.BlockSpec((rows, tn), lambda j: (0, j)),
        out_shape=jax.ShapeDtypeStruct((rows, n), F32),
        compiler_params=_params(1),
        name="adaln",
    )(c_pad, w, b)


def _proj_kernel(x_ref, sc_ref, sh_ref, g_ref, w_ref, gain_ref, flag_ref, o16_ref, o32_ref, h_scr, *, n16, tn):
    j = pl.program_id(1)

    @pl.when(j == 0)
    def _():
        h = _rms(x_ref[...]) * g_ref[...]
        h = h * (1.0 + sc_ref[0]) + sh_ref[0]
        h_scr[...] = h.astype(BF16)

    @pl.when(j < n16)
    def _():
        h = h_scr[...]
        for c in range(tn // MXU_COLS):
            acc = jnp.dot(h, w_ref[:, c * MXU_COLS:(c + 1) * MXU_COLS], preferred_element_type=F32)
            for g in range(MXU_COLS // LANES):
                cols = slice(c * MXU_COLS + g * LANES, c * MXU_COLS + (g + 1) * LANES)
                a = acc[:, g * LANES:(g + 1) * LANES]
                r = lax.rsqrt(jnp.mean(a * a, axis=-1, keepdims=True) + EPS)
                scale = jnp.where(flag_ref[:, cols] > 0.5, r, 1.0)
                o16_ref[:, cols] = (a * scale * gain_ref[:, cols]).astype(BF16)

    @pl.when(j >= n16)
    def _():
        o32_ref[...] = jnp.dot(h_scr[...], w_ref[...], preferred_element_type=F32)


def _proj_call(x2, sc, sh, gain1, w_perm, col_gain, col_flag, seq):
    t, d = x2.shape
    tm = min(1024, seq)
    tn = W32
    n16 = W16 // tn
    rows_per_batch = seq // tm
    return pl.pallas_call(
        functools.partial(_proj_kernel, n16=n16, tn=tn),
        grid=(t // tm, n16 + 1),
        in_specs=[pl.BlockSpec((tm, d), lambda i, j: (i, 0)),
                  pl.BlockSpec((1, 1, d), lambda i, j: (i // rows_per_batch, 0, 0)),
                  pl.BlockSpec((1, 1, d), lambda i, j: (i // rows_per_batch, 0, 0)),
                  pl.BlockSpec((1, d), lambda i, j: (0, 0)),
                  pl.BlockSpec((d, tn), lambda i, j: (0, j)),
                  pl.BlockSpec((1, tn), lambda i, j: (0, jnp.minimum(j, n16 - 1))),
                  pl.BlockSpec((1, tn), lambda i, j: (0, jnp.minimum(j, n16 - 1)))],
        out_specs=[pl.BlockSpec((tm, tn), lambda i, j: (i, jnp.minimum(j, n16 - 1))),
                   pl.BlockSpec((tm, tn), lambda i, j: (i, 0))],
        out_shape=[jax.ShapeDtypeStruct((t, W16), BF16),
                   jax.ShapeDtypeStruct((t, W32), F32)],
        scratch_shapes=[pltpu.VMEM((tm, d), BF16)],
        compiler_params=_params(2),
        name="in_proj",
    )(x2, sc, sh, gain1, w_perm, col_gain, col_flag)


def _cum_kernel(ff_ref, fb_ref, o_ref, *, seq):
    ri = lax.broadcasted_iota(jnp.int32, (LANES, LANES), 0)
    ci = lax.broadcasted_iota(jnp.int32, (LANES, LANES), 1)
    tri = jnp.where(ri >= ci, 1.0, 0.0).astype(BF16)
    carry = jnp.zeros((1, LANES), F32)
    for blk in range(seq // LANES):
        rows = slice(blk * LANES, (blk + 1) * LANES)
        x = ff_ref[0, rows, :] + fb_ref[...]
        lf = jnp.minimum(x, 0.0) - jnp.log(1.0 + jnp.exp(-jnp.abs(x)))
        hi = lf.astype(BF16)
        r1 = lf - hi.astype(F32)
        mid = r1.astype(BF16)
        lo = (r1 - mid.astype(F32)).astype(BF16)
        c = (jnp.dot(tri, hi, preferred_element_type=F32)
             + jnp.dot(tri, mid, preferred_element_type=F32)
             + jnp.dot(tri, lo, preferred_element_type=F32)) + carry
        o_ref[0, rows, :] = c
        carry = c[LANES - 1:LANES, :]


def _cum_call(o32, fb_pad, batch, seq):
    o32v = o32.reshape(batch, seq, W32)
    return pl.pallas_call(
        functools.partial(_cum_kernel, seq=seq),
        grid=(batch,),
        in_specs=[pl.BlockSpec((1, seq, LANES), lambda b: (b, 0, COL_FF // LANES)),
                  pl.BlockSpec((1, LANES), lambda b: (0, 0))],
        out_specs=pl.BlockSpec((1, seq, LANES), lambda b: (b, 0, 0)),
        out_shape=jax.ShapeDtypeStruct((batch, seq, LANES), F32),
        compiler_params=_params(1),
        name="fox_cumsum",
    )(o32v, fb_pad)


def _split3(c):
    hi = c.astype(BF16).astype(F32)
    r1 = c - hi
    mid = r1.astype(BF16).astype(F32)
    return hi, mid, (r1 - mid).astype(BF16).astype(F32)


def _fox_kernel(q_ref, k_ref, v_ref, cum_ref, o_ref, m_scr, acc_scr, s_scr, kaug_scr, vaug_scr, *, seq, tq, nh):
    hp = pl.program_id(1)
    lane = lax.broadcasted_iota(jnp.int32, (tq, LANES), 1)
    lane_s = lax.broadcasted_iota(jnp.int32, (seq, LANES), 1)
    causal = lax.broadcasted_iota(jnp.int32, (tq, tq), 1) <= lax.broadcasted_iota(jnp.int32, (tq, tq), 0)
    for j in range(nh):
        ck = jnp.sum(jnp.where(lane_s == hp * nh + j, cum_ref[0], 0.0), axis=1, keepdims=True) * LOG2E
        hi, mid, lo = _split3(ck)
        tail = jnp.where(lane_s < 3, 1.0, jnp.where(lane_s == 3, -hi, jnp.where(lane_s == 4, -mid,
                         jnp.where(lane_s == 5, -lo, 0.0))))
        for c in range(seq // tq):
            rows = slice(c * tq, (c + 1) * tq)
            kaug_scr[j, c, 0:HEAD_DIM, :] = k_ref[0, rows, j * HEAD_DIM:(j + 1) * HEAD_DIM].T
            kaug_scr[j, c, HEAD_DIM:, :] = tail[rows].T.astype(BF16)
        vaug_scr[j, :, 0:HEAD_DIM] = v_ref[0, :, j * HEAD_DIM:(j + 1) * HEAD_DIM]
        vaug_scr[j, :, HEAD_DIM:] = jnp.ones((seq, HEAD_DIM), BF16)

    def q_body(qi, carry):
        q0 = pl.multiple_of(qi * tq, tq)
        cum_t = cum_ref[0, pl.ds(q0, tq), :]
        qs = []
        for j in range(nh):
            cq = jnp.sum(jnp.where(lane == hp * nh + j, cum_t, 0.0), axis=1, keepdims=True) * LOG2E
            hi, mid, lo = _split3(cq)
            tail = jnp.where(lane == 0, hi, jnp.where(lane == 1, mid, jnp.where(lane == 2, lo,
                             jnp.where(lane < 6, 1.0, 0.0))))
            qs.append(jnp.concatenate([q_ref[0, pl.ds(q0, tq), j * HEAD_DIM:(j + 1) * HEAD_DIM],
                                       tail.astype(BF16)], axis=1))
            m_scr[j] = jnp.full((tq, LANES), NEG, F32)
            acc_scr[j] = jnp.zeros((tq, 2 * HEAD_DIM), F32)

        def scores(j, ki):
            return jnp.dot(qs[j], kaug_scr[j, ki], preferred_element_type=F32)

        def tile(ki, diagonal):
            k0 = pl.multiple_of(ki * tq, tq)
            for j in range(nh):
                s = s_scr[j]
                if diagonal:
                    s = jnp.where(causal, s, NEG)
                else:
                    s_scr[j] = scores(j, ki + 1)
                m_prev = m_scr[j]
                m_new = jnp.maximum(m_prev, jnp.max(s, axis=1, keepdims=True))
                alpha = jnp.exp2(m_prev - m_new)
                p = jnp.exp2(s - _lane_tile(m_new, tq // LANES))
                pv = jnp.dot(p.astype(BF16), vaug_scr[j, pl.ds(k0, tq), :], preferred_element_type=F32)
                acc_scr[j] = _lane_tile(alpha, 2) * acc_scr[j] + pv
                m_scr[j] = m_new

        def k_body(ki, c2):
            tile(ki, False)
            return c2

        for j in range(nh):
            s_scr[j] = scores(j, 0)
        lax.fori_loop(0, qi, k_body, 0)
        tile(qi, True)
        for j in range(nh):
            acc = acc_scr[j]
            o_ref[0, pl.ds(q0, tq), j * HEAD_DIM:(j + 1) * HEAD_DIM] = acc[:, 0:HEAD_DIM] / acc[:, HEAD_DIM:]
        return carry

    lax.fori_loop(0, seq // tq, q_body, 0)


def _fox_call(o16v, cum, batch, seq, tq):
    nh = 4
    groups = N_FOX_HEADS // nh
    w = nh * HEAD_DIM
    return pl.pallas_call(
        functools.partial(_fox_kernel, seq=seq, tq=tq, nh=nh),
        grid=(batch, groups),
        in_specs=[pl.BlockSpec((1, seq, w), lambda b, h: (b, 0, h)),
                  pl.BlockSpec((1, seq, w), lambda b, h: (b, 0, groups + h)),
                  pl.BlockSpec((1, seq, w), lambda b, h: (b, 0, 2 * groups + h)),
                  pl.BlockSpec((1, seq, LANES), lambda b, h: (b, 0, 0))],
        out_specs=pl.BlockSpec((1, seq, w), lambda b, h: (b, 0, h)),
        out_shape=jax.ShapeDtypeStruct((batch, seq, FOX_WIDTH), F32),
        scratch_shapes=[pltpu.VMEM((nh, tq, LANES), F32),
                        pltpu.VMEM((nh, tq, 2 * HEAD_DIM), F32),
                        pltpu.VMEM((nh, tq, tq), F32),
                        pltpu.VMEM((nh, seq // tq, 2 * HEAD_DIM, tq), BF16),
                        pltpu.VMEM((nh, seq, 2 * HEAD_DIM), BF16)],
        compiler_params=_params(2),
        name="fox_attention",
    )(o16v, o16v, o16v, cum)


def _cmp_kernel(x_ref, pos_ref, w1_ref, w2_ref, gain_ref, o_ref, xs_scr, *, seq, ncp):
    kv = pl.program_id(1)
    rows = xs_scr.shape[0]
    xs_scr[0:seq, :] = x_ref[0]
    xs_scr[seq:rows, :] = jnp.zeros((rows - seq, LANES), F32)
    acc = jnp.zeros((ncp, HEAD_DIM), F32)
    for l in range(CMP_BLOCK):
        xl = xs_scr[pl.ds(l, ncp, stride=CMP_STRIDE), :] + pos_ref[0, l:l + 1, :]
        acc = acc + jnp.dot(xl.astype(BF16), w1_ref[0, l], preferred_element_type=F32)
    hmid = acc * _sigmoid(acc)
    y = jnp.dot(hmid.astype(BF16), w2_ref[0], preferred_element_type=F32)
    yk = _rms(y) * gain_ref[...]
    o_ref[0, 0, 0] = jnp.where(kv == 0, yk, y).astype(BF16)


def _cmp_call(o32, pos, w1, w2, gain, batch, seq, ncp):
    o32v = o32.reshape(batch, seq, W32)
    hkv = N_NSA_KV_HEADS
    return pl.pallas_call(
        functools.partial(_cmp_kernel, seq=seq, ncp=ncp),
        grid=(batch, 2, hkv),
        in_specs=[pl.BlockSpec((1, seq, HEAD_DIM), lambda b, kv, h: (b, 0, kv * hkv + h)),
                  pl.BlockSpec((1, CMP_BLOCK, HEAD_DIM), lambda b, kv, h: (kv, 0, 0)),
                  pl.BlockSpec((1, CMP_BLOCK, HEAD_DIM, HEAD_DIM), lambda b, kv, h: (kv, 0, 0, 0)),
                  pl.BlockSpec((1, HEAD_DIM, HEAD_DIM), lambda b, kv, h: (kv, 0, 0)),
                  pl.BlockSpec((1, HEAD_DIM), lambda b, kv, h: (0, 0))],
        out_specs=pl.BlockSpec((1, 1, 1, ncp, HEAD_DIM), lambda b, kv, h: (b, kv, h, 0, 0)),
        out_shape=jax.ShapeDtypeStruct((batch, 2, hkv, ncp, HEAD_DIM), BF16),
        scratch_shapes=[pltpu.VMEM((CMP_STRIDE * ncp + CMP_BLOCK, LANES), F32)],
        compiler_params=_params(3),
        name="nsa_compress",
    )(o32v, pos, w1, w2, gain)


def _bias_kernel(rb_ref, o_ref, *, width, key_stride, key_offset):
    v = pl.program_id(0)
    i = lax.broadcasted_iota(jnp.int32, (LANES, width), 0)
    j = lax.broadcasted_iota(jnp.int32, (LANES, width), 1)
    d = v * LANES + i - (key_stride * j + key_offset)
    n = jnp.maximum(d, 0)
    max_exact = N_BUCKETS // 2
    nf = jnp.maximum(n, 1).astype(F32)
    large = max_exact + (jnp.log(nf / max_exact) / math.log(MAX_DISTANCE / max_exact)
                         * (N_BUCKETS - max_exact)).astype(jnp.int32)
    large = jnp.minimum(large, N_BUCKETS - 1)
    bkt = jnp.where(n < max_exact, n, large)
    vals = [jnp.zeros((LANES, width), F32) for _ in range(N_NSA_HEADS)]
    for bk in range(N_BUCKETS):
        hit = bkt == bk
        for h in range(N_NSA_HEADS):
            vals[h] = jnp.where(hit, rb_ref[bk * N_NSA_HEADS + h] * LOG2E, vals[h])
    for h in range(N_NSA_HEADS):
        g = h % NSA_GROUP
        o_ref[0, h // NSA_GROUP, g * LANES:(g + 1) * LANES, :] = vals[h]


def _bias_call(rb_flat, n_variants, width, key_stride, key_offset, name):
    return pl.pallas_call(
        functools.partial(_bias_kernel, width=width, key_stride=key_stride, key_offset=key_offset),
        grid=(n_variants,),
        in_specs=[pl.BlockSpec(memory_space=pltpu.SMEM)],
        out_specs=pl.BlockSpec((1, N_NSA_KV_HEADS, GROUP_ROWS, width), lambda v: (v, 0, 0, 0)),
        out_shape=jax.ShapeDtypeStruct((n_variants, N_NSA_KV_HEADS, GROUP_ROWS, width), F32),
        compiler_params=_params(1),
        name=name,
    )(rb_flat)


def _tile4(a):
    return jnp.concatenate([a] * NSA_GROUP, axis=0)


def _nsa_kernel(q_ref, ks_ref, kw_ref, vs_ref, vw_ref, g_ref, kcvc_ref, bc_ref, tz_ref, selmt_ref,
                wa_ref, wb_ref, wc_ref, wd_ref, o_ref, wa_out, wb_out, wc_out, wd_out,
                qg_scr, m_scr, acc_scr, o_scr, s_scr, ksa_scr, vsa_scr, kwp_scr, vwa_scr, *, seq, ncp):
    qt = pl.program_id(1)
    q0 = qt * LANES
    n_slc = seq // SEL_BLOCK
    top_n = min(N_SEL, n_slc)
    nkv = N_NSA_KV_HEADS
    for src, dst in ((wa_ref, wa_out), (wb_ref, wb_out), (wc_ref, wc_out), (wd_ref, wd_out)):
        dst[...] = src[...].astype(BF16)
    ri = lax.broadcasted_iota(jnp.int32, (LANES, LANES), 0)
    ci = lax.broadcasted_iota(jnp.int32, (LANES, LANES), 1)
    rk = lax.broadcasted_iota(jnp.int32, (LANES, KEY_CHUNK), 0)
    ck = lax.broadcasted_iota(jnp.int32, (LANES, KEY_CHUNK), 1)
    eye = jnp.where(ri == ci, 1.0, 0.0).astype(BF16)
    gates = _sigmoid(g_ref[0])

    def gate_col(hk, br):
        cols = []
        for g in range(NSA_GROUP):
            c = COL_NG % LANES + (hk * NSA_GROUP + g) * N_BRANCH + br
            cols.append(gates[:, c:c + 1])
        return jnp.concatenate(cols, axis=0)

    @pl.when(qt == 0)
    def _():
        ones = jnp.ones((seq, HEAD_DIM), BF16)
        blk_row = lax.broadcasted_iota(jnp.int32, (LANES, KEY_CHUNK), 0)
        for hk in range(nkv):
            hc = slice(hk * HEAD_DIM, (hk + 1) * HEAD_DIM)
            for c in range(seq // KEY_CHUNK):
                rows = slice(c * KEY_CHUNK, (c + 1) * KEY_CHUNK)
                ksa_scr[hk, c, 0:HEAD_DIM, :] = ks_ref[0, rows, hc].T
                ksa_scr[hk, c, HEAD_DIM:, :] = jnp.where((c * KEY_CHUNK + ck) // SEL_BLOCK == blk_row,
                                                         -MASK_BIG, 0.0).astype(BF16)
            for t in range(WIN_PAD // LANES):
                kwp_scr[hk, t] = jnp.zeros((HEAD_DIM, LANES), BF16)
            for t in range(seq // LANES):
                kwp_scr[hk, WIN_PAD // LANES + t] = kw_ref[0, t * LANES:(t + 1) * LANES, hc].T
            vsa_scr[hk, :, 0:HEAD_DIM] = vs_ref[0, :, hc]
            vsa_scr[hk, :, HEAD_DIM:] = ones
            vwa_scr[hk, 0:WIN_PAD, :] = jnp.zeros((WIN_PAD, 2 * HEAD_DIM), BF16)
            vwa_scr[hk, WIN_PAD:, 0:HEAD_DIM] = vw_ref[0, :, hc]
            vwa_scr[hk, WIN_PAD:, HEAD_DIM:] = ones

    def reset(br):
        m_scr[br] = jnp.full(m_scr.shape[1:], NEG, F32)
        acc_scr[br] = jnp.zeros(acc_scr.shape[1:], F32)

    def online_update(br, hk, s, vaug):
        m_prev = m_scr[br, hk]
        m_new = jnp.maximum(m_prev, jnp.max(s, axis=1, keepdims=True))
        alpha = jnp.exp2(m_prev - m_new)
        p = jnp.exp2(s - _lane_tile(m_new, s.shape[1] // LANES))
        pv = jnp.dot(p.astype(BF16), vaug, preferred_element_type=F32)
        acc_scr[br, hk] = _lane_tile(alpha, 2) * acc_scr[br, hk] + pv
        m_scr[br, hk] = m_new

    def finish(br, hk):
        acc = acc_scr[br, hk]
        return acc[:, 0:HEAD_DIM] / acc[:, HEAD_DIM:]

    def chunk_bias(hk, ta, tb):
        return jnp.concatenate([tz_ref[jnp.clip(ta, 0, 2), hk], tz_ref[jnp.clip(tb, 0, 2), hk]], axis=1)

    for hk in range(nkv):
        for g in range(NSA_GROUP):
            h = hk * NSA_GROUP + g
            qg_scr[hk, g * LANES:(g + 1) * LANES, 0:HEAD_DIM] = q_ref[0, :, h * HEAD_DIM:(h + 1) * HEAD_DIM]

    for hk in range(nkv):
        qg = qg_scr[hk, :, 0:HEAD_DIM]

        kc = kcvc_ref[0, 0, hk]
        vc = kcvc_ref[0, 1, hk]
        s = lax.dot_general(qg, kc, NT_DIMS, preferred_element_type=F32) + bc_ref[0, hk]
        rc = lax.broadcasted_iota(jnp.int32, (LANES, ncp), 0)
        cc = lax.broadcasted_iota(jnp.int32, (LANES, ncp), 1)
        valid_c = _tile4(jnp.where(q0 + rc - (CMP_STRIDE * cc + CMP_BLOCK - 1) >= 0, 1.0, 0.0)) > 0.5
        s = jnp.where(valid_c, s, NEG)
        p = jnp.where(valid_c, jnp.exp2(s - jnp.max(s, axis=1, keepdims=True)), 0.0)
        l = jnp.sum(p, axis=1, keepdims=True)
        p = p / jnp.where(l > 0.0, l, 1.0)
        o_scr[hk] = gate_col(hk, 0) * jnp.dot(p.astype(BF16), vc, preferred_element_type=F32)

        psum = p[0:LANES]
        for g in range(1, NSA_GROUP):
            psum = psum + p[g * LANES:(g + 1) * LANES]
        p_hi = psum.astype(BF16)
        p_lo = (psum - p_hi.astype(F32)).astype(BF16)
        selmt = selmt_ref[...]
        imp = (lax.dot_general(selmt, p_hi, NT_DIMS, preferred_element_type=F32)
               + lax.dot_general(selmt, p_lo, NT_DIMS, preferred_element_type=F32))
        imp = imp[0:n_slc]
        blk = lax.broadcasted_iota(jnp.int32, (n_slc, LANES), 0)
        cur = (q0 + lax.broadcasted_iota(jnp.int32, (n_slc, LANES), 1)) // SEL_BLOCK
        forced = (blk == 0) | (blk == cur) | (blk == cur - 1)
        imp = jnp.where(forced, FORCE, imp)
        imp = jnp.where(blk <= cur, imp, -jnp.inf)
        rank = jnp.zeros((n_slc, LANES), F32)
        for j in range(n_slc):
            row = imp[j:j + 1, :]
            beats = jnp.where(row > imp, 1.0, jnp.where(row == imp, jnp.where(blk > j, 1.0, 0.0), 0.0))
            rank = rank + beats
        sel_t = jnp.where(rank < top_n, jnp.where(imp > -jnp.inf, 1.0, 0.0), 0.0)
        if n_slc < LANES:
            sel_t = jnp.concatenate([sel_t, jnp.zeros((LANES - n_slc, LANES), F32)], axis=0)
        sel_q = lax.dot_general(eye, sel_t.astype(BF16), NT_DIMS, preferred_element_type=F32)
        not_sel = jnp.where(ci < n_slc, 1.0 - sel_q, 0.0).astype(BF16)
        for g in range(NSA_GROUP):
            qg_scr[hk, g * LANES:(g + 1) * LANES, HEAD_DIM:] = not_sel

    reset(WIN)
    for off, n_tiles in ((1, 2), (3, 2), (5, 2)):
        width = n_tiles * LANES
        kstart = (qt - off) * LANES
        pos = kstart + lax.broadcasted_iota(jnp.int32, (LANES, width), 1)
        d = (q0 + lax.broadcasted_iota(jnp.int32, (LANES, width), 0)) - pos
        ok = (d >= 0) & (d < WINDOW) & (pos >= 0)
        madd = _tile4(jnp.where(ok, 0.0, -MASK_BIG))
        t0 = qt - off + WIN_PAD // LANES
        p0 = pl.multiple_of(kstart + WIN_PAD, LANES)
        for hk in range(nkv):
            k_t = jnp.concatenate([kwp_scr[hk, t0 + t] for t in range(n_tiles)], axis=1)
            bias = jnp.concatenate([tz_ref[min(off - t, 2), hk] for t in range(n_tiles)], axis=1)
            s = jnp.dot(qg_scr[hk, :, 0:HEAD_DIM], k_t, preferred_element_type=F32) + bias + madd
            online_update(WIN, hk, s, vwa_scr[hk, pl.ds(p0, width), :])

    reset(SEL)

    def sel_scores(hk, c):
        ta = qt - 2 * c
        return jnp.dot(qg_scr[hk], ksa_scr[hk, c], preferred_element_type=F32) + chunk_bias(hk, ta, ta - 1)

    def sel_chunk(c, last):
        k0 = pl.multiple_of(c * KEY_CHUNK, KEY_CHUNK)
        for hk in range(nkv):
            s = s_scr[hk]
            if last:
                s = jnp.where(_tile4(jnp.where((k0 + ck) <= (q0 + rk), 1.0, 0.0)) > 0.5, s, -MASK_BIG)
            else:
                s_scr[hk] = sel_scores(hk, c + 1)
            online_update(SEL, hk, s, vsa_scr[hk, pl.ds(k0, KEY_CHUNK), :])

    def sel_body(c, carry):
        sel_chunk(c, False)
        return carry

    n_chunks = (qt + 2) // 2
    for hk in range(nkv):
        s_scr[hk] = sel_scores(hk, 0)
    lax.fori_loop(0, n_chunks - 1, sel_body, 0)
    sel_chunk(n_chunks - 1, True)
    for hk in range(nkv):
        o = (o_scr[hk] + gate_col(hk, 1) * finish(SEL, hk)) + gate_col(hk, 2) * finish(WIN, hk)
        for g in range(NSA_GROUP):
            h = hk * NSA_GROUP + g
            o_ref[0, :, h * HEAD_DIM:(h + 1) * HEAD_DIM] = o[g * LANES:(g + 1) * LANES]


def _nsa_call(o16v, o32v, kcvc, bias_c, tz, selmt, weights_f32, batch, seq, ncp):
    nqt = seq // LANES
    wspecs = []
    for w in weights_f32:
        blk, span = _cast_block(w, batch * nqt)
        wspecs.append(pl.BlockSpec(blk, functools.partial(lambda b, t, span: ((b * nqt + t) // span, 0), span=span)))
    kvw = NSA_KV_WIDTH
    nkv = N_NSA_KV_HEADS
    base = (3 * FOX_WIDTH + NSA_WIDTH) // kvw
    return pl.pallas_call(
        functools.partial(_nsa_kernel, seq=seq, ncp=ncp),
        grid=(batch, nqt),
        in_specs=[pl.BlockSpec((1, LANES, NSA_WIDTH), lambda b, t: (b, t, 3 * FOX_WIDTH // NSA_WIDTH)),
                  pl.BlockSpec((1, seq, kvw), lambda b, t: (b, 0, base)),
                  pl.BlockSpec((1, seq, kvw), lambda b, t: (b, 0, base + 1)),
                  pl.BlockSpec((1, seq, kvw), lambda b, t: (b, 0, base + 2)),
                  pl.BlockSpec((1, seq, kvw), lambda b, t: (b, 0, base + 3)),
                  pl.BlockSpec((1, LANES, LANES), lambda b, t: (b, t, COL_NG // LANES)),
                  pl.BlockSpec((1, 2, N_NSA_KV_HEADS, ncp, HEAD_DIM), lambda b, t: (b, 0, 0, 0, 0)),
                  pl.BlockSpec((1, N_NSA_KV_HEADS, GROUP_ROWS, ncp), lambda b, t: (t, 0, 0, 0)),
                  pl.BlockSpec((3, N_NSA_KV_HEADS, GROUP_ROWS, LANES), lambda b, t: (0, 0, 0, 0)),
                  pl.BlockSpec((LANES, ncp), lambda b, t: (0, 0))] + wspecs,
        out_specs=[pl.BlockSpec((1, LANES, NSA_WIDTH), lambda b, t: (b, t, 0))] + wspecs,
        out_shape=[jax.ShapeDtypeStruct((batch, seq, NSA_WIDTH), F32)]
        + [jax.ShapeDtypeStruct(w.shape, BF16) for w in weights_f32],
        scratch_shapes=[pltpu.VMEM((nkv, GROUP_ROWS, 2 * HEAD_DIM), BF16),
                        pltpu.VMEM((2, nkv, GROUP_ROWS, LANES), F32),
                        pltpu.VMEM((2, nkv, GROUP_ROWS, 2 * HEAD_DIM), F32),
                        pltpu.VMEM((nkv, GROUP_ROWS, HEAD_DIM), F32),
                        pltpu.VMEM((nkv, GROUP_ROWS, KEY_CHUNK), F32),
                        pltpu.VMEM((nkv, seq // KEY_CHUNK, 2 * HEAD_DIM, KEY_CHUNK), BF16),
                        pltpu.VMEM((nkv, seq, 2 * HEAD_DIM), BF16),
                        pltpu.VMEM((nkv, (seq + WIN_PAD) // LANES, HEAD_DIM, LANES), BF16),
                        pltpu.VMEM((nkv, seq + WIN_PAD, 2 * HEAD_DIM), BF16)],
        compiler_params=_params(2),
        name="nsa_attention",
    )(o16v, o16v, o16v, o16v, o16v, o32v, kcvc, bias_c, tz, selmt, *weights_f32)


def _out_kernel(of_ref, on_ref, gain_ref, w_ref, x_ref, g_ref, o_ref, *, tm, n_chains):
    rows_per_chain = tm // n_chains
    for c in range(n_chains):
        rows = slice(c * rows_per_chain, (c + 1) * rows_per_chain)
        y = jnp.concatenate([(_rms(of_ref[rows, :]) * gain_ref[:, 0:FOX_WIDTH]).astype(BF16),
                             (_rms(on_ref[rows, :]) * gain_ref[:, FOX_WIDTH:MIX_WIDTH]).astype(BF16)], axis=1)
        acc = jnp.dot(y, w_ref[...], preferred_element_type=F32)
        o_ref[rows, :] = x_ref[rows, :] + g_ref[0] * acc


def _out_call(o_fox, o_nsa, gain, w_out, x2, g1, seq):
    t, d = x2.shape
    tm = min(512, seq)
    rows_per_batch = seq // tm
    return pl.pallas_call(
        functools.partial(_out_kernel, tm=tm, n_chains=2),
        grid=(t // tm,),
        in_specs=[pl.BlockSpec((tm, FOX_WIDTH), lambda i: (i, 0)),
                  pl.BlockSpec((tm, NSA_WIDTH), lambda i: (i, 0)),
                  pl.BlockSpec((1, MIX_WIDTH), lambda i: (0, 0)),
                  pl.BlockSpec((MIX_WIDTH, d), lambda i: (0, 0)),
                  pl.BlockSpec((tm, d), lambda i: (i, 0)),
                  pl.BlockSpec((1, 1, d), lambda i: (i // rows_per_batch, 0, 0))],
        out_specs=pl.BlockSpec((tm, d), lambda i: (i, 0)),
        out_shape=jax.ShapeDtypeStruct((t, d), F32),
        compiler_params=_params(1),
        name="out_proj",
    )(o_fox, o_nsa, gain, w_out, x2, g1)


def _ffn_kernel(x_ref, gain_ref, sc_ref, sh_ref, g_ref, wg_ref, wu_ref, wd_ref, o_ref, h_scr, acc_scr):
    f = pl.program_id(1)

    @pl.when(f == 0)
    def _():
        h = _rms(x_ref[...]) * gain_ref[...]
        h = h * (1.0 + sc_ref[0]) + sh_ref[0]
        h_scr[...] = h.astype(BF16)
        acc_scr[...] = jnp.zeros(acc_scr.shape, F32)

    h = h_scr[...]
    a = jnp.dot(h, wg_ref[...], preferred_element_type=F32)
    u = jnp.dot(h, wu_ref[...], preferred_element_type=F32)
    t = (a * _sigmoid(a)) * u
    acc_scr[...] += jnp.dot(t.astype(BF16), wd_ref[...], preferred_element_type=F32)

    @pl.when(f == pl.num_programs(1) - 1)
    def _():
        o_ref[...] = x_ref[...] + g_ref[0] * acc_scr[...]


def _ffn_call(x1, gain2, sc, sh, g2, wg, wu, wd, seq):
    t, d = x1.shape
    dff = wg.shape[1]
    tm = min(512, seq)
    tf = 512 if dff % 512 == 0 else dff
    rows_per_batch = seq // tm
    return pl.pallas_call(
        _ffn_kernel,
        grid=(t // tm, dff // tf),
        in_specs=[pl.BlockSpec((tm, d), lambda i, f: (i, 0)),
                  pl.BlockSpec((1, d), lambda i, f: (0, 0)),
                  pl.BlockSpec((1, 1, d), lambda i, f: (i // rows_per_batch, 0, 0)),
                  pl.BlockSpec((1, 1, d), lambda i, f: (i // rows_per_batch, 0, 0)),
                  pl.BlockSpec((1, 1, d), lambda i, f: (i // rows_per_batch, 0, 0)),
                  pl.BlockSpec((d, tf), lambda i, f: (0, f)),
                  pl.BlockSpec((d, tf), lambda i, f: (0, f)),
                  pl.BlockSpec((tf, d), lambda i, f: (f, 0))],
        out_specs=pl.BlockSpec((tm, d), lambda i, f: (i, 0)),
        out_shape=jax.ShapeDtypeStruct((t, d), F32),
        scratch_shapes=[pltpu.VMEM((tm, d), BF16), pltpu.VMEM((tm, d), F32)],
        compiler_params=_params(2),
        name="swiglu_ffn",
    )(x1, gain2, sc, sh, g2, wg, wu, wd)


def _selection_matrix_t(ncp, n_slc):
    r, q = SEL_BLOCK // CMP_STRIDE, CMP_BLOCK // CMP_STRIDE
    m = np.zeros((LANES, ncp), np.float32)
    for j in range(n_slc):
        for a in range(r):
            for b in range(q):
                c = r * j + a - b
                if 0 <= c < ncp:
                    m[j, c] += 1.0
    return m


def _w_in_column_moves():
    o = 0
    start = {}
    for name, width in (("fq", FOX_WIDTH), ("fk", FOX_WIDTH), ("fv", FOX_WIDTH), ("ff", N_FOX_HEADS),
                        ("nq", NSA_WIDTH), ("nk", N_BRANCH * NSA_KV_WIDTH), ("nv", N_BRANCH * NSA_KV_WIDTH),
                        ("ng", N_BRANCH * N_NSA_HEADS)):
        start[name] = o
        o += width
    kvw = NSA_KV_WIDTH
    return [(start["fq"], 3 * FOX_WIDTH), (start["nq"], NSA_WIDTH),
            (start["nk"] + kvw, 2 * kvw), (start["nv"] + kvw, 2 * kvw),
            (start["nk"], kvw), (start["nv"], kvw),
            (start["ff"], N_FOX_HEADS), (start["ng"], N_BRANCH * N_NSA_HEADS)]


def _repack_kernel(w_ref, o_ref):
    o = 0
    for src, width in _w_in_column_moves():
        o_ref[:, o:o + width] = w_ref[:, src:src + width].astype(BF16)
        o += width
    o_ref[:, o:] = jnp.zeros((o_ref.shape[0], o_ref.shape[1] - o), BF16)


def _permute_w_in(w):
    d, n = w.shape
    tr = 256 if d % 256 == 0 else d
    return pl.pallas_call(
        _repack_kernel,
        grid=(d // tr,),
        in_specs=[pl.BlockSpec((tr, n), lambda i: (i, 0))],
        out_specs=pl.BlockSpec((tr, W16 + W32), lambda i: (i, 0)),
        out_shape=jax.ShapeDtypeStruct((d, W16 + W32), BF16),
        compiler_params=_params(1),
        name="w_in_repack",
    )(w)


def kernel(x, c, ada_w, ada_b, norm1_gain, norm2_gain, w_in, fox_f_bias, fox_q_gain, fox_k_gain, nsa_q_gain,
           nsa_k_gain, nsa_cmp_pos, nsa_cmp_w1, nsa_cmp_w2, rel_bias, mix_out_gain, w_out, ffn_w_gate, ffn_w_up,
           ffn_w_down):
    batch, seq, d = x.shape
    assert seq % KEY_CHUNK == 0 and seq >= WINDOW and d % LANES == 0 and seq // SEL_BLOCK <= LANES
    depth = ada_w.shape[0]
    nqt = seq // LANES
    ncp = -(-(seq // CMP_STRIDE) // LANES) * LANES
    tq_fox = min(256, seq)

    selmt = jnp.asarray(_selection_matrix_t(ncp, seq // SEL_BLOCK), BF16)
    rb_flat = rel_bias.reshape(-1)
    bias_c = _bias_call(rb_flat, nqt, ncp, CMP_STRIDE, CMP_BLOCK - 1, "t5_bias_compressed")
    tz = _bias_call(rb_flat, 3, LANES, 1, 0, "t5_bias_toeplitz")

    ones_h = jnp.ones((HEAD_DIM,), F32)
    c_pad = jnp.pad(c, ((0, 8 - batch % 8 if batch % 8 else 0), (0, 0)))
    x2 = x.reshape(batch * seq, d)
    for layer in range(depth):
        mod = _ada_call(c_pad, ada_w[layer], ada_b[layer][None, :])[:batch]
        sh1, sc1, g1, sh2, sc2, g2 = [mod[:, i * d:(i + 1) * d][:, None, :] for i in range(N_MOD)]

        kg = nsa_k_gain[layer]
        col_gain = jnp.concatenate([
            jnp.tile(fox_q_gain[layer] * QSCALE, N_FOX_HEADS), jnp.tile(fox_k_gain[layer], N_FOX_HEADS),
            jnp.tile(ones_h, N_FOX_HEADS), jnp.tile(nsa_q_gain[layer] * QSCALE, N_NSA_HEADS),
            jnp.tile(kg[1], N_NSA_KV_HEADS), jnp.tile(kg[2], N_NSA_KV_HEADS),
            jnp.tile(ones_h, 2 * N_NSA_KV_HEADS)])[None, :]
        col_flag = jnp.concatenate([
            jnp.ones((2 * FOX_WIDTH,), F32), jnp.zeros((FOX_WIDTH,), F32), jnp.ones((NSA_WIDTH,), F32),
            jnp.ones((2 * NSA_KV_WIDTH,), F32), jnp.zeros((2 * NSA_KV_WIDTH,), F32)])[None, :]
        o16, o32 = _proj_call(x2, sc1, sh1, norm1_gain[layer][None, :], _permute_w_in(w_in[layer]),
                              col_gain, col_flag, seq)
        o16v = o16.reshape(batch, seq, W16)
        o32v = o32.reshape(batch, seq, W32)

        fb_pad = jnp.pad(fox_f_bias[layer], (0, LANES - N_FOX_HEADS))[None, :]
        cum = _cum_call(o32, fb_pad, batch, seq)
        o_fox = _fox_call(o16v, cum, batch, seq, tq_fox)

        w1 = nsa_cmp_w1[layer].reshape(2, CMP_BLOCK, HEAD_DIM, HEAD_DIM).astype(BF16)
        kcvc = _cmp_call(o32, nsa_cmp_pos[layer], w1, nsa_cmp_w2[layer].astype(BF16), kg[0][None, :],
                         batch, seq, ncp)
        o_nsa, wg16, wu16, wo16, wd16 = _nsa_call(
            o16v, o32v, kcvc, bias_c, tz, selmt,
            (ffn_w_gate[layer], ffn_w_up[layer], w_out[layer], ffn_w_down[layer]), batch, seq, ncp)

        x1 = _out_call(o_fox.reshape(batch * seq, FOX_WIDTH), o_nsa.reshape(batch * seq, NSA_WIDTH),
                       mix_out_gain[layer][None, :], wo16, x2, g1, seq)
        x2 = _ffn_call(x1, norm2_gain[layer][None, :], sc2, sh2, g2, wg16, wu16, wd16, seq)
    return x2.reshape(batch, seq, d)
```

```python
import functools
import math

import numpy as np
import jax
import jax.numpy as jnp
from jax import lax
from jax.experimental import pallas as pl
from jax.experimental.pallas import tpu as pltpu

HEAD_DIM = 128
N_FOX_HEADS = 8
N_NSA_HEADS = 8
N_NSA_KV_HEADS = 2
NSA_GROUP = N_NSA_HEADS // N_NSA_KV_HEADS
FOX_WIDTH = N_FOX_HEADS * HEAD_DIM
NSA_WIDTH = N_NSA_HEADS * HEAD_DIM
NSA_KV_WIDTH = N_NSA_KV_HEADS * HEAD_DIM
MIX_WIDTH = FOX_WIDTH + NSA_WIDTH
N_BRANCH = 3
CMP_BLOCK = 32
CMP_STRIDE = 16
SEL_BLOCK = 64
N_SEL = 8
WINDOW = 512
N_BUCKETS = 32
MAX_DISTANCE = 128
N_MOD = 6
SCALE = HEAD_DIM ** -0.5
LOG2E = math.log2(math.e)
QSCALE = SCALE * LOG2E
EPS = 1e-6
NEG = -1e30
FORCE = 1e6

LANES = 128
GROUP_ROWS = NSA_GROUP * LANES
VMEM_LIMIT = 56 * 1024 * 1024
MXU_COLS = 256
KEY_CHUNK = MXU_COLS
WIN_PAD = WINDOW + LANES
MASK_BIG = 2.0 ** 100
SEL, WIN = 0, 1

W16 = 3 * FOX_WIDTH + NSA_WIDTH + 4 * NSA_KV_WIDTH
W32 = 5 * LANES
COL_FF = 2 * NSA_KV_WIDTH
COL_NG = COL_FF + N_FOX_HEADS

F32 = jnp.float32
BF16 = jnp.bfloat16
NT_DIMS = (((1,), (1,)), ((), ()))


def _params(n_axes):
    return pltpu.CompilerParams(dimension_semantics=("arbitrary",) * n_axes,
                                vmem_limit_bytes=VMEM_LIMIT)


def _sigmoid(x):
    return 1.0 / (1.0 + jnp.exp(-x))


def _lane_tile(a, n):
    return jnp.concatenate([a] * n, axis=1)


BF16_SUBLANES = 16


def _cast_block(w, n_steps):
    rows, cols = w.shape
    assert rows % n_steps == 0, (w.shape, n_steps)
    per_step = rows // n_steps
    span = BF16_SUBLANES // math.gcd(BF16_SUBLANES, per_step)
    assert n_steps % span == 0, (w.shape, n_steps)
    return (per_step * span, cols), span


def _rms(x):
    return x * lax.rsqrt(jnp.mean(x * x, axis=-1, keepdims=True) + EPS)


def _ada_kernel(c_ref, w_ref, b_ref, o_ref):
    c = c_ref[...]
    s = (c * _sigmoid(c)).astype(BF16)
    o_ref[...] = jnp.dot(s, w_ref[...].astype(BF16), preferred_element_type=F32) + b_ref[...]


def _ada_call(c_pad, w, b):
    rows, d = c_pad.shape
    n = w.shape[1]
    tn = next(t for t in (1024, 768, 512, 384, 256, 128) if n % t == 0)
    return pl.pallas_call(
        _ada_kernel,
        grid=(n // tn,),
        in_specs=[pl.BlockSpec((rows, d), lambda j: (0, 0)),
                  pl.BlockSpec((d, tn), lambda j: (0, j)),
                  pl.BlockSpec((1, tn), lambda j: (0, j))],
        out_specs=pl.BlockSpec((rows, tn), lambda j: (0, j)),
        out_shape=jax.ShapeDtypeStruct((rows, n), F32),
        compiler_params=_params(1),
        name="adaln",
    )(c_pad, w, b)


def _proj_kernel(x_ref, sc_ref, sh_ref, g_ref, w16_ref, w32_ref, gain_ref, flag_ref, o16_ref, o32_ref, h_scr, *, tn):
    j = pl.program_id(1)

    @pl.when(j == 0)
    def _():
        h = _rms(x_ref[...]) * g_ref[...]
        h = h * (1.0 + sc_ref[0]) + sh_ref[0]
        h_scr[...] = h.astype(BF16)

    h = h_scr[...]
    for c in range(tn // MXU_COLS):
        acc = jnp.dot(h, w16_ref[:, c * MXU_COLS:(c + 1) * MXU_COLS], preferred_element_type=F32)
        for g in range(MXU_COLS // LANES):
            cols = slice(c * MXU_COLS + g * LANES, c * MXU_COLS + (g + 1) * LANES)
            a = acc[:, g * LANES:(g + 1) * LANES]
            r = lax.rsqrt(jnp.mean(a * a, axis=-1, keepdims=True) + EPS)
            scale = jnp.where(flag_ref[:, cols] > 0.5, r, 1.0)
            o16_ref[:, cols] = (a * scale * gain_ref[:, cols]).astype(BF16)

    @pl.when(j == pl.num_programs(1) - 1)
    def _():
        o32_ref[...] = jnp.dot(h_scr[...], w32_ref[...], preferred_element_type=F32)


def _proj_call(x2, sc, sh, gain1, w16, w32, col_gain, col_flag, seq):
    t, d = x2.shape
    tm = min(1024, seq)
    tn = 1024
    rows_per_batch = seq // tm
    return pl.pallas_call(
        functools.partial(_proj_kernel, tn=tn),
        grid=(t // tm, W16 // tn),
        in_specs=[pl.BlockSpec((tm, d), lambda i, j: (i, 0)),
                  pl.BlockSpec((1, 1, d), lambda i, j: (i // rows_per_batch, 0, 0)),
                  pl.BlockSpec((1, 1, d), lambda i, j: (i // rows_per_batch, 0, 0)),
                  pl.BlockSpec((1, d), lambda i, j: (0, 0)),
                  pl.BlockSpec((d, tn), lambda i, j: (0, j)),
                  pl.BlockSpec((d, W32), lambda i, j: (0, 0)),
                  pl.BlockSpec((1, tn), lambda i, j: (0, j)),
                  pl.BlockSpec((1, tn), lambda i, j: (0, j))],
        out_specs=[pl.BlockSpec((tm, tn), lambda i, j: (i, j)),
                   pl.BlockSpec((tm, W32), lambda i, j: (i, 0))],
        out_shape=[jax.ShapeDtypeStruct((t, W16), BF16),
                   jax.ShapeDtypeStruct((t, W32), F32)],
        scratch_shapes=[pltpu.VMEM((tm, d), BF16)],
        compiler_params=_params(2),
        name="in_proj",
    )(x2, sc, sh, gain1, w16, w32, col_gain, col_flag)


def _cum_kernel(ff_ref, fb_ref, o_ref, *, seq):
    ri = lax.broadcasted_iota(jnp.int32, (LANES, LANES), 0)
    ci = lax.broadcasted_iota(jnp.int32, (LANES, LANES), 1)
    tri = jnp.where(ri >= ci, 1.0, 0.0).astype(BF16)
    carry = jnp.zeros((1, LANES), F32)
    for blk in range(seq // LANES):
        rows = slice(blk * LANES, (blk + 1) * LANES)
        x = ff_ref[0, rows, :] + fb_ref[...]
        lf = jnp.minimum(x, 0.0) - jnp.log(1.0 + jnp.exp(-jnp.abs(x)))
        hi = lf.astype(BF16)
        r1 = lf - hi.astype(F32)
        mid = r1.astype(BF16)
        lo = (r1 - mid.astype(F32)).astype(BF16)
        c = (jnp.dot(tri, hi, preferred_element_type=F32)
             + jnp.dot(tri, mid, preferred_element_type=F32)
             + jnp.dot(tri, lo, preferred_element_type=F32)) + carry
        o_ref[0, rows, :] = c
        carry = c[LANES - 1:LANES, :]


def _cum_call(o32, fb_pad, batch, seq):
    o32v = o32.reshape(batch, seq, W32)
    return pl.pallas_call(
        functools.partial(_cum_kernel, seq=seq),
        grid=(batch,),
        in_specs=[pl.BlockSpec((1, seq, LANES), lambda b: (b, 0, COL_FF // LANES)),
                  pl.BlockSpec((1, LANES), lambda b: (0, 0))],
        out_specs=pl.BlockSpec((1, seq, LANES), lambda b: (b, 0, 0)),
        out_shape=jax.ShapeDtypeStruct((batch, seq, LANES), F32),
        compiler_params=_params(1),
        name="fox_cumsum",
    )(o32v, fb_pad)


def _split3(c):
    hi = c.astype(BF16).astype(F32)
    r1 = c - hi
    mid = r1.astype(BF16).astype(F32)
    return hi, mid, (r1 - mid).astype(BF16).astype(F32)


def _fox_kernel(q_ref, k_ref, v_ref, cum_ref, o_ref, m_scr, acc_scr, s_scr, kaug_scr, vaug_scr, *, seq, tq, nh):
    hp = pl.program_id(1)
    lane = lax.broadcasted_iota(jnp.int32, (tq, LANES), 1)
    lane_s = lax.broadcasted_iota(jnp.int32, (seq, LANES), 1)
    causal = lax.broadcasted_iota(jnp.int32, (tq, tq), 1) <= lax.broadcasted_iota(jnp.int32, (tq, tq), 0)
    for j in range(nh):
        ck = jnp.sum(jnp.where(lane_s == hp * nh + j, cum_ref[0], 0.0), axis=1, keepdims=True) * LOG2E
        hi, mid, lo = _split3(ck)
        tail = jnp.where(lane_s < 3, 1.0, jnp.where(lane_s == 3, -hi, jnp.where(lane_s == 4, -mid,
                         jnp.where(lane_s == 5, -lo, 0.0))))
        for c in range(seq // tq):
            rows = slice(c * tq, (c + 1) * tq)
            kaug_scr[j, c, 0:HEAD_DIM, :] = k_ref[0, rows, j * HEAD_DIM:(j + 1) * HEAD_DIM].T
            kaug_scr[j, c, HEAD_DIM:, :] = tail[rows].T.astype(BF16)
        vaug_scr[j, :, 0:HEAD_DIM] = v_ref[0, :, j * HEAD_DIM:(j + 1) * HEAD_DIM]
        vaug_scr[j, :, HEAD_DIM:] = jnp.ones((seq, HEAD_DIM), BF16)

    def q_body(qi, carry):
        q0 = pl.multiple_of(qi * tq, tq)
        cum_t = cum_ref[0, pl.ds(q0, tq), :]
        qs = []
        for j in range(nh):
            cq = jnp.sum(jnp.where(lane == hp * nh + j, cum_t, 0.0), axis=1, keepdims=True) * LOG2E
            hi, mid, lo = _split3(cq)
            tail = jnp.where(lane == 0, hi, jnp.where(lane == 1, mid, jnp.where(lane == 2, lo,
                             jnp.where(lane < 6, 1.0, 0.0))))
            qs.append(jnp.concatenate([q_ref[0, pl.ds(q0, tq), j * HEAD_DIM:(j + 1) * HEAD_DIM],
                                       tail.astype(BF16)], axis=1))
            m_scr[j] = jnp.full((tq, LANES), NEG, F32)
            acc_scr[j] = jnp.zeros((tq, 2 * HEAD_DIM), F32)

        def scores(j, ki):
            return jnp.dot(qs[j], kaug_scr[j, ki], preferred_element_type=F32)

        def tile(ki, diagonal):
            k0 = pl.multiple_of(ki * tq, tq)
            for j in range(nh):
                s = s_scr[j]
                if diagonal:
                    s = jnp.where(causal, s, NEG)
                else:
                    s_scr[j] = scores(j, ki + 1)
                m_prev = m_scr[j]
                m_new = jnp.maximum(m_prev, jnp.max(s, axis=1, keepdims=True))
                alpha = jnp.exp2(m_prev - m_new)
                p = jnp.exp2(s - _lane_tile(m_new, tq // LANES))
                pv = jnp.dot(p.astype(BF16), vaug_scr[j, pl.ds(k0, tq), :], preferred_element_type=F32)
                acc_scr[j] = _lane_tile(alpha, 2) * acc_scr[j] + pv
                m_scr[j] = m_new

        def k_body(ki, c2):
            tile(ki, False)
            return c2

        for j in range(nh):
            s_scr[j] = scores(j, 0)
        lax.fori_loop(0, qi, k_body, 0)
        tile(qi, True)
        for j in range(nh):
            acc = acc_scr[j]
            o_ref[0, pl.ds(q0, tq), j * HEAD_DIM:(j + 1) * HEAD_DIM] = acc[:, 0:HEAD_DIM] / acc[:, HEAD_DIM:]
        return carry

    lax.fori_loop(0, seq // tq, q_body, 0)


def _fox_call(o16v, cum, batch, seq, tq):
    nh = 4
    groups = N_FOX_HEADS // nh
    w = nh * HEAD_DIM
    return pl.pallas_call(
        functools.partial(_fox_kernel, seq=seq, tq=tq, nh=nh),
        grid=(batch, groups),
        in_specs=[pl.BlockSpec((1, seq, w), lambda b, h: (b, 0, h)),
                  pl.BlockSpec((1, seq, w), lambda b, h: (b, 0, groups + h)),
                  pl.BlockSpec((1, seq, w), lambda b, h: (b, 0, 2 * groups + h)),
                  pl.BlockSpec((1, seq, LANES), lambda b, h: (b, 0, 0))],
        out_specs=pl.BlockSpec((1, seq, w), lambda b, h: (b, 0, h)),
        out_shape=jax.ShapeDtypeStruct((batch, seq, FOX_WIDTH), F32),
        scratch_shapes=[pltpu.VMEM((nh, tq, LANES), F32),
                        pltpu.VMEM((nh, tq, 2 * HEAD_DIM), F32),
                        pltpu.VMEM((nh, tq, tq), F32),
                        pltpu.VMEM((nh, seq // tq, 2 * HEAD_DIM, tq), BF16),
                        pltpu.VMEM((nh, seq, 2 * HEAD_DIM), BF16)],
        compiler_params=_params(2),
        name="fox_attention",
    )(o16v, o16v, o16v, cum)


def _cmp_kernel(x_ref, pos_ref, w1_ref, w2_ref, gain_ref, o_ref, xs_scr, *, seq, ncp):
    kv = pl.program_id(1)
    rows = xs_scr.shape[0]
    xs_scr[0:seq, :] = x_ref[0]
    xs_scr[seq:rows, :] = jnp.zeros((rows - seq, LANES), F32)
    acc = jnp.zeros((ncp, HEAD_DIM), F32)
    for l in range(CMP_BLOCK):
        xl = xs_scr[pl.ds(l, ncp, stride=CMP_STRIDE), :] + pos_ref[0, l:l + 1, :]
        acc = acc + jnp.dot(xl.astype(BF16), w1_ref[0, l], preferred_element_type=F32)
    hmid = acc * _sigmoid(acc)
    y = jnp.dot(hmid.astype(BF16), w2_ref[0], preferred_element_type=F32)
    yk = _rms(y) * gain_ref[...]
    o_ref[0, 0, 0] = jnp.where(kv == 0, yk, y).astype(BF16)


def _cmp_call(o32, pos, w1, w2, gain, batch, seq, ncp):
    o32v = o32.reshape(batch, seq, W32)
    hkv = N_NSA_KV_HEADS
    return pl.pallas_call(
        functools.partial(_cmp_kernel, seq=seq, ncp=ncp),
        grid=(batch, 2, hkv),
        in_specs=[pl.BlockSpec((1, seq, HEAD_DIM), lambda b, kv, h: (b, 0, kv * hkv + h)),
                  pl.BlockSpec((1, CMP_BLOCK, HEAD_DIM), lambda b, kv, h: (kv, 0, 0)),
                  pl.BlockSpec((1, CMP_BLOCK, HEAD_DIM, HEAD_DIM), lambda b, kv, h: (kv, 0, 0, 0)),
                  pl.BlockSpec((1, HEAD_DIM, HEAD_DIM), lambda b, kv, h: (kv, 0, 0)),
                  pl.BlockSpec((1, HEAD_DIM), lambda b, kv, h: (0, 0))],
        out_specs=pl.BlockSpec((1, 1, 1, ncp, HEAD_DIM), lambda b, kv, h: (b, kv, h, 0, 0)),
        out_shape=jax.ShapeDtypeStruct((batch, 2, hkv, ncp, HEAD_DIM), BF16),
        scratch_shapes=[pltpu.VMEM((CMP_STRIDE * ncp + CMP_BLOCK, LANES), F32)],
        compiler_params=_params(3),
        name="nsa_compress",
    )(o32v, pos, w1, w2, gain)


def _bias_kernel(rb_ref, o_ref, *, width, key_stride, key_offset):
    v = pl.program_id(0)
    i = lax.broadcasted_iota(jnp.int32, (LANES, width), 0)
    j = lax.broadcasted_iota(jnp.int32, (LANES, width), 1)
    d = v * LANES + i - (key_stride * j + key_offset)
    n = jnp.maximum(d, 0)
    max_exact = N_BUCKETS // 2
    nf = jnp.maximum(n, 1).astype(F32)
    large = max_exact + (jnp.log(nf / max_exact) / math.log(MAX_DISTANCE / max_exact)
                         * (N_BUCKETS - max_exact)).astype(jnp.int32)
    large = jnp.minimum(large, N_BUCKETS - 1)
    bkt = jnp.where(n < max_exact, n, large)
    vals = [jnp.zeros((LANES, width), F32) for _ in range(N_NSA_HEADS)]
    for bk in range(N_BUCKETS):
        hit = bkt == bk
        for h in range(N_NSA_HEADS):
            vals[h] = jnp.where(hit, rb_ref[bk * N_NSA_HEADS + h] * LOG2E, vals[h])
    for h in range(N_NSA_HEADS):
        g = h % NSA_GROUP
        o_ref[0, h // NSA_GROUP, g * LANES:(g + 1) * LANES, :] = vals[h]


def _bias_call(rb_flat, n_variants, width, key_stride, key_offset, name):
    return pl.pallas_call(
        functools.partial(_bias_kernel, width=width, key_stride=key_stride, key_offset=key_offset),
        grid=(n_variants,),
        in_specs=[pl.BlockSpec(memory_space=pltpu.SMEM)],
        out_specs=pl.BlockSpec((1, N_NSA_KV_HEADS, GROUP_ROWS, width), lambda v: (v, 0, 0, 0)),
        out_shape=jax.ShapeDtypeStruct((n_variants, N_NSA_KV_HEADS, GROUP_ROWS, width), F32),
        compiler_params=_params(1),
        name=name,
    )(rb_flat)


def _tile4(a):
    return jnp.concatenate([a] * NSA_GROUP, axis=0)


def _nsa_kernel(q_ref, ks_ref, kw_ref, vs_ref, vw_ref, g_ref, kcvc_ref, bc_ref, tz_ref, selmt_ref,
                wa_ref, wb_ref, wc_ref, wd_ref, o_ref, wa_out, wb_out, wc_out, wd_out,
                qg_scr, m_scr, acc_scr, o_scr, s_scr, ksa_scr, vsa_scr, kwp_scr, vwa_scr, *, seq, ncp):
    qt = pl.program_id(1)
    q0 = qt * LANES
    n_slc = seq // SEL_BLOCK
    top_n = min(N_SEL, n_slc)
    nkv = N_NSA_KV_HEADS
    for src, dst in ((wa_ref, wa_out), (wb_ref, wb_out), (wc_ref, wc_out), (wd_ref, wd_out)):
        dst[...] = src[...].astype(BF16)
    ri = lax.broadcasted_iota(jnp.int32, (LANES, LANES), 0)
    ci = lax.broadcasted_iota(jnp.int32, (LANES, LANES), 1)
    rk = lax.broadcasted_iota(jnp.int32, (LANES, KEY_CHUNK), 0)
    ck = lax.broadcasted_iota(jnp.int32, (LANES, KEY_CHUNK), 1)
    eye = jnp.where(ri == ci, 1.0, 0.0).astype(BF16)
    gates = _sigmoid(g_ref[0])

    def gate_col(hk, br):
        cols = []
        for g in range(NSA_GROUP):
            c = COL_NG % LANES + (hk * NSA_GROUP + g) * N_BRANCH + br
            cols.append(gates[:, c:c + 1])
        return jnp.concatenate(cols, axis=0)

    @pl.when(qt == 0)
    def _():
        ones = jnp.ones((seq, HEAD_DIM), BF16)
        blk_row = lax.broadcasted_iota(jnp.int32, (LANES, KEY_CHUNK), 0)
        for hk in range(nkv):
            hc = slice(hk * HEAD_DIM, (hk + 1) * HEAD_DIM)
            for c in range(seq // KEY_CHUNK):
                rows = slice(c * KEY_CHUNK, (c + 1) * KEY_CHUNK)
                ksa_scr[hk, c, 0:HEAD_DIM, :] = ks_ref[0, rows, hc].T
                ksa_scr[hk, c, HEAD_DIM:, :] = jnp.where((c * KEY_CHUNK + ck) // SEL_BLOCK == blk_row,
                                                         -MASK_BIG, 0.0).astype(BF16)
            for t in range(WIN_PAD // LANES):
                kwp_scr[hk, t] = jnp.zeros((HEAD_DIM, LANES), BF16)
            for t in range(seq // LANES):
                kwp_scr[hk, WIN_PAD // LANES + t] = kw_ref[0, t * LANES:(t + 1) * LANES, hc].T
            vsa_scr[hk, :, 0:HEAD_DIM] = vs_ref[0, :, hc]
            vsa_scr[hk, :, HEAD_DIM:] = ones
            vwa_scr[hk, 0:WIN_PAD, :] = jnp.zeros((WIN_PAD, 2 * HEAD_DIM), BF16)
            vwa_scr[hk, WIN_PAD:, 0:HEAD_DIM] = vw_ref[0, :, hc]
            vwa_scr[hk, WIN_PAD:, HEAD_DIM:] = ones

    def reset(br):
        m_scr[br] = jnp.full(m_scr.shape[1:], NEG, F32)
        acc_scr[br] = jnp.zeros(acc_scr.shape[1:], F32)

    def online_update(br, hk, s, vaug):
        m_prev = m_scr[br, hk]
        m_new = jnp.maximum(m_prev, jnp.max(s, axis=1, keepdims=True))
        alpha = jnp.exp2(m_prev - m_new)
        p = jnp.exp2(s - _lane_tile(m_new, s.shape[1] // LANES))
        pv = jnp.dot(p.astype(BF16), vaug, preferred_element_type=F32)
        acc_scr[br, hk] = _lane_tile(alpha, 2) * acc_scr[br, hk] + pv
        m_scr[br, hk] = m_new

    def finish(br, hk):
        acc = acc_scr[br, hk]
        return acc[:, 0:HEAD_DIM] / acc[:, HEAD_DIM:]

    def chunk_bias(hk, ta, tb):
        return jnp.concatenate([tz_ref[jnp.clip(ta, 0, 2), hk], tz_ref[jnp.clip(tb, 0, 2), hk]], axis=1)

    for hk in range(nkv):
        for g in range(NSA_GROUP):
            h = hk * NSA_GROUP + g
            qg_scr[hk, g * LANES:(g + 1) * LANES, 0:HEAD_DIM] = q_ref[0, :, h * HEAD_DIM:(h + 1) * HEAD_DIM]

    for hk in range(nkv):
        qg = qg_scr[hk, :, 0:HEAD_DIM]

        kc = kcvc_ref[0, 0, hk]
        vc = kcvc_ref[0, 1, hk]
        s = lax.dot_general(qg, kc, NT_DIMS, preferred_element_type=F32) + bc_ref[0, hk]
        rc = lax.broadcasted_iota(jnp.int32, (LANES, ncp), 0)
        cc = lax.broadcasted_iota(jnp.int32, (LANES, ncp), 1)
        valid_c = _tile4(jnp.where(q0 + rc - (CMP_STRIDE * cc + CMP_BLOCK - 1) >= 0, 1.0, 0.0)) > 0.5
        s = jnp.where(valid_c, s, NEG)
        p = jnp.where(valid_c, jnp.exp2(s - jnp.max(s, axis=1, keepdims=True)), 0.0)
        l = jnp.sum(p, axis=1, keepdims=True)
        p = p / jnp.where(l > 0.0, l, 1.0)
        o_scr[hk] = gate_col(hk, 0) * jnp.dot(p.astype(BF16), vc, preferred_element_type=F32)

        psum = p[0:LANES]
        for g in range(1, NSA_GROUP):
            psum = psum + p[g * LANES:(g + 1) * LANES]
        p_hi = psum.astype(BF16)
        p_lo = (psum - p_hi.astype(F32)).astype(BF16)
        selmt = selmt_ref[...]
        imp = (lax.dot_general(selmt, p_hi, NT_DIMS, preferred_element_type=F32)
               + lax.dot_general(selmt, p_lo, NT_DIMS, preferred_element_type=F32))
        imp = imp[0:n_slc]
        blk = lax.broadcasted_iota(jnp.int32, (n_slc, LANES), 0)
        cur = (q0 + lax.broadcasted_iota(jnp.int32, (n_slc, LANES), 1)) // SEL_BLOCK
        forced = (blk == 0) | (blk == cur) | (blk == cur - 1)
        imp = jnp.where(forced, FORCE, imp)
        imp = jnp.where(blk <= cur, imp, -jnp.inf)
        rank = jnp.zeros((n_slc, LANES), F32)
        for j in range(n_slc):
            row = imp[j:j + 1, :]
            beats = jnp.where(row > imp, 1.0, jnp.where(row == imp, jnp.where(blk > j, 1.0, 0.0), 0.0))
            rank = rank + beats
        sel_t = jnp.where(rank < top_n, jnp.where(imp > -jnp.inf, 1.0, 0.0), 0.0)
        if n_slc < LANES:
            sel_t = jnp.concatenate([sel_t, jnp.zeros((LANES - n_slc, LANES), F32)], axis=0)
        sel_q = lax.dot_general(eye, sel_t.astype(BF16), NT_DIMS, preferred_element_type=F32)
        not_sel = jnp.where(ci < n_slc, 1.0 - sel_q, 0.0).astype(BF16)
        for g in range(NSA_GROUP):
            qg_scr[hk, g * LANES:(g + 1) * LANES, HEAD_DIM:] = not_sel

    reset(WIN)
    for off, n_tiles in ((1, 2), (3, 2), (5, 2)):
        width = n_tiles * LANES
        kstart = (qt - off) * LANES
        pos = kstart + lax.broadcasted_iota(jnp.int32, (LANES, width), 1)
        d = (q0 + lax.broadcasted_iota(jnp.int32, (LANES, width), 0)) - pos
        ok = (d >= 0) & (d < WINDOW) & (pos >= 0)
        madd = _tile4(jnp.where(ok, 0.0, -MASK_BIG))
        t0 = qt - off + WIN_PAD // LANES
        p0 = pl.multiple_of(kstart + WIN_PAD, LANES)
        for hk in range(nkv):
            k_t = jnp.concatenate([kwp_scr[hk, t0 + t] for t in range(n_tiles)], axis=1)
            bias = jnp.concatenate([tz_ref[min(off - t, 2), hk] for t in range(n_tiles)], axis=1)
            s = jnp.dot(qg_scr[hk, :, 0:HEAD_DIM], k_t, preferred_element_type=F32) + bias + madd
            online_update(WIN, hk, s, vwa_scr[hk, pl.ds(p0, width), :])

    reset(SEL)

    def sel_scores(hk, c):
        ta = qt - 2 * c
        return jnp.dot(qg_scr[hk], ksa_scr[hk, c], preferred_element_type=F32) + chunk_bias(hk, ta, ta - 1)

    def sel_chunk(c, last):
        k0 = pl.multiple_of(c * KEY_CHUNK, KEY_CHUNK)
        for hk in range(nkv):
            s = s_scr[hk]
            if last:
                s = jnp.where(_tile4(jnp.where((k0 + ck) <= (q0 + rk), 1.0, 0.0)) > 0.5, s, -MASK_BIG)
            else:
                s_scr[hk] = sel_scores(hk, c + 1)
            online_update(SEL, hk, s, vsa_scr[hk, pl.ds(k0, KEY_CHUNK), :])

    def sel_body(c, carry):
        sel_chunk(c, False)
        return carry

    n_chunks = (qt + 2) // 2
    for hk in range(nkv):
        s_scr[hk] = sel_scores(hk, 0)
    lax.fori_loop(0, n_chunks - 1, sel_body, 0)
    sel_chunk(n_chunks - 1, True)
    for hk in range(nkv):
        o = (o_scr[hk] + gate_col(hk, 1) * finish(SEL, hk)) + gate_col(hk, 2) * finish(WIN, hk)
        for g in range(NSA_GROUP):
            h = hk * NSA_GROUP + g
            o_ref[0, :, h * HEAD_DIM:(h + 1) * HEAD_DIM] = o[g * LANES:(g + 1) * LANES]


def _nsa_call(o16v, o32v, kcvc, bias_c, tz, selmt, weights_f32, batch, seq, ncp):
    nqt = seq // LANES
    wspecs = []
    for w in weights_f32:
        blk, span = _cast_block(w, batch * nqt)
        wspecs.append(pl.BlockSpec(blk, functools.partial(lambda b, t, span: ((b * nqt + t) // span, 0), span=span)))
    kvw = NSA_KV_WIDTH
    nkv = N_NSA_KV_HEADS
    base = (3 * FOX_WIDTH + NSA_WIDTH) // kvw
    return pl.pallas_call(
        functools.partial(_nsa_kernel, seq=seq, ncp=ncp),
        grid=(batch, nqt),
        in_specs=[pl.BlockSpec((1, LANES, NSA_WIDTH), lambda b, t: (b, t, 3 * FOX_WIDTH // NSA_WIDTH)),
                  pl.BlockSpec((1, seq, kvw), lambda b, t: (b, 0, base)),
                  pl.BlockSpec((1, seq, kvw), lambda b, t: (b, 0, base + 1)),
                  pl.BlockSpec((1, seq, kvw), lambda b, t: (b, 0, base + 2)),
                  pl.BlockSpec((1, seq, kvw), lambda b, t: (b, 0, base + 3)),
                  pl.BlockSpec((1, LANES, LANES), lambda b, t: (b, t, COL_NG // LANES)),
                  pl.BlockSpec((1, 2, N_NSA_KV_HEADS, ncp, HEAD_DIM), lambda b, t: (b, 0, 0, 0, 0)),
                  pl.BlockSpec((1, N_NSA_KV_HEADS, GROUP_ROWS, ncp), lambda b, t: (t, 0, 0, 0)),
                  pl.BlockSpec((3, N_NSA_KV_HEADS, GROUP_ROWS, LANES), lambda b, t: (0, 0, 0, 0)),
                  pl.BlockSpec((LANES, ncp), lambda b, t: (0, 0))] + wspecs,
        out_specs=[pl.BlockSpec((1, LANES, NSA_WIDTH), lambda b, t: (b, t, 0))] + wspecs,
        out_shape=[jax.ShapeDtypeStruct((batch, seq, NSA_WIDTH), F32)]
        + [jax.ShapeDtypeStruct(w.shape, BF16) for w in weights_f32],
        scratch_shapes=[pltpu.VMEM((nkv, GROUP_ROWS, 2 * HEAD_DIM), BF16),
                        pltpu.VMEM((2, nkv, GROUP_ROWS, LANES), F32),
                        pltpu.VMEM((2, nkv, GROUP_ROWS, 2 * HEAD_DIM), F32),
                        pltpu.VMEM((nkv, GROUP_ROWS, HEAD_DIM), F32),
                        pltpu.VMEM((nkv, GROUP_ROWS, KEY_CHUNK), F32),
                        pltpu.VMEM((nkv, seq // KEY_CHUNK, 2 * HEAD_DIM, KEY_CHUNK), BF16),
                        pltpu.VMEM((nkv, seq, 2 * HEAD_DIM), BF16),
                        pltpu.VMEM((nkv, (seq + WIN_PAD) // LANES, HEAD_DIM, LANES), BF16),
                        pltpu.VMEM((nkv, seq + WIN_PAD, 2 * HEAD_DIM), BF16)],
        compiler_params=_params(2),
        name="nsa_attention",
    )(o16v, o16v, o16v, o16v, o16v, o32v, kcvc, bias_c, tz, selmt, *weights_f32)


def _out_kernel(of_ref, on_ref, gain_ref, w_ref, x_ref, g_ref, o_ref, *, tm, n_chains):
    rows_per_chain = tm // n_chains
    for c in range(n_chains):
        rows = slice(c * rows_per_chain, (c + 1) * rows_per_chain)
        y = jnp.concatenate([(_rms(of_ref[rows, :]) * gain_ref[:, 0:FOX_WIDTH]).astype(BF16),
                             (_rms(on_ref[rows, :]) * gain_ref[:, FOX_WIDTH:MIX_WIDTH]).astype(BF16)], axis=1)
        acc = jnp.dot(y, w_ref[...], preferred_element_type=F32)
        o_ref[rows, :] = x_ref[rows, :] + g_ref[0] * acc


def _out_call(o_fox, o_nsa, gain, w_out, x2, g1, seq):
    t, d = x2.shape
    tm = min(512, seq)
    rows_per_batch = seq // tm
    return pl.pallas_call(
        functools.partial(_out_kernel, tm=tm, n_chains=2),
        grid=(t // tm,),
        in_specs=[pl.BlockSpec((tm, FOX_WIDTH), lambda i: (i, 0)),
                  pl.BlockSpec((tm, NSA_WIDTH), lambda i: (i, 0)),
                  pl.BlockSpec((1, MIX_WIDTH), lambda i: (0, 0)),
                  pl.BlockSpec((MIX_WIDTH, d), lambda i: (0, 0)),
                  pl.BlockSpec((tm, d), lambda i: (i, 0)),
                  pl.BlockSpec((1, 1, d), lambda i: (i // rows_per_batch, 0, 0))],
        out_specs=pl.BlockSpec((tm, d), lambda i: (i, 0)),
        out_shape=jax.ShapeDtypeStruct((t, d), F32),
        compiler_params=_params(1),
        name="out_proj",
    )(o_fox, o_nsa, gain, w_out, x2, g1)


def _ffn_kernel(x_ref, gain_ref, sc_ref, sh_ref, g_ref, wg_ref, wu_ref, wd_ref, o_ref, h_scr, acc_scr):
    f = pl.program_id(1)

    @pl.when(f == 0)
    def _():
        h = _rms(x_ref[...]) * gain_ref[...]
        h = h * (1.0 + sc_ref[0]) + sh_ref[0]
        h_scr[...] = h.astype(BF16)
        acc_scr[...] = jnp.zeros(acc_scr.shape, F32)

    h = h_scr[...]
    a = jnp.dot(h, wg_ref[...], preferred_element_type=F32)
    u = jnp.dot(h, wu_ref[...], preferred_element_type=F32)
    t = (a * _sigmoid(a)) * u
    acc_scr[...] += jnp.dot(t.astype(BF16), wd_ref[...], preferred_element_type=F32)

    @pl.when(f == pl.num_programs(1) - 1)
    def _():
        o_ref[...] = x_ref[...] + g_ref[0] * acc_scr[...]


def _ffn_call(x1, gain2, sc, sh, g2, wg, wu, wd, seq):
    t, d = x1.shape
    dff = wg.shape[1]
    tm = min(512, seq)
    tf = 512 if dff % 512 == 0 else dff
    rows_per_batch = seq // tm
    return pl.pallas_call(
        _ffn_kernel,
        grid=(t // tm, dff // tf),
        in_specs=[pl.BlockSpec((tm, d), lambda i, f: (i, 0)),
                  pl.BlockSpec((1, d), lambda i, f: (0, 0)),
                  pl.BlockSpec((1, 1, d), lambda i, f: (i // rows_per_batch, 0, 0)),
                  pl.BlockSpec((1, 1, d), lambda i, f: (i // rows_per_batch, 0, 0)),
                  pl.BlockSpec((1, 1, d), lambda i, f: (i // rows_per_batch, 0, 0)),
                  pl.BlockSpec((d, tf), lambda i, f: (0, f)),
                  pl.BlockSpec((d, tf), lambda i, f: (0, f)),
                  pl.BlockSpec((tf, d), lambda i, f: (f, 0))],
        out_specs=pl.BlockSpec((tm, d), lambda i, f: (i, 0)),
        out_shape=jax.ShapeDtypeStruct((t, d), F32),
        scratch_shapes=[pltpu.VMEM((tm, d), BF16), pltpu.VMEM((tm, d), F32)],
        compiler_params=_params(2),
        name="swiglu_ffn",
    )(x1, gain2, sc, sh, g2, wg, wu, wd)


def _selection_matrix_t(ncp, n_slc):
    r, q = SEL_BLOCK // CMP_STRIDE, CMP_BLOCK // CMP_STRIDE
    m = np.zeros((LANES, ncp), np.float32)
    for j in range(n_slc):
        for a in range(r):
            for b in range(q):
                c = r * j + a - b
                if 0 <= c < ncp:
                    m[j, c] += 1.0
    return m


def _w_in_block_sources():
    o = 0
    start = {}
    for name, width in (("fq", FOX_WIDTH), ("fk", FOX_WIDTH), ("fv", FOX_WIDTH), ("ff", N_FOX_HEADS),
                        ("nq", NSA_WIDTH), ("nk", N_BRANCH * NSA_KV_WIDTH), ("nv", N_BRANCH * NSA_KV_WIDTH),
                        ("ng", N_BRANCH * N_NSA_HEADS)):
        start[name] = o
        o += width
    kvw = NSA_KV_WIDTH
    groups = [(start["fq"], 3 * FOX_WIDTH), (start["nq"], NSA_WIDTH),
              (start["nk"] + kvw, 2 * kvw), (start["nv"] + kvw, 2 * kvw),
              (start["nk"], kvw), (start["nv"], kvw)]
    blocks = [s + LANES * b for s, width in groups for b in range(width // LANES)]
    return blocks, (start["ff"], N_FOX_HEADS), (start["ng"], N_BRANCH * N_NSA_HEADS)


def _repack_kernel(src_ref, wt_hbm, o16_ref, o32_ref, buf, sem, *, n16, n_whole, ff, ng):
    k = pl.program_id(0)
    slot = k % 2

    def whole_copy(kk, s):
        r0 = pl.multiple_of(src_ref[kk], 8)
        return pltpu.make_async_copy(wt_hbm.at[pl.ds(r0, LANES), :], buf.at[s], sem.at[s])

    def narrow_copies(s):
        return [pltpu.make_async_copy(wt_hbm.at[pl.ds(ff[0], ff[1]), :], buf.at[s, pl.ds(0, ff[1]), :], sem.at[s]),
                pltpu.make_async_copy(wt_hbm.at[pl.ds(ng[0], ng[1]), :], buf.at[s, pl.ds(ff[1], ng[1]), :],
                                      sem.at[s])]

    @pl.when(k == 0)
    def _():
        whole_copy(0, 0).start()

    @pl.when(k + 1 < n_whole)
    def _():
        whole_copy(k + 1, 1 - slot).start()

    @pl.when(k + 1 == n_whole)
    def _():
        for cp in narrow_copies(1 - slot):
            cp.start()

    @pl.when(k < n_whole)
    def _():
        whole_copy(k, slot).wait()

    @pl.when(k == n_whole)
    def _():
        for cp in narrow_copies(slot):
            cp.wait()

    xt = buf[slot].T
    lane = lax.broadcasted_iota(jnp.int32, xt.shape, 1)
    xt = jnp.where(jnp.logical_or(k < n_whole, lane < ff[1] + ng[1]), xt, 0.0).astype(BF16)

    @pl.when(k < n16)
    def _():
        o16_ref[...] = xt

    @pl.when(k >= n16)
    def _():
        o32_ref[...] = xt


def _repack_w_in(wt):
    n, d = wt.shape
    blocks, ff, ng = _w_in_block_sources()
    n16 = W16 // LANES
    n_whole = len(blocks)
    assert n_whole + 1 == (W16 + W32) // LANES
    return pl.pallas_call(
        functools.partial(_repack_kernel, n16=n16, n_whole=n_whole, ff=ff, ng=ng),
        grid_spec=pltpu.PrefetchScalarGridSpec(
            num_scalar_prefetch=1,
            grid=(n_whole + 1,),
            in_specs=[pl.BlockSpec(memory_space=pl.ANY)],
            out_specs=[pl.BlockSpec((d, LANES), lambda k, src: (0, jnp.minimum(k, n16 - 1))),
                       pl.BlockSpec((d, LANES), lambda k, src: (0, jnp.maximum(k - n16, 0)))],
            scratch_shapes=[pltpu.VMEM((2, LANES, d), F32), pltpu.SemaphoreType.DMA((2,))]),
        out_shape=[jax.ShapeDtypeStruct((d, W16), BF16), jax.ShapeDtypeStruct((d, W32), BF16)],
        compiler_params=_params(1),
        name="w_in_repack",
    )(jnp.asarray(blocks, jnp.int32), wt)


def kernel(x, c, ada_w, ada_b, norm1_gain, norm2_gain, w_in, fox_f_bias, fox_q_gain, fox_k_gain, nsa_q_gain,
           nsa_k_gain, nsa_cmp_pos, nsa_cmp_w1, nsa_cmp_w2, rel_bias, mix_out_gain, w_out, ffn_w_gate, ffn_w_up,
           ffn_w_down):
    batch, seq, d = x.shape
    assert seq % KEY_CHUNK == 0 and seq >= WINDOW and d % LANES == 0 and seq // SEL_BLOCK <= LANES
    depth = ada_w.shape[0]
    nqt = seq // LANES
    ncp = -(-(seq // CMP_STRIDE) // LANES) * LANES
    tq_fox = min(256, seq)

    selmt = jnp.asarray(_selection_matrix_t(ncp, seq // SEL_BLOCK), BF16)
    rb_flat = rel_bias.reshape(-1)
    bias_c = _bias_call(rb_flat, nqt, ncp, CMP_STRIDE, CMP_BLOCK - 1, "t5_bias_compressed")
    tz = _bias_call(rb_flat, 3, LANES, 1, 0, "t5_bias_toeplitz")

    ones_h = jnp.ones((HEAD_DIM,), F32)
    c_pad = jnp.pad(c, ((0, 8 - batch % 8 if batch % 8 else 0), (0, 0)))
    x2 = x.reshape(batch * seq, d)
    for layer in range(depth):
        mod = _ada_call(c_pad, ada_w[layer], ada_b[layer][None, :])[:batch]
        sh1, sc1, g1, sh2, sc2, g2 = [mod[:, i * d:(i + 1) * d][:, None, :] for i in range(N_MOD)]

        kg = nsa_k_gain[layer]
        col_gain = jnp.concatenate([
            jnp.tile(fox_q_gain[layer] * QSCALE, N_FOX_HEADS), jnp.tile(fox_k_gain[layer], N_FOX_HEADS),
            jnp.tile(ones_h, N_FOX_HEADS), jnp.tile(nsa_q_gain[layer] * QSCALE, N_NSA_HEADS),
            jnp.tile(kg[1], N_NSA_KV_HEADS), jnp.tile(kg[2], N_NSA_KV_HEADS),
            jnp.tile(ones_h, 2 * N_NSA_KV_HEADS)])[None, :]
        col_flag = jnp.concatenate([
            jnp.ones((2 * FOX_WIDTH,), F32), jnp.zeros((FOX_WIDTH,), F32), jnp.ones((NSA_WIDTH,), F32),
            jnp.ones((2 * NSA_KV_WIDTH,), F32), jnp.zeros((2 * NSA_KV_WIDTH,), F32)])[None, :]
        w16, w32 = _repack_w_in(jnp.swapaxes(w_in, 1, 2)[layer])
        o16, o32 = _proj_call(x2, sc1, sh1, norm1_gain[layer][None, :], w16, w32,
                              col_gain, col_flag, seq)
        o16v = o16.reshape(batch, seq, W16)
        o32v = o32.reshape(batch, seq, W32)

        fb_pad = jnp.pad(fox_f_bias[layer], (0, LANES - N_FOX_HEADS))[None, :]
        cum = _cum_call(o32, fb_pad, batch, seq)
        o_fox = _fox_call(o16v, cum, batch, seq, tq_fox)

        w1 = nsa_cmp_w1[layer].reshape(2, CMP_BLOCK, HEAD_DIM, HEAD_DIM).astype(BF16)
        kcvc = _cmp_call(o32, nsa_cmp_pos[layer], w1, nsa_cmp_w2[layer].astype(BF16), kg[0][None, :],
                         batch, seq, ncp)
        o_nsa, wg16, wu16, wo16, wd16 = _nsa_call(
            o16v, o32v, kcvc, bias_c, tz, selmt,
            (ffn_w_gate[layer], ffn_w_up[layer], w_out[layer], ffn_w_down[layer]), batch, seq, ncp)

        x1 = _out_call(o_fox.reshape(batch * seq, FOX_WIDTH), o_nsa.reshape(batch * seq, NSA_WIDTH),
                       mix_out_gain[layer][None, :], wo16, x2, g1, seq)
        x2 = _ffn_call(x1, norm2_gain[layer][None, :], sc2, sh2, g2, wg16, wu16, wd16, seq)
    return x2.reshape(batch, seq, d)
```

```python
import functools
import math

import numpy as np
import jax
import jax.numpy as jnp
from jax import lax
from jax.experimental import pallas as pl
from jax.experimental.pallas import tpu as pltpu

HEAD_DIM = 128
N_FOX_HEADS = 8
N_NSA_HEADS = 8
N_NSA_KV_HEADS = 2
NSA_GROUP = N_NSA_HEADS // N_NSA_KV_HEADS
FOX_WIDTH = N_FOX_HEADS * HEAD_DIM
NSA_WIDTH = N_NSA_HEADS * HEAD_DIM
NSA_KV_WIDTH = N_NSA_KV_HEADS * HEAD_DIM
MIX_WIDTH = FOX_WIDTH + NSA_WIDTH
N_BRANCH = 3
CMP_BLOCK = 32
CMP_STRIDE = 16
SEL_BLOCK = 64
N_SEL = 8
WINDOW = 512
N_BUCKETS = 32
MAX_DISTANCE = 128
N_MOD = 6
SCALE = HEAD_DIM ** -0.5
LOG2E = math.log2(math.e)
QSCALE = SCALE * LOG2E
EPS = 1e-6
NEG = -1e30
FORCE = 1e6

LANES = 128
GROUP_ROWS = NSA_GROUP * LANES
VMEM_LIMIT = 56 * 1024 * 1024
MXU_COLS = 256
KEY_CHUNK = MXU_COLS
WIN_PAD = WINDOW + LANES
MASK_BIG = 2.0 ** 100
SEL, WIN = 0, 1
REPACK_SLOTS = 4

W16 = 3 * FOX_WIDTH + NSA_WIDTH + 4 * NSA_KV_WIDTH
W32 = 5 * LANES
COL_FF = 2 * NSA_KV_WIDTH
COL_NG = COL_FF + N_FOX_HEADS

F32 = jnp.float32
BF16 = jnp.bfloat16
NT_DIMS = (((1,), (1,)), ((), ()))


def _params(n_axes):
    return pltpu.CompilerParams(dimension_semantics=("arbitrary",) * n_axes,
                                vmem_limit_bytes=VMEM_LIMIT)


def _sigmoid(x):
    return 1.0 / (1.0 + jnp.exp(-x))


def _lane_tile(a, n):
    return jnp.concatenate([a] * n, axis=1)


BF16_SUBLANES = 16


def _cast_block(w, n_steps):
    rows, cols = w.shape
    assert rows % n_steps == 0, (w.shape, n_steps)
    per_step = rows // n_steps
    span = BF16_SUBLANES // math.gcd(BF16_SUBLANES, per_step)
    assert n_steps % span == 0, (w.shape, n_steps)
    return (per_step * span, cols), span


def _rms(x):
    return x * lax.rsqrt(jnp.mean(x * x, axis=-1, keepdims=True) + EPS)


def _ada_kernel(c_ref, w_ref, b_ref, o_ref):
    c = c_ref[...]
    s = (c * _sigmoid(c)).astype(BF16)
    o_ref[...] = jnp.dot(s, w_ref[...].astype(BF16), preferred_element_type=F32) + b_ref[...]


def _ada_call(c_pad, w, b):
    rows, d = c_pad.shape
    n = w.shape[1]
    tn = next(t for t in (1024, 768, 512, 384, 256, 128) if n % t == 0)
    return pl.pallas_call(
        _ada_kernel,
        grid=(n // tn,),
        in_specs=[pl.BlockSpec((rows, d), lambda j: (0, 0)),
                  pl.BlockSpec((d, tn), lambda j: (0, j)),
                  pl.BlockSpec((1, tn), lambda j: (0, j))],
        out_specs=pl.BlockSpec((rows, tn), lambda j: (0, j)),
        out_shape=jax.ShapeDtypeStruct((rows, n), F32),
        compiler_params=_params(1),
        name="adaln",
    )(c_pad, w, b)


def _proj_kernel(x_ref, sc_ref, sh_ref, g_ref, w16_ref, w32_ref, gain_ref, flag_ref, o16_ref, o32_ref, h_scr, *, tn):
    j = pl.program_id(1)

    @pl.when(j == 0)
    def _():
        h = _rms(x_ref[...]) * g_ref[...]
        h = h * (1.0 + sc_ref[0]) + sh_ref[0]
        h_scr[...] = h.astype(BF16)

    h = h_scr[...]
    for c in range(tn // MXU_COLS):
        acc = jnp.dot(h, w16_ref[:, c * MXU_COLS:(c + 1) * MXU_COLS], preferred_element_type=F32)
        for g in range(MXU_COLS // LANES):
            cols = slice(c * MXU_COLS + g * LANES, c * MXU_COLS + (g + 1) * LANES)
            a = acc[:, g * LANES:(g + 1) * LANES]
            r = lax.rsqrt(jnp.mean(a * a, axis=-1, keepdims=True) + EPS)
            scale = jnp.where(flag_ref[:, cols] > 0.5, r, 1.0)
            o16_ref[:, cols] = (a * scale * gain_ref[:, cols]).astype(BF16)

    @pl.when(j == pl.num_programs(1) - 1)
    def _():
        o32_ref[...] = jnp.dot(h_scr[...], w32_ref[...], preferred_element_type=F32)


def _proj_call(x2, sc, sh, gain1, w16, w32, col_gain, col_flag, seq):
    t, d = x2.shape
    tm = min(1024, seq)
    tn = 1024
    rows_per_batch = seq // tm
    return pl.pallas_call(
        functools.partial(_proj_kernel, tn=tn),
        grid=(t // tm, W16 // tn),
        in_specs=[pl.BlockSpec((tm, d), lambda i, j: (i, 0)),
                  pl.BlockSpec((1, 1, d), lambda i, j: (i // rows_per_batch, 0, 0)),
                  pl.BlockSpec((1, 1, d), lambda i, j: (i // rows_per_batch, 0, 0)),
                  pl.BlockSpec((1, d), lambda i, j: (0, 0)),
                  pl.BlockSpec((d, tn), lambda i, j: (0, j)),
                  pl.BlockSpec((d, W32), lambda i, j: (0, 0)),
                  pl.BlockSpec((1, tn), lambda i, j: (0, j)),
                  pl.BlockSpec((1, tn), lambda i, j: (0, j))],
        out_specs=[pl.BlockSpec((tm, tn), lambda i, j: (i, j)),
                   pl.BlockSpec((tm, W32), lambda i, j: (i, 0))],
        out_shape=[jax.ShapeDtypeStruct((t, W16), BF16),
                   jax.ShapeDtypeStruct((t, W32), F32)],
        scratch_shapes=[pltpu.VMEM((tm, d), BF16)],
        compiler_params=_params(2),
        name="in_proj",
    )(x2, sc, sh, gain1, w16, w32, col_gain, col_flag)


def _cum_kernel(ff_ref, fb_ref, o_ref, *, seq):
    ri = lax.broadcasted_iota(jnp.int32, (LANES, LANES), 0)
    ci = lax.broadcasted_iota(jnp.int32, (LANES, LANES), 1)
    tri = jnp.where(ri >= ci, 1.0, 0.0).astype(BF16)
    carry = jnp.zeros((1, LANES), F32)
    for blk in range(seq // LANES):
        rows = slice(blk * LANES, (blk + 1) * LANES)
        x = ff_ref[0, rows, :] + fb_ref[...]
        lf = jnp.minimum(x, 0.0) - jnp.log(1.0 + jnp.exp(-jnp.abs(x)))
        hi = lf.astype(BF16)
        r1 = lf - hi.astype(F32)
        mid = r1.astype(BF16)
        lo = (r1 - mid.astype(F32)).astype(BF16)
        c = (jnp.dot(tri, hi, preferred_element_type=F32)
             + jnp.dot(tri, mid, preferred_element_type=F32)
             + jnp.dot(tri, lo, preferred_element_type=F32)) + carry
        o_ref[0, rows, :] = c
        carry = c[LANES - 1:LANES, :]


def _cum_call(o32, fb_pad, batch, seq):
    o32v = o32.reshape(batch, seq, W32)
    return pl.pallas_call(
        functools.partial(_cum_kernel, seq=seq),
        grid=(batch,),
        in_specs=[pl.BlockSpec((1, seq, LANES), lambda b: (b, 0, COL_FF // LANES)),
                  pl.BlockSpec((1, LANES), lambda b: (0, 0))],
        out_specs=pl.BlockSpec((1, seq, LANES), lambda b: (b, 0, 0)),
        out_shape=jax.ShapeDtypeStruct((batch, seq, LANES), F32),
        compiler_params=_params(1),
        name="fox_cumsum",
    )(o32v, fb_pad)


def _split3(c):
    hi = c.astype(BF16).astype(F32)
    r1 = c - hi
    mid = r1.astype(BF16).astype(F32)
    return hi, mid, (r1 - mid).astype(BF16).astype(F32)


def _fox_kernel(q_ref, k_ref, v_ref, cum_ref, o_ref, m_scr, acc_scr, s_scr, kaug_scr, vaug_scr, *, seq, tq, nh):
    hp = pl.program_id(1)
    lane = lax.broadcasted_iota(jnp.int32, (tq, LANES), 1)
    lane_s = lax.broadcasted_iota(jnp.int32, (seq, LANES), 1)
    causal = lax.broadcasted_iota(jnp.int32, (tq, tq), 1) <= lax.broadcasted_iota(jnp.int32, (tq, tq), 0)
    for j in range(nh):
        ck = jnp.sum(jnp.where(lane_s == hp * nh + j, cum_ref[0], 0.0), axis=1, keepdims=True) * LOG2E
        hi, mid, lo = _split3(ck)
        tail = jnp.where(lane_s < 3, 1.0, jnp.where(lane_s == 3, -hi, jnp.where(lane_s == 4, -mid,
                         jnp.where(lane_s == 5, -lo, 0.0))))
        for c in range(seq // tq):
            rows = slice(c * tq, (c + 1) * tq)
            kaug_scr[j, c, 0:HEAD_DIM, :] = k_ref[0, rows, j * HEAD_DIM:(j + 1) * HEAD_DIM].T
            kaug_scr[j, c, HEAD_DIM:, :] = tail[rows].T.astype(BF16)
        vaug_scr[j, :, 0:HEAD_DIM] = v_ref[0, :, j * HEAD_DIM:(j + 1) * HEAD_DIM]
        vaug_scr[j, :, HEAD_DIM:] = jnp.ones((seq, HEAD_DIM), BF16)

    def q_body(qi, carry):
        q0 = pl.multiple_of(qi * tq, tq)
        cum_t = cum_ref[0, pl.ds(q0, tq), :]
        qs = []
        for j in range(nh):
            cq = jnp.sum(jnp.where(lane == hp * nh + j, cum_t, 0.0), axis=1, keepdims=True) * LOG2E
            hi, mid, lo = _split3(cq)
            tail = jnp.where(lane == 0, hi, jnp.where(lane == 1, mid, jnp.where(lane == 2, lo,
                             jnp.where(lane < 6, 1.0, 0.0))))
            qs.append(jnp.concatenate([q_ref[0, pl.ds(q0, tq), j * HEAD_DIM:(j + 1) * HEAD_DIM],
                                       tail.astype(BF16)], axis=1))
            m_scr[j] = jnp.full((tq, LANES), NEG, F32)
            acc_scr[j] = jnp.zeros((tq, 2 * HEAD_DIM), F32)

        def scores(j, ki):
            return jnp.dot(qs[j], kaug_scr[j, ki], preferred_element_type=F32)

        def tile(ki, diagonal):
            k0 = pl.multiple_of(ki * tq, tq)
            for j in range(nh):
                s = s_scr[j]
                if diagonal:
                    s = jnp.where(causal, s, NEG)
                else:
                    s_scr[j] = scores(j, ki + 1)
                m_prev = m_scr[j]
                m_new = jnp.maximum(m_prev, jnp.max(s, axis=1, keepdims=True))
                alpha = jnp.exp2(m_prev - m_new)
                p = jnp.exp2(s - _lane_tile(m_new, tq // LANES))
                pv = jnp.dot(p.astype(BF16), vaug_scr[j, pl.ds(k0, tq), :], preferred_element_type=F32)
                acc_scr[j] = _lane_tile(alpha, 2) * acc_scr[j] + pv
                m_scr[j] = m_new

        def k_body(ki, c2):
            tile(ki, False)
            return c2

        for j in range(nh):
            s_scr[j] = scores(j, 0)
        lax.fori_loop(0, qi, k_body, 0)
        tile(qi, True)
        for j in range(nh):
            acc = acc_scr[j]
            o_ref[0, pl.ds(q0, tq), j * HEAD_DIM:(j + 1) * HEAD_DIM] = acc[:, 0:HEAD_DIM] / acc[:, HEAD_DIM:]
        return carry

    lax.fori_loop(0, seq // tq, q_body, 0)


def _fox_call(o16v, cum, batch, seq, tq):
    nh = 4
    groups = N_FOX_HEADS // nh
    w = nh * HEAD_DIM
    return pl.pallas_call(
        functools.partial(_fox_kernel, seq=seq, tq=tq, nh=nh),
        grid=(batch, groups),
        in_specs=[pl.BlockSpec((1, seq, w), lambda b, h: (b, 0, h)),
                  pl.BlockSpec((1, seq, w), lambda b, h: (b, 0, groups + h)),
                  pl.BlockSpec((1, seq, w), lambda b, h: (b, 0, 2 * groups + h)),
                  pl.BlockSpec((1, seq, LANES), lambda b, h: (b, 0, 0))],
        out_specs=pl.BlockSpec((1, seq, w), lambda b, h: (b, 0, h)),
        out_shape=jax.ShapeDtypeStruct((batch, seq, FOX_WIDTH), F32),
        scratch_shapes=[pltpu.VMEM((nh, tq, LANES), F32),
                        pltpu.VMEM((nh, tq, 2 * HEAD_DIM), F32),
                        pltpu.VMEM((nh, tq, tq), F32),
                        pltpu.VMEM((nh, seq // tq, 2 * HEAD_DIM, tq), BF16),
                        pltpu.VMEM((nh, seq, 2 * HEAD_DIM), BF16)],
        compiler_params=_params(2),
        name="fox_attention",
    )(o16v, o16v, o16v, cum)


def _cmp_kernel(x_ref, pos_ref, w1_ref, w2_ref, gain_ref, o_ref, xs_scr, *, seq, ncp):
    kv = pl.program_id(1)
    rows = xs_scr.shape[0]
    xs_scr[0:seq, :] = x_ref[0]
    xs_scr[seq:rows, :] = jnp.zeros((rows - seq, LANES), F32)
    acc = jnp.zeros((ncp, HEAD_DIM), F32)
    for l in range(CMP_BLOCK):
        xl = xs_scr[pl.ds(l, ncp, stride=CMP_STRIDE), :] + pos_ref[0, l:l + 1, :]
        acc = acc + jnp.dot(xl.astype(BF16), w1_ref[0, l], preferred_element_type=F32)
    hmid = acc * _sigmoid(acc)
    y = jnp.dot(hmid.astype(BF16), w2_ref[0], preferred_element_type=F32)
    yk = _rms(y) * gain_ref[...]
    o_ref[0, 0, 0] = jnp.where(kv == 0, yk, y).astype(BF16)


def _cmp_call(o32, pos, w1, w2, gain, batch, seq, ncp):
    o32v = o32.reshape(batch, seq, W32)
    hkv = N_NSA_KV_HEADS
    return pl.pallas_call(
        functools.partial(_cmp_kernel, seq=seq, ncp=ncp),
        grid=(batch, 2, hkv),
        in_specs=[pl.BlockSpec((1, seq, HEAD_DIM), lambda b, kv, h: (b, 0, kv * hkv + h)),
                  pl.BlockSpec((1, CMP_BLOCK, HEAD_DIM), lambda b, kv, h: (kv, 0, 0)),
                  pl.BlockSpec((1, CMP_BLOCK, HEAD_DIM, HEAD_DIM), lambda b, kv, h: (kv, 0, 0, 0)),
                  pl.BlockSpec((1, HEAD_DIM, HEAD_DIM), lambda b, kv, h: (kv, 0, 0)),
                  pl.BlockSpec((1, HEAD_DIM), lambda b, kv, h: (0, 0))],
        out_specs=pl.BlockSpec((1, 1, 1, ncp, HEAD_DIM), lambda b, kv, h: (b, kv, h, 0, 0)),
        out_shape=jax.ShapeDtypeStruct((batch, 2, hkv, ncp, HEAD_DIM), BF16),
        scratch_shapes=[pltpu.VMEM((CMP_STRIDE * ncp + CMP_BLOCK, LANES), F32)],
        compiler_params=_params(3),
        name="nsa_compress",
    )(o32v, pos, w1, w2, gain)


def _bias_kernel(rb_ref, o_ref, *, width, key_stride, key_offset, first_tile, rolled_tiles):
    v = pl.program_id(0) + first_tile
    i = lax.broadcasted_iota(jnp.int32, (LANES, width), 0)
    j = lax.broadcasted_iota(jnp.int32, (LANES, width), 1)
    d = v * LANES + i - (key_stride * j + key_offset)
    n = jnp.maximum(d, 0)
    max_exact = N_BUCKETS // 2
    nf = jnp.maximum(n, 1).astype(F32)
    large = max_exact + (jnp.log(nf / max_exact) / math.log(MAX_DISTANCE / max_exact)
                         * (N_BUCKETS - max_exact)).astype(jnp.int32)
    large = jnp.minimum(large, N_BUCKETS - 1)
    bkt = jnp.where(n < max_exact, n, large)
    vals = [jnp.zeros((LANES, width), F32) for _ in range(N_NSA_HEADS)]
    for bk in range(N_BUCKETS):
        hit = bkt == bk
        for h in range(N_NSA_HEADS):
            vals[h] = jnp.where(hit, rb_ref[bk * N_NSA_HEADS + h] * LOG2E, vals[h])
    for h in range(N_NSA_HEADS):
        g = h % NSA_GROUP
        rows = slice(g * LANES, (g + 1) * LANES)
        if rolled_tiles:
            per_tile = LANES // key_stride
            for t in range(rolled_tiles):
                shift = (width - per_tile * (rolled_tiles - 1 - t)) % width
                o_ref[t, h // NSA_GROUP, rows, :] = vals[h] if shift == 0 else pltpu.roll(vals[h], shift, 1)
        else:
            o_ref[0, h // NSA_GROUP, rows, :] = vals[h]


def _bias_call(rb_flat, n_tiles, width, key_stride, key_offset, name, rolled=False):
    kern = functools.partial(_bias_kernel, width=width, key_stride=key_stride, key_offset=key_offset,
                             first_tile=n_tiles - 1 if rolled else 0, rolled_tiles=n_tiles if rolled else 0)
    block = (n_tiles if rolled else 1, N_NSA_KV_HEADS, GROUP_ROWS, width)
    return pl.pallas_call(
        kern,
        grid=(1 if rolled else n_tiles,),
        in_specs=[pl.BlockSpec(memory_space=pltpu.SMEM)],
        out_specs=pl.BlockSpec(block, lambda v: (v, 0, 0, 0)),
        out_shape=jax.ShapeDtypeStruct((n_tiles, N_NSA_KV_HEADS, GROUP_ROWS, width), F32),
        compiler_params=_params(1),
        name=name,
    )(rb_flat)


def _tile4(a):
    return jnp.concatenate([a] * NSA_GROUP, axis=0)


def _nsa_kernel(q_ref, ks_ref, kw_ref, vs_ref, vw_ref, g_ref, kcvc_ref, bc_ref, tz_ref, selmt_ref,
                wa_ref, wb_ref, wc_ref, wd_ref, o_ref, wa_out, wb_out, wc_out, wd_out,
                qg_scr, m_scr, acc_scr, o_scr, s_scr, ksa_scr, vsa_scr, kwp_scr, vwa_scr, *, seq, ncp):
    qt = pl.program_id(1)
    q0 = qt * LANES
    n_slc = seq // SEL_BLOCK
    top_n = min(N_SEL, n_slc)
    nkv = N_NSA_KV_HEADS
    for src, dst in ((wa_ref, wa_out), (wb_ref, wb_out), (wc_ref, wc_out), (wd_ref, wd_out)):
        dst[...] = src[...].astype(BF16)
    ri = lax.broadcasted_iota(jnp.int32, (LANES, LANES), 0)
    ci = lax.broadcasted_iota(jnp.int32, (LANES, LANES), 1)
    rk = lax.broadcasted_iota(jnp.int32, (LANES, KEY_CHUNK), 0)
    ck = lax.broadcasted_iota(jnp.int32, (LANES, KEY_CHUNK), 1)
    eye = jnp.where(ri == ci, 1.0, 0.0).astype(BF16)
    gates = _sigmoid(g_ref[0])

    def gate_col(hk, br):
        cols = []
        for g in range(NSA_GROUP):
            c = COL_NG % LANES + (hk * NSA_GROUP + g) * N_BRANCH + br
            cols.append(gates[:, c:c + 1])
        return jnp.concatenate(cols, axis=0)

    @pl.when(qt == 0)
    def _():
        ones = jnp.ones((seq, HEAD_DIM), BF16)
        blk_row = lax.broadcasted_iota(jnp.int32, (LANES, KEY_CHUNK), 0)
        for hk in range(nkv):
            hc = slice(hk * HEAD_DIM, (hk + 1) * HEAD_DIM)
            for c in range(seq // KEY_CHUNK):
                rows = slice(c * KEY_CHUNK, (c + 1) * KEY_CHUNK)
                ksa_scr[hk, c, 0:HEAD_DIM, :] = ks_ref[0, rows, hc].T
                ksa_scr[hk, c, HEAD_DIM:, :] = jnp.where((c * KEY_CHUNK + ck) // SEL_BLOCK == blk_row,
                                                         -MASK_BIG, 0.0).astype(BF16)
            for t in range(WIN_PAD // LANES):
                kwp_scr[hk, t] = jnp.zeros((HEAD_DIM, LANES), BF16)
            for t in range(seq // LANES):
                kwp_scr[hk, WIN_PAD // LANES + t] = kw_ref[0, t * LANES:(t + 1) * LANES, hc].T
            vsa_scr[hk, :, 0:HEAD_DIM] = vs_ref[0, :, hc]
            vsa_scr[hk, :, HEAD_DIM:] = ones
            vwa_scr[hk, 0:WIN_PAD, :] = jnp.zeros((WIN_PAD, 2 * HEAD_DIM), BF16)
            vwa_scr[hk, WIN_PAD:, 0:HEAD_DIM] = vw_ref[0, :, hc]
            vwa_scr[hk, WIN_PAD:, HEAD_DIM:] = ones

    def reset(br):
        m_scr[br] = jnp.full(m_scr.shape[1:], NEG, F32)
        acc_scr[br] = jnp.zeros(acc_scr.shape[1:], F32)

    def online_update(br, hk, s, vaug):
        m_prev = m_scr[br, hk]
        m_new = jnp.maximum(m_prev, jnp.max(s, axis=1, keepdims=True))
        alpha = jnp.exp2(m_prev - m_new)
        p = jnp.exp2(s - _lane_tile(m_new, s.shape[1] // LANES))
        pv = jnp.dot(p.astype(BF16), vaug, preferred_element_type=F32)
        acc_scr[br, hk] = _lane_tile(alpha, 2) * acc_scr[br, hk] + pv
        m_scr[br, hk] = m_new

    def finish(br, hk):
        acc = acc_scr[br, hk]
        return acc[:, 0:HEAD_DIM] / acc[:, HEAD_DIM:]

    def chunk_bias(hk, ta, tb):
        return jnp.concatenate([tz_ref[jnp.clip(ta, 0, 2), hk], tz_ref[jnp.clip(tb, 0, 2), hk]], axis=1)

    for hk in range(nkv):
        for g in range(NSA_GROUP):
            h = hk * NSA_GROUP + g
            qg_scr[hk, g * LANES:(g + 1) * LANES, 0:HEAD_DIM] = q_ref[0, :, h * HEAD_DIM:(h + 1) * HEAD_DIM]

    for hk in range(nkv):
        qg = qg_scr[hk, :, 0:HEAD_DIM]

        kc = kcvc_ref[0, 0, hk]
        vc = kcvc_ref[0, 1, hk]
        s = lax.dot_general(qg, kc, NT_DIMS, preferred_element_type=F32) + bc_ref[0, hk]
        rc = lax.broadcasted_iota(jnp.int32, (LANES, ncp), 0)
        cc = lax.broadcasted_iota(jnp.int32, (LANES, ncp), 1)
        valid_c = _tile4(jnp.where(q0 + rc - (CMP_STRIDE * cc + CMP_BLOCK - 1) >= 0, 1.0, 0.0)) > 0.5
        s = jnp.where(valid_c, s, NEG)
        p = jnp.where(valid_c, jnp.exp2(s - jnp.max(s, axis=1, keepdims=True)), 0.0)
        l = jnp.sum(p, axis=1, keepdims=True)
        p = p / jnp.where(l > 0.0, l, 1.0)
        o_scr[hk] = gate_col(hk, 0) * jnp.dot(p.astype(BF16), vc, preferred_element_type=F32)

        psum = p[0:LANES]
        for g in range(1, NSA_GROUP):
            psum = psum + p[g * LANES:(g + 1) * LANES]
        p_hi = psum.astype(BF16)
        p_lo = (psum - p_hi.astype(F32)).astype(BF16)
        selmt = selmt_ref[...]
        imp = (lax.dot_general(selmt, p_hi, NT_DIMS, preferred_element_type=F32)
               + lax.dot_general(selmt, p_lo, NT_DIMS, preferred_element_type=F32))
        imp = imp[0:n_slc]
        blk = lax.broadcasted_iota(jnp.int32, (n_slc, LANES), 0)
        cur = (q0 + lax.broadcasted_iota(jnp.int32, (n_slc, LANES), 1)) // SEL_BLOCK
        forced = (blk == 0) | (blk == cur) | (blk == cur - 1)
        imp = jnp.where(forced, FORCE, imp)
        imp = jnp.where(blk <= cur, imp, -jnp.inf)
        rank = jnp.zeros((n_slc, LANES), F32)
        for j in range(n_slc):
            row = imp[j:j + 1, :]
            beats = jnp.where(row > imp, 1.0, jnp.where(row == imp, jnp.where(blk > j, 1.0, 0.0), 0.0))
            rank = rank + beats
        sel_t = jnp.where(rank < top_n, jnp.where(imp > -jnp.inf, 1.0, 0.0), 0.0)
        if n_slc < LANES:
            sel_t = jnp.concatenate([sel_t, jnp.zeros((LANES - n_slc, LANES), F32)], axis=0)
        sel_q = lax.dot_general(eye, sel_t.astype(BF16), NT_DIMS, preferred_element_type=F32)
        not_sel = jnp.where(ci < n_slc, 1.0 - sel_q, 0.0).astype(BF16)
        for g in range(NSA_GROUP):
            qg_scr[hk, g * LANES:(g + 1) * LANES, HEAD_DIM:] = not_sel

    reset(WIN)
    for off, n_tiles in ((1, 2), (3, 2), (5, 2)):
        width = n_tiles * LANES
        kstart = (qt - off) * LANES
        pos = kstart + lax.broadcasted_iota(jnp.int32, (LANES, width), 1)
        d = (q0 + lax.broadcasted_iota(jnp.int32, (LANES, width), 0)) - pos
        ok = (d >= 0) & (d < WINDOW) & (pos >= 0)
        madd = _tile4(jnp.where(ok, 0.0, -MASK_BIG))
        t0 = qt - off + WIN_PAD // LANES
        p0 = pl.multiple_of(kstart + WIN_PAD, LANES)
        for hk in range(nkv):
            k_t = jnp.concatenate([kwp_scr[hk, t0 + t] for t in range(n_tiles)], axis=1)
            bias = jnp.concatenate([tz_ref[min(off - t, 2), hk] for t in range(n_tiles)], axis=1)
            s = jnp.dot(qg_scr[hk, :, 0:HEAD_DIM], k_t, preferred_element_type=F32) + bias + madd
            online_update(WIN, hk, s, vwa_scr[hk, pl.ds(p0, width), :])

    reset(SEL)

    def sel_scores(hk, c):
        ta = qt - 2 * c
        return jnp.dot(qg_scr[hk], ksa_scr[hk, c], preferred_element_type=F32) + chunk_bias(hk, ta, ta - 1)

    def sel_chunk(c, last):
        k0 = pl.multiple_of(c * KEY_CHUNK, KEY_CHUNK)
        for hk in range(nkv):
            s = s_scr[hk]
            if last:
                s = jnp.where(_tile4(jnp.where((k0 + ck) <= (q0 + rk), 1.0, 0.0)) > 0.5, s, -MASK_BIG)
            else:
                s_scr[hk] = sel_scores(hk, c + 1)
            online_update(SEL, hk, s, vsa_scr[hk, pl.ds(k0, KEY_CHUNK), :])

    def sel_body(c, carry):
        sel_chunk(c, False)
        return carry

    n_chunks = (qt + 2) // 2
    for hk in range(nkv):
        s_scr[hk] = sel_scores(hk, 0)
    lax.fori_loop(0, n_chunks - 1, sel_body, 0)
    sel_chunk(n_chunks - 1, True)
    for hk in range(nkv):
        o = (o_scr[hk] + gate_col(hk, 1) * finish(SEL, hk)) + gate_col(hk, 2) * finish(WIN, hk)
        for g in range(NSA_GROUP):
            h = hk * NSA_GROUP + g
            o_ref[0, :, h * HEAD_DIM:(h + 1) * HEAD_DIM] = o[g * LANES:(g + 1) * LANES]


def _nsa_call(o16v, o32v, kcvc, bias_c, tz, selmt, weights_f32, batch, seq, ncp):
    nqt = seq // LANES
    wspecs = []
    for w in weights_f32:
        blk, span = _cast_block(w, batch * nqt)
        wspecs.append(pl.BlockSpec(blk, functools.partial(lambda b, t, span: ((b * nqt + t) // span, 0), span=span)))
    kvw = NSA_KV_WIDTH
    nkv = N_NSA_KV_HEADS
    base = (3 * FOX_WIDTH + NSA_WIDTH) // kvw
    return pl.pallas_call(
        functools.partial(_nsa_kernel, seq=seq, ncp=ncp),
        grid=(batch, nqt),
        in_specs=[pl.BlockSpec((1, LANES, NSA_WIDTH), lambda b, t: (b, t, 3 * FOX_WIDTH // NSA_WIDTH)),
                  pl.BlockSpec((1, seq, kvw), lambda b, t: (b, 0, base)),
                  pl.BlockSpec((1, seq, kvw), lambda b, t: (b, 0, base + 1)),
                  pl.BlockSpec((1, seq, kvw), lambda b, t: (b, 0, base + 2)),
                  pl.BlockSpec((1, seq, kvw), lambda b, t: (b, 0, base + 3)),
                  pl.BlockSpec((1, LANES, LANES), lambda b, t: (b, t, COL_NG // LANES)),
                  pl.BlockSpec((1, 2, N_NSA_KV_HEADS, ncp, HEAD_DIM), lambda b, t: (b, 0, 0, 0, 0)),
                  pl.BlockSpec((1, N_NSA_KV_HEADS, GROUP_ROWS, ncp), lambda b, t: (t, 0, 0, 0)),
                  pl.BlockSpec((3, N_NSA_KV_HEADS, GROUP_ROWS, LANES), lambda b, t: (0, 0, 0, 0)),
                  pl.BlockSpec((LANES, ncp), lambda b, t: (0, 0))] + wspecs,
        out_specs=[pl.BlockSpec((1, LANES, NSA_WIDTH), lambda b, t: (b, t, 0))] + wspecs,
        out_shape=[jax.ShapeDtypeStruct((batch, seq, NSA_WIDTH), F32)]
        + [jax.ShapeDtypeStruct(w.shape, BF16) for w in weights_f32],
        scratch_shapes=[pltpu.VMEM((nkv, GROUP_ROWS, 2 * HEAD_DIM), BF16),
                        pltpu.VMEM((2, nkv, GROUP_ROWS, LANES), F32),
                        pltpu.VMEM((2, nkv, GROUP_ROWS, 2 * HEAD_DIM), F32),
                        pltpu.VMEM((nkv, GROUP_ROWS, HEAD_DIM), F32),
                        pltpu.VMEM((nkv, GROUP_ROWS, KEY_CHUNK), F32),
                        pltpu.VMEM((nkv, seq // KEY_CHUNK, 2 * HEAD_DIM, KEY_CHUNK), BF16),
                        pltpu.VMEM((nkv, seq, 2 * HEAD_DIM), BF16),
                        pltpu.VMEM((nkv, (seq + WIN_PAD) // LANES, HEAD_DIM, LANES), BF16),
                        pltpu.VMEM((nkv, seq + WIN_PAD, 2 * HEAD_DIM), BF16)],
        compiler_params=_params(2),
        name="nsa_attention",
    )(o16v, o16v, o16v, o16v, o16v, o32v, kcvc, bias_c, tz, selmt, *weights_f32)


def _out_kernel(of_ref, on_ref, gain_ref, w_ref, x_ref, g_ref, o_ref, *, tm, n_chains):
    rows_per_chain = tm // n_chains
    for c in range(n_chains):
        rows = slice(c * rows_per_chain, (c + 1) * rows_per_chain)
        y = jnp.concatenate([(_rms(of_ref[rows, :]) * gain_ref[:, 0:FOX_WIDTH]).astype(BF16),
                             (_rms(on_ref[rows, :]) * gain_ref[:, FOX_WIDTH:MIX_WIDTH]).astype(BF16)], axis=1)
        acc = jnp.dot(y, w_ref[...], preferred_element_type=F32)
        o_ref[rows, :] = x_ref[rows, :] + g_ref[0] * acc


def _out_call(o_fox, o_nsa, gain, w_out, x2, g1, seq):
    t, d = x2.shape
    tm = min(512, seq)
    rows_per_batch = seq // tm
    return pl.pallas_call(
        functools.partial(_out_kernel, tm=tm, n_chains=2),
        grid=(t // tm,),
        in_specs=[pl.BlockSpec((tm, FOX_WIDTH), lambda i: (i, 0)),
                  pl.BlockSpec((tm, NSA_WIDTH), lambda i: (i, 0)),
                  pl.BlockSpec((1, MIX_WIDTH), lambda i: (0, 0)),
                  pl.BlockSpec((MIX_WIDTH, d), lambda i: (0, 0)),
                  pl.BlockSpec((tm, d), lambda i: (i, 0)),
                  pl.BlockSpec((1, 1, d), lambda i: (i // rows_per_batch, 0, 0))],
        out_specs=pl.BlockSpec((tm, d), lambda i: (i, 0)),
        out_shape=jax.ShapeDtypeStruct((t, d), F32),
        compiler_params=_params(1),
        name="out_proj",
    )(o_fox, o_nsa, gain, w_out, x2, g1)


def _ffn_kernel(x_ref, gain_ref, sc_ref, sh_ref, g_ref, wg_ref, wu_ref, wd_ref, o_ref, h_scr, acc_scr):
    f = pl.program_id(1)

    @pl.when(f == 0)
    def _():
        h = _rms(x_ref[...]) * gain_ref[...]
        h = h * (1.0 + sc_ref[0]) + sh_ref[0]
        h_scr[...] = h.astype(BF16)
        acc_scr[...] = jnp.zeros(acc_scr.shape, F32)

    h = h_scr[...]
    a = jnp.dot(h, wg_ref[...], preferred_element_type=F32)
    u = jnp.dot(h, wu_ref[...], preferred_element_type=F32)
    t = (a * _sigmoid(a)) * u
    acc_scr[...] += jnp.dot(t.astype(BF16), wd_ref[...], preferred_element_type=F32)

    @pl.when(f == pl.num_programs(1) - 1)
    def _():
        o_ref[...] = x_ref[...] + g_ref[0] * acc_scr[...]


def _ffn_call(x1, gain2, sc, sh, g2, wg, wu, wd, seq):
    t, d = x1.shape
    dff = wg.shape[1]
    tm = min(512, seq)
    tf = 512 if dff % 512 == 0 else dff
    rows_per_batch = seq // tm
    return pl.pallas_call(
        _ffn_kernel,
        grid=(t // tm, dff // tf),
        in_specs=[pl.BlockSpec((tm, d), lambda i, f: (i, 0)),
                  pl.BlockSpec((1, d), lambda i, f: (0, 0)),
                  pl.BlockSpec((1, 1, d), lambda i, f: (i // rows_per_batch, 0, 0)),
                  pl.BlockSpec((1, 1, d), lambda i, f: (i // rows_per_batch, 0, 0)),
                  pl.BlockSpec((1, 1, d), lambda i, f: (i // rows_per_batch, 0, 0)),
                  pl.BlockSpec((d, tf), lambda i, f: (0, f)),
                  pl.BlockSpec((d, tf), lambda i, f: (0, f)),
                  pl.BlockSpec((tf, d), lambda i, f: (f, 0))],
        out_specs=pl.BlockSpec((tm, d), lambda i, f: (i, 0)),
        out_shape=jax.ShapeDtypeStruct((t, d), F32),
        scratch_shapes=[pltpu.VMEM((tm, d), BF16), pltpu.VMEM((tm, d), F32)],
        compiler_params=_params(2),
        name="swiglu_ffn",
    )(x1, gain2, sc, sh, g2, wg, wu, wd)


def _selection_matrix_t(ncp, n_slc):
    r, q = SEL_BLOCK // CMP_STRIDE, CMP_BLOCK // CMP_STRIDE
    m = np.zeros((LANES, ncp), np.float32)
    for j in range(n_slc):
        for a in range(r):
            for b in range(q):
                c = r * j + a - b
                if 0 <= c < ncp:
                    m[j, c] += 1.0
    return m


def _w_in_block_sources():
    o = 0
    start = {}
    for name, width in (("fq", FOX_WIDTH), ("fk", FOX_WIDTH), ("fv", FOX_WIDTH), ("ff", N_FOX_HEADS),
                        ("nq", NSA_WIDTH), ("nk", N_BRANCH * NSA_KV_WIDTH), ("nv", N_BRANCH * NSA_KV_WIDTH),
                        ("ng", N_BRANCH * N_NSA_HEADS)):
        start[name] = o
        o += width
    kvw = NSA_KV_WIDTH
    groups = [(start["fq"], 3 * FOX_WIDTH), (start["nq"], NSA_WIDTH),
              (start["nk"] + kvw, 2 * kvw), (start["nv"] + kvw, 2 * kvw),
              (start["nk"], kvw), (start["nv"], kvw)]
    blocks = [s + LANES * b for s, width in groups for b in range(width // LANES)]
    return blocks, (start["ff"], N_FOX_HEADS), (start["ng"], N_BRANCH * N_NSA_HEADS)


def _repack_kernel(src_ref, wt_hbm, o16_ref, o32_ref, buf, sem, *, n16, n_whole, ff, ng):
    k = pl.program_id(0)
    n_slots = buf.shape[0]
    ahead = n_slots - 1
    slot = k % n_slots

    def whole_copy(kk, s):
        r0 = pl.multiple_of(src_ref[kk], 8)
        return pltpu.make_async_copy(wt_hbm.at[pl.ds(r0, LANES), :], buf.at[s], sem.at[s])

    def narrow_copies(s):
        return [pltpu.make_async_copy(wt_hbm.at[pl.ds(ff[0], ff[1]), :], buf.at[s, pl.ds(0, ff[1]), :], sem.at[s]),
                pltpu.make_async_copy(wt_hbm.at[pl.ds(ng[0], ng[1]), :], buf.at[s, pl.ds(ff[1], ng[1]), :],
                                      sem.at[s])]

    def start_block(kk):
        @pl.when(kk < n_whole)
        def _():
            whole_copy(kk, kk % n_slots).start()

        @pl.when(kk == n_whole)
        def _():
            for cp in narrow_copies(kk % n_slots):
                cp.start()

    @pl.when(k == 0)
    def _():
        for kk in range(ahead):
            start_block(kk)

    start_block(k + ahead)

    @pl.when(k < n_whole)
    def _():
        whole_copy(k, slot).wait()

    @pl.when(k == n_whole)
    def _():
        for cp in narrow_copies(slot):
            cp.wait()

    xt = buf[slot].T
    lane = lax.broadcasted_iota(jnp.int32, xt.shape, 1)
    xt = jnp.where(jnp.logical_or(k < n_whole, lane < ff[1] + ng[1]), xt, 0.0).astype(BF16)

    @pl.when(k < n16)
    def _():
        o16_ref[...] = xt

    @pl.when(k >= n16)
    def _():
        o32_ref[...] = xt


def _repack_w_in(wt):
    n, d = wt.shape
    blocks, ff, ng = _w_in_block_sources()
    n16 = W16 // LANES
    n_whole = len(blocks)
    assert n_whole + 1 == (W16 + W32) // LANES
    return pl.pallas_call(
        functools.partial(_repack_kernel, n16=n16, n_whole=n_whole, ff=ff, ng=ng),
        grid_spec=pltpu.PrefetchScalarGridSpec(
            num_scalar_prefetch=1,
            grid=(n_whole + 1,),
            in_specs=[pl.BlockSpec(memory_space=pl.ANY)],
            out_specs=[pl.BlockSpec((d, LANES), lambda k, src: (0, jnp.minimum(k, n16 - 1))),
                       pl.BlockSpec((d, LANES), lambda k, src: (0, jnp.maximum(k - n16, 0)))],
            scratch_shapes=[pltpu.VMEM((REPACK_SLOTS, LANES, d), F32), pltpu.SemaphoreType.DMA((REPACK_SLOTS,))]),
        out_shape=[jax.ShapeDtypeStruct((d, W16), BF16), jax.ShapeDtypeStruct((d, W32), BF16)],
        compiler_params=_params(1),
        name="w_in_repack",
    )(jnp.asarray(blocks, jnp.int32), wt)


def kernel(x, c, ada_w, ada_b, norm1_gain, norm2_gain, w_in, fox_f_bias, fox_q_gain, fox_k_gain, nsa_q_gain,
           nsa_k_gain, nsa_cmp_pos, nsa_cmp_w1, nsa_cmp_w2, rel_bias, mix_out_gain, w_out, ffn_w_gate, ffn_w_up,
           ffn_w_down):
    batch, seq, d = x.shape
    assert seq % KEY_CHUNK == 0 and seq >= WINDOW and d % LANES == 0 and seq // SEL_BLOCK <= LANES
    depth = ada_w.shape[0]
    nqt = seq // LANES
    ncp = -(-(seq // CMP_STRIDE) // LANES) * LANES
    tq_fox = min(256, seq)

    selmt = jnp.asarray(_selection_matrix_t(ncp, seq // SEL_BLOCK), BF16)
    rb_flat = rel_bias.reshape(-1)
    bias_c = _bias_call(rb_flat, nqt, ncp, CMP_STRIDE, CMP_BLOCK - 1, "t5_bias_compressed", rolled=True)
    tz = _bias_call(rb_flat, 3, LANES, 1, 0, "t5_bias_toeplitz")

    ones_h = jnp.ones((HEAD_DIM,), F32)
    c_pad = jnp.pad(c, ((0, 8 - batch % 8 if batch % 8 else 0), (0, 0)))
    x2 = x.reshape(batch * seq, d)
    for layer in range(depth):
        mod = _ada_call(c_pad, ada_w[layer], ada_b[layer][None, :])[:batch]
        sh1, sc1, g1, sh2, sc2, g2 = [mod[:, i * d:(i + 1) * d][:, None, :] for i in range(N_MOD)]

        kg = nsa_k_gain[layer]
        col_gain = jnp.concatenate([
            jnp.tile(fox_q_gain[layer] * QSCALE, N_FOX_HEADS), jnp.tile(fox_k_gain[layer], N_FOX_HEADS),
            jnp.tile(ones_h, N_FOX_HEADS), jnp.tile(nsa_q_gain[layer] * QSCALE, N_NSA_HEADS),
            jnp.tile(kg[1], N_NSA_KV_HEADS), jnp.tile(kg[2], N_NSA_KV_HEADS),
            jnp.tile(ones_h, 2 * N_NSA_KV_HEADS)])[None, :]
        col_flag = jnp.concatenate([
            jnp.ones((2 * FOX_WIDTH,), F32), jnp.zeros((FOX_WIDTH,), F32), jnp.ones((NSA_WIDTH,), F32),
            jnp.ones((2 * NSA_KV_WIDTH,), F32), jnp.zeros((2 * NSA_KV_WIDTH,), F32)])[None, :]
        w16, w32 = _repack_w_in(jnp.swapaxes(w_in, 1, 2)[layer])
        o16, o32 = _proj_call(x2, sc1, sh1, norm1_gain[layer][None, :], w16, w32,
                              col_gain, col_flag, seq)
        o16v = o16.reshape(batch, seq, W16)
        o32v = o32.reshape(batch, seq, W32)

        fb_pad = jnp.pad(fox_f_bias[layer], (0, LANES - N_FOX_HEADS))[None, :]
        cum = _cum_call(o32, fb_pad, batch, seq)
        o_fox = _fox_call(o16v, cum, batch, seq, tq_fox)

        w1 = nsa_cmp_w1[layer].reshape(2, CMP_BLOCK, HEAD_DIM, HEAD_DIM).astype(BF16)
        kcvc = _cmp_call(o32, nsa_cmp_pos[layer], w1, nsa_cmp_w2[layer].astype(BF16), kg[0][None, :],
                         batch, seq, ncp)
        o_nsa, wg16, wu16, wo16, wd16 = _nsa_call(
            o16v, o32v, kcvc, bias_c, tz, selmt,
            (ffn_w_gate[layer], ffn_w_up[layer], w_out[layer], ffn_w_down[layer]), batch, seq, ncp)

        x1 = _out_call(o_fox.reshape(batch * seq, FOX_WIDTH), o_nsa.reshape(batch * seq, NSA_WIDTH),
                       mix_out_gain[layer][None, :], wo16, x2, g1, seq)
        x2 = _ffn_call(x1, norm2_gain[layer][None, :], sc2, sh2, g2, wg16, wu16, wd16, seq)
    return x2.reshape(batch, seq, d)
```

```python
import functools
import math

import numpy as np
import jax
import jax.numpy as jnp
from jax import lax
from jax.experimental import pallas as pl
from jax.experimental.pallas import tpu as pltpu

HEAD_DIM = 128
N_FOX_HEADS = 8
N_NSA_HEADS = 8
N_NSA_KV_HEADS = 2
NSA_GROUP = N_NSA_HEADS // N_NSA_KV_HEADS
FOX_WIDTH = N_FOX_HEADS * HEAD_DIM
NSA_WIDTH = N_NSA_HEADS * HEAD_DIM
NSA_KV_WIDTH = N_NSA_KV_HEADS * HEAD_DIM
MIX_WIDTH = FOX_WIDTH + NSA_WIDTH
N_BRANCH = 3
CMP_BLOCK = 32
CMP_STRIDE = 16
SEL_BLOCK = 64
N_SEL = 8
WINDOW = 512
N_BUCKETS = 32
MAX_DISTANCE = 128
N_MOD = 6
SCALE = HEAD_DIM ** -0.5
LOG2E = math.log2(math.e)
QSCALE = SCALE * LOG2E
EPS = 1e-6
NEG = -1e30
FORCE = 1e6

LANES = 128
GROUP_ROWS = NSA_GROUP * LANES
VMEM_LIMIT = 56 * 1024 * 1024
MXU_COLS = 256
KEY_CHUNK = MXU_COLS
WIN_PAD = WINDOW + LANES
MASK_BIG = 2.0 ** 100
SEL, WIN = 0, 1
REPACK_SLOTS = 4
AUG_FAR, AUG_PAD = 120, 127
SEL_PAD_TILES = 1

W16 = 3 * FOX_WIDTH + NSA_WIDTH + 4 * NSA_KV_WIDTH
W32 = 5 * LANES
COL_FF = 2 * NSA_KV_WIDTH
COL_NG = COL_FF + N_FOX_HEADS

F32 = jnp.float32
BF16 = jnp.bfloat16
NT_DIMS = (((1,), (1,)), ((), ()))


def _params(n_axes):
    return pltpu.CompilerParams(dimension_semantics=("arbitrary",) * n_axes,
                                vmem_limit_bytes=VMEM_LIMIT)


def _sigmoid(x):
    return 1.0 / (1.0 + jnp.exp(-x))


def _lane_tile(a, n):
    return jnp.concatenate([a] * n, axis=1)


BF16_SUBLANES = 16


def _cast_block(w, n_steps):
    rows, cols = w.shape
    assert rows % n_steps == 0, (w.shape, n_steps)
    per_step = rows // n_steps
    span = BF16_SUBLANES // math.gcd(BF16_SUBLANES, per_step)
    assert n_steps % span == 0, (w.shape, n_steps)
    return (per_step * span, cols), span


def _rms(x):
    return x * lax.rsqrt(jnp.mean(x * x, axis=-1, keepdims=True) + EPS)


NORM_ROWS = 16


def _modulated_norm(x_ref, gain_ref, sc_ref, sh_ref, h_ref):
    gm = gain_ref[...] * (1.0 + sc_ref[0])
    sh = sh_ref[0]
    for r0 in range(0, x_ref.shape[0], NORM_ROWS):
        rows = slice(r0, r0 + NORM_ROWS)
        h_ref[rows, :] = (_rms(x_ref[rows, :]) * gm + sh).astype(BF16)


def _ada_kernel(c_ref, w_ref, b_ref, o_ref):
    c = c_ref[...]
    s = (c * _sigmoid(c)).astype(BF16)
    o_ref[...] = jnp.dot(s, w_ref[...].astype(BF16), preferred_element_type=F32) + b_ref[...]


def _ada_call(c_pad, w, b):
    rows, d = c_pad.shape
    n = w.shape[1]
    tn = next(t for t in (1024, 768, 512, 384, 256, 128) if n % t == 0)
    return pl.pallas_call(
        _ada_kernel,
        grid=(n // tn,),
        in_specs=[pl.BlockSpec((rows, d), lambda j: (0, 0)),
                  pl.BlockSpec((d, tn), lambda j: (0, j)),
                  pl.BlockSpec((1, tn), lambda j: (0, j))],
        out_specs=pl.BlockSpec((rows, tn), lambda j: (0, j)),
        out_shape=jax.ShapeDtypeStruct((rows, n), F32),
        compiler_params=_params(1),
        name="adaln",
    )(c_pad, w, b)


def _proj_kernel(x_ref, sc_ref, sh_ref, g_ref, w16_ref, w32_ref, gain_ref, flag_ref, o16_ref, o32_ref, h_scr, *, tn):
    j = pl.program_id(1)

    @pl.when(j == 0)
    def _():
        _modulated_norm(x_ref, g_ref, sc_ref, sh_ref, h_scr)

    h = h_scr[...]
    for c in range(tn // MXU_COLS):
        acc = jnp.dot(h, w16_ref[:, c * MXU_COLS:(c + 1) * MXU_COLS], preferred_element_type=F32)
        for g in range(MXU_COLS // LANES):
            cols = slice(c * MXU_COLS + g * LANES, c * MXU_COLS + (g + 1) * LANES)
            a = acc[:, g * LANES:(g + 1) * LANES]
            r = lax.rsqrt(jnp.mean(a * a, axis=-1, keepdims=True) + EPS)
            scale = jnp.where(flag_ref[:, cols] > 0.5, r, 1.0)
            o16_ref[:, cols] = (a * scale * gain_ref[:, cols]).astype(BF16)

    @pl.when(j == pl.num_programs(1) - 1)
    def _():
        o32_ref[...] = jnp.dot(h_scr[...], w32_ref[...], preferred_element_type=F32)


def _proj_call(x2, sc, sh, gain1, w16, w32, col_gain, col_flag, seq):
    t, d = x2.shape
    tm = min(1024, seq)
    tn = 1024
    rows_per_batch = seq // tm
    return pl.pallas_call(
        functools.partial(_proj_kernel, tn=tn),
        grid=(t // tm, W16 // tn),
        in_specs=[pl.BlockSpec((tm, d), lambda i, j: (i, 0)),
                  pl.BlockSpec((1, 1, d), lambda i, j: (i // rows_per_batch, 0, 0)),
                  pl.BlockSpec((1, 1, d), lambda i, j: (i // rows_per_batch, 0, 0)),
                  pl.BlockSpec((1, d), lambda i, j: (0, 0)),
                  pl.BlockSpec((d, tn), lambda i, j: (0, j)),
                  pl.BlockSpec((d, W32), lambda i, j: (0, 0)),
                  pl.BlockSpec((1, tn), lambda i, j: (0, j)),
                  pl.BlockSpec((1, tn), lambda i, j: (0, j))],
        out_specs=[pl.BlockSpec((tm, tn), lambda i, j: (i, j)),
                   pl.BlockSpec((tm, W32), lambda i, j: (i, 0))],
        out_shape=[jax.ShapeDtypeStruct((t, W16), BF16),
                   jax.ShapeDtypeStruct((t, W32), F32)],
        scratch_shapes=[pltpu.VMEM((tm, d), BF16)],
        compiler_params=_params(2),
        name="in_proj",
    )(x2, sc, sh, gain1, w16, w32, col_gain, col_flag)


def _cum_kernel(ff_ref, fb_ref, o_ref, *, seq):
    ri = lax.broadcasted_iota(jnp.int32, (LANES, LANES), 0)
    ci = lax.broadcasted_iota(jnp.int32, (LANES, LANES), 1)
    tri = jnp.where(ri >= ci, 1.0, 0.0).astype(BF16)
    carry = jnp.zeros((1, LANES), F32)
    for blk in range(seq // LANES):
        rows = slice(blk * LANES, (blk + 1) * LANES)
        x = ff_ref[0, rows, :] + fb_ref[...]
        lf = jnp.minimum(x, 0.0) - jnp.log(1.0 + jnp.exp(-jnp.abs(x)))
        hi = lf.astype(BF16)
        r1 = lf - hi.astype(F32)
        mid = r1.astype(BF16)
        lo = (r1 - mid.astype(F32)).astype(BF16)
        c = (jnp.dot(tri, hi, preferred_element_type=F32)
             + jnp.dot(tri, mid, preferred_element_type=F32)
             + jnp.dot(tri, lo, preferred_element_type=F32)) + carry
        o_ref[0, rows, :] = c
        carry = c[LANES - 1:LANES, :]


def _cum_call(o32, fb_pad, batch, seq):
    o32v = o32.reshape(batch, seq, W32)
    return pl.pallas_call(
        functools.partial(_cum_kernel, seq=seq),
        grid=(batch,),
        in_specs=[pl.BlockSpec((1, seq, LANES), lambda b: (b, 0, COL_FF // LANES)),
                  pl.BlockSpec((1, LANES), lambda b: (0, 0))],
        out_specs=pl.BlockSpec((1, seq, LANES), lambda b: (b, 0, 0)),
        out_shape=jax.ShapeDtypeStruct((batch, seq, LANES), F32),
        compiler_params=_params(1),
        name="fox_cumsum",
    )(o32v, fb_pad)


def _split3(c):
    hi = c.astype(BF16).astype(F32)
    r1 = c - hi
    mid = r1.astype(BF16).astype(F32)
    return hi, mid, (r1 - mid).astype(BF16).astype(F32)


def _fox_kernel(q_ref, k_ref, v_ref, cum_ref, o_ref, m_scr, acc_scr, s_scr, kaug_scr, vaug_scr, *, seq, tq, nh):
    hp = pl.program_id(1)
    lane = lax.broadcasted_iota(jnp.int32, (tq, LANES), 1)
    causal = lax.broadcasted_iota(jnp.int32, (tq, tq), 1) <= lax.broadcasted_iota(jnp.int32, (tq, tq), 0)
    for j in range(nh):
        for c in range(seq // tq):
            rows = slice(c * tq, (c + 1) * tq)
            ck = jnp.sum(jnp.where(lane == hp * nh + j, cum_ref[0, rows, :], 0.0), axis=1, keepdims=True) * LOG2E
            hi, mid, lo = _split3(ck)
            tail = jnp.where(lane < 3, 1.0, jnp.where(lane == 3, -hi, jnp.where(lane == 4, -mid,
                             jnp.where(lane == 5, -lo, 0.0))))
            kaug_scr[j, c, 0:HEAD_DIM, :] = k_ref[0, rows, j * HEAD_DIM:(j + 1) * HEAD_DIM].T
            kaug_scr[j, c, HEAD_DIM:, :] = tail.T.astype(BF16)
        vaug_scr[j, :, 0:HEAD_DIM] = v_ref[0, :, j * HEAD_DIM:(j + 1) * HEAD_DIM]
        vaug_scr[j, :, HEAD_DIM:] = jnp.ones((seq, HEAD_DIM), BF16)

    def q_body(qi, carry):
        q0 = pl.multiple_of(qi * tq, tq)
        cum_t = cum_ref[0, pl.ds(q0, tq), :]
        qs = []
        for j in range(nh):
            cq = jnp.sum(jnp.where(lane == hp * nh + j, cum_t, 0.0), axis=1, keepdims=True) * LOG2E
            hi, mid, lo = _split3(cq)
            tail = jnp.where(lane == 0, hi, jnp.where(lane == 1, mid, jnp.where(lane == 2, lo,
                             jnp.where(lane < 6, 1.0, 0.0))))
            qs.append(jnp.concatenate([q_ref[0, pl.ds(q0, tq), j * HEAD_DIM:(j + 1) * HEAD_DIM],
                                       tail.astype(BF16)], axis=1))
            m_scr[j] = jnp.full((tq, LANES), NEG, F32)
            acc_scr[j] = jnp.zeros((tq, 2 * HEAD_DIM), F32)

        def scores(j, ki):
            return jnp.dot(qs[j], kaug_scr[j, ki], preferred_element_type=F32)

        def tile(ki, diagonal):
            k0 = pl.multiple_of(ki * tq, tq)
            for j in range(nh):
                s = s_scr[j]
                if diagonal:
                    s = jnp.where(causal, s, NEG)
                else:
                    s_scr[j] = scores(j, ki + 1)
                m_prev = m_scr[j]
                m_new = jnp.maximum(m_prev, jnp.max(s, axis=1, keepdims=True))
                alpha = jnp.exp2(m_prev - m_new)
                p = jnp.exp2(s - _lane_tile(m_new, tq // LANES))
                pv = jnp.dot(p.astype(BF16), vaug_scr[j, pl.ds(k0, tq), :], preferred_element_type=F32)
                acc_scr[j] = _lane_tile(alpha, 2) * acc_scr[j] + pv
                m_scr[j] = m_new

        def k_body(ki, c2):
            tile(ki, False)
            return c2

        for j in range(nh):
            s_scr[j] = scores(j, 0)
        lax.fori_loop(0, qi, k_body, 0)
        tile(qi, True)
        for j in range(nh):
            acc = acc_scr[j]
            o_ref[0, pl.ds(q0, tq), j * HEAD_DIM:(j + 1) * HEAD_DIM] = acc[:, 0:HEAD_DIM] / acc[:, HEAD_DIM:]
        return carry

    lax.fori_loop(0, seq // tq, q_body, 0)


def _fox_call(o16v, cum, batch, seq, tq):
    nh = 4
    groups = N_FOX_HEADS // nh
    w = nh * HEAD_DIM
    return pl.pallas_call(
        functools.partial(_fox_kernel, seq=seq, tq=tq, nh=nh),
        grid=(batch, groups),
        in_specs=[pl.BlockSpec((1, seq, w), lambda b, h: (b, 0, h)),
                  pl.BlockSpec((1, seq, w), lambda b, h: (b, 0, groups + h)),
                  pl.BlockSpec((1, seq, w), lambda b, h: (b, 0, 2 * groups + h)),
                  pl.BlockSpec((1, seq, LANES), lambda b, h: (b, 0, 0))],
        out_specs=pl.BlockSpec((1, seq, w), lambda b, h: (b, 0, h)),
        out_shape=jax.ShapeDtypeStruct((batch, seq, FOX_WIDTH), F32),
        scratch_shapes=[pltpu.VMEM((nh, tq, LANES), F32),
                        pltpu.VMEM((nh, tq, 2 * HEAD_DIM), F32),
                        pltpu.VMEM((nh, tq, tq), F32),
                        pltpu.VMEM((nh, seq // tq, 2 * HEAD_DIM, tq), BF16),
                        pltpu.VMEM((nh, seq, 2 * HEAD_DIM), BF16)],
        compiler_params=_params(2),
        name="fox_attention",
    )(o16v, o16v, o16v, cum)


def _cmp_kernel(x_ref, pos_ref, w1_ref, w2_ref, gain_ref, o_ref, xs_scr, *, seq, ncp):
    kv = pl.program_id(1)
    rows = xs_scr.shape[0]
    xs_scr[0:seq, :] = x_ref[0]
    xs_scr[seq:rows, :] = jnp.zeros((rows - seq, LANES), F32)
    acc = jnp.zeros((ncp, HEAD_DIM), F32)
    for l in range(CMP_BLOCK):
        xl = xs_scr[pl.ds(l, ncp, stride=CMP_STRIDE), :] + pos_ref[0, l:l + 1, :]
        acc = acc + jnp.dot(xl.astype(BF16), w1_ref[0, l], preferred_element_type=F32)
    hmid = acc * _sigmoid(acc)
    y = jnp.dot(hmid.astype(BF16), w2_ref[0], preferred_element_type=F32)
    yk = _rms(y) * gain_ref[...]
    o_ref[0, 0, 0] = jnp.where(kv == 0, yk, y).astype(BF16)


def _cmp_call(o32, pos, w1, w2, gain, batch, seq, ncp):
    o32v = o32.reshape(batch, seq, W32)
    hkv = N_NSA_KV_HEADS
    return pl.pallas_call(
        functools.partial(_cmp_kernel, seq=seq, ncp=ncp),
        grid=(batch, 2, hkv),
        in_specs=[pl.BlockSpec((1, seq, HEAD_DIM), lambda b, kv, h: (b, 0, kv * hkv + h)),
                  pl.BlockSpec((1, CMP_BLOCK, HEAD_DIM), lambda b, kv, h: (kv, 0, 0)),
                  pl.BlockSpec((1, CMP_BLOCK, HEAD_DIM, HEAD_DIM), lambda b, kv, h: (kv, 0, 0, 0)),
                  pl.BlockSpec((1, HEAD_DIM, HEAD_DIM), lambda b, kv, h: (kv, 0, 0)),
                  pl.BlockSpec((1, HEAD_DIM), lambda b, kv, h: (0, 0))],
        out_specs=pl.BlockSpec((1, 1, 1, ncp, HEAD_DIM), lambda b, kv, h: (b, kv, h, 0, 0)),
        out_shape=jax.ShapeDtypeStruct((batch, 2, hkv, ncp, HEAD_DIM), BF16),
        scratch_shapes=[pltpu.VMEM((CMP_STRIDE * ncp + CMP_BLOCK, LANES), F32)],
        compiler_params=_params(3),
        name="nsa_compress",
    )(o32v, pos, w1, w2, gain)


def _bias_kernel(rb_ref, o_ref, *, width, key_stride, key_offset, first_tile, rolled_tiles):
    v = pl.program_id(0) + first_tile
    i = lax.broadcasted_iota(jnp.int32, (LANES, width), 0)
    j = lax.broadcasted_iota(jnp.int32, (LANES, width), 1)
    d = v * LANES + i - (key_stride * j + key_offset)
    n = jnp.maximum(d, 0)
    max_exact = N_BUCKETS // 2
    nf = jnp.maximum(n, 1).astype(F32)
    large = max_exact + (jnp.log(nf / max_exact) / math.log(MAX_DISTANCE / max_exact)
                         * (N_BUCKETS - max_exact)).astype(jnp.int32)
    large = jnp.minimum(large, N_BUCKETS - 1)
    bkt = jnp.where(n < max_exact, n, large)
    vals = [jnp.zeros((LANES, width), F32) for _ in range(N_NSA_HEADS)]
    for bk in range(N_BUCKETS):
        hit = bkt == bk
        for h in range(N_NSA_HEADS):
            vals[h] = jnp.where(hit, rb_ref[bk * N_NSA_HEADS + h] * LOG2E, vals[h])
    for h in range(N_NSA_HEADS):
        g = h % NSA_GROUP
        rows = slice(g * LANES, (g + 1) * LANES)
        if rolled_tiles:
            per_tile = LANES // key_stride
            for t in range(rolled_tiles):
                shift = (width - per_tile * (rolled_tiles - 1 - t)) % width
                o_ref[t, h // NSA_GROUP, rows, :] = vals[h] if shift == 0 else pltpu.roll(vals[h], shift, 1)
        else:
            o_ref[0, h // NSA_GROUP, rows, :] = vals[h]


def _bias_call(rb_flat, n_tiles, width, key_stride, key_offset, name, rolled=False):
    kern = functools.partial(_bias_kernel, width=width, key_stride=key_stride, key_offset=key_offset,
                             first_tile=n_tiles - 1 if rolled else 0, rolled_tiles=n_tiles if rolled else 0)
    block = (n_tiles if rolled else 1, N_NSA_KV_HEADS, GROUP_ROWS, width)
    return pl.pallas_call(
        kern,
        grid=(1 if rolled else n_tiles,),
        in_specs=[pl.BlockSpec(memory_space=pltpu.SMEM)],
        out_specs=pl.BlockSpec(block, lambda v: (v, 0, 0, 0)),
        out_shape=jax.ShapeDtypeStruct((n_tiles, N_NSA_KV_HEADS, GROUP_ROWS, width), F32),
        compiler_params=_params(1),
        name=name,
    )(rb_flat)


def _tile4(a):
    return jnp.concatenate([a] * NSA_GROUP, axis=0)


def _nsa_kernel(q_ref, ks_ref, kw_ref, vs_ref, vw_ref, g_ref, kcvc_ref, bc_ref, tz_ref, selmt_ref,
                wa_ref, wb_ref, wc_ref, wd_ref, o_ref, wa_out, wb_out, wc_out, wd_out,
                qg_scr, qw_scr, caug_scr, near_scr, edge_scr, m_scr, acc_scr, o_scr, s_scr,
                ksa_scr, vsa_scr, kwp_scr, vwa_scr, *, seq, ncp):
    qt = pl.program_id(1)
    q0 = qt * LANES
    n_slc = seq // SEL_BLOCK
    top_n = min(N_SEL, n_slc)
    nkv = N_NSA_KV_HEADS
    for src, dst in ((wa_ref, wa_out), (wb_ref, wb_out), (wc_ref, wc_out), (wd_ref, wd_out)):
        dst[...] = src[...].astype(BF16)
    ri = lax.broadcasted_iota(jnp.int32, (LANES, LANES), 0)
    ci = lax.broadcasted_iota(jnp.int32, (LANES, LANES), 1)
    eye = jnp.where(ri == ci, 1.0, 0.0).astype(BF16)
    gates = _sigmoid(g_ref[0])

    def gate_col(hk, br):
        cols = []
        for g in range(NSA_GROUP):
            c = COL_NG % LANES + (hk * NSA_GROUP + g) * N_BRANCH + br
            cols.append(gates[:, c:c + 1])
        return jnp.concatenate(cols, axis=0)

    @pl.when(qt == 0)
    def _():
        ones = jnp.ones((seq, HEAD_DIM), BF16)
        row = lax.broadcasted_iota(jnp.int32, (LANES, LANES), 0)
        is_far = jnp.logical_and(row >= AUG_FAR, row < AUG_FAR + 3)
        pad_aug = jnp.where(row == AUG_PAD, -MASK_BIG, jnp.where(is_far, 1.0, 0.0)).astype(BF16)
        win_aug = jnp.where(is_far, 1.0, 0.0).astype(BF16)
        lane512 = lax.broadcasted_iota(jnp.int32, (GROUP_ROWS, LANES), 1)
        causal = _tile4(jnp.where(ci <= ri, 0.0, -MASK_BIG))
        edge_scr[...] = jnp.concatenate([jnp.full((GROUP_ROWS, LANES), -MASK_BIG, F32),
                                         _tile4(jnp.where(ri < ci, 0.0, -MASK_BIG))], axis=1)
        for hk in range(nkv):
            hc = slice(hk * HEAD_DIM, (hk + 1) * HEAD_DIM)
            far = tz_ref[2, hk]
            hi, mid, lo = _split3(far)
            caug = jnp.where(lane512 == AUG_FAR, hi, jnp.where(lane512 == AUG_FAR + 1, mid,
                             jnp.where(lane512 == AUG_FAR + 2, lo, jnp.where(lane512 == AUG_PAD, 1.0, 0.0))))
            caug_scr[hk] = caug.astype(BF16)
            qw_scr[hk, :, HEAD_DIM:] = caug.astype(BF16)
            near_scr[hk] = jnp.concatenate([tz_ref[1, hk] - far, (tz_ref[0, hk] - far) + causal], axis=1)
            ksa_scr[hk, 0, 0:HEAD_DIM, :] = jnp.zeros((HEAD_DIM, LANES), BF16)
            ksa_scr[hk, 0, HEAD_DIM:, :] = pad_aug
            for t in range(seq // LANES):
                rows = slice(t * LANES, (t + 1) * LANES)
                ksa_scr[hk, SEL_PAD_TILES + t, 0:HEAD_DIM, :] = ks_ref[0, rows, hc].T
                ksa_scr[hk, SEL_PAD_TILES + t, HEAD_DIM:, :] = jnp.where(
                    (t * LANES + ci) // SEL_BLOCK == ri, -MASK_BIG, jnp.where(is_far, 1.0, 0.0)).astype(BF16)
                kwp_scr[hk, WIN_PAD // LANES + t, 0:HEAD_DIM, :] = kw_ref[0, rows, hc].T
                kwp_scr[hk, WIN_PAD // LANES + t, HEAD_DIM:, :] = win_aug
            for t in range(WIN_PAD // LANES):
                kwp_scr[hk, t, 0:HEAD_DIM, :] = jnp.zeros((HEAD_DIM, LANES), BF16)
                kwp_scr[hk, t, HEAD_DIM:, :] = pad_aug
            vsa_scr[hk, 0:SEL_PAD_TILES * LANES, :] = jnp.zeros((SEL_PAD_TILES * LANES, 2 * HEAD_DIM), BF16)
            vsa_scr[hk, SEL_PAD_TILES * LANES:, 0:HEAD_DIM] = vs_ref[0, :, hc]
            vsa_scr[hk, SEL_PAD_TILES * LANES:, HEAD_DIM:] = ones
            vwa_scr[hk, 0:WIN_PAD, :] = jnp.zeros((WIN_PAD, 2 * HEAD_DIM), BF16)
            vwa_scr[hk, WIN_PAD:, 0:HEAD_DIM] = vw_ref[0, :, hc]
            vwa_scr[hk, WIN_PAD:, HEAD_DIM:] = ones

    def reset(br):
        m_scr[br] = jnp.full(m_scr.shape[1:], NEG, F32)
        acc_scr[br] = jnp.zeros(acc_scr.shape[1:], F32)

    def online_update(br, hk, s, vaug):
        m_prev = m_scr[br, hk]
        m_new = jnp.maximum(m_prev, jnp.max(s, axis=1, keepdims=True))
        alpha = jnp.exp2(m_prev - m_new)
        p = jnp.exp2(s - _lane_tile(m_new, s.shape[1] // LANES))
        pv = jnp.dot(p.astype(BF16), vaug, preferred_element_type=F32)
        acc_scr[br, hk] = _lane_tile(alpha, 2) * acc_scr[br, hk] + pv
        m_scr[br, hk] = m_new

    def finish(br, hk):
        acc = acc_scr[br, hk]
        return acc[:, 0:HEAD_DIM] / acc[:, HEAD_DIM:]

    for hk in range(nkv):
        for g in range(NSA_GROUP):
            h = hk * NSA_GROUP + g
            q_h = q_ref[0, :, h * HEAD_DIM:(h + 1) * HEAD_DIM]
            qg_scr[hk, g * LANES:(g + 1) * LANES, 0:HEAD_DIM] = q_h
            qw_scr[hk, g * LANES:(g + 1) * LANES, 0:HEAD_DIM] = q_h

    for hk in range(nkv):
        qg = qg_scr[hk, :, 0:HEAD_DIM]

        kc = kcvc_ref[0, 0, hk]
        vc = kcvc_ref[0, 1, hk]
        s = lax.dot_general(qg, kc, NT_DIMS, preferred_element_type=F32) + bc_ref[0, hk]
        rc = lax.broadcasted_iota(jnp.int32, (LANES, ncp), 0)
        cc = lax.broadcasted_iota(jnp.int32, (LANES, ncp), 1)
        valid_c = _tile4(jnp.where(q0 + rc - (CMP_STRIDE * cc + CMP_BLOCK - 1) >= 0, 1.0, 0.0)) > 0.5
        s = jnp.where(valid_c, s, NEG)
        p = jnp.where(valid_c, jnp.exp2(s - jnp.max(s, axis=1, keepdims=True)), 0.0)
        l = jnp.sum(p, axis=1, keepdims=True)
        p = p / jnp.where(l > 0.0, l, 1.0)
        o_scr[hk] = gate_col(hk, 0) * jnp.dot(p.astype(BF16), vc, preferred_element_type=F32)

        psum = p[0:LANES]
        for g in range(1, NSA_GROUP):
            psum = psum + p[g * LANES:(g + 1) * LANES]
        p_hi = psum.astype(BF16)
        p_lo = (psum - p_hi.astype(F32)).astype(BF16)
        selmt = selmt_ref[...]
        imp = (lax.dot_general(selmt, p_hi, NT_DIMS, preferred_element_type=F32)
               + lax.dot_general(selmt, p_lo, NT_DIMS, preferred_element_type=F32))
        imp = imp[0:n_slc]
        blk = lax.broadcasted_iota(jnp.int32, (n_slc, LANES), 0)
        cur = (q0 + lax.broadcasted_iota(jnp.int32, (n_slc, LANES), 1)) // SEL_BLOCK
        forced = (blk == 0) | (blk == cur) | (blk == cur - 1)
        imp = jnp.where(forced, FORCE, imp)
        imp = jnp.where(blk <= cur, imp, -jnp.inf)
        rank = jnp.zeros((n_slc, LANES), F32)
        for j in range(n_slc):
            row = imp[j:j + 1, :]
            beats = jnp.where(row > imp, 1.0, jnp.where(row == imp, jnp.where(blk > j, 1.0, 0.0), 0.0))
            rank = rank + beats
        sel_t = jnp.where(rank < top_n, jnp.where(imp > -jnp.inf, 1.0, 0.0), 0.0)
        if n_slc < LANES:
            sel_t = jnp.concatenate([sel_t, jnp.zeros((LANES - n_slc, LANES), F32)], axis=0)
        sel_q = lax.dot_general(eye, sel_t.astype(BF16), NT_DIMS, preferred_element_type=F32)
        not_sel = jnp.where(ci < n_slc, 1.0 - sel_q, 0.0).astype(BF16)
        for g in range(NSA_GROUP):
            rows = slice(g * LANES, (g + 1) * LANES)
            qg_scr[hk, rows, HEAD_DIM:] = jnp.where(ci < n_slc, not_sel, caug_scr[hk, rows, :])

    reset(WIN)
    for off, table in ((1, near_scr), (3, None), (5, edge_scr)):
        t0 = qt - off + WIN_PAD // LANES
        p0 = pl.multiple_of(t0 * LANES, LANES)
        for hk in range(nkv):
            k_t = jnp.concatenate([kwp_scr[hk, t0], kwp_scr[hk, t0 + 1]], axis=1)
            s = jnp.dot(qw_scr[hk], k_t, preferred_element_type=F32)
            if table is not None:
                s = s + (table[hk] if table is near_scr else table[...])
            online_update(WIN, hk, s, vwa_scr[hk, pl.ds(p0, KEY_CHUNK), :])

    reset(SEL)
    n_chunks = (qt + 2) // 2
    first_tile = SEL_PAD_TILES - (qt + 1) % 2

    def sel_scores(hk, c):
        t0 = first_tile + 2 * c
        k_t = jnp.concatenate([ksa_scr[hk, t0], ksa_scr[hk, t0 + 1]], axis=1)
        return jnp.dot(qg_scr[hk], k_t, preferred_element_type=F32)

    def sel_chunk(c, last):
        p0 = pl.multiple_of((first_tile + 2 * c) * LANES, LANES)
        for hk in range(nkv):
            s = s_scr[hk]
            if last:
                s = s + near_scr[hk]
            else:
                s_scr[hk] = sel_scores(hk, c + 1)
            online_update(SEL, hk, s, vsa_scr[hk, pl.ds(p0, KEY_CHUNK), :])

    def sel_body(c, carry):
        sel_chunk(c, False)
        return carry

    for hk in range(nkv):
        s_scr[hk] = sel_scores(hk, 0)
    lax.fori_loop(0, n_chunks - 1, sel_body, 0)
    sel_chunk(n_chunks - 1, True)
    for hk in range(nkv):
        o = (o_scr[hk] + gate_col(hk, 1) * finish(SEL, hk)) + gate_col(hk, 2) * finish(WIN, hk)
        for g in range(NSA_GROUP):
            h = hk * NSA_GROUP + g
            o_ref[0, :, h * HEAD_DIM:(h + 1) * HEAD_DIM] = o[g * LANES:(g + 1) * LANES]


def _nsa_call(o16v, o32v, kcvc, bias_c, tz, selmt, weights_f32, batch, seq, ncp):
    nqt = seq // LANES
    wspecs = []
    for w in weights_f32:
        blk, span = _cast_block(w, batch * nqt)
        wspecs.append(pl.BlockSpec(blk, functools.partial(lambda b, t, span: ((b * nqt + t) // span, 0), span=span)))
    kvw = NSA_KV_WIDTH
    nkv = N_NSA_KV_HEADS
    base = (3 * FOX_WIDTH + NSA_WIDTH) // kvw
    return pl.pallas_call(
        functools.partial(_nsa_kernel, seq=seq, ncp=ncp),
        grid=(batch, nqt),
        in_specs=[pl.BlockSpec((1, LANES, NSA_WIDTH), lambda b, t: (b, t, 3 * FOX_WIDTH // NSA_WIDTH)),
                  pl.BlockSpec((1, seq, kvw), lambda b, t: (b, 0, base)),
                  pl.BlockSpec((1, seq, kvw), lambda b, t: (b, 0, base + 1)),
                  pl.BlockSpec((1, seq, kvw), lambda b, t: (b, 0, base + 2)),
                  pl.BlockSpec((1, seq, kvw), lambda b, t: (b, 0, base + 3)),
                  pl.BlockSpec((1, LANES, LANES), lambda b, t: (b, t, COL_NG // LANES)),
                  pl.BlockSpec((1, 2, N_NSA_KV_HEADS, ncp, HEAD_DIM), lambda b, t: (b, 0, 0, 0, 0)),
                  pl.BlockSpec((1, N_NSA_KV_HEADS, GROUP_ROWS, ncp), lambda b, t: (t, 0, 0, 0)),
                  pl.BlockSpec((3, N_NSA_KV_HEADS, GROUP_ROWS, LANES), lambda b, t: (0, 0, 0, 0)),
                  pl.BlockSpec((LANES, ncp), lambda b, t: (0, 0))] + wspecs,
        out_specs=[pl.BlockSpec((1, LANES, NSA_WIDTH), lambda b, t: (b, t, 0))] + wspecs,
        out_shape=[jax.ShapeDtypeStruct((batch, seq, NSA_WIDTH), F32)]
        + [jax.ShapeDtypeStruct(w.shape, BF16) for w in weights_f32],
        scratch_shapes=[pltpu.VMEM((nkv, GROUP_ROWS, 2 * HEAD_DIM), BF16),
                        pltpu.VMEM((nkv, GROUP_ROWS, 2 * HEAD_DIM), BF16),
                        pltpu.VMEM((nkv, GROUP_ROWS, LANES), BF16),
                        pltpu.VMEM((nkv, GROUP_ROWS, KEY_CHUNK), F32),
                        pltpu.VMEM((GROUP_ROWS, KEY_CHUNK), F32),
                        pltpu.VMEM((2, nkv, GROUP_ROWS, LANES), F32),
                        pltpu.VMEM((2, nkv, GROUP_ROWS, 2 * HEAD_DIM), F32),
                        pltpu.VMEM((nkv, GROUP_ROWS, HEAD_DIM), F32),
                        pltpu.VMEM((nkv, GROUP_ROWS, KEY_CHUNK), F32),
                        pltpu.VMEM((nkv, SEL_PAD_TILES + seq // LANES, 2 * HEAD_DIM, LANES), BF16),
                        pltpu.VMEM((nkv, SEL_PAD_TILES * LANES + seq, 2 * HEAD_DIM), BF16),
                        pltpu.VMEM((nkv, (seq + WIN_PAD) // LANES, 2 * HEAD_DIM, LANES), BF16),
                        pltpu.VMEM((nkv, seq + WIN_PAD, 2 * HEAD_DIM), BF16)],
        compiler_params=_params(2),
        name="nsa_attention",
    )(o16v, o16v, o16v, o16v, o16v, o32v, kcvc, bias_c, tz, selmt, *weights_f32)


def _out_kernel(of_ref, on_ref, gain_ref, w_ref, x_ref, g_ref, o_ref, *, tm, n_chains):
    rows_per_chain = tm // n_chains
    for c in range(n_chains):
        rows = slice(c * rows_per_chain, (c + 1) * rows_per_chain)
        y = jnp.concatenate([(_rms(of_ref[rows, :]) * gain_ref[:, 0:FOX_WIDTH]).astype(BF16),
                             (_rms(on_ref[rows, :]) * gain_ref[:, FOX_WIDTH:MIX_WIDTH]).astype(BF16)], axis=1)
        acc = jnp.dot(y, w_ref[...], preferred_element_type=F32)
        o_ref[rows, :] = x_ref[rows, :] + g_ref[0] * acc


def _out_call(o_fox, o_nsa, gain, w_out, x2, g1, seq):
    t, d = x2.shape
    tm = min(512, seq)
    rows_per_batch = seq // tm
    return pl.pallas_call(
        functools.partial(_out_kernel, tm=tm, n_chains=2),
        grid=(t // tm,),
        in_specs=[pl.BlockSpec((tm, FOX_WIDTH), lambda i: (i, 0)),
                  pl.BlockSpec((tm, NSA_WIDTH), lambda i: (i, 0)),
                  pl.BlockSpec((1, MIX_WIDTH), lambda i: (0, 0)),
                  pl.BlockSpec((MIX_WIDTH, d), lambda i: (0, 0)),
                  pl.BlockSpec((tm, d), lambda i: (i, 0)),
                  pl.BlockSpec((1, 1, d), lambda i: (i // rows_per_batch, 0, 0))],
        out_specs=pl.BlockSpec((tm, d), lambda i: (i, 0)),
        out_shape=jax.ShapeDtypeStruct((t, d), F32),
        compiler_params=_params(1),
        name="out_proj",
    )(o_fox, o_nsa, gain, w_out, x2, g1)


def _ffn_kernel(x_ref, gain_ref, sc_ref, sh_ref, g_ref, wg_ref, wu_ref, wd_ref, o_ref, h_scr, acc_scr):
    f = pl.program_id(1)

    @pl.when(f == 0)
    def _():
        _modulated_norm(x_ref, gain_ref, sc_ref, sh_ref, h_scr)
        acc_scr[...] = jnp.zeros(acc_scr.shape, F32)

    h = h_scr[...]
    a = jnp.dot(h, wg_ref[...], preferred_element_type=F32)
    u = jnp.dot(h, wu_ref[...], preferred_element_type=F32)
    t = (a * _sigmoid(a)) * u
    acc_scr[...] += jnp.dot(t.astype(BF16), wd_ref[...], preferred_element_type=F32)

    @pl.when(f == pl.num_programs(1) - 1)
    def _():
        o_ref[...] = x_ref[...] + g_ref[0] * acc_scr[...]


def _ffn_call(x1, gain2, sc, sh, g2, wg, wu, wd, seq):
    t, d = x1.shape
    dff = wg.shape[1]
    tm = min(512, seq)
    tf = 512 if dff % 512 == 0 else dff
    rows_per_batch = seq // tm
    return pl.pallas_call(
        _ffn_kernel,
        grid=(t // tm, dff // tf),
        in_specs=[pl.BlockSpec((tm, d), lambda i, f: (i, 0)),
                  pl.BlockSpec((1, d), lambda i, f: (0, 0)),
                  pl.BlockSpec((1, 1, d), lambda i, f: (i // rows_per_batch, 0, 0)),
                  pl.BlockSpec((1, 1, d), lambda i, f: (i // rows_per_batch, 0, 0)),
                  pl.BlockSpec((1, 1, d), lambda i, f: (i // rows_per_batch, 0, 0)),
                  pl.BlockSpec((d, tf), lambda i, f: (0, f)),
                  pl.BlockSpec((d, tf), lambda i, f: (0, f)),
                  pl.BlockSpec((tf, d), lambda i, f: (f, 0))],
        out_specs=pl.BlockSpec((tm, d), lambda i, f: (i, 0)),
        out_shape=jax.ShapeDtypeStruct((t, d), F32),
        scratch_shapes=[pltpu.VMEM((tm, d), BF16), pltpu.VMEM((tm, d), F32)],
        compiler_params=_params(2),
        name="swiglu_ffn",
    )(x1, gain2, sc, sh, g2, wg, wu, wd)


def _selection_matrix_t(ncp, n_slc):
    r, q = SEL_BLOCK // CMP_STRIDE, CMP_BLOCK // CMP_STRIDE
    m = np.zeros((LANES, ncp), np.float32)
    for j in range(n_slc):
        for a in range(r):
            for b in range(q):
                c = r * j + a - b
                if 0 <= c < ncp:
                    m[j, c] += 1.0
    return m


def _w_in_block_sources():
    o = 0
    start = {}
    for name, width in (("fq", FOX_WIDTH), ("fk", FOX_WIDTH), ("fv", FOX_WIDTH), ("ff", N_FOX_HEADS),
                        ("nq", NSA_WIDTH), ("nk", N_BRANCH * NSA_KV_WIDTH), ("nv", N_BRANCH * NSA_KV_WIDTH),
                        ("ng", N_BRANCH * N_NSA_HEADS)):
        start[name] = o
        o += width
    kvw = NSA_KV_WIDTH
    groups = [(start["fq"], 3 * FOX_WIDTH), (start["nq"], NSA_WIDTH),
              (start["nk"] + kvw, 2 * kvw), (start["nv"] + kvw, 2 * kvw),
              (start["nk"], kvw), (start["nv"], kvw)]
    blocks = [s + LANES * b for s, width in groups for b in range(width // LANES)]
    return blocks, (start["ff"], N_FOX_HEADS), (start["ng"], N_BRANCH * N_NSA_HEADS)


def _repack_kernel(src_ref, wt_hbm, o16_ref, o32_ref, buf, sem, *, n16, n_whole, ff, ng):
    k = pl.program_id(0)
    n_slots = buf.shape[0]
    ahead = n_slots - 1
    slot = k % n_slots

    def whole_copy(kk, s):
        r0 = pl.multiple_of(src_ref[kk], 8)
        return pltpu.make_async_copy(wt_hbm.at[pl.ds(r0, LANES), :], buf.at[s], sem.at[s])

    def narrow_copies(s):
        return [pltpu.make_async_copy(wt_hbm.at[pl.ds(ff[0], ff[1]), :], buf.at[s, pl.ds(0, ff[1]), :], sem.at[s]),
                pltpu.make_async_copy(wt_hbm.at[pl.ds(ng[0], ng[1]), :], buf.at[s, pl.ds(ff[1], ng[1]), :],
                                      sem.at[s])]

    def start_block(kk):
        @pl.when(kk < n_whole)
        def _():
            whole_copy(kk, kk % n_slots).start()

        @pl.when(kk == n_whole)
        def _():
            for cp in narrow_copies(kk % n_slots):
                cp.start()

    @pl.when(k == 0)
    def _():
        for kk in range(ahead):
            start_block(kk)

    start_block(k + ahead)

    @pl.when(k < n_whole)
    def _():
        whole_copy(k, slot).wait()

    @pl.when(k == n_whole)
    def _():
        for cp in narrow_copies(slot):
            cp.wait()

    xt = buf[slot].T
    lane = lax.broadcasted_iota(jnp.int32, xt.shape, 1)
    xt = jnp.where(jnp.logical_or(k < n_whole, lane < ff[1] + ng[1]), xt, 0.0).astype(BF16)

    @pl.when(k < n16)
    def _():
        o16_ref[...] = xt

    @pl.when(k >= n16)
    def _():
        o32_ref[...] = xt


def _repack_w_in(wt):
    n, d = wt.shape
    blocks, ff, ng = _w_in_block_sources()
    n16 = W16 // LANES
    n_whole = len(blocks)
    assert n_whole + 1 == (W16 + W32) // LANES
    return pl.pallas_call(
        functools.partial(_repack_kernel, n16=n16, n_whole=n_whole, ff=ff, ng=ng),
        grid_spec=pltpu.PrefetchScalarGridSpec(
            num_scalar_prefetch=1,
            grid=(n_whole + 1,),
            in_specs=[pl.BlockSpec(memory_space=pl.ANY)],
            out_specs=[pl.BlockSpec((d, LANES), lambda k, src: (0, jnp.minimum(k, n16 - 1))),
                       pl.BlockSpec((d, LANES), lambda k, src: (0, jnp.maximum(k - n16, 0)))],
            scratch_shapes=[pltpu.VMEM((REPACK_SLOTS, LANES, d), F32), pltpu.SemaphoreType.DMA((REPACK_SLOTS,))]),
        out_shape=[jax.ShapeDtypeStruct((d, W16), BF16), jax.ShapeDtypeStruct((d, W32), BF16)],
        compiler_params=_params(1),
        name="w_in_repack",
    )(jnp.asarray(blocks, jnp.int32), wt)


def kernel(x, c, ada_w, ada_b, norm1_gain, norm2_gain, w_in, fox_f_bias, fox_q_gain, fox_k_gain, nsa_q_gain,
           nsa_k_gain, nsa_cmp_pos, nsa_cmp_w1, nsa_cmp_w2, rel_bias, mix_out_gain, w_out, ffn_w_gate, ffn_w_up,
           ffn_w_down):
    batch, seq, d = x.shape
    assert seq % KEY_CHUNK == 0 and seq >= WINDOW and d % LANES == 0 and seq // SEL_BLOCK <= LANES
    depth = ada_w.shape[0]
    nqt = seq // LANES
    ncp = -(-(seq // CMP_STRIDE) // LANES) * LANES
    tq_fox = min(256, seq)

    selmt = jnp.asarray(_selection_matrix_t(ncp, seq // SEL_BLOCK), BF16)
    rb_flat = rel_bias.reshape(-1)
    bias_c = _bias_call(rb_flat, nqt, ncp, CMP_STRIDE, CMP_BLOCK - 1, "t5_bias_compressed", rolled=True)
    tz = _bias_call(rb_flat, 3, LANES, 1, 0, "t5_bias_toeplitz")

    ones_h = jnp.ones((HEAD_DIM,), F32)
    c_pad = jnp.pad(c, ((0, 8 - batch % 8 if batch % 8 else 0), (0, 0)))
    x2 = x.reshape(batch * seq, d)
    for layer in range(depth):
        mod = _ada_call(c_pad, ada_w[layer], ada_b[layer][None, :])[:batch]
        sh1, sc1, g1, sh2, sc2, g2 = [mod[:, i * d:(i + 1) * d][:, None, :] for i in range(N_MOD)]

        kg = nsa_k_gain[layer]
        col_gain = jnp.concatenate([
            jnp.tile(fox_q_gain[layer] * QSCALE, N_FOX_HEADS), jnp.tile(fox_k_gain[layer], N_FOX_HEADS),
            jnp.tile(ones_h, N_FOX_HEADS), jnp.tile(nsa_q_gain[layer] * QSCALE, N_NSA_HEADS),
            jnp.tile(kg[1], N_NSA_KV_HEADS), jnp.tile(kg[2], N_NSA_KV_HEADS),
            jnp.tile(ones_h, 2 * N_NSA_KV_HEADS)])[None, :]
        col_flag = jnp.concatenate([
            jnp.ones((2 * FOX_WIDTH,), F32), jnp.zeros((FOX_WIDTH,), F32), jnp.ones((NSA_WIDTH,), F32),
            jnp.ones((2 * NSA_KV_WIDTH,), F32), jnp.zeros((2 * NSA_KV_WIDTH,), F32)])[None, :]
        w16, w32 = _repack_w_in(jnp.swapaxes(w_in, 1, 2)[layer])
        o16, o32 = _proj_call(x2, sc1, sh1, norm1_gain[layer][None, :], w16, w32,
                              col_gain, col_flag, seq)
        o16v = o16.reshape(batch, seq, W16)
        o32v = o32.reshape(batch, seq, W32)

        fb_pad = jnp.pad(fox_f_bias[layer], (0, LANES - N_FOX_HEADS))[None, :]
        cum = _cum_call(o32, fb_pad, batch, seq)
        o_fox = _fox_call(o16v, cum, batch, seq, tq_fox)

        w1 = nsa_cmp_w1[layer].reshape(2, CMP_BLOCK, HEAD_DIM, HEAD_DIM).astype(BF16)
        kcvc = _cmp_call(o32, nsa_cmp_pos[layer], w1, nsa_cmp_w2[layer].astype(BF16), kg[0][None, :],
                         batch, seq, ncp)
        o_nsa, wg16, wu16, wo16, wd16 = _nsa_call(
            o16v, o32v, kcvc, bias_c, tz, selmt,
            (ffn_w_gate[layer], ffn_w_up[layer], w_out[layer], ffn_w_down[layer]), batch, seq, ncp)

        x1 = _out_call(o_fox.reshape(batch * seq, FOX_WIDTH), o_nsa.reshape(batch * seq, NSA_WIDTH),
                       mix_out_gain[layer][None, :], wo16, x2, g1, seq)
        x2 = _ffn_call(x1, norm2_gain[layer][None, :], sc2, sh2, g2, wg16, wu16, wd16, seq)
    return x2.reshape(batch, seq, d)
```

```python
import functools
import math

import numpy as np
import jax
import jax.numpy as jnp
from jax import lax
from jax.experimental import pallas as pl
from jax.experimental.pallas import tpu as pltpu

HEAD_DIM = 128
N_FOX_HEADS = 8
N_NSA_HEADS = 8
N_NSA_KV_HEADS = 2
NSA_GROUP = N_NSA_HEADS // N_NSA_KV_HEADS
FOX_WIDTH = N_FOX_HEADS * HEAD_DIM
NSA_WIDTH = N_NSA_HEADS * HEAD_DIM
NSA_KV_WIDTH = N_NSA_KV_HEADS * HEAD_DIM
MIX_WIDTH = FOX_WIDTH + NSA_WIDTH
N_BRANCH = 3
CMP_BLOCK = 32
CMP_STRIDE = 16
SEL_BLOCK = 64
N_SEL = 8
WINDOW = 512
N_BUCKETS = 32
MAX_DISTANCE = 128
N_MOD = 6
SCALE = HEAD_DIM ** -0.5
LOG2E = math.log2(math.e)
LOG2E_HI = float(np.float32(LOG2E))
LOG2E_LO = LOG2E - LOG2E_HI
QSCALE = SCALE * LOG2E
EPS = 1e-6
NEG = -1e30
FORCE = 1e6

LANES = 128
GROUP_ROWS = NSA_GROUP * LANES
VMEM_LIMIT = 56 * 1024 * 1024
MXU_COLS = 256
KEY_CHUNK = MXU_COLS
WIN_PAD = WINDOW + LANES
MASK_BIG = 2.0 ** 100
SEL, WIN = 0, 1
REPACK_SLOTS = 4
AUG_FAR, AUG_PAD = 120, 127
FOX_HEADS_PER_STEP = 4
SEL_PAD_TILES = 1

W16 = 3 * FOX_WIDTH + NSA_WIDTH + 4 * NSA_KV_WIDTH
W32 = 5 * LANES
COL_FF = 2 * NSA_KV_WIDTH
COL_NG = COL_FF + N_FOX_HEADS

F32 = jnp.float32
BF16 = jnp.bfloat16
NT_DIMS = (((1,), (1,)), ((), ()))


def _params(n_axes):
    return pltpu.CompilerParams(dimension_semantics=("arbitrary",) * n_axes,
                                vmem_limit_bytes=VMEM_LIMIT)


def _sigmoid(x):
    return 1.0 / (1.0 + jnp.exp(-x))


def _lane_tile(a, n):
    return jnp.concatenate([a] * n, axis=1)


BF16_SUBLANES = 16


def _cast_block(w, n_steps):
    rows, cols = w.shape
    assert rows % n_steps == 0, (w.shape, n_steps)
    per_step = rows // n_steps
    span = BF16_SUBLANES // math.gcd(BF16_SUBLANES, per_step)
    assert n_steps % span == 0, (w.shape, n_steps)
    return (per_step * span, cols), span


def _rms(x):
    return x * lax.rsqrt(jnp.mean(x * x, axis=-1, keepdims=True) + EPS)


NORM_ROWS = 16


def _modulated_norm(x_ref, gain_ref, sc_ref, sh_ref, h_ref):
    gm = gain_ref[...] * (1.0 + sc_ref[0])
    sh = sh_ref[0]
    for r0 in range(0, x_ref.shape[0], NORM_ROWS):
        rows = slice(r0, r0 + NORM_ROWS)
        h_ref[rows, :] = (_rms(x_ref[rows, :]) * gm + sh).astype(BF16)


def _ada_kernel(c_ref, w_ref, b_ref, o_ref):
    c = c_ref[...]
    s = (c * _sigmoid(c)).astype(BF16)
    o_ref[...] = jnp.dot(s, w_ref[...].astype(BF16), preferred_element_type=F32) + b_ref[...]


def _ada_call(c_pad, w, b):
    rows, d = c_pad.shape
    n = w.shape[1]
    tn = next(t for t in (1024, 768, 512, 384, 256, 128) if n % t == 0)
    return pl.pallas_call(
        _ada_kernel,
        grid=(n // tn,),
        in_specs=[pl.BlockSpec((rows, d), lambda j: (0, 0)),
                  pl.BlockSpec((d, tn), lambda j: (0, j)),
                  pl.BlockSpec((1, tn), lambda j: (0, j))],
        out_specs=pl.BlockSpec((rows, tn), lambda j: (0, j)),
        out_shape=jax.ShapeDtypeStruct((rows, n), F32),
        compiler_params=_params(1),
        name="adaln",
    )(c_pad, w, b)


def _proj_kernel(x_ref, sc_ref, sh_ref, g_ref, w16_ref, w32_ref, gain_ref, flag_ref, o16_ref, o32_ref, h_scr, *, tn):
    j = pl.program_id(1)

    @pl.when(j == 0)
    def _():
        _modulated_norm(x_ref, g_ref, sc_ref, sh_ref, h_scr)

    h = h_scr[...]
    for c in range(tn // MXU_COLS):
        acc = jnp.dot(h, w16_ref[:, c * MXU_COLS:(c + 1) * MXU_COLS], preferred_element_type=F32)
        for g in range(MXU_COLS // LANES):
            cols = slice(c * MXU_COLS + g * LANES, c * MXU_COLS + (g + 1) * LANES)
            a = acc[:, g * LANES:(g + 1) * LANES]
            r = lax.rsqrt(jnp.mean(a * a, axis=-1, keepdims=True) + EPS)
            scale = jnp.where(flag_ref[:, cols] > 0.5, r, 1.0)
            o16_ref[:, cols] = (a * scale * gain_ref[:, cols]).astype(BF16)

    @pl.when(j == pl.num_programs(1) - 1)
    def _():
        o32_ref[...] = jnp.dot(h_scr[...], w32_ref[...], preferred_element_type=F32)


def _proj_call(x2, sc, sh, gain1, w16, w32, col_gain, col_flag, seq):
    t, d = x2.shape
    tm = min(1024, seq)
    tn = 1024
    rows_per_batch = seq // tm
    return pl.pallas_call(
        functools.partial(_proj_kernel, tn=tn),
        grid=(t // tm, W16 // tn),
        in_specs=[pl.BlockSpec((tm, d), lambda i, j: (i, 0)),
                  pl.BlockSpec((1, 1, d), lambda i, j: (i // rows_per_batch, 0, 0)),
                  pl.BlockSpec((1, 1, d), lambda i, j: (i // rows_per_batch, 0, 0)),
                  pl.BlockSpec((1, d), lambda i, j: (0, 0)),
                  pl.BlockSpec((d, tn), lambda i, j: (0, j)),
                  pl.BlockSpec((d, W32), lambda i, j: (0, 0)),
                  pl.BlockSpec((1, tn), lambda i, j: (0, j)),
                  pl.BlockSpec((1, tn), lambda i, j: (0, j))],
        out_specs=[pl.BlockSpec((tm, tn), lambda i, j: (i, j)),
                   pl.BlockSpec((tm, W32), lambda i, j: (i, 0))],
        out_shape=[jax.ShapeDtypeStruct((t, W16), BF16),
                   jax.ShapeDtypeStruct((t, W32), F32)],
        scratch_shapes=[pltpu.VMEM((tm, d), BF16)],
        compiler_params=_params(2),
        name="in_proj",
    )(x2, sc, sh, gain1, w16, w32, col_gain, col_flag)


def _split3(c):
    hi = c.astype(BF16).astype(F32)
    r1 = c - hi
    mid = r1.astype(BF16).astype(F32)
    return hi, mid, (r1 - mid).astype(BF16).astype(F32)


def _cum_kernel(ff_ref, fb_ref, o_ref, *, seq, groups):
    ri = lax.broadcasted_iota(jnp.int32, (LANES, LANES), 0)
    ci = lax.broadcasted_iota(jnp.int32, (LANES, LANES), 1)
    tri = jnp.where(ri >= ci, 1.0, 0.0).astype(BF16)
    carry = jnp.zeros((1, LANES), F32)
    for blk in range(seq // LANES):
        rows = slice(blk * LANES, (blk + 1) * LANES)
        x = ff_ref[0, rows, :] + fb_ref[...]
        lf = jnp.minimum(x, 0.0) - jnp.log(1.0 + jnp.exp(-jnp.abs(x)))
        hi = lf.astype(BF16)
        r1 = lf - hi.astype(F32)
        mid = r1.astype(BF16)
        lo = (r1 - mid.astype(F32)).astype(BF16)
        c = (jnp.dot(tri, hi, preferred_element_type=F32)
             + jnp.dot(tri, mid, preferred_element_type=F32)
             + jnp.dot(tri, lo, preferred_element_type=F32)) + carry
        carry = c[LANES - 1:LANES, :]
        for t, term in enumerate(_split3(c * LOG2E_HI + c * LOG2E_LO)):
            for g in range(groups):
                shift = (-g * (N_FOX_HEADS // groups)) % LANES
                o_ref[0, g, t, rows, :] = term if shift == 0 else pltpu.roll(term, shift, 1)


def _cum_call(o32, fb_pad, batch, seq, groups):
    o32v = o32.reshape(batch, seq, W32)
    return pl.pallas_call(
        functools.partial(_cum_kernel, seq=seq, groups=groups),
        grid=(batch,),
        in_specs=[pl.BlockSpec((1, seq, LANES), lambda b: (b, 0, COL_FF // LANES)),
                  pl.BlockSpec((1, LANES), lambda b: (0, 0))],
        out_specs=pl.BlockSpec((1, groups, 3, seq, LANES), lambda b: (b, 0, 0, 0, 0)),
        out_shape=jax.ShapeDtypeStruct((batch, groups, 3, seq, LANES), F32),
        compiler_params=_params(1),
        name="fox_cumsum",
    )(o32v, fb_pad)


def _fox_kernel(q_ref, k_ref, v_ref, cs_ref, o_ref, m_scr, acc_scr, s_scr, kaug_scr, vaug_scr, *, seq, tq, tk, nh):
    n_diag = tq // tk
    lane_q = lax.broadcasted_iota(jnp.int32, (tq, LANES), 1)
    row8 = lax.broadcasted_iota(jnp.int32, (8, tk), 0)
    ri = lax.broadcasted_iota(jnp.int32, (tq, tk), 0)
    ci = lax.broadcasted_iota(jnp.int32, (tq, tk), 1)
    causal = [ci + d * tk <= ri for d in range(n_diag)]
    for c in range(seq // tk):
        rows = slice(c * tk, (c + 1) * tk)
        terms_t = [cs_ref[0, 0, t, rows, :].T for t in range(3)]
        for j in range(nh):
            tail8 = jnp.where(row8 < 3, 1.0, jnp.where(row8 == 3, -terms_t[0][j:j + 1], jnp.where(
                row8 == 4, -terms_t[1][j:j + 1], jnp.where(row8 == 5, -terms_t[2][j:j + 1], 0.0))))
            kaug_scr[j, c, 0:HEAD_DIM, :] = k_ref[0, rows, j * HEAD_DIM:(j + 1) * HEAD_DIM].T
            kaug_scr[j, c, HEAD_DIM:, :] = jnp.concatenate(
                [tail8, jnp.zeros((HEAD_DIM - 8, tk), F32)], axis=0).astype(BF16)
    for j in range(nh):
        vaug_scr[j, :, 0:HEAD_DIM] = v_ref[0, :, j * HEAD_DIM:(j + 1) * HEAD_DIM]
        vaug_scr[j, :, HEAD_DIM:] = jnp.ones((seq, HEAD_DIM), BF16)

    def q_body(qi, carry):
        q0 = pl.multiple_of(qi * tq, tq)
        terms = [cs_ref[0, 0, t, pl.ds(q0, tq), :] for t in range(3)]
        qs = []
        for j in range(nh):
            hi, mid, lo = [term if t == j else pltpu.roll(term, (t - j) % LANES, 1) for t, term in enumerate(terms)]
            tail = jnp.where(lane_q == 0, hi, jnp.where(lane_q == 1, mid, jnp.where(lane_q == 2, lo,
                             jnp.where(lane_q < 6, 1.0, 0.0))))
            qs.append(jnp.concatenate([q_ref[0, pl.ds(q0, tq), j * HEAD_DIM:(j + 1) * HEAD_DIM],
                                       tail.astype(BF16)], axis=1))
            m_scr[j] = jnp.full((tq, LANES), NEG, F32)
            acc_scr[j] = jnp.zeros((tq, 2 * HEAD_DIM), F32)

        all_rows = slice(0, tq)

        def scores(j, ki, rows):
            return jnp.dot(qs[j][rows], kaug_scr[j, ki], preferred_element_type=F32)

        def tile(ki, mask, rows, next_rows):
            k0 = pl.multiple_of(ki * tk, tk)
            for j in range(nh):
                s = s_scr[j, rows, :]
                if mask is not None:
                    s = jnp.where(mask[rows], s, NEG)
                if next_rows is not None:
                    s_scr[j, next_rows, :] = scores(j, ki + 1, next_rows)
                m_prev = m_scr[j, rows, :]
                m_new = jnp.maximum(m_prev, jnp.max(s, axis=1, keepdims=True))
                alpha = jnp.exp2(m_prev - m_new)
                p = jnp.exp2(s - _lane_tile(m_new, tk // LANES))
                pv = jnp.dot(p.astype(BF16), vaug_scr[j, pl.ds(k0, tk), :], preferred_element_type=F32)
                acc_scr[j, rows, :] = _lane_tile(alpha, 2) * acc_scr[j, rows, :] + pv
                m_scr[j, rows, :] = m_new

        def k_body(ki, c2):
            tile(ki, None, all_rows, all_rows)
            return c2

        for j in range(nh):
            s_scr[j] = scores(j, 0, all_rows)
        n_off = qi * n_diag
        lax.fori_loop(0, n_off, k_body, 0)
        for d in range(n_diag):
            nxt = slice((d + 1) * tk, tq) if d + 1 < n_diag else None
            tile(n_off + d, causal[d], slice(d * tk, tq), nxt)
        for j in range(nh):
            acc = acc_scr[j]
            o_ref[0, pl.ds(q0, tq), j * HEAD_DIM:(j + 1) * HEAD_DIM] = acc[:, 0:HEAD_DIM] / acc[:, HEAD_DIM:]
        return carry

    lax.fori_loop(0, seq // tq, q_body, 0)


def _fox_call(o16v, cs, batch, seq):
    nh = FOX_HEADS_PER_STEP
    tq, tk = min(512, seq), min(256, seq)
    groups = N_FOX_HEADS // nh
    w = nh * HEAD_DIM
    return pl.pallas_call(
        functools.partial(_fox_kernel, seq=seq, tq=tq, tk=tk, nh=nh),
        grid=(batch, groups),
        in_specs=[pl.BlockSpec((1, seq, w), lambda b, h: (b, 0, h)),
                  pl.BlockSpec((1, seq, w), lambda b, h: (b, 0, groups + h)),
                  pl.BlockSpec((1, seq, w), lambda b, h: (b, 0, 2 * groups + h)),
                  pl.BlockSpec((1, 1, 3, seq, LANES), lambda b, h: (b, h, 0, 0, 0))],
        out_specs=pl.BlockSpec((1, seq, w), lambda b, h: (b, 0, h)),
        out_shape=jax.ShapeDtypeStruct((batch, seq, FOX_WIDTH), F32),
        scratch_shapes=[pltpu.VMEM((nh, tq, LANES), F32),
                        pltpu.VMEM((nh, tq, 2 * HEAD_DIM), F32),
                        pltpu.VMEM((nh, tq, tk), F32),
                        pltpu.VMEM((nh, seq // tk, 2 * HEAD_DIM, tk), BF16),
                        pltpu.VMEM((nh, seq, 2 * HEAD_DIM), BF16)],
        compiler_params=_params(2),
        name="fox_attention",
    )(o16v, o16v, o16v, cs)


def _cmp_kernel(x_ref, pos_ref, w1_ref, w2_ref, gain_ref, o_ref, xs_scr, *, seq, ncp):
    kv = pl.program_id(1)
    rows = xs_scr.shape[0]
    xs_scr[0:seq, :] = x_ref[0]
    xs_scr[seq:rows, :] = jnp.zeros((rows - seq, LANES), F32)
    acc = jnp.zeros((ncp, HEAD_DIM), F32)
    for l in range(CMP_BLOCK):
        xl = xs_scr[pl.ds(l, ncp, stride=CMP_STRIDE), :] + pos_ref[0, l:l + 1, :]
        acc = acc + jnp.dot(xl.astype(BF16), w1_ref[0, l], preferred_element_type=F32)
    hmid = acc * _sigmoid(acc)
    y = jnp.dot(hmid.astype(BF16), w2_ref[0], preferred_element_type=F32)
    yk = _rms(y) * gain_ref[...]
    o_ref[0, 0, 0] = jnp.where(kv == 0, yk, y).astype(BF16)


def _cmp_call(o32, pos, w1, w2, gain, batch, seq, ncp):
    o32v = o32.reshape(batch, seq, W32)
    hkv = N_NSA_KV_HEADS
    return pl.pallas_call(
        functools.partial(_cmp_kernel, seq=seq, ncp=ncp),
        grid=(batch, 2, hkv),
        in_specs=[pl.BlockSpec((1, seq, HEAD_DIM), lambda b, kv, h: (b, 0, kv * hkv + h)),
                  pl.BlockSpec((1, CMP_BLOCK, HEAD_DIM), lambda b, kv, h: (kv, 0, 0)),
                  pl.BlockSpec((1, CMP_BLOCK, HEAD_DIM, HEAD_DIM), lambda b, kv, h: (kv, 0, 0, 0)),
                  pl.BlockSpec((1, HEAD_DIM, HEAD_DIM), lambda b, kv, h: (kv, 0, 0)),
                  pl.BlockSpec((1, HEAD_DIM), lambda b, kv, h: (0, 0))],
        out_specs=pl.BlockSpec((1, 1, 1, ncp, HEAD_DIM), lambda b, kv, h: (b, kv, h, 0, 0)),
        out_shape=jax.ShapeDtypeStruct((batch, 2, hkv, ncp, HEAD_DIM), BF16),
        scratch_shapes=[pltpu.VMEM((CMP_STRIDE * ncp + CMP_BLOCK, LANES), F32)],
        compiler_params=_params(3),
        name="nsa_compress",
    )(o32v, pos, w1, w2, gain)


def _bias_kernel(rb_ref, o_ref, *, width, key_stride, key_offset, first_tile, rolled_tiles):
    v = pl.program_id(0) + first_tile
    i = lax.broadcasted_iota(jnp.int32, (LANES, width), 0)
    j = lax.broadcasted_iota(jnp.int32, (LANES, width), 1)
    d = v * LANES + i - (key_stride * j + key_offset)
    n = jnp.maximum(d, 0)
    max_exact = N_BUCKETS // 2
    nf = jnp.maximum(n, 1).astype(F32)
    large = max_exact + jnp.trunc(jnp.log(nf / max_exact) / math.log(MAX_DISTANCE / max_exact)
                                  * (N_BUCKETS - max_exact))
    large = jnp.minimum(large, float(N_BUCKETS - 1))
    bkt = jnp.where(n < max_exact, n.astype(F32), large)
    vals = [jnp.zeros((LANES, width), F32) for _ in range(N_NSA_HEADS)]
    for bk in range(N_BUCKETS):
        hit = bkt == float(bk)
        for h in range(N_NSA_HEADS):
            vals[h] = jnp.where(hit, rb_ref[bk * N_NSA_HEADS + h] * LOG2E, vals[h])
    for h in range(N_NSA_HEADS):
        g = h % NSA_GROUP
        rows = slice(g * LANES, (g + 1) * LANES)
        if rolled_tiles:
            per_tile = LANES // key_stride
            for t in range(rolled_tiles):
                shift = (width - per_tile * (rolled_tiles - 1 - t)) % width
                o_ref[t, h // NSA_GROUP, rows, :] = vals[h] if shift == 0 else pltpu.roll(vals[h], shift, 1)
        else:
            o_ref[0, h // NSA_GROUP, rows, :] = vals[h]


def _bias_call(rb_flat, n_tiles, width, key_stride, key_offset, name, rolled=False):
    kern = functools.partial(_bias_kernel, width=width, key_stride=key_stride, key_offset=key_offset,
                             first_tile=n_tiles - 1 if rolled else 0, rolled_tiles=n_tiles if rolled else 0)
    block = (n_tiles if rolled else 1, N_NSA_KV_HEADS, GROUP_ROWS, width)
    return pl.pallas_call(
        kern,
        grid=(1 if rolled else n_tiles,),
        in_specs=[pl.BlockSpec(memory_space=pltpu.SMEM)],
        out_specs=pl.BlockSpec(block, lambda v: (v, 0, 0, 0)),
        out_shape=jax.ShapeDtypeStruct((n_tiles, N_NSA_KV_HEADS, GROUP_ROWS, width), F32),
        compiler_params=_params(1),
        name=name,
    )(rb_flat)


def _tile4(a):
    return jnp.concatenate([a] * NSA_GROUP, axis=0)


def _nsa_kernel(q_ref, ks_ref, kw_ref, vs_ref, vw_ref, g_ref, kcvc_ref, bc_ref, tz_ref, selmt_ref,
                wa_ref, wb_ref, wc_ref, wd_ref, o_ref, wa_out, wb_out, wc_out, wd_out,
                qg_scr, qw_scr, caug_scr, near_scr, edge_scr, m_scr, acc_scr, o_scr, s_scr,
                ksa_scr, vsa_scr, kwp_scr, vwa_scr, *, seq, ncp):
    qt = pl.program_id(1)
    q0 = qt * LANES
    n_slc = seq // SEL_BLOCK
    top_n = min(N_SEL, n_slc)
    nkv = N_NSA_KV_HEADS
    for src, dst in ((wa_ref, wa_out), (wb_ref, wb_out), (wc_ref, wc_out), (wd_ref, wd_out)):
        dst[...] = src[...].astype(BF16)
    ri = lax.broadcasted_iota(jnp.int32, (LANES, LANES), 0)
    ci = lax.broadcasted_iota(jnp.int32, (LANES, LANES), 1)
    eye = jnp.where(ri == ci, 1.0, 0.0).astype(BF16)
    gates = _sigmoid(g_ref[0])

    def gate_col(hk, br):
        cols = []
        for g in range(NSA_GROUP):
            c = COL_NG % LANES + (hk * NSA_GROUP + g) * N_BRANCH + br
            cols.append(gates[:, c:c + 1])
        return jnp.concatenate(cols, axis=0)

    @pl.when(qt == 0)
    def _():
        ones = jnp.ones((seq, HEAD_DIM), BF16)
        row = lax.broadcasted_iota(jnp.int32, (LANES, LANES), 0)
        is_far = jnp.logical_and(row >= AUG_FAR, row < AUG_FAR + 3)
        pad_aug = jnp.where(row == AUG_PAD, -MASK_BIG, jnp.where(is_far, 1.0, 0.0)).astype(BF16)
        win_aug = jnp.where(is_far, 1.0, 0.0).astype(BF16)
        lane512 = lax.broadcasted_iota(jnp.int32, (GROUP_ROWS, LANES), 1)
        causal = _tile4(jnp.where(ci <= ri, 0.0, -MASK_BIG))
        edge_scr[...] = jnp.concatenate([jnp.full((GROUP_ROWS, LANES), -MASK_BIG, F32),
                                         _tile4(jnp.where(ri < ci, 0.0, -MASK_BIG))], axis=1)
        for hk in range(nkv):
            hc = slice(hk * HEAD_DIM, (hk + 1) * HEAD_DIM)
            far = tz_ref[2, hk]
            hi, mid, lo = _split3(far)
            caug = jnp.where(lane512 == AUG_FAR, hi, jnp.where(lane512 == AUG_FAR + 1, mid,
                             jnp.where(lane512 == AUG_FAR + 2, lo, jnp.where(lane512 == AUG_PAD, 1.0, 0.0))))
            caug_scr[hk] = caug.astype(BF16)
            qw_scr[hk, :, HEAD_DIM:] = caug.astype(BF16)
            near_scr[hk] = jnp.concatenate([tz_ref[1, hk] - far, (tz_ref[0, hk] - far) + causal], axis=1)
            ksa_scr[hk, 0, 0:HEAD_DIM, :] = jnp.zeros((HEAD_DIM, LANES), BF16)
            ksa_scr[hk, 0, HEAD_DIM:, :] = pad_aug
            for t in range(seq // LANES):
                rows = slice(t * LANES, (t + 1) * LANES)
                ksa_scr[hk, SEL_PAD_TILES + t, 0:HEAD_DIM, :] = ks_ref[0, rows, hc].T
                ksa_scr[hk, SEL_PAD_TILES + t, HEAD_DIM:, :] = jnp.where(
                    (t * LANES + ci) // SEL_BLOCK == ri, -MASK_BIG, jnp.where(is_far, 1.0, 0.0)).astype(BF16)
                kwp_scr[hk, WIN_PAD // LANES + t, 0:HEAD_DIM, :] = kw_ref[0, rows, hc].T
                kwp_scr[hk, WIN_PAD // LANES + t, HEAD_DIM:, :] = win_aug
            for t in range(WIN_PAD // LANES):
                kwp_scr[hk, t, 0:HEAD_DIM, :] = jnp.zeros((HEAD_DIM, LANES), BF16)
                kwp_scr[hk, t, HEAD_DIM:, :] = pad_aug
            vsa_scr[hk, 0:SEL_PAD_TILES * LANES, :] = jnp.zeros((SEL_PAD_TILES * LANES, 2 * HEAD_DIM), BF16)
            vsa_scr[hk, SEL_PAD_TILES * LANES:, 0:HEAD_DIM] = vs_ref[0, :, hc]
            vsa_scr[hk, SEL_PAD_TILES * LANES:, HEAD_DIM:] = ones
            vwa_scr[hk, 0:WIN_PAD, :] = jnp.zeros((WIN_PAD, 2 * HEAD_DIM), BF16)
            vwa_scr[hk, WIN_PAD:, 0:HEAD_DIM] = vw_ref[0, :, hc]
            vwa_scr[hk, WIN_PAD:, HEAD_DIM:] = ones

    def reset(br):
        m_scr[br] = jnp.full(m_scr.shape[1:], NEG, F32)
        acc_scr[br] = jnp.zeros(acc_scr.shape[1:], F32)

    def online_update(br, hk, s, vaug):
        m_prev = m_scr[br, hk]
        m_new = jnp.maximum(m_prev, jnp.max(s, axis=1, keepdims=True))
        alpha = jnp.exp2(m_prev - m_new)
        p = jnp.exp2(s - _lane_tile(m_new, s.shape[1] // LANES))
        pv = jnp.dot(p.astype(BF16), vaug, preferred_element_type=F32)
        acc_scr[br, hk] = _lane_tile(alpha, 2) * acc_scr[br, hk] + pv
        m_scr[br, hk] = m_new

    def finish(br, hk):
        acc = acc_scr[br, hk]
        return acc[:, 0:HEAD_DIM] / acc[:, HEAD_DIM:]

    for hk in range(nkv):
        for g in range(NSA_GROUP):
            h = hk * NSA_GROUP + g
            q_h = q_ref[0, :, h * HEAD_DIM:(h + 1) * HEAD_DIM]
            qg_scr[hk, g * LANES:(g + 1) * LANES, 0:HEAD_DIM] = q_h
            qw_scr[hk, g * LANES:(g + 1) * LANES, 0:HEAD_DIM] = q_h

    for hk in range(nkv):
        qg = qg_scr[hk, :, 0:HEAD_DIM]

        kc = kcvc_ref[0, 0, hk]
        vc = kcvc_ref[0, 1, hk]
        s = lax.dot_general(qg, kc, NT_DIMS, preferred_element_type=F32) + bc_ref[0, hk]
        rc = lax.broadcasted_iota(jnp.int32, (LANES, ncp), 0)
        cc = lax.broadcasted_iota(jnp.int32, (LANES, ncp), 1)
        valid_c = _tile4(jnp.where(q0 + rc - (CMP_STRIDE * cc + CMP_BLOCK - 1) >= 0, 1.0, 0.0)) > 0.5
        s = jnp.where(valid_c, s, NEG)
        p = jnp.where(valid_c, jnp.exp2(s - jnp.max(s, axis=1, keepdims=True)), 0.0)
        l = jnp.sum(p, axis=1, keepdims=True)
        p = p / jnp.where(l > 0.0, l, 1.0)
        o_scr[hk] = gate_col(hk, 0) * jnp.dot(p.astype(BF16), vc, preferred_element_type=F32)

        psum = p[0:LANES]
        for g in range(1, NSA_GROUP):
            psum = psum + p[g * LANES:(g + 1) * LANES]
        p_hi = psum.astype(BF16)
        p_lo = (psum - p_hi.astype(F32)).astype(BF16)
        selmt = selmt_ref[...]
        imp = (lax.dot_general(selmt, p_hi, NT_DIMS, preferred_element_type=F32)
               + lax.dot_general(selmt, p_lo, NT_DIMS, preferred_element_type=F32))
        imp = imp[0:n_slc]
        blk = lax.broadcasted_iota(jnp.int32, (n_slc, LANES), 0)
        cur = (q0 + lax.broadcasted_iota(jnp.int32, (n_slc, LANES), 1)) // SEL_BLOCK
        forced = (blk == 0) | (blk == cur) | (blk == cur - 1)
        imp = jnp.where(forced, FORCE, imp)
        imp = jnp.where(blk <= cur, imp, -jnp.inf)
        rank = jnp.zeros((n_slc, LANES), F32)
        for j in range(n_slc):
            row = imp[j:j + 1, :]
            beats = jnp.where(row > imp, 1.0, jnp.where(row == imp, jnp.where(blk > j, 1.0, 0.0), 0.0))
            rank = rank + beats
        sel_t = jnp.where(rank < top_n, jnp.where(imp > -jnp.inf, 1.0, 0.0), 0.0)
        if n_slc < LANES:
            sel_t = jnp.concatenate([sel_t, jnp.zeros((LANES - n_slc, LANES), F32)], axis=0)
        sel_q = lax.dot_general(eye, sel_t.astype(BF16), NT_DIMS, preferred_element_type=F32)
        not_sel = jnp.where(ci < n_slc, 1.0 - sel_q, 0.0).astype(BF16)
        for g in range(NSA_GROUP):
            rows = slice(g * LANES, (g + 1) * LANES)
            qg_scr[hk, rows, HEAD_DIM:] = jnp.where(ci < n_slc, not_sel, caug_scr[hk, rows, :])

    reset(WIN)
    for off, table in ((1, near_scr), (3, None), (5, edge_scr)):
        t0 = qt - off + WIN_PAD // LANES
        p0 = pl.multiple_of(t0 * LANES, LANES)
        for hk in range(nkv):
            k_t = jnp.concatenate([kwp_scr[hk, t0], kwp_scr[hk, t0 + 1]], axis=1)
            s = jnp.dot(qw_scr[hk], k_t, preferred_element_type=F32)
            if table is not None:
                s = s + (table[hk] if table is near_scr else table[...])
            online_update(WIN, hk, s, vwa_scr[hk, pl.ds(p0, KEY_CHUNK), :])

    reset(SEL)
    n_chunks = (qt + 2) // 2
    first_tile = SEL_PAD_TILES - (qt + 1) % 2

    def sel_scores(hk, c):
        t0 = first_tile + 2 * c
        k_t = jnp.concatenate([ksa_scr[hk, t0], ksa_scr[hk, t0 + 1]], axis=1)
        return jnp.dot(qg_scr[hk], k_t, preferred_element_type=F32)

    def sel_chunk(c, last):
        p0 = pl.multiple_of((first_tile + 2 * c) * LANES, LANES)
        for hk in range(nkv):
            s = s_scr[hk]
            if last:
                s = s + near_scr[hk]
            else:
                s_scr[hk] = sel_scores(hk, c + 1)
            online_update(SEL, hk, s, vsa_scr[hk, pl.ds(p0, KEY_CHUNK), :])

    def sel_body(c, carry):
        sel_chunk(c, False)
        return carry

    for hk in range(nkv):
        s_scr[hk] = sel_scores(hk, 0)
    lax.fori_loop(0, n_chunks - 1, sel_body, 0)
    sel_chunk(n_chunks - 1, True)
    for hk in range(nkv):
        o = (o_scr[hk] + gate_col(hk, 1) * finish(SEL, hk)) + gate_col(hk, 2) * finish(WIN, hk)
        for g in range(NSA_GROUP):
            h = hk * NSA_GROUP + g
            o_ref[0, :, h * HEAD_DIM:(h + 1) * HEAD_DIM] = o[g * LANES:(g + 1) * LANES]


def _nsa_call(o16v, o32v, kcvc, bias_c, tz, selmt, weights_f32, batch, seq, ncp):
    nqt = seq // LANES
    wspecs = []
    for w in weights_f32:
        blk, span = _cast_block(w, batch * nqt)
        wspecs.append(pl.BlockSpec(blk, functools.partial(lambda b, t, span: ((b * nqt + t) // span, 0), span=span)))
    kvw = NSA_KV_WIDTH
    nkv = N_NSA_KV_HEADS
    base = (3 * FOX_WIDTH + NSA_WIDTH) // kvw
    return pl.pallas_call(
        functools.partial(_nsa_kernel, seq=seq, ncp=ncp),
        grid=(batch, nqt),
        in_specs=[pl.BlockSpec((1, LANES, NSA_WIDTH), lambda b, t: (b, t, 3 * FOX_WIDTH // NSA_WIDTH)),
                  pl.BlockSpec((1, seq, kvw), lambda b, t: (b, 0, base)),
                  pl.BlockSpec((1, seq, kvw), lambda b, t: (b, 0, base + 1)),
                  pl.BlockSpec((1, seq, kvw), lambda b, t: (b, 0, base + 2)),
                  pl.BlockSpec((1, seq, kvw), lambda b, t: (b, 0, base + 3)),
                  pl.BlockSpec((1, LANES, LANES), lambda b, t: (b, t, COL_NG // LANES)),
                  pl.BlockSpec((1, 2, N_NSA_KV_HEADS, ncp, HEAD_DIM), lambda b, t: (b, 0, 0, 0, 0)),
                  pl.BlockSpec((1, N_NSA_KV_HEADS, GROUP_ROWS, ncp), lambda b, t: (t, 0, 0, 0)),
                  pl.BlockSpec((3, N_NSA_KV_HEADS, GROUP_ROWS, LANES), lambda b, t: (0, 0, 0, 0)),
                  pl.BlockSpec((LANES, ncp), lambda b, t: (0, 0))] + wspecs,
        out_specs=[pl.BlockSpec((1, LANES, NSA_WIDTH), lambda b, t: (b, t, 0))] + wspecs,
        out_shape=[jax.ShapeDtypeStruct((batch, seq, NSA_WIDTH), F32)]
        + [jax.ShapeDtypeStruct(w.shape, BF16) for w in weights_f32],
        scratch_shapes=[pltpu.VMEM((nkv, GROUP_ROWS, 2 * HEAD_DIM), BF16),
                        pltpu.VMEM((nkv, GROUP_ROWS, 2 * HEAD_DIM), BF16),
                        pltpu.VMEM((nkv, GROUP_ROWS, LANES), BF16),
                        pltpu.VMEM((nkv, GROUP_ROWS, KEY_CHUNK), F32),
                        pltpu.VMEM((GROUP_ROWS, KEY_CHUNK), F32),
                        pltpu.VMEM((2, nkv, GROUP_ROWS, LANES), F32),
                        pltpu.VMEM((2, nkv, GROUP_ROWS, 2 * HEAD_DIM), F32),
                        pltpu.VMEM((nkv, GROUP_ROWS, HEAD_DIM), F32),
                        pltpu.VMEM((nkv, GROUP_ROWS, KEY_CHUNK), F32),
                        pltpu.VMEM((nkv, SEL_PAD_TILES + seq // LANES, 2 * HEAD_DIM, LANES), BF16),
                        pltpu.VMEM((nkv, SEL_PAD_TILES * LANES + seq, 2 * HEAD_DIM), BF16),
                        pltpu.VMEM((nkv, (seq + WIN_PAD) // LANES, 2 * HEAD_DIM, LANES), BF16),
                        pltpu.VMEM((nkv, seq + WIN_PAD, 2 * HEAD_DIM), BF16)],
        compiler_params=_params(2),
        name="nsa_attention",
    )(o16v, o16v, o16v, o16v, o16v, o32v, kcvc, bias_c, tz, selmt, *weights_f32)


def _out_kernel(of_ref, on_ref, gain_ref, w_ref, x_ref, g_ref, o_ref, *, tm, n_chains):
    rows_per_chain = tm // n_chains
    for c in range(n_chains):
        rows = slice(c * rows_per_chain, (c + 1) * rows_per_chain)
        y = jnp.concatenate([(_rms(of_ref[rows, :]) * gain_ref[:, 0:FOX_WIDTH]).astype(BF16),
                             (_rms(on_ref[rows, :]) * gain_ref[:, FOX_WIDTH:MIX_WIDTH]).astype(BF16)], axis=1)
        acc = jnp.dot(y, w_ref[...], preferred_element_type=F32)
        o_ref[rows, :] = x_ref[rows, :] + g_ref[0] * acc


def _out_call(o_fox, o_nsa, gain, w_out, x2, g1, seq):
    t, d = x2.shape
    tm = min(512, seq)
    rows_per_batch = seq // tm
    return pl.pallas_call(
        functools.partial(_out_kernel, tm=tm, n_chains=2),
        grid=(t // tm,),
        in_specs=[pl.BlockSpec((tm, FOX_WIDTH), lambda i: (i, 0)),
                  pl.BlockSpec((tm, NSA_WIDTH), lambda i: (i, 0)),
                  pl.BlockSpec((1, MIX_WIDTH), lambda i: (0, 0)),
                  pl.BlockSpec((MIX_WIDTH, d), lambda i: (0, 0)),
                  pl.BlockSpec((tm, d), lambda i: (i, 0)),
                  pl.BlockSpec((1, 1, d), lambda i: (i // rows_per_batch, 0, 0))],
        out_specs=pl.BlockSpec((tm, d), lambda i: (i, 0)),
        out_shape=jax.ShapeDtypeStruct((t, d), F32),
        compiler_params=_params(1),
        name="out_proj",
    )(o_fox, o_nsa, gain, w_out, x2, g1)


def _ffn_kernel(x_ref, gain_ref, sc_ref, sh_ref, g_ref, wg_ref, wu_ref, wd_ref, o_ref, h_scr, acc_scr):
    f = pl.program_id(1)

    @pl.when(f == 0)
    def _():
        _modulated_norm(x_ref, gain_ref, sc_ref, sh_ref, h_scr)
        acc_scr[...] = jnp.zeros(acc_scr.shape, F32)

    h = h_scr[...]
    a = jnp.dot(h, wg_ref[...], preferred_element_type=F32)
    u = jnp.dot(h, wu_ref[...], preferred_element_type=F32)
    t = (a * _sigmoid(a)) * u
    acc_scr[...] += jnp.dot(t.astype(BF16), wd_ref[...], preferred_element_type=F32)

    @pl.when(f == pl.num_programs(1) - 1)
    def _():
        o_ref[...] = x_ref[...] + g_ref[0] * acc_scr[...]


def _ffn_call(x1, gain2, sc, sh, g2, wg, wu, wd, seq):
    t, d = x1.shape
    dff = wg.shape[1]
    tm = min(512, seq)
    tf = 512 if dff % 512 == 0 else dff
    rows_per_batch = seq // tm
    return pl.pallas_call(
        _ffn_kernel,
        grid=(t // tm, dff // tf),
        in_specs=[pl.BlockSpec((tm, d), lambda i, f: (i, 0)),
                  pl.BlockSpec((1, d), lambda i, f: (0, 0)),
                  pl.BlockSpec((1, 1, d), lambda i, f: (i // rows_per_batch, 0, 0)),
                  pl.BlockSpec((1, 1, d), lambda i, f: (i // rows_per_batch, 0, 0)),
                  pl.BlockSpec((1, 1, d), lambda i, f: (i // rows_per_batch, 0, 0)),
                  pl.BlockSpec((d, tf), lambda i, f: (0, f)),
                  pl.BlockSpec((d, tf), lambda i, f: (0, f)),
                  pl.BlockSpec((tf, d), lambda i, f: (f, 0))],
        out_specs=pl.BlockSpec((tm, d), lambda i, f: (i, 0)),
        out_shape=jax.ShapeDtypeStruct((t, d), F32),
        scratch_shapes=[pltpu.VMEM((tm, d), BF16), pltpu.VMEM((tm, d), F32)],
        compiler_params=_params(2),
        name="swiglu_ffn",
    )(x1, gain2, sc, sh, g2, wg, wu, wd)


def _selection_matrix_t(ncp, n_slc):
    r, q = SEL_BLOCK // CMP_STRIDE, CMP_BLOCK // CMP_STRIDE
    m = np.zeros((LANES, ncp), np.float32)
    for j in range(n_slc):
        for a in range(r):
            for b in range(q):
                c = r * j + a - b
                if 0 <= c < ncp:
                    m[j, c] += 1.0
    return m


def _w_in_block_sources():
    o = 0
    start = {}
    for name, width in (("fq", FOX_WIDTH), ("fk", FOX_WIDTH), ("fv", FOX_WIDTH), ("ff", N_FOX_HEADS),
                        ("nq", NSA_WIDTH), ("nk", N_BRANCH * NSA_KV_WIDTH), ("nv", N_BRANCH * NSA_KV_WIDTH),
                        ("ng", N_BRANCH * N_NSA_HEADS)):
        start[name] = o
        o += width
    kvw = NSA_KV_WIDTH
    groups = [(start["fq"], 3 * FOX_WIDTH), (start["nq"], NSA_WIDTH),
              (start["nk"] + kvw, 2 * kvw), (start["nv"] + kvw, 2 * kvw),
              (start["nk"], kvw), (start["nv"], kvw)]
    blocks = [s + LANES * b for s, width in groups for b in range(width // LANES)]
    return blocks, (start["ff"], N_FOX_HEADS), (start["ng"], N_BRANCH * N_NSA_HEADS)


def _repack_kernel(src_ref, wt_hbm, o16_ref, o32_ref, buf, sem, *, n16, n_whole, ff, ng):
    k = pl.program_id(0)
    n_slots = buf.shape[0]
    ahead = n_slots - 1
    slot = k % n_slots

    def whole_copy(kk, s):
        r0 = pl.multiple_of(src_ref[kk], 8)
        return pltpu.make_async_copy(wt_hbm.at[pl.ds(r0, LANES), :], buf.at[s], sem.at[s])

    def narrow_copies(s):
        return [pltpu.make_async_copy(wt_hbm.at[pl.ds(ff[0], ff[1]), :], buf.at[s, pl.ds(0, ff[1]), :], sem.at[s]),
                pltpu.make_async_copy(wt_hbm.at[pl.ds(ng[0], ng[1]), :], buf.at[s, pl.ds(ff[1], ng[1]), :],
                                      sem.at[s])]

    def start_block(kk):
        @pl.when(kk < n_whole)
        def _():
            whole_copy(kk, kk % n_slots).start()

        @pl.when(kk == n_whole)
        def _():
            for cp in narrow_copies(kk % n_slots):
                cp.start()

    @pl.when(k == 0)
    def _():
        for kk in range(ahead):
            start_block(kk)

    start_block(k + ahead)

    @pl.when(k < n_whole)
    def _():
        whole_copy(k, slot).wait()

    @pl.when(k == n_whole)
    def _():
        for cp in narrow_copies(slot):
            cp.wait()

    xt = buf[slot].T
    lane = lax.broadcasted_iota(jnp.int32, xt.shape, 1)
    xt = jnp.where(jnp.logical_or(k < n_whole, lane < ff[1] + ng[1]), xt, 0.0).astype(BF16)

    @pl.when(k < n16)
    def _():
        o16_ref[...] = xt

    @pl.when(k >= n16)
    def _():
        o32_ref[...] = xt


def _repack_w_in(wt):
    n, d = wt.shape
    blocks, ff, ng = _w_in_block_sources()
    n16 = W16 // LANES
    n_whole = len(blocks)
    assert n_whole + 1 == (W16 + W32) // LANES
    return pl.pallas_call(
        functools.partial(_repack_kernel, n16=n16, n_whole=n_whole, ff=ff, ng=ng),
        grid_spec=pltpu.PrefetchScalarGridSpec(
            num_scalar_prefetch=1,
            grid=(n_whole + 1,),
            in_specs=[pl.BlockSpec(memory_space=pl.ANY)],
            out_specs=[pl.BlockSpec((d, LANES), lambda k, src: (0, jnp.minimum(k, n16 - 1))),
                       pl.BlockSpec((d, LANES), lambda k, src: (0, jnp.maximum(k - n16, 0)))],
            scratch_shapes=[pltpu.VMEM((REPACK_SLOTS, LANES, d), F32), pltpu.SemaphoreType.DMA((REPACK_SLOTS,))]),
        out_shape=[jax.ShapeDtypeStruct((d, W16), BF16), jax.ShapeDtypeStruct((d, W32), BF16)],
        compiler_params=_params(1),
        name="w_in_repack",
    )(jnp.asarray(blocks, jnp.int32), wt)


def kernel(x, c, ada_w, ada_b, norm1_gain, norm2_gain, w_in, fox_f_bias, fox_q_gain, fox_k_gain, nsa_q_gain,
           nsa_k_gain, nsa_cmp_pos, nsa_cmp_w1, nsa_cmp_w2, rel_bias, mix_out_gain, w_out, ffn_w_gate, ffn_w_up,
           ffn_w_down):
    batch, seq, d = x.shape
    assert seq % KEY_CHUNK == 0 and seq >= WINDOW and d % LANES == 0 and seq // SEL_BLOCK <= LANES
    depth = ada_w.shape[0]
    nqt = seq // LANES
    ncp = -(-(seq // CMP_STRIDE) // LANES) * LANES

    selmt = jnp.asarray(_selection_matrix_t(ncp, seq // SEL_BLOCK), BF16)
    rb_flat = rel_bias.reshape(-1)
    bias_c = _bias_call(rb_flat, nqt, ncp, CMP_STRIDE, CMP_BLOCK - 1, "t5_bias_compressed", rolled=True)
    tz = _bias_call(rb_flat, 3, LANES, 1, 0, "t5_bias_toeplitz")

    ones_h = jnp.ones((HEAD_DIM,), F32)
    c_pad = jnp.pad(c, ((0, 8 - batch % 8 if batch % 8 else 0), (0, 0)))
    x2 = x.reshape(batch * seq, d)
    for layer in range(depth):
        mod = _ada_call(c_pad, ada_w[layer], ada_b[layer][None, :])[:batch]
        sh1, sc1, g1, sh2, sc2, g2 = [mod[:, i * d:(i + 1) * d][:, None, :] for i in range(N_MOD)]

        kg = nsa_k_gain[layer]
        col_gain = jnp.concatenate([
            jnp.tile(fox_q_gain[layer] * QSCALE, N_FOX_HEADS), jnp.tile(fox_k_gain[layer], N_FOX_HEADS),
            jnp.tile(ones_h, N_FOX_HEADS), jnp.tile(nsa_q_gain[layer] * QSCALE, N_NSA_HEADS),
            jnp.tile(kg[1], N_NSA_KV_HEADS), jnp.tile(kg[2], N_NSA_KV_HEADS),
            jnp.tile(ones_h, 2 * N_NSA_KV_HEADS)])[None, :]
        col_flag = jnp.concatenate([
            jnp.ones((2 * FOX_WIDTH,), F32), jnp.zeros((FOX_WIDTH,), F32), jnp.ones((NSA_WIDTH,), F32),
            jnp.ones((2 * NSA_KV_WIDTH,), F32), jnp.zeros((2 * NSA_KV_WIDTH,), F32)])[None, :]
        w16, w32 = _repack_w_in(jnp.swapaxes(w_in, 1, 2)[layer])
        o16, o32 = _proj_call(x2, sc1, sh1, norm1_gain[layer][None, :], w16, w32,
                              col_gain, col_flag, seq)
        o16v = o16.reshape(batch, seq, W16)
        o32v = o32.reshape(batch, seq, W32)

        fb_pad = jnp.pad(fox_f_bias[layer], (0, LANES - N_FOX_HEADS))[None, :]
        cs = _cum_call(o32, fb_pad, batch, seq, N_FOX_HEADS // FOX_HEADS_PER_STEP)
        o_fox = _fox_call(o16v, cs, batch, seq)

        w1 = nsa_cmp_w1[layer].reshape(2, CMP_BLOCK, HEAD_DIM, HEAD_DIM).astype(BF16)
        kcvc = _cmp_call(o32, nsa_cmp_pos[layer], w1, nsa_cmp_w2[layer].astype(BF16), kg[0][None, :],
                         batch, seq, ncp)
        o_nsa, wg16, wu16, wo16, wd16 = _nsa_call(
            o16v, o32v, kcvc, bias_c, tz, selmt,
            (ffn_w_gate[layer], ffn_w_up[layer], w_out[layer], ffn_w_down[layer]), batch, seq, ncp)

        x1 = _out_call(o_fox.reshape(batch * seq, FOX_WIDTH), o_nsa.reshape(batch * seq, NSA_WIDTH),
                       mix_out_gain[layer][None, :], wo16, x2, g1, seq)
        x2 = _ffn_call(x1, norm2_gain[layer][None, :], sc2, sh2, g2, wg16, wu16, wd16, seq)
    return x2.reshape(batch, seq, d)
```

```python
import functools
import math

import numpy as np
import jax
import jax.numpy as jnp
from jax import lax
from jax.experimental import pallas as pl
from jax.experimental.pallas import tpu as pltpu

HEAD_DIM = 128
N_FOX_HEADS = 8
N_NSA_HEADS = 8
N_NSA_KV_HEADS = 2
NSA_GROUP = N_NSA_HEADS // N_NSA_KV_HEADS
FOX_WIDTH = N_FOX_HEADS * HEAD_DIM
NSA_WIDTH = N_NSA_HEADS * HEAD_DIM
NSA_KV_WIDTH = N_NSA_KV_HEADS * HEAD_DIM
MIX_WIDTH = FOX_WIDTH + NSA_WIDTH
N_BRANCH = 3
CMP_BLOCK = 32
CMP_STRIDE = 16
SEL_BLOCK = 64
N_SEL = 8
WINDOW = 512
N_BUCKETS = 32
MAX_DISTANCE = 128
N_MOD = 6
SCALE = HEAD_DIM ** -0.5
LOG2E = math.log2(math.e)
LOG2E_HI = float(np.float32(LOG2E))
LOG2E_LO = LOG2E - LOG2E_HI
QSCALE = SCALE * LOG2E
EPS = 1e-6
NEG = -1e30
FORCE = 1e6

LANES = 128
GROUP_ROWS = NSA_GROUP * LANES
VMEM_LIMIT = 56 * 1024 * 1024
MXU_COLS = 256
KEY_CHUNK = MXU_COLS
WIN_PAD = WINDOW + LANES
MASK_BIG = 2.0 ** 100
SEL, WIN = 0, 1
REPACK_SLOTS = 4
AUG_FAR, AUG_PAD = 120, 127
FOX_HEADS_PER_STEP = 4
SEL_PAD_TILES = 1

W16 = 3 * FOX_WIDTH + NSA_WIDTH + 4 * NSA_KV_WIDTH
W32 = 5 * LANES
COL_FF = 2 * NSA_KV_WIDTH
COL_NG = COL_FF + N_FOX_HEADS

F32 = jnp.float32
BF16 = jnp.bfloat16
NT_DIMS = (((1,), (1,)), ((), ()))


def _params(n_axes):
    return pltpu.CompilerParams(dimension_semantics=("arbitrary",) * n_axes,
                                vmem_limit_bytes=VMEM_LIMIT)


def _sigmoid(x):
    return 1.0 / (1.0 + jnp.exp(-x))


def _lane_tile(a, n):
    return jnp.concatenate([a] * n, axis=1)


BF16_SUBLANES = 16


def _cast_block(w, n_steps):
    rows, cols = w.shape
    assert rows % n_steps == 0, (w.shape, n_steps)
    per_step = rows // n_steps
    span = BF16_SUBLANES // math.gcd(BF16_SUBLANES, per_step)
    assert n_steps % span == 0, (w.shape, n_steps)
    return (per_step * span, cols), span


def _rms(x):
    return x * lax.rsqrt(jnp.mean(x * x, axis=-1, keepdims=True) + EPS)


NORM_ROWS = 16


def _modulated_norm(x_ref, gain_ref, sc_ref, sh_ref, h_ref):
    gm = gain_ref[...] * (1.0 + sc_ref[0])
    sh = sh_ref[0]
    for r0 in range(0, x_ref.shape[0], NORM_ROWS):
        rows = slice(r0, r0 + NORM_ROWS)
        h_ref[rows, :] = (_rms(x_ref[rows, :]) * gm + sh).astype(BF16)


def _ada_kernel(c_ref, w_ref, b_ref, o_ref):
    c = c_ref[...]
    s = (c * _sigmoid(c)).astype(BF16)
    o_ref[...] = jnp.dot(s, w_ref[...].astype(BF16), preferred_element_type=F32) + b_ref[...]


def _ada_call(c_pad, w, b):
    rows, d = c_pad.shape
    n = w.shape[1]
    tn = next(t for t in (1024, 768, 512, 384, 256, 128) if n % t == 0)
    return pl.pallas_call(
        _ada_kernel,
        grid=(n // tn,),
        in_specs=[pl.BlockSpec((rows, d), lambda j: (0, 0)),
                  pl.BlockSpec((d, tn), lambda j: (0, j)),
                  pl.BlockSpec((1, tn), lambda j: (0, j))],
        out_specs=pl.BlockSpec((rows, tn), lambda j: (0, j)),
        out_shape=jax.ShapeDtypeStruct((rows, n), F32),
        compiler_params=_params(1),
        name="adaln",
    )(c_pad, w, b)


def _proj_kernel(x_ref, sc_ref, sh_ref, g_ref, w16_ref, w32_ref, gain_ref, flag_ref, o16_ref, o32_ref, h_scr, *, tn):
    j = pl.program_id(1)

    @pl.when(j == 0)
    def _():
        _modulated_norm(x_ref, g_ref, sc_ref, sh_ref, h_scr)

    h = h_scr[...]
    for c in range(tn // MXU_COLS):
        acc = jnp.dot(h, w16_ref[:, c * MXU_COLS:(c + 1) * MXU_COLS], preferred_element_type=F32)
        for g in range(MXU_COLS // LANES):
            cols = slice(c * MXU_COLS + g * LANES, c * MXU_COLS + (g + 1) * LANES)
            a = acc[:, g * LANES:(g + 1) * LANES]
            r = lax.rsqrt(jnp.mean(a * a, axis=-1, keepdims=True) + EPS)
            scale = jnp.where(flag_ref[:, cols] > 0.5, r, 1.0)
            o16_ref[:, cols] = (a * scale * gain_ref[:, cols]).astype(BF16)

    @pl.when(j == pl.num_programs(1) - 1)
    def _():
        o32_ref[...] = jnp.dot(h_scr[...], w32_ref[...], preferred_element_type=F32)


def _proj_call(x2, sc, sh, gain1, w16, w32, col_gain, col_flag, seq):
    t, d = x2.shape
    tm = min(1024, seq)
    tn = 1024
    rows_per_batch = seq // tm
    return pl.pallas_call(
        functools.partial(_proj_kernel, tn=tn),
        grid=(t // tm, W16 // tn),
        in_specs=[pl.BlockSpec((tm, d), lambda i, j: (i, 0)),
                  pl.BlockSpec((1, 1, d), lambda i, j: (i // rows_per_batch, 0, 0)),
                  pl.BlockSpec((1, 1, d), lambda i, j: (i // rows_per_batch, 0, 0)),
                  pl.BlockSpec((1, d), lambda i, j: (0, 0)),
                  pl.BlockSpec((d, tn), lambda i, j: (0, j)),
                  pl.BlockSpec((d, W32), lambda i, j: (0, 0)),
                  pl.BlockSpec((1, tn), lambda i, j: (0, j)),
                  pl.BlockSpec((1, tn), lambda i, j: (0, j))],
        out_specs=[pl.BlockSpec((tm, tn), lambda i, j: (i, j)),
                   pl.BlockSpec((tm, W32), lambda i, j: (i, 0))],
        out_shape=[jax.ShapeDtypeStruct((t, W16), BF16),
                   jax.ShapeDtypeStruct((t, W32), F32)],
        scratch_shapes=[pltpu.VMEM((tm, d), BF16)],
        compiler_params=_params(2),
        name="in_proj",
    )(x2, sc, sh, gain1, w16, w32, col_gain, col_flag)


def _split3(c):
    hi = c.astype(BF16).astype(F32)
    r1 = c - hi
    mid = r1.astype(BF16).astype(F32)
    return hi, mid, (r1 - mid).astype(BF16).astype(F32)


def _cum_kernel(ff_ref, fb_ref, o_ref, *, seq, groups):
    ri = lax.broadcasted_iota(jnp.int32, (LANES, LANES), 0)
    ci = lax.broadcasted_iota(jnp.int32, (LANES, LANES), 1)
    tri = jnp.where(ri >= ci, 1.0, 0.0).astype(BF16)
    carry = jnp.zeros((1, LANES), F32)
    for blk in range(seq // LANES):
        rows = slice(blk * LANES, (blk + 1) * LANES)
        x = ff_ref[0, rows, :] + fb_ref[...]
        lf = jnp.minimum(x, 0.0) - jnp.log(1.0 + jnp.exp(-jnp.abs(x)))
        hi = lf.astype(BF16)
        r1 = lf - hi.astype(F32)
        mid = r1.astype(BF16)
        lo = (r1 - mid.astype(F32)).astype(BF16)
        c = (jnp.dot(tri, hi, preferred_element_type=F32)
             + jnp.dot(tri, mid, preferred_element_type=F32)
             + jnp.dot(tri, lo, preferred_element_type=F32)) + carry
        carry = c[LANES - 1:LANES, :]
        for t, term in enumerate(_split3(c * LOG2E_HI + c * LOG2E_LO)):
            for g in range(groups):
                shift = (-g * (N_FOX_HEADS // groups)) % LANES
                o_ref[0, g, t, rows, :] = term if shift == 0 else pltpu.roll(term, shift, 1)


def _cum_call(o32, fb_pad, batch, seq, groups):
    o32v = o32.reshape(batch, seq, W32)
    return pl.pallas_call(
        functools.partial(_cum_kernel, seq=seq, groups=groups),
        grid=(batch,),
        in_specs=[pl.BlockSpec((1, seq, LANES), lambda b: (b, 0, COL_FF // LANES)),
                  pl.BlockSpec((1, LANES), lambda b: (0, 0))],
        out_specs=pl.BlockSpec((1, groups, 3, seq, LANES), lambda b: (b, 0, 0, 0, 0)),
        out_shape=jax.ShapeDtypeStruct((batch, groups, 3, seq, LANES), F32),
        compiler_params=_params(1),
        name="fox_cumsum",
    )(o32v, fb_pad)


def _fox_kernel(q_ref, k_ref, v_ref, cs_ref, o_ref, m_scr, acc_scr, s_scr, kaug_scr, vaug_scr, *, seq, tq, tk, nh):
    n_diag = tq // tk
    lane_q = lax.broadcasted_iota(jnp.int32, (tq, LANES), 1)
    row8 = lax.broadcasted_iota(jnp.int32, (8, tk), 0)
    ri = lax.broadcasted_iota(jnp.int32, (tq, tk), 0)
    ci = lax.broadcasted_iota(jnp.int32, (tq, tk), 1)
    causal = [ci + d * tk <= ri for d in range(n_diag)]
    for c in range(seq // tk):
        rows = slice(c * tk, (c + 1) * tk)
        terms_t = [cs_ref[0, 0, t, rows, :].T for t in range(3)]
        for j in range(nh):
            tail8 = jnp.where(row8 < 3, 1.0, jnp.where(row8 == 3, -terms_t[0][j:j + 1], jnp.where(
                row8 == 4, -terms_t[1][j:j + 1], jnp.where(row8 == 5, -terms_t[2][j:j + 1], 0.0))))
            kaug_scr[j, c, 0:HEAD_DIM, :] = k_ref[0, rows, j * HEAD_DIM:(j + 1) * HEAD_DIM].T
            kaug_scr[j, c, HEAD_DIM:, :] = jnp.concatenate(
                [tail8, jnp.zeros((HEAD_DIM - 8, tk), F32)], axis=0).astype(BF16)
    for j in range(nh):
        vaug_scr[j, :, 0:HEAD_DIM] = v_ref[0, :, j * HEAD_DIM:(j + 1) * HEAD_DIM]
        vaug_scr[j, :, HEAD_DIM:] = jnp.ones((seq, HEAD_DIM), BF16)

    def q_body(qi, carry):
        q0 = pl.multiple_of(qi * tq, tq)
        terms = [cs_ref[0, 0, t, pl.ds(q0, tq), :] for t in range(3)]
        qs = []
        for j in range(nh):
            hi, mid, lo = [term if t == j else pltpu.roll(term, (t - j) % LANES, 1) for t, term in enumerate(terms)]
            tail = jnp.where(lane_q == 0, hi, jnp.where(lane_q == 1, mid, jnp.where(lane_q == 2, lo,
                             jnp.where(lane_q < 6, 1.0, 0.0))))
            qs.append(jnp.concatenate([q_ref[0, pl.ds(q0, tq), j * HEAD_DIM:(j + 1) * HEAD_DIM],
                                       tail.astype(BF16)], axis=1))
            m_scr[j] = jnp.full((tq, LANES), NEG, F32)
            acc_scr[j] = jnp.zeros((tq, 2 * HEAD_DIM), F32)

        all_rows = slice(0, tq)

        def scores(j, ki, rows):
            return jnp.dot(qs[j][rows], kaug_scr[j, ki], preferred_element_type=F32)

        def tile(ki, mask, rows, next_rows):
            k0 = pl.multiple_of(ki * tk, tk)
            for j in range(nh):
                s = s_scr[j, rows, :]
                if mask is not None:
                    s = jnp.where(mask[rows], s, NEG)
                if next_rows is not None:
                    s_scr[j, next_rows, :] = scores(j, ki + 1, next_rows)
                m_prev = m_scr[j, rows, :]
                m_new = jnp.maximum(m_prev, jnp.max(s, axis=1, keepdims=True))
                alpha = jnp.exp2(m_prev - m_new)
                p = jnp.exp2(s - _lane_tile(m_new, tk // LANES))
                pv = jnp.dot(p.astype(BF16), vaug_scr[j, pl.ds(k0, tk), :], preferred_element_type=F32)
                acc_scr[j, rows, :] = _lane_tile(alpha, 2) * acc_scr[j, rows, :] + pv
                m_scr[j, rows, :] = m_new

        def k_body(ki, c2):
            tile(ki, None, all_rows, all_rows)
            return c2

        for j in range(nh):
            s_scr[j] = scores(j, 0, all_rows)
        n_off = qi * n_diag
        lax.fori_loop(0, n_off, k_body, 0)
        for d in range(n_diag):
            nxt = slice((d + 1) * tk, tq) if d + 1 < n_diag else None
            tile(n_off + d, causal[d], slice(d * tk, tq), nxt)
        for j in range(nh):
            acc = acc_scr[j]
            o_ref[0, pl.ds(q0, tq), j * HEAD_DIM:(j + 1) * HEAD_DIM] = acc[:, 0:HEAD_DIM] / acc[:, HEAD_DIM:]
        return carry

    lax.fori_loop(0, seq // tq, q_body, 0)


def _fox_call(o16v, cs, batch, seq):
    nh = FOX_HEADS_PER_STEP
    tq, tk = min(512, seq), min(256, seq)
    groups = N_FOX_HEADS // nh
    w = nh * HEAD_DIM
    return pl.pallas_call(
        functools.partial(_fox_kernel, seq=seq, tq=tq, tk=tk, nh=nh),
        grid=(batch, groups),
        in_specs=[pl.BlockSpec((1, seq, w), lambda b, h: (b, 0, h)),
                  pl.BlockSpec((1, seq, w), lambda b, h: (b, 0, groups + h)),
                  pl.BlockSpec((1, seq, w), lambda b, h: (b, 0, 2 * groups + h)),
                  pl.BlockSpec((1, 1, 3, seq, LANES), lambda b, h: (b, h, 0, 0, 0))],
        out_specs=pl.BlockSpec((1, seq, w), lambda b, h: (b, 0, h)),
        out_shape=jax.ShapeDtypeStruct((batch, seq, FOX_WIDTH), F32),
        scratch_shapes=[pltpu.VMEM((nh, tq, LANES), F32),
                        pltpu.VMEM((nh, tq, 2 * HEAD_DIM), F32),
                        pltpu.VMEM((nh, tq, tk), F32),
                        pltpu.VMEM((nh, seq // tk, 2 * HEAD_DIM, tk), BF16),
                        pltpu.VMEM((nh, seq, 2 * HEAD_DIM), BF16)],
        compiler_params=_params(2),
        name="fox_attention",
    )(o16v, o16v, o16v, cs)


def _cmp_kernel(x_ref, pos_ref, w1_ref, w2_ref, gain_ref, o_ref, xs_scr, *, seq, ncp):
    rows = xs_scr.shape[1]
    for kv in range(2):
        for h in range(N_NSA_KV_HEADS):
            n = kv * N_NSA_KV_HEADS + h
            xs_scr[n, 0:seq, :] = x_ref[0, :, n * HEAD_DIM:(n + 1) * HEAD_DIM]
            xs_scr[n, seq:rows, :] = jnp.zeros((rows - seq, LANES), F32)
            acc = jnp.zeros((ncp, HEAD_DIM), F32)
            for l in range(CMP_BLOCK):
                xl = xs_scr[n, pl.ds(l, ncp, stride=CMP_STRIDE), :] + pos_ref[kv, l:l + 1, :]
                acc = acc + jnp.dot(xl.astype(BF16), w1_ref[kv, l], preferred_element_type=F32)
            hmid = acc * _sigmoid(acc)
            y = jnp.dot(hmid.astype(BF16), w2_ref[kv], preferred_element_type=F32)
            if kv == 0:
                y = _rms(y) * gain_ref[...]
            o_ref[0, kv, h] = y.astype(BF16)


def _cmp_call(o32, pos, w1, w2, gain, batch, seq, ncp):
    o32v = o32.reshape(batch, seq, W32)
    hkv = N_NSA_KV_HEADS
    width = 2 * NSA_KV_WIDTH
    return pl.pallas_call(
        functools.partial(_cmp_kernel, seq=seq, ncp=ncp),
        grid=(batch,),
        in_specs=[pl.BlockSpec((1, seq, width), lambda b: (b, 0, 0)),
                  pl.BlockSpec((2, CMP_BLOCK, HEAD_DIM), lambda b: (0, 0, 0)),
                  pl.BlockSpec((2, CMP_BLOCK, HEAD_DIM, HEAD_DIM), lambda b: (0, 0, 0, 0)),
                  pl.BlockSpec((2, HEAD_DIM, HEAD_DIM), lambda b: (0, 0, 0)),
                  pl.BlockSpec((1, HEAD_DIM), lambda b: (0, 0))],
        out_specs=pl.BlockSpec((1, 2, hkv, ncp, HEAD_DIM), lambda b: (b, 0, 0, 0, 0)),
        out_shape=jax.ShapeDtypeStruct((batch, 2, hkv, ncp, HEAD_DIM), BF16),
        scratch_shapes=[pltpu.VMEM((2 * hkv, CMP_STRIDE * ncp + CMP_BLOCK, LANES), F32)],
        compiler_params=_params(1),
        name="nsa_compress",
    )(o32v, pos, w1, w2, gain)


def _bias_kernel(rb_ref, o_ref, *, width, key_stride, key_offset, first_tile, rolled_tiles):
    v = pl.program_id(0) + first_tile
    i = lax.broadcasted_iota(jnp.int32, (LANES, width), 0)
    j = lax.broadcasted_iota(jnp.int32, (LANES, width), 1)
    d = v * LANES + i - (key_stride * j + key_offset)
    n = jnp.maximum(d, 0)
    max_exact = N_BUCKETS // 2
    nf = jnp.maximum(n, 1).astype(F32)
    large = max_exact + jnp.trunc(jnp.log(nf / max_exact) / math.log(MAX_DISTANCE / max_exact)
                                  * (N_BUCKETS - max_exact))
    large = jnp.minimum(large, float(N_BUCKETS - 1))
    bkt = jnp.where(n < max_exact, n.astype(F32), large)
    vals = [jnp.zeros((LANES, width), F32) for _ in range(N_NSA_HEADS)]
    for bk in range(N_BUCKETS):
        hit = bkt == float(bk)
        for h in range(N_NSA_HEADS):
            vals[h] = jnp.where(hit, rb_ref[bk * N_NSA_HEADS + h] * LOG2E, vals[h])
    for h in range(N_NSA_HEADS):
        g = h % NSA_GROUP
        rows = slice(g * LANES, (g + 1) * LANES)
        if rolled_tiles:
            per_tile = LANES // key_stride
            for t in range(rolled_tiles):
                shift = (width - per_tile * (rolled_tiles - 1 - t)) % width
                o_ref[t, h // NSA_GROUP, rows, :] = vals[h] if shift == 0 else pltpu.roll(vals[h], shift, 1)
        else:
            o_ref[0, h // NSA_GROUP, rows, :] = vals[h]


def _bias_call(rb_flat, n_tiles, width, key_stride, key_offset, name, rolled=False):
    kern = functools.partial(_bias_kernel, width=width, key_stride=key_stride, key_offset=key_offset,
                             first_tile=n_tiles - 1 if rolled else 0, rolled_tiles=n_tiles if rolled else 0)
    block = (n_tiles if rolled else 1, N_NSA_KV_HEADS, GROUP_ROWS, width)
    return pl.pallas_call(
        kern,
        grid=(1 if rolled else n_tiles,),
        in_specs=[pl.BlockSpec(memory_space=pltpu.SMEM)],
        out_specs=pl.BlockSpec(block, lambda v: (v, 0, 0, 0)),
        out_shape=jax.ShapeDtypeStruct((n_tiles, N_NSA_KV_HEADS, GROUP_ROWS, width), F32),
        compiler_params=_params(1),
        name=name,
    )(rb_flat)


def _tile4(a):
    return jnp.concatenate([a] * NSA_GROUP, axis=0)


def _nsa_kernel(q_ref, ks_ref, kw_ref, vs_ref, vw_ref, g_ref, kcvc_ref, bc_ref, tz_ref, selmt_ref,
                wa_ref, wb_ref, wc_ref, wd_ref, o_ref, wa_out, wb_out, wc_out, wd_out,
                qg_scr, qw_scr, caug_scr, near_scr, edge_scr, m_scr, acc_scr, o_scr, s_scr,
                ksa_scr, vsa_scr, kwp_scr, vwa_scr, *, seq, ncp):
    qt = pl.program_id(1)
    q0 = qt * LANES
    n_slc = seq // SEL_BLOCK
    top_n = min(N_SEL, n_slc)
    nkv = N_NSA_KV_HEADS
    for src, dst in ((wa_ref, wa_out), (wb_ref, wb_out), (wc_ref, wc_out), (wd_ref, wd_out)):
        dst[...] = src[...].astype(BF16)
    ri = lax.broadcasted_iota(jnp.int32, (LANES, LANES), 0)
    ci = lax.broadcasted_iota(jnp.int32, (LANES, LANES), 1)
    eye = jnp.where(ri == ci, 1.0, 0.0).astype(BF16)
    gates = _sigmoid(g_ref[0])

    def gate_col(hk, br):
        cols = []
        for g in range(NSA_GROUP):
            c = COL_NG % LANES + (hk * NSA_GROUP + g) * N_BRANCH + br
            cols.append(gates[:, c:c + 1])
        return jnp.concatenate(cols, axis=0)

    @pl.when(qt == 0)
    def _():
        ones = jnp.ones((seq, HEAD_DIM), BF16)
        row = lax.broadcasted_iota(jnp.int32, (LANES, LANES), 0)
        is_far = jnp.logical_and(row >= AUG_FAR, row < AUG_FAR + 3)
        pad_aug = jnp.where(row == AUG_PAD, -MASK_BIG, jnp.where(is_far, 1.0, 0.0)).astype(BF16)
        win_aug = jnp.where(is_far, 1.0, 0.0).astype(BF16)
        lane512 = lax.broadcasted_iota(jnp.int32, (GROUP_ROWS, LANES), 1)
        causal = _tile4(jnp.where(ci <= ri, 0.0, -MASK_BIG))
        edge_scr[...] = jnp.concatenate([jnp.full((GROUP_ROWS, LANES), -MASK_BIG, F32),
                                         _tile4(jnp.where(ri < ci, 0.0, -MASK_BIG))], axis=1)
        for hk in range(nkv):
            hc = slice(hk * HEAD_DIM, (hk + 1) * HEAD_DIM)
            far = tz_ref[2, hk]
            hi, mid, lo = _split3(far)
            caug = jnp.where(lane512 == AUG_FAR, hi, jnp.where(lane512 == AUG_FAR + 1, mid,
                             jnp.where(lane512 == AUG_FAR + 2, lo, jnp.where(lane512 == AUG_PAD, 1.0, 0.0))))
            caug_scr[hk] = caug.astype(BF16)
            qw_scr[hk, :, HEAD_DIM:] = caug.astype(BF16)
            near_scr[hk] = jnp.concatenate([tz_ref[1, hk] - far, (tz_ref[0, hk] - far) + causal], axis=1)
            ksa_scr[hk, 0, 0:HEAD_DIM, :] = jnp.zeros((HEAD_DIM, LANES), BF16)
            ksa_scr[hk, 0, HEAD_DIM:, :] = pad_aug
            for t in range(seq // LANES):
                rows = slice(t * LANES, (t + 1) * LANES)
                ksa_scr[hk, SEL_PAD_TILES + t, 0:HEAD_DIM, :] = ks_ref[0, rows, hc].T
                ksa_scr[hk, SEL_PAD_TILES + t, HEAD_DIM:, :] = jnp.where(
                    (t * LANES + ci) // SEL_BLOCK == ri, -MASK_BIG, jnp.where(is_far, 1.0, 0.0)).astype(BF16)
                kwp_scr[hk, WIN_PAD // LANES + t, 0:HEAD_DIM, :] = kw_ref[0, rows, hc].T
                kwp_scr[hk, WIN_PAD // LANES + t, HEAD_DIM:, :] = win_aug
            for t in range(WIN_PAD // LANES):
                kwp_scr[hk, t, 0:HEAD_DIM, :] = jnp.zeros((HEAD_DIM, LANES), BF16)
                kwp_scr[hk, t, HEAD_DIM:, :] = pad_aug
            vsa_scr[hk, 0:SEL_PAD_TILES * LANES, :] = jnp.zeros((SEL_PAD_TILES * LANES, 2 * HEAD_DIM), BF16)
            vsa_scr[hk, SEL_PAD_TILES * LANES:, 0:HEAD_DIM] = vs_ref[0, :, hc]
            vsa_scr[hk, SEL_PAD_TILES * LANES:, HEAD_DIM:] = ones
            vwa_scr[hk, 0:WIN_PAD, :] = jnp.zeros((WIN_PAD, 2 * HEAD_DIM), BF16)
            vwa_scr[hk, WIN_PAD:, 0:HEAD_DIM] = vw_ref[0, :, hc]
            vwa_scr[hk, WIN_PAD:, HEAD_DIM:] = ones

    def reset(br):
        m_scr[br] = jnp.full(m_scr.shape[1:], NEG, F32)
        acc_scr[br] = jnp.zeros(acc_scr.shape[1:], F32)

    def online_update(br, hk, s, vaug):
        m_prev = m_scr[br, hk]
        m_new = jnp.maximum(m_prev, jnp.max(s, axis=1, keepdims=True))
        alpha = jnp.exp2(m_prev - m_new)
        p = jnp.exp2(s - _lane_tile(m_new, s.shape[1] // LANES))
        pv = jnp.dot(p.astype(BF16), vaug, preferred_element_type=F32)
        acc_scr[br, hk] = _lane_tile(alpha, 2) * acc_scr[br, hk] + pv
        m_scr[br, hk] = m_new

    def finish(br, hk):
        acc = acc_scr[br, hk]
        return acc[:, 0:HEAD_DIM] / acc[:, HEAD_DIM:]

    for hk in range(nkv):
        for g in range(NSA_GROUP):
            h = hk * NSA_GROUP + g
            q_h = q_ref[0, :, h * HEAD_DIM:(h + 1) * HEAD_DIM]
            qg_scr[hk, g * LANES:(g + 1) * LANES, 0:HEAD_DIM] = q_h
            qw_scr[hk, g * LANES:(g + 1) * LANES, 0:HEAD_DIM] = q_h

    def compressed_and_importance(hk):
        qg = qg_scr[hk, :, 0:HEAD_DIM]
        kc = kcvc_ref[0, 0, hk]
        vc = kcvc_ref[0, 1, hk]
        s = lax.dot_general(qg, kc, NT_DIMS, preferred_element_type=F32) + bc_ref[0, hk]
        rc = lax.broadcasted_iota(jnp.int32, (LANES, ncp), 0)
        cc = lax.broadcasted_iota(jnp.int32, (LANES, ncp), 1)
        valid_c = _tile4(jnp.where(q0 + rc - (CMP_STRIDE * cc + CMP_BLOCK - 1) >= 0, 1.0, 0.0)) > 0.5
        s = jnp.where(valid_c, s, NEG)
        p = jnp.where(valid_c, jnp.exp2(s - jnp.max(s, axis=1, keepdims=True)), 0.0)
        l = jnp.sum(p, axis=1, keepdims=True)
        p = p / jnp.where(l > 0.0, l, 1.0)
        o_scr[hk] = gate_col(hk, 0) * jnp.dot(p.astype(BF16), vc, preferred_element_type=F32)
        psum = p[0:LANES]
        for g in range(1, NSA_GROUP):
            psum = psum + p[g * LANES:(g + 1) * LANES]
        p_hi = psum.astype(BF16)
        p_lo = (psum - p_hi.astype(F32)).astype(BF16)
        selmt = selmt_ref[...]
        imp = (lax.dot_general(selmt, p_hi, NT_DIMS, preferred_element_type=F32)
               + lax.dot_general(selmt, p_lo, NT_DIMS, preferred_element_type=F32))
        return imp[0:n_slc]

    def select_blocks(hk, imp):
        blk = lax.broadcasted_iota(jnp.int32, (n_slc, LANES), 0)
        cur = (q0 + lax.broadcasted_iota(jnp.int32, (n_slc, LANES), 1)) // SEL_BLOCK
        forced = (blk == 0) | (blk == cur) | (blk == cur - 1)
        imp = jnp.where(forced, FORCE, imp)
        imp = jnp.where(blk <= cur, imp, -jnp.inf)
        beats = []
        for j in range(n_slc):
            row = imp[j:j + 1, :]
            beats.append(jnp.where(row > imp, 1.0, jnp.where(row == imp, jnp.where(blk > j, 1.0, 0.0), 0.0)))
        while len(beats) > 1:
            beats = [beats[i] + beats[i + 1] for i in range(0, len(beats) - 1, 2)] + beats[len(beats) & ~1:]
        rank = beats[0]
        sel_t = jnp.where(rank < top_n, jnp.where(imp > -jnp.inf, 1.0, 0.0), 0.0)
        if n_slc < LANES:
            sel_t = jnp.concatenate([sel_t, jnp.zeros((LANES - n_slc, LANES), F32)], axis=0)
        sel_q = lax.dot_general(eye, sel_t.astype(BF16), NT_DIMS, preferred_element_type=F32)
        not_sel = jnp.where(ci < n_slc, 1.0 - sel_q, 0.0).astype(BF16)
        for g in range(NSA_GROUP):
            rows = slice(g * LANES, (g + 1) * LANES)
            qg_scr[hk, rows, HEAD_DIM:] = jnp.where(ci < n_slc, not_sel, caug_scr[hk, rows, :])

    def window_chunk(off, table):
        t0 = qt - off + WIN_PAD // LANES
        p0 = pl.multiple_of(t0 * LANES, LANES)
        for hk in range(nkv):
            k_t = jnp.concatenate([kwp_scr[hk, t0], kwp_scr[hk, t0 + 1]], axis=1)
            s = jnp.dot(qw_scr[hk], k_t, preferred_element_type=F32)
            if table is not None:
                s = s + (table[hk] if table is near_scr else table[...])
            online_update(WIN, hk, s, vwa_scr[hk, pl.ds(p0, KEY_CHUNK), :])

    reset(WIN)
    importance = [compressed_and_importance(hk) for hk in range(nkv)]
    window_chunk(1, near_scr)
    for hk in range(nkv):
        select_blocks(hk, importance[hk])
    window_chunk(3, None)
    window_chunk(5, edge_scr)

    reset(SEL)
    n_chunks = (qt + 2) // 2
    first_tile = SEL_PAD_TILES - (qt + 1) % 2

    def sel_scores(hk, c):
        t0 = first_tile + 2 * c
        k_t = jnp.concatenate([ksa_scr[hk, t0], ksa_scr[hk, t0 + 1]], axis=1)
        return jnp.dot(qg_scr[hk], k_t, preferred_element_type=F32)

    def sel_chunk(c, last):
        p0 = pl.multiple_of((first_tile + 2 * c) * LANES, LANES)
        for hk in range(nkv):
            s = s_scr[hk]
            if last:
                s = s + near_scr[hk]
            else:
                s_scr[hk] = sel_scores(hk, c + 1)
            online_update(SEL, hk, s, vsa_scr[hk, pl.ds(p0, KEY_CHUNK), :])

    def sel_body(c, carry):
        sel_chunk(c, False)
        return carry

    for hk in range(nkv):
        s_scr[hk] = sel_scores(hk, 0)
    lax.fori_loop(0, n_chunks - 1, sel_body, 0)
    sel_chunk(n_chunks - 1, True)
    for hk in range(nkv):
        o = (o_scr[hk] + gate_col(hk, 1) * finish(SEL, hk)) + gate_col(hk, 2) * finish(WIN, hk)
        for g in range(NSA_GROUP):
            h = hk * NSA_GROUP + g
            o_ref[0, :, h * HEAD_DIM:(h + 1) * HEAD_DIM] = o[g * LANES:(g + 1) * LANES]


def _nsa_call(o16v, o32v, kcvc, bias_c, tz, selmt, weights_f32, batch, seq, ncp):
    nqt = seq // LANES
    wspecs = []
    for w in weights_f32:
        blk, span = _cast_block(w, batch * nqt)
        wspecs.append(pl.BlockSpec(blk, functools.partial(lambda b, t, span: ((b * nqt + t) // span, 0), span=span)))
    kvw = NSA_KV_WIDTH
    nkv = N_NSA_KV_HEADS
    base = (3 * FOX_WIDTH + NSA_WIDTH) // kvw
    return pl.pallas_call(
        functools.partial(_nsa_kernel, seq=seq, ncp=ncp),
        grid=(batch, nqt),
        in_specs=[pl.BlockSpec((1, LANES, NSA_WIDTH), lambda b, t: (b, t, 3 * FOX_WIDTH // NSA_WIDTH)),
                  pl.BlockSpec((1, seq, kvw), lambda b, t: (b, 0, base)),
                  pl.BlockSpec((1, seq, kvw), lambda b, t: (b, 0, base + 1)),
                  pl.BlockSpec((1, seq, kvw), lambda b, t: (b, 0, base + 2)),
                  pl.BlockSpec((1, seq, kvw), lambda b, t: (b, 0, base + 3)),
                  pl.BlockSpec((1, LANES, LANES), lambda b, t: (b, t, COL_NG // LANES)),
                  pl.BlockSpec((1, 2, N_NSA_KV_HEADS, ncp, HEAD_DIM), lambda b, t: (b, 0, 0, 0, 0)),
                  pl.BlockSpec((1, N_NSA_KV_HEADS, GROUP_ROWS, ncp), lambda b, t: (t, 0, 0, 0)),
                  pl.BlockSpec((3, N_NSA_KV_HEADS, GROUP_ROWS, LANES), lambda b, t: (0, 0, 0, 0)),
                  pl.BlockSpec((LANES, ncp), lambda b, t: (0, 0))] + wspecs,
        out_specs=[pl.BlockSpec((1, LANES, NSA_WIDTH), lambda b, t: (b, t, 0))] + wspecs,
        out_shape=[jax.ShapeDtypeStruct((batch, seq, NSA_WIDTH), F32)]
        + [jax.ShapeDtypeStruct(w.shape, BF16) for w in weights_f32],
        scratch_shapes=[pltpu.VMEM((nkv, GROUP_ROWS, 2 * HEAD_DIM), BF16),
                        pltpu.VMEM((nkv, GROUP_ROWS, 2 * HEAD_DIM), BF16),
                        pltpu.VMEM((nkv, GROUP_ROWS, LANES), BF16),
                        pltpu.VMEM((nkv, GROUP_ROWS, KEY_CHUNK), F32),
                        pltpu.VMEM((GROUP_ROWS, KEY_CHUNK), F32),
                        pltpu.VMEM((2, nkv, GROUP_ROWS, LANES), F32),
                        pltpu.VMEM((2, nkv, GROUP_ROWS, 2 * HEAD_DIM), F32),
                        pltpu.VMEM((nkv, GROUP_ROWS, HEAD_DIM), F32),
                        pltpu.VMEM((nkv, GROUP_ROWS, KEY_CHUNK), F32),
                        pltpu.VMEM((nkv, SEL_PAD_TILES + seq // LANES, 2 * HEAD_DIM, LANES), BF16),
                        pltpu.VMEM((nkv, SEL_PAD_TILES * LANES + seq, 2 * HEAD_DIM), BF16),
                        pltpu.VMEM((nkv, (seq + WIN_PAD) // LANES, 2 * HEAD_DIM, LANES), BF16),
                        pltpu.VMEM((nkv, seq + WIN_PAD, 2 * HEAD_DIM), BF16)],
        compiler_params=_params(2),
        name="nsa_attention",
    )(o16v, o16v, o16v, o16v, o16v, o32v, kcvc, bias_c, tz, selmt, *weights_f32)


def _out_kernel(of_ref, on_ref, gain_ref, w_ref, x_ref, g_ref, o_ref, y_scr, *, tm, n_chains):
    rows_per_chain = tm // n_chains
    gain_f = gain_ref[:, 0:FOX_WIDTH]
    gain_n = gain_ref[:, FOX_WIDTH:MIX_WIDTH]
    for c in range(n_chains):
        rows = slice(c * rows_per_chain, (c + 1) * rows_per_chain)
        for r0 in range(c * rows_per_chain, (c + 1) * rows_per_chain, NORM_ROWS):
            r = slice(r0, r0 + NORM_ROWS)
            y_scr[r, 0:FOX_WIDTH] = (_rms(of_ref[r, :]) * gain_f).astype(BF16)
            y_scr[r, FOX_WIDTH:MIX_WIDTH] = (_rms(on_ref[r, :]) * gain_n).astype(BF16)
        acc = jnp.dot(y_scr[rows, :], w_ref[...], preferred_element_type=F32)
        o_ref[rows, :] = x_ref[rows, :] + g_ref[0] * acc


def _out_call(o_fox, o_nsa, gain, w_out, x2, g1, seq):
    t, d = x2.shape
    tm = min(512, seq)
    rows_per_batch = seq // tm
    return pl.pallas_call(
        functools.partial(_out_kernel, tm=tm, n_chains=2),
        grid=(t // tm,),
        in_specs=[pl.BlockSpec((tm, FOX_WIDTH), lambda i: (i, 0)),
                  pl.BlockSpec((tm, NSA_WIDTH), lambda i: (i, 0)),
                  pl.BlockSpec((1, MIX_WIDTH), lambda i: (0, 0)),
                  pl.BlockSpec((MIX_WIDTH, d), lambda i: (0, 0)),
                  pl.BlockSpec((tm, d), lambda i: (i, 0)),
                  pl.BlockSpec((1, 1, d), lambda i: (i // rows_per_batch, 0, 0))],
        out_specs=pl.BlockSpec((tm, d), lambda i: (i, 0)),
        out_shape=jax.ShapeDtypeStruct((t, d), F32),
        scratch_shapes=[pltpu.VMEM((tm, MIX_WIDTH), BF16)],
        compiler_params=_params(1),
        name="out_proj",
    )(o_fox, o_nsa, gain, w_out, x2, g1)


def _ffn_kernel(x_ref, gain_ref, sc_ref, sh_ref, g_ref, wg_ref, wu_ref, wd_ref, o_ref, h_scr, acc_scr):
    f = pl.program_id(1)

    @pl.when(f == 0)
    def _():
        _modulated_norm(x_ref, gain_ref, sc_ref, sh_ref, h_scr)
        acc_scr[...] = jnp.zeros(acc_scr.shape, F32)

    h = h_scr[...]
    a = jnp.dot(h, wg_ref[...], preferred_element_type=F32)
    u = jnp.dot(h, wu_ref[...], preferred_element_type=F32)
    t = (a * _sigmoid(a)) * u
    acc_scr[...] += jnp.dot(t.astype(BF16), wd_ref[...], preferred_element_type=F32)

    @pl.when(f == pl.num_programs(1) - 1)
    def _():
        o_ref[...] = x_ref[...] + g_ref[0] * acc_scr[...]


def _ffn_call(x1, gain2, sc, sh, g2, wg, wu, wd, seq):
    t, d = x1.shape
    dff = wg.shape[1]
    tm = min(512, seq)
    tf = 512 if dff % 512 == 0 else dff
    rows_per_batch = seq // tm
    return pl.pallas_call(
        _ffn_kernel,
        grid=(t // tm, dff // tf),
        in_specs=[pl.BlockSpec((tm, d), lambda i, f: (i, 0)),
                  pl.BlockSpec((1, d), lambda i, f: (0, 0)),
                  pl.BlockSpec((1, 1, d), lambda i, f: (i // rows_per_batch, 0, 0)),
                  pl.BlockSpec((1, 1, d), lambda i, f: (i // rows_per_batch, 0, 0)),
                  pl.BlockSpec((1, 1, d), lambda i, f: (i // rows_per_batch, 0, 0)),
                  pl.BlockSpec((d, tf), lambda i, f: (0, f)),
                  pl.BlockSpec((d, tf), lambda i, f: (0, f)),
                  pl.BlockSpec((tf, d), lambda i, f: (f, 0))],
        out_specs=pl.BlockSpec((tm, d), lambda i, f: (i, 0)),
        out_shape=jax.ShapeDtypeStruct((t, d), F32),
        scratch_shapes=[pltpu.VMEM((tm, d), BF16), pltpu.VMEM((tm, d), F32)],
        compiler_params=_params(2),
        name="swiglu_ffn",
    )(x1, gain2, sc, sh, g2, wg, wu, wd)


def _selection_matrix_t(ncp, n_slc):
    r, q = SEL_BLOCK // CMP_STRIDE, CMP_BLOCK // CMP_STRIDE
    m = np.zeros((LANES, ncp), np.float32)
    for j in range(n_slc):
        for a in range(r):
            for b in range(q):
                c = r * j + a - b
                if 0 <= c < ncp:
                    m[j, c] += 1.0
    return m


def _w_in_block_sources():
    o = 0
    start = {}
    for name, width in (("fq", FOX_WIDTH), ("fk", FOX_WIDTH), ("fv", FOX_WIDTH), ("ff", N_FOX_HEADS),
                        ("nq", NSA_WIDTH), ("nk", N_BRANCH * NSA_KV_WIDTH), ("nv", N_BRANCH * NSA_KV_WIDTH),
                        ("ng", N_BRANCH * N_NSA_HEADS)):
        start[name] = o
        o += width
    kvw = NSA_KV_WIDTH
    groups = [(start["fq"], 3 * FOX_WIDTH), (start["nq"], NSA_WIDTH),
              (start["nk"] + kvw, 2 * kvw), (start["nv"] + kvw, 2 * kvw),
              (start["nk"], kvw), (start["nv"], kvw)]
    blocks = [s + LANES * b for s, width in groups for b in range(width // LANES)]
    return blocks, (start["ff"], N_FOX_HEADS), (start["ng"], N_BRANCH * N_NSA_HEADS)


def _repack_kernel(src_ref, wt_hbm, o16_ref, o32_ref, buf, sem, *, n16, n_whole, ff, ng):
    k = pl.program_id(0)
    n_slots = buf.shape[0]
    ahead = n_slots - 1
    slot = k % n_slots

    def whole_copy(kk, s):
        r0 = pl.multiple_of(src_ref[kk], 8)
        return pltpu.make_async_copy(wt_hbm.at[pl.ds(r0, LANES), :], buf.at[s], sem.at[s])

    def narrow_copies(s):
        return [pltpu.make_async_copy(wt_hbm.at[pl.ds(ff[0], ff[1]), :], buf.at[s, pl.ds(0, ff[1]), :], sem.at[s]),
                pltpu.make_async_copy(wt_hbm.at[pl.ds(ng[0], ng[1]), :], buf.at[s, pl.ds(ff[1], ng[1]), :],
                                      sem.at[s])]

    def start_block(kk):
        @pl.when(kk < n_whole)
        def _():
            whole_copy(kk, kk % n_slots).start()

        @pl.when(kk == n_whole)
        def _():
            for cp in narrow_copies(kk % n_slots):
                cp.start()

    @pl.when(k == 0)
    def _():
        for kk in range(ahead):
            start_block(kk)

    start_block(k + ahead)

    @pl.when(k < n_whole)
    def _():
        whole_copy(k, slot).wait()

    @pl.when(k == n_whole)
    def _():
        for cp in narrow_copies(slot):
            cp.wait()

    xt = buf[slot].T
    lane = lax.broadcasted_iota(jnp.int32, xt.shape, 1)
    xt = jnp.where(jnp.logical_or(k < n_whole, lane < ff[1] + ng[1]), xt, 0.0).astype(BF16)

    @pl.when(k < n16)
    def _():
        o16_ref[...] = xt

    @pl.when(k >= n16)
    def _():
        o32_ref[...] = xt


def _repack_w_in(wt):
    n, d = wt.shape
    blocks, ff, ng = _w_in_block_sources()
    n16 = W16 // LANES
    n_whole = len(blocks)
    assert n_whole + 1 == (W16 + W32) // LANES
    return pl.pallas_call(
        functools.partial(_repack_kernel, n16=n16, n_whole=n_whole, ff=ff, ng=ng),
        grid_spec=pltpu.PrefetchScalarGridSpec(
            num_scalar_prefetch=1,
            grid=(n_whole + 1,),
            in_specs=[pl.BlockSpec(memory_space=pl.ANY)],
            out_specs=[pl.BlockSpec((d, LANES), lambda k, src: (0, jnp.minimum(k, n16 - 1))),
                       pl.BlockSpec((d, LANES), lambda k, src: (0, jnp.maximum(k - n16, 0)))],
            scratch_shapes=[pltpu.VMEM((REPACK_SLOTS, LANES, d), F32), pltpu.SemaphoreType.DMA((REPACK_SLOTS,))]),
        out_shape=[jax.ShapeDtypeStruct((d, W16), BF16), jax.ShapeDtypeStruct((d, W32), BF16)],
        compiler_params=_params(1),
        name="w_in_repack",
    )(jnp.asarray(blocks, jnp.int32), wt)


def kernel(x, c, ada_w, ada_b, norm1_gain, norm2_gain, w_in, fox_f_bias, fox_q_gain, fox_k_gain, nsa_q_gain,
           nsa_k_gain, nsa_cmp_pos, nsa_cmp_w1, nsa_cmp_w2, rel_bias, mix_out_gain, w_out, ffn_w_gate, ffn_w_up,
           ffn_w_down):
    batch, seq, d = x.shape
    assert seq % KEY_CHUNK == 0 and seq >= WINDOW and d % LANES == 0 and seq // SEL_BLOCK <= LANES
    depth = ada_w.shape[0]
    nqt = seq // LANES
    ncp = -(-(seq // CMP_STRIDE) // LANES) * LANES

    selmt = jnp.asarray(_selection_matrix_t(ncp, seq // SEL_BLOCK), BF16)
    rb_flat = rel_bias.reshape(-1)
    bias_c = _bias_call(rb_flat, nqt, ncp, CMP_STRIDE, CMP_BLOCK - 1, "t5_bias_compressed", rolled=True)
    tz = _bias_call(rb_flat, 3, LANES, 1, 0, "t5_bias_toeplitz")

    ones_h = jnp.ones((HEAD_DIM,), F32)
    c_pad = jnp.pad(c, ((0, 8 - batch % 8 if batch % 8 else 0), (0, 0)))
    x2 = x.reshape(batch * seq, d)
    for layer in range(depth):
        mod = _ada_call(c_pad, ada_w[layer], ada_b[layer][None, :])[:batch]
        sh1, sc1, g1, sh2, sc2, g2 = [mod[:, i * d:(i + 1) * d][:, None, :] for i in range(N_MOD)]

        kg = nsa_k_gain[layer]
        col_gain = jnp.concatenate([
            jnp.tile(fox_q_gain[layer] * QSCALE, N_FOX_HEADS), jnp.tile(fox_k_gain[layer], N_FOX_HEADS),
            jnp.tile(ones_h, N_FOX_HEADS), jnp.tile(nsa_q_gain[layer] * QSCALE, N_NSA_HEADS),
            jnp.tile(kg[1], N_NSA_KV_HEADS), jnp.tile(kg[2], N_NSA_KV_HEADS),
            jnp.tile(ones_h, 2 * N_NSA_KV_HEADS)])[None, :]
        col_flag = jnp.concatenate([
            jnp.ones((2 * FOX_WIDTH,), F32), jnp.zeros((FOX_WIDTH,), F32), jnp.ones((NSA_WIDTH,), F32),
            jnp.ones((2 * NSA_KV_WIDTH,), F32), jnp.zeros((2 * NSA_KV_WIDTH,), F32)])[None, :]
        w16, w32 = _repack_w_in(jnp.swapaxes(w_in, 1, 2)[layer])
        o16, o32 = _proj_call(x2, sc1, sh1, norm1_gain[layer][None, :], w16, w32,
                              col_gain, col_flag, seq)
        o16v = o16.reshape(batch, seq, W16)
        o32v = o32.reshape(batch, seq, W32)

        fb_pad = jnp.pad(fox_f_bias[layer], (0, LANES - N_FOX_HEADS))[None, :]
        cs = _cum_call(o32, fb_pad, batch, seq, N_FOX_HEADS // FOX_HEADS_PER_STEP)
        o_fox = _fox_call(o16v, cs, batch, seq)

        w1 = nsa_cmp_w1[layer].reshape(2, CMP_BLOCK, HEAD_DIM, HEAD_DIM).astype(BF16)
        kcvc = _cmp_call(o32, nsa_cmp_pos[layer], w1, nsa_cmp_w2[layer].astype(BF16), kg[0][None, :],
                         batch, seq, ncp)
        o_nsa, wg16, wu16, wo16, wd16 = _nsa_call(
            o16v, o32v, kcvc, bias_c, tz, selmt,
            (ffn_w_gate[layer], ffn_w_up[layer], w_out[layer], ffn_w_down[layer]), batch, seq, ncp)

        x1 = _out_call(o_fox.reshape(batch * seq, FOX_WIDTH), o_nsa.reshape(batch * seq, NSA_WIDTH),
                       mix_out_gain[layer][None, :], wo16, x2, g1, seq)
        x2 = _ffn_call(x1, norm2_gain[layer][None, :], sc2, sh2, g2, wg16, wu16, wd16, seq)
    return x2.reshape(batch, seq, d)
```

```python
import functools
import math

import numpy as np
import jax
import jax.numpy as jnp
from jax import lax
from jax.experimental import pallas as pl
from jax.experimental.pallas import tpu as pltpu

HEAD_DIM = 128
N_FOX_HEADS = 8
N_NSA_HEADS = 8
N_NSA_KV_HEADS = 2
NSA_GROUP = N_NSA_HEADS // N_NSA_KV_HEADS
FOX_WIDTH = N_FOX_HEADS * HEAD_DIM
NSA_WIDTH = N_NSA_HEADS * HEAD_DIM
NSA_KV_WIDTH = N_NSA_KV_HEADS * HEAD_DIM
MIX_WIDTH = FOX_WIDTH + NSA_WIDTH
N_BRANCH = 3
CMP_BLOCK = 32
CMP_STRIDE = 16
SEL_BLOCK = 64
N_SEL = 8
WINDOW = 512
N_BUCKETS = 32
MAX_DISTANCE = 128
N_MOD = 6
SCALE = HEAD_DIM ** -0.5
LOG2E = math.log2(math.e)
LOG2E_HI = float(np.float32(LOG2E))
LOG2E_LO = LOG2E - LOG2E_HI
QSCALE = SCALE * LOG2E
EPS = 1e-6
NEG = -1e30
FORCE = 1e6

LANES = 128
GROUP_ROWS = NSA_GROUP * LANES
VMEM_LIMIT = 56 * 1024 * 1024
MXU_COLS = 256
KEY_CHUNK = MXU_COLS
WIN_PAD = WINDOW + LANES
MASK_BIG = 2.0 ** 100
SEL, WIN = 0, 1
REPACK_SLOTS = 4
AUG_FAR, AUG_PAD = 120, 127
FOX_HEADS_PER_STEP = 4
SEL_PAD_TILES = 1

W16 = 3 * FOX_WIDTH + NSA_WIDTH + 4 * NSA_KV_WIDTH
W32 = 5 * LANES
COL_FF = 2 * NSA_KV_WIDTH
COL_NG = COL_FF + N_FOX_HEADS

F32 = jnp.float32
BF16 = jnp.bfloat16
NT_DIMS = (((1,), (1,)), ((), ()))


def _params(n_axes):
    return pltpu.CompilerParams(dimension_semantics=("arbitrary",) * n_axes,
                                vmem_limit_bytes=VMEM_LIMIT)


def _sigmoid(x):
    return 1.0 / (1.0 + jnp.exp(-x))


def _lane_tile(a, n):
    return jnp.concatenate([a] * n, axis=1)


BF16_SUBLANES = 16


def _cast_block(w, n_steps):
    rows, cols = w.shape
    assert rows % n_steps == 0, (w.shape, n_steps)
    per_step = rows // n_steps
    span = BF16_SUBLANES // math.gcd(BF16_SUBLANES, per_step)
    assert n_steps % span == 0, (w.shape, n_steps)
    return (per_step * span, cols), span


def _rms(x):
    return x * lax.rsqrt(jnp.mean(x * x, axis=-1, keepdims=True) + EPS)


NORM_ROWS = 16


def _modulated_norm(x_ref, gain_ref, sc_ref, sh_ref, h_ref):
    gm = gain_ref[...] * (1.0 + sc_ref[0])
    sh = sh_ref[0]
    for r0 in range(0, x_ref.shape[0], NORM_ROWS):
        rows = slice(r0, r0 + NORM_ROWS)
        h_ref[rows, :] = (_rms(x_ref[rows, :]) * gm + sh).astype(BF16)


def _ada_kernel(c_ref, w_ref, b_ref, o_ref):
    c = c_ref[...]
    s = (c * _sigmoid(c)).astype(BF16)
    o_ref[...] = jnp.dot(s, w_ref[...].astype(BF16), preferred_element_type=F32) + b_ref[...]


def _ada_call(c_pad, w, b):
    rows, d = c_pad.shape
    n = w.shape[1]
    tn = next(t for t in (1024, 768, 512, 384, 256, 128) if n % t == 0)
    return pl.pallas_call(
        _ada_kernel,
        grid=(n // tn,),
        in_specs=[pl.BlockSpec((rows, d), lambda j: (0, 0)),
                  pl.BlockSpec((d, tn), lambda j: (0, j)),
                  pl.BlockSpec((1, tn), lambda j: (0, j))],
        out_specs=pl.BlockSpec((rows, tn), lambda j: (0, j)),
        out_shape=jax.ShapeDtypeStruct((rows, n), F32),
        compiler_params=_params(1),
        name="adaln",
    )(c_pad, w, b)


def _proj_kernel(x_ref, sc_ref, sh_ref, g_ref, w16_ref, w32_ref, gain_ref, flag_ref, o16_ref, o32_ref, h_scr, *, tn):
    j = pl.program_id(1)

    @pl.when(j == 0)
    def _():
        _modulated_norm(x_ref, g_ref, sc_ref, sh_ref, h_scr)

    h = h_scr[...]
    for c in range(tn // MXU_COLS):
        acc = jnp.dot(h, w16_ref[:, c * MXU_COLS:(c + 1) * MXU_COLS], preferred_element_type=F32)
        for g in range(MXU_COLS // LANES):
            cols = slice(c * MXU_COLS + g * LANES, c * MXU_COLS + (g + 1) * LANES)
            a = acc[:, g * LANES:(g + 1) * LANES]
            r = lax.rsqrt(jnp.mean(a * a, axis=-1, keepdims=True) + EPS)
            scale = jnp.where(flag_ref[:, cols] > 0.5, r, 1.0)
            o16_ref[:, cols] = (a * scale * gain_ref[:, cols]).astype(BF16)

    @pl.when(j == pl.num_programs(1) - 1)
    def _():
        o32_ref[...] = jnp.dot(h_scr[...], w32_ref[...], preferred_element_type=F32)


def _proj_call(x2, sc, sh, gain1, w16, w32, col_gain, col_flag, seq):
    t, d = x2.shape
    tm = min(1024, seq)
    tn = 1024
    rows_per_batch = seq // tm
    return pl.pallas_call(
        functools.partial(_proj_kernel, tn=tn),
        grid=(t // tm, W16 // tn),
        in_specs=[pl.BlockSpec((tm, d), lambda i, j: (i, 0)),
                  pl.BlockSpec((1, 1, d), lambda i, j: (i // rows_per_batch, 0, 0)),
                  pl.BlockSpec((1, 1, d), lambda i, j: (i // rows_per_batch, 0, 0)),
                  pl.BlockSpec((1, d), lambda i, j: (0, 0)),
                  pl.BlockSpec((d, tn), lambda i, j: (0, j)),
                  pl.BlockSpec((d, W32), lambda i, j: (0, 0)),
                  pl.BlockSpec((1, tn), lambda i, j: (0, j)),
                  pl.BlockSpec((1, tn), lambda i, j: (0, j))],
        out_specs=[pl.BlockSpec((tm, tn), lambda i, j: (i, j)),
                   pl.BlockSpec((tm, W32), lambda i, j: (i, 0))],
        out_shape=[jax.ShapeDtypeStruct((t, W16), BF16),
                   jax.ShapeDtypeStruct((t, W32), F32)],
        scratch_shapes=[pltpu.VMEM((tm, d), BF16)],
        compiler_params=_params(2),
        name="in_proj",
    )(x2, sc, sh, gain1, w16, w32, col_gain, col_flag)


def _split3(c):
    hi = c.astype(BF16).astype(F32)
    r1 = c - hi
    mid = r1.astype(BF16).astype(F32)
    return hi, mid, (r1 - mid).astype(BF16).astype(F32)


def _cum_kernel(ff_ref, fb_ref, o_ref, *, seq, groups):
    ri = lax.broadcasted_iota(jnp.int32, (LANES, LANES), 0)
    ci = lax.broadcasted_iota(jnp.int32, (LANES, LANES), 1)
    tri = jnp.where(ri >= ci, 1.0, 0.0).astype(BF16)
    carry = jnp.zeros((1, LANES), F32)
    for blk in range(seq // LANES):
        rows = slice(blk * LANES, (blk + 1) * LANES)
        x = ff_ref[0, rows, :] + fb_ref[...]
        lf = jnp.minimum(x, 0.0) - jnp.log(1.0 + jnp.exp(-jnp.abs(x)))
        hi = lf.astype(BF16)
        r1 = lf - hi.astype(F32)
        mid = r1.astype(BF16)
        lo = (r1 - mid.astype(F32)).astype(BF16)
        c = (jnp.dot(tri, hi, preferred_element_type=F32)
             + jnp.dot(tri, mid, preferred_element_type=F32)
             + jnp.dot(tri, lo, preferred_element_type=F32)) + carry
        carry = c[LANES - 1:LANES, :]
        for t, term in enumerate(_split3(c * LOG2E_HI + c * LOG2E_LO)):
            for g in range(groups):
                shift = (-g * (N_FOX_HEADS // groups)) % LANES
                o_ref[0, g, t, rows, :] = term if shift == 0 else pltpu.roll(term, shift, 1)


def _cum_call(o32, fb_pad, batch, seq, groups):
    o32v = o32.reshape(batch, seq, W32)
    return pl.pallas_call(
        functools.partial(_cum_kernel, seq=seq, groups=groups),
        grid=(batch,),
        in_specs=[pl.BlockSpec((1, seq, LANES), lambda b: (b, 0, COL_FF // LANES)),
                  pl.BlockSpec((1, LANES), lambda b: (0, 0))],
        out_specs=pl.BlockSpec((1, groups, 3, seq, LANES), lambda b: (b, 0, 0, 0, 0)),
        out_shape=jax.ShapeDtypeStruct((batch, groups, 3, seq, LANES), F32),
        compiler_params=_params(1),
        name="fox_cumsum",
    )(o32v, fb_pad)


def _fox_kernel(q_ref, k_ref, v_ref, cs_ref, o_ref, m_scr, acc_scr, s_scr, kaug_scr, vaug_scr, *, seq, tq, tk, nh):
    n_diag = tq // tk
    lane_q = lax.broadcasted_iota(jnp.int32, (tq, LANES), 1)
    row8 = lax.broadcasted_iota(jnp.int32, (8, tk), 0)
    ri = lax.broadcasted_iota(jnp.int32, (tq, tk), 0)
    ci = lax.broadcasted_iota(jnp.int32, (tq, tk), 1)
    causal = [ci + d * tk <= ri for d in range(n_diag)]
    for c in range(seq // tk):
        rows = slice(c * tk, (c + 1) * tk)
        terms_t = [cs_ref[0, 0, t, rows, :].T for t in range(3)]
        for j in range(nh):
            tail8 = jnp.where(row8 < 3, 1.0, jnp.where(row8 == 3, -terms_t[0][j:j + 1], jnp.where(
                row8 == 4, -terms_t[1][j:j + 1], jnp.where(row8 == 5, -terms_t[2][j:j + 1], 0.0))))
            kaug_scr[j, c, 0:HEAD_DIM, :] = k_ref[0, rows, j * HEAD_DIM:(j + 1) * HEAD_DIM].T
            kaug_scr[j, c, HEAD_DIM:, :] = jnp.concatenate(
                [tail8, jnp.zeros((HEAD_DIM - 8, tk), F32)], axis=0).astype(BF16)
    for j in range(nh):
        vaug_scr[j, :, 0:HEAD_DIM] = v_ref[0, :, j * HEAD_DIM:(j + 1) * HEAD_DIM]
        vaug_scr[j, :, HEAD_DIM:] = jnp.ones((seq, HEAD_DIM), BF16)

    def q_body(qi, carry):
        q0 = pl.multiple_of(qi * tq, tq)
        terms = [cs_ref[0, 0, t, pl.ds(q0, tq), :] for t in range(3)]
        qs = []
        for j in range(nh):
            hi, mid, lo = [term if t == j else pltpu.roll(term, (t - j) % LANES, 1) for t, term in enumerate(terms)]
            tail = jnp.where(lane_q == 0, hi, jnp.where(lane_q == 1, mid, jnp.where(lane_q == 2, lo,
                             jnp.where(lane_q < 6, 1.0, 0.0))))
            qs.append(jnp.concatenate([q_ref[0, pl.ds(q0, tq), j * HEAD_DIM:(j + 1) * HEAD_DIM],
                                       tail.astype(BF16)], axis=1))
            m_scr[j] = jnp.full((tq, LANES), NEG, F32)
            acc_scr[j] = jnp.zeros((tq, 2 * HEAD_DIM), F32)

        all_rows = slice(0, tq)

        def scores(j, ki, rows):
            return jnp.dot(qs[j][rows], kaug_scr[j, ki], preferred_element_type=F32)

        def tile(ki, mask, rows, next_rows):
            k0 = pl.multiple_of(ki * tk, tk)
            for j in range(nh):
                s = s_scr[j, rows, :]
                if mask is not None:
                    s = jnp.where(mask[rows], s, NEG)
                if next_rows is not None:
                    s_scr[j, next_rows, :] = scores(j, ki + 1, next_rows)
                m_prev = m_scr[j, rows, :]
                m_new = jnp.maximum(m_prev, jnp.max(s, axis=1, keepdims=True))
                alpha = jnp.exp2(m_prev - m_new)
                p = jnp.exp2(s - _lane_tile(m_new, tk // LANES))
                pv = jnp.dot(p.astype(BF16), vaug_scr[j, pl.ds(k0, tk), :], preferred_element_type=F32)
                acc_scr[j, rows, :] = _lane_tile(alpha, 2) * acc_scr[j, rows, :] + pv
                m_scr[j, rows, :] = m_new

        def k_body(ki, c2):
            tile(ki, None, all_rows, all_rows)
            return c2

        for j in range(nh):
            s_scr[j] = scores(j, 0, all_rows)
        n_off = qi * n_diag
        lax.fori_loop(0, n_off, k_body, 0)
        for d in range(n_diag):
            nxt = slice((d + 1) * tk, tq) if d + 1 < n_diag else None
            tile(n_off + d, causal[d], slice(d * tk, tq), nxt)
        for j in range(nh):
            acc = acc_scr[j]
            o_ref[0, pl.ds(q0, tq), j * HEAD_DIM:(j + 1) * HEAD_DIM] = acc[:, 0:HEAD_DIM] / acc[:, HEAD_DIM:]
        return carry

    lax.fori_loop(0, seq // tq, q_body, 0)


def _fox_call(o16v, cs, batch, seq):
    nh = FOX_HEADS_PER_STEP
    tq, tk = min(512, seq), min(256, seq)
    groups = N_FOX_HEADS // nh
    w = nh * HEAD_DIM
    return pl.pallas_call(
        functools.partial(_fox_kernel, seq=seq, tq=tq, tk=tk, nh=nh),
        grid=(batch, groups),
        in_specs=[pl.BlockSpec((1, seq, w), lambda b, h: (b, 0, h)),
                  pl.BlockSpec((1, seq, w), lambda b, h: (b, 0, groups + h)),
                  pl.BlockSpec((1, seq, w), lambda b, h: (b, 0, 2 * groups + h)),
                  pl.BlockSpec((1, 1, 3, seq, LANES), lambda b, h: (b, h, 0, 0, 0))],
        out_specs=pl.BlockSpec((1, seq, w), lambda b, h: (b, 0, h)),
        out_shape=jax.ShapeDtypeStruct((batch, seq, FOX_WIDTH), F32),
        scratch_shapes=[pltpu.VMEM((nh, tq, LANES), F32),
                        pltpu.VMEM((nh, tq, 2 * HEAD_DIM), F32),
                        pltpu.VMEM((nh, tq, tk), F32),
                        pltpu.VMEM((nh, seq // tk, 2 * HEAD_DIM, tk), BF16),
                        pltpu.VMEM((nh, seq, 2 * HEAD_DIM), BF16)],
        compiler_params=_params(2),
        name="fox_attention",
    )(o16v, o16v, o16v, cs)


def _cmp_kernel(x_ref, pos_ref, w1_ref, w2_ref, gain_ref, o_ref, xs_scr, *, seq, ncp):
    rows = xs_scr.shape[1]
    for kv in range(2):
        for h in range(N_NSA_KV_HEADS):
            n = kv * N_NSA_KV_HEADS + h
            xs_scr[n, 0:seq, :] = x_ref[0, :, n * HEAD_DIM:(n + 1) * HEAD_DIM]
            xs_scr[n, seq:rows, :] = jnp.zeros((rows - seq, LANES), F32)
            acc = jnp.zeros((ncp, HEAD_DIM), F32)
            for l in range(CMP_BLOCK):
                xl = xs_scr[n, pl.ds(l, ncp, stride=CMP_STRIDE), :] + pos_ref[kv, l:l + 1, :]
                acc = acc + jnp.dot(xl.astype(BF16), w1_ref[kv, l], preferred_element_type=F32)
            hmid = acc * _sigmoid(acc)
            y = jnp.dot(hmid.astype(BF16), w2_ref[kv], preferred_element_type=F32)
            if kv == 0:
                y = _rms(y) * gain_ref[...]
            o_ref[0, kv, h] = y.astype(BF16)


def _cmp_call(o32, pos, w1, w2, gain, batch, seq, ncp):
    o32v = o32.reshape(batch, seq, W32)
    hkv = N_NSA_KV_HEADS
    width = 2 * NSA_KV_WIDTH
    return pl.pallas_call(
        functools.partial(_cmp_kernel, seq=seq, ncp=ncp),
        grid=(batch,),
        in_specs=[pl.BlockSpec((1, seq, width), lambda b: (b, 0, 0)),
                  pl.BlockSpec((2, CMP_BLOCK, HEAD_DIM), lambda b: (0, 0, 0)),
                  pl.BlockSpec((2, CMP_BLOCK, HEAD_DIM, HEAD_DIM), lambda b: (0, 0, 0, 0)),
                  pl.BlockSpec((2, HEAD_DIM, HEAD_DIM), lambda b: (0, 0, 0)),
                  pl.BlockSpec((1, HEAD_DIM), lambda b: (0, 0))],
        out_specs=pl.BlockSpec((1, 2, hkv, ncp, HEAD_DIM), lambda b: (b, 0, 0, 0, 0)),
        out_shape=jax.ShapeDtypeStruct((batch, 2, hkv, ncp, HEAD_DIM), BF16),
        scratch_shapes=[pltpu.VMEM((2 * hkv, CMP_STRIDE * ncp + CMP_BLOCK, LANES), F32)],
        compiler_params=_params(1),
        name="nsa_compress",
    )(o32v, pos, w1, w2, gain)


def _bias_kernel(rb_ref, o_ref, *, width, key_stride, key_offset, first_tile, rolled_tiles):
    v = pl.program_id(0) + first_tile
    i = lax.broadcasted_iota(jnp.int32, (LANES, width), 0)
    j = lax.broadcasted_iota(jnp.int32, (LANES, width), 1)
    d = v * LANES + i - (key_stride * j + key_offset)
    n = jnp.maximum(d, 0)
    max_exact = N_BUCKETS // 2
    nf = jnp.maximum(n, 1).astype(F32)
    large = max_exact + jnp.trunc(jnp.log(nf / max_exact) / math.log(MAX_DISTANCE / max_exact)
                                  * (N_BUCKETS - max_exact))
    large = jnp.minimum(large, float(N_BUCKETS - 1))
    bkt = jnp.where(n < max_exact, n.astype(F32), large)
    vals = [jnp.zeros((LANES, width), F32) for _ in range(N_NSA_HEADS)]
    for bk in range(N_BUCKETS):
        hit = bkt == float(bk)
        for h in range(N_NSA_HEADS):
            vals[h] = jnp.where(hit, rb_ref[bk * N_NSA_HEADS + h] * LOG2E, vals[h])
    for h in range(N_NSA_HEADS):
        g = h % NSA_GROUP
        rows = slice(g * LANES, (g + 1) * LANES)
        if rolled_tiles:
            per_tile = LANES // key_stride
            for t in range(rolled_tiles):
                shift = (width - per_tile * (rolled_tiles - 1 - t)) % width
                o_ref[t, h // NSA_GROUP, rows, :] = vals[h] if shift == 0 else pltpu.roll(vals[h], shift, 1)
        else:
            o_ref[0, h // NSA_GROUP, rows, :] = vals[h]


def _bias_call(rb_flat, n_tiles, width, key_stride, key_offset, name, rolled=False):
    kern = functools.partial(_bias_kernel, width=width, key_stride=key_stride, key_offset=key_offset,
                             first_tile=n_tiles - 1 if rolled else 0, rolled_tiles=n_tiles if rolled else 0)
    block = (n_tiles if rolled else 1, N_NSA_KV_HEADS, GROUP_ROWS, width)
    return pl.pallas_call(
        kern,
        grid=(1 if rolled else n_tiles,),
        in_specs=[pl.BlockSpec(memory_space=pltpu.SMEM)],
        out_specs=pl.BlockSpec(block, lambda v: (v, 0, 0, 0)),
        out_shape=jax.ShapeDtypeStruct((n_tiles, N_NSA_KV_HEADS, GROUP_ROWS, width), F32),
        compiler_params=_params(1),
        name=name,
    )(rb_flat)


def _tile4(a):
    return jnp.concatenate([a] * NSA_GROUP, axis=0)


def _nsa_kernel(q_ref, ks_ref, kw_ref, vs_ref, vw_ref, g_ref, kcvc_ref, bc_ref, tz_ref, selmt_ref,
                wa_ref, wb_ref, wc_ref, wd_ref, o_ref, wa_out, wb_out, wc_out, wd_out,
                qg_scr, qw_scr, caug_scr, near_scr, edge_scr, m_scr, acc_scr, o_scr, s_scr,
                ksa_scr, vsa_scr, kwp_scr, vwa_scr, *, seq, ncp):
    qt = pl.program_id(1)
    q0 = qt * LANES
    n_slc = seq // SEL_BLOCK
    top_n = min(N_SEL, n_slc)
    nkv = N_NSA_KV_HEADS
    for src, dst in ((wa_ref, wa_out), (wb_ref, wb_out), (wc_ref, wc_out), (wd_ref, wd_out)):
        dst[...] = src[...].astype(BF16)
    ri = lax.broadcasted_iota(jnp.int32, (LANES, LANES), 0)
    ci = lax.broadcasted_iota(jnp.int32, (LANES, LANES), 1)
    eye = jnp.where(ri == ci, 1.0, 0.0).astype(BF16)
    gates = _sigmoid(g_ref[0])

    def gate_col(hk, br):
        cols = []
        for g in range(NSA_GROUP):
            c = COL_NG % LANES + (hk * NSA_GROUP + g) * N_BRANCH + br
            cols.append(gates[:, c:c + 1])
        return jnp.concatenate(cols, axis=0)

    @pl.when(qt == 0)
    def _():
        ones = jnp.ones((seq, HEAD_DIM), BF16)
        row = lax.broadcasted_iota(jnp.int32, (LANES, LANES), 0)
        is_far = jnp.logical_and(row >= AUG_FAR, row < AUG_FAR + 3)
        pad_aug = jnp.where(row == AUG_PAD, -MASK_BIG, jnp.where(is_far, 1.0, 0.0)).astype(BF16)
        win_aug = jnp.where(is_far, 1.0, 0.0).astype(BF16)
        lane512 = lax.broadcasted_iota(jnp.int32, (GROUP_ROWS, LANES), 1)
        causal = _tile4(jnp.where(ci <= ri, 0.0, -MASK_BIG))
        edge_scr[...] = jnp.concatenate([jnp.full((GROUP_ROWS, LANES), -MASK_BIG, F32),
                                         _tile4(jnp.where(ri < ci, 0.0, -MASK_BIG))], axis=1)
        for hk in range(nkv):
            hc = slice(hk * HEAD_DIM, (hk + 1) * HEAD_DIM)
            far = tz_ref[2, hk]
            hi, mid, lo = _split3(far)
            caug = jnp.where(lane512 == AUG_FAR, hi, jnp.where(lane512 == AUG_FAR + 1, mid,
                             jnp.where(lane512 == AUG_FAR + 2, lo, jnp.where(lane512 == AUG_PAD, 1.0, 0.0))))
            caug_scr[hk] = caug.astype(BF16)
            qw_scr[hk, :, HEAD_DIM:] = caug.astype(BF16)
            near_scr[hk] = jnp.concatenate([tz_ref[1, hk] - far, (tz_ref[0, hk] - far) + causal], axis=1)
            ksa_scr[hk, 0, 0:HEAD_DIM, :] = jnp.zeros((HEAD_DIM, LANES), BF16)
            ksa_scr[hk, 0, HEAD_DIM:, :] = pad_aug
            for t in range(seq // LANES):
                rows = slice(t * LANES, (t + 1) * LANES)
                ksa_scr[hk, SEL_PAD_TILES + t, 0:HEAD_DIM, :] = ks_ref[0, rows, hc].T
                ksa_scr[hk, SEL_PAD_TILES + t, HEAD_DIM:, :] = jnp.where(
                    (t * LANES + ci) // SEL_BLOCK == ri, -MASK_BIG, jnp.where(is_far, 1.0, 0.0)).astype(BF16)
                kwp_scr[hk, WIN_PAD // LANES + t, 0:HEAD_DIM, :] = kw_ref[0, rows, hc].T
                kwp_scr[hk, WIN_PAD // LANES + t, HEAD_DIM:, :] = win_aug
            for t in range(WIN_PAD // LANES):
                kwp_scr[hk, t, 0:HEAD_DIM, :] = jnp.zeros((HEAD_DIM, LANES), BF16)
                kwp_scr[hk, t, HEAD_DIM:, :] = pad_aug
            vsa_scr[hk, 0:SEL_PAD_TILES * LANES, :] = jnp.zeros((SEL_PAD_TILES * LANES, 2 * HEAD_DIM), BF16)
            vsa_scr[hk, SEL_PAD_TILES * LANES:, 0:HEAD_DIM] = vs_ref[0, :, hc]
            vsa_scr[hk, SEL_PAD_TILES * LANES:, HEAD_DIM:] = ones
            vwa_scr[hk, 0:WIN_PAD, :] = jnp.zeros((WIN_PAD, 2 * HEAD_DIM), BF16)
            vwa_scr[hk, WIN_PAD:, 0:HEAD_DIM] = vw_ref[0, :, hc]
            vwa_scr[hk, WIN_PAD:, HEAD_DIM:] = ones

    def reset(br):
        m_scr[br] = jnp.full(m_scr.shape[1:], NEG, F32)
        acc_scr[br] = jnp.zeros(acc_scr.shape[1:], F32)

    def online_update(br, hk, s, vaug):
        m_prev = m_scr[br, hk]
        m_new = jnp.maximum(m_prev, jnp.max(s, axis=1, keepdims=True))
        alpha = jnp.exp2(m_prev - m_new)
        p = jnp.exp2(s - _lane_tile(m_new, s.shape[1] // LANES))
        pv = jnp.dot(p.astype(BF16), vaug, preferred_element_type=F32)
        acc_scr[br, hk] = _lane_tile(alpha, 2) * acc_scr[br, hk] + pv
        m_scr[br, hk] = m_new

    def finish(br, hk):
        acc = acc_scr[br, hk]
        return acc[:, 0:HEAD_DIM] / acc[:, HEAD_DIM:]

    for hk in range(nkv):
        for g in range(NSA_GROUP):
            h = hk * NSA_GROUP + g
            q_h = q_ref[0, :, h * HEAD_DIM:(h + 1) * HEAD_DIM]
            qg_scr[hk, g * LANES:(g + 1) * LANES, 0:HEAD_DIM] = q_h
            qw_scr[hk, g * LANES:(g + 1) * LANES, 0:HEAD_DIM] = q_h

    for hk in range(nkv):
        qg = qg_scr[hk, :, 0:HEAD_DIM]

        kc = kcvc_ref[0, 0, hk]
        vc = kcvc_ref[0, 1, hk]
        s = lax.dot_general(qg, kc, NT_DIMS, preferred_element_type=F32) + bc_ref[0, hk]
        rc = lax.broadcasted_iota(jnp.int32, (LANES, ncp), 0)
        cc = lax.broadcasted_iota(jnp.int32, (LANES, ncp), 1)
        valid_c = _tile4(jnp.where(q0 + rc - (CMP_STRIDE * cc + CMP_BLOCK - 1) >= 0, 1.0, 0.0)) > 0.5
        s = jnp.where(valid_c, s, NEG)
        p = jnp.where(valid_c, jnp.exp2(s - jnp.max(s, axis=1, keepdims=True)), 0.0)
        l = jnp.sum(p, axis=1, keepdims=True)
        p = p / jnp.where(l > 0.0, l, 1.0)
        o_scr[hk] = gate_col(hk, 0) * jnp.dot(p.astype(BF16), vc, preferred_element_type=F32)

        psum = p[0:LANES]
        for g in range(1, NSA_GROUP):
            psum = psum + p[g * LANES:(g + 1) * LANES]
        p_hi = psum.astype(BF16)
        p_lo = (psum - p_hi.astype(F32)).astype(BF16)
        selmt = selmt_ref[...]
        imp = (lax.dot_general(selmt, p_hi, NT_DIMS, preferred_element_type=F32)
               + lax.dot_general(selmt, p_lo, NT_DIMS, preferred_element_type=F32))
        imp = imp[0:n_slc]
        blk = lax.broadcasted_iota(jnp.int32, (n_slc, LANES), 0)
        cur = (q0 + lax.broadcasted_iota(jnp.int32, (n_slc, LANES), 1)) // SEL_BLOCK
        forced = (blk == 0) | (blk == cur) | (blk == cur - 1)
        imp = jnp.where(forced, FORCE, imp)
        imp = jnp.where(blk <= cur, imp, -jnp.inf)
        rank = jnp.zeros((n_slc, LANES), F32)
        for j in range(n_slc):
            row = imp[j:j + 1, :]
            beats = jnp.where(row > imp, 1.0, jnp.where(row == imp, jnp.where(blk > j, 1.0, 0.0), 0.0))
            rank = rank + beats
        sel_t = jnp.where(rank < top_n, jnp.where(imp > -jnp.inf, 1.0, 0.0), 0.0)
        if n_slc < LANES:
            sel_t = jnp.concatenate([sel_t, jnp.zeros((LANES - n_slc, LANES), F32)], axis=0)
        sel_q = lax.dot_general(eye, sel_t.astype(BF16), NT_DIMS, preferred_element_type=F32)
        not_sel = jnp.where(ci < n_slc, 1.0 - sel_q, 0.0).astype(BF16)
        for g in range(NSA_GROUP):
            rows = slice(g * LANES, (g + 1) * LANES)
            qg_scr[hk, rows, HEAD_DIM:] = jnp.where(ci < n_slc, not_sel, caug_scr[hk, rows, :])

    reset(WIN)
    for off, table in ((1, near_scr), (3, None), (5, edge_scr)):
        t0 = qt - off + WIN_PAD // LANES
        p0 = pl.multiple_of(t0 * LANES, LANES)
        for hk in range(nkv):
            k_t = jnp.concatenate([kwp_scr[hk, t0], kwp_scr[hk, t0 + 1]], axis=1)
            s = jnp.dot(qw_scr[hk], k_t, preferred_element_type=F32)
            if table is not None:
                s = s + (table[hk] if table is near_scr else table[...])
            online_update(WIN, hk, s, vwa_scr[hk, pl.ds(p0, KEY_CHUNK), :])

    reset(SEL)
    n_chunks = (qt + 2) // 2
    first_tile = SEL_PAD_TILES - (qt + 1) % 2

    def sel_scores(hk, c):
        t0 = first_tile + 2 * c
        k_t = jnp.concatenate([ksa_scr[hk, t0], ksa_scr[hk, t0 + 1]], axis=1)
        return jnp.dot(qg_scr[hk], k_t, preferred_element_type=F32)

    def sel_chunk(c, last):
        p0 = pl.multiple_of((first_tile + 2 * c) * LANES, LANES)
        for hk in range(nkv):
            s = s_scr[hk]
            if last:
                s = s + near_scr[hk]
            else:
                s_scr[hk] = sel_scores(hk, c + 1)
            online_update(SEL, hk, s, vsa_scr[hk, pl.ds(p0, KEY_CHUNK), :])

    def sel_pair(i, carry):
        sel_chunk(2 * i, False)
        sel_chunk(2 * i + 1, False)
        return carry

    for hk in range(nkv):
        s_scr[hk] = sel_scores(hk, 0)
    n_far = n_chunks - 1
    lax.fori_loop(0, n_far // 2, sel_pair, 0)
    pl.when(n_far % 2 == 1)(lambda: sel_chunk(n_far - 1, False))
    sel_chunk(n_chunks - 1, True)
    for hk in range(nkv):
        o = (o_scr[hk] + gate_col(hk, 1) * finish(SEL, hk)) + gate_col(hk, 2) * finish(WIN, hk)
        for g in range(NSA_GROUP):
            h = hk * NSA_GROUP + g
            o_ref[0, :, h * HEAD_DIM:(h + 1) * HEAD_DIM] = o[g * LANES:(g + 1) * LANES]


def _nsa_call(o16v, o32v, kcvc, bias_c, tz, selmt, weights_f32, batch, seq, ncp):
    nqt = seq // LANES
    wspecs = []
    for w in weights_f32:
        blk, span = _cast_block(w, batch * nqt)
        wspecs.append(pl.BlockSpec(blk, functools.partial(lambda b, t, span: ((b * nqt + t) // span, 0), span=span)))
    kvw = NSA_KV_WIDTH
    nkv = N_NSA_KV_HEADS
    base = (3 * FOX_WIDTH + NSA_WIDTH) // kvw
    return pl.pallas_call(
        functools.partial(_nsa_kernel, seq=seq, ncp=ncp),
        grid=(batch, nqt),
        in_specs=[pl.BlockSpec((1, LANES, NSA_WIDTH), lambda b, t: (b, t, 3 * FOX_WIDTH // NSA_WIDTH)),
                  pl.BlockSpec((1, seq, kvw), lambda b, t: (b, 0, base)),
                  pl.BlockSpec((1, seq, kvw), lambda b, t: (b, 0, base + 1)),
                  pl.BlockSpec((1, seq, kvw), lambda b, t: (b, 0, base + 2)),
                  pl.BlockSpec((1, seq, kvw), lambda b, t: (b, 0, base + 3)),
                  pl.BlockSpec((1, LANES, LANES), lambda b, t: (b, t, COL_NG // LANES)),
                  pl.BlockSpec((1, 2, N_NSA_KV_HEADS, ncp, HEAD_DIM), lambda b, t: (b, 0, 0, 0, 0)),
                  pl.BlockSpec((1, N_NSA_KV_HEADS, GROUP_ROWS, ncp), lambda b, t: (t, 0, 0, 0)),
                  pl.BlockSpec((3, N_NSA_KV_HEADS, GROUP_ROWS, LANES), lambda b, t: (0, 0, 0, 0)),
                  pl.BlockSpec((LANES, ncp), lambda b, t: (0, 0))] + wspecs,
        out_specs=[pl.BlockSpec((1, LANES, NSA_WIDTH), lambda b, t: (b, t, 0))] + wspecs,
        out_shape=[jax.ShapeDtypeStruct((batch, seq, NSA_WIDTH), F32)]
        + [jax.ShapeDtypeStruct(w.shape, BF16) for w in weights_f32],
        scratch_shapes=[pltpu.VMEM((nkv, GROUP_ROWS, 2 * HEAD_DIM), BF16),
                        pltpu.VMEM((nkv, GROUP_ROWS, 2 * HEAD_DIM), BF16),
                        pltpu.VMEM((nkv, GROUP_ROWS, LANES), BF16),
                        pltpu.VMEM((nkv, GROUP_ROWS, KEY_CHUNK), F32),
                        pltpu.VMEM((GROUP_ROWS, KEY_CHUNK), F32),
                        pltpu.VMEM((2, nkv, GROUP_ROWS, LANES), F32),
                        pltpu.VMEM((2, nkv, GROUP_ROWS, 2 * HEAD_DIM), F32),
                        pltpu.VMEM((nkv, GROUP_ROWS, HEAD_DIM), F32),
                        pltpu.VMEM((nkv, GROUP_ROWS, KEY_CHUNK), F32),
                        pltpu.VMEM((nkv, SEL_PAD_TILES + seq // LANES, 2 * HEAD_DIM, LANES), BF16),
                        pltpu.VMEM((nkv, SEL_PAD_TILES * LANES + seq, 2 * HEAD_DIM), BF16),
                        pltpu.VMEM((nkv, (seq + WIN_PAD) // LANES, 2 * HEAD_DIM, LANES), BF16),
                        pltpu.VMEM((nkv, seq + WIN_PAD, 2 * HEAD_DIM), BF16)],
        compiler_params=_params(2),
        name="nsa_attention",
    )(o16v, o16v, o16v, o16v, o16v, o32v, kcvc, bias_c, tz, selmt, *weights_f32)


def _out_kernel(of_ref, on_ref, gain_ref, w_ref, x_ref, g_ref, o_ref, y_scr, *, tm, n_chains):
    rows_per_chain = tm // n_chains
    gain_f = gain_ref[:, 0:FOX_WIDTH]
    gain_n = gain_ref[:, FOX_WIDTH:MIX_WIDTH]
    for c in range(n_chains):
        rows = slice(c * rows_per_chain, (c + 1) * rows_per_chain)
        for r0 in range(c * rows_per_chain, (c + 1) * rows_per_chain, NORM_ROWS):
            r = slice(r0, r0 + NORM_ROWS)
            y_scr[r, 0:FOX_WIDTH] = (_rms(of_ref[r, :]) * gain_f).astype(BF16)
            y_scr[r, FOX_WIDTH:MIX_WIDTH] = (_rms(on_ref[r, :]) * gain_n).astype(BF16)
        acc = jnp.dot(y_scr[rows, :], w_ref[...], preferred_element_type=F32)
        o_ref[rows, :] = x_ref[rows, :] + g_ref[0] * acc


def _out_call(o_fox, o_nsa, gain, w_out, x2, g1, seq):
    t, d = x2.shape
    tm = min(512, seq)
    rows_per_batch = seq // tm
    return pl.pallas_call(
        functools.partial(_out_kernel, tm=tm, n_chains=2),
        grid=(t // tm,),
        in_specs=[pl.BlockSpec((tm, FOX_WIDTH), lambda i: (i, 0)),
                  pl.BlockSpec((tm, NSA_WIDTH), lambda i: (i, 0)),
                  pl.BlockSpec((1, MIX_WIDTH), lambda i: (0, 0)),
                  pl.BlockSpec((MIX_WIDTH, d), lambda i: (0, 0)),
                  pl.BlockSpec((tm, d), lambda i: (i, 0)),
                  pl.BlockSpec((1, 1, d), lambda i: (i // rows_per_batch, 0, 0))],
        out_specs=pl.BlockSpec((tm, d), lambda i: (i, 0)),
        out_shape=jax.ShapeDtypeStruct((t, d), F32),
        scratch_shapes=[pltpu.VMEM((tm, MIX_WIDTH), BF16)],
        compiler_params=_params(1),
        name="out_proj",
    )(o_fox, o_nsa, gain, w_out, x2, g1)


def _ffn_kernel(x_ref, gain_ref, sc_ref, sh_ref, g_ref, wg_ref, wu_ref, wd_ref, o_ref, h_scr, acc_scr):
    f = pl.program_id(1)

    @pl.when(f == 0)
    def _():
        _modulated_norm(x_ref, gain_ref, sc_ref, sh_ref, h_scr)
        acc_scr[...] = jnp.zeros(acc_scr.shape, F32)

    h = h_scr[...]
    a = jnp.dot(h, wg_ref[...], preferred_element_type=F32)
    u = jnp.dot(h, wu_ref[...], preferred_element_type=F32)
    t = (a * _sigmoid(a)) * u
    acc_scr[...] += jnp.dot(t.astype(BF16), wd_ref[...], preferred_element_type=F32)

    @pl.when(f == pl.num_programs(1) - 1)
    def _():
        o_ref[...] = x_ref[...] + g_ref[0] * acc_scr[...]


def _ffn_call(x1, gain2, sc, sh, g2, wg, wu, wd, seq):
    t, d = x1.shape
    dff = wg.shape[1]
    tm = min(512, seq)
    tf = 512 if dff % 512 == 0 else dff
    rows_per_batch = seq // tm
    return pl.pallas_call(
        _ffn_kernel,
        grid=(t // tm, dff // tf),
        in_specs=[pl.BlockSpec((tm, d), lambda i, f: (i, 0)),
                  pl.BlockSpec((1, d), lambda i, f: (0, 0)),
                  pl.BlockSpec((1, 1, d), lambda i, f: (i // rows_per_batch, 0, 0)),
                  pl.BlockSpec((1, 1, d), lambda i, f: (i // rows_per_batch, 0, 0)),
                  pl.BlockSpec((1, 1, d), lambda i, f: (i // rows_per_batch, 0, 0)),
                  pl.BlockSpec((d, tf), lambda i, f: (0, f)),
                  pl.BlockSpec((d, tf), lambda i, f: (0, f)),
                  pl.BlockSpec((tf, d), lambda i, f: (f, 0))],
        out_specs=pl.BlockSpec((tm, d), lambda i, f: (i, 0)),
        out_shape=jax.ShapeDtypeStruct((t, d), F32),
        scratch_shapes=[pltpu.VMEM((tm, d), BF16), pltpu.VMEM((tm, d), F32)],
        compiler_params=_params(2),
        name="swiglu_ffn",
    )(x1, gain2, sc, sh, g2, wg, wu, wd)


def _selection_matrix_t(ncp, n_slc):
    r, q = SEL_BLOCK // CMP_STRIDE, CMP_BLOCK // CMP_STRIDE
    m = np.zeros((LANES, ncp), np.float32)
    for j in range(n_slc):
        for a in range(r):
            for b in range(q):
                c = r * j + a - b
                if 0 <= c < ncp:
                    m[j, c] += 1.0
    return m


def _w_in_block_sources():
    o = 0
    start = {}
    for name, width in (("fq", FOX_WIDTH), ("fk", FOX_WIDTH), ("fv", FOX_WIDTH), ("ff", N_FOX_HEADS),
                        ("nq", NSA_WIDTH), ("nk", N_BRANCH * NSA_KV_WIDTH), ("nv", N_BRANCH * NSA_KV_WIDTH),
                        ("ng", N_BRANCH * N_NSA_HEADS)):
        start[name] = o
        o += width
    kvw = NSA_KV_WIDTH
    groups = [(start["fq"], 3 * FOX_WIDTH), (start["nq"], NSA_WIDTH),
              (start["nk"] + kvw, 2 * kvw), (start["nv"] + kvw, 2 * kvw),
              (start["nk"], kvw), (start["nv"], kvw)]
    blocks = [s + LANES * b for s, width in groups for b in range(width // LANES)]
    return blocks, (start["ff"], N_FOX_HEADS), (start["ng"], N_BRANCH * N_NSA_HEADS)


def _repack_kernel(src_ref, wt_hbm, o16_ref, o32_ref, buf, sem, *, n16, n_whole, ff, ng):
    k = pl.program_id(0)
    n_slots = buf.shape[0]
    ahead = n_slots - 1
    slot = k % n_slots

    def whole_copy(kk, s):
        r0 = pl.multiple_of(src_ref[kk], 8)
        return pltpu.make_async_copy(wt_hbm.at[pl.ds(r0, LANES), :], buf.at[s], sem.at[s])

    def narrow_copies(s):
        return [pltpu.make_async_copy(wt_hbm.at[pl.ds(ff[0], ff[1]), :], buf.at[s, pl.ds(0, ff[1]), :], sem.at[s]),
                pltpu.make_async_copy(wt_hbm.at[pl.ds(ng[0], ng[1]), :], buf.at[s, pl.ds(ff[1], ng[1]), :],
                                      sem.at[s])]

    def start_block(kk):
        @pl.when(kk < n_whole)
        def _():
            whole_copy(kk, kk % n_slots).start()

        @pl.when(kk == n_whole)
        def _():
            for cp in narrow_copies(kk % n_slots):
                cp.start()

    @pl.when(k == 0)
    def _():
        for kk in range(ahead):
            start_block(kk)

    start_block(k + ahead)

    @pl.when(k < n_whole)
    def _():
        whole_copy(k, slot).wait()

    @pl.when(k == n_whole)
    def _():
        for cp in narrow_copies(slot):
            cp.wait()

    xt = buf[slot].T
    lane = lax.broadcasted_iota(jnp.int32, xt.shape, 1)
    xt = jnp.where(jnp.logical_or(k < n_whole, lane < ff[1] + ng[1]), xt, 0.0).astype(BF16)

    @pl.when(k < n16)
    def _():
        o16_ref[...] = xt

    @pl.when(k >= n16)
    def _():
        o32_ref[...] = xt


def _repack_w_in(wt):
    n, d = wt.shape
    blocks, ff, ng = _w_in_block_sources()
    n16 = W16 // LANES
    n_whole = len(blocks)
    assert n_whole + 1 == (W16 + W32) // LANES
    return pl.pallas_call(
        functools.partial(_repack_kernel, n16=n16, n_whole=n_whole, ff=ff, ng=ng),
        grid_spec=pltpu.PrefetchScalarGridSpec(
            num_scalar_prefetch=1,
            grid=(n_whole + 1,),
            in_specs=[pl.BlockSpec(memory_space=pl.ANY)],
            out_specs=[pl.BlockSpec((d, LANES), lambda k, src: (0, jnp.minimum(k, n16 - 1))),
                       pl.BlockSpec((d, LANES), lambda k, src: (0, jnp.maximum(k - n16, 0)))],
            scratch_shapes=[pltpu.VMEM((REPACK_SLOTS, LANES, d), F32), pltpu.SemaphoreType.DMA((REPACK_SLOTS,))]),
        out_shape=[jax.ShapeDtypeStruct((d, W16), BF16), jax.ShapeDtypeStruct((d, W32), BF16)],
        compiler_params=_params(1),
        name="w_in_repack",
    )(jnp.asarray(blocks, jnp.int32), wt)


def kernel(x, c, ada_w, ada_b, norm1_gain, norm2_gain, w_in, fox_f_bias, fox_q_gain, fox_k_gain, nsa_q_gain,
           nsa_k_gain, nsa_cmp_pos, nsa_cmp_w1, nsa_cmp_w2, rel_bias, mix_out_gain, w_out, ffn_w_gate, ffn_w_up,
           ffn_w_down):
    batch, seq, d = x.shape
    assert seq % KEY_CHUNK == 0 and seq >= WINDOW and d % LANES == 0 and seq // SEL_BLOCK <= LANES
    depth = ada_w.shape[0]
    nqt = seq // LANES
    ncp = -(-(seq // CMP_STRIDE) // LANES) * LANES

    selmt = jnp.asarray(_selection_matrix_t(ncp, seq // SEL_BLOCK), BF16)
    rb_flat = rel_bias.reshape(-1)
    bias_c = _bias_call(rb_flat, nqt, ncp, CMP_STRIDE, CMP_BLOCK - 1, "t5_bias_compressed", rolled=True)
    tz = _bias_call(rb_flat, 3, LANES, 1, 0, "t5_bias_toeplitz")

    ones_h = jnp.ones((HEAD_DIM,), F32)
    c_pad = jnp.pad(c, ((0, 8 - batch % 8 if batch % 8 else 0), (0, 0)))
    x2 = x.reshape(batch * seq, d)
    for layer in range(depth):
        mod = _ada_call(c_pad, ada_w[layer], ada_b[layer][None, :])[:batch]
        sh1, sc1, g1, sh2, sc2, g2 = [mod[:, i * d:(i + 1) * d][:, None, :] for i in range(N_MOD)]

        kg = nsa_k_gain[layer]
        col_gain = jnp.concatenate([
            jnp.tile(fox_q_gain[layer] * QSCALE, N_FOX_HEADS), jnp.tile(fox_k_gain[layer], N_FOX_HEADS),
            jnp.tile(ones_h, N_FOX_HEADS), jnp.tile(nsa_q_gain[layer] * QSCALE, N_NSA_HEADS),
            jnp.tile(kg[1], N_NSA_KV_HEADS), jnp.tile(kg[2], N_NSA_KV_HEADS),
            jnp.tile(ones_h, 2 * N_NSA_KV_HEADS)])[None, :]
        col_flag = jnp.concatenate([
            jnp.ones((2 * FOX_WIDTH,), F32), jnp.zeros((FOX_WIDTH,), F32), jnp.ones((NSA_WIDTH,), F32),
            jnp.ones((2 * NSA_KV_WIDTH,), F32), jnp.zeros((2 * NSA_KV_WIDTH,), F32)])[None, :]
        w16, w32 = _repack_w_in(jnp.swapaxes(w_in, 1, 2)[layer])
        o16, o32 = _proj_call(x2, sc1, sh1, norm1_gain[layer][None, :], w16, w32,
                              col_gain, col_flag, seq)
        o16v = o16.reshape(batch, seq, W16)
        o32v = o32.reshape(batch, seq, W32)

        fb_pad = jnp.pad(fox_f_bias[layer], (0, LANES - N_FOX_HEADS))[None, :]
        cs = _cum_call(o32, fb_pad, batch, seq, N_FOX_HEADS // FOX_HEADS_PER_STEP)
        o_fox = _fox_call(o16v, cs, batch, seq)

        w1 = nsa_cmp_w1[layer].reshape(2, CMP_BLOCK, HEAD_DIM, HEAD_DIM).astype(BF16)
        kcvc = _cmp_call(o32, nsa_cmp_pos[layer], w1, nsa_cmp_w2[layer].astype(BF16), kg[0][None, :],
                         batch, seq, ncp)
        o_nsa, wg16, wu16, wo16, wd16 = _nsa_call(
            o16v, o32v, kcvc, bias_c, tz, selmt,
            (ffn_w_gate[layer], ffn_w_up[layer], w_out[layer], ffn_w_down[layer]), batch, seq, ncp)

        x1 = _out_call(o_fox.reshape(batch * seq, FOX_WIDTH), o_nsa.reshape(batch * seq, NSA_WIDTH),
                       mix_out_gain[layer][None, :], wo16, x2, g1, seq)
        x2 = _ffn_call(x1, norm2_gain[layer][None, :], sc2, sh2, g2, wg16, wu16, wd16, seq)
    return x2.reshape(batch, seq, d)
```

```python
import functools
import math

import numpy as np
import jax
import jax.numpy as jnp
from jax import lax
from jax.experimental import pallas as pl
from jax.experimental.pallas import tpu as pltpu

HEAD_DIM = 128
N_FOX_HEADS = 8
N_NSA_HEADS = 8
N_NSA_KV_HEADS = 2
NSA_GROUP = N_NSA_HEADS // N_NSA_KV_HEADS
FOX_WIDTH = N_FOX_HEADS * HEAD_DIM
NSA_WIDTH = N_NSA_HEADS * HEAD_DIM
NSA_KV_WIDTH = N_NSA_KV_HEADS * HEAD_DIM
MIX_WIDTH = FOX_WIDTH + NSA_WIDTH
N_BRANCH = 3
CMP_BLOCK = 32
CMP_STRIDE = 16
SEL_BLOCK = 64
N_SEL = 8
WINDOW = 512
N_BUCKETS = 32
MAX_DISTANCE = 128
N_MOD = 6
SCALE = HEAD_DIM ** -0.5
LOG2E = math.log2(math.e)
LOG2E_HI = float(np.float32(LOG2E))
LOG2E_LO = LOG2E - LOG2E_HI
QSCALE = SCALE * LOG2E
EPS = 1e-6
NEG = -1e30
FORCE = 1e6

LANES = 128
GROUP_ROWS = NSA_GROUP * LANES
VMEM_LIMIT = 56 * 1024 * 1024
MXU_COLS = 256
KEY_CHUNK = MXU_COLS
WIN_PAD = WINDOW + LANES
MASK_BIG = 2.0 ** 100
SEL, WIN = 0, 1
REPACK_SLOTS = 4
AUG_FAR, AUG_PAD = 120, 127
FOX_HEADS_PER_STEP = 4
SEL_PAD_TILES = 1

W16 = 3 * FOX_WIDTH + NSA_WIDTH + 4 * NSA_KV_WIDTH
W32 = 5 * LANES
COL_FF = 2 * NSA_KV_WIDTH
COL_NG = COL_FF + N_FOX_HEADS

F32 = jnp.float32
BF16 = jnp.bfloat16
NT_DIMS = (((1,), (1,)), ((), ()))


def _params(n_axes):
    return pltpu.CompilerParams(dimension_semantics=("arbitrary",) * n_axes,
                                vmem_limit_bytes=VMEM_LIMIT)


def _sigmoid(x):
    return 1.0 / (1.0 + jnp.exp(-x))


def _lane_tile(a, n):
    return jnp.concatenate([a] * n, axis=1)


BF16_SUBLANES = 16


def _cast_block(w, n_steps):
    rows, cols = w.shape
    assert rows % n_steps == 0, (w.shape, n_steps)
    per_step = rows // n_steps
    span = BF16_SUBLANES // math.gcd(BF16_SUBLANES, per_step)
    assert n_steps % span == 0, (w.shape, n_steps)
    return (per_step * span, cols), span


def _rms(x):
    return x * lax.rsqrt(jnp.mean(x * x, axis=-1, keepdims=True) + EPS)


NORM_ROWS = 16


def _modulated_norm(x_ref, gain_ref, sc_ref, sh_ref, h_ref):
    gm = gain_ref[...] * (1.0 + sc_ref[0])
    sh = sh_ref[0]
    for r0 in range(0, x_ref.shape[0], NORM_ROWS):
        rows = slice(r0, r0 + NORM_ROWS)
        h_ref[rows, :] = (_rms(x_ref[rows, :]) * gm + sh).astype(BF16)


def _ada_kernel(c_ref, w_ref, b_ref, o_ref):
    c = c_ref[...]
    s = (c * _sigmoid(c)).astype(BF16)
    o_ref[...] = jnp.dot(s, w_ref[...].astype(BF16), preferred_element_type=F32) + b_ref[...]


def _ada_call(c_pad, w, b):
    rows, d = c_pad.shape
    n = w.shape[1]
    tn = next(t for t in (1024, 768, 512, 384, 256, 128) if n % t == 0)
    return pl.pallas_call(
        _ada_kernel,
        grid=(n // tn,),
        in_specs=[pl.BlockSpec((rows, d), lambda j: (0, 0)),
                  pl.BlockSpec((d, tn), lambda j: (0, j)),
                  pl.BlockSpec((1, tn), lambda j: (0, j))],
        out_specs=pl.BlockSpec((rows, tn), lambda j: (0, j)),
        out_shape=jax.ShapeDtypeStruct((rows, n), F32),
        compiler_params=_params(1),
        name="adaln",
    )(c_pad, w, b)


def _proj_kernel(x_ref, sc_ref, sh_ref, g_ref, w16_ref, w32_ref, gain_ref, flag_ref, o16_ref, o32_ref, h_scr, *, tn):
    j = pl.program_id(1)

    @pl.when(j == 0)
    def _():
        _modulated_norm(x_ref, g_ref, sc_ref, sh_ref, h_scr)

    h = h_scr[...]
    for c in range(tn // MXU_COLS):
        acc = jnp.dot(h, w16_ref[:, c * MXU_COLS:(c + 1) * MXU_COLS], preferred_element_type=F32)
        for g in range(MXU_COLS // LANES):
            cols = slice(c * MXU_COLS + g * LANES, c * MXU_COLS + (g + 1) * LANES)
            a = acc[:, g * LANES:(g + 1) * LANES]
            r = lax.rsqrt(jnp.mean(a * a, axis=-1, keepdims=True) + EPS)
            scale = jnp.where(flag_ref[:, cols] > 0.5, r, 1.0)
            o16_ref[:, cols] = (a * scale * gain_ref[:, cols]).astype(BF16)

    @pl.when(j == pl.num_programs(1) - 1)
    def _():
        o32_ref[...] = jnp.dot(h_scr[...], w32_ref[...], preferred_element_type=F32)


def _proj_call(x2, sc, sh, gain1, w16, w32, col_gain, col_flag, seq):
    t, d = x2.shape
    tm = min(1024, seq)
    tn = 1024
    rows_per_batch = seq // tm
    return pl.pallas_call(
        functools.partial(_proj_kernel, tn=tn),
        grid=(t // tm, W16 // tn),
        in_specs=[pl.BlockSpec((tm, d), lambda i, j: (i, 0)),
                  pl.BlockSpec((1, 1, d), lambda i, j: (i // rows_per_batch, 0, 0)),
                  pl.BlockSpec((1, 1, d), lambda i, j: (i // rows_per_batch, 0, 0)),
                  pl.BlockSpec((1, d), lambda i, j: (0, 0)),
                  pl.BlockSpec((d, tn), lambda i, j: (0, j)),
                  pl.BlockSpec((d, W32), lambda i, j: (0, 0)),
                  pl.BlockSpec((1, tn), lambda i, j: (0, j)),
                  pl.BlockSpec((1, tn), lambda i, j: (0, j))],
        out_specs=[pl.BlockSpec((tm, tn), lambda i, j: (i, j)),
                   pl.BlockSpec((tm, W32), lambda i, j: (i, 0))],
        out_shape=[jax.ShapeDtypeStruct((t, W16), BF16),
                   jax.ShapeDtypeStruct((t, W32), F32)],
        scratch_shapes=[pltpu.VMEM((tm, d), BF16)],
        compiler_params=_params(2),
        name="in_proj",
    )(x2, sc, sh, gain1, w16, w32, col_gain, col_flag)


def _split3(c):
    hi = c.astype(BF16).astype(F32)
    r1 = c - hi
    mid = r1.astype(BF16).astype(F32)
    return hi, mid, (r1 - mid).astype(BF16).astype(F32)


def _cum_kernel(ff_ref, fb_ref, o_ref, *, seq, groups):
    ri = lax.broadcasted_iota(jnp.int32, (LANES, LANES), 0)
    ci = lax.broadcasted_iota(jnp.int32, (LANES, LANES), 1)
    tri = jnp.where(ri >= ci, 1.0, 0.0).astype(BF16)
    carry = jnp.zeros((1, LANES), F32)
    for blk in range(seq // LANES):
        rows = slice(blk * LANES, (blk + 1) * LANES)
        x = ff_ref[0, rows, :] + fb_ref[...]
        lf = jnp.minimum(x, 0.0) - jnp.log(1.0 + jnp.exp(-jnp.abs(x)))
        hi = lf.astype(BF16)
        r1 = lf - hi.astype(F32)
        mid = r1.astype(BF16)
        lo = (r1 - mid.astype(F32)).astype(BF16)
        c = (jnp.dot(tri, hi, preferred_element_type=F32)
             + jnp.dot(tri, mid, preferred_element_type=F32)
             + jnp.dot(tri, lo, preferred_element_type=F32)) + carry
        carry = c[LANES - 1:LANES, :]
        for t, term in enumerate(_split3(c * LOG2E_HI + c * LOG2E_LO)):
            for g in range(groups):
                shift = (-g * (N_FOX_HEADS // groups)) % LANES
                o_ref[0, g, t, rows, :] = term if shift == 0 else pltpu.roll(term, shift, 1)


def _cum_call(o32, fb_pad, batch, seq, groups):
    o32v = o32.reshape(batch, seq, W32)
    return pl.pallas_call(
        functools.partial(_cum_kernel, seq=seq, groups=groups),
        grid=(batch,),
        in_specs=[pl.BlockSpec((1, seq, LANES), lambda b: (b, 0, COL_FF // LANES)),
                  pl.BlockSpec((1, LANES), lambda b: (0, 0))],
        out_specs=pl.BlockSpec((1, groups, 3, seq, LANES), lambda b: (b, 0, 0, 0, 0)),
        out_shape=jax.ShapeDtypeStruct((batch, groups, 3, seq, LANES), F32),
        compiler_params=_params(1),
        name="fox_cumsum",
    )(o32v, fb_pad)


def _fox_kernel(q_ref, k_ref, v_ref, cs_ref, o_ref, m_scr, acc_scr, s_scr, kaug_scr, vaug_scr, *, seq, tq, tk, nh):
    n_diag = tq // tk
    lane_q = lax.broadcasted_iota(jnp.int32, (tq, LANES), 1)
    row8 = lax.broadcasted_iota(jnp.int32, (8, tk), 0)
    ri = lax.broadcasted_iota(jnp.int32, (tq, tk), 0)
    ci = lax.broadcasted_iota(jnp.int32, (tq, tk), 1)
    causal = [ci + d * tk <= ri for d in range(n_diag)]
    for c in range(seq // tk):
        rows = slice(c * tk, (c + 1) * tk)
        terms_t = [cs_ref[0, 0, t, rows, :].T for t in range(3)]
        for j in range(nh):
            tail8 = jnp.where(row8 < 3, 1.0, jnp.where(row8 == 3, -terms_t[0][j:j + 1], jnp.where(
                row8 == 4, -terms_t[1][j:j + 1], jnp.where(row8 == 5, -terms_t[2][j:j + 1], 0.0))))
            kaug_scr[j, c, 0:HEAD_DIM, :] = k_ref[0, rows, j * HEAD_DIM:(j + 1) * HEAD_DIM].T
            kaug_scr[j, c, HEAD_DIM:, :] = jnp.concatenate(
                [tail8, jnp.zeros((HEAD_DIM - 8, tk), F32)], axis=0).astype(BF16)
    for j in range(nh):
        vaug_scr[j, :, 0:HEAD_DIM] = v_ref[0, :, j * HEAD_DIM:(j + 1) * HEAD_DIM]
        vaug_scr[j, :, HEAD_DIM:] = jnp.ones((seq, HEAD_DIM), BF16)

    def q_body(qi, carry):
        q0 = pl.multiple_of(qi * tq, tq)
        terms = [cs_ref[0, 0, t, pl.ds(q0, tq), :] for t in range(3)]
        qs = []
        for j in range(nh):
            hi, mid, lo = [term if t == j else pltpu.roll(term, (t - j) % LANES, 1) for t, term in enumerate(terms)]
            tail = jnp.where(lane_q == 0, hi, jnp.where(lane_q == 1, mid, jnp.where(lane_q == 2, lo,
                             jnp.where(lane_q < 6, 1.0, 0.0))))
            qs.append(jnp.concatenate([q_ref[0, pl.ds(q0, tq), j * HEAD_DIM:(j + 1) * HEAD_DIM],
                                       tail.astype(BF16)], axis=1))
            m_scr[j] = jnp.full((tq, LANES), NEG, F32)
            acc_scr[j] = jnp.zeros((tq, 2 * HEAD_DIM), F32)

        all_rows = slice(0, tq)

        def scores(j, ki, rows):
            return jnp.dot(qs[j][rows], kaug_scr[j, ki], preferred_element_type=F32)

        def tile(ki, mask, rows, next_rows):
            k0 = pl.multiple_of(ki * tk, tk)
            for j in range(nh):
                s = s_scr[j, rows, :]
                if mask is not None:
                    s = jnp.where(mask[rows], s, NEG)
                if next_rows is not None:
                    s_scr[j, next_rows, :] = scores(j, ki + 1, next_rows)
                m_prev = m_scr[j, rows, :]
                m_new = jnp.maximum(m_prev, jnp.max(s, axis=1, keepdims=True))
                alpha = jnp.exp2(m_prev - m_new)
                p = jnp.exp2(s - _lane_tile(m_new, tk // LANES))
                pv = jnp.dot(p.astype(BF16), vaug_scr[j, pl.ds(k0, tk), :], preferred_element_type=F32)
                acc_scr[j, rows, :] = _lane_tile(alpha, 2) * acc_scr[j, rows, :] + pv
                m_scr[j, rows, :] = m_new

        def k_body(ki, c2):
            tile(ki, None, all_rows, all_rows)
            return c2

        for j in range(nh):
            s_scr[j] = scores(j, 0, all_rows)
        n_off = qi * n_diag
        lax.fori_loop(0, n_off, k_body, 0)
        for d in range(n_diag):
            nxt = slice((d + 1) * tk, tq) if d + 1 < n_diag else None
            tile(n_off + d, causal[d], slice(d * tk, tq), nxt)
        for j in range(nh):
            acc = acc_scr[j]
            o_ref[0, pl.ds(q0, tq), j * HEAD_DIM:(j + 1) * HEAD_DIM] = acc[:, 0:HEAD_DIM] / acc[:, HEAD_DIM:]
        return carry

    lax.fori_loop(0, seq // tq, q_body, 0)


def _fox_call(o16v, cs, batch, seq):
    nh = FOX_HEADS_PER_STEP
    tq, tk = min(512, seq), min(256, seq)
    groups = N_FOX_HEADS // nh
    w = nh * HEAD_DIM
    return pl.pallas_call(
        functools.partial(_fox_kernel, seq=seq, tq=tq, tk=tk, nh=nh),
        grid=(batch, groups),
        in_specs=[pl.BlockSpec((1, seq, w), lambda b, h: (b, 0, h)),
                  pl.BlockSpec((1, seq, w), lambda b, h: (b, 0, groups + h)),
                  pl.BlockSpec((1, seq, w), lambda b, h: (b, 0, 2 * groups + h)),
                  pl.BlockSpec((1, 1, 3, seq, LANES), lambda b, h: (b, h, 0, 0, 0))],
        out_specs=pl.BlockSpec((1, seq, w), lambda b, h: (b, 0, h)),
        out_shape=jax.ShapeDtypeStruct((batch, seq, FOX_WIDTH), F32),
        scratch_shapes=[pltpu.VMEM((nh, tq, LANES), F32),
                        pltpu.VMEM((nh, tq, 2 * HEAD_DIM), F32),
                        pltpu.VMEM((nh, tq, tk), F32),
                        pltpu.VMEM((nh, seq // tk, 2 * HEAD_DIM, tk), BF16),
                        pltpu.VMEM((nh, seq, 2 * HEAD_DIM), BF16)],
        compiler_params=_params(2),
        name="fox_attention",
    )(o16v, o16v, o16v, cs)


def _cmp_kernel(x_ref, pos_ref, w1_ref, w2_ref, gain_ref, o_ref, xs_scr, *, seq, ncp):
    rows = xs_scr.shape[1]
    for kv in range(2):
        for h in range(N_NSA_KV_HEADS):
            n = kv * N_NSA_KV_HEADS + h
            xs_scr[n, 0:seq, :] = x_ref[0, :, n * HEAD_DIM:(n + 1) * HEAD_DIM]
            xs_scr[n, seq:rows, :] = jnp.zeros((rows - seq, LANES), F32)
            acc = jnp.zeros((ncp, HEAD_DIM), F32)
            for l in range(CMP_BLOCK):
                xl = xs_scr[n, pl.ds(l, ncp, stride=CMP_STRIDE), :] + pos_ref[kv, l:l + 1, :]
                acc = acc + jnp.dot(xl.astype(BF16), w1_ref[kv, l], preferred_element_type=F32)
            hmid = acc * _sigmoid(acc)
            y = jnp.dot(hmid.astype(BF16), w2_ref[kv], preferred_element_type=F32)
            if kv == 0:
                y = _rms(y) * gain_ref[...]
            o_ref[0, kv, h] = y.astype(BF16)


def _cmp_call(o32, pos, w1, w2, gain, batch, seq, ncp):
    o32v = o32.reshape(batch, seq, W32)
    hkv = N_NSA_KV_HEADS
    width = 2 * NSA_KV_WIDTH
    return pl.pallas_call(
        functools.partial(_cmp_kernel, seq=seq, ncp=ncp),
        grid=(batch,),
        in_specs=[pl.BlockSpec((1, seq, width), lambda b: (b, 0, 0)),
                  pl.BlockSpec((2, CMP_BLOCK, HEAD_DIM), lambda b: (0, 0, 0)),
                  pl.BlockSpec((2, CMP_BLOCK, HEAD_DIM, HEAD_DIM), lambda b: (0, 0, 0, 0)),
                  pl.BlockSpec((2, HEAD_DIM, HEAD_DIM), lambda b: (0, 0, 0)),
                  pl.BlockSpec((1, HEAD_DIM), lambda b: (0, 0))],
        out_specs=pl.BlockSpec((1, 2, hkv, ncp, HEAD_DIM), lambda b: (b, 0, 0, 0, 0)),
        out_shape=jax.ShapeDtypeStruct((batch, 2, hkv, ncp, HEAD_DIM), BF16),
        scratch_shapes=[pltpu.VMEM((2 * hkv, CMP_STRIDE * ncp + CMP_BLOCK, LANES), F32)],
        compiler_params=_params(1),
        name="nsa_compress",
    )(o32v, pos, w1, w2, gain)


def _bias_kernel(rb_ref, o_ref, *, width, key_stride, key_offset, first_tile, rolled_tiles):
    v = pl.program_id(0) + first_tile
    i = lax.broadcasted_iota(jnp.int32, (LANES, width), 0)
    j = lax.broadcasted_iota(jnp.int32, (LANES, width), 1)
    d = v * LANES + i - (key_stride * j + key_offset)
    n = jnp.maximum(d, 0)
    max_exact = N_BUCKETS // 2
    nf = jnp.maximum(n, 1).astype(F32)
    large = max_exact + jnp.trunc(jnp.log(nf / max_exact) / math.log(MAX_DISTANCE / max_exact)
                                  * (N_BUCKETS - max_exact))
    large = jnp.minimum(large, float(N_BUCKETS - 1))
    bkt = jnp.where(n < max_exact, n.astype(F32), large)
    vals = [jnp.zeros((LANES, width), F32) for _ in range(N_NSA_HEADS)]
    for bk in range(N_BUCKETS):
        hit = bkt == float(bk)
        for h in range(N_NSA_HEADS):
            vals[h] = jnp.where(hit, rb_ref[bk * N_NSA_HEADS + h] * LOG2E, vals[h])
    for h in range(N_NSA_HEADS):
        g = h % NSA_GROUP
        rows = slice(g * LANES, (g + 1) * LANES)
        if rolled_tiles:
            per_tile = LANES // key_stride
            for t in range(rolled_tiles):
                shift = (width - per_tile * (rolled_tiles - 1 - t)) % width
                o_ref[t, h // NSA_GROUP, rows, :] = vals[h] if shift == 0 else pltpu.roll(vals[h], shift, 1)
        else:
            o_ref[0, h // NSA_GROUP, rows, :] = vals[h]


def _bias_call(rb_flat, n_tiles, width, key_stride, key_offset, name, rolled=False):
    kern = functools.partial(_bias_kernel, width=width, key_stride=key_stride, key_offset=key_offset,
                             first_tile=n_tiles - 1 if rolled else 0, rolled_tiles=n_tiles if rolled else 0)
    block = (n_tiles if rolled else 1, N_NSA_KV_HEADS, GROUP_ROWS, width)
    return pl.pallas_call(
        kern,
        grid=(1 if rolled else n_tiles,),
        in_specs=[pl.BlockSpec(memory_space=pltpu.SMEM)],
        out_specs=pl.BlockSpec(block, lambda v: (v, 0, 0, 0)),
        out_shape=jax.ShapeDtypeStruct((n_tiles, N_NSA_KV_HEADS, GROUP_ROWS, width), F32),
        compiler_params=_params(1),
        name=name,
    )(rb_flat)


def _tile4(a):
    return jnp.concatenate([a] * NSA_GROUP, axis=0)


def _nsa_kernel(q_ref, ks_ref, kw_ref, vs_ref, vw_ref, g_ref, kcvc_ref, bc_ref, tz_ref, selmt_ref,
                wa_ref, wb_ref, wc_ref, wd_ref, o_ref, wa_out, wb_out, wc_out, wd_out,
                qg_scr, qw_scr, caug_scr, near_scr, edge_scr, m_scr, acc_scr, o_scr, s_scr,
                ksa_scr, vsa_scr, kwp_scr, vwa_scr, *, seq, ncp):
    qt = pl.program_id(1)
    q0 = qt * LANES
    n_slc = seq // SEL_BLOCK
    top_n = min(N_SEL, n_slc)
    nkv = N_NSA_KV_HEADS
    for src, dst in ((wa_ref, wa_out), (wb_ref, wb_out), (wc_ref, wc_out), (wd_ref, wd_out)):
        dst[...] = src[...].astype(BF16)
    ri = lax.broadcasted_iota(jnp.int32, (LANES, LANES), 0)
    ci = lax.broadcasted_iota(jnp.int32, (LANES, LANES), 1)
    eye = jnp.where(ri == ci, 1.0, 0.0).astype(BF16)
    gates = _sigmoid(g_ref[0])

    def gate_col(hk, br):
        cols = []
        for g in range(NSA_GROUP):
            c = COL_NG % LANES + (hk * NSA_GROUP + g) * N_BRANCH + br
            cols.append(gates[:, c:c + 1])
        return jnp.concatenate(cols, axis=0)

    @pl.when(qt == 0)
    def _():
        ones = jnp.ones((seq, HEAD_DIM), BF16)
        row = lax.broadcasted_iota(jnp.int32, (LANES, LANES), 0)
        is_far = jnp.logical_and(row >= AUG_FAR, row < AUG_FAR + 3)
        pad_aug = jnp.where(row == AUG_PAD, -MASK_BIG, jnp.where(is_far, 1.0, 0.0)).astype(BF16)
        win_aug = jnp.where(is_far, 1.0, 0.0).astype(BF16)
        lane512 = lax.broadcasted_iota(jnp.int32, (GROUP_ROWS, LANES), 1)
        causal = _tile4(jnp.where(ci <= ri, 0.0, -MASK_BIG))
        edge_scr[...] = jnp.concatenate([jnp.full((GROUP_ROWS, LANES), -MASK_BIG, F32),
                                         _tile4(jnp.where(ri < ci, 0.0, -MASK_BIG))], axis=1)
        for hk in range(nkv):
            hc = slice(hk * HEAD_DIM, (hk + 1) * HEAD_DIM)
            far = tz_ref[2, hk]
            hi, mid, lo = _split3(far)
            caug = jnp.where(lane512 == AUG_FAR, hi, jnp.where(lane512 == AUG_FAR + 1, mid,
                             jnp.where(lane512 == AUG_FAR + 2, lo, jnp.where(lane512 == AUG_PAD, 1.0, 0.0))))
            caug_scr[hk] = caug.astype(BF16)
            qw_scr[hk, :, HEAD_DIM:] = caug.astype(BF16)
            near_scr[hk] = jnp.concatenate([tz_ref[1, hk] - far, (tz_ref[0, hk] - far) + causal], axis=1)
            ksa_scr[hk, 0, 0:HEAD_DIM, :] = jnp.zeros((HEAD_DIM, LANES), BF16)
            ksa_scr[hk, 0, HEAD_DIM:, :] = pad_aug
            for t in range(seq // LANES):
                rows = slice(t * LANES, (t + 1) * LANES)
                ksa_scr[hk, SEL_PAD_TILES + t, 0:HEAD_DIM, :] = ks_ref[0, rows, hc].T
                ksa_scr[hk, SEL_PAD_TILES + t, HEAD_DIM:, :] = jnp.where(
                    (t * LANES + ci) // SEL_BLOCK == ri, -MASK_BIG, jnp.where(is_far, 1.0, 0.0)).astype(BF16)
                kwp_scr[hk, WIN_PAD // LANES + t, 0:HEAD_DIM, :] = kw_ref[0, rows, hc].T
                kwp_scr[hk, WIN_PAD // LANES + t, HEAD_DIM:, :] = win_aug
            for t in range(WIN_PAD // LANES):
                kwp_scr[hk, t, 0:HEAD_DIM, :] = jnp.zeros((HEAD_DIM, LANES), BF16)
                kwp_scr[hk, t, HEAD_DIM:, :] = pad_aug
            vsa_scr[hk, 0:SEL_PAD_TILES * LANES, :] = jnp.zeros((SEL_PAD_TILES * LANES, 2 * HEAD_DIM), BF16)
            vsa_scr[hk, SEL_PAD_TILES * LANES:, 0:HEAD_DIM] = vs_ref[0, :, hc]
            vsa_scr[hk, SEL_PAD_TILES * LANES:, HEAD_DIM:] = ones
            vwa_scr[hk, 0:WIN_PAD, :] = jnp.zeros((WIN_PAD, 2 * HEAD_DIM), BF16)
            vwa_scr[hk, WIN_PAD:, 0:HEAD_DIM] = vw_ref[0, :, hc]
            vwa_scr[hk, WIN_PAD:, HEAD_DIM:] = ones

    def reset(br):
        m_scr[br] = jnp.full(m_scr.shape[1:], NEG, F32)
        acc_scr[br] = jnp.zeros(acc_scr.shape[1:], F32)

    def online_update(br, hk, s, vaug):
        m_prev = m_scr[br, hk]
        m_new = jnp.maximum(m_prev, jnp.max(s, axis=1, keepdims=True))
        alpha = jnp.exp2(m_prev - m_new)
        p = jnp.exp2(s - _lane_tile(m_new, s.shape[1] // LANES))
        pv = jnp.dot(p.astype(BF16), vaug, preferred_element_type=F32)
        acc_scr[br, hk] = _lane_tile(alpha, 2) * acc_scr[br, hk] + pv
        m_scr[br, hk] = m_new

    def finish(br, hk):
        acc = acc_scr[br, hk]
        return acc[:, 0:HEAD_DIM] / acc[:, HEAD_DIM:]

    for hk in range(nkv):
        for g in range(NSA_GROUP):
            h = hk * NSA_GROUP + g
            q_h = q_ref[0, :, h * HEAD_DIM:(h + 1) * HEAD_DIM]
            qg_scr[hk, g * LANES:(g + 1) * LANES, 0:HEAD_DIM] = q_h
            qw_scr[hk, g * LANES:(g + 1) * LANES, 0:HEAD_DIM] = q_h

    for hk in range(nkv):
        qg = qg_scr[hk, :, 0:HEAD_DIM]

        kc = kcvc_ref[0, 0, hk]
        vc = kcvc_ref[0, 1, hk]
        s = lax.dot_general(qg, kc, NT_DIMS, preferred_element_type=F32) + bc_ref[0, hk]
        rc = lax.broadcasted_iota(jnp.int32, (LANES, ncp), 0)
        cc = lax.broadcasted_iota(jnp.int32, (LANES, ncp), 1)
        valid_c = _tile4(jnp.where(q0 + rc - (CMP_STRIDE * cc + CMP_BLOCK - 1) >= 0, 1.0, 0.0)) > 0.5
        s = jnp.where(valid_c, s, NEG)
        p = jnp.where(valid_c, jnp.exp2(s - jnp.max(s, axis=1, keepdims=True)), 0.0)
        l = jnp.sum(p, axis=1, keepdims=True)
        p = p / jnp.where(l > 0.0, l, 1.0)
        o_scr[hk] = gate_col(hk, 0) * jnp.dot(p.astype(BF16), vc, preferred_element_type=F32)

        psum = p[0:LANES]
        for g in range(1, NSA_GROUP):
            psum = psum + p[g * LANES:(g + 1) * LANES]
        p_hi = psum.astype(BF16)
        p_lo = (psum - p_hi.astype(F32)).astype(BF16)
        selmt = selmt_ref[...]
        imp = (lax.dot_general(selmt, p_hi, NT_DIMS, preferred_element_type=F32)
               + lax.dot_general(selmt, p_lo, NT_DIMS, preferred_element_type=F32))
        imp = imp[0:n_slc]
        blk = lax.broadcasted_iota(jnp.int32, (n_slc, LANES), 0)
        cur = (q0 + lax.broadcasted_iota(jnp.int32, (n_slc, LANES), 1)) // SEL_BLOCK
        forced = (blk == 0) | (blk == cur) | (blk == cur - 1)
        imp = jnp.where(forced, FORCE, imp)
        imp = jnp.where(blk <= cur, imp, -jnp.inf)
        rank = jnp.zeros((n_slc, LANES), F32)
        for j in range(n_slc):
            row = imp[j:j + 1, :]
            beats = jnp.where(row > imp, 1.0, jnp.where(row == imp, jnp.where(blk > j, 1.0, 0.0), 0.0))
            rank = rank + beats
        sel_t = jnp.where(rank < top_n, jnp.where(imp > -jnp.inf, 1.0, 0.0), 0.0)
        if n_slc < LANES:
            sel_t = jnp.concatenate([sel_t, jnp.zeros((LANES - n_slc, LANES), F32)], axis=0)
        sel_q = lax.dot_general(eye, sel_t.astype(BF16), NT_DIMS, preferred_element_type=F32)
        not_sel = jnp.where(ci < n_slc, 1.0 - sel_q, 0.0).astype(BF16)
        for g in range(NSA_GROUP):
            rows = slice(g * LANES, (g + 1) * LANES)
            qg_scr[hk, rows, HEAD_DIM:] = jnp.where(ci < n_slc, not_sel, caug_scr[hk, rows, :])

    reset(WIN)
    for off, table in ((1, near_scr), (3, None), (5, edge_scr)):
        t0 = qt - off + WIN_PAD // LANES
        p0 = pl.multiple_of(t0 * LANES, LANES)
        for hk in range(nkv):
            k_t = jnp.concatenate([kwp_scr[hk, t0], kwp_scr[hk, t0 + 1]], axis=1)
            s = jnp.dot(qw_scr[hk], k_t, preferred_element_type=F32)
            if table is not None:
                s = s + (table[hk] if table is near_scr else table[...])
            online_update(WIN, hk, s, vwa_scr[hk, pl.ds(p0, KEY_CHUNK), :])

    reset(SEL)
    n_chunks = (qt + 2) // 2
    first_tile = SEL_PAD_TILES - (qt + 1) % 2

    def sel_scores(hk, c):
        t0 = first_tile + 2 * c
        k_t = jnp.concatenate([ksa_scr[hk, t0], ksa_scr[hk, t0 + 1]], axis=1)
        return jnp.dot(qg_scr[hk], k_t, preferred_element_type=F32)

    def sel_chunk(c, last):
        p0 = pl.multiple_of((first_tile + 2 * c) * LANES, LANES)
        for hk in range(nkv):
            s = s_scr[hk]
            if last:
                s = s + near_scr[hk]
            else:
                s_scr[hk] = sel_scores(hk, c + 1)
            online_update(SEL, hk, s, vsa_scr[hk, pl.ds(p0, KEY_CHUNK), :])

    def sel_pair(i, carry):
        sel_chunk(2 * i, False)
        sel_chunk(2 * i + 1, False)
        return carry

    for hk in range(nkv):
        s_scr[hk] = sel_scores(hk, 0)
    n_far = n_chunks - 1
    lax.fori_loop(0, n_far // 2, sel_pair, 0)
    pl.when(n_far % 2 == 1)(lambda: sel_chunk(n_far - 1, False))
    sel_chunk(n_chunks - 1, True)
    for hk in range(nkv):
        o = (o_scr[hk] + gate_col(hk, 1) * finish(SEL, hk)) + gate_col(hk, 2) * finish(WIN, hk)
        for g in range(NSA_GROUP):
            h = hk * NSA_GROUP + g
            o_ref[0, :, h * HEAD_DIM:(h + 1) * HEAD_DIM] = o[g * LANES:(g + 1) * LANES]


def _nsa_call(o16v, o32v, kcvc, bias_c, tz, selmt, weights_f32, batch, seq, ncp):
    nqt = seq // LANES
    wspecs = []
    for w in weights_f32:
        blk, span = _cast_block(w, batch * nqt)
        wspecs.append(pl.BlockSpec(blk, functools.partial(lambda b, t, span: ((b * nqt + t) // span, 0), span=span)))
    kvw = NSA_KV_WIDTH
    nkv = N_NSA_KV_HEADS
    base = (3 * FOX_WIDTH + NSA_WIDTH) // kvw
    return pl.pallas_call(
        functools.partial(_nsa_kernel, seq=seq, ncp=ncp),
        grid=(batch, nqt),
        in_specs=[pl.BlockSpec((1, LANES, NSA_WIDTH), lambda b, t: (b, t, 3 * FOX_WIDTH // NSA_WIDTH)),
                  pl.BlockSpec((1, seq, kvw), lambda b, t: (b, 0, base)),
                  pl.BlockSpec((1, seq, kvw), lambda b, t: (b, 0, base + 1)),
                  pl.BlockSpec((1, seq, kvw), lambda b, t: (b, 0, base + 2)),
                  pl.BlockSpec((1, seq, kvw), lambda b, t: (b, 0, base + 3)),
                  pl.BlockSpec((1, LANES, LANES), lambda b, t: (b, t, COL_NG // LANES)),
                  pl.BlockSpec((1, 2, N_NSA_KV_HEADS, ncp, HEAD_DIM), lambda b, t: (b, 0, 0, 0, 0)),
                  pl.BlockSpec((1, N_NSA_KV_HEADS, GROUP_ROWS, ncp), lambda b, t: (t, 0, 0, 0)),
                  pl.BlockSpec((3, N_NSA_KV_HEADS, GROUP_ROWS, LANES), lambda b, t: (0, 0, 0, 0)),
                  pl.BlockSpec((LANES, ncp), lambda b, t: (0, 0))] + wspecs,
        out_specs=[pl.BlockSpec((1, LANES, NSA_WIDTH), lambda b, t: (b, t, 0))] + wspecs,
        out_shape=[jax.ShapeDtypeStruct((batch, seq, NSA_WIDTH), F32)]
        + [jax.ShapeDtypeStruct(w.shape, BF16) for w in weights_f32],
        scratch_shapes=[pltpu.VMEM((nkv, GROUP_ROWS, 2 * HEAD_DIM), BF16),
                        pltpu.VMEM((nkv, GROUP_ROWS, 2 * HEAD_DIM), BF16),
                        pltpu.VMEM((nkv, GROUP_ROWS, LANES), BF16),
                        pltpu.VMEM((nkv, GROUP_ROWS, KEY_CHUNK), F32),
                        pltpu.VMEM((GROUP_ROWS, KEY_CHUNK), F32),
                        pltpu.VMEM((2, nkv, GROUP_ROWS, LANES), F32),
                        pltpu.VMEM((2, nkv, GROUP_ROWS, 2 * HEAD_DIM), F32),
                        pltpu.VMEM((nkv, GROUP_ROWS, HEAD_DIM), F32),
                        pltpu.VMEM((nkv, GROUP_ROWS, KEY_CHUNK), F32),
                        pltpu.VMEM((nkv, SEL_PAD_TILES + seq // LANES, 2 * HEAD_DIM, LANES), BF16),
                        pltpu.VMEM((nkv, SEL_PAD_TILES * LANES + seq, 2 * HEAD_DIM), BF16),
                        pltpu.VMEM((nkv, (seq + WIN_PAD) // LANES, 2 * HEAD_DIM, LANES), BF16),
                        pltpu.VMEM((nkv, seq + WIN_PAD, 2 * HEAD_DIM), BF16)],
        compiler_params=_params(2),
        name="nsa_attention",
    )(o16v, o16v, o16v, o16v, o16v, o32v, kcvc, bias_c, tz, selmt, *weights_f32)


def _out_kernel(of_ref, on_ref, gain_ref, w_ref, x_ref, g_ref, o_ref, y_scr, *, tm, n_chains):
    rows_per_chain = tm // n_chains
    gain_f = gain_ref[:, 0:FOX_WIDTH]
    gain_n = gain_ref[:, FOX_WIDTH:MIX_WIDTH]
    for c in range(n_chains):
        rows = slice(c * rows_per_chain, (c + 1) * rows_per_chain)
        for r0 in range(c * rows_per_chain, (c + 1) * rows_per_chain, NORM_ROWS):
            r = slice(r0, r0 + NORM_ROWS)
            y_scr[r, 0:FOX_WIDTH] = (_rms(of_ref[r, :]) * gain_f).astype(BF16)
            y_scr[r, FOX_WIDTH:MIX_WIDTH] = (_rms(on_ref[r, :]) * gain_n).astype(BF16)
        acc = jnp.dot(y_scr[rows, :], w_ref[...], preferred_element_type=F32)
        o_ref[rows, :] = x_ref[rows, :] + g_ref[0] * acc


def _out_call(o_fox, o_nsa, gain, w_out, x2, g1, seq):
    t, d = x2.shape
    tm = min(512, seq)
    rows_per_batch = seq // tm
    return pl.pallas_call(
        functools.partial(_out_kernel, tm=tm, n_chains=2),
        grid=(t // tm,),
        in_specs=[pl.BlockSpec((tm, FOX_WIDTH), lambda i: (i, 0)),
                  pl.BlockSpec((tm, NSA_WIDTH), lambda i: (i, 0)),
                  pl.BlockSpec((1, MIX_WIDTH), lambda i: (0, 0)),
                  pl.BlockSpec((MIX_WIDTH, d), lambda i: (0, 0)),
                  pl.BlockSpec((tm, d), lambda i: (i, 0)),
                  pl.BlockSpec((1, 1, d), lambda i: (i // rows_per_batch, 0, 0))],
        out_specs=pl.BlockSpec((tm, d), lambda i: (i, 0)),
        out_shape=jax.ShapeDtypeStruct((t, d), F32),
        scratch_shapes=[pltpu.VMEM((tm, MIX_WIDTH), BF16)],
        compiler_params=_params(1),
        name="out_proj",
    )(o_fox, o_nsa, gain, w_out, x2, g1)


def _ffn_kernel(x_ref, gain_ref, sc_ref, sh_ref, g_ref, wg_ref, wu_ref, wd_ref, o_ref, h_scr):
    f = pl.program_id(1)

    @pl.when(f == 0)
    def _():
        _modulated_norm(x_ref, gain_ref, sc_ref, sh_ref, h_scr)
        o_ref[...] = jnp.zeros(o_ref.shape, F32)

    h = h_scr[...]
    a = jnp.dot(h, wg_ref[...], preferred_element_type=F32)
    u = jnp.dot(h, wu_ref[...], preferred_element_type=F32)
    t = (a * _sigmoid(a)) * u
    o_ref[...] += jnp.dot(t.astype(BF16), wd_ref[...], preferred_element_type=F32)

    @pl.when(f == pl.num_programs(1) - 1)
    def _():
        o_ref[...] = x_ref[...] + g_ref[0] * o_ref[...]


def _ffn_call(x1, gain2, sc, sh, g2, wg, wu, wd, seq):
    t, d = x1.shape
    dff = wg.shape[1]
    tm = min(1024, seq)
    tf = 512 if dff % 512 == 0 else dff
    rows_per_batch = seq // tm
    return pl.pallas_call(
        _ffn_kernel,
        grid=(t // tm, dff // tf),
        in_specs=[pl.BlockSpec((tm, d), lambda i, f: (i, 0)),
                  pl.BlockSpec((1, d), lambda i, f: (0, 0)),
                  pl.BlockSpec((1, 1, d), lambda i, f: (i // rows_per_batch, 0, 0)),
                  pl.BlockSpec((1, 1, d), lambda i, f: (i // rows_per_batch, 0, 0)),
                  pl.BlockSpec((1, 1, d), lambda i, f: (i // rows_per_batch, 0, 0)),
                  pl.BlockSpec((d, tf), lambda i, f: (0, f)),
                  pl.BlockSpec((d, tf), lambda i, f: (0, f)),
                  pl.BlockSpec((tf, d), lambda i, f: (f, 0))],
        out_specs=pl.BlockSpec((tm, d), lambda i, f: (i, 0)),
        out_shape=jax.ShapeDtypeStruct((t, d), F32),
        scratch_shapes=[pltpu.VMEM((tm, d), BF16)],
        compiler_params=_params(2),
        name="swiglu_ffn",
    )(x1, gain2, sc, sh, g2, wg, wu, wd)


def _selection_matrix_t(ncp, n_slc):
    r, q = SEL_BLOCK // CMP_STRIDE, CMP_BLOCK // CMP_STRIDE
    m = np.zeros((LANES, ncp), np.float32)
    for j in range(n_slc):
        for a in range(r):
            for b in range(q):
                c = r * j + a - b
                if 0 <= c < ncp:
                    m[j, c] += 1.0
    return m


def _w_in_block_sources():
    o = 0
    start = {}
    for name, width in (("fq", FOX_WIDTH), ("fk", FOX_WIDTH), ("fv", FOX_WIDTH), ("ff", N_FOX_HEADS),
                        ("nq", NSA_WIDTH), ("nk", N_BRANCH * NSA_KV_WIDTH), ("nv", N_BRANCH * NSA_KV_WIDTH),
                        ("ng", N_BRANCH * N_NSA_HEADS)):
        start[name] = o
        o += width
    kvw = NSA_KV_WIDTH
    groups = [(start["fq"], 3 * FOX_WIDTH), (start["nq"], NSA_WIDTH),
              (start["nk"] + kvw, 2 * kvw), (start["nv"] + kvw, 2 * kvw),
              (start["nk"], kvw), (start["nv"], kvw)]
    blocks = [s + LANES * b for s, width in groups for b in range(width // LANES)]
    return blocks, (start["ff"], N_FOX_HEADS), (start["ng"], N_BRANCH * N_NSA_HEADS)


def _repack_kernel(src_ref, wt_hbm, o16_ref, o32_ref, buf, sem, *, n16, n_whole, ff, ng):
    k = pl.program_id(0)
    n_slots = buf.shape[0]
    ahead = n_slots - 1
    slot = k % n_slots

    def whole_copy(kk, s):
        r0 = pl.multiple_of(src_ref[kk], 8)
        return pltpu.make_async_copy(wt_hbm.at[pl.ds(r0, LANES), :], buf.at[s], sem.at[s])

    def narrow_copies(s):
        return [pltpu.make_async_copy(wt_hbm.at[pl.ds(ff[0], ff[1]), :], buf.at[s, pl.ds(0, ff[1]), :], sem.at[s]),
                pltpu.make_async_copy(wt_hbm.at[pl.ds(ng[0], ng[1]), :], buf.at[s, pl.ds(ff[1], ng[1]), :],
                                      sem.at[s])]

    def start_block(kk):
        @pl.when(kk < n_whole)
        def _():
            whole_copy(kk, kk % n_slots).start()

        @pl.when(kk == n_whole)
        def _():
            for cp in narrow_copies(kk % n_slots):
                cp.start()

    @pl.when(k == 0)
    def _():
        for kk in range(ahead):
            start_block(kk)

    start_block(k + ahead)

    @pl.when(k < n_whole)
    def _():
        whole_copy(k, slot).wait()

    @pl.when(k == n_whole)
    def _():
        for cp in narrow_copies(slot):
            cp.wait()

    xt = buf[slot].T
    lane = lax.broadcasted_iota(jnp.int32, xt.shape, 1)
    xt = jnp.where(jnp.logical_or(k < n_whole, lane < ff[1] + ng[1]), xt, 0.0).astype(BF16)

    @pl.when(k < n16)
    def _():
        o16_ref[...] = xt

    @pl.when(k >= n16)
    def _():
        o32_ref[...] = xt


def _repack_w_in(wt):
    n, d = wt.shape
    blocks, ff, ng = _w_in_block_sources()
    n16 = W16 // LANES
    n_whole = len(blocks)
    assert n_whole + 1 == (W16 + W32) // LANES
    return pl.pallas_call(
        functools.partial(_repack_kernel, n16=n16, n_whole=n_whole, ff=ff, ng=ng),
        grid_spec=pltpu.PrefetchScalarGridSpec(
            num_scalar_prefetch=1,
            grid=(n_whole + 1,),
            in_specs=[pl.BlockSpec(memory_space=pl.ANY)],
            out_specs=[pl.BlockSpec((d, LANES), lambda k, src: (0, jnp.minimum(k, n16 - 1))),
                       pl.BlockSpec((d, LANES), lambda k, src: (0, jnp.maximum(k - n16, 0)))],
            scratch_shapes=[pltpu.VMEM((REPACK_SLOTS, LANES, d), F32), pltpu.SemaphoreType.DMA((REPACK_SLOTS,))]),
        out_shape=[jax.ShapeDtypeStruct((d, W16), BF16), jax.ShapeDtypeStruct((d, W32), BF16)],
        compiler_params=_params(1),
        name="w_in_repack",
    )(jnp.asarray(blocks, jnp.int32), wt)


def kernel(x, c, ada_w, ada_b, norm1_gain, norm2_gain, w_in, fox_f_bias, fox_q_gain, fox_k_gain, nsa_q_gain,
           nsa_k_gain, nsa_cmp_pos, nsa_cmp_w1, nsa_cmp_w2, rel_bias, mix_out_gain, w_out, ffn_w_gate, ffn_w_up,
           ffn_w_down):
    batch, seq, d = x.shape
    assert seq % KEY_CHUNK == 0 and seq >= WINDOW and d % LANES == 0 and seq // SEL_BLOCK <= LANES
    depth = ada_w.shape[0]
    nqt = seq // LANES
    ncp = -(-(seq // CMP_STRIDE) // LANES) * LANES

    selmt = jnp.asarray(_selection_matrix_t(ncp, seq // SEL_BLOCK), BF16)
    rb_flat = rel_bias.reshape(-1)
    bias_c = _bias_call(rb_flat, nqt, ncp, CMP_STRIDE, CMP_BLOCK - 1, "t5_bias_compressed", rolled=True)
    tz = _bias_call(rb_flat, 3, LANES, 1, 0, "t5_bias_toeplitz")

    ones_h = jnp.ones((HEAD_DIM,), F32)
    c_pad = jnp.pad(c, ((0, 8 - batch % 8 if batch % 8 else 0), (0, 0)))
    x2 = x.reshape(batch * seq, d)
    for layer in range(depth):
        mod = _ada_call(c_pad, ada_w[layer], ada_b[layer][None, :])[:batch]
        sh1, sc1, g1, sh2, sc2, g2 = [mod[:, i * d:(i + 1) * d][:, None, :] for i in range(N_MOD)]

        kg = nsa_k_gain[layer]
        col_gain = jnp.concatenate([
            jnp.tile(fox_q_gain[layer] * QSCALE, N_FOX_HEADS), jnp.tile(fox_k_gain[layer], N_FOX_HEADS),
            jnp.tile(ones_h, N_FOX_HEADS), jnp.tile(nsa_q_gain[layer] * QSCALE, N_NSA_HEADS),
            jnp.tile(kg[1], N_NSA_KV_HEADS), jnp.tile(kg[2], N_NSA_KV_HEADS),
            jnp.tile(ones_h, 2 * N_NSA_KV_HEADS)])[None, :]
        col_flag = jnp.concatenate([
            jnp.ones((2 * FOX_WIDTH,), F32), jnp.zeros((FOX_WIDTH,), F32), jnp.ones((NSA_WIDTH,), F32),
            jnp.ones((2 * NSA_KV_WIDTH,), F32), jnp.zeros((2 * NSA_KV_WIDTH,), F32)])[None, :]
        w16, w32 = _repack_w_in(jnp.swapaxes(w_in, 1, 2)[layer])
        o16, o32 = _proj_call(x2, sc1, sh1, norm1_gain[layer][None, :], w16, w32,
                              col_gain, col_flag, seq)
        o16v = o16.reshape(batch, seq, W16)
        o32v = o32.reshape(batch, seq, W32)

        fb_pad = jnp.pad(fox_f_bias[layer], (0, LANES - N_FOX_HEADS))[None, :]
        cs = _cum_call(o32, fb_pad, batch, seq, N_FOX_HEADS // FOX_HEADS_PER_STEP)
        o_fox = _fox_call(o16v, cs, batch, seq)

        w1 = nsa_cmp_w1[layer].reshape(2, CMP_BLOCK, HEAD_DIM, HEAD_DIM).astype(BF16)
        kcvc = _cmp_call(o32, nsa_cmp_pos[layer], w1, nsa_cmp_w2[layer].astype(BF16), kg[0][None, :],
                         batch, seq, ncp)
        o_nsa, wg16, wu16, wo16, wd16 = _nsa_call(
            o16v, o32v, kcvc, bias_c, tz, selmt,
            (ffn_w_gate[layer], ffn_w_up[layer], w_out[layer], ffn_w_down[layer]), batch, seq, ncp)

        x1 = _out_call(o_fox.reshape(batch * seq, FOX_WIDTH), o_nsa.reshape(batch * seq, NSA_WIDTH),
                       mix_out_gain[layer][None, :], wo16, x2, g1, seq)
        x2 = _ffn_call(x1, norm2_gain[layer][None, :], sc2, sh2, g2, wg16, wu16, wd16, seq)
    return x2.reshape(batch, seq, d)
```

```python
import functools
import math

import numpy as np
import jax
import jax.numpy as jnp
from jax import lax
from jax.experimental import pallas as pl
from jax.experimental.pallas import tpu as pltpu

HEAD_DIM = 128
N_FOX_HEADS = 8
N_NSA_HEADS = 8
N_NSA_KV_HEADS = 2
NSA_GROUP = N_NSA_HEADS // N_NSA_KV_HEADS
FOX_WIDTH = N_FOX_HEADS * HEAD_DIM
NSA_WIDTH = N_NSA_HEADS * HEAD_DIM
NSA_KV_WIDTH = N_NSA_KV_HEADS * HEAD_DIM
MIX_WIDTH = FOX_WIDTH + NSA_WIDTH
N_BRANCH = 3
CMP_BLOCK = 32
CMP_STRIDE = 16
SEL_BLOCK = 64
N_SEL = 8
WINDOW = 512
N_BUCKETS = 32
MAX_DISTANCE = 128
N_MOD = 6
SCALE = HEAD_DIM ** -0.5
LOG2E = math.log2(math.e)
LOG2E_HI = float(np.float32(LOG2E))
LOG2E_LO = LOG2E - LOG2E_HI
QSCALE = SCALE * LOG2E
EPS = 1e-6
NEG = -1e30
FORCE = 1e6

LANES = 128
GROUP_ROWS = NSA_GROUP * LANES
VMEM_LIMIT = 56 * 1024 * 1024
MXU_COLS = 256
KEY_CHUNK = MXU_COLS
WIN_PAD = WINDOW + LANES
MASK_BIG = 2.0 ** 100
SEL, WIN = 0, 1
REPACK_SLOTS = 4
AUG_FAR, AUG_PAD = 120, 127
FOX_HEADS_PER_STEP = 4
SEL_PAD_TILES = 1

W16 = 3 * FOX_WIDTH + NSA_WIDTH + 4 * NSA_KV_WIDTH
W32 = 5 * LANES
COL_FF = 2 * NSA_KV_WIDTH
COL_NG = COL_FF + N_FOX_HEADS

F32 = jnp.float32
BF16 = jnp.bfloat16
NT_DIMS = (((1,), (1,)), ((), ()))


def _params(n_axes):
    return pltpu.CompilerParams(dimension_semantics=("arbitrary",) * n_axes,
                                vmem_limit_bytes=VMEM_LIMIT)


def _sigmoid(x):
    return 1.0 / (1.0 + jnp.exp(-x))


def _lane_tile(a, n):
    return jnp.concatenate([a] * n, axis=1)


BF16_SUBLANES = 16


def _cast_block(w, n_steps):
    rows, cols = w.shape
    assert rows % n_steps == 0, (w.shape, n_steps)
    per_step = rows // n_steps
    span = BF16_SUBLANES // math.gcd(BF16_SUBLANES, per_step)
    assert n_steps % span == 0, (w.shape, n_steps)
    return (per_step * span, cols), span


def _rms(x):
    return x * lax.rsqrt(jnp.mean(x * x, axis=-1, keepdims=True) + EPS)


NORM_ROWS = 16


def _modulated_norm(x_ref, gain_ref, sc_ref, sh_ref, h_ref):
    gm = gain_ref[...] * (1.0 + sc_ref[0])
    sh = sh_ref[0]
    for r0 in range(0, x_ref.shape[0], NORM_ROWS):
        rows = slice(r0, r0 + NORM_ROWS)
        h_ref[rows, :] = (_rms(x_ref[rows, :]) * gm + sh).astype(BF16)


def _ada_kernel(c_ref, w_ref, b_ref, o_ref):
    c = c_ref[...]
    s = (c * _sigmoid(c)).astype(BF16)
    o_ref[...] = jnp.dot(s, w_ref[...].astype(BF16), preferred_element_type=F32) + b_ref[...]


def _ada_call(c_pad, w, b):
    rows, d = c_pad.shape
    n = w.shape[1]
    tn = next(t for t in (1024, 768, 512, 384, 256, 128) if n % t == 0)
    return pl.pallas_call(
        _ada_kernel,
        grid=(n // tn,),
        in_specs=[pl.BlockSpec((rows, d), lambda j: (0, 0)),
                  pl.BlockSpec((d, tn), lambda j: (0, j)),
                  pl.BlockSpec((1, tn), lambda j: (0, j))],
        out_specs=pl.BlockSpec((rows, tn), lambda j: (0, j)),
        out_shape=jax.ShapeDtypeStruct((rows, n), F32),
        compiler_params=_params(1),
        name="adaln",
    )(c_pad, w, b)


def _proj_kernel(x_ref, sc_ref, sh_ref, g_ref, w16_ref, w32_ref, gain_ref, flag_ref, o16_ref, o32_ref, h_scr, *, tn):
    j = pl.program_id(1)

    @pl.when(j == 0)
    def _():
        _modulated_norm(x_ref, g_ref, sc_ref, sh_ref, h_scr)

    def column_step(with_side_outputs):
        h = h_scr[...]
        for c in range(tn // MXU_COLS):
            acc = jnp.dot(h, w16_ref[:, c * MXU_COLS:(c + 1) * MXU_COLS], preferred_element_type=F32)
            for g in range(MXU_COLS // LANES):
                cols = slice(c * MXU_COLS + g * LANES, c * MXU_COLS + (g + 1) * LANES)
                a = acc[:, g * LANES:(g + 1) * LANES]
                r = lax.rsqrt(jnp.mean(a * a, axis=-1, keepdims=True) + EPS)
                scale = jnp.where(flag_ref[:, cols] > 0.5, r, 1.0)
                o16_ref[:, cols] = (a * scale * gain_ref[:, cols]).astype(BF16)
        if with_side_outputs:
            o32_ref[...] = jnp.dot(h, w32_ref[...], preferred_element_type=F32)

    last = pl.num_programs(1) - 1
    pl.when(j < last)(functools.partial(column_step, False))
    pl.when(j == last)(functools.partial(column_step, True))


def _proj_call(x2, sc, sh, gain1, w16, w32, col_gain, col_flag, seq):
    t, d = x2.shape
    tm = min(1024, seq)
    tn = 1280 if W16 % 1280 == 0 else 1024
    rows_per_batch = seq // tm
    return pl.pallas_call(
        functools.partial(_proj_kernel, tn=tn),
        grid=(t // tm, W16 // tn),
        in_specs=[pl.BlockSpec((tm, d), lambda i, j: (i, 0)),
                  pl.BlockSpec((1, 1, d), lambda i, j: (i // rows_per_batch, 0, 0)),
                  pl.BlockSpec((1, 1, d), lambda i, j: (i // rows_per_batch, 0, 0)),
                  pl.BlockSpec((1, d), lambda i, j: (0, 0)),
                  pl.BlockSpec((d, tn), lambda i, j: (0, j)),
                  pl.BlockSpec((d, W32), lambda i, j: (0, 0)),
                  pl.BlockSpec((1, tn), lambda i, j: (0, j)),
                  pl.BlockSpec((1, tn), lambda i, j: (0, j))],
        out_specs=[pl.BlockSpec((tm, tn), lambda i, j: (i, j)),
                   pl.BlockSpec((tm, W32), lambda i, j: (i, 0))],
        out_shape=[jax.ShapeDtypeStruct((t, W16), BF16),
                   jax.ShapeDtypeStruct((t, W32), F32)],
        scratch_shapes=[pltpu.VMEM((tm, d), BF16)],
        compiler_params=_params(2),
        name="in_proj",
    )(x2, sc, sh, gain1, w16, w32, col_gain, col_flag)


def _split3(c):
    hi = c.astype(BF16).astype(F32)
    r1 = c - hi
    mid = r1.astype(BF16).astype(F32)
    return hi, mid, (r1 - mid).astype(BF16).astype(F32)


def _cum_kernel(ff_ref, fb_ref, o_ref, *, seq, groups):
    ri = lax.broadcasted_iota(jnp.int32, (LANES, LANES), 0)
    ci = lax.broadcasted_iota(jnp.int32, (LANES, LANES), 1)
    tri = jnp.where(ri >= ci, 1.0, 0.0).astype(BF16)
    carry = jnp.zeros((1, LANES), F32)
    for blk in range(seq // LANES):
        rows = slice(blk * LANES, (blk + 1) * LANES)
        x = ff_ref[0, rows, :] + fb_ref[...]
        lf = jnp.minimum(x, 0.0) - jnp.log(1.0 + jnp.exp(-jnp.abs(x)))
        hi = lf.astype(BF16)
        r1 = lf - hi.astype(F32)
        mid = r1.astype(BF16)
        lo = (r1 - mid.astype(F32)).astype(BF16)
        c = (jnp.dot(tri, hi, preferred_element_type=F32)
             + jnp.dot(tri, mid, preferred_element_type=F32)
             + jnp.dot(tri, lo, preferred_element_type=F32)) + carry
        carry = c[LANES - 1:LANES, :]
        for t, term in enumerate(_split3(c * LOG2E_HI + c * LOG2E_LO)):
            for g in range(groups):
                shift = (-g * (N_FOX_HEADS // groups)) % LANES
                o_ref[0, g, t, rows, :] = term if shift == 0 else pltpu.roll(term, shift, 1)


def _cum_call(o32, fb_pad, batch, seq, groups):
    o32v = o32.reshape(batch, seq, W32)
    return pl.pallas_call(
        functools.partial(_cum_kernel, seq=seq, groups=groups),
        grid=(batch,),
        in_specs=[pl.BlockSpec((1, seq, LANES), lambda b: (b, 0, COL_FF // LANES)),
                  pl.BlockSpec((1, LANES), lambda b: (0, 0))],
        out_specs=pl.BlockSpec((1, groups, 3, seq, LANES), lambda b: (b, 0, 0, 0, 0)),
        out_shape=jax.ShapeDtypeStruct((batch, groups, 3, seq, LANES), F32),
        compiler_params=_params(1),
        name="fox_cumsum",
    )(o32v, fb_pad)


def _fox_kernel(q_ref, k_ref, v_ref, cs_ref, o_ref, m_scr, acc_scr, s_scr, kaug_scr, vaug_scr, *, seq, tq, tk, nh):
    n_diag = tq // tk
    lane_q = lax.broadcasted_iota(jnp.int32, (tq, LANES), 1)
    row8 = lax.broadcasted_iota(jnp.int32, (8, tk), 0)
    ri = lax.broadcasted_iota(jnp.int32, (tq, tk), 0)
    ci = lax.broadcasted_iota(jnp.int32, (tq, tk), 1)
    causal = [ci + d * tk <= ri for d in range(n_diag)]
    for c in range(seq // tk):
        rows = slice(c * tk, (c + 1) * tk)
        terms_t = [cs_ref[0, 0, t, rows, :].T for t in range(3)]
        for j in range(nh):
            tail8 = jnp.where(row8 < 3, 1.0, jnp.where(row8 == 3, -terms_t[0][j:j + 1], jnp.where(
                row8 == 4, -terms_t[1][j:j + 1], jnp.where(row8 == 5, -terms_t[2][j:j + 1], 0.0))))
            kaug_scr[j, c, 0:HEAD_DIM, :] = k_ref[0, rows, j * HEAD_DIM:(j + 1) * HEAD_DIM].T
            kaug_scr[j, c, HEAD_DIM:, :] = jnp.concatenate(
                [tail8, jnp.zeros((HEAD_DIM - 8, tk), F32)], axis=0).astype(BF16)
    for j in range(nh):
        vaug_scr[j, :, 0:HEAD_DIM] = v_ref[0, :, j * HEAD_DIM:(j + 1) * HEAD_DIM]
        vaug_scr[j, :, HEAD_DIM:] = jnp.ones((seq, HEAD_DIM), BF16)

    def q_body(qi, carry):
        q0 = pl.multiple_of(qi * tq, tq)
        terms = [cs_ref[0, 0, t, pl.ds(q0, tq), :] for t in range(3)]
        qs = []
        for j in range(nh):
            hi, mid, lo = [term if t == j else pltpu.roll(term, (t - j) % LANES, 1) for t, term in enumerate(terms)]
            tail = jnp.where(lane_q == 0, hi, jnp.where(lane_q == 1, mid, jnp.where(lane_q == 2, lo,
                             jnp.where(lane_q < 6, 1.0, 0.0))))
            qs.append(jnp.concatenate([q_ref[0, pl.ds(q0, tq), j * HEAD_DIM:(j + 1) * HEAD_DIM],
                                       tail.astype(BF16)], axis=1))
            m_scr[j] = jnp.full((tq, LANES), NEG, F32)
            acc_scr[j] = jnp.zeros((tq, 2 * HEAD_DIM), F32)

        all_rows = slice(0, tq)

        def scores(j, ki, rows):
            return jnp.dot(qs[j][rows], kaug_scr[j, ki], preferred_element_type=F32)

        def tile(ki, mask, rows, next_rows):
            k0 = pl.multiple_of(ki * tk, tk)
            for j in range(nh):
                s = s_scr[j, rows, :]
                if mask is not None:
                    s = jnp.where(mask[rows], s, NEG)
                if next_rows is not None:
                    s_scr[j, next_rows, :] = scores(j, ki + 1, next_rows)
                m_prev = m_scr[j, rows, :]
                m_new = jnp.maximum(m_prev, jnp.max(s, axis=1, keepdims=True))
                alpha = jnp.exp2(m_prev - m_new)
                p = jnp.exp2(s - _lane_tile(m_new, tk // LANES))
                pv = jnp.dot(p.astype(BF16), vaug_scr[j, pl.ds(k0, tk), :], preferred_element_type=F32)
                acc_scr[j, rows, :] = _lane_tile(alpha, 2) * acc_scr[j, rows, :] + pv
                m_scr[j, rows, :] = m_new

        def k_body(ki, c2):
            tile(ki, None, all_rows, all_rows)
            return c2

        for j in range(nh):
            s_scr[j] = scores(j, 0, all_rows)
        n_off = qi * n_diag
        lax.fori_loop(0, n_off, k_body, 0)
        for d in range(n_diag):
            nxt = slice((d + 1) * tk, tq) if d + 1 < n_diag else None
            tile(n_off + d, causal[d], slice(d * tk, tq), nxt)
        for j in range(nh):
            acc = acc_scr[j]
            o_ref[0, pl.ds(q0, tq), j * HEAD_DIM:(j + 1) * HEAD_DIM] = acc[:, 0:HEAD_DIM] / acc[:, HEAD_DIM:]
        return carry

    lax.fori_loop(0, seq // tq, q_body, 0)


def _fox_call(o16v, cs, batch, seq):
    nh = FOX_HEADS_PER_STEP
    tq, tk = min(512, seq), min(256, seq)
    groups = N_FOX_HEADS // nh
    w = nh * HEAD_DIM
    return pl.pallas_call(
        functools.partial(_fox_kernel, seq=seq, tq=tq, tk=tk, nh=nh),
        grid=(batch, groups),
        in_specs=[pl.BlockSpec((1, seq, w), lambda b, h: (b, 0, h)),
                  pl.BlockSpec((1, seq, w), lambda b, h: (b, 0, groups + h)),
                  pl.BlockSpec((1, seq, w), lambda b, h: (b, 0, 2 * groups + h)),
                  pl.BlockSpec((1, 1, 3, seq, LANES), lambda b, h: (b, h, 0, 0, 0))],
        out_specs=pl.BlockSpec((1, seq, w), lambda b, h: (b, 0, h)),
        out_shape=jax.ShapeDtypeStruct((batch, seq, FOX_WIDTH), F32),
        scratch_shapes=[pltpu.VMEM((nh, tq, LANES), F32),
                        pltpu.VMEM((nh, tq, 2 * HEAD_DIM), F32),
                        pltpu.VMEM((nh, tq, tk), F32),
                        pltpu.VMEM((nh, seq // tk, 2 * HEAD_DIM, tk), BF16),
                        pltpu.VMEM((nh, seq, 2 * HEAD_DIM), BF16)],
        compiler_params=_params(2),
        name="fox_attention",
    )(o16v, o16v, o16v, cs)


def _cmp_kernel(x_ref, pos_ref, w1_ref, w2_ref, gain_ref, o_ref, xs_scr, *, seq, ncp):
    rows = xs_scr.shape[1]
    for kv in range(2):
        for h in range(N_NSA_KV_HEADS):
            n = kv * N_NSA_KV_HEADS + h
            xs_scr[n, 0:seq, :] = x_ref[0, :, n * HEAD_DIM:(n + 1) * HEAD_DIM]
            xs_scr[n, seq:rows, :] = jnp.zeros((rows - seq, LANES), F32)
            acc = jnp.zeros((ncp, HEAD_DIM), F32)
            for l in range(CMP_BLOCK):
                xl = xs_scr[n, pl.ds(l, ncp, stride=CMP_STRIDE), :] + pos_ref[kv, l:l + 1, :]
                acc = acc + jnp.dot(xl.astype(BF16), w1_ref[kv, l], preferred_element_type=F32)
            hmid = acc * _sigmoid(acc)
            y = jnp.dot(hmid.astype(BF16), w2_ref[kv], preferred_element_type=F32)
            if kv == 0:
                y = _rms(y) * gain_ref[...]
            o_ref[0, kv, h] = y.astype(BF16)


def _cmp_call(o32, pos, w1, w2, gain, batch, seq, ncp):
    o32v = o32.reshape(batch, seq, W32)
    hkv = N_NSA_KV_HEADS
    width = 2 * NSA_KV_WIDTH
    return pl.pallas_call(
        functools.partial(_cmp_kernel, seq=seq, ncp=ncp),
        grid=(batch,),
        in_specs=[pl.BlockSpec((1, seq, width), lambda b: (b, 0, 0)),
                  pl.BlockSpec((2, CMP_BLOCK, HEAD_DIM), lambda b: (0, 0, 0)),
                  pl.BlockSpec((2, CMP_BLOCK, HEAD_DIM, HEAD_DIM), lambda b: (0, 0, 0, 0)),
                  pl.BlockSpec((2, HEAD_DIM, HEAD_DIM), lambda b: (0, 0, 0)),
                  pl.BlockSpec((1, HEAD_DIM), lambda b: (0, 0))],
        out_specs=pl.BlockSpec((1, 2, hkv, ncp, HEAD_DIM), lambda b: (b, 0, 0, 0, 0)),
        out_shape=jax.ShapeDtypeStruct((batch, 2, hkv, ncp, HEAD_DIM), BF16),
        scratch_shapes=[pltpu.VMEM((2 * hkv, CMP_STRIDE * ncp + CMP_BLOCK, LANES), F32)],
        compiler_params=_params(1),
        name="nsa_compress",
    )(o32v, pos, w1, w2, gain)


def _bias_kernel(rb_ref, o_ref, *, width, key_stride, key_offset, first_tile, rolled_tiles):
    v = pl.program_id(0) + first_tile
    i = lax.broadcasted_iota(jnp.int32, (LANES, width), 0)
    j = lax.broadcasted_iota(jnp.int32, (LANES, width), 1)
    d = v * LANES + i - (key_stride * j + key_offset)
    n = jnp.maximum(d, 0)
    max_exact = N_BUCKETS // 2
    nf = jnp.maximum(n, 1).astype(F32)
    large = max_exact + jnp.trunc(jnp.log(nf / max_exact) / math.log(MAX_DISTANCE / max_exact)
                                  * (N_BUCKETS - max_exact))
    large = jnp.minimum(large, float(N_BUCKETS - 1))
    bkt = jnp.where(n < max_exact, n.astype(F32), large)
    vals = [jnp.zeros((LANES, width), F32) for _ in range(N_NSA_HEADS)]
    for bk in range(N_BUCKETS):
        hit = bkt == float(bk)
        for h in range(N_NSA_HEADS):
            vals[h] = jnp.where(hit, rb_ref[bk * N_NSA_HEADS + h] * LOG2E, vals[h])
    for h in range(N_NSA_HEADS):
        g = h % NSA_GROUP
        rows = slice(g * LANES, (g + 1) * LANES)
        if rolled_tiles:
            per_tile = LANES // key_stride
            for t in range(rolled_tiles):
                shift = (width - per_tile * (rolled_tiles - 1 - t)) % width
                o_ref[t, h // NSA_GROUP, rows, :] = vals[h] if shift == 0 else pltpu.roll(vals[h], shift, 1)
        else:
            o_ref[0, h // NSA_GROUP, rows, :] = vals[h]


def _bias_call(rb_flat, n_tiles, width, key_stride, key_offset, name, rolled=False):
    kern = functools.partial(_bias_kernel, width=width, key_stride=key_stride, key_offset=key_offset,
                             first_tile=n_tiles - 1 if rolled else 0, rolled_tiles=n_tiles if rolled else 0)
    block = (n_tiles if rolled else 1, N_NSA_KV_HEADS, GROUP_ROWS, width)
    return pl.pallas_call(
        kern,
        grid=(1 if rolled else n_tiles,),
        in_specs=[pl.BlockSpec(memory_space=pltpu.SMEM)],
        out_specs=pl.BlockSpec(block, lambda v: (v, 0, 0, 0)),
        out_shape=jax.ShapeDtypeStruct((n_tiles, N_NSA_KV_HEADS, GROUP_ROWS, width), F32),
        compiler_params=_params(1),
        name=name,
    )(rb_flat)


def _tile4(a):
    return jnp.concatenate([a] * NSA_GROUP, axis=0)


def _nsa_kernel(q_ref, ks_ref, kw_ref, vs_ref, vw_ref, g_ref, kcvc_ref, bc_ref, tz_ref, selmt_ref,
                wa_ref, wb_ref, wc_ref, wd_ref, o_ref, wa_out, wb_out, wc_out, wd_out,
                qg_scr, qw_scr, caug_scr, near_scr, edge_scr, m_scr, acc_scr, o_scr, s_scr,
                ksa_scr, vsa_scr, kwp_scr, vwa_scr, *, seq, ncp):
    qt = pl.program_id(1)
    q0 = qt * LANES
    n_slc = seq // SEL_BLOCK
    top_n = min(N_SEL, n_slc)
    nkv = N_NSA_KV_HEADS
    for src, dst in ((wa_ref, wa_out), (wb_ref, wb_out), (wc_ref, wc_out), (wd_ref, wd_out)):
        dst[...] = src[...].astype(BF16)
    ri = lax.broadcasted_iota(jnp.int32, (LANES, LANES), 0)
    ci = lax.broadcasted_iota(jnp.int32, (LANES, LANES), 1)
    eye = jnp.where(ri == ci, 1.0, 0.0).astype(BF16)
    gates = _sigmoid(g_ref[0])

    def gate_col(hk, br):
        cols = []
        for g in range(NSA_GROUP):
            c = COL_NG % LANES + (hk * NSA_GROUP + g) * N_BRANCH + br
            cols.append(gates[:, c:c + 1])
        return jnp.concatenate(cols, axis=0)

    @pl.when(qt == 0)
    def _():
        ones = jnp.ones((seq, HEAD_DIM), BF16)
        row = lax.broadcasted_iota(jnp.int32, (LANES, LANES), 0)
        is_far = jnp.logical_and(row >= AUG_FAR, row < AUG_FAR + 3)
        pad_aug = jnp.where(row == AUG_PAD, -MASK_BIG, jnp.where(is_far, 1.0, 0.0)).astype(BF16)
        win_aug = jnp.where(is_far, 1.0, 0.0).astype(BF16)
        lane512 = lax.broadcasted_iota(jnp.int32, (GROUP_ROWS, LANES), 1)
        causal = _tile4(jnp.where(ci <= ri, 0.0, -MASK_BIG))
        edge_scr[...] = jnp.concatenate([jnp.full((GROUP_ROWS, LANES), -MASK_BIG, F32),
                                         _tile4(jnp.where(ri < ci, 0.0, -MASK_BIG))], axis=1)
        for hk in range(nkv):
            hc = slice(hk * HEAD_DIM, (hk + 1) * HEAD_DIM)
            far = tz_ref[2, hk]
            hi, mid, lo = _split3(far)
            caug = jnp.where(lane512 == AUG_FAR, hi, jnp.where(lane512 == AUG_FAR + 1, mid,
                             jnp.where(lane512 == AUG_FAR + 2, lo, jnp.where(lane512 == AUG_PAD, 1.0, 0.0))))
            caug_scr[hk] = caug.astype(BF16)
            qw_scr[hk, :, HEAD_DIM:] = caug.astype(BF16)
            near_scr[hk] = jnp.concatenate([tz_ref[1, hk] - far, (tz_ref[0, hk] - far) + causal], axis=1)
            ksa_scr[hk, 0, 0:HEAD_DIM, :] = jnp.zeros((HEAD_DIM, LANES), BF16)
            ksa_scr[hk, 0, HEAD_DIM:, :] = pad_aug
            for t in range(seq // LANES):
                rows = slice(t * LANES, (t + 1) * LANES)
                ksa_scr[hk, SEL_PAD_TILES + t, 0:HEAD_DIM, :] = ks_ref[0, rows, hc].T
                ksa_scr[hk, SEL_PAD_TILES + t, HEAD_DIM:, :] = jnp.where(
                    (t * LANES + ci) // SEL_BLOCK == ri, -MASK_BIG, jnp.where(is_far, 1.0, 0.0)).astype(BF16)
                kwp_scr[hk, WIN_PAD // LANES + t, 0:HEAD_DIM, :] = kw_ref[0, rows, hc].T
                kwp_scr[hk, WIN_PAD // LANES + t, HEAD_DIM:, :] = win_aug
            for t in range(WIN_PAD // LANES):
                kwp_scr[hk, t, 0:HEAD_DIM, :] = jnp.zeros((HEAD_DIM, LANES), BF16)
                kwp_scr[hk, t, HEAD_DIM:, :] = pad_aug
            vsa_scr[hk, 0:SEL_PAD_TILES * LANES, :] = jnp.zeros((SEL_PAD_TILES * LANES, 2 * HEAD_DIM), BF16)
            vsa_scr[hk, SEL_PAD_TILES * LANES:, 0:HEAD_DIM] = vs_ref[0, :, hc]
            vsa_scr[hk, SEL_PAD_TILES * LANES:, HEAD_DIM:] = ones
            vwa_scr[hk, 0:WIN_PAD, :] = jnp.zeros((WIN_PAD, 2 * HEAD_DIM), BF16)
            vwa_scr[hk, WIN_PAD:, 0:HEAD_DIM] = vw_ref[0, :, hc]
            vwa_scr[hk, WIN_PAD:, HEAD_DIM:] = ones

    def reset(br):
        m_scr[br] = jnp.full(m_scr.shape[1:], NEG, F32)
        acc_scr[br] = jnp.zeros(acc_scr.shape[1:], F32)

    def online_update(br, hk, s, vaug):
        m_prev = m_scr[br, hk]
        m_new = jnp.maximum(m_prev, jnp.max(s, axis=1, keepdims=True))
        alpha = jnp.exp2(m_prev - m_new)
        p = jnp.exp2(s - _lane_tile(m_new, s.shape[1] // LANES))
        pv = jnp.dot(p.astype(BF16), vaug, preferred_element_type=F32)
        acc_scr[br, hk] = _lane_tile(alpha, 2) * acc_scr[br, hk] + pv
        m_scr[br, hk] = m_new

    def finish(br, hk):
        acc = acc_scr[br, hk]
        return acc[:, 0:HEAD_DIM] / acc[:, HEAD_DIM:]

    for hk in range(nkv):
        for g in range(NSA_GROUP):
            h = hk * NSA_GROUP + g
            q_h = q_ref[0, :, h * HEAD_DIM:(h + 1) * HEAD_DIM]
            qg_scr[hk, g * LANES:(g + 1) * LANES, 0:HEAD_DIM] = q_h
            qw_scr[hk, g * LANES:(g + 1) * LANES, 0:HEAD_DIM] = q_h

    for hk in range(nkv):
        qg = qg_scr[hk, :, 0:HEAD_DIM]

        kc = kcvc_ref[0, 0, hk]
        vc = kcvc_ref[0, 1, hk]
        s = lax.dot_general(qg, kc, NT_DIMS, preferred_element_type=F32) + bc_ref[0, hk]
        rc = lax.broadcasted_iota(jnp.int32, (LANES, ncp), 0)
        cc = lax.broadcasted_iota(jnp.int32, (LANES, ncp), 1)
        valid_c = _tile4(jnp.where(q0 + rc - (CMP_STRIDE * cc + CMP_BLOCK - 1) >= 0, 1.0, 0.0)) > 0.5
        s = jnp.where(valid_c, s, NEG)
        p = jnp.where(valid_c, jnp.exp2(s - jnp.max(s, axis=1, keepdims=True)), 0.0)
        l = jnp.sum(p, axis=1, keepdims=True)
        p = p / jnp.where(l > 0.0, l, 1.0)
        o_scr[hk] = gate_col(hk, 0) * jnp.dot(p.astype(BF16), vc, preferred_element_type=F32)

        psum = p[0:LANES]
        for g in range(1, NSA_GROUP):
            psum = psum + p[g * LANES:(g + 1) * LANES]
        p_hi = psum.astype(BF16)
        p_lo = (psum - p_hi.astype(F32)).astype(BF16)
        selmt = selmt_ref[...]
        imp = (lax.dot_general(selmt, p_hi, NT_DIMS, preferred_element_type=F32)
               + lax.dot_general(selmt, p_lo, NT_DIMS, preferred_element_type=F32))
        imp = imp[0:n_slc]
        blk = lax.broadcasted_iota(jnp.int32, (n_slc, LANES), 0)
        cur = (q0 + lax.broadcasted_iota(jnp.int32, (n_slc, LANES), 1)) // SEL_BLOCK
        forced = (blk == 0) | (blk == cur) | (blk == cur - 1)
        imp = jnp.where(forced, FORCE, imp)
        imp = jnp.where(blk <= cur, imp, -jnp.inf)
        rank = jnp.zeros((n_slc, LANES), F32)
        for j in range(n_slc):
            row = imp[j:j + 1, :]
            beats = jnp.where(row > imp, 1.0, jnp.where(row == imp, jnp.where(blk > j, 1.0, 0.0), 0.0))
            rank = rank + beats
        sel_t = jnp.where(rank < top_n, jnp.where(imp > -jnp.inf, 1.0, 0.0), 0.0)
        if n_slc < LANES:
            sel_t = jnp.concatenate([sel_t, jnp.zeros((LANES - n_slc, LANES), F32)], axis=0)
        sel_q = lax.dot_general(eye, sel_t.astype(BF16), NT_DIMS, preferred_element_type=F32)
        not_sel = jnp.where(ci < n_slc, 1.0 - sel_q, 0.0).astype(BF16)
        for g in range(NSA_GROUP):
            rows = slice(g * LANES, (g + 1) * LANES)
            qg_scr[hk, rows, HEAD_DIM:] = jnp.where(ci < n_slc, not_sel, caug_scr[hk, rows, :])

    reset(WIN)
    for off, table in ((1, near_scr), (3, None), (5, edge_scr)):
        t0 = qt - off + WIN_PAD // LANES
        p0 = pl.multiple_of(t0 * LANES, LANES)
        for hk in range(nkv):
            k_t = jnp.concatenate([kwp_scr[hk, t0], kwp_scr[hk, t0 + 1]], axis=1)
            s = jnp.dot(qw_scr[hk], k_t, preferred_element_type=F32)
            if table is not None:
                s = s + (table[hk] if table is near_scr else table[...])
            online_update(WIN, hk, s, vwa_scr[hk, pl.ds(p0, KEY_CHUNK), :])

    reset(SEL)
    n_chunks = (qt + 2) // 2
    first_tile = SEL_PAD_TILES - (qt + 1) % 2

    def sel_scores(hk, c):
        t0 = first_tile + 2 * c
        k_t = jnp.concatenate([ksa_scr[hk, t0], ksa_scr[hk, t0 + 1]], axis=1)
        return jnp.dot(qg_scr[hk], k_t, preferred_element_type=F32)

    def sel_chunk(c, last):
        p0 = pl.multiple_of((first_tile + 2 * c) * LANES, LANES)
        for hk in range(nkv):
            s = s_scr[hk]
            if last:
                s = s + near_scr[hk]
            else:
                s_scr[hk] = sel_scores(hk, c + 1)
            online_update(SEL, hk, s, vsa_scr[hk, pl.ds(p0, KEY_CHUNK), :])

    def sel_pair(i, carry):
        sel_chunk(2 * i, False)
        sel_chunk(2 * i + 1, False)
        return carry

    for hk in range(nkv):
        s_scr[hk] = sel_scores(hk, 0)
    n_far = n_chunks - 1
    lax.fori_loop(0, n_far // 2, sel_pair, 0)
    pl.when(n_far % 2 == 1)(lambda: sel_chunk(n_far - 1, False))
    sel_chunk(n_chunks - 1, True)
    for hk in range(nkv):
        o = (o_scr[hk] + gate_col(hk, 1) * finish(SEL, hk)) + gate_col(hk, 2) * finish(WIN, hk)
        for g in range(NSA_GROUP):
            h = hk * NSA_GROUP + g
            o_ref[0, :, h * HEAD_DIM:(h + 1) * HEAD_DIM] = o[g * LANES:(g + 1) * LANES]


def _nsa_call(o16v, o32v, kcvc, bias_c, tz, selmt, weights_f32, batch, seq, ncp):
    nqt = seq // LANES
    wspecs = []
    for w in weights_f32:
        blk, span = _cast_block(w, batch * nqt)
        wspecs.append(pl.BlockSpec(blk, functools.partial(lambda b, t, span: ((b * nqt + t) // span, 0), span=span)))
    kvw = NSA_KV_WIDTH
    nkv = N_NSA_KV_HEADS
    base = (3 * FOX_WIDTH + NSA_WIDTH) // kvw
    return pl.pallas_call(
        functools.partial(_nsa_kernel, seq=seq, ncp=ncp),
        grid=(batch, nqt),
        in_specs=[pl.BlockSpec((1, LANES, NSA_WIDTH), lambda b, t: (b, t, 3 * FOX_WIDTH // NSA_WIDTH)),
                  pl.BlockSpec((1, seq, kvw), lambda b, t: (b, 0, base)),
                  pl.BlockSpec((1, seq, kvw), lambda b, t: (b, 0, base + 1)),
                  pl.BlockSpec((1, seq, kvw), lambda b, t: (b, 0, base + 2)),
                  pl.BlockSpec((1, seq, kvw), lambda b, t: (b, 0, base + 3)),
                  pl.BlockSpec((1, LANES, LANES), lambda b, t: (b, t, COL_NG // LANES)),
                  pl.BlockSpec((1, 2, N_NSA_KV_HEADS, ncp, HEAD_DIM), lambda b, t: (b, 0, 0, 0, 0)),
                  pl.BlockSpec((1, N_NSA_KV_HEADS, GROUP_ROWS, ncp), lambda b, t: (t, 0, 0, 0)),
                  pl.BlockSpec((3, N_NSA_KV_HEADS, GROUP_ROWS, LANES), lambda b, t: (0, 0, 0, 0)),
                  pl.BlockSpec((LANES, ncp), lambda b, t: (0, 0))] + wspecs,
        out_specs=[pl.BlockSpec((1, LANES, NSA_WIDTH), lambda b, t: (b, t, 0))] + wspecs,
        out_shape=[jax.ShapeDtypeStruct((batch, seq, NSA_WIDTH), F32)]
        + [jax.ShapeDtypeStruct(w.shape, BF16) for w in weights_f32],
        scratch_shapes=[pltpu.VMEM((nkv, GROUP_ROWS, 2 * HEAD_DIM), BF16),
                        pltpu.VMEM((nkv, GROUP_ROWS, 2 * HEAD_DIM), BF16),
                        pltpu.VMEM((nkv, GROUP_ROWS, LANES), BF16),
                        pltpu.VMEM((nkv, GROUP_ROWS, KEY_CHUNK), F32),
                        pltpu.VMEM((GROUP_ROWS, KEY_CHUNK), F32),
                        pltpu.VMEM((2, nkv, GROUP_ROWS, LANES), F32),
                        pltpu.VMEM((2, nkv, GROUP_ROWS, 2 * HEAD_DIM), F32),
                        pltpu.VMEM((nkv, GROUP_ROWS, HEAD_DIM), F32),
                        pltpu.VMEM((nkv, GROUP_ROWS, KEY_CHUNK), F32),
                        pltpu.VMEM((nkv, SEL_PAD_TILES + seq // LANES, 2 * HEAD_DIM, LANES), BF16),
                        pltpu.VMEM((nkv, SEL_PAD_TILES * LANES + seq, 2 * HEAD_DIM), BF16),
                        pltpu.VMEM((nkv, (seq + WIN_PAD) // LANES, 2 * HEAD_DIM, LANES), BF16),
                        pltpu.VMEM((nkv, seq + WIN_PAD, 2 * HEAD_DIM), BF16)],
        compiler_params=_params(2),
        name="nsa_attention",
    )(o16v, o16v, o16v, o16v, o16v, o32v, kcvc, bias_c, tz, selmt, *weights_f32)


def _out_kernel(of_ref, on_ref, gain_ref, w_ref, x_ref, g_ref, o_ref, y_scr, *, tm, n_chains):
    rows_per_chain = tm // n_chains
    gain_f = gain_ref[:, 0:FOX_WIDTH]
    gain_n = gain_ref[:, FOX_WIDTH:MIX_WIDTH]
    for c in range(n_chains):
        rows = slice(c * rows_per_chain, (c + 1) * rows_per_chain)
        for r0 in range(c * rows_per_chain, (c + 1) * rows_per_chain, NORM_ROWS):
            r = slice(r0, r0 + NORM_ROWS)
            y_scr[r, 0:FOX_WIDTH] = (_rms(of_ref[r, :]) * gain_f).astype(BF16)
            y_scr[r, FOX_WIDTH:MIX_WIDTH] = (_rms(on_ref[r, :]) * gain_n).astype(BF16)
        acc = jnp.dot(y_scr[rows, :], w_ref[...], preferred_element_type=F32)
        o_ref[rows, :] = x_ref[rows, :] + g_ref[0] * acc


def _out_call(o_fox, o_nsa, gain, w_out, x2, g1, seq):
    t, d = x2.shape
    tm = min(512, seq)
    rows_per_batch = seq // tm
    return pl.pallas_call(
        functools.partial(_out_kernel, tm=tm, n_chains=2),
        grid=(t // tm,),
        in_specs=[pl.BlockSpec((tm, FOX_WIDTH), lambda i: (i, 0)),
                  pl.BlockSpec((tm, NSA_WIDTH), lambda i: (i, 0)),
                  pl.BlockSpec((1, MIX_WIDTH), lambda i: (0, 0)),
                  pl.BlockSpec((MIX_WIDTH, d), lambda i: (0, 0)),
                  pl.BlockSpec((tm, d), lambda i: (i, 0)),
                  pl.BlockSpec((1, 1, d), lambda i: (i // rows_per_batch, 0, 0))],
        out_specs=pl.BlockSpec((tm, d), lambda i: (i, 0)),
        out_shape=jax.ShapeDtypeStruct((t, d), F32),
        scratch_shapes=[pltpu.VMEM((tm, MIX_WIDTH), BF16)],
        compiler_params=_params(1),
        name="out_proj",
    )(o_fox, o_nsa, gain, w_out, x2, g1)


def _ffn_kernel(x_ref, gain_ref, sc_ref, sh_ref, g_ref, wg_ref, wu_ref, wd_ref, o_ref, h_scr):
    f = pl.program_id(1)

    @pl.when(f == 0)
    def _():
        _modulated_norm(x_ref, gain_ref, sc_ref, sh_ref, h_scr)
        o_ref[...] = jnp.zeros(o_ref.shape, F32)

    h = h_scr[...]
    a = jnp.dot(h, wg_ref[...], preferred_element_type=F32)
    u = jnp.dot(h, wu_ref[...], preferred_element_type=F32)
    t = (a * _sigmoid(a)) * u
    o_ref[...] += jnp.dot(t.astype(BF16), wd_ref[...], preferred_element_type=F32)

    @pl.when(f == pl.num_programs(1) - 1)
    def _():
        o_ref[...] = x_ref[...] + g_ref[0] * o_ref[...]


def _ffn_call(x1, gain2, sc, sh, g2, wg, wu, wd, seq):
    t, d = x1.shape
    dff = wg.shape[1]
    tm = min(1024, seq)
    tf = 512 if dff % 512 == 0 else dff
    rows_per_batch = seq // tm
    return pl.pallas_call(
        _ffn_kernel,
        grid=(t // tm, dff // tf),
        in_specs=[pl.BlockSpec((tm, d), lambda i, f: (i, 0)),
                  pl.BlockSpec((1, d), lambda i, f: (0, 0)),
                  pl.BlockSpec((1, 1, d), lambda i, f: (i // rows_per_batch, 0, 0)),
                  pl.BlockSpec((1, 1, d), lambda i, f: (i // rows_per_batch, 0, 0)),
                  pl.BlockSpec((1, 1, d), lambda i, f: (i // rows_per_batch, 0, 0)),
                  pl.BlockSpec((d, tf), lambda i, f: (0, f)),
                  pl.BlockSpec((d, tf), lambda i, f: (0, f)),
                  pl.BlockSpec((tf, d), lambda i, f: (f, 0))],
        out_specs=pl.BlockSpec((tm, d), lambda i, f: (i, 0)),
        out_shape=jax.ShapeDtypeStruct((t, d), F32),
        scratch_shapes=[pltpu.VMEM((tm, d), BF16)],
        compiler_params=_params(2),
        name="swiglu_ffn",
    )(x1, gain2, sc, sh, g2, wg, wu, wd)


def _selection_matrix_t(ncp, n_slc):
    r, q = SEL_BLOCK // CMP_STRIDE, CMP_BLOCK // CMP_STRIDE
    m = np.zeros((LANES, ncp), np.float32)
    for j in range(n_slc):
        for a in range(r):
            for b in range(q):
                c = r * j + a - b
                if 0 <= c < ncp:
                    m[j, c] += 1.0
    return m


def _w_in_block_sources():
    o = 0
    start = {}
    for name, width in (("fq", FOX_WIDTH), ("fk", FOX_WIDTH), ("fv", FOX_WIDTH), ("ff", N_FOX_HEADS),
                        ("nq", NSA_WIDTH), ("nk", N_BRANCH * NSA_KV_WIDTH), ("nv", N_BRANCH * NSA_KV_WIDTH),
                        ("ng", N_BRANCH * N_NSA_HEADS)):
        start[name] = o
        o += width
    kvw = NSA_KV_WIDTH
    groups = [(start["fq"], 3 * FOX_WIDTH), (start["nq"], NSA_WIDTH),
              (start["nk"] + kvw, 2 * kvw), (start["nv"] + kvw, 2 * kvw),
              (start["nk"], kvw), (start["nv"], kvw)]
    blocks = [s + LANES * b for s, width in groups for b in range(width // LANES)]
    return blocks, (start["ff"], N_FOX_HEADS), (start["ng"], N_BRANCH * N_NSA_HEADS)


def _repack_kernel(src_ref, wt_hbm, o16_ref, o32_ref, buf, sem, *, n16, n_whole, ff, ng):
    k = pl.program_id(0)
    n_slots = buf.shape[0]
    ahead = n_slots - 1
    slot = k % n_slots

    def whole_copy(kk, s):
        r0 = pl.multiple_of(src_ref[kk], 8)
        return pltpu.make_async_copy(wt_hbm.at[pl.ds(r0, LANES), :], buf.at[s], sem.at[s])

    def narrow_copies(s):
        return [pltpu.make_async_copy(wt_hbm.at[pl.ds(ff[0], ff[1]), :], buf.at[s, pl.ds(0, ff[1]), :], sem.at[s]),
                pltpu.make_async_copy(wt_hbm.at[pl.ds(ng[0], ng[1]), :], buf.at[s, pl.ds(ff[1], ng[1]), :],
                                      sem.at[s])]

    def start_block(kk):
        @pl.when(kk < n_whole)
        def _():
            whole_copy(kk, kk % n_slots).start()

        @pl.when(kk == n_whole)
        def _():
            for cp in narrow_copies(kk % n_slots):
                cp.start()

    @pl.when(k == 0)
    def _():
        for kk in range(ahead):
            start_block(kk)

    start_block(k + ahead)

    @pl.when(k < n_whole)
    def _():
        whole_copy(k, slot).wait()

    @pl.when(k == n_whole)
    def _():
        for cp in narrow_copies(slot):
            cp.wait()

    xt = buf[slot].T
    lane = lax.broadcasted_iota(jnp.int32, xt.shape, 1)
    xt = jnp.where(jnp.logical_or(k < n_whole, lane < ff[1] + ng[1]), xt, 0.0).astype(BF16)

    @pl.when(k < n16)
    def _():
        o16_ref[...] = xt

    @pl.when(k >= n16)
    def _():
        o32_ref[...] = xt


def _repack_w_in(wt):
    n, d = wt.shape
    blocks, ff, ng = _w_in_block_sources()
    n16 = W16 // LANES
    n_whole = len(blocks)
    assert n_whole + 1 == (W16 + W32) // LANES
    return pl.pallas_call(
        functools.partial(_repack_kernel, n16=n16, n_whole=n_whole, ff=ff, ng=ng),
        grid_spec=pltpu.PrefetchScalarGridSpec(
            num_scalar_prefetch=1,
            grid=(n_whole + 1,),
            in_specs=[pl.BlockSpec(memory_space=pl.ANY)],
            out_specs=[pl.BlockSpec((d, LANES), lambda k, src: (0, jnp.minimum(k, n16 - 1))),
                       pl.BlockSpec((d, LANES), lambda k, src: (0, jnp.maximum(k - n16, 0)))],
            scratch_shapes=[pltpu.VMEM((REPACK_SLOTS, LANES, d), F32), pltpu.SemaphoreType.DMA((REPACK_SLOTS,))]),
        out_shape=[jax.ShapeDtypeStruct((d, W16), BF16), jax.ShapeDtypeStruct((d, W32), BF16)],
        compiler_params=_params(1),
        name="w_in_repack",
    )(jnp.asarray(blocks, jnp.int32), wt)


def kernel(x, c, ada_w, ada_b, norm1_gain, norm2_gain, w_in, fox_f_bias, fox_q_gain, fox_k_gain, nsa_q_gain,
           nsa_k_gain, nsa_cmp_pos, nsa_cmp_w1, nsa_cmp_w2, rel_bias, mix_out_gain, w_out, ffn_w_gate, ffn_w_up,
           ffn_w_down):
    batch, seq, d = x.shape
    assert seq % KEY_CHUNK == 0 and seq >= WINDOW and d % LANES == 0 and seq // SEL_BLOCK <= LANES
    depth = ada_w.shape[0]
    nqt = seq // LANES
    ncp = -(-(seq // CMP_STRIDE) // LANES) * LANES

    selmt = jnp.asarray(_selection_matrix_t(ncp, seq // SEL_BLOCK), BF16)
    rb_flat = rel_bias.reshape(-1)
    bias_c = _bias_call(rb_flat, nqt, ncp, CMP_STRIDE, CMP_BLOCK - 1, "t5_bias_compressed", rolled=True)
    tz = _bias_call(rb_flat, 3, LANES, 1, 0, "t5_bias_toeplitz")

    ones_h = jnp.ones((HEAD_DIM,), F32)
    c_pad = jnp.pad(c, ((0, 8 - batch % 8 if batch % 8 else 0), (0, 0)))
    x2 = x.reshape(batch * seq, d)
    for layer in range(depth):
        mod = _ada_call(c_pad, ada_w[layer], ada_b[layer][None, :])[:batch]
        sh1, sc1, g1, sh2, sc2, g2 = [mod[:, i * d:(i + 1) * d][:, None, :] for i in range(N_MOD)]

        kg = nsa_k_gain[layer]
        col_gain = jnp.concatenate([
            jnp.tile(fox_q_gain[layer] * QSCALE, N_FOX_HEADS), jnp.tile(fox_k_gain[layer], N_FOX_HEADS),
            jnp.tile(ones_h, N_FOX_HEADS), jnp.tile(nsa_q_gain[layer] * QSCALE, N_NSA_HEADS),
            jnp.tile(kg[1], N_NSA_KV_HEADS), jnp.tile(kg[2], N_NSA_KV_HEADS),
            jnp.tile(ones_h, 2 * N_NSA_KV_HEADS)])[None, :]
        col_flag = jnp.concatenate([
            jnp.ones((2 * FOX_WIDTH,), F32), jnp.zeros((FOX_WIDTH,), F32), jnp.ones((NSA_WIDTH,), F32),
            jnp.ones((2 * NSA_KV_WIDTH,), F32), jnp.zeros((2 * NSA_KV_WIDTH,), F32)])[None, :]
        w16, w32 = _repack_w_in(jnp.swapaxes(w_in, 1, 2)[layer])
        o16, o32 = _proj_call(x2, sc1, sh1, norm1_gain[layer][None, :], w16, w32,
                              col_gain, col_flag, seq)
        o16v = o16.reshape(batch, seq, W16)
        o32v = o32.reshape(batch, seq, W32)

        fb_pad = jnp.pad(fox_f_bias[layer], (0, LANES - N_FOX_HEADS))[None, :]
        cs = _cum_call(o32, fb_pad, batch, seq, N_FOX_HEADS // FOX_HEADS_PER_STEP)
        o_fox = _fox_call(o16v, cs, batch, seq)

        w1 = nsa_cmp_w1[layer].reshape(2, CMP_BLOCK, HEAD_DIM, HEAD_DIM).astype(BF16)
        kcvc = _cmp_call(o32, nsa_cmp_pos[layer], w1, nsa_cmp_w2[layer].astype(BF16), kg[0][None, :],
                         batch, seq, ncp)
        o_nsa, wg16, wu16, wo16, wd16 = _nsa_call(
            o16v, o32v, kcvc, bias_c, tz, selmt,
            (ffn_w_gate[layer], ffn_w_up[layer], w_out[layer], ffn_w_down[layer]), batch, seq, ncp)

        x1 = _out_call(o_fox.reshape(batch * seq, FOX_WIDTH), o_nsa.reshape(batch * seq, NSA_WIDTH),
                       mix_out_gain[layer][None, :], wo16, x2, g1, seq)
        x2 = _ffn_call(x1, norm2_gain[layer][None, :], sc2, sh2, g2, wg16, wu16, wd16, seq)
    return x2.reshape(batch, seq, d)
```

```python
import functools
import math

import numpy as np
import jax
import jax.numpy as jnp
from jax import lax
from jax.experimental import pallas as pl
from jax.experimental.pallas import tpu as pltpu

HEAD_DIM = 128
N_FOX_HEADS = 8
N_NSA_HEADS = 8
N_NSA_KV_HEADS = 2
NSA_GROUP = N_NSA_HEADS // N_NSA_KV_HEADS
FOX_WIDTH = N_FOX_HEADS * HEAD_DIM
NSA_WIDTH = N_NSA_HEADS * HEAD_DIM
NSA_KV_WIDTH = N_NSA_KV_HEADS * HEAD_DIM
MIX_WIDTH = FOX_WIDTH + NSA_WIDTH
N_BRANCH = 3
CMP_BLOCK = 32
CMP_STRIDE = 16
SEL_BLOCK = 64
N_SEL = 8
WINDOW = 512
N_BUCKETS = 32
MAX_DISTANCE = 128
N_MOD = 6
SCALE = HEAD_DIM ** -0.5
LOG2E = math.log2(math.e)
LOG2E_HI = float(np.float32(LOG2E))
LOG2E_LO = LOG2E - LOG2E_HI
QSCALE = SCALE * LOG2E
EPS = 1e-6
NEG = -1e30
FORCE = 1e6

LANES = 128
GROUP_ROWS = NSA_GROUP * LANES
VMEM_LIMIT = 56 * 1024 * 1024
MXU_COLS = 256
KEY_CHUNK = MXU_COLS
WIN_PAD = WINDOW + LANES
MASK_BIG = 2.0 ** 100
SEL, WIN = 0, 1
REPACK_SLOTS = 4
AUG_FAR, AUG_PAD = 120, 127
FOX_HEADS_PER_STEP = 4
SEL_PAD_TILES = 1

W16 = 3 * FOX_WIDTH + NSA_WIDTH + 4 * NSA_KV_WIDTH
W32 = 5 * LANES
COL_FF = 2 * NSA_KV_WIDTH
COL_NG = COL_FF + N_FOX_HEADS

F32 = jnp.float32
BF16 = jnp.bfloat16
NT_DIMS = (((1,), (1,)), ((), ()))


def _params(n_axes):
    return pltpu.CompilerParams(dimension_semantics=("arbitrary",) * n_axes,
                                vmem_limit_bytes=VMEM_LIMIT)


def _sigmoid(x):
    return 1.0 / (1.0 + jnp.exp(-x))


def _lane_tile(a, n):
    return jnp.concatenate([a] * n, axis=1)


BF16_SUBLANES = 16


def _cast_block(w, n_steps):
    rows, cols = w.shape
    assert rows % n_steps == 0, (w.shape, n_steps)
    per_step = rows // n_steps
    span = BF16_SUBLANES // math.gcd(BF16_SUBLANES, per_step)
    assert n_steps % span == 0, (w.shape, n_steps)
    return (per_step * span, cols), span


def _rms(x):
    return x * lax.rsqrt(jnp.mean(x * x, axis=-1, keepdims=True) + EPS)


NORM_ROWS = 16


def _modulated_norm(x_ref, gain_ref, sc_ref, sh_ref, h_ref):
    gm = gain_ref[...] * (1.0 + sc_ref[0])
    sh = sh_ref[0]
    for r0 in range(0, x_ref.shape[0], NORM_ROWS):
        rows = slice(r0, r0 + NORM_ROWS)
        h_ref[rows, :] = (_rms(x_ref[rows, :]) * gm + sh).astype(BF16)


def _ada_kernel(c_ref, w_ref, b_ref, o_ref):
    c = c_ref[...]
    s = (c * _sigmoid(c)).astype(BF16)
    o_ref[...] = jnp.dot(s, w_ref[...].astype(BF16), preferred_element_type=F32) + b_ref[...]


def _ada_call(c_pad, w, b):
    rows, d = c_pad.shape
    n = w.shape[1]
    tn = next(t for t in (2048, 1024, 768, 512, 384, 256, 128) if n % t == 0)
    return pl.pallas_call(
        _ada_kernel,
        grid=(n // tn,),
        in_specs=[pl.BlockSpec((rows, d), lambda j: (0, 0)),
                  pl.BlockSpec((d, tn), lambda j: (0, j)),
                  pl.BlockSpec((1, tn), lambda j: (0, j))],
        out_specs=pl.BlockSpec((rows, tn), lambda j: (0, j)),
        out_shape=jax.ShapeDtypeStruct((rows, n), F32),
        compiler_params=_params(1),
        name="adaln",
    )(c_pad, w, b)


def _proj_kernel(x_ref, sc_ref, sh_ref, g_ref, w16_ref, w32_ref, gain_ref, flag_ref, o16_ref, o32_ref, h_scr, *, tn):
    j = pl.program_id(1)

    @pl.when(j == 0)
    def _():
        _modulated_norm(x_ref, g_ref, sc_ref, sh_ref, h_scr)

    def column_step(with_side_outputs):
        h = h_scr[...]
        for c in range(tn // MXU_COLS):
            acc = jnp.dot(h, w16_ref[:, c * MXU_COLS:(c + 1) * MXU_COLS], preferred_element_type=F32)
            for g in range(MXU_COLS // LANES):
                cols = slice(c * MXU_COLS + g * LANES, c * MXU_COLS + (g + 1) * LANES)
                a = acc[:, g * LANES:(g + 1) * LANES]
                r = lax.rsqrt(jnp.mean(a * a, axis=-1, keepdims=True) + EPS)
                scale = jnp.where(flag_ref[:, cols] > 0.5, r, 1.0)
                o16_ref[:, cols] = (a * scale * gain_ref[:, cols]).astype(BF16)
        if with_side_outputs:
            o32_ref[...] = jnp.dot(h, w32_ref[...], preferred_element_type=F32)

    last = pl.num_programs(1) - 1
    pl.when(j < last)(functools.partial(column_step, False))
    pl.when(j == last)(functools.partial(column_step, True))


def _proj_call(x2, sc, sh, gain1, w16, w32, col_gain, col_flag, seq):
    t, d = x2.shape
    tm = min(1024, seq)
    tn = 1280 if W16 % 1280 == 0 else 1024
    rows_per_batch = seq // tm
    return pl.pallas_call(
        functools.partial(_proj_kernel, tn=tn),
        grid=(t // tm, W16 // tn),
        in_specs=[pl.BlockSpec((tm, d), lambda i, j: (i, 0)),
                  pl.BlockSpec((1, 1, d), lambda i, j: (i // rows_per_batch, 0, 0)),
                  pl.BlockSpec((1, 1, d), lambda i, j: (i // rows_per_batch, 0, 0)),
                  pl.BlockSpec((1, d), lambda i, j: (0, 0)),
                  pl.BlockSpec((d, tn), lambda i, j: (0, j)),
                  pl.BlockSpec((d, W32), lambda i, j: (0, 0)),
                  pl.BlockSpec((1, tn), lambda i, j: (0, j)),
                  pl.BlockSpec((1, tn), lambda i, j: (0, j))],
        out_specs=[pl.BlockSpec((tm, tn), lambda i, j: (i, j)),
                   pl.BlockSpec((tm, W32), lambda i, j: (i, 0))],
        out_shape=[jax.ShapeDtypeStruct((t, W16), BF16),
                   jax.ShapeDtypeStruct((t, W32), F32)],
        scratch_shapes=[pltpu.VMEM((tm, d), BF16)],
        compiler_params=_params(2),
        name="in_proj",
    )(x2, sc, sh, gain1, w16, w32, col_gain, col_flag)


def _split3(c):
    hi = c.astype(BF16).astype(F32)
    r1 = c - hi
    mid = r1.astype(BF16).astype(F32)
    return hi, mid, (r1 - mid).astype(BF16).astype(F32)


def _cum_kernel(ff_ref, fb_ref, o_ref, *, seq, groups):
    ri = lax.broadcasted_iota(jnp.int32, (LANES, LANES), 0)
    ci = lax.broadcasted_iota(jnp.int32, (LANES, LANES), 1)
    tri = jnp.where(ri >= ci, 1.0, 0.0).astype(BF16)
    carry = jnp.zeros((1, LANES), F32)
    for blk in range(seq // LANES):
        rows = slice(blk * LANES, (blk + 1) * LANES)
        x = ff_ref[0, rows, :] + fb_ref[...]
        lf = jnp.minimum(x, 0.0) - jnp.log(1.0 + jnp.exp(-jnp.abs(x)))
        hi = lf.astype(BF16)
        r1 = lf - hi.astype(F32)
        mid = r1.astype(BF16)
        lo = (r1 - mid.astype(F32)).astype(BF16)
        c = (jnp.dot(tri, hi, preferred_element_type=F32)
             + jnp.dot(tri, mid, preferred_element_type=F32)
             + jnp.dot(tri, lo, preferred_element_type=F32)) + carry
        carry = c[LANES - 1:LANES, :]
        for t, term in enumerate(_split3(c * LOG2E_HI + c * LOG2E_LO)):
            for g in range(groups):
                shift = (-g * (N_FOX_HEADS // groups)) % LANES
                o_ref[0, g, t, rows, :] = term if shift == 0 else pltpu.roll(term, shift, 1)


def _cum_call(o32, fb_pad, batch, seq, groups):
    o32v = o32.reshape(batch, seq, W32)
    return pl.pallas_call(
        functools.partial(_cum_kernel, seq=seq, groups=groups),
        grid=(batch,),
        in_specs=[pl.BlockSpec((1, seq, LANES), lambda b: (b, 0, COL_FF // LANES)),
                  pl.BlockSpec((1, LANES), lambda b: (0, 0))],
        out_specs=pl.BlockSpec((1, groups, 3, seq, LANES), lambda b: (b, 0, 0, 0, 0)),
        out_shape=jax.ShapeDtypeStruct((batch, groups, 3, seq, LANES), F32),
        compiler_params=_params(1),
        name="fox_cumsum",
    )(o32v, fb_pad)


def _fox_kernel(q_ref, k_ref, v_ref, cs_ref, o_ref, m_scr, acc_scr, s_scr, kaug_scr, vaug_scr, *, seq, tq, tk, nh):
    n_diag = tq // tk
    lane_q = lax.broadcasted_iota(jnp.int32, (tq, LANES), 1)
    row8 = lax.broadcasted_iota(jnp.int32, (8, tk), 0)
    ri = lax.broadcasted_iota(jnp.int32, (tq, tk), 0)
    ci = lax.broadcasted_iota(jnp.int32, (tq, tk), 1)
    causal = [ci + d * tk <= ri for d in range(n_diag)]
    for c in range(seq // tk):
        rows = slice(c * tk, (c + 1) * tk)
        terms_t = [cs_ref[0, 0, t, rows, :].T for t in range(3)]
        for j in range(nh):
            tail8 = jnp.where(row8 < 3, 1.0, jnp.where(row8 == 3, -terms_t[0][j:j + 1], jnp.where(
                row8 == 4, -terms_t[1][j:j + 1], jnp.where(row8 == 5, -terms_t[2][j:j + 1], 0.0))))
            kaug_scr[j, c, 0:HEAD_DIM, :] = k_ref[0, rows, j * HEAD_DIM:(j + 1) * HEAD_DIM].T
            kaug_scr[j, c, HEAD_DIM:, :] = jnp.concatenate(
                [tail8, jnp.zeros((HEAD_DIM - 8, tk), F32)], axis=0).astype(BF16)
    for j in range(nh):
        vaug_scr[j, :, 0:HEAD_DIM] = v_ref[0, :, j * HEAD_DIM:(j + 1) * HEAD_DIM]
        vaug_scr[j, :, HEAD_DIM:] = jnp.ones((seq, HEAD_DIM), BF16)
    acc_scr[...] = jnp.zeros(acc_scr.shape, F32)

    def q_body(qi, carry):
        q0 = pl.multiple_of(qi * tq, tq)
        terms = [cs_ref[0, 0, t, pl.ds(q0, tq), :] for t in range(3)]
        qs = []
        for j in range(nh):
            hi, mid, lo = [term if t == j else pltpu.roll(term, (t - j) % LANES, 1) for t, term in enumerate(terms)]
            tail = jnp.where(lane_q == 0, hi, jnp.where(lane_q == 1, mid, jnp.where(lane_q == 2, lo,
                             jnp.where(lane_q < 6, 1.0, 0.0))))
            qs.append(jnp.concatenate([q_ref[0, pl.ds(q0, tq), j * HEAD_DIM:(j + 1) * HEAD_DIM],
                                       tail.astype(BF16)], axis=1))
            m_scr[j] = jnp.full((tq, LANES), NEG, F32)

        all_rows = slice(0, tq)

        def scores(j, ki, rows):
            return jnp.dot(qs[j][rows], kaug_scr[j, ki], preferred_element_type=F32)

        def tile(ki, mask, rows, next_rows):
            k0 = pl.multiple_of(ki * tk, tk)
            for j in range(nh):
                s = s_scr[j, rows, :]
                if mask is not None:
                    s = jnp.where(mask[rows], s, NEG)
                if next_rows is not None:
                    s_scr[j, next_rows, :] = scores(j, ki + 1, next_rows)
                m_prev = m_scr[j, rows, :]
                m_new = jnp.maximum(m_prev, jnp.max(s, axis=1, keepdims=True))
                alpha = jnp.exp2(m_prev - m_new)
                p = jnp.exp2(s - _lane_tile(m_new, tk // LANES))
                pv = jnp.dot(p.astype(BF16), vaug_scr[j, pl.ds(k0, tk), :], preferred_element_type=F32)
                acc_scr[j, rows, :] = _lane_tile(alpha, 2) * acc_scr[j, rows, :] + pv
                m_scr[j, rows, :] = m_new

        def k_body(ki, c2):
            tile(ki, None, all_rows, all_rows)
            return c2

        for j in range(nh):
            s_scr[j] = scores(j, 0, all_rows)
        n_off = qi * n_diag
        lax.fori_loop(0, n_off, k_body, 0)
        for d in range(n_diag):
            nxt = slice((d + 1) * tk, tq) if d + 1 < n_diag else None
            tile(n_off + d, causal[d], slice(d * tk, tq), nxt)
        for j in range(nh):
            acc = acc_scr[j]
            o_ref[0, pl.ds(q0, tq), j * HEAD_DIM:(j + 1) * HEAD_DIM] = acc[:, 0:HEAD_DIM] / acc[:, HEAD_DIM:]
        return carry

    lax.fori_loop(0, seq // tq, q_body, 0)


def _fox_call(o16v, cs, batch, seq):
    nh = FOX_HEADS_PER_STEP
    tq, tk = min(512, seq), min(256, seq)
    groups = N_FOX_HEADS // nh
    w = nh * HEAD_DIM
    return pl.pallas_call(
        functools.partial(_fox_kernel, seq=seq, tq=tq, tk=tk, nh=nh),
        grid=(batch, groups),
        in_specs=[pl.BlockSpec((1, seq, w), lambda b, h: (b, 0, h)),
                  pl.BlockSpec((1, seq, w), lambda b, h: (b, 0, groups + h)),
                  pl.BlockSpec((1, seq, w), lambda b, h: (b, 0, 2 * groups + h)),
                  pl.BlockSpec((1, 1, 3, seq, LANES), lambda b, h: (b, h, 0, 0, 0))],
        out_specs=pl.BlockSpec((1, seq, w), lambda b, h: (b, 0, h)),
        out_shape=jax.ShapeDtypeStruct((batch, seq, FOX_WIDTH), F32),
        scratch_shapes=[pltpu.VMEM((nh, tq, LANES), F32),
                        pltpu.VMEM((nh, tq, 2 * HEAD_DIM), F32),
                        pltpu.VMEM((nh, tq, tk), F32),
                        pltpu.VMEM((nh, seq // tk, 2 * HEAD_DIM, tk), BF16),
                        pltpu.VMEM((nh, seq, 2 * HEAD_DIM), BF16)],
        compiler_params=_params(2),
        name="fox_attention",
    )(o16v, o16v, o16v, cs)


def _cmp_kernel(x_ref, pos_ref, w1_ref, w2_ref, gain_ref, o_ref, xs_scr, *, seq, ncp):
    rows = xs_scr.shape[1]
    for kv in range(2):
        for h in range(N_NSA_KV_HEADS):
            n = kv * N_NSA_KV_HEADS + h
            xs_scr[n, 0:seq, :] = x_ref[0, :, n * HEAD_DIM:(n + 1) * HEAD_DIM]
            xs_scr[n, seq:rows, :] = jnp.zeros((rows - seq, LANES), F32)
            acc = jnp.zeros((ncp, HEAD_DIM), F32)
            for l in range(CMP_BLOCK):
                xl = xs_scr[n, pl.ds(l, ncp, stride=CMP_STRIDE), :] + pos_ref[kv, l:l + 1, :]
                acc = acc + jnp.dot(xl.astype(BF16), w1_ref[kv, l], preferred_element_type=F32)
            hmid = acc * _sigmoid(acc)
            y = jnp.dot(hmid.astype(BF16), w2_ref[kv], preferred_element_type=F32)
            if kv == 0:
                y = _rms(y) * gain_ref[...]
            o_ref[0, kv, h] = y.astype(BF16)


def _cmp_call(o32, pos, w1, w2, gain, batch, seq, ncp):
    o32v = o32.reshape(batch, seq, W32)
    hkv = N_NSA_KV_HEADS
    width = 2 * NSA_KV_WIDTH
    return pl.pallas_call(
        functools.partial(_cmp_kernel, seq=seq, ncp=ncp),
        grid=(batch,),
        in_specs=[pl.BlockSpec((1, seq, width), lambda b: (b, 0, 0)),
                  pl.BlockSpec((2, CMP_BLOCK, HEAD_DIM), lambda b: (0, 0, 0)),
                  pl.BlockSpec((2, CMP_BLOCK, HEAD_DIM, HEAD_DIM), lambda b: (0, 0, 0, 0)),
                  pl.BlockSpec((2, HEAD_DIM, HEAD_DIM), lambda b: (0, 0, 0)),
                  pl.BlockSpec((1, HEAD_DIM), lambda b: (0, 0))],
        out_specs=pl.BlockSpec((1, 2, hkv, ncp, HEAD_DIM), lambda b: (b, 0, 0, 0, 0)),
        out_shape=jax.ShapeDtypeStruct((batch, 2, hkv, ncp, HEAD_DIM), BF16),
        scratch_shapes=[pltpu.VMEM((2 * hkv, CMP_STRIDE * ncp + CMP_BLOCK, LANES), F32)],
        compiler_params=_params(1),
        name="nsa_compress",
    )(o32v, pos, w1, w2, gain)


def _bias_kernel(rb_ref, o_ref, *, width, key_stride, key_offset, first_tile, rolled_tiles):
    v = pl.program_id(0) + first_tile
    i = lax.broadcasted_iota(jnp.int32, (LANES, width), 0)
    j = lax.broadcasted_iota(jnp.int32, (LANES, width), 1)
    d = v * LANES + i - (key_stride * j + key_offset)
    n = jnp.maximum(d, 0)
    max_exact = N_BUCKETS // 2
    nf = jnp.maximum(n, 1).astype(F32)
    large = max_exact + jnp.trunc(jnp.log(nf / max_exact) / math.log(MAX_DISTANCE / max_exact)
                                  * (N_BUCKETS - max_exact))
    large = jnp.minimum(large, float(N_BUCKETS - 1))
    bkt = jnp.where(n < max_exact, n.astype(F32), large)
    vals = [jnp.zeros((LANES, width), F32) for _ in range(N_NSA_HEADS)]
    for bk in range(N_BUCKETS):
        hit = bkt == float(bk)
        for h in range(N_NSA_HEADS):
            vals[h] = jnp.where(hit, rb_ref[bk * N_NSA_HEADS + h] * LOG2E, vals[h])
    for h in range(N_NSA_HEADS):
        g = h % NSA_GROUP
        rows = slice(g * LANES, (g + 1) * LANES)
        if rolled_tiles:
            per_tile = LANES // key_stride
            for t in range(rolled_tiles):
                shift = (width - per_tile * (rolled_tiles - 1 - t)) % width
                o_ref[t, h // NSA_GROUP, rows, :] = vals[h] if shift == 0 else pltpu.roll(vals[h], shift, 1)
        else:
            o_ref[0, h // NSA_GROUP, rows, :] = vals[h]


def _bias_call(rb_flat, n_tiles, width, key_stride, key_offset, name, rolled=False):
    kern = functools.partial(_bias_kernel, width=width, key_stride=key_stride, key_offset=key_offset,
                             first_tile=n_tiles - 1 if rolled else 0, rolled_tiles=n_tiles if rolled else 0)
    block = (n_tiles if rolled else 1, N_NSA_KV_HEADS, GROUP_ROWS, width)
    return pl.pallas_call(
        kern,
        grid=(1 if rolled else n_tiles,),
        in_specs=[pl.BlockSpec(memory_space=pltpu.SMEM)],
        out_specs=pl.BlockSpec(block, lambda v: (v, 0, 0, 0)),
        out_shape=jax.ShapeDtypeStruct((n_tiles, N_NSA_KV_HEADS, GROUP_ROWS, width), F32),
        compiler_params=_params(1),
        name=name,
    )(rb_flat)


def _tile4(a):
    return jnp.concatenate([a] * NSA_GROUP, axis=0)


def _nsa_kernel(q_ref, ks_ref, kw_ref, vs_ref, vw_ref, g_ref, kcvc_ref, bc_ref, tz_ref, selmt_ref,
                wa_ref, wb_ref, wc_ref, wd_ref, o_ref, wa_out, wb_out, wc_out, wd_out,
                qg_scr, qw_scr, caug_scr, near_scr, edge_scr, m_scr, acc_scr, o_scr, s_scr,
                ksa_scr, vsa_scr, kwp_scr, vwa_scr, *, seq, ncp):
    qt = pl.program_id(1)
    q0 = qt * LANES
    n_slc = seq // SEL_BLOCK
    top_n = min(N_SEL, n_slc)
    nkv = N_NSA_KV_HEADS
    for src, dst in ((wa_ref, wa_out), (wb_ref, wb_out), (wc_ref, wc_out), (wd_ref, wd_out)):
        dst[...] = src[...].astype(BF16)
    ri = lax.broadcasted_iota(jnp.int32, (LANES, LANES), 0)
    ci = lax.broadcasted_iota(jnp.int32, (LANES, LANES), 1)
    eye = jnp.where(ri == ci, 1.0, 0.0).astype(BF16)
    gates = _sigmoid(g_ref[0])

    def gate_col(hk, br):
        cols = []
        for g in range(NSA_GROUP):
            c = COL_NG % LANES + (hk * NSA_GROUP + g) * N_BRANCH + br
            cols.append(gates[:, c:c + 1])
        return jnp.concatenate(cols, axis=0)

    @pl.when(qt == 0)
    def _():
        ones = jnp.ones((seq, HEAD_DIM), BF16)
        row = lax.broadcasted_iota(jnp.int32, (LANES, LANES), 0)
        is_far = jnp.logical_and(row >= AUG_FAR, row < AUG_FAR + 3)
        pad_aug = jnp.where(row == AUG_PAD, -MASK_BIG, jnp.where(is_far, 1.0, 0.0)).astype(BF16)
        win_aug = jnp.where(is_far, 1.0, 0.0).astype(BF16)
        lane512 = lax.broadcasted_iota(jnp.int32, (GROUP_ROWS, LANES), 1)
        causal = _tile4(jnp.where(ci <= ri, 0.0, -MASK_BIG))
        acc_scr[...] = jnp.zeros(acc_scr.shape, F32)
        edge_scr[...] = jnp.concatenate([jnp.full((GROUP_ROWS, LANES), -MASK_BIG, F32),
                                         _tile4(jnp.where(ri < ci, 0.0, -MASK_BIG))], axis=1)
        for hk in range(nkv):
            hc = slice(hk * HEAD_DIM, (hk + 1) * HEAD_DIM)
            far = tz_ref[2, hk]
            hi, mid, lo = _split3(far)
            caug = jnp.where(lane512 == AUG_FAR, hi, jnp.where(lane512 == AUG_FAR + 1, mid,
                             jnp.where(lane512 == AUG_FAR + 2, lo, jnp.where(lane512 == AUG_PAD, 1.0, 0.0))))
            caug_scr[hk] = caug.astype(BF16)
            qw_scr[hk, :, HEAD_DIM:] = caug.astype(BF16)
            near_scr[hk] = jnp.concatenate([tz_ref[1, hk] - far, (tz_ref[0, hk] - far) + causal], axis=1)
            ksa_scr[hk, 0, 0:HEAD_DIM, :] = jnp.zeros((HEAD_DIM, LANES), BF16)
            ksa_scr[hk, 0, HEAD_DIM:, :] = pad_aug
            for t in range(seq // LANES):
                rows = slice(t * LANES, (t + 1) * LANES)
                ksa_scr[hk, SEL_PAD_TILES + t, 0:HEAD_DIM, :] = ks_ref[0, rows, hc].T
                ksa_scr[hk, SEL_PAD_TILES + t, HEAD_DIM:, :] = jnp.where(
                    (t * LANES + ci) // SEL_BLOCK == ri, -MASK_BIG, jnp.where(is_far, 1.0, 0.0)).astype(BF16)
                kwp_scr[hk, WIN_PAD // LANES + t, 0:HEAD_DIM, :] = kw_ref[0, rows, hc].T
                kwp_scr[hk, WIN_PAD // LANES + t, HEAD_DIM:, :] = win_aug
            for t in range(WIN_PAD // LANES):
                kwp_scr[hk, t, 0:HEAD_DIM, :] = jnp.zeros((HEAD_DIM, LANES), BF16)
                kwp_scr[hk, t, HEAD_DIM:, :] = pad_aug
            vsa_scr[hk, 0:SEL_PAD_TILES * LANES, :] = jnp.zeros((SEL_PAD_TILES * LANES, 2 * HEAD_DIM), BF16)
            vsa_scr[hk, SEL_PAD_TILES * LANES:, 0:HEAD_DIM] = vs_ref[0, :, hc]
            vsa_scr[hk, SEL_PAD_TILES * LANES:, HEAD_DIM:] = ones
            vwa_scr[hk, 0:WIN_PAD, :] = jnp.zeros((WIN_PAD, 2 * HEAD_DIM), BF16)
            vwa_scr[hk, WIN_PAD:, 0:HEAD_DIM] = vw_ref[0, :, hc]
            vwa_scr[hk, WIN_PAD:, HEAD_DIM:] = ones

    def reset(br):
        m_scr[br] = jnp.full(m_scr.shape[1:], NEG, F32)

    def online_update(br, hk, s, vaug):
        m_prev = m_scr[br, hk]
        m_new = jnp.maximum(m_prev, jnp.max(s, axis=1, keepdims=True))
        alpha = jnp.exp2(m_prev - m_new)
        p = jnp.exp2(s - _lane_tile(m_new, s.shape[1] // LANES))
        pv = jnp.dot(p.astype(BF16), vaug, preferred_element_type=F32)
        acc_scr[br, hk] = _lane_tile(alpha, 2) * acc_scr[br, hk] + pv
        m_scr[br, hk] = m_new

    def finish(br, hk):
        acc = acc_scr[br, hk]
        return acc[:, 0:HEAD_DIM] / acc[:, HEAD_DIM:]

    for hk in range(nkv):
        for g in range(NSA_GROUP):
            h = hk * NSA_GROUP + g
            q_h = q_ref[0, :, h * HEAD_DIM:(h + 1) * HEAD_DIM]
            qg_scr[hk, g * LANES:(g + 1) * LANES, 0:HEAD_DIM] = q_h
            qw_scr[hk, g * LANES:(g + 1) * LANES, 0:HEAD_DIM] = q_h

    for hk in range(nkv):
        qg = qg_scr[hk, :, 0:HEAD_DIM]

        kc = kcvc_ref[0, 0, hk]
        vc = kcvc_ref[0, 1, hk]
        s = lax.dot_general(qg, kc, NT_DIMS, preferred_element_type=F32) + bc_ref[0, hk]
        rc = lax.broadcasted_iota(jnp.int32, (LANES, ncp), 0)
        cc = lax.broadcasted_iota(jnp.int32, (LANES, ncp), 1)
        valid_c = _tile4(jnp.where(q0 + rc - (CMP_STRIDE * cc + CMP_BLOCK - 1) >= 0, 1.0, 0.0)) > 0.5
        s = jnp.where(valid_c, s, NEG)
        p = jnp.where(valid_c, jnp.exp2(s - jnp.max(s, axis=1, keepdims=True)), 0.0)
        l = jnp.sum(p, axis=1, keepdims=True)
        p = p / jnp.where(l > 0.0, l, 1.0)
        o_scr[hk] = gate_col(hk, 0) * jnp.dot(p.astype(BF16), vc, preferred_element_type=F32)

        psum = p[0:LANES]
        for g in range(1, NSA_GROUP):
            psum = psum + p[g * LANES:(g + 1) * LANES]
        p_hi = psum.astype(BF16)
        p_lo = (psum - p_hi.astype(F32)).astype(BF16)
        selmt = selmt_ref[...]
        imp = (lax.dot_general(selmt, p_hi, NT_DIMS, preferred_element_type=F32)
               + lax.dot_general(selmt, p_lo, NT_DIMS, preferred_element_type=F32))
        imp = imp[0:n_slc]
        blk = lax.broadcasted_iota(jnp.int32, (n_slc, LANES), 0)
        cur = (q0 + lax.broadcasted_iota(jnp.int32, (n_slc, LANES), 1)) // SEL_BLOCK
        forced = (blk == 0) | (blk == cur) | (blk == cur - 1)
        imp = jnp.where(forced, FORCE, imp)
        imp = jnp.where(blk <= cur, imp, -jnp.inf)
        rank = jnp.zeros((n_slc, LANES), F32)
        for j in range(n_slc):
            row = imp[j:j + 1, :]
            beats = jnp.where(row > imp, 1.0, jnp.where(row == imp, jnp.where(blk > j, 1.0, 0.0), 0.0))
            rank = rank + beats
        sel_t = jnp.where(rank < top_n, jnp.where(imp > -jnp.inf, 1.0, 0.0), 0.0)
        if n_slc < LANES:
            sel_t = jnp.concatenate([sel_t, jnp.zeros((LANES - n_slc, LANES), F32)], axis=0)
        sel_q = lax.dot_general(eye, sel_t.astype(BF16), NT_DIMS, preferred_element_type=F32)
        not_sel = jnp.where(ci < n_slc, 1.0 - sel_q, 0.0).astype(BF16)
        for g in range(NSA_GROUP):
            rows = slice(g * LANES, (g + 1) * LANES)
            qg_scr[hk, rows, HEAD_DIM:] = jnp.where(ci < n_slc, not_sel, caug_scr[hk, rows, :])

    reset(WIN)
    for off, table in ((1, near_scr), (3, None), (5, edge_scr)):
        t0 = qt - off + WIN_PAD // LANES
        p0 = pl.multiple_of(t0 * LANES, LANES)
        for hk in range(nkv):
            k_t = jnp.concatenate([kwp_scr[hk, t0], kwp_scr[hk, t0 + 1]], axis=1)
            s = jnp.dot(qw_scr[hk], k_t, preferred_element_type=F32)
            if table is not None:
                s = s + (table[hk] if table is near_scr else table[...])
            online_update(WIN, hk, s, vwa_scr[hk, pl.ds(p0, KEY_CHUNK), :])

    reset(SEL)
    n_chunks = (qt + 2) // 2
    first_tile = SEL_PAD_TILES - (qt + 1) % 2

    def sel_scores(hk, c):
        t0 = first_tile + 2 * c
        k_t = jnp.concatenate([ksa_scr[hk, t0], ksa_scr[hk, t0 + 1]], axis=1)
        return jnp.dot(qg_scr[hk], k_t, preferred_element_type=F32)

    def sel_chunk(c, last):
        p0 = pl.multiple_of((first_tile + 2 * c) * LANES, LANES)
        for hk in range(nkv):
            s = s_scr[hk]
            if last:
                s = s + near_scr[hk]
            else:
                s_scr[hk] = sel_scores(hk, c + 1)
            online_update(SEL, hk, s, vsa_scr[hk, pl.ds(p0, KEY_CHUNK), :])

    def sel_pair(i, carry):
        sel_chunk(2 * i, False)
        sel_chunk(2 * i + 1, False)
        return carry

    for hk in range(nkv):
        s_scr[hk] = sel_scores(hk, 0)
    n_far = n_chunks - 1
    lax.fori_loop(0, n_far // 2, sel_pair, 0)
    pl.when(n_far % 2 == 1)(lambda: sel_chunk(n_far - 1, False))
    sel_chunk(n_chunks - 1, True)
    for hk in range(nkv):
        o = (o_scr[hk] + gate_col(hk, 1) * finish(SEL, hk)) + gate_col(hk, 2) * finish(WIN, hk)
        for g in range(NSA_GROUP):
            h = hk * NSA_GROUP + g
            o_ref[0, :, h * HEAD_DIM:(h + 1) * HEAD_DIM] = o[g * LANES:(g + 1) * LANES]


def _nsa_call(o16v, o32v, kcvc, bias_c, tz, selmt, weights_f32, batch, seq, ncp):
    nqt = seq // LANES
    wspecs = []
    for w in weights_f32:
        blk, span = _cast_block(w, batch * nqt)
        wspecs.append(pl.BlockSpec(blk, functools.partial(lambda b, t, span: ((b * nqt + t) // span, 0), span=span)))
    kvw = NSA_KV_WIDTH
    nkv = N_NSA_KV_HEADS
    base = (3 * FOX_WIDTH + NSA_WIDTH) // kvw
    return pl.pallas_call(
        functools.partial(_nsa_kernel, seq=seq, ncp=ncp),
        grid=(batch, nqt),
        in_specs=[pl.BlockSpec((1, LANES, NSA_WIDTH), lambda b, t: (b, t, 3 * FOX_WIDTH // NSA_WIDTH)),
                  pl.BlockSpec((1, seq, kvw), lambda b, t: (b, 0, base)),
                  pl.BlockSpec((1, seq, kvw), lambda b, t: (b, 0, base + 1)),
                  pl.BlockSpec((1, seq, kvw), lambda b, t: (b, 0, base + 2)),
                  pl.BlockSpec((1, seq, kvw), lambda b, t: (b, 0, base + 3)),
                  pl.BlockSpec((1, LANES, LANES), lambda b, t: (b, t, COL_NG // LANES)),
                  pl.BlockSpec((1, 2, N_NSA_KV_HEADS, ncp, HEAD_DIM), lambda b, t: (b, 0, 0, 0, 0)),
                  pl.BlockSpec((1, N_NSA_KV_HEADS, GROUP_ROWS, ncp), lambda b, t: (t, 0, 0, 0)),
                  pl.BlockSpec((3, N_NSA_KV_HEADS, GROUP_ROWS, LANES), lambda b, t: (0, 0, 0, 0)),
                  pl.BlockSpec((LANES, ncp), lambda b, t: (0, 0))] + wspecs,
        out_specs=[pl.BlockSpec((1, LANES, NSA_WIDTH), lambda b, t: (b, t, 0))] + wspecs,
        out_shape=[jax.ShapeDtypeStruct((batch, seq, NSA_WIDTH), F32)]
        + [jax.ShapeDtypeStruct(w.shape, BF16) for w in weights_f32],
        scratch_shapes=[pltpu.VMEM((nkv, GROUP_ROWS, 2 * HEAD_DIM), BF16),
                        pltpu.VMEM((nkv, GROUP_ROWS, 2 * HEAD_DIM), BF16),
                        pltpu.VMEM((nkv, GROUP_ROWS, LANES), BF16),
                        pltpu.VMEM((nkv, GROUP_ROWS, KEY_CHUNK), F32),
                        pltpu.VMEM((GROUP_ROWS, KEY_CHUNK), F32),
                        pltpu.VMEM((2, nkv, GROUP_ROWS, LANES), F32),
                        pltpu.VMEM((2, nkv, GROUP_ROWS, 2 * HEAD_DIM), F32),
                        pltpu.VMEM((nkv, GROUP_ROWS, HEAD_DIM), F32),
                        pltpu.VMEM((nkv, GROUP_ROWS, KEY_CHUNK), F32),
                        pltpu.VMEM((nkv, SEL_PAD_TILES + seq // LANES, 2 * HEAD_DIM, LANES), BF16),
                        pltpu.VMEM((nkv, SEL_PAD_TILES * LANES + seq, 2 * HEAD_DIM), BF16),
                        pltpu.VMEM((nkv, (seq + WIN_PAD) // LANES, 2 * HEAD_DIM, LANES), BF16),
                        pltpu.VMEM((nkv, seq + WIN_PAD, 2 * HEAD_DIM), BF16)],
        compiler_params=_params(2),
        name="nsa_attention",
    )(o16v, o16v, o16v, o16v, o16v, o32v, kcvc, bias_c, tz, selmt, *weights_f32)


def _out_kernel(of_ref, on_ref, gain_ref, w_ref, x_ref, g_ref, o_ref, y_scr, *, tm, n_chains):
    rows_per_chain = tm // n_chains
    gain_f = gain_ref[:, 0:FOX_WIDTH]
    gain_n = gain_ref[:, FOX_WIDTH:MIX_WIDTH]
    for c in range(n_chains):
        rows = slice(c * rows_per_chain, (c + 1) * rows_per_chain)
        for r0 in range(c * rows_per_chain, (c + 1) * rows_per_chain, NORM_ROWS):
            r = slice(r0, r0 + NORM_ROWS)
            y_scr[r, 0:FOX_WIDTH] = (_rms(of_ref[r, :]) * gain_f).astype(BF16)
            y_scr[r, FOX_WIDTH:MIX_WIDTH] = (_rms(on_ref[r, :]) * gain_n).astype(BF16)
        acc = jnp.dot(y_scr[rows, :], w_ref[...], preferred_element_type=F32)
        o_ref[rows, :] = x_ref[rows, :] + g_ref[0] * acc


def _out_call(o_fox, o_nsa, gain, w_out, x2, g1, seq):
    t, d = x2.shape
    tm = min(512, seq)
    rows_per_batch = seq // tm
    return pl.pallas_call(
        functools.partial(_out_kernel, tm=tm, n_chains=2),
        grid=(t // tm,),
        in_specs=[pl.BlockSpec((tm, FOX_WIDTH), lambda i: (i, 0)),
                  pl.BlockSpec((tm, NSA_WIDTH), lambda i: (i, 0)),
                  pl.BlockSpec((1, MIX_WIDTH), lambda i: (0, 0)),
                  pl.BlockSpec((MIX_WIDTH, d), lambda i: (0, 0)),
                  pl.BlockSpec((tm, d), lambda i: (i, 0)),
                  pl.BlockSpec((1, 1, d), lambda i: (i // rows_per_batch, 0, 0))],
        out_specs=pl.BlockSpec((tm, d), lambda i: (i, 0)),
        out_shape=jax.ShapeDtypeStruct((t, d), F32),
        scratch_shapes=[pltpu.VMEM((tm, MIX_WIDTH), BF16)],
        compiler_params=_params(1),
        name="out_proj",
    )(o_fox, o_nsa, gain, w_out, x2, g1)


def _ffn_kernel(x_ref, gain_ref, sc_ref, sh_ref, g_ref, wg_ref, wu_ref, wd_ref, o_ref, h_scr):
    f = pl.program_id(1)

    @pl.when(f == 0)
    def _():
        _modulated_norm(x_ref, gain_ref, sc_ref, sh_ref, h_scr)
        o_ref[...] = jnp.zeros(o_ref.shape, F32)

    h = h_scr[...]
    a = jnp.dot(h, wg_ref[...], preferred_element_type=F32)
    u = jnp.dot(h, wu_ref[...], preferred_element_type=F32)
    t = (a * _sigmoid(a)) * u
    o_ref[...] += jnp.dot(t.astype(BF16), wd_ref[...], preferred_element_type=F32)

    @pl.when(f == pl.num_programs(1) - 1)
    def _():
        o_ref[...] = x_ref[...] + g_ref[0] * o_ref[...]


def _ffn_call(x1, gain2, sc, sh, g2, wg, wu, wd, seq):
    t, d = x1.shape
    dff = wg.shape[1]
    tm = min(1024, seq)
    tf = 512 if dff % 512 == 0 else dff
    rows_per_batch = seq // tm
    return pl.pallas_call(
        _ffn_kernel,
        grid=(t // tm, dff // tf),
        in_specs=[pl.BlockSpec((tm, d), lambda i, f: (i, 0)),
                  pl.BlockSpec((1, d), lambda i, f: (0, 0)),
                  pl.BlockSpec((1, 1, d), lambda i, f: (i // rows_per_batch, 0, 0)),
                  pl.BlockSpec((1, 1, d), lambda i, f: (i // rows_per_batch, 0, 0)),
                  pl.BlockSpec((1, 1, d), lambda i, f: (i // rows_per_batch, 0, 0)),
                  pl.BlockSpec((d, tf), lambda i, f: (0, f)),
                  pl.BlockSpec((d, tf), lambda i, f: (0, f)),
                  pl.BlockSpec((tf, d), lambda i, f: (f, 0))],
        out_specs=pl.BlockSpec((tm, d), lambda i, f: (i, 0)),
        out_shape=jax.ShapeDtypeStruct((t, d), F32),
        scratch_shapes=[pltpu.VMEM((tm, d), BF16)],
        compiler_params=_params(2),
        name="swiglu_ffn",
    )(x1, gain2, sc, sh, g2, wg, wu, wd)


def _selection_matrix_t(ncp, n_slc):
    r, q = SEL_BLOCK // CMP_STRIDE, CMP_BLOCK // CMP_STRIDE
    m = np.zeros((LANES, ncp), np.float32)
    for j in range(n_slc):
        for a in range(r):
            for b in range(q):
                c = r * j + a - b
                if 0 <= c < ncp:
                    m[j, c] += 1.0
    return m


def _w_in_block_sources():
    o = 0
    start = {}
    for name, width in (("fq", FOX_WIDTH), ("fk", FOX_WIDTH), ("fv", FOX_WIDTH), ("ff", N_FOX_HEADS),
                        ("nq", NSA_WIDTH), ("nk", N_BRANCH * NSA_KV_WIDTH), ("nv", N_BRANCH * NSA_KV_WIDTH),
                        ("ng", N_BRANCH * N_NSA_HEADS)):
        start[name] = o
        o += width
    kvw = NSA_KV_WIDTH
    groups = [(start["fq"], 3 * FOX_WIDTH), (start["nq"], NSA_WIDTH),
              (start["nk"] + kvw, 2 * kvw), (start["nv"] + kvw, 2 * kvw),
              (start["nk"], kvw), (start["nv"], kvw)]
    blocks = [s + LANES * b for s, width in groups for b in range(width // LANES)]
    return blocks, (start["ff"], N_FOX_HEADS), (start["ng"], N_BRANCH * N_NSA_HEADS)


def _repack_kernel(src_ref, wt_hbm, o16_ref, o32_ref, buf, sem, *, n16, n_whole, ff, ng):
    k = pl.program_id(0)
    n_slots = buf.shape[0]
    ahead = n_slots - 1
    slot = k % n_slots

    def whole_copy(kk, s):
        r0 = pl.multiple_of(src_ref[kk], 8)
        return pltpu.make_async_copy(wt_hbm.at[pl.ds(r0, LANES), :], buf.at[s], sem.at[s])

    def narrow_copies(s):
        return [pltpu.make_async_copy(wt_hbm.at[pl.ds(ff[0], ff[1]), :], buf.at[s, pl.ds(0, ff[1]), :], sem.at[s]),
                pltpu.make_async_copy(wt_hbm.at[pl.ds(ng[0], ng[1]), :], buf.at[s, pl.ds(ff[1], ng[1]), :],
                                      sem.at[s])]

    def start_block(kk):
        @pl.when(kk < n_whole)
        def _():
            whole_copy(kk, kk % n_slots).start()

        @pl.when(kk == n_whole)
        def _():
            for cp in narrow_copies(kk % n_slots):
                cp.start()

    @pl.when(k == 0)
    def _():
        for kk in range(ahead):
            start_block(kk)

    start_block(k + ahead)

    @pl.when(k < n_whole)
    def _():
        whole_copy(k, slot).wait()

    @pl.when(k == n_whole)
    def _():
        for cp in narrow_copies(slot):
            cp.wait()

    xt = buf[slot].T
    lane = lax.broadcasted_iota(jnp.int32, xt.shape, 1)
    xt = jnp.where(jnp.logical_or(k < n_whole, lane < ff[1] + ng[1]), xt, 0.0).astype(BF16)

    @pl.when(k < n16)
    def _():
        o16_ref[...] = xt

    @pl.when(k >= n16)
    def _():
        o32_ref[...] = xt


def _repack_w_in(wt):
    n, d = wt.shape
    blocks, ff, ng = _w_in_block_sources()
    n16 = W16 // LANES
    n_whole = len(blocks)
    assert n_whole + 1 == (W16 + W32) // LANES
    return pl.pallas_call(
        functools.partial(_repack_kernel, n16=n16, n_whole=n_whole, ff=ff, ng=ng),
        grid_spec=pltpu.PrefetchScalarGridSpec(
            num_scalar_prefetch=1,
            grid=(n_whole + 1,),
            in_specs=[pl.BlockSpec(memory_space=pl.ANY)],
            out_specs=[pl.BlockSpec((d, LANES), lambda k, src: (0, jnp.minimum(k, n16 - 1))),
                       pl.BlockSpec((d, LANES), lambda k, src: (0, jnp.maximum(k - n16, 0)))],
            scratch_shapes=[pltpu.VMEM((REPACK_SLOTS, LANES, d), F32), pltpu.SemaphoreType.DMA((REPACK_SLOTS,))]),
        out_shape=[jax.ShapeDtypeStruct((d, W16), BF16), jax.ShapeDtypeStruct((d, W32), BF16)],
        compiler_params=_params(1),
        name="w_in_repack",
    )(jnp.asarray(blocks, jnp.int32), wt)


def kernel(x, c, ada_w, ada_b, norm1_gain, norm2_gain, w_in, fox_f_bias, fox_q_gain, fox_k_gain, nsa_q_gain,
           nsa_k_gain, nsa_cmp_pos, nsa_cmp_w1, nsa_cmp_w2, rel_bias, mix_out_gain, w_out, ffn_w_gate, ffn_w_up,
           ffn_w_down):
    batch, seq, d = x.shape
    assert seq % KEY_CHUNK == 0 and seq >= WINDOW and d % LANES == 0 and seq // SEL_BLOCK <= LANES
    depth = ada_w.shape[0]
    nqt = seq // LANES
    ncp = -(-(seq // CMP_STRIDE) // LANES) * LANES

    selmt = jnp.asarray(_selection_matrix_t(ncp, seq // SEL_BLOCK), BF16)
    rb_flat = rel_bias.reshape(-1)
    bias_c = _bias_call(rb_flat, nqt, ncp, CMP_STRIDE, CMP_BLOCK - 1, "t5_bias_compressed", rolled=True)
    tz = _bias_call(rb_flat, 3, LANES, 1, 0, "t5_bias_toeplitz")

    ones_h = jnp.ones((HEAD_DIM,), F32)
    c_pad = jnp.pad(c, ((0, 8 - batch % 8 if batch % 8 else 0), (0, 0)))
    x2 = x.reshape(batch * seq, d)
    for layer in range(depth):
        mod = _ada_call(c_pad, ada_w[layer], ada_b[layer][None, :])[:batch]
        sh1, sc1, g1, sh2, sc2, g2 = [mod[:, i * d:(i + 1) * d][:, None, :] for i in range(N_MOD)]

        kg = nsa_k_gain[layer]
        col_gain = jnp.concatenate([
            jnp.tile(fox_q_gain[layer] * QSCALE, N_FOX_HEADS), jnp.tile(fox_k_gain[layer], N_FOX_HEADS),
            jnp.tile(ones_h, N_FOX_HEADS), jnp.tile(nsa_q_gain[layer] * QSCALE, N_NSA_HEADS),
            jnp.tile(kg[1], N_NSA_KV_HEADS), jnp.tile(kg[2], N_NSA_KV_HEADS),
            jnp.tile(ones_h, 2 * N_NSA_KV_HEADS)])[None, :]
        col_flag = jnp.concatenate([
            jnp.ones((2 * FOX_WIDTH,), F32), jnp.zeros((FOX_WIDTH,), F32), jnp.ones((NSA_WIDTH,), F32),
            jnp.ones((2 * NSA_KV_WIDTH,), F32), jnp.zeros((2 * NSA_KV_WIDTH,), F32)])[None, :]
        w16, w32 = _repack_w_in(jnp.swapaxes(w_in, 1, 2)[layer])
        o16, o32 = _proj_call(x2, sc1, sh1, norm1_gain[layer][None, :], w16, w32,
                              col_gain, col_flag, seq)
        o16v = o16.reshape(batch, seq, W16)
        o32v = o32.reshape(batch, seq, W32)

        fb_pad = jnp.pad(fox_f_bias[layer], (0, LANES - N_FOX_HEADS))[None, :]
        cs = _cum_call(o32, fb_pad, batch, seq, N_FOX_HEADS // FOX_HEADS_PER_STEP)
        o_fox = _fox_call(o16v, cs, batch, seq)

        w1 = nsa_cmp_w1[layer].reshape(2, CMP_BLOCK, HEAD_DIM, HEAD_DIM).astype(BF16)
        kcvc = _cmp_call(o32, nsa_cmp_pos[layer], w1, nsa_cmp_w2[layer].astype(BF16), kg[0][None, :],
                         batch, seq, ncp)
        o_nsa, wg16, wu16, wo16, wd16 = _nsa_call(
            o16v, o32v, kcvc, bias_c, tz, selmt,
            (ffn_w_gate[layer], ffn_w_up[layer], w_out[layer], ffn_w_down[layer]), batch, seq, ncp)

        x1 = _out_call(o_fox.reshape(batch * seq, FOX_WIDTH), o_nsa.reshape(batch * seq, NSA_WIDTH),
                       mix_out_gain[layer][None, :], wo16, x2, g1, seq)
        x2 = _ffn_call(x1, norm2_gain[layer][None, :], sc2, sh2, g2, wg16, wu16, wd16, seq)
    return x2.reshape(batch, seq, d)
```

```python
import functools
import math

import numpy as np
import jax
import jax.numpy as jnp
from jax import lax
from jax.experimental import pallas as pl
from jax.experimental.pallas import tpu as pltpu

HEAD_DIM = 128
N_FOX_HEADS = 8
N_NSA_HEADS = 8
N_NSA_KV_HEADS = 2
NSA_GROUP = N_NSA_HEADS // N_NSA_KV_HEADS
FOX_WIDTH = N_FOX_HEADS * HEAD_DIM
NSA_WIDTH = N_NSA_HEADS * HEAD_DIM
NSA_KV_WIDTH = N_NSA_KV_HEADS * HEAD_DIM
MIX_WIDTH = FOX_WIDTH + NSA_WIDTH
N_BRANCH = 3
CMP_BLOCK = 32
CMP_STRIDE = 16
SEL_BLOCK = 64
N_SEL = 8
WINDOW = 512
N_BUCKETS = 32
MAX_DISTANCE = 128
N_MOD = 6
SCALE = HEAD_DIM ** -0.5
LOG2E = math.log2(math.e)
LOG2E_HI = float(np.float32(LOG2E))
LOG2E_LO = LOG2E - LOG2E_HI
QSCALE = SCALE * LOG2E
EPS = 1e-6
NEG = -1e30
FORCE = 1e6

LANES = 128
GROUP_ROWS = NSA_GROUP * LANES
VMEM_LIMIT = 56 * 1024 * 1024
MXU_COLS = 256
KEY_CHUNK = MXU_COLS
WIN_PAD = WINDOW + LANES
MASK_BIG = 2.0 ** 100
SEL, WIN = 0, 1
REPACK_SLOTS = 4
AUG_FAR, AUG_PAD = 120, 127
FOX_HEADS_PER_STEP = 4
SEL_PAD_TILES = 1

W16 = 3 * FOX_WIDTH + NSA_WIDTH + 4 * NSA_KV_WIDTH
W32 = 5 * LANES
COL_FF = 2 * NSA_KV_WIDTH
COL_NG = COL_FF + N_FOX_HEADS

F32 = jnp.float32
BF16 = jnp.bfloat16
NT_DIMS = (((1,), (1,)), ((), ()))


def _params(n_axes):
    return pltpu.CompilerParams(dimension_semantics=("arbitrary",) * n_axes,
                                vmem_limit_bytes=VMEM_LIMIT)


def _sigmoid(x):
    return 1.0 / (1.0 + jnp.exp(-x))


def _lane_tile(a, n):
    return jnp.concatenate([a] * n, axis=1)


BF16_SUBLANES = 16


def _cast_block(w, n_steps):
    rows, cols = w.shape
    assert rows % n_steps == 0, (w.shape, n_steps)
    per_step = rows // n_steps
    span = BF16_SUBLANES // math.gcd(BF16_SUBLANES, per_step)
    assert n_steps % span == 0, (w.shape, n_steps)
    return (per_step * span, cols), span


def _rms(x):
    return x * lax.rsqrt(jnp.mean(x * x, axis=-1, keepdims=True) + EPS)


NORM_ROWS = 16


def _modulated_norm(x_ref, gain_ref, sc_ref, sh_ref, h_ref, first_row, n_rows):
    gm = gain_ref[...] * (1.0 + sc_ref[0])
    sh = sh_ref[0]
    for r0 in range(first_row, first_row + n_rows, NORM_ROWS):
        rows = slice(r0, r0 + NORM_ROWS)
        h_ref[rows, :] = (_rms(x_ref[rows, :]) * gm + sh).astype(BF16)


ROW_CHAINS = 2


def _ada_kernel(c_ref, w_ref, b_ref, o_ref):
    c = c_ref[...]
    s = (c * _sigmoid(c)).astype(BF16)
    o_ref[...] = jnp.dot(s, w_ref[...].astype(BF16), preferred_element_type=F32) + b_ref[...]


def _ada_call(c_pad, w, b):
    rows, d = c_pad.shape
    n = w.shape[1]
    tn = next(t for t in (1024, 768, 512, 384, 256, 128) if n % t == 0)
    return pl.pallas_call(
        _ada_kernel,
        grid=(n // tn,),
        in_specs=[pl.BlockSpec((rows, d), lambda j: (0, 0)),
                  pl.BlockSpec((d, tn), lambda j: (0, j)),
                  pl.BlockSpec((1, tn), lambda j: (0, j))],
        out_specs=pl.BlockSpec((rows, tn), lambda j: (0, j)),
        out_shape=jax.ShapeDtypeStruct((rows, n), F32),
        compiler_params=_params(1),
        name="adaln",
    )(c_pad, w, b)


def _proj_kernel(x_ref, sc_ref, sh_ref, g_ref, w16_ref, w32_ref, gain_ref, flag_ref, o16_ref, o32_ref, h_scr, *, tn):
    j = pl.program_id(1)
    tm = h_scr.shape[0]

    def column_step(first_row, n_rows, with_side_outputs):
        rows = slice(first_row, first_row + n_rows)
        h = h_scr[rows, :]
        for c in range(tn // MXU_COLS):
            acc = jnp.dot(h, w16_ref[:, c * MXU_COLS:(c + 1) * MXU_COLS], preferred_element_type=F32)
            for g in range(MXU_COLS // LANES):
                cols = slice(c * MXU_COLS + g * LANES, c * MXU_COLS + (g + 1) * LANES)
                a = acc[:, g * LANES:(g + 1) * LANES]
                r = lax.rsqrt(jnp.mean(a * a, axis=-1, keepdims=True) + EPS)
                scale = jnp.where(flag_ref[:, cols] > 0.5, r, 1.0)
                o16_ref[rows, cols] = (a * scale * gain_ref[:, cols]).astype(BF16)
        if with_side_outputs:
            o32_ref[rows, :] = jnp.dot(h, w32_ref[...], preferred_element_type=F32)

    @pl.when(j == 0)
    def _():
        n_rows = tm // ROW_CHAINS
        for c in range(ROW_CHAINS):
            _modulated_norm(x_ref, g_ref, sc_ref, sh_ref, h_scr, c * n_rows, n_rows)
            column_step(c * n_rows, n_rows, False)

    last = pl.num_programs(1) - 1
    pl.when(jnp.logical_and(j > 0, j < last))(functools.partial(column_step, 0, tm, False))
    pl.when(j == last)(functools.partial(column_step, 0, tm, True))


def _proj_call(x2, sc, sh, gain1, w16, w32, col_gain, col_flag, seq):
    t, d = x2.shape
    tm = min(1024, seq)
    tn = 1280 if W16 % 1280 == 0 else 1024
    assert W16 // tn >= 2
    rows_per_batch = seq // tm
    return pl.pallas_call(
        functools.partial(_proj_kernel, tn=tn),
        grid=(t // tm, W16 // tn),
        in_specs=[pl.BlockSpec((tm, d), lambda i, j: (i, 0)),
                  pl.BlockSpec((1, 1, d), lambda i, j: (i // rows_per_batch, 0, 0)),
                  pl.BlockSpec((1, 1, d), lambda i, j: (i // rows_per_batch, 0, 0)),
                  pl.BlockSpec((1, d), lambda i, j: (0, 0)),
                  pl.BlockSpec((d, tn), lambda i, j: (0, j)),
                  pl.BlockSpec((d, W32), lambda i, j: (0, 0)),
                  pl.BlockSpec((1, tn), lambda i, j: (0, j)),
                  pl.BlockSpec((1, tn), lambda i, j: (0, j))],
        out_specs=[pl.BlockSpec((tm, tn), lambda i, j: (i, j)),
                   pl.BlockSpec((tm, W32), lambda i, j: (i, 0))],
        out_shape=[jax.ShapeDtypeStruct((t, W16), BF16),
                   jax.ShapeDtypeStruct((t, W32), F32)],
        scratch_shapes=[pltpu.VMEM((tm, d), BF16)],
        compiler_params=_params(2),
        name="in_proj",
    )(x2, sc, sh, gain1, w16, w32, col_gain, col_flag)


def _split3(c):
    hi = c.astype(BF16).astype(F32)
    r1 = c - hi
    mid = r1.astype(BF16).astype(F32)
    return hi, mid, (r1 - mid).astype(BF16).astype(F32)


def _cum_kernel(ff_ref, fb_ref, o_ref, *, seq, groups):
    ri = lax.broadcasted_iota(jnp.int32, (LANES, LANES), 0)
    ci = lax.broadcasted_iota(jnp.int32, (LANES, LANES), 1)
    tri = jnp.where(ri >= ci, 1.0, 0.0).astype(BF16)
    carry = jnp.zeros((1, LANES), F32)
    for blk in range(seq // LANES):
        rows = slice(blk * LANES, (blk + 1) * LANES)
        x = ff_ref[0, rows, :] + fb_ref[...]
        lf = jnp.minimum(x, 0.0) - jnp.log(1.0 + jnp.exp(-jnp.abs(x)))
        hi = lf.astype(BF16)
        r1 = lf - hi.astype(F32)
        mid = r1.astype(BF16)
        lo = (r1 - mid.astype(F32)).astype(BF16)
        c = (jnp.dot(tri, hi, preferred_element_type=F32)
             + jnp.dot(tri, mid, preferred_element_type=F32)
             + jnp.dot(tri, lo, preferred_element_type=F32)) + carry
        carry = c[LANES - 1:LANES, :]
        for t, term in enumerate(_split3(c * LOG2E_HI + c * LOG2E_LO)):
            for g in range(groups):
                shift = (-g * (N_FOX_HEADS // groups)) % LANES
                o_ref[0, g, t, rows, :] = term if shift == 0 else pltpu.roll(term, shift, 1)


def _cum_call(o32, fb_pad, batch, seq, groups):
    o32v = o32.reshape(batch, seq, W32)
    return pl.pallas_call(
        functools.partial(_cum_kernel, seq=seq, groups=groups),
        grid=(batch,),
        in_specs=[pl.BlockSpec((1, seq, LANES), lambda b: (b, 0, COL_FF // LANES)),
                  pl.BlockSpec((1, LANES), lambda b: (0, 0))],
        out_specs=pl.BlockSpec((1, groups, 3, seq, LANES), lambda b: (b, 0, 0, 0, 0)),
        out_shape=jax.ShapeDtypeStruct((batch, groups, 3, seq, LANES), F32),
        compiler_params=_params(1),
        name="fox_cumsum",
    )(o32v, fb_pad)


def _fox_kernel(q_ref, k_ref, v_ref, cs_ref, o_ref, m_scr, acc_scr, s_scr, kaug_scr, vaug_scr, *, seq, tq, tk, nh):
    n_diag = tq // tk
    lane_q = lax.broadcasted_iota(jnp.int32, (tq, LANES), 1)
    row8 = lax.broadcasted_iota(jnp.int32, (8, tk), 0)
    ri = lax.broadcasted_iota(jnp.int32, (tq, tk), 0)
    ci = lax.broadcasted_iota(jnp.int32, (tq, tk), 1)
    causal = [ci + d * tk <= ri for d in range(n_diag)]
    for c in range(seq // tk):
        rows = slice(c * tk, (c + 1) * tk)
        terms_t = [cs_ref[0, 0, t, rows, :].T for t in range(3)]
        for j in range(nh):
            tail8 = jnp.where(row8 < 3, 1.0, jnp.where(row8 == 3, -terms_t[0][j:j + 1], jnp.where(
                row8 == 4, -terms_t[1][j:j + 1], jnp.where(row8 == 5, -terms_t[2][j:j + 1], 0.0))))
            kaug_scr[j, c, 0:HEAD_DIM, :] = k_ref[0, rows, j * HEAD_DIM:(j + 1) * HEAD_DIM].T
            kaug_scr[j, c, HEAD_DIM:, :] = jnp.concatenate(
                [tail8, jnp.zeros((HEAD_DIM - 8, tk), F32)], axis=0).astype(BF16)
    for j in range(nh):
        vaug_scr[j, :, 0:HEAD_DIM] = v_ref[0, :, j * HEAD_DIM:(j + 1) * HEAD_DIM]
        vaug_scr[j, :, HEAD_DIM:] = jnp.ones((seq, HEAD_DIM), BF16)
    acc_scr[...] = jnp.zeros(acc_scr.shape, F32)

    def q_body(qi, carry):
        q0 = pl.multiple_of(qi * tq, tq)
        terms = [cs_ref[0, 0, t, pl.ds(q0, tq), :] for t in range(3)]
        qs = []
        for j in range(nh):
            hi, mid, lo = [term if t == j else pltpu.roll(term, (t - j) % LANES, 1) for t, term in enumerate(terms)]
            tail = jnp.where(lane_q == 0, hi, jnp.where(lane_q == 1, mid, jnp.where(lane_q == 2, lo,
                             jnp.where(lane_q < 6, 1.0, 0.0))))
            qs.append(jnp.concatenate([q_ref[0, pl.ds(q0, tq), j * HEAD_DIM:(j + 1) * HEAD_DIM],
                                       tail.astype(BF16)], axis=1))
            m_scr[j] = jnp.full((tq, LANES), NEG, F32)

        all_rows = slice(0, tq)

        def scores(j, ki, rows):
            return jnp.dot(qs[j][rows], kaug_scr[j, ki], preferred_element_type=F32)

        def tile(ki, mask, rows, next_rows):
            k0 = pl.multiple_of(ki * tk, tk)
            for j in range(nh):
                s = s_scr[j, rows, :]
                if mask is not None:
                    s = jnp.where(mask[rows], s, NEG)
                if next_rows is not None:
                    s_scr[j, next_rows, :] = scores(j, ki + 1, next_rows)
                m_prev = m_scr[j, rows, :]
                m_new = jnp.maximum(m_prev, jnp.max(s, axis=1, keepdims=True))
                alpha = jnp.exp2(m_prev - m_new)
                p = jnp.exp2(s - _lane_tile(m_new, tk // LANES))
                pv = jnp.dot(p.astype(BF16), vaug_scr[j, pl.ds(k0, tk), :], preferred_element_type=F32)
                acc_scr[j, rows, :] = _lane_tile(alpha, 2) * acc_scr[j, rows, :] + pv
                m_scr[j, rows, :] = m_new

        def k_body(ki, c2):
            tile(ki, None, all_rows, all_rows)
            return c2

        for j in range(nh):
            s_scr[j] = scores(j, 0, all_rows)
        n_off = qi * n_diag
        lax.fori_loop(0, n_off, k_body, 0)
        for d in range(n_diag):
            nxt = slice((d + 1) * tk, tq) if d + 1 < n_diag else None
            tile(n_off + d, causal[d], slice(d * tk, tq), nxt)
        for j in range(nh):
            acc = acc_scr[j]
            o_ref[0, pl.ds(q0, tq), j * HEAD_DIM:(j + 1) * HEAD_DIM] = acc[:, 0:HEAD_DIM] / acc[:, HEAD_DIM:]
        return carry

    lax.fori_loop(0, seq // tq, q_body, 0)


def _fox_call(o16v, cs, batch, seq):
    nh = FOX_HEADS_PER_STEP
    tq, tk = min(512, seq), min(256, seq)
    groups = N_FOX_HEADS // nh
    w = nh * HEAD_DIM
    return pl.pallas_call(
        functools.partial(_fox_kernel, seq=seq, tq=tq, tk=tk, nh=nh),
        grid=(batch, groups),
        in_specs=[pl.BlockSpec((1, seq, w), lambda b, h: (b, 0, h)),
                  pl.BlockSpec((1, seq, w), lambda b, h: (b, 0, groups + h)),
                  pl.BlockSpec((1, seq, w), lambda b, h: (b, 0, 2 * groups + h)),
                  pl.BlockSpec((1, 1, 3, seq, LANES), lambda b, h: (b, h, 0, 0, 0))],
        out_specs=pl.BlockSpec((1, seq, w), lambda b, h: (b, 0, h)),
        out_shape=jax.ShapeDtypeStruct((batch, seq, FOX_WIDTH), F32),
        scratch_shapes=[pltpu.VMEM((nh, tq, LANES), F32),
                        pltpu.VMEM((nh, tq, 2 * HEAD_DIM), F32),
                        pltpu.VMEM((nh, tq, tk), F32),
                        pltpu.VMEM((nh, seq // tk, 2 * HEAD_DIM, tk), BF16),
                        pltpu.VMEM((nh, seq, 2 * HEAD_DIM), BF16)],
        compiler_params=_params(2),
        name="fox_attention",
    )(o16v, o16v, o16v, cs)


def _cmp_kernel(x_ref, pos_ref, w1_ref, w2_ref, gain_ref, o_ref, xs_scr, *, seq, ncp):
    rows = xs_scr.shape[1]
    for kv in range(2):
        for h in range(N_NSA_KV_HEADS):
            n = kv * N_NSA_KV_HEADS + h
            xs_scr[n, 0:seq, :] = x_ref[0, :, n * HEAD_DIM:(n + 1) * HEAD_DIM]
            xs_scr[n, seq:rows, :] = jnp.zeros((rows - seq, LANES), F32)
            acc = jnp.zeros((ncp, HEAD_DIM), F32)
            for l in range(CMP_BLOCK):
                xl = xs_scr[n, pl.ds(l, ncp, stride=CMP_STRIDE), :] + pos_ref[kv, l:l + 1, :]
                acc = acc + jnp.dot(xl.astype(BF16), w1_ref[kv, l], preferred_element_type=F32)
            hmid = acc * _sigmoid(acc)
            y = jnp.dot(hmid.astype(BF16), w2_ref[kv], preferred_element_type=F32)
            if kv == 0:
                y = _rms(y) * gain_ref[...]
            o_ref[0, kv, h] = y.astype(BF16)


def _cmp_call(o32, pos, w1, w2, gain, batch, seq, ncp):
    o32v = o32.reshape(batch, seq, W32)
    hkv = N_NSA_KV_HEADS
    width = 2 * NSA_KV_WIDTH
    return pl.pallas_call(
        functools.partial(_cmp_kernel, seq=seq, ncp=ncp),
        grid=(batch,),
        in_specs=[pl.BlockSpec((1, seq, width), lambda b: (b, 0, 0)),
                  pl.BlockSpec((2, CMP_BLOCK, HEAD_DIM), lambda b: (0, 0, 0)),
                  pl.BlockSpec((2, CMP_BLOCK, HEAD_DIM, HEAD_DIM), lambda b: (0, 0, 0, 0)),
                  pl.BlockSpec((2, HEAD_DIM, HEAD_DIM), lambda b: (0, 0, 0)),
                  pl.BlockSpec((1, HEAD_DIM), lambda b: (0, 0))],
        out_specs=pl.BlockSpec((1, 2, hkv, ncp, HEAD_DIM), lambda b: (b, 0, 0, 0, 0)),
        out_shape=jax.ShapeDtypeStruct((batch, 2, hkv, ncp, HEAD_DIM), BF16),
        scratch_shapes=[pltpu.VMEM((2 * hkv, CMP_STRIDE * ncp + CMP_BLOCK, LANES), F32)],
        compiler_params=_params(1),
        name="nsa_compress",
    )(o32v, pos, w1, w2, gain)


def _bias_kernel(rb_ref, o_ref, *, width, key_stride, key_offset, first_tile, rolled_tiles):
    v = pl.program_id(0) + first_tile
    i = lax.broadcasted_iota(jnp.int32, (LANES, width), 0)
    j = lax.broadcasted_iota(jnp.int32, (LANES, width), 1)
    d = v * LANES + i - (key_stride * j + key_offset)
    n = jnp.maximum(d, 0)
    max_exact = N_BUCKETS // 2
    nf = jnp.maximum(n, 1).astype(F32)
    large = max_exact + jnp.trunc(jnp.log(nf / max_exact) / math.log(MAX_DISTANCE / max_exact)
                                  * (N_BUCKETS - max_exact))
    large = jnp.minimum(large, float(N_BUCKETS - 1))
    bkt = jnp.where(n < max_exact, n.astype(F32), large)
    vals = [jnp.zeros((LANES, width), F32) for _ in range(N_NSA_HEADS)]
    for bk in range(N_BUCKETS):
        hit = bkt == float(bk)
        for h in range(N_NSA_HEADS):
            vals[h] = jnp.where(hit, rb_ref[bk * N_NSA_HEADS + h] * LOG2E, vals[h])
    for h in range(N_NSA_HEADS):
        g = h % NSA_GROUP
        rows = slice(g * LANES, (g + 1) * LANES)
        if rolled_tiles:
            per_tile = LANES // key_stride
            for t in range(rolled_tiles):
                shift = (width - per_tile * (rolled_tiles - 1 - t)) % width
                o_ref[t, h // NSA_GROUP, rows, :] = vals[h] if shift == 0 else pltpu.roll(vals[h], shift, 1)
        else:
            o_ref[0, h // NSA_GROUP, rows, :] = vals[h]


def _bias_call(rb_flat, n_tiles, width, key_stride, key_offset, name, rolled=False):
    kern = functools.partial(_bias_kernel, width=width, key_stride=key_stride, key_offset=key_offset,
                             first_tile=n_tiles - 1 if rolled else 0, rolled_tiles=n_tiles if rolled else 0)
    block = (n_tiles if rolled else 1, N_NSA_KV_HEADS, GROUP_ROWS, width)
    return pl.pallas_call(
        kern,
        grid=(1 if rolled else n_tiles,),
        in_specs=[pl.BlockSpec(memory_space=pltpu.SMEM)],
        out_specs=pl.BlockSpec(block, lambda v: (v, 0, 0, 0)),
        out_shape=jax.ShapeDtypeStruct((n_tiles, N_NSA_KV_HEADS, GROUP_ROWS, width), F32),
        compiler_params=_params(1),
        name=name,
    )(rb_flat)


def _tile4(a):
    return jnp.concatenate([a] * NSA_GROUP, axis=0)


def _nsa_kernel(q_ref, ks_ref, kw_ref, vs_ref, vw_ref, g_ref, kcvc_ref, bc_ref, tz_ref, selmt_ref,
                wa_ref, wb_ref, wc_ref, wd_ref, o_ref, wa_out, wb_out, wc_out, wd_out,
                qg_scr, qw_scr, caug_scr, near_scr, edge_scr, m_scr, acc_scr, o_scr, s_scr,
                ksa_scr, vsa_scr, kwp_scr, vwa_scr, *, seq, ncp):
    qt = pl.program_id(1)
    q0 = qt * LANES
    n_slc = seq // SEL_BLOCK
    top_n = min(N_SEL, n_slc)
    nkv = N_NSA_KV_HEADS
    for src, dst in ((wa_ref, wa_out), (wb_ref, wb_out), (wc_ref, wc_out), (wd_ref, wd_out)):
        dst[...] = src[...].astype(BF16)
    ri = lax.broadcasted_iota(jnp.int32, (LANES, LANES), 0)
    ci = lax.broadcasted_iota(jnp.int32, (LANES, LANES), 1)
    eye = jnp.where(ri == ci, 1.0, 0.0).astype(BF16)
    gates = _sigmoid(g_ref[0])

    def gate_col(hk, br):
        cols = []
        for g in range(NSA_GROUP):
            c = COL_NG % LANES + (hk * NSA_GROUP + g) * N_BRANCH + br
            cols.append(gates[:, c:c + 1])
        return jnp.concatenate(cols, axis=0)

    @pl.when(qt == 0)
    def _():
        ones = jnp.ones((seq, HEAD_DIM), BF16)
        row = lax.broadcasted_iota(jnp.int32, (LANES, LANES), 0)
        is_far = jnp.logical_and(row >= AUG_FAR, row < AUG_FAR + 3)
        pad_aug = jnp.where(row == AUG_PAD, -MASK_BIG, jnp.where(is_far, 1.0, 0.0)).astype(BF16)
        win_aug = jnp.where(is_far, 1.0, 0.0).astype(BF16)
        lane512 = lax.broadcasted_iota(jnp.int32, (GROUP_ROWS, LANES), 1)
        causal = _tile4(jnp.where(ci <= ri, 0.0, -MASK_BIG))
        acc_scr[...] = jnp.zeros(acc_scr.shape, F32)
        edge_scr[...] = jnp.concatenate([jnp.full((GROUP_ROWS, LANES), -MASK_BIG, F32),
                                         _tile4(jnp.where(ri < ci, 0.0, -MASK_BIG))], axis=1)
        for hk in range(nkv):
            hc = slice(hk * HEAD_DIM, (hk + 1) * HEAD_DIM)
            far = tz_ref[2, hk]
            hi, mid, lo = _split3(far)
            caug = jnp.where(lane512 == AUG_FAR, hi, jnp.where(lane512 == AUG_FAR + 1, mid,
                             jnp.where(lane512 == AUG_FAR + 2, lo, jnp.where(lane512 == AUG_PAD, 1.0, 0.0))))
            caug_scr[hk] = caug.astype(BF16)
            qw_scr[hk, :, HEAD_DIM:] = caug.astype(BF16)
            near_scr[hk] = jnp.concatenate([tz_ref[1, hk] - far, (tz_ref[0, hk] - far) + causal], axis=1)
            ksa_scr[hk, 0, 0:HEAD_DIM, :] = jnp.zeros((HEAD_DIM, LANES), BF16)
            ksa_scr[hk, 0, HEAD_DIM:, :] = pad_aug
            for t in range(seq // LANES):
                rows = slice(t * LANES, (t + 1) * LANES)
                ksa_scr[hk, SEL_PAD_TILES + t, 0:HEAD_DIM, :] = ks_ref[0, rows, hc].T
                ksa_scr[hk, SEL_PAD_TILES + t, HEAD_DIM:, :] = jnp.where(
                    (t * LANES + ci) // SEL_BLOCK == ri, -MASK_BIG, jnp.where(is_far, 1.0, 0.0)).astype(BF16)
                kwp_scr[hk, WIN_PAD // LANES + t, 0:HEAD_DIM, :] = kw_ref[0, rows, hc].T
                kwp_scr[hk, WIN_PAD // LANES + t, HEAD_DIM:, :] = win_aug
            for t in range(WIN_PAD // LANES):
                kwp_scr[hk, t, 0:HEAD_DIM, :] = jnp.zeros((HEAD_DIM, LANES), BF16)
                kwp_scr[hk, t, HEAD_DIM:, :] = pad_aug
            vsa_scr[hk, 0:SEL_PAD_TILES * LANES, :] = jnp.zeros((SEL_PAD_TILES * LANES, 2 * HEAD_DIM), BF16)
            vsa_scr[hk, SEL_PAD_TILES * LANES:, 0:HEAD_DIM] = vs_ref[0, :, hc]
            vsa_scr[hk, SEL_PAD_TILES * LANES:, HEAD_DIM:] = ones
            vwa_scr[hk, 0:WIN_PAD, :] = jnp.zeros((WIN_PAD, 2 * HEAD_DIM), BF16)
            vwa_scr[hk, WIN_PAD:, 0:HEAD_DIM] = vw_ref[0, :, hc]
            vwa_scr[hk, WIN_PAD:, HEAD_DIM:] = ones

    def reset(br):
        m_scr[br] = jnp.full(m_scr.shape[1:], NEG, F32)

    def online_update(br, hk, s, vaug):
        m_prev = m_scr[br, hk]
        m_new = jnp.maximum(m_prev, jnp.max(s, axis=1, keepdims=True))
        alpha = jnp.exp2(m_prev - m_new)
        p = jnp.exp2(s - _lane_tile(m_new, s.shape[1] // LANES))
        pv = jnp.dot(p.astype(BF16), vaug, preferred_element_type=F32)
        acc_scr[br, hk] = _lane_tile(alpha, 2) * acc_scr[br, hk] + pv
        m_scr[br, hk] = m_new

    def finish(br, hk):
        acc = acc_scr[br, hk]
        return acc[:, 0:HEAD_DIM] / acc[:, HEAD_DIM:]

    for hk in range(nkv):
        for g in range(NSA_GROUP):
            h = hk * NSA_GROUP + g
            q_h = q_ref[0, :, h * HEAD_DIM:(h + 1) * HEAD_DIM]
            qg_scr[hk, g * LANES:(g + 1) * LANES, 0:HEAD_DIM] = q_h
            qw_scr[hk, g * LANES:(g + 1) * LANES, 0:HEAD_DIM] = q_h

    for hk in range(nkv):
        qg = qg_scr[hk, :, 0:HEAD_DIM]

        kc = kcvc_ref[0, 0, hk]
        vc = kcvc_ref[0, 1, hk]
        s = lax.dot_general(qg, kc, NT_DIMS, preferred_element_type=F32) + bc_ref[0, hk]
        rc = lax.broadcasted_iota(jnp.int32, (LANES, ncp), 0)
        cc = lax.broadcasted_iota(jnp.int32, (LANES, ncp), 1)
        valid_c = _tile4(jnp.where(q0 + rc - (CMP_STRIDE * cc + CMP_BLOCK - 1) >= 0, 1.0, 0.0)) > 0.5
        s = jnp.where(valid_c, s, NEG)
        p = jnp.where(valid_c, jnp.exp2(s - jnp.max(s, axis=1, keepdims=True)), 0.0)
        l = jnp.sum(p, axis=1, keepdims=True)
        p = p / jnp.where(l > 0.0, l, 1.0)
        o_scr[hk] = gate_col(hk, 0) * jnp.dot(p.astype(BF16), vc, preferred_element_type=F32)

        psum = p[0:LANES]
        for g in range(1, NSA_GROUP):
            psum = psum + p[g * LANES:(g + 1) * LANES]
        p_hi = psum.astype(BF16)
        p_lo = (psum - p_hi.astype(F32)).astype(BF16)
        selmt = selmt_ref[...]
        imp = (lax.dot_general(selmt, p_hi, NT_DIMS, preferred_element_type=F32)
               + lax.dot_general(selmt, p_lo, NT_DIMS, preferred_element_type=F32))
        imp = imp[0:n_slc]
        blk = lax.broadcasted_iota(jnp.int32, (n_slc, LANES), 0)
        cur = (q0 + lax.broadcasted_iota(jnp.int32, (n_slc, LANES), 1)) // SEL_BLOCK
        forced = (blk == 0) | (blk == cur) | (blk == cur - 1)
        imp = jnp.where(forced, FORCE, imp)
        imp = jnp.where(blk <= cur, imp, -jnp.inf)
        rank = jnp.zeros((n_slc, LANES), F32)
        for j in range(n_slc):
            row = imp[j:j + 1, :]
            beats = jnp.where(row > imp, 1.0, jnp.where(row == imp, jnp.where(blk > j, 1.0, 0.0), 0.0))
            rank = rank + beats
        sel_t = jnp.where(rank < top_n, jnp.where(imp > -jnp.inf, 1.0, 0.0), 0.0)
        if n_slc < LANES:
            sel_t = jnp.concatenate([sel_t, jnp.zeros((LANES - n_slc, LANES), F32)], axis=0)
        sel_q = lax.dot_general(eye, sel_t.astype(BF16), NT_DIMS, preferred_element_type=F32)
        not_sel = jnp.where(ci < n_slc, 1.0 - sel_q, 0.0).astype(BF16)
        for g in range(NSA_GROUP):
            rows = slice(g * LANES, (g + 1) * LANES)
            qg_scr[hk, rows, HEAD_DIM:] = jnp.where(ci < n_slc, not_sel, caug_scr[hk, rows, :])

    reset(WIN)
    for off, table in ((1, near_scr), (3, None), (5, edge_scr)):
        t0 = qt - off + WIN_PAD // LANES
        p0 = pl.multiple_of(t0 * LANES, LANES)
        for hk in range(nkv):
            k_t = jnp.concatenate([kwp_scr[hk, t0], kwp_scr[hk, t0 + 1]], axis=1)
            s = jnp.dot(qw_scr[hk], k_t, preferred_element_type=F32)
            if table is not None:
                s = s + (table[hk] if table is near_scr else table[...])
            online_update(WIN, hk, s, vwa_scr[hk, pl.ds(p0, KEY_CHUNK), :])

    reset(SEL)
    n_chunks = (qt + 2) // 2
    first_tile = SEL_PAD_TILES - (qt + 1) % 2

    def sel_scores(hk, c):
        t0 = first_tile + 2 * c
        k_t = jnp.concatenate([ksa_scr[hk, t0], ksa_scr[hk, t0 + 1]], axis=1)
        return jnp.dot(qg_scr[hk], k_t, preferred_element_type=F32)

    def sel_chunk(c, last):
        p0 = pl.multiple_of((first_tile + 2 * c) * LANES, LANES)
        for hk in range(nkv):
            s = s_scr[hk]
            if last:
                s = s + near_scr[hk]
            else:
                s_scr[hk] = sel_scores(hk, c + 1)
            online_update(SEL, hk, s, vsa_scr[hk, pl.ds(p0, KEY_CHUNK), :])

    def sel_pair(i, carry):
        sel_chunk(2 * i, False)
        sel_chunk(2 * i + 1, False)
        return carry

    for hk in range(nkv):
        s_scr[hk] = sel_scores(hk, 0)
    n_far = n_chunks - 1
    lax.fori_loop(0, n_far // 2, sel_pair, 0)
    pl.when(n_far % 2 == 1)(lambda: sel_chunk(n_far - 1, False))
    sel_chunk(n_chunks - 1, True)
    for hk in range(nkv):
        o = (o_scr[hk] + gate_col(hk, 1) * finish(SEL, hk)) + gate_col(hk, 2) * finish(WIN, hk)
        for g in range(NSA_GROUP):
            h = hk * NSA_GROUP + g
            o_ref[0, :, h * HEAD_DIM:(h + 1) * HEAD_DIM] = o[g * LANES:(g + 1) * LANES]


def _nsa_call(o16v, o32v, kcvc, bias_c, tz, selmt, weights_f32, batch, seq, ncp):
    nqt = seq // LANES
    wspecs = []
    for w in weights_f32:
        blk, span = _cast_block(w, batch * nqt)
        wspecs.append(pl.BlockSpec(blk, functools.partial(lambda b, t, span: ((b * nqt + t) // span, 0), span=span)))
    kvw = NSA_KV_WIDTH
    nkv = N_NSA_KV_HEADS
    base = (3 * FOX_WIDTH + NSA_WIDTH) // kvw
    return pl.pallas_call(
        functools.partial(_nsa_kernel, seq=seq, ncp=ncp),
        grid=(batch, nqt),
        in_specs=[pl.BlockSpec((1, LANES, NSA_WIDTH), lambda b, t: (b, t, 3 * FOX_WIDTH // NSA_WIDTH)),
                  pl.BlockSpec((1, seq, kvw), lambda b, t: (b, 0, base)),
                  pl.BlockSpec((1, seq, kvw), lambda b, t: (b, 0, base + 1)),
                  pl.BlockSpec((1, seq, kvw), lambda b, t: (b, 0, base + 2)),
                  pl.BlockSpec((1, seq, kvw), lambda b, t: (b, 0, base + 3)),
                  pl.BlockSpec((1, LANES, LANES), lambda b, t: (b, t, COL_NG // LANES)),
                  pl.BlockSpec((1, 2, N_NSA_KV_HEADS, ncp, HEAD_DIM), lambda b, t: (b, 0, 0, 0, 0)),
                  pl.BlockSpec((1, N_NSA_KV_HEADS, GROUP_ROWS, ncp), lambda b, t: (t, 0, 0, 0)),
                  pl.BlockSpec((3, N_NSA_KV_HEADS, GROUP_ROWS, LANES), lambda b, t: (0, 0, 0, 0)),
                  pl.BlockSpec((LANES, ncp), lambda b, t: (0, 0))] + wspecs,
        out_specs=[pl.BlockSpec((1, LANES, NSA_WIDTH), lambda b, t: (b, t, 0))] + wspecs,
        out_shape=[jax.ShapeDtypeStruct((batch, seq, NSA_WIDTH), F32)]
        + [jax.ShapeDtypeStruct(w.shape, BF16) for w in weights_f32],
        scratch_shapes=[pltpu.VMEM((nkv, GROUP_ROWS, 2 * HEAD_DIM), BF16),
                        pltpu.VMEM((nkv, GROUP_ROWS, 2 * HEAD_DIM), BF16),
                        pltpu.VMEM((nkv, GROUP_ROWS, LANES), BF16),
                        pltpu.VMEM((nkv, GROUP_ROWS, KEY_CHUNK), F32),
                        pltpu.VMEM((GROUP_ROWS, KEY_CHUNK), F32),
                        pltpu.VMEM((2, nkv, GROUP_ROWS, LANES), F32),
                        pltpu.VMEM((2, nkv, GROUP_ROWS, 2 * HEAD_DIM), F32),
                        pltpu.VMEM((nkv, GROUP_ROWS, HEAD_DIM), F32),
                        pltpu.VMEM((nkv, GROUP_ROWS, KEY_CHUNK), F32),
                        pltpu.VMEM((nkv, SEL_PAD_TILES + seq // LANES, 2 * HEAD_DIM, LANES), BF16),
                        pltpu.VMEM((nkv, SEL_PAD_TILES * LANES + seq, 2 * HEAD_DIM), BF16),
                        pltpu.VMEM((nkv, (seq + WIN_PAD) // LANES, 2 * HEAD_DIM, LANES), BF16),
                        pltpu.VMEM((nkv, seq + WIN_PAD, 2 * HEAD_DIM), BF16)],
        compiler_params=_params(2),
        name="nsa_attention",
    )(o16v, o16v, o16v, o16v, o16v, o32v, kcvc, bias_c, tz, selmt, *weights_f32)


def _out_kernel(of_ref, on_ref, gain_ref, w_ref, x_ref, g_ref, o_ref, y_scr, *, tm, n_chains):
    rows_per_chain = tm // n_chains
    gain_f = gain_ref[:, 0:FOX_WIDTH]
    gain_n = gain_ref[:, FOX_WIDTH:MIX_WIDTH]
    for c in range(n_chains):
        rows = slice(c * rows_per_chain, (c + 1) * rows_per_chain)
        for r0 in range(c * rows_per_chain, (c + 1) * rows_per_chain, NORM_ROWS):
            r = slice(r0, r0 + NORM_ROWS)
            y_scr[r, 0:FOX_WIDTH] = (_rms(of_ref[r, :]) * gain_f).astype(BF16)
            y_scr[r, FOX_WIDTH:MIX_WIDTH] = (_rms(on_ref[r, :]) * gain_n).astype(BF16)
        acc = jnp.dot(y_scr[rows, :], w_ref[...], preferred_element_type=F32)
        o_ref[rows, :] = x_ref[rows, :] + g_ref[0] * acc


def _out_call(o_fox, o_nsa, gain, w_out, x2, g1, seq):
    t, d = x2.shape
    tm = min(512, seq)
    rows_per_batch = seq // tm
    return pl.pallas_call(
        functools.partial(_out_kernel, tm=tm, n_chains=2),
        grid=(t // tm,),
        in_specs=[pl.BlockSpec((tm, FOX_WIDTH), lambda i: (i, 0)),
                  pl.BlockSpec((tm, NSA_WIDTH), lambda i: (i, 0)),
                  pl.BlockSpec((1, MIX_WIDTH), lambda i: (0, 0)),
                  pl.BlockSpec((MIX_WIDTH, d), lambda i: (0, 0)),
                  pl.BlockSpec((tm, d), lambda i: (i, 0)),
                  pl.BlockSpec((1, 1, d), lambda i: (i // rows_per_batch, 0, 0))],
        out_specs=pl.BlockSpec((tm, d), lambda i: (i, 0)),
        out_shape=jax.ShapeDtypeStruct((t, d), F32),
        scratch_shapes=[pltpu.VMEM((tm, MIX_WIDTH), BF16)],
        compiler_params=_params(1),
        name="out_proj",
    )(o_fox, o_nsa, gain, w_out, x2, g1)


def _ffn_kernel(x_ref, gain_ref, sc_ref, sh_ref, g_ref, wg_ref, wu_ref, wd_ref, o_ref, h_scr):
    f = pl.program_id(1)
    tm = h_scr.shape[0]

    def hidden_step(first_row, n_rows, first):
        rows = slice(first_row, first_row + n_rows)
        h = h_scr[rows, :]
        a = jnp.dot(h, wg_ref[...], preferred_element_type=F32)
        u = jnp.dot(h, wu_ref[...], preferred_element_type=F32)
        t = (a * _sigmoid(a)) * u
        part = jnp.dot(t.astype(BF16), wd_ref[...], preferred_element_type=F32)
        o_ref[rows, :] = part if first else o_ref[rows, :] + part

    @pl.when(f == 0)
    def _():
        n_rows = tm // ROW_CHAINS
        for c in range(ROW_CHAINS):
            _modulated_norm(x_ref, gain_ref, sc_ref, sh_ref, h_scr, c * n_rows, n_rows)
            hidden_step(c * n_rows, n_rows, True)

    pl.when(f > 0)(functools.partial(hidden_step, 0, tm, False))

    @pl.when(f == pl.num_programs(1) - 1)
    def _():
        o_ref[...] = x_ref[...] + g_ref[0] * o_ref[...]


def _ffn_call(x1, gain2, sc, sh, g2, wg, wu, wd, seq):
    t, d = x1.shape
    dff = wg.shape[1]
    tm = min(1024, seq)
    tf = 512 if dff % 512 == 0 else dff
    rows_per_batch = seq // tm
    return pl.pallas_call(
        _ffn_kernel,
        grid=(t // tm, dff // tf),
        in_specs=[pl.BlockSpec((tm, d), lambda i, f: (i, 0)),
                  pl.BlockSpec((1, d), lambda i, f: (0, 0)),
                  pl.BlockSpec((1, 1, d), lambda i, f: (i // rows_per_batch, 0, 0)),
                  pl.BlockSpec((1, 1, d), lambda i, f: (i // rows_per_batch, 0, 0)),
                  pl.BlockSpec((1, 1, d), lambda i, f: (i // rows_per_batch, 0, 0)),
                  pl.BlockSpec((d, tf), lambda i, f: (0, f)),
                  pl.BlockSpec((d, tf), lambda i, f: (0, f)),
                  pl.BlockSpec((tf, d), lambda i, f: (f, 0))],
        out_specs=pl.BlockSpec((tm, d), lambda i, f: (i, 0)),
        out_shape=jax.ShapeDtypeStruct((t, d), F32),
        scratch_shapes=[pltpu.VMEM((tm, d), BF16)],
        compiler_params=_params(2),
        name="swiglu_ffn",
    )(x1, gain2, sc, sh, g2, wg, wu, wd)


def _selection_matrix_t(ncp, n_slc):
    r, q = SEL_BLOCK // CMP_STRIDE, CMP_BLOCK // CMP_STRIDE
    m = np.zeros((LANES, ncp), np.float32)
    for j in range(n_slc):
        for a in range(r):
            for b in range(q):
                c = r * j + a - b
                if 0 <= c < ncp:
                    m[j, c] += 1.0
    return m


def _w_in_block_sources():
    o = 0
    start = {}
    for name, width in (("fq", FOX_WIDTH), ("fk", FOX_WIDTH), ("fv", FOX_WIDTH), ("ff", N_FOX_HEADS),
                        ("nq", NSA_WIDTH), ("nk", N_BRANCH * NSA_KV_WIDTH), ("nv", N_BRANCH * NSA_KV_WIDTH),
                        ("ng", N_BRANCH * N_NSA_HEADS)):
        start[name] = o
        o += width
    kvw = NSA_KV_WIDTH
    groups = [(start["fq"], 3 * FOX_WIDTH), (start["nq"], NSA_WIDTH),
              (start["nk"] + kvw, 2 * kvw), (start["nv"] + kvw, 2 * kvw),
              (start["nk"], kvw), (start["nv"], kvw)]
    blocks = [s + LANES * b for s, width in groups for b in range(width // LANES)]
    return blocks, (start["ff"], N_FOX_HEADS), (start["ng"], N_BRANCH * N_NSA_HEADS)


def _repack_kernel(src_ref, wt_hbm, o16_ref, o32_ref, buf, sem, *, n16, n_whole, ff, ng):
    k = pl.program_id(0)
    n_slots = buf.shape[0]
    ahead = n_slots - 1
    slot = k % n_slots

    def whole_copy(kk, s):
        r0 = pl.multiple_of(src_ref[kk], 8)
        return pltpu.make_async_copy(wt_hbm.at[pl.ds(r0, LANES), :], buf.at[s], sem.at[s])

    def narrow_copies(s):
        return [pltpu.make_async_copy(wt_hbm.at[pl.ds(ff[0], ff[1]), :], buf.at[s, pl.ds(0, ff[1]), :], sem.at[s]),
                pltpu.make_async_copy(wt_hbm.at[pl.ds(ng[0], ng[1]), :], buf.at[s, pl.ds(ff[1], ng[1]), :],
                                      sem.at[s])]

    def start_block(kk):
        @pl.when(kk < n_whole)
        def _():
            whole_copy(kk, kk % n_slots).start()

        @pl.when(kk == n_whole)
        def _():
            for cp in narrow_copies(kk % n_slots):
                cp.start()

    @pl.when(k == 0)
    def _():
        for kk in range(ahead):
            start_block(kk)

    start_block(k + ahead)

    @pl.when(k < n_whole)
    def _():
        whole_copy(k, slot).wait()

    @pl.when(k == n_whole)
    def _():
        for cp in narrow_copies(slot):
            cp.wait()

    xt = buf[slot].T
    lane = lax.broadcasted_iota(jnp.int32, xt.shape, 1)
    xt = jnp.where(jnp.logical_or(k < n_whole, lane < ff[1] + ng[1]), xt, 0.0).astype(BF16)

    @pl.when(k < n16)
    def _():
        o16_ref[...] = xt

    @pl.when(k >= n16)
    def _():
        o32_ref[...] = xt


def _repack_w_in(wt):
    n, d = wt.shape
    blocks, ff, ng = _w_in_block_sources()
    n16 = W16 // LANES
    n_whole = len(blocks)
    assert n_whole + 1 == (W16 + W32) // LANES
    return pl.pallas_call(
        functools.partial(_repack_kernel, n16=n16, n_whole=n_whole, ff=ff, ng=ng),
        grid_spec=pltpu.PrefetchScalarGridSpec(
            num_scalar_prefetch=1,
            grid=(n_whole + 1,),
            in_specs=[pl.BlockSpec(memory_space=pl.ANY)],
            out_specs=[pl.BlockSpec((d, LANES), lambda k, src: (0, jnp.minimum(k, n16 - 1))),
                       pl.BlockSpec((d, LANES), lambda k, src: (0, jnp.maximum(k - n16, 0)))],
            scratch_shapes=[pltpu.VMEM((REPACK_SLOTS, LANES, d), F32), pltpu.SemaphoreType.DMA((REPACK_SLOTS,))]),
        out_shape=[jax.ShapeDtypeStruct((d, W16), BF16), jax.ShapeDtypeStruct((d, W32), BF16)],
        compiler_params=_params(1),
        name="w_in_repack",
    )(jnp.asarray(blocks, jnp.int32), wt)


def kernel(x, c, ada_w, ada_b, norm1_gain, norm2_gain, w_in, fox_f_bias, fox_q_gain, fox_k_gain, nsa_q_gain,
           nsa_k_gain, nsa_cmp_pos, nsa_cmp_w1, nsa_cmp_w2, rel_bias, mix_out_gain, w_out, ffn_w_gate, ffn_w_up,
           ffn_w_down):
    batch, seq, d = x.shape
    assert seq % KEY_CHUNK == 0 and seq >= WINDOW and d % LANES == 0 and seq // SEL_BLOCK <= LANES
    depth = ada_w.shape[0]
    nqt = seq // LANES
    ncp = -(-(seq // CMP_STRIDE) // LANES) * LANES

    selmt = jnp.asarray(_selection_matrix_t(ncp, seq // SEL_BLOCK), BF16)
    rb_flat = rel_bias.reshape(-1)
    bias_c = _bias_call(rb_flat, nqt, ncp, CMP_STRIDE, CMP_BLOCK - 1, "t5_bias_compressed", rolled=True)
    tz = _bias_call(rb_flat, 3, LANES, 1, 0, "t5_bias_toeplitz")

    ones_h = jnp.ones((HEAD_DIM,), F32)
    c_pad = jnp.pad(c, ((0, 8 - batch % 8 if batch % 8 else 0), (0, 0)))
    x2 = x.reshape(batch * seq, d)
    for layer in range(depth):
        mod = _ada_call(c_pad, ada_w[layer], ada_b[layer][None, :])[:batch]
        sh1, sc1, g1, sh2, sc2, g2 = [mod[:, i * d:(i + 1) * d][:, None, :] for i in range(N_MOD)]

        kg = nsa_k_gain[layer]
        col_gain = jnp.concatenate([
            jnp.tile(fox_q_gain[layer] * QSCALE, N_FOX_HEADS), jnp.tile(fox_k_gain[layer], N_FOX_HEADS),
            jnp.tile(ones_h, N_FOX_HEADS), jnp.tile(nsa_q_gain[layer] * QSCALE, N_NSA_HEADS),
            jnp.tile(kg[1], N_NSA_KV_HEADS), jnp.tile(kg[2], N_NSA_KV_HEADS),
            jnp.tile(ones_h, 2 * N_NSA_KV_HEADS)])[None, :]
        col_flag = jnp.concatenate([
            jnp.ones((2 * FOX_WIDTH,), F32), jnp.zeros((FOX_WIDTH,), F32), jnp.ones((NSA_WIDTH,), F32),
            jnp.ones((2 * NSA_KV_WIDTH,), F32), jnp.zeros((2 * NSA_KV_WIDTH,), F32)])[None, :]
        w16, w32 = _repack_w_in(jnp.swapaxes(w_in, 1, 2)[layer])
        o16, o32 = _proj_call(x2, sc1, sh1, norm1_gain[layer][None, :], w16, w32,
                              col_gain, col_flag, seq)
        o16v = o16.reshape(batch, seq, W16)
        o32v = o32.reshape(batch, seq, W32)

        fb_pad = jnp.pad(fox_f_bias[layer], (0, LANES - N_FOX_HEADS))[None, :]
        cs = _cum_call(o32, fb_pad, batch, seq, N_FOX_HEADS // FOX_HEADS_PER_STEP)
        o_fox = _fox_call(o16v, cs, batch, seq)

        w1 = nsa_cmp_w1[layer].reshape(2, CMP_BLOCK, HEAD_DIM, HEAD_DIM).astype(BF16)
        kcvc = _cmp_call(o32, nsa_cmp_pos[layer], w1, nsa_cmp_w2[layer].astype(BF16), kg[0][None, :],
                         batch, seq, ncp)
        o_nsa, wg16, wu16, wo16, wd16 = _nsa_call(
            o16v, o32v, kcvc, bias_c, tz, selmt,
            (ffn_w_gate[layer], ffn_w_up[layer], w_out[layer], ffn_w_down[layer]), batch, seq, ncp)

        x1 = _out_call(o_fox.reshape(batch * seq, FOX_WIDTH), o_nsa.reshape(batch * seq, NSA_WIDTH),
                       mix_out_gain[layer][None, :], wo16, x2, g1, seq)
        x2 = _ffn_call(x1, norm2_gain[layer][None, :], sc2, sh2, g2, wg16, wu16, wd16, seq)
    return x2.reshape(batch, seq, d)
```

```python
import functools
import math

import numpy as np
import jax
import jax.numpy as jnp
from jax import lax
from jax.experimental import pallas as pl
from jax.experimental.pallas import tpu as pltpu

HEAD_DIM = 128
N_FOX_HEADS = 8
N_NSA_HEADS = 8
N_NSA_KV_HEADS = 2
NSA_GROUP = N_NSA_HEADS // N_NSA_KV_HEADS
FOX_WIDTH = N_FOX_HEADS * HEAD_DIM
NSA_WIDTH = N_NSA_HEADS * HEAD_DIM
NSA_KV_WIDTH = N_NSA_KV_HEADS * HEAD_DIM
MIX_WIDTH = FOX_WIDTH + NSA_WIDTH
N_BRANCH = 3
CMP_BLOCK = 32
CMP_STRIDE = 16
SEL_BLOCK = 64
N_SEL = 8
WINDOW = 512
N_BUCKETS = 32
MAX_DISTANCE = 128
N_MOD = 6
SCALE = HEAD_DIM ** -0.5
LOG2E = math.log2(math.e)
LOG2E_HI = float(np.float32(LOG2E))
LOG2E_LO = LOG2E - LOG2E_HI
QSCALE = SCALE * LOG2E
EPS = 1e-6
NEG = -1e30
FORCE = 1e6

LANES = 128
GROUP_ROWS = NSA_GROUP * LANES
VMEM_LIMIT = 56 * 1024 * 1024
MXU_COLS = 256
KEY_CHUNK = MXU_COLS
WIN_PAD = WINDOW + LANES
MASK_BIG = 2.0 ** 100
SEL, WIN = 0, 1
REPACK_SLOTS = 4
AUG_FAR, AUG_PAD = 120, 127
FOX_HEADS_PER_STEP = 4
SEL_PAD_TILES = 1

W16 = 3 * FOX_WIDTH + NSA_WIDTH + 4 * NSA_KV_WIDTH
W32 = 5 * LANES
COL_FF = 2 * NSA_KV_WIDTH
COL_NG = COL_FF + N_FOX_HEADS

F32 = jnp.float32
BF16 = jnp.bfloat16
NT_DIMS = (((1,), (1,)), ((), ()))


def _params(n_axes):
    return pltpu.CompilerParams(dimension_semantics=("arbitrary",) * n_axes,
                                vmem_limit_bytes=VMEM_LIMIT)


def _sigmoid(x):
    return 1.0 / (1.0 + jnp.exp(-x))


def _lane_tile(a, n):
    return jnp.concatenate([a] * n, axis=1)


BF16_SUBLANES = 16


def _cast_block(w, n_steps):
    rows, cols = w.shape
    assert rows % n_steps == 0, (w.shape, n_steps)
    per_step = rows // n_steps
    span = BF16_SUBLANES // math.gcd(BF16_SUBLANES, per_step)
    assert n_steps % span == 0, (w.shape, n_steps)
    return (per_step * span, cols), span


def _rms(x):
    return x * lax.rsqrt(jnp.mean(x * x, axis=-1, keepdims=True) + EPS)


NORM_ROWS = 16


def _modulated_norm(x_ref, gain_ref, sc_ref, sh_ref, h_ref, first_row, n_rows):
    gm = gain_ref[...] * (1.0 + sc_ref[0])
    sh = sh_ref[0]
    for r0 in range(first_row, first_row + n_rows, NORM_ROWS):
        rows = slice(r0, r0 + NORM_ROWS)
        h_ref[rows, :] = (_rms(x_ref[rows, :]) * gm + sh).astype(BF16)


ROW_CHAINS = 2


def _ada_kernel(c_ref, w_ref, b_ref, o_ref):
    c = c_ref[...]
    s = (c * _sigmoid(c)).astype(BF16)
    o_ref[...] = jnp.dot(s, w_ref[...].astype(BF16), preferred_element_type=F32) + b_ref[...]


def _ada_call(c_pad, w, b):
    rows, d = c_pad.shape
    n = w.shape[1]
    tn = next(t for t in (1024, 768, 512, 384, 256, 128) if n % t == 0)
    return pl.pallas_call(
        _ada_kernel,
        grid=(n // tn,),
        in_specs=[pl.BlockSpec((rows, d), lambda j: (0, 0)),
                  pl.BlockSpec((d, tn), lambda j: (0, j)),
                  pl.BlockSpec((1, tn), lambda j: (0, j))],
        out_specs=pl.BlockSpec((rows, tn), lambda j: (0, j)),
        out_shape=jax.ShapeDtypeStruct((rows, n), F32),
        compiler_params=_params(1),
        name="adaln",
    )(c_pad, w, b)


def _proj_kernel(x_ref, sc_ref, sh_ref, g_ref, w16_ref, w32_ref, gain_ref, flag_ref, o16_ref, o32_ref, h_scr, *, tn):
    j = pl.program_id(1)
    tm = h_scr.shape[0]

    def column_step(first_row, n_rows, with_side_outputs):
        rows = slice(first_row, first_row + n_rows)
        h = h_scr[rows, :]
        for c in range(tn // MXU_COLS):
            acc = jnp.dot(h, w16_ref[:, c * MXU_COLS:(c + 1) * MXU_COLS], preferred_element_type=F32)
            for g in range(MXU_COLS // LANES):
                cols = slice(c * MXU_COLS + g * LANES, c * MXU_COLS + (g + 1) * LANES)
                a = acc[:, g * LANES:(g + 1) * LANES]
                r = lax.rsqrt(jnp.mean(a * a, axis=-1, keepdims=True) + EPS)
                scale = jnp.where(flag_ref[:, cols] > 0.5, r, 1.0)
                o16_ref[rows, cols] = (a * scale * gain_ref[:, cols]).astype(BF16)
        if with_side_outputs:
            o32_ref[rows, :] = jnp.dot(h, w32_ref[...], preferred_element_type=F32)

    @pl.when(j == 0)
    def _():
        n_rows = tm // ROW_CHAINS
        for c in range(ROW_CHAINS):
            _modulated_norm(x_ref, g_ref, sc_ref, sh_ref, h_scr, c * n_rows, n_rows)
            column_step(c * n_rows, n_rows, False)

    last = pl.num_programs(1) - 1
    pl.when(jnp.logical_and(j > 0, j < last))(functools.partial(column_step, 0, tm, False))
    pl.when(j == last)(functools.partial(column_step, 0, tm, True))


def _proj_call(x2, sc, sh, gain1, w16, w32, col_gain, col_flag, seq):
    t, d = x2.shape
    tm = min(1024, seq)
    tn = 1280 if W16 % 1280 == 0 else 1024
    assert W16 // tn >= 2
    rows_per_batch = seq // tm
    return pl.pallas_call(
        functools.partial(_proj_kernel, tn=tn),
        grid=(t // tm, W16 // tn),
        in_specs=[pl.BlockSpec((tm, d), lambda i, j: (i, 0)),
                  pl.BlockSpec((1, 1, d), lambda i, j: (i // rows_per_batch, 0, 0)),
                  pl.BlockSpec((1, 1, d), lambda i, j: (i // rows_per_batch, 0, 0)),
                  pl.BlockSpec((1, d), lambda i, j: (0, 0)),
                  pl.BlockSpec((d, tn), lambda i, j: (0, j)),
                  pl.BlockSpec((d, W32), lambda i, j: (0, 0)),
                  pl.BlockSpec((1, tn), lambda i, j: (0, j)),
                  pl.BlockSpec((1, tn), lambda i, j: (0, j))],
        out_specs=[pl.BlockSpec((tm, tn), lambda i, j: (i, j)),
                   pl.BlockSpec((tm, W32), lambda i, j: (i, 0))],
        out_shape=[jax.ShapeDtypeStruct((t, W16), BF16),
                   jax.ShapeDtypeStruct((t, W32), F32)],
        scratch_shapes=[pltpu.VMEM((tm, d), BF16)],
        compiler_params=_params(2),
        name="in_proj",
    )(x2, sc, sh, gain1, w16, w32, col_gain, col_flag)


def _split3(c):
    hi = c.astype(BF16).astype(F32)
    r1 = c - hi
    mid = r1.astype(BF16).astype(F32)
    return hi, mid, (r1 - mid).astype(BF16).astype(F32)


def _cum_kernel(ff_ref, fb_ref, o_ref, *, seq, groups):
    ri = lax.broadcasted_iota(jnp.int32, (LANES, LANES), 0)
    ci = lax.broadcasted_iota(jnp.int32, (LANES, LANES), 1)
    tri = jnp.where(ri >= ci, 1.0, 0.0).astype(BF16)
    carry = jnp.zeros((1, LANES), F32)
    for blk in range(seq // LANES):
        rows = slice(blk * LANES, (blk + 1) * LANES)
        x = ff_ref[0, rows, :] + fb_ref[...]
        lf = jnp.minimum(x, 0.0) - jnp.log(1.0 + jnp.exp(-jnp.abs(x)))
        hi = lf.astype(BF16)
        r1 = lf - hi.astype(F32)
        mid = r1.astype(BF16)
        lo = (r1 - mid.astype(F32)).astype(BF16)
        c = (jnp.dot(tri, hi, preferred_element_type=F32)
             + jnp.dot(tri, mid, preferred_element_type=F32)
             + jnp.dot(tri, lo, preferred_element_type=F32)) + carry
        carry = c[LANES - 1:LANES, :]
        for t, term in enumerate(_split3(c * LOG2E_HI + c * LOG2E_LO)):
            for g in range(groups):
                shift = (-g * (N_FOX_HEADS // groups)) % LANES
                o_ref[0, g, t, rows, :] = term if shift == 0 else pltpu.roll(term, shift, 1)


def _cum_call(o32, fb_pad, batch, seq, groups):
    o32v = o32.reshape(batch, seq, W32)
    return pl.pallas_call(
        functools.partial(_cum_kernel, seq=seq, groups=groups),
        grid=(batch,),
        in_specs=[pl.BlockSpec((1, seq, LANES), lambda b: (b, 0, COL_FF // LANES)),
                  pl.BlockSpec((1, LANES), lambda b: (0, 0))],
        out_specs=pl.BlockSpec((1, groups, 3, seq, LANES), lambda b: (b, 0, 0, 0, 0)),
        out_shape=jax.ShapeDtypeStruct((batch, groups, 3, seq, LANES), F32),
        compiler_params=_params(1),
        name="fox_cumsum",
    )(o32v, fb_pad)


def _fox_kernel(q_ref, k_ref, v_ref, cs_ref, o_ref, m_scr, acc_scr, s_scr, kaug_scr, vaug_scr, *, seq, tq, tk, nh):
    n_diag = tq // tk
    lane_q = lax.broadcasted_iota(jnp.int32, (tq, LANES), 1)
    row8 = lax.broadcasted_iota(jnp.int32, (8, tk), 0)
    ri = lax.broadcasted_iota(jnp.int32, (tq, tk), 0)
    ci = lax.broadcasted_iota(jnp.int32, (tq, tk), 1)
    causal = [ci + d * tk <= ri for d in range(n_diag)]
    for c in range(seq // tk):
        rows = slice(c * tk, (c + 1) * tk)
        terms_t = [cs_ref[0, 0, t, rows, :].T for t in range(3)]
        for j in range(nh):
            tail8 = jnp.where(row8 < 3, 1.0, jnp.where(row8 == 3, -terms_t[0][j:j + 1], jnp.where(
                row8 == 4, -terms_t[1][j:j + 1], jnp.where(row8 == 5, -terms_t[2][j:j + 1], 0.0))))
            kaug_scr[j, c, 0:HEAD_DIM, :] = k_ref[0, rows, j * HEAD_DIM:(j + 1) * HEAD_DIM].T
            kaug_scr[j, c, HEAD_DIM:, :] = jnp.concatenate(
                [tail8, jnp.zeros((HEAD_DIM - 8, tk), F32)], axis=0).astype(BF16)
    for j in range(nh):
        vaug_scr[j, :, 0:HEAD_DIM] = v_ref[0, :, j * HEAD_DIM:(j + 1) * HEAD_DIM]
        vaug_scr[j, :, HEAD_DIM:] = jnp.ones((seq, HEAD_DIM), BF16)
    acc_scr[...] = jnp.zeros(acc_scr.shape, F32)

    def q_body(qi, carry):
        q0 = pl.multiple_of(qi * tq, tq)
        terms = [cs_ref[0, 0, t, pl.ds(q0, tq), :] for t in range(3)]
        qs = []
        for j in range(nh):
            hi, mid, lo = [term if t == j else pltpu.roll(term, (t - j) % LANES, 1) for t, term in enumerate(terms)]
            tail = jnp.where(lane_q == 0, hi, jnp.where(lane_q == 1, mid, jnp.where(lane_q == 2, lo,
                             jnp.where(lane_q < 6, 1.0, 0.0))))
            qs.append(jnp.concatenate([q_ref[0, pl.ds(q0, tq), j * HEAD_DIM:(j + 1) * HEAD_DIM],
                                       tail.astype(BF16)], axis=1))
            m_scr[j] = jnp.full((tq, LANES), NEG, F32)

        all_rows = slice(0, tq)

        def scores(j, ki, rows):
            return jnp.dot(qs[j][rows], kaug_scr[j, ki], preferred_element_type=F32)

        def tile(ki, mask, rows, next_rows):
            k0 = pl.multiple_of(ki * tk, tk)
            for j in range(nh):
                s = s_scr[j, rows, :]
                if mask is not None:
                    s = jnp.where(mask[rows], s, NEG)
                if next_rows is not None:
                    s_scr[j, next_rows, :] = scores(j, ki + 1, next_rows)
                m_prev = m_scr[j, rows, :]
                m_new = jnp.maximum(m_prev, jnp.max(s, axis=1, keepdims=True))
                alpha = jnp.exp2(m_prev - m_new)
                p = jnp.exp2(s - _lane_tile(m_new, tk // LANES))
                pv = jnp.dot(p.astype(BF16), vaug_scr[j, pl.ds(k0, tk), :], preferred_element_type=F32)
                acc_scr[j, rows, :] = _lane_tile(alpha, 2) * acc_scr[j, rows, :] + pv
                m_scr[j, rows, :] = m_new

        def k_body(ki, c2):
            tile(ki, None, all_rows, all_rows)
            return c2

        for j in range(nh):
            s_scr[j] = scores(j, 0, all_rows)
        n_off = qi * n_diag
        lax.fori_loop(0, n_off, k_body, 0)
        for d in range(n_diag):
            nxt = slice((d + 1) * tk, tq) if d + 1 < n_diag else None
            tile(n_off + d, causal[d], slice(d * tk, tq), nxt)
        for j in range(nh):
            acc = acc_scr[j]
            o_ref[0, pl.ds(q0, tq), j * HEAD_DIM:(j + 1) * HEAD_DIM] = acc[:, 0:HEAD_DIM] / acc[:, HEAD_DIM:]
        return carry

    lax.fori_loop(0, seq // tq, q_body, 0)


def _fox_call(o16v, cs, batch, seq):
    nh = FOX_HEADS_PER_STEP
    tq, tk = min(512, seq), min(256, seq)
    groups = N_FOX_HEADS // nh
    w = nh * HEAD_DIM
    return pl.pallas_call(
        functools.partial(_fox_kernel, seq=seq, tq=tq, tk=tk, nh=nh),
        grid=(batch, groups),
        in_specs=[pl.BlockSpec((1, seq, w), lambda b, h: (b, 0, h)),
                  pl.BlockSpec((1, seq, w), lambda b, h: (b, 0, groups + h)),
                  pl.BlockSpec((1, seq, w), lambda b, h: (b, 0, 2 * groups + h)),
                  pl.BlockSpec((1, 1, 3, seq, LANES), lambda b, h: (b, h, 0, 0, 0))],
        out_specs=pl.BlockSpec((1, seq, w), lambda b, h: (b, 0, h)),
        out_shape=jax.ShapeDtypeStruct((batch, seq, FOX_WIDTH), F32),
        scratch_shapes=[pltpu.VMEM((nh, tq, LANES), F32),
                        pltpu.VMEM((nh, tq, 2 * HEAD_DIM), F32),
                        pltpu.VMEM((nh, tq, tk), F32),
                        pltpu.VMEM((nh, seq // tk, 2 * HEAD_DIM, tk), BF16),
                        pltpu.VMEM((nh, seq, 2 * HEAD_DIM), BF16)],
        compiler_params=_params(2),
        name="fox_attention",
    )(o16v, o16v, o16v, cs)


def _cmp_kernel(x_ref, pos_ref, w1_ref, w2_ref, gain_ref, o_ref, xs_scr, *, seq, ncp):
    rows = xs_scr.shape[1]
    for kv in range(2):
        for h in range(N_NSA_KV_HEADS):
            n = kv * N_NSA_KV_HEADS + h
            xs_scr[n, 0:seq, :] = x_ref[0, :, n * HEAD_DIM:(n + 1) * HEAD_DIM]
            xs_scr[n, seq:rows, :] = jnp.zeros((rows - seq, LANES), F32)
            acc = jnp.zeros((ncp, HEAD_DIM), F32)
            for l in range(CMP_BLOCK):
                xl = xs_scr[n, pl.ds(l, ncp, stride=CMP_STRIDE), :] + pos_ref[kv, l:l + 1, :]
                acc = acc + jnp.dot(xl.astype(BF16), w1_ref[kv, l], preferred_element_type=F32)
            hmid = acc * _sigmoid(acc)
            y = jnp.dot(hmid.astype(BF16), w2_ref[kv], preferred_element_type=F32)
            if kv == 0:
                y = _rms(y) * gain_ref[...]
            o_ref[0, kv, h] = y.astype(BF16)


def _cmp_call(o32, pos, w1, w2, gain, batch, seq, ncp):
    o32v = o32.reshape(batch, seq, W32)
    hkv = N_NSA_KV_HEADS
    width = 2 * NSA_KV_WIDTH
    return pl.pallas_call(
        functools.partial(_cmp_kernel, seq=seq, ncp=ncp),
        grid=(batch,),
        in_specs=[pl.BlockSpec((1, seq, width), lambda b: (b, 0, 0)),
                  pl.BlockSpec((2, CMP_BLOCK, HEAD_DIM), lambda b: (0, 0, 0)),
                  pl.BlockSpec((2, CMP_BLOCK, HEAD_DIM, HEAD_DIM), lambda b: (0, 0, 0, 0)),
                  pl.BlockSpec((2, HEAD_DIM, HEAD_DIM), lambda b: (0, 0, 0)),
                  pl.BlockSpec((1, HEAD_DIM), lambda b: (0, 0))],
        out_specs=pl.BlockSpec((1, 2, hkv, ncp, HEAD_DIM), lambda b: (b, 0, 0, 0, 0)),
        out_shape=jax.ShapeDtypeStruct((batch, 2, hkv, ncp, HEAD_DIM), BF16),
        scratch_shapes=[pltpu.VMEM((2 * hkv, CMP_STRIDE * ncp + CMP_BLOCK, LANES), F32)],
        compiler_params=_params(1),
        name="nsa_compress",
    )(o32v, pos, w1, w2, gain)


def _bias_kernel(rb_ref, o_ref, *, width, key_stride, key_offset, first_tile, rolled_tiles):
    v = pl.program_id(0) + first_tile
    i = lax.broadcasted_iota(jnp.int32, (LANES, width), 0)
    j = lax.broadcasted_iota(jnp.int32, (LANES, width), 1)
    d = v * LANES + i - (key_stride * j + key_offset)
    n = jnp.maximum(d, 0)
    max_exact = N_BUCKETS // 2
    nf = jnp.maximum(n, 1).astype(F32)
    large = max_exact + jnp.trunc(jnp.log(nf / max_exact) / math.log(MAX_DISTANCE / max_exact)
                                  * (N_BUCKETS - max_exact))
    large = jnp.minimum(large, float(N_BUCKETS - 1))
    bkt = jnp.where(n < max_exact, n.astype(F32), large)
    vals = [jnp.zeros((LANES, width), F32) for _ in range(N_NSA_HEADS)]
    for bk in range(N_BUCKETS):
        hit = bkt == float(bk)
        for h in range(N_NSA_HEADS):
            vals[h] = jnp.where(hit, rb_ref[bk * N_NSA_HEADS + h] * LOG2E, vals[h])
    for h in range(N_NSA_HEADS):
        g = h % NSA_GROUP
        rows = slice(g * LANES, (g + 1) * LANES)
        if rolled_tiles:
            per_tile = LANES // key_stride
            for t in range(rolled_tiles):
                shift = (width - per_tile * (rolled_tiles - 1 - t)) % width
                o_ref[t, h // NSA_GROUP, rows, :] = vals[h] if shift == 0 else pltpu.roll(vals[h], shift, 1)
        else:
            o_ref[0, h // NSA_GROUP, rows, :] = vals[h]


def _bias_call(rb_flat, n_tiles, width, key_stride, key_offset, name, rolled=False):
    kern = functools.partial(_bias_kernel, width=width, key_stride=key_stride, key_offset=key_offset,
                             first_tile=n_tiles - 1 if rolled else 0, rolled_tiles=n_tiles if rolled else 0)
    block = (n_tiles if rolled else 1, N_NSA_KV_HEADS, GROUP_ROWS, width)
    return pl.pallas_call(
        kern,
        grid=(1 if rolled else n_tiles,),
        in_specs=[pl.BlockSpec(memory_space=pltpu.SMEM)],
        out_specs=pl.BlockSpec(block, lambda v: (v, 0, 0, 0)),
        out_shape=jax.ShapeDtypeStruct((n_tiles, N_NSA_KV_HEADS, GROUP_ROWS, width), F32),
        compiler_params=_params(1),
        name=name,
    )(rb_flat)


def _tile4(a):
    return jnp.concatenate([a] * NSA_GROUP, axis=0)


def _nsa_kernel(q_ref, ks_ref, kw_ref, vs_ref, vw_ref, g_ref, kcvc_ref, bc_ref, tz_ref, selmt_ref,
                wa_ref, wb_ref, wc_ref, wd_ref, o_ref, wa_out, wb_out, wc_out, wd_out,
                qg_scr, qw_scr, caug_scr, near_scr, edge_scr, m_scr, acc_scr, o_scr, s_scr,
                ksa_scr, vsa_scr, kwp_scr, vwa_scr, *, seq, ncp):
    qt = pl.program_id(1)
    q0 = qt * LANES
    n_slc = seq // SEL_BLOCK
    top_n = min(N_SEL, n_slc)
    nkv = N_NSA_KV_HEADS
    for src, dst in ((wa_ref, wa_out), (wb_ref, wb_out), (wc_ref, wc_out), (wd_ref, wd_out)):
        dst[...] = src[...].astype(BF16)
    ri = lax.broadcasted_iota(jnp.int32, (LANES, LANES), 0)
    ci = lax.broadcasted_iota(jnp.int32, (LANES, LANES), 1)
    eye = jnp.where(ri == ci, 1.0, 0.0).astype(BF16)
    gates = _sigmoid(g_ref[0])

    def gate_col(hk, br):
        cols = []
        for g in range(NSA_GROUP):
            c = COL_NG % LANES + (hk * NSA_GROUP + g) * N_BRANCH + br
            cols.append(gates[:, c:c + 1])
        return jnp.concatenate(cols, axis=0)

    @pl.when(qt == 0)
    def _():
        ones = jnp.ones((seq, HEAD_DIM), BF16)
        row = lax.broadcasted_iota(jnp.int32, (LANES, LANES), 0)
        is_far = jnp.logical_and(row >= AUG_FAR, row < AUG_FAR + 3)
        pad_aug = jnp.where(row == AUG_PAD, -MASK_BIG, jnp.where(is_far, 1.0, 0.0)).astype(BF16)
        win_aug = jnp.where(is_far, 1.0, 0.0).astype(BF16)
        lane512 = lax.broadcasted_iota(jnp.int32, (GROUP_ROWS, LANES), 1)
        causal = _tile4(jnp.where(ci <= ri, 0.0, -MASK_BIG))
        acc_scr[...] = jnp.zeros(acc_scr.shape, F32)
        edge_scr[...] = jnp.concatenate([jnp.full((GROUP_ROWS, LANES), -MASK_BIG, F32),
                                         _tile4(jnp.where(ri < ci, 0.0, -MASK_BIG))], axis=1)
        for hk in range(nkv):
            hc = slice(hk * HEAD_DIM, (hk + 1) * HEAD_DIM)
            far = tz_ref[2, hk]
            hi, mid, lo = _split3(far)
            caug = jnp.where(lane512 == AUG_FAR, hi, jnp.where(lane512 == AUG_FAR + 1, mid,
                             jnp.where(lane512 == AUG_FAR + 2, lo, jnp.where(lane512 == AUG_PAD, 1.0, 0.0))))
            caug_scr[hk] = caug.astype(BF16)
            qw_scr[hk, :, HEAD_DIM:] = caug.astype(BF16)
            near_scr[hk] = jnp.concatenate([tz_ref[1, hk] - far, (tz_ref[0, hk] - far) + causal], axis=1)
            ksa_scr[hk, 0, 0:HEAD_DIM, :] = jnp.zeros((HEAD_DIM, LANES), BF16)
            ksa_scr[hk, 0, HEAD_DIM:, :] = pad_aug
            for t in range(seq // LANES):
                rows = slice(t * LANES, (t + 1) * LANES)
                ksa_scr[hk, SEL_PAD_TILES + t, 0:HEAD_DIM, :] = ks_ref[0, rows, hc].T
                ksa_scr[hk, SEL_PAD_TILES + t, HEAD_DIM:, :] = jnp.where(
                    (t * LANES + ci) // SEL_BLOCK == ri, -MASK_BIG, jnp.where(is_far, 1.0, 0.0)).astype(BF16)
                kwp_scr[hk, WIN_PAD // LANES + t, 0:HEAD_DIM, :] = kw_ref[0, rows, hc].T
                kwp_scr[hk, WIN_PAD // LANES + t, HEAD_DIM:, :] = win_aug
            for t in range(WIN_PAD // LANES):
                kwp_scr[hk, t, 0:HEAD_DIM, :] = jnp.zeros((HEAD_DIM, LANES), BF16)
                kwp_scr[hk, t, HEAD_DIM:, :] = pad_aug
            vsa_scr[hk, 0:SEL_PAD_TILES * LANES, :] = jnp.zeros((SEL_PAD_TILES * LANES, 2 * HEAD_DIM), BF16)
            vsa_scr[hk, SEL_PAD_TILES * LANES:, 0:HEAD_DIM] = vs_ref[0, :, hc]
            vsa_scr[hk, SEL_PAD_TILES * LANES:, HEAD_DIM:] = ones
            vwa_scr[hk, 0:WIN_PAD, :] = jnp.zeros((WIN_PAD, 2 * HEAD_DIM), BF16)
            vwa_scr[hk, WIN_PAD:, 0:HEAD_DIM] = vw_ref[0, :, hc]
            vwa_scr[hk, WIN_PAD:, HEAD_DIM:] = ones

    def reset(br):
        m_scr[br] = jnp.full(m_scr.shape[1:], NEG, F32)

    def online_update(br, hk, s, vaug):
        m_prev = m_scr[br, hk]
        m_new = jnp.maximum(m_prev, jnp.max(s, axis=1, keepdims=True))
        alpha = jnp.exp2(m_prev - m_new)
        p = jnp.exp2(s - _lane_tile(m_new, s.shape[1] // LANES))
        pv = jnp.dot(p.astype(BF16), vaug, preferred_element_type=F32)
        acc_scr[br, hk] = _lane_tile(alpha, 2) * acc_scr[br, hk] + pv
        m_scr[br, hk] = m_new

    def finish(br, hk):
        acc = acc_scr[br, hk]
        return acc[:, 0:HEAD_DIM] / acc[:, HEAD_DIM:]

    for hk in range(nkv):
        for g in range(NSA_GROUP):
            h = hk * NSA_GROUP + g
            q_h = q_ref[0, :, h * HEAD_DIM:(h + 1) * HEAD_DIM]
            qg_scr[hk, g * LANES:(g + 1) * LANES, 0:HEAD_DIM] = q_h
            qw_scr[hk, g * LANES:(g + 1) * LANES, 0:HEAD_DIM] = q_h

    for hk in range(nkv):
        qg = qg_scr[hk, :, 0:HEAD_DIM]

        kc = kcvc_ref[0, 0, hk]
        vc = kcvc_ref[0, 1, hk]
        s = lax.dot_general(qg, kc, NT_DIMS, preferred_element_type=F32) + bc_ref[0, hk]
        rc = lax.broadcasted_iota(jnp.int32, (LANES, ncp), 0)
        cc = lax.broadcasted_iota(jnp.int32, (LANES, ncp), 1)
        valid_c = _tile4(jnp.where(q0 + rc - (CMP_STRIDE * cc + CMP_BLOCK - 1) >= 0, 1.0, 0.0)) > 0.5
        s = jnp.where(valid_c, s, NEG)
        p = jnp.where(valid_c, jnp.exp2(s - jnp.max(s, axis=1, keepdims=True)), 0.0)
        l = jnp.sum(p, axis=1, keepdims=True)
        p = p / jnp.where(l > 0.0, l, 1.0)
        o_scr[hk] = gate_col(hk, 0) * jnp.dot(p.astype(BF16), vc, preferred_element_type=F32)

        psum = p[0:LANES]
        for g in range(1, NSA_GROUP):
            psum = psum + p[g * LANES:(g + 1) * LANES]
        p_hi = psum.astype(BF16)
        p_lo = (psum - p_hi.astype(F32)).astype(BF16)
        selmt = selmt_ref[...]
        imp = (lax.dot_general(selmt, p_hi, NT_DIMS, preferred_element_type=F32)
               + lax.dot_general(selmt, p_lo, NT_DIMS, preferred_element_type=F32))
        imp = imp[0:n_slc]
        blk = lax.broadcasted_iota(jnp.int32, (n_slc, LANES), 0)
        cur = (q0 + lax.broadcasted_iota(jnp.int32, (n_slc, LANES), 1)) // SEL_BLOCK
        forced = (blk == 0) | (blk == cur) | (blk == cur - 1)
        imp = jnp.where(forced, FORCE, imp)
        imp = jnp.where(blk <= cur, imp, -jnp.inf)
        rank = jnp.zeros((n_slc, LANES), F32)
        for j in range(n_slc):
            row = imp[j:j + 1, :]
            beats = jnp.where(row > imp, 1.0, jnp.where(row == imp, jnp.where(blk > j, 1.0, 0.0), 0.0))
            rank = rank + beats
        sel_t = jnp.where(rank < top_n, jnp.where(imp > -jnp.inf, 1.0, 0.0), 0.0)
        if n_slc < LANES:
            sel_t = jnp.concatenate([sel_t, jnp.zeros((LANES - n_slc, LANES), F32)], axis=0)
        sel_q = lax.dot_general(eye, sel_t.astype(BF16), NT_DIMS, preferred_element_type=F32)
        not_sel = jnp.where(ci < n_slc, 1.0 - sel_q, 0.0).astype(BF16)
        for g in range(NSA_GROUP):
            rows = slice(g * LANES, (g + 1) * LANES)
            qg_scr[hk, rows, HEAD_DIM:] = jnp.where(ci < n_slc, not_sel, caug_scr[hk, rows, :])

    reset(WIN)
    for off, table in ((1, near_scr), (3, None), (5, edge_scr)):
        t0 = qt - off + WIN_PAD // LANES
        p0 = pl.multiple_of(t0 * LANES, LANES)
        for hk in range(nkv):
            k_t = jnp.concatenate([kwp_scr[hk, t0], kwp_scr[hk, t0 + 1]], axis=1)
            s = jnp.dot(qw_scr[hk], k_t, preferred_element_type=F32)
            if table is not None:
                s = s + (table[hk] if table is near_scr else table[...])
            online_update(WIN, hk, s, vwa_scr[hk, pl.ds(p0, KEY_CHUNK), :])

    reset(SEL)
    n_chunks = (qt + 2) // 2
    first_tile = SEL_PAD_TILES - (qt + 1) % 2

    def sel_scores(hk, c):
        t0 = first_tile + 2 * c
        k_t = jnp.concatenate([ksa_scr[hk, t0], ksa_scr[hk, t0 + 1]], axis=1)
        return jnp.dot(qg_scr[hk], k_t, preferred_element_type=F32)

    def sel_chunk(c, last):
        p0 = pl.multiple_of((first_tile + 2 * c) * LANES, LANES)
        for hk in range(nkv):
            s = s_scr[hk]
            if last:
                s = s + near_scr[hk]
            else:
                s_scr[hk] = sel_scores(hk, c + 1)
            online_update(SEL, hk, s, vsa_scr[hk, pl.ds(p0, KEY_CHUNK), :])

    def sel_pair(i, carry):
        sel_chunk(2 * i, False)
        sel_chunk(2 * i + 1, False)
        return carry

    for hk in range(nkv):
        s_scr[hk] = sel_scores(hk, 0)
    n_far = n_chunks - 1
    lax.fori_loop(0, n_far // 2, sel_pair, 0)
    pl.when(n_far % 2 == 1)(lambda: sel_chunk(n_far - 1, False))
    sel_chunk(n_chunks - 1, True)
    for hk in range(nkv):
        o = (o_scr[hk] + gate_col(hk, 1) * finish(SEL, hk)) + gate_col(hk, 2) * finish(WIN, hk)
        for g in range(NSA_GROUP):
            h = hk * NSA_GROUP + g
            o_ref[0, :, h * HEAD_DIM:(h + 1) * HEAD_DIM] = o[g * LANES:(g + 1) * LANES]


def _nsa_call(o16v, o32v, kcvc, bias_c, tz, selmt, weights_f32, batch, seq, ncp):
    nqt = seq // LANES
    wspecs = []
    for w in weights_f32:
        blk, span = _cast_block(w, batch * nqt)
        wspecs.append(pl.BlockSpec(blk, functools.partial(lambda b, t, span: ((b * nqt + t) // span, 0), span=span)))
    kvw = NSA_KV_WIDTH
    nkv = N_NSA_KV_HEADS
    base = (3 * FOX_WIDTH + NSA_WIDTH) // kvw
    return pl.pallas_call(
        functools.partial(_nsa_kernel, seq=seq, ncp=ncp),
        grid=(batch, nqt),
        in_specs=[pl.BlockSpec((1, LANES, NSA_WIDTH), lambda b, t: (b, t, 3 * FOX_WIDTH // NSA_WIDTH)),
                  pl.BlockSpec((1, seq, kvw), lambda b, t: (b, 0, base)),
                  pl.BlockSpec((1, seq, kvw), lambda b, t: (b, 0, base + 1)),
                  pl.BlockSpec((1, seq, kvw), lambda b, t: (b, 0, base + 2)),
                  pl.BlockSpec((1, seq, kvw), lambda b, t: (b, 0, base + 3)),
                  pl.BlockSpec((1, LANES, LANES), lambda b, t: (b, t, COL_NG // LANES)),
                  pl.BlockSpec((1, 2, N_NSA_KV_HEADS, ncp, HEAD_DIM), lambda b, t: (b, 0, 0, 0, 0)),
                  pl.BlockSpec((1, N_NSA_KV_HEADS, GROUP_ROWS, ncp), lambda b, t: (t, 0, 0, 0)),
                  pl.BlockSpec((3, N_NSA_KV_HEADS, GROUP_ROWS, LANES), lambda b, t: (0, 0, 0, 0)),
                  pl.BlockSpec((LANES, ncp), lambda b, t: (0, 0))] + wspecs,
        out_specs=[pl.BlockSpec((1, LANES, NSA_WIDTH), lambda b, t: (b, t, 0))] + wspecs,
        out_shape=[jax.ShapeDtypeStruct((batch, seq, NSA_WIDTH), F32)]
        + [jax.ShapeDtypeStruct(w.shape, BF16) for w in weights_f32],
        scratch_shapes=[pltpu.VMEM((nkv, GROUP_ROWS, 2 * HEAD_DIM), BF16),
                        pltpu.VMEM((nkv, GROUP_ROWS, 2 * HEAD_DIM), BF16),
                        pltpu.VMEM((nkv, GROUP_ROWS, LANES), BF16),
                        pltpu.VMEM((nkv, GROUP_ROWS, KEY_CHUNK), F32),
                        pltpu.VMEM((GROUP_ROWS, KEY_CHUNK), F32),
                        pltpu.VMEM((2, nkv, GROUP_ROWS, LANES), F32),
                        pltpu.VMEM((2, nkv, GROUP_ROWS, 2 * HEAD_DIM), F32),
                        pltpu.VMEM((nkv, GROUP_ROWS, HEAD_DIM), F32),
                        pltpu.VMEM((nkv, GROUP_ROWS, KEY_CHUNK), F32),
                        pltpu.VMEM((nkv, SEL_PAD_TILES + seq // LANES, 2 * HEAD_DIM, LANES), BF16),
                        pltpu.VMEM((nkv, SEL_PAD_TILES * LANES + seq, 2 * HEAD_DIM), BF16),
                        pltpu.VMEM((nkv, (seq + WIN_PAD) // LANES, 2 * HEAD_DIM, LANES), BF16),
                        pltpu.VMEM((nkv, seq + WIN_PAD, 2 * HEAD_DIM), BF16)],
        compiler_params=_params(2),
        name="nsa_attention",
    )(o16v, o16v, o16v, o16v, o16v, o32v, kcvc, bias_c, tz, selmt, *weights_f32)


def _out_kernel(of_ref, on_ref, gain_ref, w_ref, x_ref, g_ref, o_ref, y_scr, *, tm, n_chains):
    rows_per_chain = tm // n_chains
    gain_f = gain_ref[:, 0:FOX_WIDTH]
    gain_n = gain_ref[:, FOX_WIDTH:MIX_WIDTH]
    for c in range(n_chains):
        rows = slice(c * rows_per_chain, (c + 1) * rows_per_chain)
        for r0 in range(c * rows_per_chain, (c + 1) * rows_per_chain, NORM_ROWS):
            r = slice(r0, r0 + NORM_ROWS)
            y_scr[r, 0:FOX_WIDTH] = (_rms(of_ref[r, :]) * gain_f).astype(BF16)
            y_scr[r, FOX_WIDTH:MIX_WIDTH] = (_rms(on_ref[r, :]) * gain_n).astype(BF16)
        acc = jnp.dot(y_scr[rows, :], w_ref[...], preferred_element_type=F32)
        o_ref[rows, :] = x_ref[rows, :] + g_ref[0] * acc


def _out_call(o_fox, o_nsa, gain, w_out, x2, g1, seq):
    t, d = x2.shape
    tm = min(512, seq)
    rows_per_batch = seq // tm
    return pl.pallas_call(
        functools.partial(_out_kernel, tm=tm, n_chains=2),
        grid=(t // tm,),
        in_specs=[pl.BlockSpec((tm, FOX_WIDTH), lambda i: (i, 0)),
                  pl.BlockSpec((tm, NSA_WIDTH), lambda i: (i, 0)),
                  pl.BlockSpec((1, MIX_WIDTH), lambda i: (0, 0)),
                  pl.BlockSpec((MIX_WIDTH, d), lambda i: (0, 0)),
                  pl.BlockSpec((tm, d), lambda i: (i, 0)),
                  pl.BlockSpec((1, 1, d), lambda i: (i // rows_per_batch, 0, 0))],
        out_specs=pl.BlockSpec((tm, d), lambda i: (i, 0)),
        out_shape=jax.ShapeDtypeStruct((t, d), F32),
        scratch_shapes=[pltpu.VMEM((tm, MIX_WIDTH), BF16)],
        compiler_params=_params(1),
        name="out_proj",
    )(o_fox, o_nsa, gain, w_out, x2, g1)


def _ffn_kernel(x_ref, gain_ref, sc_ref, sh_ref, g_ref, wg_ref, wu_ref, wd_ref, o_ref, h_scr, *, n_f):
    f = pl.program_id(1)
    tm = h_scr.shape[0]

    def hidden_step(first_row, n_rows, first, last):
        rows = slice(first_row, first_row + n_rows)
        h = h_scr[rows, :]
        a = jnp.dot(h, wg_ref[...], preferred_element_type=F32)
        u = jnp.dot(h, wu_ref[...], preferred_element_type=F32)
        t = (a * _sigmoid(a)) * u
        part = jnp.dot(t.astype(BF16), wd_ref[...], preferred_element_type=F32)
        acc = part if first else o_ref[rows, :] + part
        o_ref[rows, :] = x_ref[rows, :] + g_ref[0] * acc if last else acc

    def edge_step(first, last):
        n_rows = tm // ROW_CHAINS
        for c in range(ROW_CHAINS):
            if first:
                _modulated_norm(x_ref, gain_ref, sc_ref, sh_ref, h_scr, c * n_rows, n_rows)
            hidden_step(c * n_rows, n_rows, first, last)

    if n_f == 1:
        edge_step(True, True)
    else:
        pl.when(f == 0)(functools.partial(edge_step, True, False))
        pl.when(jnp.logical_and(f > 0, f < n_f - 1))(functools.partial(hidden_step, 0, tm, False, False))
        pl.when(f == n_f - 1)(functools.partial(edge_step, False, True))


def _ffn_call(x1, gain2, sc, sh, g2, wg, wu, wd, seq):
    t, d = x1.shape
    dff = wg.shape[1]
    tm = min(1024, seq)
    tf = 512 if dff % 512 == 0 else dff
    rows_per_batch = seq // tm
    return pl.pallas_call(
        functools.partial(_ffn_kernel, n_f=dff // tf),
        grid=(t // tm, dff // tf),
        in_specs=[pl.BlockSpec((tm, d), lambda i, f: (i, 0)),
                  pl.BlockSpec((1, d), lambda i, f: (0, 0)),
                  pl.BlockSpec((1, 1, d), lambda i, f: (i // rows_per_batch, 0, 0)),
                  pl.BlockSpec((1, 1, d), lambda i, f: (i // rows_per_batch, 0, 0)),
                  pl.BlockSpec((1, 1, d), lambda i, f: (i // rows_per_batch, 0, 0)),
                  pl.BlockSpec((d, tf), lambda i, f: (0, f)),
                  pl.BlockSpec((d, tf), lambda i, f: (0, f)),
                  pl.BlockSpec((tf, d), lambda i, f: (f, 0))],
        out_specs=pl.BlockSpec((tm, d), lambda i, f: (i, 0)),
        out_shape=jax.ShapeDtypeStruct((t, d), F32),
        scratch_shapes=[pltpu.VMEM((tm, d), BF16)],
        compiler_params=_params(2),
        name="swiglu_ffn",
    )(x1, gain2, sc, sh, g2, wg, wu, wd)


def _selection_matrix_t(ncp, n_slc):
    r, q = SEL_BLOCK // CMP_STRIDE, CMP_BLOCK // CMP_STRIDE
    m = np.zeros((LANES, ncp), np.float32)
    for j in range(n_slc):
        for a in range(r):
            for b in range(q):
                c = r * j + a - b
                if 0 <= c < ncp:
                    m[j, c] += 1.0
    return m


def _w_in_block_sources():
    o = 0
    start = {}
    for name, width in (("fq", FOX_WIDTH), ("fk", FOX_WIDTH), ("fv", FOX_WIDTH), ("ff", N_FOX_HEADS),
                        ("nq", NSA_WIDTH), ("nk", N_BRANCH * NSA_KV_WIDTH), ("nv", N_BRANCH * NSA_KV_WIDTH),
                        ("ng", N_BRANCH * N_NSA_HEADS)):
        start[name] = o
        o += width
    kvw = NSA_KV_WIDTH
    groups = [(start["fq"], 3 * FOX_WIDTH), (start["nq"], NSA_WIDTH),
              (start["nk"] + kvw, 2 * kvw), (start["nv"] + kvw, 2 * kvw),
              (start["nk"], kvw), (start["nv"], kvw)]
    blocks = [s + LANES * b for s, width in groups for b in range(width // LANES)]
    return blocks, (start["ff"], N_FOX_HEADS), (start["ng"], N_BRANCH * N_NSA_HEADS)


def _repack_kernel(src_ref, wt_hbm, o16_ref, o32_ref, buf, sem, *, n16, n_whole, ff, ng):
    k = pl.program_id(0)
    n_slots = buf.shape[0]
    ahead = n_slots - 1
    slot = k % n_slots

    def whole_copy(kk, s):
        r0 = pl.multiple_of(src_ref[kk], 8)
        return pltpu.make_async_copy(wt_hbm.at[pl.ds(r0, LANES), :], buf.at[s], sem.at[s])

    def narrow_copies(s):
        return [pltpu.make_async_copy(wt_hbm.at[pl.ds(ff[0], ff[1]), :], buf.at[s, pl.ds(0, ff[1]), :], sem.at[s]),
                pltpu.make_async_copy(wt_hbm.at[pl.ds(ng[0], ng[1]), :], buf.at[s, pl.ds(ff[1], ng[1]), :],
                                      sem.at[s])]

    def start_block(kk):
        @pl.when(kk < n_whole)
        def _():
            whole_copy(kk, kk % n_slots).start()

        @pl.when(kk == n_whole)
        def _():
            for cp in narrow_copies(kk % n_slots):
                cp.start()

    @pl.when(k == 0)
    def _():
        for kk in range(ahead):
            start_block(kk)

    start_block(k + ahead)

    @pl.when(k < n_whole)
    def _():
        whole_copy(k, slot).wait()

    @pl.when(k == n_whole)
    def _():
        for cp in narrow_copies(slot):
            cp.wait()

    xt = buf[slot].T
    lane = lax.broadcasted_iota(jnp.int32, xt.shape, 1)
    xt = jnp.where(jnp.logical_or(k < n_whole, lane < ff[1] + ng[1]), xt, 0.0).astype(BF16)

    @pl.when(k < n16)
    def _():
        o16_ref[...] = xt

    @pl.when(k >= n16)
    def _():
        o32_ref[...] = xt


def _repack_w_in(wt):
    n, d = wt.shape
    blocks, ff, ng = _w_in_block_sources()
    n16 = W16 // LANES
    n_whole = len(blocks)
    assert n_whole + 1 == (W16 + W32) // LANES
    return pl.pallas_call(
        functools.partial(_repack_kernel, n16=n16, n_whole=n_whole, ff=ff, ng=ng),
        grid_spec=pltpu.PrefetchScalarGridSpec(
            num_scalar_prefetch=1,
            grid=(n_whole + 1,),
            in_specs=[pl.BlockSpec(memory_space=pl.ANY)],
            out_specs=[pl.BlockSpec((d, LANES), lambda k, src: (0, jnp.minimum(k, n16 - 1))),
                       pl.BlockSpec((d, LANES), lambda k, src: (0, jnp.maximum(k - n16, 0)))],
            scratch_shapes=[pltpu.VMEM((REPACK_SLOTS, LANES, d), F32), pltpu.SemaphoreType.DMA((REPACK_SLOTS,))]),
        out_shape=[jax.ShapeDtypeStruct((d, W16), BF16), jax.ShapeDtypeStruct((d, W32), BF16)],
        compiler_params=_params(1),
        name="w_in_repack",
    )(jnp.asarray(blocks, jnp.int32), wt)


def kernel(x, c, ada_w, ada_b, norm1_gain, norm2_gain, w_in, fox_f_bias, fox_q_gain, fox_k_gain, nsa_q_gain,
           nsa_k_gain, nsa_cmp_pos, nsa_cmp_w1, nsa_cmp_w2, rel_bias, mix_out_gain, w_out, ffn_w_gate, ffn_w_up,
           ffn_w_down):
    batch, seq, d = x.shape
    assert seq % KEY_CHUNK == 0 and seq >= WINDOW and d % LANES == 0 and seq // SEL_BLOCK <= LANES
    depth = ada_w.shape[0]
    nqt = seq // LANES
    ncp = -(-(seq // CMP_STRIDE) // LANES) * LANES

    selmt = jnp.asarray(_selection_matrix_t(ncp, seq // SEL_BLOCK), BF16)
    rb_flat = rel_bias.reshape(-1)
    bias_c = _bias_call(rb_flat, nqt, ncp, CMP_STRIDE, CMP_BLOCK - 1, "t5_bias_compressed", rolled=True)
    tz = _bias_call(rb_flat, 3, LANES, 1, 0, "t5_bias_toeplitz")

    ones_h = jnp.ones((HEAD_DIM,), F32)
    c_pad = jnp.pad(c, ((0, 8 - batch % 8 if batch % 8 else 0), (0, 0)))
    x2 = x.reshape(batch * seq, d)
    for layer in range(depth):
        mod = _ada_call(c_pad, ada_w[layer], ada_b[layer][None, :])[:batch]
        sh1, sc1, g1, sh2, sc2, g2 = [mod[:, i * d:(i + 1) * d][:, None, :] for i in range(N_MOD)]

        kg = nsa_k_gain[layer]
        col_gain = jnp.concatenate([
            jnp.tile(fox_q_gain[layer] * QSCALE, N_FOX_HEADS), jnp.tile(fox_k_gain[layer], N_FOX_HEADS),
            jnp.tile(ones_h, N_FOX_HEADS), jnp.tile(nsa_q_gain[layer] * QSCALE, N_NSA_HEADS),
            jnp.tile(kg[1], N_NSA_KV_HEADS), jnp.tile(kg[2], N_NSA_KV_HEADS),
            jnp.tile(ones_h, 2 * N_NSA_KV_HEADS)])[None, :]
        col_flag = jnp.concatenate([
            jnp.ones((2 * FOX_WIDTH,), F32), jnp.zeros((FOX_WIDTH,), F32), jnp.ones((NSA_WIDTH,), F32),
            jnp.ones((2 * NSA_KV_WIDTH,), F32), jnp.zeros((2 * NSA_KV_WIDTH,), F32)])[None, :]
        w16, w32 = _repack_w_in(jnp.swapaxes(w_in, 1, 2)[layer])
        o16, o32 = _proj_call(x2, sc1, sh1, norm1_gain[layer][None, :], w16, w32,
                              col_gain, col_flag, seq)
        o16v = o16.reshape(batch, seq, W16)
        o32v = o32.reshape(batch, seq, W32)

        fb_pad = jnp.pad(fox_f_bias[layer], (0, LANES - N_FOX_HEADS))[None, :]
        cs = _cum_call(o32, fb_pad, batch, seq, N_FOX_HEADS // FOX_HEADS_PER_STEP)
        o_fox = _fox_call(o16v, cs, batch, seq)

        w1 = nsa_cmp_w1[layer].reshape(2, CMP_BLOCK, HEAD_DIM, HEAD_DIM).astype(BF16)
        kcvc = _cmp_call(o32, nsa_cmp_pos[layer], w1, nsa_cmp_w2[layer].astype(BF16), kg[0][None, :],
                         batch, seq, ncp)
        o_nsa, wg16, wu16, wo16, wd16 = _nsa_call(
            o16v, o32v, kcvc, bias_c, tz, selmt,
            (ffn_w_gate[layer], ffn_w_up[layer], w_out[layer], ffn_w_down[layer]), batch, seq, ncp)

        x1 = _out_call(o_fox.reshape(batch * seq, FOX_WIDTH), o_nsa.reshape(batch * seq, NSA_WIDTH),
                       mix_out_gain[layer][None, :], wo16, x2, g1, seq)
        x2 = _ffn_call(x1, norm2_gain[layer][None, :], sc2, sh2, g2, wg16, wu16, wd16, seq)
    return x2.reshape(batch, seq, d)
```

```python
import functools
import math

import numpy as np
import jax
import jax.numpy as jnp
from jax import lax
from jax.experimental import pallas as pl
from jax.experimental.pallas import tpu as pltpu

HEAD_DIM = 128
N_FOX_HEADS = 8
N_NSA_HEADS = 8
N_NSA_KV_HEADS = 2
NSA_GROUP = N_NSA_HEADS // N_NSA_KV_HEADS
FOX_WIDTH = N_FOX_HEADS * HEAD_DIM
NSA_WIDTH = N_NSA_HEADS * HEAD_DIM
NSA_KV_WIDTH = N_NSA_KV_HEADS * HEAD_DIM
MIX_WIDTH = FOX_WIDTH + NSA_WIDTH
N_BRANCH = 3
CMP_BLOCK = 32
CMP_STRIDE = 16
SEL_BLOCK = 64
N_SEL = 8
WINDOW = 512
N_BUCKETS = 32
MAX_DISTANCE = 128
N_MOD = 6
SCALE = HEAD_DIM ** -0.5
LOG2E = math.log2(math.e)
LOG2E_HI = float(np.float32(LOG2E))
LOG2E_LO = LOG2E - LOG2E_HI
QSCALE = SCALE * LOG2E
EPS = 1e-6
NEG = -1e30
FORCE = 1e6

LANES = 128
GROUP_ROWS = NSA_GROUP * LANES
VMEM_LIMIT = 56 * 1024 * 1024
MXU_COLS = 256
KEY_CHUNK = MXU_COLS
WIN_PAD = WINDOW + LANES
MASK_BIG = 2.0 ** 100
SEL, WIN = 0, 1
REPACK_SLOTS = 4
AUG_FAR, AUG_PAD = 120, 127
FOX_HEADS_PER_STEP = 4
SEL_PAD_TILES = 1

W16 = 3 * FOX_WIDTH + NSA_WIDTH + 4 * NSA_KV_WIDTH
W32 = 5 * LANES
COL_FF = 2 * NSA_KV_WIDTH
COL_NG = COL_FF + N_FOX_HEADS

F32 = jnp.float32
BF16 = jnp.bfloat16
NT_DIMS = (((1,), (1,)), ((), ()))


def _params(n_axes):
    return pltpu.CompilerParams(dimension_semantics=("arbitrary",) * n_axes,
                                vmem_limit_bytes=VMEM_LIMIT)


def _sigmoid(x):
    return 1.0 / (1.0 + jnp.exp(-x))


def _lane_tile(a, n):
    return jnp.concatenate([a] * n, axis=1)


BF16_SUBLANES = 16


def _cast_block(w, n_steps):
    rows, cols = w.shape
    assert rows % n_steps == 0, (w.shape, n_steps)
    per_step = rows // n_steps
    span = BF16_SUBLANES // math.gcd(BF16_SUBLANES, per_step)
    assert n_steps % span == 0, (w.shape, n_steps)
    return (per_step * span, cols), span


def _rms(x):
    return x * lax.rsqrt(jnp.mean(x * x, axis=-1, keepdims=True) + EPS)


NORM_ROWS = 16


def _modulated_norm(x_ref, gain_ref, sc_ref, sh_ref, h_ref, first_row, n_rows):
    gm = gain_ref[...] * (1.0 + sc_ref[0])
    sh = sh_ref[0]
    for r0 in range(first_row, first_row + n_rows, NORM_ROWS):
        rows = slice(r0, r0 + NORM_ROWS)
        h_ref[rows, :] = (_rms(x_ref[rows, :]) * gm + sh).astype(BF16)


ROW_CHAINS = 2


def _ada_kernel(c_ref, w_ref, b_ref, o_ref):
    c = c_ref[...]
    s = (c * _sigmoid(c)).astype(BF16)
    o_ref[...] = jnp.dot(s, w_ref[...].astype(BF16), preferred_element_type=F32) + b_ref[...]


def _ada_call(c_pad, w, b):
    rows, d = c_pad.shape
    n = w.shape[1]
    tn = next(t for t in (1024, 768, 512, 384, 256, 128) if n % t == 0)
    return pl.pallas_call(
        _ada_kernel,
        grid=(n // tn,),
        in_specs=[pl.BlockSpec((rows, d), lambda j: (0, 0)),
                  pl.BlockSpec((d, tn), lambda j: (0, j)),
                  pl.BlockSpec((1, tn), lambda j: (0, j))],
        out_specs=pl.BlockSpec((rows, tn), lambda j: (0, j)),
        out_shape=jax.ShapeDtypeStruct((rows, n), F32),
        compiler_params=_params(1),
        name="adaln",
    )(c_pad, w, b)


def _proj_kernel(x_ref, sc_ref, sh_ref, g_ref, w16_ref, w32_ref, gain_ref, flag_ref, o16_ref, o32_ref, h_scr, *, tn):
    j = pl.program_id(1)
    tm = h_scr.shape[0]

    def column_step(first_row, n_rows, with_side_outputs):
        rows = slice(first_row, first_row + n_rows)
        h = h_scr[rows, :]
        for c in range(tn // MXU_COLS):
            acc = jnp.dot(h, w16_ref[:, c * MXU_COLS:(c + 1) * MXU_COLS], preferred_element_type=F32)
            for g in range(MXU_COLS // LANES):
                cols = slice(c * MXU_COLS + g * LANES, c * MXU_COLS + (g + 1) * LANES)
                a = acc[:, g * LANES:(g + 1) * LANES]
                r = lax.rsqrt(jnp.mean(a * a, axis=-1, keepdims=True) + EPS)
                scale = jnp.where(flag_ref[:, cols] > 0.5, r, 1.0)
                o16_ref[rows, cols] = (a * scale * gain_ref[:, cols]).astype(BF16)
        if with_side_outputs:
            o32_ref[rows, :] = jnp.dot(h, w32_ref[...], preferred_element_type=F32)

    @pl.when(j == 0)
    def _():
        n_rows = tm // ROW_CHAINS
        for c in range(ROW_CHAINS):
            _modulated_norm(x_ref, g_ref, sc_ref, sh_ref, h_scr, c * n_rows, n_rows)
            column_step(c * n_rows, n_rows, False)

    last = pl.num_programs(1) - 1
    pl.when(jnp.logical_and(j > 0, j < last))(functools.partial(column_step, 0, tm, False))
    pl.when(j == last)(functools.partial(column_step, 0, tm, True))


def _proj_call(x2, sc, sh, gain1, w16, w32, col_gain, col_flag, seq):
    t, d = x2.shape
    tm = min(1024, seq)
    tn = 1280 if W16 % 1280 == 0 else 1024
    assert W16 // tn >= 2
    rows_per_batch = seq // tm
    return pl.pallas_call(
        functools.partial(_proj_kernel, tn=tn),
        grid=(t // tm, W16 // tn),
        in_specs=[pl.BlockSpec((tm, d), lambda i, j: (i, 0)),
                  pl.BlockSpec((1, 1, d), lambda i, j: (i // rows_per_batch, 0, 0)),
                  pl.BlockSpec((1, 1, d), lambda i, j: (i // rows_per_batch, 0, 0)),
                  pl.BlockSpec((1, d), lambda i, j: (0, 0)),
                  pl.BlockSpec((d, tn), lambda i, j: (0, j)),
                  pl.BlockSpec((d, W32), lambda i, j: (0, 0)),
                  pl.BlockSpec((1, tn), lambda i, j: (0, j)),
                  pl.BlockSpec((1, tn), lambda i, j: (0, j))],
        out_specs=[pl.BlockSpec((tm, tn), lambda i, j: (i, j)),
                   pl.BlockSpec((tm, W32), lambda i, j: (i, 0))],
        out_shape=[jax.ShapeDtypeStruct((t, W16), BF16),
                   jax.ShapeDtypeStruct((t, W32), F32)],
        scratch_shapes=[pltpu.VMEM((tm, d), BF16)],
        compiler_params=_params(2),
        name="in_proj",
    )(x2, sc, sh, gain1, w16, w32, col_gain, col_flag)


def _split3(c):
    hi = c.astype(BF16).astype(F32)
    r1 = c - hi
    mid = r1.astype(BF16).astype(F32)
    return hi, mid, (r1 - mid).astype(BF16).astype(F32)


def _cum_kernel(ff_ref, fb_ref, o_ref, *, seq, groups):
    ri = lax.broadcasted_iota(jnp.int32, (LANES, LANES), 0)
    ci = lax.broadcasted_iota(jnp.int32, (LANES, LANES), 1)
    tri = jnp.where(ri >= ci, 1.0, 0.0).astype(BF16)
    carry = jnp.zeros((1, LANES), F32)
    for blk in range(seq // LANES):
        rows = slice(blk * LANES, (blk + 1) * LANES)
        x = ff_ref[0, rows, :] + fb_ref[...]
        lf = jnp.minimum(x, 0.0) - jnp.log(1.0 + jnp.exp(-jnp.abs(x)))
        hi = lf.astype(BF16)
        r1 = lf - hi.astype(F32)
        mid = r1.astype(BF16)
        lo = (r1 - mid.astype(F32)).astype(BF16)
        c = (jnp.dot(tri, hi, preferred_element_type=F32)
             + jnp.dot(tri, mid, preferred_element_type=F32)
             + jnp.dot(tri, lo, preferred_element_type=F32)) + carry
        carry = c[LANES - 1:LANES, :]
        terms = _split3(c * LOG2E_HI + c * LOG2E_LO)
        per = N_FOX_HEADS // groups
        for g in range(groups):
            packed = jnp.zeros((LANES, LANES), F32)
            for t, term in enumerate(terms):
                shift = (t * per - g * per) % LANES
                moved = term if shift == 0 else pltpu.roll(term, shift, 1)
                packed = jnp.where(jnp.logical_and(ci >= t * per, ci < (t + 1) * per), moved, packed)
            o_ref[0, g, rows, :] = packed


def _cum_call(o32, fb_pad, batch, seq, groups):
    o32v = o32.reshape(batch, seq, W32)
    return pl.pallas_call(
        functools.partial(_cum_kernel, seq=seq, groups=groups),
        grid=(batch,),
        in_specs=[pl.BlockSpec((1, seq, LANES), lambda b: (b, 0, COL_FF // LANES)),
                  pl.BlockSpec((1, LANES), lambda b: (0, 0))],
        out_specs=pl.BlockSpec((1, groups, seq, LANES), lambda b: (b, 0, 0, 0)),
        out_shape=jax.ShapeDtypeStruct((batch, groups, seq, LANES), F32),
        compiler_params=_params(1),
        name="fox_cumsum",
    )(o32v, fb_pad)


def _fox_kernel(q_ref, k_ref, v_ref, cs_ref, o_ref, m_scr, acc_scr, s_scr, kaug_scr, vaug_scr, *, seq, tq, tk, nh):
    n_diag = tq // tk
    lane_q = lax.broadcasted_iota(jnp.int32, (tq, LANES), 1)
    row8 = lax.broadcasted_iota(jnp.int32, (8, tk), 0)
    ri = lax.broadcasted_iota(jnp.int32, (tq, tk), 0)
    ci = lax.broadcasted_iota(jnp.int32, (tq, tk), 1)
    causal = [ci + d * tk <= ri for d in range(n_diag)]
    for c in range(seq // tk):
        rows = slice(c * tk, (c + 1) * tk)
        terms_t = cs_ref[0, 0, rows, :].T
        for j in range(nh):
            hi, mid, lo = [terms_t[t * nh + j:t * nh + j + 1] for t in range(3)]
            tail8 = jnp.where(row8 < 3, 1.0, jnp.where(row8 == 3, -hi, jnp.where(
                row8 == 4, -mid, jnp.where(row8 == 5, -lo, 0.0))))
            kaug_scr[j, c, 0:HEAD_DIM, :] = k_ref[0, rows, j * HEAD_DIM:(j + 1) * HEAD_DIM].T
            kaug_scr[j, c, HEAD_DIM:, :] = jnp.concatenate(
                [tail8, jnp.zeros((HEAD_DIM - 8, tk), F32)], axis=0).astype(BF16)
    for j in range(nh):
        vaug_scr[j, :, 0:HEAD_DIM] = v_ref[0, :, j * HEAD_DIM:(j + 1) * HEAD_DIM]
        vaug_scr[j, :, HEAD_DIM:] = jnp.ones((seq, HEAD_DIM), BF16)
    acc_scr[...] = jnp.zeros(acc_scr.shape, F32)

    def q_body(qi, carry):
        q0 = pl.multiple_of(qi * tq, tq)
        terms = cs_ref[0, 0, pl.ds(q0, tq), :]
        qs = []
        for j in range(nh):
            hi, mid, lo = [terms if t * nh + j == t else pltpu.roll(terms, (t - (t * nh + j)) % LANES, 1)
                           for t in range(3)]
            tail = jnp.where(lane_q == 0, hi, jnp.where(lane_q == 1, mid, jnp.where(lane_q == 2, lo,
                             jnp.where(lane_q < 6, 1.0, 0.0))))
            qs.append(jnp.concatenate([q_ref[0, pl.ds(q0, tq), j * HEAD_DIM:(j + 1) * HEAD_DIM],
                                       tail.astype(BF16)], axis=1))
            m_scr[j] = jnp.full((tq, LANES), NEG, F32)

        all_rows = slice(0, tq)

        def scores(j, ki, rows):
            return jnp.dot(qs[j][rows], kaug_scr[j, ki], preferred_element_type=F32)

        def tile(ki, mask, rows, next_rows):
            k0 = pl.multiple_of(ki * tk, tk)
            for j in range(nh):
                s = s_scr[j, rows, :]
                if mask is not None:
                    s = jnp.where(mask[rows], s, NEG)
                if next_rows is not None:
                    s_scr[j, next_rows, :] = scores(j, ki + 1, next_rows)
                m_prev = m_scr[j, rows, :]
                m_new = jnp.maximum(m_prev, jnp.max(s, axis=1, keepdims=True))
                alpha = jnp.exp2(m_prev - m_new)
                p = jnp.exp2(s - _lane_tile(m_new, tk // LANES))
                pv = jnp.dot(p.astype(BF16), vaug_scr[j, pl.ds(k0, tk), :], preferred_element_type=F32)
                acc_scr[j, rows, :] = _lane_tile(alpha, 2) * acc_scr[j, rows, :] + pv
                m_scr[j, rows, :] = m_new

        def k_body(ki, c2):
            tile(ki, None, all_rows, all_rows)
            return c2

        for j in range(nh):
            s_scr[j] = scores(j, 0, all_rows)
        n_off = qi * n_diag
        lax.fori_loop(0, n_off, k_body, 0)
        for d in range(n_diag):
            nxt = slice((d + 1) * tk, tq) if d + 1 < n_diag else None
            tile(n_off + d, causal[d], slice(d * tk, tq), nxt)
        for j in range(nh):
            acc = acc_scr[j]
            o_ref[0, pl.ds(q0, tq), j * HEAD_DIM:(j + 1) * HEAD_DIM] = acc[:, 0:HEAD_DIM] / acc[:, HEAD_DIM:]
        return carry

    lax.fori_loop(0, seq // tq, q_body, 0)


def _fox_call(o16v, cs, batch, seq):
    nh = FOX_HEADS_PER_STEP
    tq, tk = min(512, seq), min(256, seq)
    groups = N_FOX_HEADS // nh
    w = nh * HEAD_DIM
    return pl.pallas_call(
        functools.partial(_fox_kernel, seq=seq, tq=tq, tk=tk, nh=nh),
        grid=(batch, groups),
        in_specs=[pl.BlockSpec((1, seq, w), lambda b, h: (b, 0, h)),
                  pl.BlockSpec((1, seq, w), lambda b, h: (b, 0, groups + h)),
                  pl.BlockSpec((1, seq, w), lambda b, h: (b, 0, 2 * groups + h)),
                  pl.BlockSpec((1, 1, seq, LANES), lambda b, h: (b, h, 0, 0))],
        out_specs=pl.BlockSpec((1, seq, w), lambda b, h: (b, 0, h)),
        out_shape=jax.ShapeDtypeStruct((batch, seq, FOX_WIDTH), F32),
        scratch_shapes=[pltpu.VMEM((nh, tq, LANES), F32),
                        pltpu.VMEM((nh, tq, 2 * HEAD_DIM), F32),
                        pltpu.VMEM((nh, tq, tk), F32),
                        pltpu.VMEM((nh, seq // tk, 2 * HEAD_DIM, tk), BF16),
                        pltpu.VMEM((nh, seq, 2 * HEAD_DIM), BF16)],
        compiler_params=_params(2),
        name="fox_attention",
    )(o16v, o16v, o16v, cs)


def _cmp_kernel(x_ref, pos_ref, w1_ref, w2_ref, gain_ref, o_ref, xs_scr, *, seq, ncp):
    rows = xs_scr.shape[1]
    for kv in range(2):
        for h in range(N_NSA_KV_HEADS):
            n = kv * N_NSA_KV_HEADS + h
            xs_scr[n, 0:seq, :] = x_ref[0, :, n * HEAD_DIM:(n + 1) * HEAD_DIM]
            xs_scr[n, seq:rows, :] = jnp.zeros((rows - seq, LANES), F32)
            acc = jnp.zeros((ncp, HEAD_DIM), F32)
            for l in range(CMP_BLOCK):
                xl = xs_scr[n, pl.ds(l, ncp, stride=CMP_STRIDE), :] + pos_ref[kv, l:l + 1, :]
                acc = acc + jnp.dot(xl.astype(BF16), w1_ref[kv, l], preferred_element_type=F32)
            hmid = acc * _sigmoid(acc)
            y = jnp.dot(hmid.astype(BF16), w2_ref[kv], preferred_element_type=F32)
            if kv == 0:
                y = _rms(y) * gain_ref[...]
            o_ref[0, kv, h] = y.astype(BF16)


def _cmp_call(o32, pos, w1, w2, gain, batch, seq, ncp):
    o32v = o32.reshape(batch, seq, W32)
    hkv = N_NSA_KV_HEADS
    width = 2 * NSA_KV_WIDTH
    return pl.pallas_call(
        functools.partial(_cmp_kernel, seq=seq, ncp=ncp),
        grid=(batch,),
        in_specs=[pl.BlockSpec((1, seq, width), lambda b: (b, 0, 0)),
                  pl.BlockSpec((2, CMP_BLOCK, HEAD_DIM), lambda b: (0, 0, 0)),
                  pl.BlockSpec((2, CMP_BLOCK, HEAD_DIM, HEAD_DIM), lambda b: (0, 0, 0, 0)),
                  pl.BlockSpec((2, HEAD_DIM, HEAD_DIM), lambda b: (0, 0, 0)),
                  pl.BlockSpec((1, HEAD_DIM), lambda b: (0, 0))],
        out_specs=pl.BlockSpec((1, 2, hkv, ncp, HEAD_DIM), lambda b: (b, 0, 0, 0, 0)),
        out_shape=jax.ShapeDtypeStruct((batch, 2, hkv, ncp, HEAD_DIM), BF16),
        scratch_shapes=[pltpu.VMEM((2 * hkv, CMP_STRIDE * ncp + CMP_BLOCK, LANES), F32)],
        compiler_params=_params(1),
        name="nsa_compress",
    )(o32v, pos, w1, w2, gain)


def _bias_kernel(rb_ref, o_ref, *, width, key_stride, key_offset, first_tile, rolled_tiles):
    v = pl.program_id(0) + first_tile
    i = lax.broadcasted_iota(jnp.int32, (LANES, width), 0)
    j = lax.broadcasted_iota(jnp.int32, (LANES, width), 1)
    d = v * LANES + i - (key_stride * j + key_offset)
    n = jnp.maximum(d, 0)
    max_exact = N_BUCKETS // 2
    nf = jnp.maximum(n, 1).astype(F32)
    large = max_exact + jnp.trunc(jnp.log(nf / max_exact) / math.log(MAX_DISTANCE / max_exact)
                                  * (N_BUCKETS - max_exact))
    large = jnp.minimum(large, float(N_BUCKETS - 1))
    bkt = jnp.where(n < max_exact, n.astype(F32), large)
    vals = [jnp.zeros((LANES, width), F32) for _ in range(N_NSA_HEADS)]
    for bk in range(N_BUCKETS):
        hit = bkt == float(bk)
        for h in range(N_NSA_HEADS):
            vals[h] = jnp.where(hit, rb_ref[bk * N_NSA_HEADS + h] * LOG2E, vals[h])
    for h in range(N_NSA_HEADS):
        g = h % NSA_GROUP
        rows = slice(g * LANES, (g + 1) * LANES)
        if rolled_tiles:
            per_tile = LANES // key_stride
            for t in range(rolled_tiles):
                shift = (width - per_tile * (rolled_tiles - 1 - t)) % width
                o_ref[t, h // NSA_GROUP, rows, :] = vals[h] if shift == 0 else pltpu.roll(vals[h], shift, 1)
        else:
            o_ref[0, h // NSA_GROUP, rows, :] = vals[h]


def _bias_call(rb_flat, n_tiles, width, key_stride, key_offset, name, rolled=False):
    kern = functools.partial(_bias_kernel, width=width, key_stride=key_stride, key_offset=key_offset,
                             first_tile=n_tiles - 1 if rolled else 0, rolled_tiles=n_tiles if rolled else 0)
    block = (n_tiles if rolled else 1, N_NSA_KV_HEADS, GROUP_ROWS, width)
    return pl.pallas_call(
        kern,
        grid=(1 if rolled else n_tiles,),
        in_specs=[pl.BlockSpec(memory_space=pltpu.SMEM)],
        out_specs=pl.BlockSpec(block, lambda v: (v, 0, 0, 0)),
        out_shape=jax.ShapeDtypeStruct((n_tiles, N_NSA_KV_HEADS, GROUP_ROWS, width), F32),
        compiler_params=_params(1),
        name=name,
    )(rb_flat)


def _tile4(a):
    return jnp.concatenate([a] * NSA_GROUP, axis=0)


def _nsa_kernel(q_ref, ks_ref, kw_ref, vs_ref, vw_ref, g_ref, kcvc_ref, bc_ref, tz_ref, selmt_ref,
                wa_ref, wb_ref, wc_ref, wd_ref, o_ref, wa_out, wb_out, wc_out, wd_out,
                qg_scr, qw_scr, caug_scr, near_scr, edge_scr, m_scr, acc_scr, o_scr, s_scr,
                ksa_scr, vsa_scr, kwp_scr, vwa_scr, *, seq, ncp):
    qt = pl.program_id(1)
    q0 = qt * LANES
    n_slc = seq // SEL_BLOCK
    top_n = min(N_SEL, n_slc)
    nkv = N_NSA_KV_HEADS
    for src, dst in ((wa_ref, wa_out), (wb_ref, wb_out), (wc_ref, wc_out), (wd_ref, wd_out)):
        dst[...] = src[...].astype(BF16)
    ri = lax.broadcasted_iota(jnp.int32, (LANES, LANES), 0)
    ci = lax.broadcasted_iota(jnp.int32, (LANES, LANES), 1)
    eye = jnp.where(ri == ci, 1.0, 0.0).astype(BF16)
    gates = _sigmoid(g_ref[0])

    def gate_col(hk, br):
        cols = []
        for g in range(NSA_GROUP):
            c = COL_NG % LANES + (hk * NSA_GROUP + g) * N_BRANCH + br
            cols.append(gates[:, c:c + 1])
        return jnp.concatenate(cols, axis=0)

    @pl.when(qt == 0)
    def _():
        ones = jnp.ones((seq, HEAD_DIM), BF16)
        row = lax.broadcasted_iota(jnp.int32, (LANES, LANES), 0)
        is_far = jnp.logical_and(row >= AUG_FAR, row < AUG_FAR + 3)
        pad_aug = jnp.where(row == AUG_PAD, -MASK_BIG, jnp.where(is_far, 1.0, 0.0)).astype(BF16)
        win_aug = jnp.where(is_far, 1.0, 0.0).astype(BF16)
        lane512 = lax.broadcasted_iota(jnp.int32, (GROUP_ROWS, LANES), 1)
        causal = _tile4(jnp.where(ci <= ri, 0.0, -MASK_BIG))
        acc_scr[...] = jnp.zeros(acc_scr.shape, F32)
        edge_scr[...] = jnp.concatenate([jnp.full((GROUP_ROWS, LANES), -MASK_BIG, F32),
                                         _tile4(jnp.where(ri < ci, 0.0, -MASK_BIG))], axis=1)
        for hk in range(nkv):
            hc = slice(hk * HEAD_DIM, (hk + 1) * HEAD_DIM)
            far = tz_ref[2, hk]
            hi, mid, lo = _split3(far)
            caug = jnp.where(lane512 == AUG_FAR, hi, jnp.where(lane512 == AUG_FAR + 1, mid,
                             jnp.where(lane512 == AUG_FAR + 2, lo, jnp.where(lane512 == AUG_PAD, 1.0, 0.0))))
            caug_scr[hk] = caug.astype(BF16)
            qw_scr[hk, :, HEAD_DIM:] = caug.astype(BF16)
            near_scr[hk] = jnp.concatenate([tz_ref[1, hk] - far, (tz_ref[0, hk] - far) + causal], axis=1)
            ksa_scr[hk, 0, 0:HEAD_DIM, :] = jnp.zeros((HEAD_DIM, LANES), BF16)
            ksa_scr[hk, 0, HEAD_DIM:, :] = pad_aug
            for t in range(seq // LANES):
                rows = slice(t * LANES, (t + 1) * LANES)
                ksa_scr[hk, SEL_PAD_TILES + t, 0:HEAD_DIM, :] = ks_ref[0, rows, hc].T
                ksa_scr[hk, SEL_PAD_TILES + t, HEAD_DIM:, :] = jnp.where(
                    (t * LANES + ci) // SEL_BLOCK == ri, -MASK_BIG, jnp.where(is_far, 1.0, 0.0)).astype(BF16)
                kwp_scr[hk, WIN_PAD // LANES + t, 0:HEAD_DIM, :] = kw_ref[0, rows, hc].T
                kwp_scr[hk, WIN_PAD // LANES + t, HEAD_DIM:, :] = win_aug
            for t in range(WIN_PAD // LANES):
                kwp_scr[hk, t, 0:HEAD_DIM, :] = jnp.zeros((HEAD_DIM, LANES), BF16)
                kwp_scr[hk, t, HEAD_DIM:, :] = pad_aug
            vsa_scr[hk, 0:SEL_PAD_TILES * LANES, :] = jnp.zeros((SEL_PAD_TILES * LANES, 2 * HEAD_DIM), BF16)
            vsa_scr[hk, SEL_PAD_TILES * LANES:, 0:HEAD_DIM] = vs_ref[0, :, hc]
            vsa_scr[hk, SEL_PAD_TILES * LANES:, HEAD_DIM:] = ones
            vwa_scr[hk, 0:WIN_PAD, :] = jnp.zeros((WIN_PAD, 2 * HEAD_DIM), BF16)
            vwa_scr[hk, WIN_PAD:, 0:HEAD_DIM] = vw_ref[0, :, hc]
            vwa_scr[hk, WIN_PAD:, HEAD_DIM:] = ones

    def reset(br):
        m_scr[br] = jnp.full(m_scr.shape[1:], NEG, F32)

    def online_update(br, hk, s, vaug):
        m_prev = m_scr[br, hk]
        m_new = jnp.maximum(m_prev, jnp.max(s, axis=1, keepdims=True))
        alpha = jnp.exp2(m_prev - m_new)
        p = jnp.exp2(s - _lane_tile(m_new, s.shape[1] // LANES))
        pv = jnp.dot(p.astype(BF16), vaug, preferred_element_type=F32)
        acc_scr[br, hk] = _lane_tile(alpha, 2) * acc_scr[br, hk] + pv
        m_scr[br, hk] = m_new

    def finish(br, hk):
        acc = acc_scr[br, hk]
        return acc[:, 0:HEAD_DIM] / acc[:, HEAD_DIM:]

    for hk in range(nkv):
        for g in range(NSA_GROUP):
            h = hk * NSA_GROUP + g
            q_h = q_ref[0, :, h * HEAD_DIM:(h + 1) * HEAD_DIM]
            qg_scr[hk, g * LANES:(g + 1) * LANES, 0:HEAD_DIM] = q_h
            qw_scr[hk, g * LANES:(g + 1) * LANES, 0:HEAD_DIM] = q_h

    for hk in range(nkv):
        qg = qg_scr[hk, :, 0:HEAD_DIM]

        kc = kcvc_ref[0, 0, hk]
        vc = kcvc_ref[0, 1, hk]
        s = lax.dot_general(qg, kc, NT_DIMS, preferred_element_type=F32) + bc_ref[0, hk]
        rc = lax.broadcasted_iota(jnp.int32, (LANES, ncp), 0)
        cc = lax.broadcasted_iota(jnp.int32, (LANES, ncp), 1)
        valid_c = _tile4(jnp.where(q0 + rc - (CMP_STRIDE * cc + CMP_BLOCK - 1) >= 0, 1.0, 0.0)) > 0.5
        s = jnp.where(valid_c, s, NEG)
        p = jnp.where(valid_c, jnp.exp2(s - jnp.max(s, axis=1, keepdims=True)), 0.0)
        l = jnp.sum(p, axis=1, keepdims=True)
        p = p / jnp.where(l > 0.0, l, 1.0)
        o_scr[hk] = gate_col(hk, 0) * jnp.dot(p.astype(BF16), vc, preferred_element_type=F32)

        psum = p[0:LANES]
        for g in range(1, NSA_GROUP):
            psum = psum + p[g * LANES:(g + 1) * LANES]
        p_hi = psum.astype(BF16)
        p_lo = (psum - p_hi.astype(F32)).astype(BF16)
        selmt = selmt_ref[...]
        imp = (lax.dot_general(selmt, p_hi, NT_DIMS, preferred_element_type=F32)
               + lax.dot_general(selmt, p_lo, NT_DIMS, preferred_element_type=F32))
        imp = imp[0:n_slc]
        blk = lax.broadcasted_iota(jnp.int32, (n_slc, LANES), 0)
        cur = (q0 + lax.broadcasted_iota(jnp.int32, (n_slc, LANES), 1)) // SEL_BLOCK
        forced = (blk == 0) | (blk == cur) | (blk == cur - 1)
        imp = jnp.where(forced, FORCE, imp)
        imp = jnp.where(blk <= cur, imp, -jnp.inf)
        rank = jnp.zeros((n_slc, LANES), F32)
        for j in range(n_slc):
            row = imp[j:j + 1, :]
            beats = jnp.where(row > imp, 1.0, jnp.where(row == imp, jnp.where(blk > j, 1.0, 0.0), 0.0))
            rank = rank + beats
        sel_t = jnp.where(rank < top_n, jnp.where(imp > -jnp.inf, 1.0, 0.0), 0.0)
        if n_slc < LANES:
            sel_t = jnp.concatenate([sel_t, jnp.zeros((LANES - n_slc, LANES), F32)], axis=0)
        sel_q = lax.dot_general(eye, sel_t.astype(BF16), NT_DIMS, preferred_element_type=F32)
        not_sel = jnp.where(ci < n_slc, 1.0 - sel_q, 0.0).astype(BF16)
        for g in range(NSA_GROUP):
            rows = slice(g * LANES, (g + 1) * LANES)
            qg_scr[hk, rows, HEAD_DIM:] = jnp.where(ci < n_slc, not_sel, caug_scr[hk, rows, :])

    reset(WIN)
    for off, table in ((1, near_scr), (3, None), (5, edge_scr)):
        t0 = qt - off + WIN_PAD // LANES
        p0 = pl.multiple_of(t0 * LANES, LANES)
        for hk in range(nkv):
            k_t = jnp.concatenate([kwp_scr[hk, t0], kwp_scr[hk, t0 + 1]], axis=1)
            s = jnp.dot(qw_scr[hk], k_t, preferred_element_type=F32)
            if table is not None:
                s = s + (table[hk] if table is near_scr else table[...])
            online_update(WIN, hk, s, vwa_scr[hk, pl.ds(p0, KEY_CHUNK), :])

    reset(SEL)
    n_chunks = (qt + 2) // 2
    first_tile = SEL_PAD_TILES - (qt + 1) % 2

    def sel_scores(hk, c):
        t0 = first_tile + 2 * c
        k_t = jnp.concatenate([ksa_scr[hk, t0], ksa_scr[hk, t0 + 1]], axis=1)
        return jnp.dot(qg_scr[hk], k_t, preferred_element_type=F32)

    def sel_chunk(c, last):
        p0 = pl.multiple_of((first_tile + 2 * c) * LANES, LANES)
        for hk in range(nkv):
            s = s_scr[hk]
            if last:
                s = s + near_scr[hk]
            else:
                s_scr[hk] = sel_scores(hk, c + 1)
            online_update(SEL, hk, s, vsa_scr[hk, pl.ds(p0, KEY_CHUNK), :])

    def sel_pair(i, carry):
        sel_chunk(2 * i, False)
        sel_chunk(2 * i + 1, False)
        return carry

    for hk in range(nkv):
        s_scr[hk] = sel_scores(hk, 0)
    n_far = n_chunks - 1
    lax.fori_loop(0, n_far // 2, sel_pair, 0)
    pl.when(n_far % 2 == 1)(lambda: sel_chunk(n_far - 1, False))
    sel_chunk(n_chunks - 1, True)
    for hk in range(nkv):
        o = (o_scr[hk] + gate_col(hk, 1) * finish(SEL, hk)) + gate_col(hk, 2) * finish(WIN, hk)
        for g in range(NSA_GROUP):
            h = hk * NSA_GROUP + g
            o_ref[0, :, h * HEAD_DIM:(h + 1) * HEAD_DIM] = o[g * LANES:(g + 1) * LANES]


def _nsa_call(o16v, o32v, kcvc, bias_c, tz, selmt, weights_f32, batch, seq, ncp):
    nqt = seq // LANES
    wspecs = []
    for w in weights_f32:
        blk, span = _cast_block(w, batch * nqt)
        wspecs.append(pl.BlockSpec(blk, functools.partial(lambda b, t, span: ((b * nqt + t) // span, 0), span=span)))
    kvw = NSA_KV_WIDTH
    nkv = N_NSA_KV_HEADS
    base = (3 * FOX_WIDTH + NSA_WIDTH) // kvw
    return pl.pallas_call(
        functools.partial(_nsa_kernel, seq=seq, ncp=ncp),
        grid=(batch, nqt),
        in_specs=[pl.BlockSpec((1, LANES, NSA_WIDTH), lambda b, t: (b, t, 3 * FOX_WIDTH // NSA_WIDTH)),
                  pl.BlockSpec((1, seq, kvw), lambda b, t: (b, 0, base)),
                  pl.BlockSpec((1, seq, kvw), lambda b, t: (b, 0, base + 1)),
                  pl.BlockSpec((1, seq, kvw), lambda b, t: (b, 0, base + 2)),
                  pl.BlockSpec((1, seq, kvw), lambda b, t: (b, 0, base + 3)),
                  pl.BlockSpec((1, LANES, LANES), lambda b, t: (b, t, COL_NG // LANES)),
                  pl.BlockSpec((1, 2, N_NSA_KV_HEADS, ncp, HEAD_DIM), lambda b, t: (b, 0, 0, 0, 0)),
                  pl.BlockSpec((1, N_NSA_KV_HEADS, GROUP_ROWS, ncp), lambda b, t: (t, 0, 0, 0)),
                  pl.BlockSpec((3, N_NSA_KV_HEADS, GROUP_ROWS, LANES), lambda b, t: (0, 0, 0, 0)),
                  pl.BlockSpec((LANES, ncp), lambda b, t: (0, 0))] + wspecs,
        out_specs=[pl.BlockSpec((1, LANES, NSA_WIDTH), lambda b, t: (b, t, 0))] + wspecs,
        out_shape=[jax.ShapeDtypeStruct((batch, seq, NSA_WIDTH), F32)]
        + [jax.ShapeDtypeStruct(w.shape, BF16) for w in weights_f32],
        scratch_shapes=[pltpu.VMEM((nkv, GROUP_ROWS, 2 * HEAD_DIM), BF16),
                        pltpu.VMEM((nkv, GROUP_ROWS, 2 * HEAD_DIM), BF16),
                        pltpu.VMEM((nkv, GROUP_ROWS, LANES), BF16),
                        pltpu.VMEM((nkv, GROUP_ROWS, KEY_CHUNK), F32),
                        pltpu.VMEM((GROUP_ROWS, KEY_CHUNK), F32),
                        pltpu.VMEM((2, nkv, GROUP_ROWS, LANES), F32),
                        pltpu.VMEM((2, nkv, GROUP_ROWS, 2 * HEAD_DIM), F32),
                        pltpu.VMEM((nkv, GROUP_ROWS, HEAD_DIM), F32),
                        pltpu.VMEM((nkv, GROUP_ROWS, KEY_CHUNK), F32),
                        pltpu.VMEM((nkv, SEL_PAD_TILES + seq // LANES, 2 * HEAD_DIM, LANES), BF16),
                        pltpu.VMEM((nkv, SEL_PAD_TILES * LANES + seq, 2 * HEAD_DIM), BF16),
                        pltpu.VMEM((nkv, (seq + WIN_PAD) // LANES, 2 * HEAD_DIM, LANES), BF16),
                        pltpu.VMEM((nkv, seq + WIN_PAD, 2 * HEAD_DIM), BF16)],
        compiler_params=_params(2),
        name="nsa_attention",
    )(o16v, o16v, o16v, o16v, o16v, o32v, kcvc, bias_c, tz, selmt, *weights_f32)


def _out_kernel(of_ref, on_ref, gain_ref, w_ref, x_ref, g_ref, o_ref, y_scr, *, tm, n_chains):
    rows_per_chain = tm // n_chains
    gain_f = gain_ref[:, 0:FOX_WIDTH]
    gain_n = gain_ref[:, FOX_WIDTH:MIX_WIDTH]
    for c in range(n_chains):
        rows = slice(c * rows_per_chain, (c + 1) * rows_per_chain)
        for r0 in range(c * rows_per_chain, (c + 1) * rows_per_chain, NORM_ROWS):
            r = slice(r0, r0 + NORM_ROWS)
            y_scr[r, 0:FOX_WIDTH] = (_rms(of_ref[r, :]) * gain_f).astype(BF16)
            y_scr[r, FOX_WIDTH:MIX_WIDTH] = (_rms(on_ref[r, :]) * gain_n).astype(BF16)
        acc = jnp.dot(y_scr[rows, :], w_ref[...], preferred_element_type=F32)
        o_ref[rows, :] = x_ref[rows, :] + g_ref[0] * acc


def _out_call(o_fox, o_nsa, gain, w_out, x2, g1, seq):
    t, d = x2.shape
    tm = min(512, seq)
    rows_per_batch = seq // tm
    return pl.pallas_call(
        functools.partial(_out_kernel, tm=tm, n_chains=2),
        grid=(t // tm,),
        in_specs=[pl.BlockSpec((tm, FOX_WIDTH), lambda i: (i, 0)),
                  pl.BlockSpec((tm, NSA_WIDTH), lambda i: (i, 0)),
                  pl.BlockSpec((1, MIX_WIDTH), lambda i: (0, 0)),
                  pl.BlockSpec((MIX_WIDTH, d), lambda i: (0, 0)),
                  pl.BlockSpec((tm, d), lambda i: (i, 0)),
                  pl.BlockSpec((1, 1, d), lambda i: (i // rows_per_batch, 0, 0))],
        out_specs=pl.BlockSpec((tm, d), lambda i: (i, 0)),
        out_shape=jax.ShapeDtypeStruct((t, d), F32),
        scratch_shapes=[pltpu.VMEM((tm, MIX_WIDTH), BF16)],
        compiler_params=_params(1),
        name="out_proj",
    )(o_fox, o_nsa, gain, w_out, x2, g1)


def _ffn_kernel(x_ref, gain_ref, sc_ref, sh_ref, g_ref, wg_ref, wu_ref, wd_ref, o_ref, h_scr, *, n_f):
    f = pl.program_id(1)
    tm = h_scr.shape[0]

    def hidden_step(first_row, n_rows, first, last):
        rows = slice(first_row, first_row + n_rows)
        h = h_scr[rows, :]
        a = jnp.dot(h, wg_ref[...], preferred_element_type=F32)
        u = jnp.dot(h, wu_ref[...], preferred_element_type=F32)
        t = (a * _sigmoid(a)) * u
        part = jnp.dot(t.astype(BF16), wd_ref[...], preferred_element_type=F32)
        acc = part if first else o_ref[rows, :] + part
        o_ref[rows, :] = x_ref[rows, :] + g_ref[0] * acc if last else acc

    def edge_step(first, last):
        n_rows = tm // ROW_CHAINS
        for c in range(ROW_CHAINS):
            if first:
                _modulated_norm(x_ref, gain_ref, sc_ref, sh_ref, h_scr, c * n_rows, n_rows)
            hidden_step(c * n_rows, n_rows, first, last)

    if n_f == 1:
        edge_step(True, True)
    else:
        pl.when(f == 0)(functools.partial(edge_step, True, False))
        pl.when(jnp.logical_and(f > 0, f < n_f - 1))(functools.partial(hidden_step, 0, tm, False, False))
        pl.when(f == n_f - 1)(functools.partial(edge_step, False, True))


def _ffn_call(x1, gain2, sc, sh, g2, wg, wu, wd, seq):
    t, d = x1.shape
    dff = wg.shape[1]
    tm = min(1024, seq)
    tf = 512 if dff % 512 == 0 else dff
    rows_per_batch = seq // tm
    return pl.pallas_call(
        functools.partial(_ffn_kernel, n_f=dff // tf),
        grid=(t // tm, dff // tf),
        in_specs=[pl.BlockSpec((tm, d), lambda i, f: (i, 0)),
                  pl.BlockSpec((1, d), lambda i, f: (0, 0)),
                  pl.BlockSpec((1, 1, d), lambda i, f: (i // rows_per_batch, 0, 0)),
                  pl.BlockSpec((1, 1, d), lambda i, f: (i // rows_per_batch, 0, 0)),
                  pl.BlockSpec((1, 1, d), lambda i, f: (i // rows_per_batch, 0, 0)),
                  pl.BlockSpec((d, tf), lambda i, f: (0, f)),
                  pl.BlockSpec((d, tf), lambda i, f: (0, f)),
                  pl.BlockSpec((tf, d), lambda i, f: (f, 0))],
        out_specs=pl.BlockSpec((tm, d), lambda i, f: (i, 0)),
        out_shape=jax.ShapeDtypeStruct((t, d), F32),
        scratch_shapes=[pltpu.VMEM((tm, d), BF16)],
        compiler_params=_params(2),
        name="swiglu_ffn",
    )(x1, gain2, sc, sh, g2, wg, wu, wd)


def _selection_matrix_t(ncp, n_slc):
    r, q = SEL_BLOCK // CMP_STRIDE, CMP_BLOCK // CMP_STRIDE
    m = np.zeros((LANES, ncp), np.float32)
    for j in range(n_slc):
        for a in range(r):
            for b in range(q):
                c = r * j + a - b
                if 0 <= c < ncp:
                    m[j, c] += 1.0
    return m


def _w_in_block_sources():
    o = 0
    start = {}
    for name, width in (("fq", FOX_WIDTH), ("fk", FOX_WIDTH), ("fv", FOX_WIDTH), ("ff", N_FOX_HEADS),
                        ("nq", NSA_WIDTH), ("nk", N_BRANCH * NSA_KV_WIDTH), ("nv", N_BRANCH * NSA_KV_WIDTH),
                        ("ng", N_BRANCH * N_NSA_HEADS)):
        start[name] = o
        o += width
    kvw = NSA_KV_WIDTH
    groups = [(start["fq"], 3 * FOX_WIDTH), (start["nq"], NSA_WIDTH),
              (start["nk"] + kvw, 2 * kvw), (start["nv"] + kvw, 2 * kvw),
              (start["nk"], kvw), (start["nv"], kvw)]
    blocks = [s + LANES * b for s, width in groups for b in range(width // LANES)]
    return blocks, (start["ff"], N_FOX_HEADS), (start["ng"], N_BRANCH * N_NSA_HEADS)


def _repack_kernel(src_ref, wt_hbm, o16_ref, o32_ref, buf, sem, *, n16, n_whole, ff, ng):
    k = pl.program_id(0)
    n_slots = buf.shape[0]
    ahead = n_slots - 1
    slot = k % n_slots

    def whole_copy(kk, s):
        r0 = pl.multiple_of(src_ref[kk], 8)
        return pltpu.make_async_copy(wt_hbm.at[pl.ds(r0, LANES), :], buf.at[s], sem.at[s])

    def narrow_copies(s):
        return [pltpu.make_async_copy(wt_hbm.at[pl.ds(ff[0], ff[1]), :], buf.at[s, pl.ds(0, ff[1]), :], sem.at[s]),
                pltpu.make_async_copy(wt_hbm.at[pl.ds(ng[0], ng[1]), :], buf.at[s, pl.ds(ff[1], ng[1]), :],
                                      sem.at[s])]

    def start_block(kk):
        @pl.when(kk < n_whole)
        def _():
            whole_copy(kk, kk % n_slots).start()

        @pl.when(kk == n_whole)
        def _():
            for cp in narrow_copies(kk % n_slots):
                cp.start()

    @pl.when(k == 0)
    def _():
        for kk in range(ahead):
            start_block(kk)

    start_block(k + ahead)

    @pl.when(k < n_whole)
    def _():
        whole_copy(k, slot).wait()

    @pl.when(k == n_whole)
    def _():
        for cp in narrow_copies(slot):
            cp.wait()

    xt = buf[slot].T
    lane = lax.broadcasted_iota(jnp.int32, xt.shape, 1)
    xt = jnp.where(jnp.logical_or(k < n_whole, lane < ff[1] + ng[1]), xt, 0.0).astype(BF16)

    @pl.when(k < n16)
    def _():
        o16_ref[...] = xt

    @pl.when(k >= n16)
    def _():
        o32_ref[...] = xt


def _repack_w_in(wt):
    n, d = wt.shape
    blocks, ff, ng = _w_in_block_sources()
    n16 = W16 // LANES
    n_whole = len(blocks)
    assert n_whole + 1 == (W16 + W32) // LANES
    return pl.pallas_call(
        functools.partial(_repack_kernel, n16=n16, n_whole=n_whole, ff=ff, ng=ng),
        grid_spec=pltpu.PrefetchScalarGridSpec(
            num_scalar_prefetch=1,
            grid=(n_whole + 1,),
            in_specs=[pl.BlockSpec(memory_space=pl.ANY)],
            out_specs=[pl.BlockSpec((d, LANES), lambda k, src: (0, jnp.minimum(k, n16 - 1))),
                       pl.BlockSpec((d, LANES), lambda k, src: (0, jnp.maximum(k - n16, 0)))],
            scratch_shapes=[pltpu.VMEM((REPACK_SLOTS, LANES, d), F32), pltpu.SemaphoreType.DMA((REPACK_SLOTS,))]),
        out_shape=[jax.ShapeDtypeStruct((d, W16), BF16), jax.ShapeDtypeStruct((d, W32), BF16)],
        compiler_params=_params(1),
        name="w_in_repack",
    )(jnp.asarray(blocks, jnp.int32), wt)


def kernel(x, c, ada_w, ada_b, norm1_gain, norm2_gain, w_in, fox_f_bias, fox_q_gain, fox_k_gain, nsa_q_gain,
           nsa_k_gain, nsa_cmp_pos, nsa_cmp_w1, nsa_cmp_w2, rel_bias, mix_out_gain, w_out, ffn_w_gate, ffn_w_up,
           ffn_w_down):
    batch, seq, d = x.shape
    assert seq % KEY_CHUNK == 0 and seq >= WINDOW and d % LANES == 0 and seq // SEL_BLOCK <= LANES
    depth = ada_w.shape[0]
    nqt = seq // LANES
    ncp = -(-(seq // CMP_STRIDE) // LANES) * LANES

    selmt = jnp.asarray(_selection_matrix_t(ncp, seq // SEL_BLOCK), BF16)
    rb_flat = rel_bias.reshape(-1)
    bias_c = _bias_call(rb_flat, nqt, ncp, CMP_STRIDE, CMP_BLOCK - 1, "t5_bias_compressed", rolled=True)
    tz = _bias_call(rb_flat, 3, LANES, 1, 0, "t5_bias_toeplitz")

    ones_h = jnp.ones((HEAD_DIM,), F32)
    c_pad = jnp.pad(c, ((0, 8 - batch % 8 if batch % 8 else 0), (0, 0)))
    x2 = x.reshape(batch * seq, d)
    for layer in range(depth):
        mod = _ada_call(c_pad, ada_w[layer], ada_b[layer][None, :])[:batch]
        sh1, sc1, g1, sh2, sc2, g2 = [mod[:, i * d:(i + 1) * d][:, None, :] for i in range(N_MOD)]

        kg = nsa_k_gain[layer]
        col_gain = jnp.concatenate([
            jnp.tile(fox_q_gain[layer] * QSCALE, N_FOX_HEADS), jnp.tile(fox_k_gain[layer], N_FOX_HEADS),
            jnp.tile(ones_h, N_FOX_HEADS), jnp.tile(nsa_q_gain[layer] * QSCALE, N_NSA_HEADS),
            jnp.tile(kg[1], N_NSA_KV_HEADS), jnp.tile(kg[2], N_NSA_KV_HEADS),
            jnp.tile(ones_h, 2 * N_NSA_KV_HEADS)])[None, :]
        col_flag = jnp.concatenate([
            jnp.ones((2 * FOX_WIDTH,), F32), jnp.zeros((FOX_WIDTH,), F32), jnp.ones((NSA_WIDTH,), F32),
            jnp.ones((2 * NSA_KV_WIDTH,), F32), jnp.zeros((2 * NSA_KV_WIDTH,), F32)])[None, :]
        w16, w32 = _repack_w_in(jnp.swapaxes(w_in, 1, 2)[layer])
        o16, o32 = _proj_call(x2, sc1, sh1, norm1_gain[layer][None, :], w16, w32,
                              col_gain, col_flag, seq)
        o16v = o16.reshape(batch, seq, W16)
        o32v = o32.reshape(batch, seq, W32)

        fb_pad = jnp.pad(fox_f_bias[layer], (0, LANES - N_FOX_HEADS))[None, :]
        cs = _cum_call(o32, fb_pad, batch, seq, N_FOX_HEADS // FOX_HEADS_PER_STEP)
        o_fox = _fox_call(o16v, cs, batch, seq)

        w1 = nsa_cmp_w1[layer].reshape(2, CMP_BLOCK, HEAD_DIM, HEAD_DIM).astype(BF16)
        kcvc = _cmp_call(o32, nsa_cmp_pos[layer], w1, nsa_cmp_w2[layer].astype(BF16), kg[0][None, :],
                         batch, seq, ncp)
        o_nsa, wg16, wu16, wo16, wd16 = _nsa_call(
            o16v, o32v, kcvc, bias_c, tz, selmt,
            (ffn_w_gate[layer], ffn_w_up[layer], w_out[layer], ffn_w_down[layer]), batch, seq, ncp)

        x1 = _out_call(o_fox.reshape(batch * seq, FOX_WIDTH), o_nsa.reshape(batch * seq, NSA_WIDTH),
                       mix_out_gain[layer][None, :], wo16, x2, g1, seq)
        x2 = _ffn_call(x1, norm2_gain[layer][None, :], sc2, sh2, g2, wg16, wu16, wd16, seq)
    return x2.reshape(batch, seq, d)
```

```python
import functools
import math

import numpy as np
import jax
import jax.numpy as jnp
from jax import lax
from jax.experimental import pallas as pl
from jax.experimental.pallas import tpu as pltpu

HEAD_DIM = 128
N_FOX_HEADS = 8
N_NSA_HEADS = 8
N_NSA_KV_HEADS = 2
NSA_GROUP = N_NSA_HEADS // N_NSA_KV_HEADS
FOX_WIDTH = N_FOX_HEADS * HEAD_DIM
NSA_WIDTH = N_NSA_HEADS * HEAD_DIM
NSA_KV_WIDTH = N_NSA_KV_HEADS * HEAD_DIM
MIX_WIDTH = FOX_WIDTH + NSA_WIDTH
N_BRANCH = 3
CMP_BLOCK = 32
CMP_STRIDE = 16
SEL_BLOCK = 64
N_SEL = 8
WINDOW = 512
N_BUCKETS = 32
MAX_DISTANCE = 128
N_MOD = 6
SCALE = HEAD_DIM ** -0.5
LOG2E = math.log2(math.e)
LOG2E_HI = float(np.float32(LOG2E))
LOG2E_LO = LOG2E - LOG2E_HI
QSCALE = SCALE * LOG2E
EPS = 1e-6
NEG = -1e30
FORCE = 1e6

LANES = 128
GROUP_ROWS = NSA_GROUP * LANES
VMEM_LIMIT = 56 * 1024 * 1024
MXU_COLS = 256
KEY_CHUNK = MXU_COLS
WIN_PAD = WINDOW + LANES
MASK_BIG = 2.0 ** 100
SEL, WIN = 0, 1
REPACK_SLOTS = 4
AUG_FAR, AUG_PAD = 120, 127
FOX_HEADS_PER_STEP = 4
SEL_PAD_TILES = 1

W16 = 3 * FOX_WIDTH + NSA_WIDTH + 4 * NSA_KV_WIDTH
W32 = 5 * LANES
COL_FF = 2 * NSA_KV_WIDTH
COL_NG = COL_FF + N_FOX_HEADS

F32 = jnp.float32
BF16 = jnp.bfloat16
NT_DIMS = (((1,), (1,)), ((), ()))


def _params(n_axes):
    return pltpu.CompilerParams(dimension_semantics=("arbitrary",) * n_axes,
                                vmem_limit_bytes=VMEM_LIMIT)


def _sigmoid(x):
    return 1.0 / (1.0 + jnp.exp(-x))


def _lane_tile(a, n):
    return jnp.concatenate([a] * n, axis=1)


BF16_SUBLANES = 16


def _cast_block(w, n_steps):
    rows, cols = w.shape
    assert rows % n_steps == 0, (w.shape, n_steps)
    per_step = rows // n_steps
    span = BF16_SUBLANES // math.gcd(BF16_SUBLANES, per_step)
    assert n_steps % span == 0, (w.shape, n_steps)
    return (per_step * span, cols), span


def _rms(x):
    return x * lax.rsqrt(jnp.mean(x * x, axis=-1, keepdims=True) + EPS)


NORM_ROWS = 16


def _modulated_norm(x_ref, gain_ref, sc_ref, sh_ref, h_ref, first_row, n_rows):
    gm = gain_ref[...] * (1.0 + sc_ref[0])
    sh = sh_ref[0]
    for r0 in range(first_row, first_row + n_rows, NORM_ROWS):
        rows = slice(r0, r0 + NORM_ROWS)
        h_ref[rows, :] = (_rms(x_ref[rows, :]) * gm + sh).astype(BF16)


ROW_CHAINS = 2


def _ada_kernel(c_ref, w_ref, b_ref, o_ref):
    c = c_ref[...]
    s = (c * _sigmoid(c)).astype(BF16)
    o_ref[...] = jnp.dot(s, w_ref[...].astype(BF16), preferred_element_type=F32) + b_ref[...]


def _ada_call(c_pad, w, b):
    rows, d = c_pad.shape
    n = w.shape[1]
    tn = next(t for t in (1024, 768, 512, 384, 256, 128) if n % t == 0)
    return pl.pallas_call(
        _ada_kernel,
        grid=(n // tn,),
        in_specs=[pl.BlockSpec((rows, d), lambda j: (0, 0)),
                  pl.BlockSpec((d, tn), lambda j: (0, j)),
                  pl.BlockSpec((1, tn), lambda j: (0, j))],
        out_specs=pl.BlockSpec((rows, tn), lambda j: (0, j)),
        out_shape=jax.ShapeDtypeStruct((rows, n), F32),
        compiler_params=_params(1),
        name="adaln",
    )(c_pad, w, b)


def _proj_kernel(x_ref, sc_ref, sh_ref, g_ref, w16_ref, w32_ref, gain_ref, flag_ref, o16_ref, o32_ref, h_scr, *, tn):
    j = pl.program_id(1)
    tm = h_scr.shape[0]

    def column_step(first_row, n_rows, with_side_outputs):
        rows = slice(first_row, first_row + n_rows)
        h = h_scr[rows, :]
        for c in range(tn // MXU_COLS):
            acc = jnp.dot(h, w16_ref[:, c * MXU_COLS:(c + 1) * MXU_COLS], preferred_element_type=F32)
            for g in range(MXU_COLS // LANES):
                cols = slice(c * MXU_COLS + g * LANES, c * MXU_COLS + (g + 1) * LANES)
                a = acc[:, g * LANES:(g + 1) * LANES]
                r = lax.rsqrt(jnp.mean(a * a, axis=-1, keepdims=True) + EPS)
                scale = jnp.where(flag_ref[:, cols] > 0.5, r, 1.0)
                o16_ref[rows, cols] = (a * scale * gain_ref[:, cols]).astype(BF16)
        if with_side_outputs:
            o32_ref[rows, :] = jnp.dot(h, w32_ref[...], preferred_element_type=F32)

    def step(first, with_side_outputs):
        n_rows = tm // ROW_CHAINS
        for c in range(ROW_CHAINS):
            if first:
                _modulated_norm(x_ref, g_ref, sc_ref, sh_ref, h_scr, c * n_rows, n_rows)
            column_step(c * n_rows, n_rows, with_side_outputs)

    last = pl.num_programs(1) - 1
    pl.when(j == 0)(functools.partial(step, True, False))
    pl.when(jnp.logical_and(j > 0, j < last))(functools.partial(step, False, False))
    pl.when(j == last)(functools.partial(step, False, True))


def _proj_call(x2, sc, sh, gain1, w16, w32, col_gain, col_flag, seq):
    t, d = x2.shape
    tm = min(1024, seq)
    tn = 1280 if W16 % 1280 == 0 else 1024
    assert W16 // tn >= 2
    rows_per_batch = seq // tm
    return pl.pallas_call(
        functools.partial(_proj_kernel, tn=tn),
        grid=(t // tm, W16 // tn),
        in_specs=[pl.BlockSpec((tm, d), lambda i, j: (i, 0)),
                  pl.BlockSpec((1, 1, d), lambda i, j: (i // rows_per_batch, 0, 0)),
                  pl.BlockSpec((1, 1, d), lambda i, j: (i // rows_per_batch, 0, 0)),
                  pl.BlockSpec((1, d), lambda i, j: (0, 0)),
                  pl.BlockSpec((d, tn), lambda i, j: (0, j)),
                  pl.BlockSpec((d, W32), lambda i, j: (0, 0)),
                  pl.BlockSpec((1, tn), lambda i, j: (0, j)),
                  pl.BlockSpec((1, tn), lambda i, j: (0, j))],
        out_specs=[pl.BlockSpec((tm, tn), lambda i, j: (i, j)),
                   pl.BlockSpec((tm, W32), lambda i, j: (i, 0))],
        out_shape=[jax.ShapeDtypeStruct((t, W16), BF16),
                   jax.ShapeDtypeStruct((t, W32), F32)],
        scratch_shapes=[pltpu.VMEM((tm, d), BF16)],
        compiler_params=_params(2),
        name="in_proj",
    )(x2, sc, sh, gain1, w16, w32, col_gain, col_flag)


def _split3(c):
    hi = c.astype(BF16).astype(F32)
    r1 = c - hi
    mid = r1.astype(BF16).astype(F32)
    return hi, mid, (r1 - mid).astype(BF16).astype(F32)


def _cum_kernel(ff_ref, fb_ref, o_ref, *, seq, groups):
    ri = lax.broadcasted_iota(jnp.int32, (LANES, LANES), 0)
    ci = lax.broadcasted_iota(jnp.int32, (LANES, LANES), 1)
    tri = jnp.where(ri >= ci, 1.0, 0.0).astype(BF16)
    carry = jnp.zeros((1, LANES), F32)
    for blk in range(seq // LANES):
        rows = slice(blk * LANES, (blk + 1) * LANES)
        x = ff_ref[0, rows, :] + fb_ref[...]
        lf = jnp.minimum(x, 0.0) - jnp.log(1.0 + jnp.exp(-jnp.abs(x)))
        hi = lf.astype(BF16)
        r1 = lf - hi.astype(F32)
        mid = r1.astype(BF16)
        lo = (r1 - mid.astype(F32)).astype(BF16)
        c = (jnp.dot(tri, hi, preferred_element_type=F32)
             + jnp.dot(tri, mid, preferred_element_type=F32)
             + jnp.dot(tri, lo, preferred_element_type=F32)) + carry
        carry = c[LANES - 1:LANES, :]
        terms = _split3(c * LOG2E_HI + c * LOG2E_LO)
        per = N_FOX_HEADS // groups
        for g in range(groups):
            packed = jnp.zeros((LANES, LANES), F32)
            for t, term in enumerate(terms):
                shift = (t * per - g * per) % LANES
                moved = term if shift == 0 else pltpu.roll(term, shift, 1)
                packed = jnp.where(jnp.logical_and(ci >= t * per, ci < (t + 1) * per), moved, packed)
            o_ref[0, g, rows, :] = packed


def _cum_call(o32, fb_pad, batch, seq, groups):
    o32v = o32.reshape(batch, seq, W32)
    return pl.pallas_call(
        functools.partial(_cum_kernel, seq=seq, groups=groups),
        grid=(batch,),
        in_specs=[pl.BlockSpec((1, seq, LANES), lambda b: (b, 0, COL_FF // LANES)),
                  pl.BlockSpec((1, LANES), lambda b: (0, 0))],
        out_specs=pl.BlockSpec((1, groups, seq, LANES), lambda b: (b, 0, 0, 0)),
        out_shape=jax.ShapeDtypeStruct((batch, groups, seq, LANES), F32),
        compiler_params=_params(1),
        name="fox_cumsum",
    )(o32v, fb_pad)


def _fox_kernel(q_ref, k_ref, v_ref, cs_ref, o_ref, m_scr, acc_scr, s_scr, kaug_scr, vaug_scr, *, seq, tq, tk, nh):
    n_diag = tq // tk
    lane_q = lax.broadcasted_iota(jnp.int32, (tq, LANES), 1)
    row8 = lax.broadcasted_iota(jnp.int32, (8, tk), 0)
    ri = lax.broadcasted_iota(jnp.int32, (tq, tk), 0)
    ci = lax.broadcasted_iota(jnp.int32, (tq, tk), 1)
    causal = [ci + d * tk <= ri for d in range(n_diag)]
    for c in range(seq // tk):
        rows = slice(c * tk, (c + 1) * tk)
        terms_t = cs_ref[0, 0, rows, :].T
        for j in range(nh):
            hi, mid, lo = [terms_t[t * nh + j:t * nh + j + 1] for t in range(3)]
            tail8 = jnp.where(row8 < 3, 1.0, jnp.where(row8 == 3, -hi, jnp.where(
                row8 == 4, -mid, jnp.where(row8 == 5, -lo, 0.0))))
            kaug_scr[j, c, 0:HEAD_DIM, :] = k_ref[0, rows, j * HEAD_DIM:(j + 1) * HEAD_DIM].T
            kaug_scr[j, c, HEAD_DIM:, :] = jnp.concatenate(
                [tail8, jnp.zeros((HEAD_DIM - 8, tk), F32)], axis=0).astype(BF16)
    for j in range(nh):
        vaug_scr[j, :, 0:HEAD_DIM] = v_ref[0, :, j * HEAD_DIM:(j + 1) * HEAD_DIM]
        vaug_scr[j, :, HEAD_DIM:] = jnp.ones((seq, HEAD_DIM), BF16)
    acc_scr[...] = jnp.zeros(acc_scr.shape, F32)

    def q_body(qi, carry):
        q0 = pl.multiple_of(qi * tq, tq)
        terms = cs_ref[0, 0, pl.ds(q0, tq), :]
        qs = []
        for j in range(nh):
            hi, mid, lo = [terms if t * nh + j == t else pltpu.roll(terms, (t - (t * nh + j)) % LANES, 1)
                           for t in range(3)]
            tail = jnp.where(lane_q == 0, hi, jnp.where(lane_q == 1, mid, jnp.where(lane_q == 2, lo,
                             jnp.where(lane_q < 6, 1.0, 0.0))))
            qs.append(jnp.concatenate([q_ref[0, pl.ds(q0, tq), j * HEAD_DIM:(j + 1) * HEAD_DIM],
                                       tail.astype(BF16)], axis=1))
            m_scr[j] = jnp.full((tq, LANES), NEG, F32)

        all_rows = slice(0, tq)

        def scores(j, ki, rows):
            return jnp.dot(qs[j][rows], kaug_scr[j, ki], preferred_element_type=F32)

        def tile(ki, mask, rows, next_rows):
            k0 = pl.multiple_of(ki * tk, tk)
            for j in range(nh):
                s = s_scr[j, rows, :]
                if mask is not None:
                    s = jnp.where(mask[rows], s, NEG)
                if next_rows is not None:
                    s_scr[j, next_rows, :] = scores(j, ki + 1, next_rows)
                m_prev = m_scr[j, rows, :]
                m_new = jnp.maximum(m_prev, jnp.max(s, axis=1, keepdims=True))
                alpha = jnp.exp2(m_prev - m_new)
                p = jnp.exp2(s - _lane_tile(m_new, tk // LANES))
                pv = jnp.dot(p.astype(BF16), vaug_scr[j, pl.ds(k0, tk), :], preferred_element_type=F32)
                acc_scr[j, rows, :] = _lane_tile(alpha, 2) * acc_scr[j, rows, :] + pv
                m_scr[j, rows, :] = m_new

        def k_body(ki, c2):
            tile(ki, None, all_rows, all_rows)
            return c2

        for j in range(nh):
            s_scr[j] = scores(j, 0, all_rows)
        n_off = qi * n_diag
        lax.fori_loop(0, n_off, k_body, 0)
        for d in range(n_diag):
            nxt = slice((d + 1) * tk, tq) if d + 1 < n_diag else None
            tile(n_off + d, causal[d], slice(d * tk, tq), nxt)
        for j in range(nh):
            acc = acc_scr[j]
            o_ref[0, pl.ds(q0, tq), j * HEAD_DIM:(j + 1) * HEAD_DIM] = acc[:, 0:HEAD_DIM] / acc[:, HEAD_DIM:]
        return carry

    lax.fori_loop(0, seq // tq, q_body, 0)


def _fox_call(o16v, cs, batch, seq):
    nh = FOX_HEADS_PER_STEP
    tq, tk = min(512, seq), min(256, seq)
    groups = N_FOX_HEADS // nh
    w = nh * HEAD_DIM
    return pl.pallas_call(
        functools.partial(_fox_kernel, seq=seq, tq=tq, tk=tk, nh=nh),
        grid=(batch, groups),
        in_specs=[pl.BlockSpec((1, seq, w), lambda b, h: (b, 0, h)),
                  pl.BlockSpec((1, seq, w), lambda b, h: (b, 0, groups + h)),
                  pl.BlockSpec((1, seq, w), lambda b, h: (b, 0, 2 * groups + h)),
                  pl.BlockSpec((1, 1, seq, LANES), lambda b, h: (b, h, 0, 0))],
        out_specs=pl.BlockSpec((1, seq, w), lambda b, h: (b, 0, h)),
        out_shape=jax.ShapeDtypeStruct((batch, seq, FOX_WIDTH), F32),
        scratch_shapes=[pltpu.VMEM((nh, tq, LANES), F32),
                        pltpu.VMEM((nh, tq, 2 * HEAD_DIM), F32),
                        pltpu.VMEM((nh, tq, tk), F32),
                        pltpu.VMEM((nh, seq // tk, 2 * HEAD_DIM, tk), BF16),
                        pltpu.VMEM((nh, seq, 2 * HEAD_DIM), BF16)],
        compiler_params=_params(2),
        name="fox_attention",
    )(o16v, o16v, o16v, cs)


def _cmp_kernel(x_ref, pos_ref, w1_ref, w2_ref, gain_ref, o_ref, xs_scr, *, seq, ncp):
    rows = xs_scr.shape[1]
    for kv in range(2):
        for h in range(N_NSA_KV_HEADS):
            n = kv * N_NSA_KV_HEADS + h
            xs_scr[n, 0:seq, :] = x_ref[0, :, n * HEAD_DIM:(n + 1) * HEAD_DIM]
            xs_scr[n, seq:rows, :] = jnp.zeros((rows - seq, LANES), F32)
            acc = jnp.zeros((ncp, HEAD_DIM), F32)
            for l in range(CMP_BLOCK):
                xl = xs_scr[n, pl.ds(l, ncp, stride=CMP_STRIDE), :] + pos_ref[kv, l:l + 1, :]
                acc = acc + jnp.dot(xl.astype(BF16), w1_ref[kv, l].astype(BF16), preferred_element_type=F32)
            hmid = acc * _sigmoid(acc)
            y = jnp.dot(hmid.astype(BF16), w2_ref[kv].astype(BF16), preferred_element_type=F32)
            if kv == 0:
                y = _rms(y) * gain_ref[...]
            o_ref[0, kv, h] = y.astype(BF16)


def _cmp_call(o32, pos, w1, w2, gain, batch, seq, ncp):
    o32v = o32.reshape(batch, seq, W32)
    hkv = N_NSA_KV_HEADS
    width = 2 * NSA_KV_WIDTH
    return pl.pallas_call(
        functools.partial(_cmp_kernel, seq=seq, ncp=ncp),
        grid=(batch,),
        in_specs=[pl.BlockSpec((1, seq, width), lambda b: (b, 0, 0)),
                  pl.BlockSpec((2, CMP_BLOCK, HEAD_DIM), lambda b: (0, 0, 0)),
                  pl.BlockSpec((2, CMP_BLOCK, HEAD_DIM, HEAD_DIM), lambda b: (0, 0, 0, 0)),
                  pl.BlockSpec((2, HEAD_DIM, HEAD_DIM), lambda b: (0, 0, 0)),
                  pl.BlockSpec((1, HEAD_DIM), lambda b: (0, 0))],
        out_specs=pl.BlockSpec((1, 2, hkv, ncp, HEAD_DIM), lambda b: (b, 0, 0, 0, 0)),
        out_shape=jax.ShapeDtypeStruct((batch, 2, hkv, ncp, HEAD_DIM), BF16),
        scratch_shapes=[pltpu.VMEM((2 * hkv, CMP_STRIDE * ncp + CMP_BLOCK, LANES), F32)],
        compiler_params=_params(1),
        name="nsa_compress",
    )(o32v, pos, w1, w2, gain)


def _bias_kernel(rb_ref, o_ref, *, width, key_stride, key_offset, first_tile, rolled_tiles):
    v = pl.program_id(0) + first_tile
    i = lax.broadcasted_iota(jnp.int32, (LANES, width), 0)
    j = lax.broadcasted_iota(jnp.int32, (LANES, width), 1)
    d = v * LANES + i - (key_stride * j + key_offset)
    n = jnp.maximum(d, 0)
    max_exact = N_BUCKETS // 2
    nf = jnp.maximum(n, 1).astype(F32)
    large = max_exact + jnp.trunc(jnp.log(nf / max_exact) / math.log(MAX_DISTANCE / max_exact)
                                  * (N_BUCKETS - max_exact))
    large = jnp.minimum(large, float(N_BUCKETS - 1))
    bkt = jnp.where(n < max_exact, n.astype(F32), large)
    vals = [jnp.zeros((LANES, width), F32) for _ in range(N_NSA_HEADS)]
    for bk in range(N_BUCKETS):
        hit = bkt == float(bk)
        for h in range(N_NSA_HEADS):
            vals[h] = jnp.where(hit, rb_ref[bk * N_NSA_HEADS + h] * LOG2E, vals[h])
    for h in range(N_NSA_HEADS):
        g = h % NSA_GROUP
        rows = slice(g * LANES, (g + 1) * LANES)
        if rolled_tiles:
            per_tile = LANES // key_stride
            for t in range(rolled_tiles):
                shift = (width - per_tile * (rolled_tiles - 1 - t)) % width
                o_ref[t, h // NSA_GROUP, rows, :] = vals[h] if shift == 0 else pltpu.roll(vals[h], shift, 1)
        else:
            o_ref[0, h // NSA_GROUP, rows, :] = vals[h]


def _bias_call(rb_flat, n_tiles, width, key_stride, key_offset, name, rolled=False):
    kern = functools.partial(_bias_kernel, width=width, key_stride=key_stride, key_offset=key_offset,
                             first_tile=n_tiles - 1 if rolled else 0, rolled_tiles=n_tiles if rolled else 0)
    block = (n_tiles if rolled else 1, N_NSA_KV_HEADS, GROUP_ROWS, width)
    return pl.pallas_call(
        kern,
        grid=(1 if rolled else n_tiles,),
        in_specs=[pl.BlockSpec(memory_space=pltpu.SMEM)],
        out_specs=pl.BlockSpec(block, lambda v: (v, 0, 0, 0)),
        out_shape=jax.ShapeDtypeStruct((n_tiles, N_NSA_KV_HEADS, GROUP_ROWS, width), F32),
        compiler_params=_params(1),
        name=name,
    )(rb_flat)


def _tile4(a):
    return jnp.concatenate([a] * NSA_GROUP, axis=0)


def _nsa_kernel(q_ref, ks_ref, kw_ref, vs_ref, vw_ref, g_ref, kcvc_ref, bc_ref, tz_ref, selmt_ref,
                wa_ref, wb_ref, wc_ref, wd_ref, o_ref, wa_out, wb_out, wc_out, wd_out,
                qg_scr, qw_scr, caug_scr, near_scr, edge_scr, m_scr, acc_scr, o_scr, s_scr,
                ksa_scr, vsa_scr, kwp_scr, vwa_scr, *, seq, ncp):
    qt = pl.program_id(1)
    q0 = qt * LANES
    n_slc = seq // SEL_BLOCK
    top_n = min(N_SEL, n_slc)
    nkv = N_NSA_KV_HEADS
    for src, dst in ((wa_ref, wa_out), (wb_ref, wb_out), (wc_ref, wc_out), (wd_ref, wd_out)):
        dst[...] = src[...].astype(BF16)
    ri = lax.broadcasted_iota(jnp.int32, (LANES, LANES), 0)
    ci = lax.broadcasted_iota(jnp.int32, (LANES, LANES), 1)
    eye = jnp.where(ri == ci, 1.0, 0.0).astype(BF16)
    gates = _sigmoid(g_ref[0])

    def gate_col(hk, br):
        cols = []
        for g in range(NSA_GROUP):
            c = COL_NG % LANES + (hk * NSA_GROUP + g) * N_BRANCH + br
            cols.append(gates[:, c:c + 1])
        return jnp.concatenate(cols, axis=0)

    @pl.when(qt == 0)
    def _():
        ones = jnp.ones((seq, HEAD_DIM), BF16)
        row = lax.broadcasted_iota(jnp.int32, (LANES, LANES), 0)
        is_far = jnp.logical_and(row >= AUG_FAR, row < AUG_FAR + 3)
        pad_aug = jnp.where(row == AUG_PAD, -MASK_BIG, jnp.where(is_far, 1.0, 0.0)).astype(BF16)
        win_aug = jnp.where(is_far, 1.0, 0.0).astype(BF16)
        lane512 = lax.broadcasted_iota(jnp.int32, (GROUP_ROWS, LANES), 1)
        causal = _tile4(jnp.where(ci <= ri, 0.0, -MASK_BIG))
        acc_scr[...] = jnp.zeros(acc_scr.shape, F32)
        edge_scr[...] = jnp.concatenate([jnp.full((GROUP_ROWS, LANES), -MASK_BIG, F32),
                                         _tile4(jnp.where(ri < ci, 0.0, -MASK_BIG))], axis=1)
        for hk in range(nkv):
            hc = slice(hk * HEAD_DIM, (hk + 1) * HEAD_DIM)
            far = tz_ref[2, hk]
            hi, mid, lo = _split3(far)
            caug = jnp.where(lane512 == AUG_FAR, hi, jnp.where(lane512 == AUG_FAR + 1, mid,
                             jnp.where(lane512 == AUG_FAR + 2, lo, jnp.where(lane512 == AUG_PAD, 1.0, 0.0))))
            caug_scr[hk] = caug.astype(BF16)
            qw_scr[hk, :, HEAD_DIM:] = caug.astype(BF16)
            near_scr[hk] = jnp.concatenate([tz_ref[1, hk] - far, (tz_ref[0, hk] - far) + causal], axis=1)
            ksa_scr[hk, 0, 0:HEAD_DIM, :] = jnp.zeros((HEAD_DIM, LANES), BF16)
            ksa_scr[hk, 0, HEAD_DIM:, :] = pad_aug
            for t in range(seq // LANES):
                rows = slice(t * LANES, (t + 1) * LANES)
                ksa_scr[hk, SEL_PAD_TILES + t, 0:HEAD_DIM, :] = ks_ref[0, rows, hc].T
                ksa_scr[hk, SEL_PAD_TILES + t, HEAD_DIM:, :] = jnp.where(
                    (t * LANES + ci) // SEL_BLOCK == ri, -MASK_BIG, jnp.where(is_far, 1.0, 0.0)).astype(BF16)
                kwp_scr[hk, WIN_PAD // LANES + t, 0:HEAD_DIM, :] = kw_ref[0, rows, hc].T
                kwp_scr[hk, WIN_PAD // LANES + t, HEAD_DIM:, :] = win_aug
            for t in range(WIN_PAD // LANES):
                kwp_scr[hk, t, 0:HEAD_DIM, :] = jnp.zeros((HEAD_DIM, LANES), BF16)
                kwp_scr[hk, t, HEAD_DIM:, :] = pad_aug
            vsa_scr[hk, 0:SEL_PAD_TILES * LANES, :] = jnp.zeros((SEL_PAD_TILES * LANES, 2 * HEAD_DIM), BF16)
            vsa_scr[hk, SEL_PAD_TILES * LANES:, 0:HEAD_DIM] = vs_ref[0, :, hc]
            vsa_scr[hk, SEL_PAD_TILES * LANES:, HEAD_DIM:] = ones
            vwa_scr[hk, 0:WIN_PAD, :] = jnp.zeros((WIN_PAD, 2 * HEAD_DIM), BF16)
            vwa_scr[hk, WIN_PAD:, 0:HEAD_DIM] = vw_ref[0, :, hc]
            vwa_scr[hk, WIN_PAD:, HEAD_DIM:] = ones

    def reset(br):
        m_scr[br] = jnp.full(m_scr.shape[1:], NEG, F32)

    def online_update(br, hk, s, vaug):
        m_prev = m_scr[br, hk]
        m_new = jnp.maximum(m_prev, jnp.max(s, axis=1, keepdims=True))
        alpha = jnp.exp2(m_prev - m_new)
        p = jnp.exp2(s - _lane_tile(m_new, s.shape[1] // LANES))
        pv = jnp.dot(p.astype(BF16), vaug, preferred_element_type=F32)
        acc_scr[br, hk] = _lane_tile(alpha, 2) * acc_scr[br, hk] + pv
        m_scr[br, hk] = m_new

    def finish(br, hk):
        acc = acc_scr[br, hk]
        return acc[:, 0:HEAD_DIM] / acc[:, HEAD_DIM:]

    for hk in range(nkv):
        for g in range(NSA_GROUP):
            h = hk * NSA_GROUP + g
            q_h = q_ref[0, :, h * HEAD_DIM:(h + 1) * HEAD_DIM]
            qg_scr[hk, g * LANES:(g + 1) * LANES, 0:HEAD_DIM] = q_h
            qw_scr[hk, g * LANES:(g + 1) * LANES, 0:HEAD_DIM] = q_h

    for hk in range(nkv):
        qg = qg_scr[hk, :, 0:HEAD_DIM]

        kc = kcvc_ref[0, 0, hk]
        vc = kcvc_ref[0, 1, hk]
        s = lax.dot_general(qg, kc, NT_DIMS, preferred_element_type=F32) + bc_ref[0, hk]
        rc = lax.broadcasted_iota(jnp.int32, (LANES, ncp), 0)
        cc = lax.broadcasted_iota(jnp.int32, (LANES, ncp), 1)
        valid_c = _tile4(jnp.where(q0 + rc - (CMP_STRIDE * cc + CMP_BLOCK - 1) >= 0, 1.0, 0.0)) > 0.5
        s = jnp.where(valid_c, s, NEG)
        p = jnp.where(valid_c, jnp.exp2(s - jnp.max(s, axis=1, keepdims=True)), 0.0)
        l = jnp.sum(p, axis=1, keepdims=True)
        p = p / jnp.where(l > 0.0, l, 1.0)
        o_scr[hk] = gate_col(hk, 0) * jnp.dot(p.astype(BF16), vc, preferred_element_type=F32)

        psum = p[0:LANES]
        for g in range(1, NSA_GROUP):
            psum = psum + p[g * LANES:(g + 1) * LANES]
        p_hi = psum.astype(BF16)
        p_lo = (psum - p_hi.astype(F32)).astype(BF16)
        selmt = selmt_ref[...]
        imp = (lax.dot_general(selmt, p_hi, NT_DIMS, preferred_element_type=F32)
               + lax.dot_general(selmt, p_lo, NT_DIMS, preferred_element_type=F32))
        imp = imp[0:n_slc]
        blk = lax.broadcasted_iota(jnp.int32, (n_slc, LANES), 0)
        cur = (q0 + lax.broadcasted_iota(jnp.int32, (n_slc, LANES), 1)) // SEL_BLOCK
        forced = (blk == 0) | (blk == cur) | (blk == cur - 1)
        imp = jnp.where(forced, FORCE, imp)
        imp = jnp.where(blk <= cur, imp, -jnp.inf)
        rank = jnp.zeros((n_slc, LANES), F32)
        for j in range(n_slc):
            row = imp[j:j + 1, :]
            beats = jnp.where(row > imp, 1.0, jnp.where(row == imp, jnp.where(blk > j, 1.0, 0.0), 0.0))
            rank = rank + beats
        sel_t = jnp.where(rank < top_n, jnp.where(imp > -jnp.inf, 1.0, 0.0), 0.0)
        if n_slc < LANES:
            sel_t = jnp.concatenate([sel_t, jnp.zeros((LANES - n_slc, LANES), F32)], axis=0)
        sel_q = lax.dot_general(eye, sel_t.astype(BF16), NT_DIMS, preferred_element_type=F32)
        not_sel = jnp.where(ci < n_slc, 1.0 - sel_q, 0.0).astype(BF16)
        for g in range(NSA_GROUP):
            rows = slice(g * LANES, (g + 1) * LANES)
            qg_scr[hk, rows, HEAD_DIM:] = jnp.where(ci < n_slc, not_sel, caug_scr[hk, rows, :])

    reset(WIN)
    for off, table in ((1, near_scr), (3, None), (5, edge_scr)):
        t0 = qt - off + WIN_PAD // LANES
        p0 = pl.multiple_of(t0 * LANES, LANES)
        for hk in range(nkv):
            k_t = jnp.concatenate([kwp_scr[hk, t0], kwp_scr[hk, t0 + 1]], axis=1)
            s = jnp.dot(qw_scr[hk], k_t, preferred_element_type=F32)
            if table is not None:
                s = s + (table[hk] if table is near_scr else table[...])
            online_update(WIN, hk, s, vwa_scr[hk, pl.ds(p0, KEY_CHUNK), :])

    reset(SEL)
    n_chunks = (qt + 2) // 2
    first_tile = SEL_PAD_TILES - (qt + 1) % 2

    def sel_scores(hk, c):
        t0 = first_tile + 2 * c
        k_t = jnp.concatenate([ksa_scr[hk, t0], ksa_scr[hk, t0 + 1]], axis=1)
        return jnp.dot(qg_scr[hk], k_t, preferred_element_type=F32)

    def sel_chunk(c, last):
        p0 = pl.multiple_of((first_tile + 2 * c) * LANES, LANES)
        for hk in range(nkv):
            s = s_scr[hk]
            if last:
                s = s + near_scr[hk]
            else:
                s_scr[hk] = sel_scores(hk, c + 1)
            online_update(SEL, hk, s, vsa_scr[hk, pl.ds(p0, KEY_CHUNK), :])

    def sel_pair(i, carry):
        sel_chunk(2 * i, False)
        sel_chunk(2 * i + 1, False)
        return carry

    for hk in range(nkv):
        s_scr[hk] = sel_scores(hk, 0)
    n_far = n_chunks - 1
    lax.fori_loop(0, n_far // 2, sel_pair, 0)
    pl.when(n_far % 2 == 1)(lambda: sel_chunk(n_far - 1, False))
    sel_chunk(n_chunks - 1, True)
    for hk in range(nkv):
        o = (o_scr[hk] + gate_col(hk, 1) * finish(SEL, hk)) + gate_col(hk, 2) * finish(WIN, hk)
        for g in range(NSA_GROUP):
            h = hk * NSA_GROUP + g
            o_ref[0, :, h * HEAD_DIM:(h + 1) * HEAD_DIM] = o[g * LANES:(g + 1) * LANES]


def _nsa_call(o16v, o32v, kcvc, bias_c, tz, selmt, weights_f32, batch, seq, ncp):
    nqt = seq // LANES
    wspecs = []
    for w in weights_f32:
        blk, span = _cast_block(w, batch * nqt)
        wspecs.append(pl.BlockSpec(blk, functools.partial(lambda b, t, span: ((b * nqt + t) // span, 0), span=span)))
    kvw = NSA_KV_WIDTH
    nkv = N_NSA_KV_HEADS
    base = (3 * FOX_WIDTH + NSA_WIDTH) // kvw
    return pl.pallas_call(
        functools.partial(_nsa_kernel, seq=seq, ncp=ncp),
        grid=(batch, nqt),
        in_specs=[pl.BlockSpec((1, LANES, NSA_WIDTH), lambda b, t: (b, t, 3 * FOX_WIDTH // NSA_WIDTH)),
                  pl.BlockSpec((1, seq, kvw), lambda b, t: (b, 0, base)),
                  pl.BlockSpec((1, seq, kvw), lambda b, t: (b, 0, base + 1)),
                  pl.BlockSpec((1, seq, kvw), lambda b, t: (b, 0, base + 2)),
                  pl.BlockSpec((1, seq, kvw), lambda b, t: (b, 0, base + 3)),
                  pl.BlockSpec((1, LANES, LANES), lambda b, t: (b, t, COL_NG // LANES)),
                  pl.BlockSpec((1, 2, N_NSA_KV_HEADS, ncp, HEAD_DIM), lambda b, t: (b, 0, 0, 0, 0)),
                  pl.BlockSpec((1, N_NSA_KV_HEADS, GROUP_ROWS, ncp), lambda b, t: (t, 0, 0, 0)),
                  pl.BlockSpec((3, N_NSA_KV_HEADS, GROUP_ROWS, LANES), lambda b, t: (0, 0, 0, 0)),
                  pl.BlockSpec((LANES, ncp), lambda b, t: (0, 0))] + wspecs,
        out_specs=[pl.BlockSpec((1, LANES, NSA_WIDTH), lambda b, t: (b, t, 0))] + wspecs,
        out_shape=[jax.ShapeDtypeStruct((batch, seq, NSA_WIDTH), F32)]
        + [jax.ShapeDtypeStruct(w.shape, BF16) for w in weights_f32],
        scratch_shapes=[pltpu.VMEM((nkv, GROUP_ROWS, 2 * HEAD_DIM), BF16),
                        pltpu.VMEM((nkv, GROUP_ROWS, 2 * HEAD_DIM), BF16),
                        pltpu.VMEM((nkv, GROUP_ROWS, LANES), BF16),
                        pltpu.VMEM((nkv, GROUP_ROWS, KEY_CHUNK), F32),
                        pltpu.VMEM((GROUP_ROWS, KEY_CHUNK), F32),
                        pltpu.VMEM((2, nkv, GROUP_ROWS, LANES), F32),
                        pltpu.VMEM((2, nkv, GROUP_ROWS, 2 * HEAD_DIM), F32),
                        pltpu.VMEM((nkv, GROUP_ROWS, HEAD_DIM), F32),
                        pltpu.VMEM((nkv, GROUP_ROWS, KEY_CHUNK), F32),
                        pltpu.VMEM((nkv, SEL_PAD_TILES + seq // LANES, 2 * HEAD_DIM, LANES), BF16),
                        pltpu.VMEM((nkv, SEL_PAD_TILES * LANES + seq, 2 * HEAD_DIM), BF16),
                        pltpu.VMEM((nkv, (seq + WIN_PAD) // LANES, 2 * HEAD_DIM, LANES), BF16),
                        pltpu.VMEM((nkv, seq + WIN_PAD, 2 * HEAD_DIM), BF16)],
        compiler_params=_params(2),
        name="nsa_attention",
    )(o16v, o16v, o16v, o16v, o16v, o32v, kcvc, bias_c, tz, selmt, *weights_f32)


def _out_kernel(of_ref, on_ref, gain_ref, w_ref, x_ref, g_ref, o_ref, y_scr, *, tm, n_chains):
    rows_per_chain = tm // n_chains
    gain_f = gain_ref[:, 0:FOX_WIDTH]
    gain_n = gain_ref[:, FOX_WIDTH:MIX_WIDTH]
    for c in range(n_chains):
        rows = slice(c * rows_per_chain, (c + 1) * rows_per_chain)
        for r0 in range(c * rows_per_chain, (c + 1) * rows_per_chain, NORM_ROWS):
            r = slice(r0, r0 + NORM_ROWS)
            y_scr[r, 0:FOX_WIDTH] = (_rms(of_ref[r, :]) * gain_f).astype(BF16)
            y_scr[r, FOX_WIDTH:MIX_WIDTH] = (_rms(on_ref[r, :]) * gain_n).astype(BF16)
        acc = jnp.dot(y_scr[rows, :], w_ref[...], preferred_element_type=F32)
        o_ref[rows, :] = x_ref[rows, :] + g_ref[0] * acc


def _out_call(o_fox, o_nsa, gain, w_out, x2, g1, seq):
    t, d = x2.shape
    tm = min(512, seq)
    rows_per_batch = seq // tm
    return pl.pallas_call(
        functools.partial(_out_kernel, tm=tm, n_chains=2),
        grid=(t // tm,),
        in_specs=[pl.BlockSpec((tm, FOX_WIDTH), lambda i: (i, 0)),
                  pl.BlockSpec((tm, NSA_WIDTH), lambda i: (i, 0)),
                  pl.BlockSpec((1, MIX_WIDTH), lambda i: (0, 0)),
                  pl.BlockSpec((MIX_WIDTH, d), lambda i: (0, 0)),
                  pl.BlockSpec((tm, d), lambda i: (i, 0)),
                  pl.BlockSpec((1, 1, d), lambda i: (i // rows_per_batch, 0, 0))],
        out_specs=pl.BlockSpec((tm, d), lambda i: (i, 0)),
        out_shape=jax.ShapeDtypeStruct((t, d), F32),
        scratch_shapes=[pltpu.VMEM((tm, MIX_WIDTH), BF16)],
        compiler_params=_params(1),
        name="out_proj",
    )(o_fox, o_nsa, gain, w_out, x2, g1)


def _ffn_kernel(x_ref, gain_ref, sc_ref, sh_ref, g_ref, wg_ref, wu_ref, wd_ref, o_ref, h_scr, *, n_f):
    f = pl.program_id(1)
    tm = h_scr.shape[0]

    def hidden_step(first_row, n_rows, first, last):
        rows = slice(first_row, first_row + n_rows)
        h = h_scr[rows, :]
        a = jnp.dot(h, wg_ref[...], preferred_element_type=F32)
        u = jnp.dot(h, wu_ref[...], preferred_element_type=F32)
        t = (a * _sigmoid(a)) * u
        part = jnp.dot(t.astype(BF16), wd_ref[...], preferred_element_type=F32)
        acc = part if first else o_ref[rows, :] + part
        o_ref[rows, :] = x_ref[rows, :] + g_ref[0] * acc if last else acc

    def edge_step(first, last):
        n_rows = tm // ROW_CHAINS
        for c in range(ROW_CHAINS):
            if first:
                _modulated_norm(x_ref, gain_ref, sc_ref, sh_ref, h_scr, c * n_rows, n_rows)
            hidden_step(c * n_rows, n_rows, first, last)

    if n_f == 1:
        edge_step(True, True)
    else:
        pl.when(f == 0)(functools.partial(edge_step, True, False))
        pl.when(jnp.logical_and(f > 0, f < n_f - 1))(functools.partial(hidden_step, 0, tm, False, False))
        pl.when(f == n_f - 1)(functools.partial(edge_step, False, True))


def _ffn_call(x1, gain2, sc, sh, g2, wg, wu, wd, seq):
    t, d = x1.shape
    dff = wg.shape[1]
    tm = min(1024, seq)
    tf = 512 if dff % 512 == 0 else dff
    rows_per_batch = seq // tm
    return pl.pallas_call(
        functools.partial(_ffn_kernel, n_f=dff // tf),
        grid=(t // tm, dff // tf),
        in_specs=[pl.BlockSpec((tm, d), lambda i, f: (i, 0)),
                  pl.BlockSpec((1, d), lambda i, f: (0, 0)),
                  pl.BlockSpec((1, 1, d), lambda i, f: (i // rows_per_batch, 0, 0)),
                  pl.BlockSpec((1, 1, d), lambda i, f: (i // rows_per_batch, 0, 0)),
                  pl.BlockSpec((1, 1, d), lambda i, f: (i // rows_per_batch, 0, 0)),
                  pl.BlockSpec((d, tf), lambda i, f: (0, f)),
                  pl.BlockSpec((d, tf), lambda i, f: (0, f)),
                  pl.BlockSpec((tf, d), lambda i, f: (f, 0))],
        out_specs=pl.BlockSpec((tm, d), lambda i, f: (i, 0)),
        out_shape=jax.ShapeDtypeStruct((t, d), F32),
        scratch_shapes=[pltpu.VMEM((tm, d), BF16)],
        compiler_params=_params(2),
        name="swiglu_ffn",
    )(x1, gain2, sc, sh, g2, wg, wu, wd)


def _selection_matrix_t(ncp, n_slc):
    r, q = SEL_BLOCK // CMP_STRIDE, CMP_BLOCK // CMP_STRIDE
    m = np.zeros((LANES, ncp), np.float32)
    for j in range(n_slc):
        for a in range(r):
            for b in range(q):
                c = r * j + a - b
                if 0 <= c < ncp:
                    m[j, c] += 1.0
    return m


def _w_in_block_sources():
    o = 0
    start = {}
    for name, width in (("fq", FOX_WIDTH), ("fk", FOX_WIDTH), ("fv", FOX_WIDTH), ("ff", N_FOX_HEADS),
                        ("nq", NSA_WIDTH), ("nk", N_BRANCH * NSA_KV_WIDTH), ("nv", N_BRANCH * NSA_KV_WIDTH),
                        ("ng", N_BRANCH * N_NSA_HEADS)):
        start[name] = o
        o += width
    kvw = NSA_KV_WIDTH
    groups = [(start["fq"], 3 * FOX_WIDTH), (start["nq"], NSA_WIDTH),
              (start["nk"] + kvw, 2 * kvw), (start["nv"] + kvw, 2 * kvw),
              (start["nk"], kvw), (start["nv"], kvw)]
    blocks = [s + LANES * b for s, width in groups for b in range(width // LANES)]
    return blocks, (start["ff"], N_FOX_HEADS), (start["ng"], N_BRANCH * N_NSA_HEADS)


def _repack_kernel(src_ref, wt_hbm, o16_ref, o32_ref, buf, sem, *, n16, n_whole, ff, ng):
    k = pl.program_id(0)
    n_slots = buf.shape[0]
    ahead = n_slots - 1
    slot = k % n_slots

    def whole_copy(kk, s):
        r0 = pl.multiple_of(src_ref[kk], 8)
        return pltpu.make_async_copy(wt_hbm.at[pl.ds(r0, LANES), :], buf.at[s], sem.at[s])

    def narrow_copies(s):
        return [pltpu.make_async_copy(wt_hbm.at[pl.ds(ff[0], ff[1]), :], buf.at[s, pl.ds(0, ff[1]), :], sem.at[s]),
                pltpu.make_async_copy(wt_hbm.at[pl.ds(ng[0], ng[1]), :], buf.at[s, pl.ds(ff[1], ng[1]), :],
                                      sem.at[s])]

    def start_block(kk):
        @pl.when(kk < n_whole)
        def _():
            whole_copy(kk, kk % n_slots).start()

        @pl.when(kk == n_whole)
        def _():
            for cp in narrow_copies(kk % n_slots):
                cp.start()

    @pl.when(k == 0)
    def _():
        for kk in range(ahead):
            start_block(kk)

    start_block(k + ahead)

    @pl.when(k < n_whole)
    def _():
        whole_copy(k, slot).wait()

    @pl.when(k == n_whole)
    def _():
        for cp in narrow_copies(slot):
            cp.wait()

    xt = buf[slot].T
    lane = lax.broadcasted_iota(jnp.int32, xt.shape, 1)
    xt = jnp.where(jnp.logical_or(k < n_whole, lane < ff[1] + ng[1]), xt, 0.0).astype(BF16)

    @pl.when(k < n16)
    def _():
        o16_ref[...] = xt

    @pl.when(k >= n16)
    def _():
        o32_ref[...] = xt


def _repack_w_in(wt):
    n, d = wt.shape
    blocks, ff, ng = _w_in_block_sources()
    n16 = W16 // LANES
    n_whole = len(blocks)
    assert n_whole + 1 == (W16 + W32) // LANES
    return pl.pallas_call(
        functools.partial(_repack_kernel, n16=n16, n_whole=n_whole, ff=ff, ng=ng),
        grid_spec=pltpu.PrefetchScalarGridSpec(
            num_scalar_prefetch=1,
            grid=(n_whole + 1,),
            in_specs=[pl.BlockSpec(memory_space=pl.ANY)],
            out_specs=[pl.BlockSpec((d, LANES), lambda k, src: (0, jnp.minimum(k, n16 - 1))),
                       pl.BlockSpec((d, LANES), lambda k, src: (0, jnp.maximum(k - n16, 0)))],
            scratch_shapes=[pltpu.VMEM((REPACK_SLOTS, LANES, d), F32), pltpu.SemaphoreType.DMA((REPACK_SLOTS,))]),
        out_shape=[jax.ShapeDtypeStruct((d, W16), BF16), jax.ShapeDtypeStruct((d, W32), BF16)],
        compiler_params=_params(1),
        name="w_in_repack",
    )(jnp.asarray(blocks, jnp.int32), wt)


def kernel(x, c, ada_w, ada_b, norm1_gain, norm2_gain, w_in, fox_f_bias, fox_q_gain, fox_k_gain, nsa_q_gain,
           nsa_k_gain, nsa_cmp_pos, nsa_cmp_w1, nsa_cmp_w2, rel_bias, mix_out_gain, w_out, ffn_w_gate, ffn_w_up,
           ffn_w_down):
    batch, seq, d = x.shape
    assert seq % KEY_CHUNK == 0 and seq >= WINDOW and d % LANES == 0 and seq // SEL_BLOCK <= LANES
    depth = ada_w.shape[0]
    nqt = seq // LANES
    ncp = -(-(seq // CMP_STRIDE) // LANES) * LANES

    selmt = jnp.asarray(_selection_matrix_t(ncp, seq // SEL_BLOCK), BF16)
    rb_flat = rel_bias.reshape(-1)
    bias_c = _bias_call(rb_flat, nqt, ncp, CMP_STRIDE, CMP_BLOCK - 1, "t5_bias_compressed", rolled=True)
    tz = _bias_call(rb_flat, 3, LANES, 1, 0, "t5_bias_toeplitz")

    ones_h = jnp.ones((HEAD_DIM,), F32)
    c_pad = jnp.pad(c, ((0, 8 - batch % 8 if batch % 8 else 0), (0, 0)))
    x2 = x.reshape(batch * seq, d)
    for layer in range(depth):
        mod = _ada_call(c_pad, ada_w[layer], ada_b[layer][None, :])[:batch]
        sh1, sc1, g1, sh2, sc2, g2 = [mod[:, i * d:(i + 1) * d][:, None, :] for i in range(N_MOD)]

        kg = nsa_k_gain[layer]
        col_gain = jnp.concatenate([
            jnp.tile(fox_q_gain[layer] * QSCALE, N_FOX_HEADS), jnp.tile(fox_k_gain[layer], N_FOX_HEADS),
            jnp.tile(ones_h, N_FOX_HEADS), jnp.tile(nsa_q_gain[layer] * QSCALE, N_NSA_HEADS),
            jnp.tile(kg[1], N_NSA_KV_HEADS), jnp.tile(kg[2], N_NSA_KV_HEADS),
            jnp.tile(ones_h, 2 * N_NSA_KV_HEADS)])[None, :]
        col_flag = jnp.concatenate([
            jnp.ones((2 * FOX_WIDTH,), F32), jnp.zeros((FOX_WIDTH,), F32), jnp.ones((NSA_WIDTH,), F32),
            jnp.ones((2 * NSA_KV_WIDTH,), F32), jnp.zeros((2 * NSA_KV_WIDTH,), F32)])[None, :]
        w16, w32 = _repack_w_in(jnp.swapaxes(w_in, 1, 2)[layer])
        o16, o32 = _proj_call(x2, sc1, sh1, norm1_gain[layer][None, :], w16, w32,
                              col_gain, col_flag, seq)
        o16v = o16.reshape(batch, seq, W16)
        o32v = o32.reshape(batch, seq, W32)

        fb_pad = jnp.pad(fox_f_bias[layer], (0, LANES - N_FOX_HEADS))[None, :]
        cs = _cum_call(o32, fb_pad, batch, seq, N_FOX_HEADS // FOX_HEADS_PER_STEP)
        o_fox = _fox_call(o16v, cs, batch, seq)

        w1 = nsa_cmp_w1[layer].reshape(2, CMP_BLOCK, HEAD_DIM, HEAD_DIM)
        kcvc = _cmp_call(o32, nsa_cmp_pos[layer], w1, nsa_cmp_w2[layer], kg[0][None, :],
                         batch, seq, ncp)
        o_nsa, wg16, wu16, wo16, wd16 = _nsa_call(
            o16v, o32v, kcvc, bias_c, tz, selmt,
            (ffn_w_gate[layer], ffn_w_up[layer], w_out[layer], ffn_w_down[layer]), batch, seq, ncp)

        x1 = _out_call(o_fox.reshape(batch * seq, FOX_WIDTH), o_nsa.reshape(batch * seq, NSA_WIDTH),
                       mix_out_gain[layer][None, :], wo16, x2, g1, seq)
        x2 = _ffn_call(x1, norm2_gain[layer][None, :], sc2, sh2, g2, wg16, wu16, wd16, seq)
    return x2.reshape(batch, seq, d)
```

```python
import functools
import math

import numpy as np
import jax
import jax.numpy as jnp
from jax import lax
from jax.experimental import pallas as pl
from jax.experimental.pallas import tpu as pltpu

HEAD_DIM = 128
N_FOX_HEADS = 8
N_NSA_HEADS = 8
N_NSA_KV_HEADS = 2
NSA_GROUP = N_NSA_HEADS // N_NSA_KV_HEADS
FOX_WIDTH = N_FOX_HEADS * HEAD_DIM
NSA_WIDTH = N_NSA_HEADS * HEAD_DIM
NSA_KV_WIDTH = N_NSA_KV_HEADS * HEAD_DIM
MIX_WIDTH = FOX_WIDTH + NSA_WIDTH
N_BRANCH = 3
CMP_BLOCK = 32
CMP_STRIDE = 16
SEL_BLOCK = 64
N_SEL = 8
WINDOW = 512
N_BUCKETS = 32
MAX_DISTANCE = 128
N_MOD = 6
SCALE = HEAD_DIM ** -0.5
LOG2E = math.log2(math.e)
LOG2E_HI = float(np.float32(LOG2E))
LOG2E_LO = LOG2E - LOG2E_HI
QSCALE = SCALE * LOG2E
EPS = 1e-6
NEG = -1e30
FORCE = 1e6

LANES = 128
GROUP_ROWS = NSA_GROUP * LANES
VMEM_LIMIT = 56 * 1024 * 1024
MXU_COLS = 256
KEY_CHUNK = MXU_COLS
WIN_PAD = WINDOW + LANES
MASK_BIG = 2.0 ** 100
SEL, WIN = 0, 1
REPACK_SLOTS = 4
AUG_FAR, AUG_PAD = 120, 127
FOX_HEADS_PER_STEP = 4
SEL_PAD_TILES = 1

W16 = 3 * FOX_WIDTH + NSA_WIDTH + 4 * NSA_KV_WIDTH
W32 = 5 * LANES
COL_FF = 2 * NSA_KV_WIDTH
COL_NG = COL_FF + N_FOX_HEADS

F32 = jnp.float32
BF16 = jnp.bfloat16
NT_DIMS = (((1,), (1,)), ((), ()))


def _params(n_axes):
    return pltpu.CompilerParams(dimension_semantics=("arbitrary",) * n_axes,
                                vmem_limit_bytes=VMEM_LIMIT)


def _sigmoid(x):
    return 1.0 / (1.0 + jnp.exp(-x))


def _lane_tile(a, n):
    return jnp.concatenate([a] * n, axis=1)


BF16_SUBLANES = 16


def _cast_block(w, n_steps):
    rows, cols = w.shape
    assert rows % n_steps == 0, (w.shape, n_steps)
    per_step = rows // n_steps
    span = BF16_SUBLANES // math.gcd(BF16_SUBLANES, per_step)
    assert n_steps % span == 0, (w.shape, n_steps)
    return (per_step * span, cols), span


def _rms(x):
    return x * lax.rsqrt(jnp.mean(x * x, axis=-1, keepdims=True) + EPS)


NORM_ROWS = 16


def _modulated_norm(x_ref, gain_ref, sc_ref, sh_ref, h_ref, first_row, n_rows):
    gm = gain_ref[...] * (1.0 + sc_ref[0])
    sh = sh_ref[0]
    for r0 in range(first_row, first_row + n_rows, NORM_ROWS):
        rows = slice(r0, r0 + NORM_ROWS)
        h_ref[rows, :] = (_rms(x_ref[rows, :]) * gm + sh).astype(BF16)


ROW_CHAINS = 2


def _ada_kernel(c_ref, w_ref, b_ref, o_ref):
    c = c_ref[...]
    s = (c * _sigmoid(c)).astype(BF16)
    o_ref[...] = jnp.dot(s, w_ref[...].astype(BF16), preferred_element_type=F32) + b_ref[...]


def _ada_call(c_pad, w, b):
    rows, d = c_pad.shape
    n = w.shape[1]
    tn = next(t for t in (1024, 768, 512, 384, 256, 128) if n % t == 0)
    return pl.pallas_call(
        _ada_kernel,
        grid=(n // tn,),
        in_specs=[pl.BlockSpec((rows, d), lambda j: (0, 0)),
                  pl.BlockSpec((d, tn), lambda j: (0, j)),
                  pl.BlockSpec((1, tn), lambda j: (0, j))],
        out_specs=pl.BlockSpec((rows, tn), lambda j: (0, j)),
        out_shape=jax.ShapeDtypeStruct((rows, n), F32),
        compiler_params=_params(1),
        name="adaln",
    )(c_pad, w, b)


def _proj_kernel(x_ref, sc_ref, sh_ref, g_ref, w16_ref, w32_ref, gain_ref, flag_ref, o16_ref, o32_ref, h_scr, *, tn):
    j = pl.program_id(1)
    tm = h_scr.shape[0]

    def column_step(first_row, n_rows, with_side_outputs):
        rows = slice(first_row, first_row + n_rows)
        h = h_scr[rows, :]
        for c in range(tn // MXU_COLS):
            acc = jnp.dot(h, w16_ref[:, c * MXU_COLS:(c + 1) * MXU_COLS], preferred_element_type=F32)
            for g in range(MXU_COLS // LANES):
                cols = slice(c * MXU_COLS + g * LANES, c * MXU_COLS + (g + 1) * LANES)
                a = acc[:, g * LANES:(g + 1) * LANES]
                r = lax.rsqrt(jnp.mean(a * a, axis=-1, keepdims=True) + EPS)
                scale = jnp.where(flag_ref[:, cols] > 0.5, r, 1.0)
                o16_ref[rows, cols] = (a * scale * gain_ref[:, cols]).astype(BF16)
        if with_side_outputs:
            o32_ref[rows, :] = jnp.dot(h, w32_ref[...], preferred_element_type=F32)

    def step(first, with_side_outputs):
        n_rows = tm // ROW_CHAINS
        for c in range(ROW_CHAINS):
            if first:
                _modulated_norm(x_ref, g_ref, sc_ref, sh_ref, h_scr, c * n_rows, n_rows)
            column_step(c * n_rows, n_rows, with_side_outputs)

    last = pl.num_programs(1) - 1
    pl.when(j == 0)(functools.partial(step, True, False))
    pl.when(jnp.logical_and(j > 0, j < last))(functools.partial(step, False, False))
    pl.when(j == last)(functools.partial(step, False, True))


def _proj_call(x2, sc, sh, gain1, w16, w32, col_gain, col_flag, seq):
    t, d = x2.shape
    tm = min(1024, seq)
    tn = 1280 if W16 % 1280 == 0 else 1024
    assert W16 // tn >= 2
    rows_per_batch = seq // tm
    return pl.pallas_call(
        functools.partial(_proj_kernel, tn=tn),
        grid=(t // tm, W16 // tn),
        in_specs=[pl.BlockSpec((tm, d), lambda i, j: (i, 0)),
                  pl.BlockSpec((1, 1, d), lambda i, j: (i // rows_per_batch, 0, 0)),
                  pl.BlockSpec((1, 1, d), lambda i, j: (i // rows_per_batch, 0, 0)),
                  pl.BlockSpec((1, d), lambda i, j: (0, 0)),
                  pl.BlockSpec((d, tn), lambda i, j: (0, j)),
                  pl.BlockSpec((d, W32), lambda i, j: (0, 0)),
                  pl.BlockSpec((1, tn), lambda i, j: (0, j)),
                  pl.BlockSpec((1, tn), lambda i, j: (0, j))],
        out_specs=[pl.BlockSpec((tm, tn), lambda i, j: (i, j)),
                   pl.BlockSpec((tm, W32), lambda i, j: (i, 0))],
        out_shape=[jax.ShapeDtypeStruct((t, W16), BF16),
                   jax.ShapeDtypeStruct((t, W32), F32)],
        scratch_shapes=[pltpu.VMEM((tm, d), BF16)],
        compiler_params=_params(2),
        name="in_proj",
    )(x2, sc, sh, gain1, w16, w32, col_gain, col_flag)


def _split3(c):
    hi = c.astype(BF16).astype(F32)
    r1 = c - hi
    mid = r1.astype(BF16).astype(F32)
    return hi, mid, (r1 - mid).astype(BF16).astype(F32)


def _cum_kernel(ff_ref, fb_ref, o_ref, *, seq, groups):
    ri = lax.broadcasted_iota(jnp.int32, (LANES, LANES), 0)
    ci = lax.broadcasted_iota(jnp.int32, (LANES, LANES), 1)
    tri = jnp.where(ri >= ci, 1.0, 0.0).astype(BF16)
    carry = jnp.zeros((1, LANES), F32)
    for blk in range(seq // LANES):
        rows = slice(blk * LANES, (blk + 1) * LANES)
        x = ff_ref[0, rows, :] + fb_ref[...]
        lf = jnp.minimum(x, 0.0) - jnp.log(1.0 + jnp.exp(-jnp.abs(x)))
        hi = lf.astype(BF16)
        r1 = lf - hi.astype(F32)
        mid = r1.astype(BF16)
        lo = (r1 - mid.astype(F32)).astype(BF16)
        c = (jnp.dot(tri, hi, preferred_element_type=F32)
             + jnp.dot(tri, mid, preferred_element_type=F32)
             + jnp.dot(tri, lo, preferred_element_type=F32)) + carry
        carry = c[LANES - 1:LANES, :]
        terms = _split3(c * LOG2E_HI + c * LOG2E_LO)
        per = N_FOX_HEADS // groups
        for g in range(groups):
            packed = jnp.zeros((LANES, LANES), F32)
            for t, term in enumerate(terms):
                shift = (t * per - g * per) % LANES
                moved = term if shift == 0 else pltpu.roll(term, shift, 1)
                packed = jnp.where(jnp.logical_and(ci >= t * per, ci < (t + 1) * per), moved, packed)
            o_ref[0, g, rows, :] = packed


def _cum_call(o32, fb_pad, batch, seq, groups):
    o32v = o32.reshape(batch, seq, W32)
    return pl.pallas_call(
        functools.partial(_cum_kernel, seq=seq, groups=groups),
        grid=(batch,),
        in_specs=[pl.BlockSpec((1, seq, LANES), lambda b: (b, 0, COL_FF // LANES)),
                  pl.BlockSpec((1, LANES), lambda b: (0, 0))],
        out_specs=pl.BlockSpec((1, groups, seq, LANES), lambda b: (b, 0, 0, 0)),
        out_shape=jax.ShapeDtypeStruct((batch, groups, seq, LANES), F32),
        compiler_params=_params(1),
        name="fox_cumsum",
    )(o32v, fb_pad)


def _fox_kernel(q_ref, k_ref, v_ref, cs_ref, o_ref, m_scr, acc_scr, s_scr, kaug_scr, vaug_scr, *, seq, tq, tk, nh):
    n_diag = tq // tk
    lane_q = lax.broadcasted_iota(jnp.int32, (tq, LANES), 1)
    row8 = lax.broadcasted_iota(jnp.int32, (8, tk), 0)
    ri = lax.broadcasted_iota(jnp.int32, (tq, tk), 0)
    ci = lax.broadcasted_iota(jnp.int32, (tq, tk), 1)
    causal = [ci + d * tk <= ri for d in range(n_diag)]
    for c in range(seq // tk):
        rows = slice(c * tk, (c + 1) * tk)
        terms_t = cs_ref[0, 0, rows, :].T
        for j in range(nh):
            hi, mid, lo = [terms_t[t * nh + j:t * nh + j + 1] for t in range(3)]
            tail8 = jnp.where(row8 < 3, 1.0, jnp.where(row8 == 3, -hi, jnp.where(
                row8 == 4, -mid, jnp.where(row8 == 5, -lo, 0.0))))
            kaug_scr[j, c, 0:HEAD_DIM, :] = k_ref[0, rows, j * HEAD_DIM:(j + 1) * HEAD_DIM].T
            kaug_scr[j, c, HEAD_DIM:, :] = jnp.concatenate(
                [tail8, jnp.zeros((HEAD_DIM - 8, tk), F32)], axis=0).astype(BF16)
    for j in range(nh):
        vaug_scr[j, :, 0:HEAD_DIM] = v_ref[0, :, j * HEAD_DIM:(j + 1) * HEAD_DIM]
        vaug_scr[j, :, HEAD_DIM:] = jnp.ones((seq, HEAD_DIM), BF16)
    acc_scr[...] = jnp.zeros(acc_scr.shape, F32)

    def q_body(qi, carry):
        q0 = pl.multiple_of(qi * tq, tq)
        terms = cs_ref[0, 0, pl.ds(q0, tq), :]
        qs = []
        for j in range(nh):
            hi, mid, lo = [terms if t * nh + j == t else pltpu.roll(terms, (t - (t * nh + j)) % LANES, 1)
                           for t in range(3)]
            tail = jnp.where(lane_q == 0, hi, jnp.where(lane_q == 1, mid, jnp.where(lane_q == 2, lo,
                             jnp.where(lane_q < 6, 1.0, 0.0))))
            qs.append(jnp.concatenate([q_ref[0, pl.ds(q0, tq), j * HEAD_DIM:(j + 1) * HEAD_DIM],
                                       tail.astype(BF16)], axis=1))
            m_scr[j] = jnp.full((tq, LANES), NEG, F32)

        all_rows = slice(0, tq)

        def scores(j, ki, rows):
            return jnp.dot(qs[j][rows], kaug_scr[j, ki], preferred_element_type=F32)

        def tile(ki, mask, rows, next_rows):
            k0 = pl.multiple_of(ki * tk, tk)
            for j in range(nh):
                s = s_scr[j, rows, :]
                if mask is not None:
                    s = jnp.where(mask[rows], s, NEG)
                if next_rows is not None:
                    s_scr[j, next_rows, :] = scores(j, ki + 1, next_rows)
                m_prev = m_scr[j, rows, :]
                m_new = jnp.maximum(m_prev, jnp.max(s, axis=1, keepdims=True))
                alpha = jnp.exp2(m_prev - m_new)
                p = jnp.exp2(s - _lane_tile(m_new, tk // LANES))
                pv = jnp.dot(p.astype(BF16), vaug_scr[j, pl.ds(k0, tk), :], preferred_element_type=F32)
                acc_scr[j, rows, :] = _lane_tile(alpha, 2) * acc_scr[j, rows, :] + pv
                m_scr[j, rows, :] = m_new

        def k_body(ki, c2):
            tile(ki, None, all_rows, all_rows)
            return c2

        for j in range(nh):
            s_scr[j] = scores(j, 0, all_rows)
        n_off = qi * n_diag
        lax.fori_loop(0, n_off, k_body, 0)
        for d in range(n_diag):
            nxt = slice((d + 1) * tk, tq) if d + 1 < n_diag else None
            tile(n_off + d, causal[d], slice(d * tk, tq), nxt)
        for j in range(nh):
            acc = acc_scr[j]
            o_ref[0, pl.ds(q0, tq), j * HEAD_DIM:(j + 1) * HEAD_DIM] = acc[:, 0:HEAD_DIM] / acc[:, HEAD_DIM:]
        return carry

    lax.fori_loop(0, seq // tq, q_body, 0)


def _fox_call(o16v, cs, batch, seq):
    nh = FOX_HEADS_PER_STEP
    tq, tk = min(512, seq), min(256, seq)
    groups = N_FOX_HEADS // nh
    w = nh * HEAD_DIM
    return pl.pallas_call(
        functools.partial(_fox_kernel, seq=seq, tq=tq, tk=tk, nh=nh),
        grid=(batch, groups),
        in_specs=[pl.BlockSpec((1, seq, w), lambda b, h: (b, 0, h)),
                  pl.BlockSpec((1, seq, w), lambda b, h: (b, 0, groups + h)),
                  pl.BlockSpec((1, seq, w), lambda b, h: (b, 0, 2 * groups + h)),
                  pl.BlockSpec((1, 1, seq, LANES), lambda b, h: (b, h, 0, 0))],
        out_specs=pl.BlockSpec((1, seq, w), lambda b, h: (b, 0, h)),
        out_shape=jax.ShapeDtypeStruct((batch, seq, FOX_WIDTH), F32),
        scratch_shapes=[pltpu.VMEM((nh, tq, LANES), F32),
                        pltpu.VMEM((nh, tq, 2 * HEAD_DIM), F32),
                        pltpu.VMEM((nh, tq, tk), F32),
                        pltpu.VMEM((nh, seq // tk, 2 * HEAD_DIM, tk), BF16),
                        pltpu.VMEM((nh, seq, 2 * HEAD_DIM), BF16)],
        compiler_params=_params(2),
        name="fox_attention",
    )(o16v, o16v, o16v, cs)


def _cmp_kernel(x_ref, pos_ref, w1_ref, w2_ref, gain_ref, o_ref, xs_scr, *, seq, ncp):
    rows = xs_scr.shape[1]
    for kv in range(2):
        for h in range(N_NSA_KV_HEADS):
            n = kv * N_NSA_KV_HEADS + h
            xs_scr[n, 0:seq, :] = x_ref[0, :, n * HEAD_DIM:(n + 1) * HEAD_DIM]
            xs_scr[n, seq:rows, :] = jnp.zeros((rows - seq, LANES), F32)
            acc = jnp.zeros((ncp, HEAD_DIM), F32)
            for l in range(CMP_BLOCK):
                xl = xs_scr[n, pl.ds(l, ncp, stride=CMP_STRIDE), :] + pos_ref[kv, l:l + 1, :]
                acc = acc + jnp.dot(xl.astype(BF16), w1_ref[kv, l].astype(BF16), preferred_element_type=F32)
            hmid = acc * _sigmoid(acc)
            y = jnp.dot(hmid.astype(BF16), w2_ref[kv].astype(BF16), preferred_element_type=F32)
            if kv == 0:
                y = _rms(y) * gain_ref[...]
            o_ref[0, kv, h] = y.astype(BF16)


def _cmp_call(o32, pos, w1, w2, gain, batch, seq, ncp):
    o32v = o32.reshape(batch, seq, W32)
    hkv = N_NSA_KV_HEADS
    width = 2 * NSA_KV_WIDTH
    return pl.pallas_call(
        functools.partial(_cmp_kernel, seq=seq, ncp=ncp),
        grid=(batch,),
        in_specs=[pl.BlockSpec((1, seq, width), lambda b: (b, 0, 0)),
                  pl.BlockSpec((2, CMP_BLOCK, HEAD_DIM), lambda b: (0, 0, 0)),
                  pl.BlockSpec((2, CMP_BLOCK, HEAD_DIM, HEAD_DIM), lambda b: (0, 0, 0, 0)),
                  pl.BlockSpec((2, HEAD_DIM, HEAD_DIM), lambda b: (0, 0, 0)),
                  pl.BlockSpec((1, HEAD_DIM), lambda b: (0, 0))],
        out_specs=pl.BlockSpec((1, 2, hkv, ncp, HEAD_DIM), lambda b: (b, 0, 0, 0, 0)),
        out_shape=jax.ShapeDtypeStruct((batch, 2, hkv, ncp, HEAD_DIM), BF16),
        scratch_shapes=[pltpu.VMEM((2 * hkv, CMP_STRIDE * ncp + CMP_BLOCK, LANES), F32)],
        compiler_params=_params(1),
        name="nsa_compress",
    )(o32v, pos, w1, w2, gain)


def _bias_kernel(rb_ref, o_ref, *, width, key_stride, key_offset, first_tile, rolled_tiles):
    v = pl.program_id(0) + first_tile
    i = lax.broadcasted_iota(jnp.int32, (LANES, width), 0)
    j = lax.broadcasted_iota(jnp.int32, (LANES, width), 1)
    d = v * LANES + i - (key_stride * j + key_offset)
    n = jnp.maximum(d, 0)
    max_exact = N_BUCKETS // 2
    nf = jnp.maximum(n, 1).astype(F32)
    large = max_exact + jnp.trunc(jnp.log(nf / max_exact) / math.log(MAX_DISTANCE / max_exact)
                                  * (N_BUCKETS - max_exact))
    large = jnp.minimum(large, float(N_BUCKETS - 1))
    bkt = jnp.where(n < max_exact, n.astype(F32), large)
    vals = [jnp.zeros((LANES, width), F32) for _ in range(N_NSA_HEADS)]
    for bk in range(N_BUCKETS):
        hit = bkt == float(bk)
        for h in range(N_NSA_HEADS):
            vals[h] = jnp.where(hit, rb_ref[bk * N_NSA_HEADS + h] * LOG2E, vals[h])
    for h in range(N_NSA_HEADS):
        g = h % NSA_GROUP
        rows = slice(g * LANES, (g + 1) * LANES)
        if rolled_tiles:
            per_tile = LANES // key_stride
            for t in range(rolled_tiles):
                shift = (width - per_tile * (rolled_tiles - 1 - t)) % width
                o_ref[t, h // NSA_GROUP, rows, :] = vals[h] if shift == 0 else pltpu.roll(vals[h], shift, 1)
        else:
            o_ref[0, h // NSA_GROUP, rows, :] = vals[h]


def _bias_call(rb_flat, n_tiles, width, key_stride, key_offset, name, rolled=False):
    kern = functools.partial(_bias_kernel, width=width, key_stride=key_stride, key_offset=key_offset,
                             first_tile=n_tiles - 1 if rolled else 0, rolled_tiles=n_tiles if rolled else 0)
    block = (n_tiles if rolled else 1, N_NSA_KV_HEADS, GROUP_ROWS, width)
    return pl.pallas_call(
        kern,
        grid=(1 if rolled else n_tiles,),
        in_specs=[pl.BlockSpec(memory_space=pltpu.SMEM)],
        out_specs=pl.BlockSpec(block, lambda v: (v, 0, 0, 0)),
        out_shape=jax.ShapeDtypeStruct((n_tiles, N_NSA_KV_HEADS, GROUP_ROWS, width), F32),
        compiler_params=_params(1),
        name=name,
    )(rb_flat)


def _tile4(a):
    return jnp.concatenate([a] * NSA_GROUP, axis=0)


def _nsa_kernel(q_ref, ks_ref, kw_ref, vs_ref, vw_ref, g_ref, kcvc_ref, bc_ref, tz_ref, selmt_ref,
                wa_ref, wb_ref, wc_ref, wd_ref, o_ref, wa_out, wb_out, wc_out, wd_out,
                qg_scr, qw_scr, caug_scr, near_scr, edge_scr, m_scr, acc_scr, o_scr, s_scr,
                ksa_scr, vsa_scr, kwp_scr, vwa_scr, *, seq, ncp):
    qt = pl.program_id(1)
    q0 = qt * LANES
    n_slc = seq // SEL_BLOCK
    top_n = min(N_SEL, n_slc)
    nkv = N_NSA_KV_HEADS
    for src, dst in ((wa_ref, wa_out), (wb_ref, wb_out), (wc_ref, wc_out), (wd_ref, wd_out)):
        dst[...] = src[...].astype(BF16)
    ri = lax.broadcasted_iota(jnp.int32, (LANES, LANES), 0)
    ci = lax.broadcasted_iota(jnp.int32, (LANES, LANES), 1)
    eye = jnp.where(ri == ci, 1.0, 0.0).astype(BF16)
    gates = _sigmoid(g_ref[0])

    def gate_col(hk, br):
        cols = []
        for g in range(NSA_GROUP):
            c = COL_NG % LANES + (hk * NSA_GROUP + g) * N_BRANCH + br
            cols.append(gates[:, c:c + 1])
        return jnp.concatenate(cols, axis=0)

    @pl.when(qt == 0)
    def _():
        ones = jnp.ones((seq, HEAD_DIM), BF16)
        row = lax.broadcasted_iota(jnp.int32, (LANES, LANES), 0)
        is_far = jnp.logical_and(row >= AUG_FAR, row < AUG_FAR + 3)
        pad_aug = jnp.where(row == AUG_PAD, -MASK_BIG, jnp.where(is_far, 1.0, 0.0)).astype(BF16)
        win_aug = jnp.where(is_far, 1.0, 0.0).astype(BF16)
        lane512 = lax.broadcasted_iota(jnp.int32, (GROUP_ROWS, LANES), 1)
        causal = _tile4(jnp.where(ci <= ri, 0.0, -MASK_BIG))
        acc_scr[...] = jnp.zeros(acc_scr.shape, F32)
        edge_scr[...] = jnp.concatenate([jnp.full((GROUP_ROWS, LANES), -MASK_BIG, F32),
                                         _tile4(jnp.where(ri < ci, 0.0, -MASK_BIG))], axis=1)
        for hk in range(nkv):
            hc = slice(hk * HEAD_DIM, (hk + 1) * HEAD_DIM)
            far = tz_ref[2, hk]
            hi, mid, lo = _split3(far)
            caug = jnp.where(lane512 == AUG_FAR, hi, jnp.where(lane512 == AUG_FAR + 1, mid,
                             jnp.where(lane512 == AUG_FAR + 2, lo, jnp.where(lane512 == AUG_PAD, 1.0, 0.0))))
            caug_scr[hk] = caug.astype(BF16)
            qw_scr[hk, :, HEAD_DIM:] = caug.astype(BF16)
            near_scr[hk] = jnp.concatenate([tz_ref[1, hk] - far, (tz_ref[0, hk] - far) + causal], axis=1)
            ksa_scr[hk, 0, 0:HEAD_DIM, :] = jnp.zeros((HEAD_DIM, LANES), BF16)
            ksa_scr[hk, 0, HEAD_DIM:, :] = pad_aug
            for t in range(seq // LANES):
                rows = slice(t * LANES, (t + 1) * LANES)
                ksa_scr[hk, SEL_PAD_TILES + t, 0:HEAD_DIM, :] = ks_ref[0, rows, hc].T
                ksa_scr[hk, SEL_PAD_TILES + t, HEAD_DIM:, :] = jnp.where(
                    (t * LANES + ci) // SEL_BLOCK == ri, -MASK_BIG, jnp.where(is_far, 1.0, 0.0)).astype(BF16)
                kwp_scr[hk, WIN_PAD // LANES + t, 0:HEAD_DIM, :] = kw_ref[0, rows, hc].T
                kwp_scr[hk, WIN_PAD // LANES + t, HEAD_DIM:, :] = win_aug
            for t in range(WIN_PAD // LANES):
                kwp_scr[hk, t, 0:HEAD_DIM, :] = jnp.zeros((HEAD_DIM, LANES), BF16)
                kwp_scr[hk, t, HEAD_DIM:, :] = pad_aug
            vsa_scr[hk, 0:SEL_PAD_TILES * LANES, :] = jnp.zeros((SEL_PAD_TILES * LANES, 2 * HEAD_DIM), BF16)
            vsa_scr[hk, SEL_PAD_TILES * LANES:, 0:HEAD_DIM] = vs_ref[0, :, hc]
            vsa_scr[hk, SEL_PAD_TILES * LANES:, HEAD_DIM:] = ones
            vwa_scr[hk, 0:WIN_PAD, :] = jnp.zeros((WIN_PAD, 2 * HEAD_DIM), BF16)
            vwa_scr[hk, WIN_PAD:, 0:HEAD_DIM] = vw_ref[0, :, hc]
            vwa_scr[hk, WIN_PAD:, HEAD_DIM:] = ones

    def reset(br):
        m_scr[br] = jnp.full(m_scr.shape[1:], NEG, F32)

    def online_update(br, hk, s, vaug):
        m_prev = m_scr[br, hk]
        m_new = jnp.maximum(m_prev, jnp.max(s, axis=1, keepdims=True))
        alpha = jnp.exp2(m_prev - m_new)
        p = jnp.exp2(s - _lane_tile(m_new, s.shape[1] // LANES))
        pv = jnp.dot(p.astype(BF16), vaug, preferred_element_type=F32)
        acc_scr[br, hk] = _lane_tile(alpha, 2) * acc_scr[br, hk] + pv
        m_scr[br, hk] = m_new

    def finish(br, hk):
        acc = acc_scr[br, hk]
        return acc[:, 0:HEAD_DIM] / acc[:, HEAD_DIM:]

    for hk in range(nkv):
        for g in range(NSA_GROUP):
            h = hk * NSA_GROUP + g
            q_h = q_ref[0, :, h * HEAD_DIM:(h + 1) * HEAD_DIM]
            qg_scr[hk, g * LANES:(g + 1) * LANES, 0:HEAD_DIM] = q_h
            qw_scr[hk, g * LANES:(g + 1) * LANES, 0:HEAD_DIM] = q_h

    for hk in range(nkv):
        qg = qg_scr[hk, :, 0:HEAD_DIM]

        kc = kcvc_ref[0, 0, hk]
        vc = kcvc_ref[0, 1, hk]
        s = lax.dot_general(qg, kc, NT_DIMS, preferred_element_type=F32) + bc_ref[0, hk]
        rc = lax.broadcasted_iota(jnp.int32, (LANES, ncp), 0)
        cc = lax.broadcasted_iota(jnp.int32, (LANES, ncp), 1)
        valid_c = _tile4(jnp.where(q0 + rc - (CMP_STRIDE * cc + CMP_BLOCK - 1) >= 0, 1.0, 0.0)) > 0.5
        s = jnp.where(valid_c, s, NEG)
        p = jnp.where(valid_c, jnp.exp2(s - jnp.max(s, axis=1, keepdims=True)), 0.0)
        l = jnp.sum(p, axis=1, keepdims=True)
        p = p / jnp.where(l > 0.0, l, 1.0)
        o_scr[hk] = gate_col(hk, 0) * jnp.dot(p.astype(BF16), vc, preferred_element_type=F32)

        psum = p[0:LANES]
        for g in range(1, NSA_GROUP):
            psum = psum + p[g * LANES:(g + 1) * LANES]
        p_hi = psum.astype(BF16)
        p_lo = (psum - p_hi.astype(F32)).astype(BF16)
        selmt = selmt_ref[...]
        imp = (lax.dot_general(selmt, p_hi, NT_DIMS, preferred_element_type=F32)
               + lax.dot_general(selmt, p_lo, NT_DIMS, preferred_element_type=F32))
        imp = imp[0:n_slc]
        blk = lax.broadcasted_iota(jnp.int32, (n_slc, LANES), 0)
        cur = (q0 + lax.broadcasted_iota(jnp.int32, (n_slc, LANES), 1)) // SEL_BLOCK
        forced = (blk == 0) | (blk == cur) | (blk == cur - 1)
        imp = jnp.where(forced, FORCE, imp)
        imp = jnp.where(blk <= cur, imp, -jnp.inf)
        rank = jnp.zeros((n_slc, LANES), F32)
        for j in range(n_slc):
            row = imp[j:j + 1, :]
            beats = jnp.where(row > imp, 1.0, jnp.where(row == imp, jnp.where(blk > j, 1.0, 0.0), 0.0))
            rank = rank + beats
        sel_t = jnp.where(rank < top_n, jnp.where(imp > -jnp.inf, 1.0, 0.0), 0.0)
        if n_slc < LANES:
            sel_t = jnp.concatenate([sel_t, jnp.zeros((LANES - n_slc, LANES), F32)], axis=0)
        sel_q = lax.dot_general(eye, sel_t.astype(BF16), NT_DIMS, preferred_element_type=F32)
        not_sel = jnp.where(ci < n_slc, 1.0 - sel_q, 0.0).astype(BF16)
        for g in range(NSA_GROUP):
            rows = slice(g * LANES, (g + 1) * LANES)
            qg_scr[hk, rows, HEAD_DIM:] = jnp.where(ci < n_slc, not_sel, caug_scr[hk, rows, :])

    reset(WIN)
    for off, table in ((1, near_scr), (3, None), (5, edge_scr)):
        t0 = qt - off + WIN_PAD // LANES
        p0 = pl.multiple_of(t0 * LANES, LANES)
        for hk in range(nkv):
            k_t = jnp.concatenate([kwp_scr[hk, t0], kwp_scr[hk, t0 + 1]], axis=1)
            s = jnp.dot(qw_scr[hk], k_t, preferred_element_type=F32)
            if table is not None:
                s = s + (table[hk] if table is near_scr else table[...])
            online_update(WIN, hk, s, vwa_scr[hk, pl.ds(p0, KEY_CHUNK), :])

    reset(SEL)
    n_chunks = (qt + 2) // 2
    first_tile = SEL_PAD_TILES - (qt + 1) % 2

    def sel_scores(hk, c):
        t0 = first_tile + 2 * c
        k_t = jnp.concatenate([ksa_scr[hk, t0], ksa_scr[hk, t0 + 1]], axis=1)
        return jnp.dot(qg_scr[hk], k_t, preferred_element_type=F32)

    def sel_chunk(c, last):
        p0 = pl.multiple_of((first_tile + 2 * c) * LANES, LANES)
        for hk in range(nkv):
            s = s_scr[hk]
            if last:
                s = s + near_scr[hk]
            else:
                s_scr[hk] = sel_scores(hk, c + 1)
            online_update(SEL, hk, s, vsa_scr[hk, pl.ds(p0, KEY_CHUNK), :])

    def sel_pair(i, carry):
        sel_chunk(2 * i, False)
        sel_chunk(2 * i + 1, False)
        return carry

    for hk in range(nkv):
        s_scr[hk] = sel_scores(hk, 0)
    n_far = n_chunks - 1
    lax.fori_loop(0, n_far // 2, sel_pair, 0)
    pl.when(n_far % 2 == 1)(lambda: sel_chunk(n_far - 1, False))
    sel_chunk(n_chunks - 1, True)
    for hk in range(nkv):
        o = (o_scr[hk] + gate_col(hk, 1) * finish(SEL, hk)) + gate_col(hk, 2) * finish(WIN, hk)
        for g in range(NSA_GROUP):
            h = hk * NSA_GROUP + g
            o_ref[0, :, h * HEAD_DIM:(h + 1) * HEAD_DIM] = o[g * LANES:(g + 1) * LANES]


def _nsa_call(o16v, o32v, kcvc, bias_c, tz, selmt, weights_f32, batch, seq, ncp):
    nqt = seq // LANES
    wspecs = []
    for w in weights_f32:
        blk, span = _cast_block(w, batch * nqt)
        wspecs.append(pl.BlockSpec(blk, functools.partial(lambda b, t, span: ((b * nqt + t) // span, 0), span=span)))
    kvw = NSA_KV_WIDTH
    nkv = N_NSA_KV_HEADS
    base = (3 * FOX_WIDTH + NSA_WIDTH) // kvw
    return pl.pallas_call(
        functools.partial(_nsa_kernel, seq=seq, ncp=ncp),
        grid=(batch, nqt),
        in_specs=[pl.BlockSpec((1, LANES, NSA_WIDTH), lambda b, t: (b, t, 3 * FOX_WIDTH // NSA_WIDTH)),
                  pl.BlockSpec((1, seq, kvw), lambda b, t: (b, 0, base)),
                  pl.BlockSpec((1, seq, kvw), lambda b, t: (b, 0, base + 1)),
                  pl.BlockSpec((1, seq, kvw), lambda b, t: (b, 0, base + 2)),
                  pl.BlockSpec((1, seq, kvw), lambda b, t: (b, 0, base + 3)),
                  pl.BlockSpec((1, LANES, LANES), lambda b, t: (b, t, COL_NG // LANES)),
                  pl.BlockSpec((1, 2, N_NSA_KV_HEADS, ncp, HEAD_DIM), lambda b, t: (b, 0, 0, 0, 0)),
                  pl.BlockSpec((1, N_NSA_KV_HEADS, GROUP_ROWS, ncp), lambda b, t: (t, 0, 0, 0)),
                  pl.BlockSpec((3, N_NSA_KV_HEADS, GROUP_ROWS, LANES), lambda b, t: (0, 0, 0, 0)),
                  pl.BlockSpec((LANES, ncp), lambda b, t: (0, 0))] + wspecs,
        out_specs=[pl.BlockSpec((1, LANES, NSA_WIDTH), lambda b, t: (b, t, 0))] + wspecs,
        out_shape=[jax.ShapeDtypeStruct((batch, seq, NSA_WIDTH), F32)]
        + [jax.ShapeDtypeStruct(w.shape, BF16) for w in weights_f32],
        scratch_shapes=[pltpu.VMEM((nkv, GROUP_ROWS, 2 * HEAD_DIM), BF16),
                        pltpu.VMEM((nkv, GROUP_ROWS, 2 * HEAD_DIM), BF16),
                        pltpu.VMEM((nkv, GROUP_ROWS, LANES), BF16),
                        pltpu.VMEM((nkv, GROUP_ROWS, KEY_CHUNK), F32),
                        pltpu.VMEM((GROUP_ROWS, KEY_CHUNK), F32),
                        pltpu.VMEM((2, nkv, GROUP_ROWS, LANES), F32),
                        pltpu.VMEM((2, nkv, GROUP_ROWS, 2 * HEAD_DIM), F32),
                        pltpu.VMEM((nkv, GROUP_ROWS, HEAD_DIM), F32),
                        pltpu.VMEM((nkv, GROUP_ROWS, KEY_CHUNK), F32),
                        pltpu.VMEM((nkv, SEL_PAD_TILES + seq // LANES, 2 * HEAD_DIM, LANES), BF16),
                        pltpu.VMEM((nkv, SEL_PAD_TILES * LANES + seq, 2 * HEAD_DIM), BF16),
                        pltpu.VMEM((nkv, (seq + WIN_PAD) // LANES, 2 * HEAD_DIM, LANES), BF16),
                        pltpu.VMEM((nkv, seq + WIN_PAD, 2 * HEAD_DIM), BF16)],
        compiler_params=_params(2),
        name="nsa_attention",
    )(o16v, o16v, o16v, o16v, o16v, o32v, kcvc, bias_c, tz, selmt, *weights_f32)


def _out_kernel(of_ref, on_ref, gain_ref, w_ref, x_ref, g_ref, o_ref, y_scr, *, tm, n_chains):
    rows_per_chain = tm // n_chains
    gain_f = gain_ref[:, 0:FOX_WIDTH]
    gain_n = gain_ref[:, FOX_WIDTH:MIX_WIDTH]
    for c in range(n_chains):
        rows = slice(c * rows_per_chain, (c + 1) * rows_per_chain)
        for r0 in range(c * rows_per_chain, (c + 1) * rows_per_chain, NORM_ROWS):
            r = slice(r0, r0 + NORM_ROWS)
            y_scr[r, 0:FOX_WIDTH] = (_rms(of_ref[r, :]) * gain_f).astype(BF16)
            y_scr[r, FOX_WIDTH:MIX_WIDTH] = (_rms(on_ref[r, :]) * gain_n).astype(BF16)
        acc = jnp.dot(y_scr[rows, :], w_ref[...], preferred_element_type=F32)
        o_ref[rows, :] = x_ref[rows, :] + g_ref[0] * acc


def _out_call(o_fox, o_nsa, gain, w_out, x2, g1, seq):
    t, d = x2.shape
    tm = min(512, seq)
    rows_per_batch = seq // tm
    return pl.pallas_call(
        functools.partial(_out_kernel, tm=tm, n_chains=2),
        grid=(t // tm,),
        in_specs=[pl.BlockSpec((tm, FOX_WIDTH), lambda i: (i, 0)),
                  pl.BlockSpec((tm, NSA_WIDTH), lambda i: (i, 0)),
                  pl.BlockSpec((1, MIX_WIDTH), lambda i: (0, 0)),
                  pl.BlockSpec((MIX_WIDTH, d), lambda i: (0, 0)),
                  pl.BlockSpec((tm, d), lambda i: (i, 0)),
                  pl.BlockSpec((1, 1, d), lambda i: (i // rows_per_batch, 0, 0))],
        out_specs=pl.BlockSpec((tm, d), lambda i: (i, 0)),
        out_shape=jax.ShapeDtypeStruct((t, d), F32),
        scratch_shapes=[pltpu.VMEM((tm, MIX_WIDTH), BF16)],
        compiler_params=_params(1),
        name="out_proj",
    )(o_fox, o_nsa, gain, w_out, x2, g1)


def _ffn_kernel(x_ref, gain_ref, sc_ref, sh_ref, g_ref, wg_ref, wu_ref, wd_ref, o_ref, h_scr, *, n_f):
    f = pl.program_id(1)
    tm = h_scr.shape[0]

    def hidden_step(first_row, n_rows, first, last):
        rows = slice(first_row, first_row + n_rows)
        h = h_scr[rows, :]
        a = jnp.dot(h, wg_ref[...], preferred_element_type=F32)
        u = jnp.dot(h, wu_ref[...], preferred_element_type=F32)
        t = (a * _sigmoid(a)) * u
        part = jnp.dot(t.astype(BF16), wd_ref[...], preferred_element_type=F32)
        acc = part if first else o_ref[rows, :] + part
        o_ref[rows, :] = x_ref[rows, :] + g_ref[0] * acc if last else acc

    def edge_step(first, last):
        n_rows = tm // ROW_CHAINS
        for c in range(ROW_CHAINS):
            if first:
                _modulated_norm(x_ref, gain_ref, sc_ref, sh_ref, h_scr, c * n_rows, n_rows)
            hidden_step(c * n_rows, n_rows, first, last)

    if n_f == 1:
        edge_step(True, True)
    else:
        pl.when(f == 0)(functools.partial(edge_step, True, False))
        pl.when(jnp.logical_and(f > 0, f < n_f - 1))(functools.partial(hidden_step, 0, tm, False, False))
        pl.when(f == n_f - 1)(functools.partial(edge_step, False, True))


def _ffn_call(x1, gain2, sc, sh, g2, wg, wu, wd, seq):
    t, d = x1.shape
    dff = wg.shape[1]
    tm = min(1024, seq)
    tf = 512 if dff % 512 == 0 else dff
    rows_per_batch = seq // tm
    return pl.pallas_call(
        functools.partial(_ffn_kernel, n_f=dff // tf),
        grid=(t // tm, dff // tf),
        in_specs=[pl.BlockSpec((tm, d), lambda i, f: (i, 0)),
                  pl.BlockSpec((1, d), lambda i, f: (0, 0)),
                  pl.BlockSpec((1, 1, d), lambda i, f: (i // rows_per_batch, 0, 0)),
                  pl.BlockSpec((1, 1, d), lambda i, f: (i // rows_per_batch, 0, 0)),
                  pl.BlockSpec((1, 1, d), lambda i, f: (i // rows_per_batch, 0, 0)),
                  pl.BlockSpec((d, tf), lambda i, f: (0, f)),
                  pl.BlockSpec((d, tf), lambda i, f: (0, f)),
                  pl.BlockSpec((tf, d), lambda i, f: (f, 0))],
        out_specs=pl.BlockSpec((tm, d), lambda i, f: (i, 0)),
        out_shape=jax.ShapeDtypeStruct((t, d), F32),
        scratch_shapes=[pltpu.VMEM((tm, d), BF16)],
        compiler_params=_params(2),
        name="swiglu_ffn",
    )(x1, gain2, sc, sh, g2, wg, wu, wd)


def _selection_matrix_t(ncp, n_slc):
    r, q = SEL_BLOCK // CMP_STRIDE, CMP_BLOCK // CMP_STRIDE
    m = np.zeros((LANES, ncp), np.float32)
    for j in range(n_slc):
        for a in range(r):
            for b in range(q):
                c = r * j + a - b
                if 0 <= c < ncp:
                    m[j, c] += 1.0
    return m


def _w_in_block_sources():
    o = 0
    start = {}
    for name, width in (("fq", FOX_WIDTH), ("fk", FOX_WIDTH), ("fv", FOX_WIDTH), ("ff", N_FOX_HEADS),
                        ("nq", NSA_WIDTH), ("nk", N_BRANCH * NSA_KV_WIDTH), ("nv", N_BRANCH * NSA_KV_WIDTH),
                        ("ng", N_BRANCH * N_NSA_HEADS)):
        start[name] = o
        o += width
    kvw = NSA_KV_WIDTH
    groups = [(start["fq"], 3 * FOX_WIDTH), (start["nq"], NSA_WIDTH),
              (start["nk"] + kvw, 2 * kvw), (start["nv"] + kvw, 2 * kvw),
              (start["nk"], kvw), (start["nv"], kvw)]
    blocks = [s + LANES * b for s, width in groups for b in range(width // LANES)]
    return blocks, (start["ff"], N_FOX_HEADS), (start["ng"], N_BRANCH * N_NSA_HEADS)


def _repack_kernel(src_ref, wt_hbm, o16_ref, o32_ref, buf, sem, *, n16, n_whole, ff, ng):
    k = pl.program_id(0)
    n_slots = buf.shape[0]
    ahead = n_slots - 1
    slot = k % n_slots

    def whole_copy(kk, s):
        r0 = pl.multiple_of(src_ref[kk], 8)
        return pltpu.make_async_copy(wt_hbm.at[pl.ds(r0, LANES), :], buf.at[s], sem.at[s])

    def narrow_copies(s):
        return [pltpu.make_async_copy(wt_hbm.at[pl.ds(ff[0], ff[1]), :], buf.at[s, pl.ds(0, ff[1]), :], sem.at[s]),
                pltpu.make_async_copy(wt_hbm.at[pl.ds(ng[0], ng[1]), :], buf.at[s, pl.ds(ff[1], ng[1]), :],
                                      sem.at[s])]

    def start_block(kk):
        @pl.when(kk < n_whole)
        def _():
            whole_copy(kk, kk % n_slots).start()

        @pl.when(kk == n_whole)
        def _():
            for cp in narrow_copies(kk % n_slots):
                cp.start()

    @pl.when(k == 0)
    def _():
        for kk in range(ahead):
            start_block(kk)

    start_block(k + ahead)

    @pl.when(k < n_whole)
    def _():
        whole_copy(k, slot).wait()

    @pl.when(k == n_whole)
    def _():
        for cp in narrow_copies(slot):
            cp.wait()

    xt = buf[slot].T
    lane = lax.broadcasted_iota(jnp.int32, xt.shape, 1)
    xt = jnp.where(jnp.logical_or(k < n_whole, lane < ff[1] + ng[1]), xt, 0.0).astype(BF16)

    @pl.when(k < n16)
    def _():
        o16_ref[...] = xt

    @pl.when(k >= n16)
    def _():
        o32_ref[...] = xt


def _repack_w_in(wt):
    n, d = wt.shape
    blocks, ff, ng = _w_in_block_sources()
    n16 = W16 // LANES
    n_whole = len(blocks)
    assert n_whole + 1 == (W16 + W32) // LANES
    return pl.pallas_call(
        functools.partial(_repack_kernel, n16=n16, n_whole=n_whole, ff=ff, ng=ng),
        grid_spec=pltpu.PrefetchScalarGridSpec(
            num_scalar_prefetch=1,
            grid=(n_whole + 1,),
            in_specs=[pl.BlockSpec(memory_space=pl.ANY)],
            out_specs=[pl.BlockSpec((d, LANES), lambda k, src: (0, jnp.minimum(k, n16 - 1))),
                       pl.BlockSpec((d, LANES), lambda k, src: (0, jnp.maximum(k - n16, 0)))],
            scratch_shapes=[pltpu.VMEM((REPACK_SLOTS, LANES, d), F32), pltpu.SemaphoreType.DMA((REPACK_SLOTS,))]),
        out_shape=[jax.ShapeDtypeStruct((d, W16), BF16), jax.ShapeDtypeStruct((d, W32), BF16)],
        compiler_params=_params(1),
        name="w_in_repack",
    )(jnp.asarray(blocks, jnp.int32), wt)


def kernel(x, c, ada_w, ada_b, norm1_gain, norm2_gain, w_in, fox_f_bias, fox_q_gain, fox_k_gain, nsa_q_gain,
           nsa_k_gain, nsa_cmp_pos, nsa_cmp_w1, nsa_cmp_w2, rel_bias, mix_out_gain, w_out, ffn_w_gate, ffn_w_up,
           ffn_w_down):
    batch, seq, d = x.shape
    assert seq % KEY_CHUNK == 0 and seq >= WINDOW and d % LANES == 0 and seq // SEL_BLOCK <= AUG_FAR
    depth = ada_w.shape[0]
    nqt = seq // LANES
    ncp = -(-(seq // CMP_STRIDE) // LANES) * LANES

    selmt = jnp.asarray(_selection_matrix_t(ncp, seq // SEL_BLOCK), BF16)
    rb_flat = rel_bias.reshape(-1)
    bias_c = _bias_call(rb_flat, nqt, ncp, CMP_STRIDE, CMP_BLOCK - 1, "t5_bias_compressed", rolled=True)
    tz = _bias_call(rb_flat, 3, LANES, 1, 0, "t5_bias_toeplitz")

    ones_h = jnp.ones((HEAD_DIM,), F32)
    c_pad = jnp.pad(c, ((0, 8 - batch % 8 if batch % 8 else 0), (0, 0)))
    x2 = x.reshape(batch * seq, d)
    for layer in range(depth):
        mod = _ada_call(c_pad, ada_w[layer], ada_b[layer][None, :])[:batch]
        sh1, sc1, g1, sh2, sc2, g2 = [mod[:, i * d:(i + 1) * d][:, None, :] for i in range(N_MOD)]

        kg = nsa_k_gain[layer]
        col_gain = jnp.concatenate([
            jnp.tile(fox_q_gain[layer] * QSCALE, N_FOX_HEADS), jnp.tile(fox_k_gain[layer], N_FOX_HEADS),
            jnp.tile(ones_h, N_FOX_HEADS), jnp.tile(nsa_q_gain[layer] * QSCALE, N_NSA_HEADS),
            jnp.tile(kg[1], N_NSA_KV_HEADS), jnp.tile(kg[2], N_NSA_KV_HEADS),
            jnp.tile(ones_h, 2 * N_NSA_KV_HEADS)])[None, :]
        col_flag = jnp.concatenate([
            jnp.ones((2 * FOX_WIDTH,), F32), jnp.zeros((FOX_WIDTH,), F32), jnp.ones((NSA_WIDTH,), F32),
            jnp.ones((2 * NSA_KV_WIDTH,), F32), jnp.zeros((2 * NSA_KV_WIDTH,), F32)])[None, :]
        w16, w32 = _repack_w_in(jnp.swapaxes(w_in, 1, 2)[layer])
        o16, o32 = _proj_call(x2, sc1, sh1, norm1_gain[layer][None, :], w16, w32,
                              col_gain, col_flag, seq)
        o16v = o16.reshape(batch, seq, W16)
        o32v = o32.reshape(batch, seq, W32)

        fb_pad = jnp.pad(fox_f_bias[layer], (0, LANES - N_FOX_HEADS))[None, :]
        cs = _cum_call(o32, fb_pad, batch, seq, N_FOX_HEADS // FOX_HEADS_PER_STEP)
        o_fox = _fox_call(o16v, cs, batch, seq)

        w1 = nsa_cmp_w1[layer].reshape(2, CMP_BLOCK, HEAD_DIM, HEAD_DIM)
        kcvc = _cmp_call(o32, nsa_cmp_pos[layer], w1, nsa_cmp_w2[layer], kg[0][None, :],
                         batch, seq, ncp)
        o_nsa, wg16, wu16, wo16, wd16 = _nsa_call(
            o16v, o32v, kcvc, bias_c, tz, selmt,
            (ffn_w_gate[layer], ffn_w_up[layer], w_out[layer], ffn_w_down[layer]), batch, seq, ncp)

        x1 = _out_call(o_fox.reshape(batch * seq, FOX_WIDTH), o_nsa.reshape(batch * seq, NSA_WIDTH),
                       mix_out_gain[layer][None, :], wo16, x2, g1, seq)
        x2 = _ffn_call(x1, norm2_gain[layer][None, :], sc2, sh2, g2, wg16, wu16, wd16, seq)
    return x2.reshape(batch, seq, d)
```

```python
import functools
import math

import numpy as np
import jax
import jax.numpy as jnp
from jax import lax
from jax.experimental import pallas as pl
from jax.experimental.pallas import tpu as pltpu

HEAD_DIM = 128
N_FOX_HEADS = 8
N_NSA_HEADS = 8
N_NSA_KV_HEADS = 2
NSA_GROUP = N_NSA_HEADS // N_NSA_KV_HEADS
FOX_WIDTH = N_FOX_HEADS * HEAD_DIM
NSA_WIDTH = N_NSA_HEADS * HEAD_DIM
NSA_KV_WIDTH = N_NSA_KV_HEADS * HEAD_DIM
MIX_WIDTH = FOX_WIDTH + NSA_WIDTH
N_BRANCH = 3
CMP_BLOCK = 32
CMP_STRIDE = 16
SEL_BLOCK = 64
N_SEL = 8
WINDOW = 512
N_BUCKETS = 32
MAX_DISTANCE = 128
N_MOD = 6
SCALE = HEAD_DIM ** -0.5
LOG2E = math.log2(math.e)
LOG2E_HI = float(np.float32(LOG2E))
LOG2E_LO = LOG2E - LOG2E_HI
QSCALE = SCALE * LOG2E
EPS = 1e-6
NEG = -1e30
FORCE = 1e6

LANES = 128
GROUP_ROWS = NSA_GROUP * LANES
VMEM_LIMIT = 56 * 1024 * 1024
MXU_COLS = 256
KEY_CHUNK = MXU_COLS
WIN_PAD = WINDOW + LANES
MASK_BIG = 2.0 ** 100
SEL, WIN = 0, 1
REPACK_SLOTS = 4
AUG_FAR, AUG_PAD = 120, 127
FOX_HEADS_PER_STEP = 4
SEL_PAD_TILES = 1

W16 = 3 * FOX_WIDTH + NSA_WIDTH + 4 * NSA_KV_WIDTH
W32 = 5 * LANES
COL_FF = 2 * NSA_KV_WIDTH
COL_NG = COL_FF + N_FOX_HEADS

F32 = jnp.float32
BF16 = jnp.bfloat16
NT_DIMS = (((1,), (1,)), ((), ()))


def _params(n_axes):
    return pltpu.CompilerParams(dimension_semantics=("arbitrary",) * n_axes,
                                vmem_limit_bytes=VMEM_LIMIT)


def _sigmoid(x):
    return 1.0 / (1.0 + jnp.exp(-x))


def _lane_tile(a, n):
    return jnp.concatenate([a] * n, axis=1)


BF16_SUBLANES = 16


def _cast_block(w, n_steps):
    rows, cols = w.shape
    assert rows % n_steps == 0, (w.shape, n_steps)
    per_step = rows // n_steps
    span = BF16_SUBLANES // math.gcd(BF16_SUBLANES, per_step)
    assert n_steps % span == 0, (w.shape, n_steps)
    return (per_step * span, cols), span


def _rms(x):
    return x * lax.rsqrt(jnp.mean(x * x, axis=-1, keepdims=True) + EPS)


NORM_ROWS = 16


def _modulated_norm(x_ref, gain_ref, sc_ref, sh_ref, h_ref, first_row, n_rows):
    gm = gain_ref[...] * (1.0 + sc_ref[0])
    sh = sh_ref[0]
    for r0 in range(first_row, first_row + n_rows, NORM_ROWS):
        rows = slice(r0, r0 + NORM_ROWS)
        h_ref[rows, :] = (_rms(x_ref[rows, :]) * gm + sh).astype(BF16)


ROW_CHAINS = 2


def _ada_kernel(c_ref, w_ref, b_ref, o_ref):
    c = c_ref[...]
    s = (c * _sigmoid(c)).astype(BF16)
    o_ref[...] = jnp.dot(s, w_ref[...].astype(BF16), preferred_element_type=F32) + b_ref[...]


def _ada_call(c_pad, w, b):
    rows, d = c_pad.shape
    n = w.shape[1]
    tn = next(t for t in (1024, 768, 512, 384, 256, 128) if n % t == 0)
    return pl.pallas_call(
        _ada_kernel,
        grid=(n // tn,),
        in_specs=[pl.BlockSpec((rows, d), lambda j: (0, 0)),
                  pl.BlockSpec((d, tn), lambda j: (0, j)),
                  pl.BlockSpec((1, tn), lambda j: (0, j))],
        out_specs=pl.BlockSpec((rows, tn), lambda j: (0, j)),
        out_shape=jax.ShapeDtypeStruct((rows, n), F32),
        compiler_params=_params(1),
        name="adaln",
    )(c_pad, w, b)


def _proj_kernel(x_ref, sc_ref, sh_ref, g_ref, w16_ref, w32_ref, gain_ref, flag_ref, o16_ref, o32_ref, h_scr, *, tn):
    j = pl.program_id(1)
    tm = h_scr.shape[0]

    def column_step(first_row, n_rows, with_side_outputs):
        rows = slice(first_row, first_row + n_rows)
        h = h_scr[rows, :]
        for c in range(tn // MXU_COLS):
            acc = jnp.dot(h, w16_ref[:, c * MXU_COLS:(c + 1) * MXU_COLS], preferred_element_type=F32)
            for g in range(MXU_COLS // LANES):
                cols = slice(c * MXU_COLS + g * LANES, c * MXU_COLS + (g + 1) * LANES)
                a = acc[:, g * LANES:(g + 1) * LANES]
                r = lax.rsqrt(jnp.mean(a * a, axis=-1, keepdims=True) + EPS)
                scale = jnp.where(flag_ref[:, cols] > 0.5, r, 1.0)
                o16_ref[rows, cols] = (a * scale * gain_ref[:, cols]).astype(BF16)
        if with_side_outputs:
            o32_ref[rows, :] = jnp.dot(h, w32_ref[...], preferred_element_type=F32)

    def step(first, with_side_outputs):
        n_rows = tm // ROW_CHAINS
        for c in range(ROW_CHAINS):
            if first:
                _modulated_norm(x_ref, g_ref, sc_ref, sh_ref, h_scr, c * n_rows, n_rows)
            column_step(c * n_rows, n_rows, with_side_outputs)

    last = pl.num_programs(1) - 1
    pl.when(j == 0)(functools.partial(step, True, False))
    pl.when(jnp.logical_and(j > 0, j < last))(functools.partial(step, False, False))
    pl.when(j == last)(functools.partial(step, False, True))


def _proj_call(x2, sc, sh, gain1, w16, w32, col_gain, col_flag, seq):
    t, d = x2.shape
    tm = min(1024, seq)
    tn = 1280 if W16 % 1280 == 0 else 1024
    assert W16 // tn >= 2
    rows_per_batch = seq // tm
    return pl.pallas_call(
        functools.partial(_proj_kernel, tn=tn),
        grid=(t // tm, W16 // tn),
        in_specs=[pl.BlockSpec((tm, d), lambda i, j: (i, 0)),
                  pl.BlockSpec((1, 1, d), lambda i, j: (i // rows_per_batch, 0, 0)),
                  pl.BlockSpec((1, 1, d), lambda i, j: (i // rows_per_batch, 0, 0)),
                  pl.BlockSpec((1, d), lambda i, j: (0, 0)),
                  pl.BlockSpec((d, tn), lambda i, j: (0, j)),
                  pl.BlockSpec((d, W32), lambda i, j: (0, 0)),
                  pl.BlockSpec((1, tn), lambda i, j: (0, j)),
                  pl.BlockSpec((1, tn), lambda i, j: (0, j))],
        out_specs=[pl.BlockSpec((tm, tn), lambda i, j: (i, j)),
                   pl.BlockSpec((tm, W32), lambda i, j: (i, 0))],
        out_shape=[jax.ShapeDtypeStruct((t, W16), BF16),
                   jax.ShapeDtypeStruct((t, W32), F32)],
        scratch_shapes=[pltpu.VMEM((tm, d), BF16)],
        compiler_params=_params(2),
        name="in_proj",
    )(x2, sc, sh, gain1, w16, w32, col_gain, col_flag)


def _split3(c):
    hi = c.astype(BF16).astype(F32)
    r1 = c - hi
    mid = r1.astype(BF16).astype(F32)
    return hi, mid, (r1 - mid).astype(BF16).astype(F32)


def _cum_kernel(ff_ref, fb_ref, o_ref, *, seq, groups):
    ri = lax.broadcasted_iota(jnp.int32, (LANES, LANES), 0)
    ci = lax.broadcasted_iota(jnp.int32, (LANES, LANES), 1)
    tri = jnp.where(ri >= ci, 1.0, 0.0).astype(BF16)
    carry = jnp.zeros((1, LANES), F32)
    for blk in range(seq // LANES):
        rows = slice(blk * LANES, (blk + 1) * LANES)
        x = ff_ref[0, rows, :] + fb_ref[...]
        lf = jnp.minimum(x, 0.0) - jnp.log(1.0 + jnp.exp(-jnp.abs(x)))
        hi = lf.astype(BF16)
        r1 = lf - hi.astype(F32)
        mid = r1.astype(BF16)
        lo = (r1 - mid.astype(F32)).astype(BF16)
        c = (jnp.dot(tri, hi, preferred_element_type=F32)
             + jnp.dot(tri, mid, preferred_element_type=F32)
             + jnp.dot(tri, lo, preferred_element_type=F32)) + carry
        carry = c[LANES - 1:LANES, :]
        terms = _split3(c * LOG2E_HI + c * LOG2E_LO)
        per = N_FOX_HEADS // groups
        for g in range(groups):
            packed = jnp.zeros((LANES, LANES), F32)
            for t, term in enumerate(terms):
                shift = (t * per - g * per) % LANES
                moved = term if shift == 0 else pltpu.roll(term, shift, 1)
                packed = jnp.where(jnp.logical_and(ci >= t * per, ci < (t + 1) * per), moved, packed)
            o_ref[0, g, rows, :] = packed


def _cum_call(o32, fb_pad, batch, seq, groups):
    o32v = o32.reshape(batch, seq, W32)
    return pl.pallas_call(
        functools.partial(_cum_kernel, seq=seq, groups=groups),
        grid=(batch,),
        in_specs=[pl.BlockSpec((1, seq, LANES), lambda b: (b, 0, COL_FF // LANES)),
                  pl.BlockSpec((1, LANES), lambda b: (0, 0))],
        out_specs=pl.BlockSpec((1, groups, seq, LANES), lambda b: (b, 0, 0, 0)),
        out_shape=jax.ShapeDtypeStruct((batch, groups, seq, LANES), F32),
        compiler_params=_params(1),
        name="fox_cumsum",
    )(o32v, fb_pad)


def _fox_kernel(q_ref, k_ref, v_ref, cs_ref, o_ref, m_scr, acc_scr, s_scr, kaug_scr, vaug_scr, *, seq, tq, tk, nh):
    n_diag = tq // tk
    lane_q = lax.broadcasted_iota(jnp.int32, (tq, LANES), 1)
    row8 = lax.broadcasted_iota(jnp.int32, (8, tk), 0)
    ri = lax.broadcasted_iota(jnp.int32, (tq, tk), 0)
    ci = lax.broadcasted_iota(jnp.int32, (tq, tk), 1)
    causal = [ci + d * tk <= ri for d in range(n_diag)]
    for c in range(seq // tk):
        rows = slice(c * tk, (c + 1) * tk)
        terms_t = cs_ref[0, 0, rows, :].T
        for j in range(nh):
            hi, mid, lo = [terms_t[t * nh + j:t * nh + j + 1] for t in range(3)]
            tail8 = jnp.where(row8 < 3, 1.0, jnp.where(row8 == 3, -hi, jnp.where(
                row8 == 4, -mid, jnp.where(row8 == 5, -lo, 0.0))))
            kaug_scr[j, c, 0:HEAD_DIM, :] = k_ref[0, rows, j * HEAD_DIM:(j + 1) * HEAD_DIM].T
            kaug_scr[j, c, HEAD_DIM:, :] = jnp.concatenate(
                [tail8, jnp.zeros((HEAD_DIM - 8, tk), F32)], axis=0).astype(BF16)
    for j in range(nh):
        vaug_scr[j, :, 0:HEAD_DIM] = v_ref[0, :, j * HEAD_DIM:(j + 1) * HEAD_DIM]
        vaug_scr[j, :, HEAD_DIM:] = jnp.ones((seq, HEAD_DIM), BF16)
    acc_scr[...] = jnp.zeros(acc_scr.shape, F32)

    def q_body(qi, carry):
        q0 = pl.multiple_of(qi * tq, tq)
        terms = cs_ref[0, 0, pl.ds(q0, tq), :]
        qs = []
        for j in range(nh):
            hi, mid, lo = [terms if t * nh + j == t else pltpu.roll(terms, (t - (t * nh + j)) % LANES, 1)
                           for t in range(3)]
            tail = jnp.where(lane_q == 0, hi, jnp.where(lane_q == 1, mid, jnp.where(lane_q == 2, lo,
                             jnp.where(lane_q < 6, 1.0, 0.0))))
            qs.append(jnp.concatenate([q_ref[0, pl.ds(q0, tq), j * HEAD_DIM:(j + 1) * HEAD_DIM],
                                       tail.astype(BF16)], axis=1))
            m_scr[j] = jnp.full((tq, LANES), NEG, F32)

        all_rows = slice(0, tq)

        def scores(j, ki, rows):
            return jnp.dot(qs[j][rows], kaug_scr[j, ki], preferred_element_type=F32)

        def tile(ki, mask, rows, next_rows):
            k0 = pl.multiple_of(ki * tk, tk)
            for j in range(nh):
                s = s_scr[j, rows, :]
                if mask is not None:
                    s = jnp.where(mask[rows], s, NEG)
                if next_rows is not None:
                    s_scr[j, next_rows, :] = scores(j, ki + 1, next_rows)
                m_prev = m_scr[j, rows, :]
                m_new = jnp.maximum(m_prev, jnp.max(s, axis=1, keepdims=True))
                alpha = jnp.exp2(m_prev - m_new)
                p = jnp.exp2(s - _lane_tile(m_new, tk // LANES))
                pv = jnp.dot(p.astype(BF16), vaug_scr[j, pl.ds(k0, tk), :], preferred_element_type=F32)
                acc_scr[j, rows, :] = _lane_tile(alpha, 2) * acc_scr[j, rows, :] + pv
                m_scr[j, rows, :] = m_new

        def k_body(ki, c2):
            tile(ki, None, all_rows, all_rows)
            return c2

        for j in range(nh):
            s_scr[j] = scores(j, 0, all_rows)
        n_off = qi * n_diag
        lax.fori_loop(0, n_off, k_body, 0)
        for d in range(n_diag):
            nxt = slice((d + 1) * tk, tq) if d + 1 < n_diag else None
            tile(n_off + d, causal[d], slice(d * tk, tq), nxt)
        for j in range(nh):
            acc = acc_scr[j]
            o_ref[0, pl.ds(q0, tq), j * HEAD_DIM:(j + 1) * HEAD_DIM] = acc[:, 0:HEAD_DIM] / acc[:, HEAD_DIM:]
        return carry

    lax.fori_loop(0, seq // tq, q_body, 0)


def _fox_call(o16v, cs, batch, seq):
    nh = FOX_HEADS_PER_STEP
    tq, tk = min(512, seq), min(512, seq)
    groups = N_FOX_HEADS // nh
    w = nh * HEAD_DIM
    return pl.pallas_call(
        functools.partial(_fox_kernel, seq=seq, tq=tq, tk=tk, nh=nh),
        grid=(batch, groups),
        in_specs=[pl.BlockSpec((1, seq, w), lambda b, h: (b, 0, h)),
                  pl.BlockSpec((1, seq, w), lambda b, h: (b, 0, groups + h)),
                  pl.BlockSpec((1, seq, w), lambda b, h: (b, 0, 2 * groups + h)),
                  pl.BlockSpec((1, 1, seq, LANES), lambda b, h: (b, h, 0, 0))],
        out_specs=pl.BlockSpec((1, seq, w), lambda b, h: (b, 0, h)),
        out_shape=jax.ShapeDtypeStruct((batch, seq, FOX_WIDTH), F32),
        scratch_shapes=[pltpu.VMEM((nh, tq, LANES), F32),
                        pltpu.VMEM((nh, tq, 2 * HEAD_DIM), F32),
                        pltpu.VMEM((nh, tq, tk), F32),
                        pltpu.VMEM((nh, seq // tk, 2 * HEAD_DIM, tk), BF16),
                        pltpu.VMEM((nh, seq, 2 * HEAD_DIM), BF16)],
        compiler_params=_params(2),
        name="fox_attention",
    )(o16v, o16v, o16v, cs)


def _cmp_kernel(x_ref, pos_ref, w1_ref, w2_ref, gain_ref, o_ref, xs_scr, *, seq, ncp):
    rows = xs_scr.shape[1]
    for kv in range(2):
        for h in range(N_NSA_KV_HEADS):
            n = kv * N_NSA_KV_HEADS + h
            xs_scr[n, 0:seq, :] = x_ref[0, :, n * HEAD_DIM:(n + 1) * HEAD_DIM]
            xs_scr[n, seq:rows, :] = jnp.zeros((rows - seq, LANES), F32)
            acc = jnp.zeros((ncp, HEAD_DIM), F32)
            for l in range(CMP_BLOCK):
                xl = xs_scr[n, pl.ds(l, ncp, stride=CMP_STRIDE), :] + pos_ref[kv, l:l + 1, :]
                acc = acc + jnp.dot(xl.astype(BF16), w1_ref[kv, l].astype(BF16), preferred_element_type=F32)
            hmid = acc * _sigmoid(acc)
            y = jnp.dot(hmid.astype(BF16), w2_ref[kv].astype(BF16), preferred_element_type=F32)
            if kv == 0:
                y = _rms(y) * gain_ref[...]
            o_ref[0, kv, h] = y.astype(BF16)


def _cmp_call(o32, pos, w1, w2, gain, batch, seq, ncp):
    o32v = o32.reshape(batch, seq, W32)
    hkv = N_NSA_KV_HEADS
    width = 2 * NSA_KV_WIDTH
    return pl.pallas_call(
        functools.partial(_cmp_kernel, seq=seq, ncp=ncp),
        grid=(batch,),
        in_specs=[pl.BlockSpec((1, seq, width), lambda b: (b, 0, 0)),
                  pl.BlockSpec((2, CMP_BLOCK, HEAD_DIM), lambda b: (0, 0, 0)),
                  pl.BlockSpec((2, CMP_BLOCK, HEAD_DIM, HEAD_DIM), lambda b: (0, 0, 0, 0)),
                  pl.BlockSpec((2, HEAD_DIM, HEAD_DIM), lambda b: (0, 0, 0)),
                  pl.BlockSpec((1, HEAD_DIM), lambda b: (0, 0))],
        out_specs=pl.BlockSpec((1, 2, hkv, ncp, HEAD_DIM), lambda b: (b, 0, 0, 0, 0)),
        out_shape=jax.ShapeDtypeStruct((batch, 2, hkv, ncp, HEAD_DIM), BF16),
        scratch_shapes=[pltpu.VMEM((2 * hkv, CMP_STRIDE * ncp + CMP_BLOCK, LANES), F32)],
        compiler_params=_params(1),
        name="nsa_compress",
    )(o32v, pos, w1, w2, gain)


def _bias_kernel(rb_ref, o_ref, *, width, key_stride, key_offset, first_tile, rolled_tiles):
    v = pl.program_id(0) + first_tile
    i = lax.broadcasted_iota(jnp.int32, (LANES, width), 0)
    j = lax.broadcasted_iota(jnp.int32, (LANES, width), 1)
    d = v * LANES + i - (key_stride * j + key_offset)
    n = jnp.maximum(d, 0)
    max_exact = N_BUCKETS // 2
    nf = jnp.maximum(n, 1).astype(F32)
    large = max_exact + jnp.trunc(jnp.log(nf / max_exact) / math.log(MAX_DISTANCE / max_exact)
                                  * (N_BUCKETS - max_exact))
    large = jnp.minimum(large, float(N_BUCKETS - 1))
    bkt = jnp.where(n < max_exact, n.astype(F32), large)
    vals = [jnp.zeros((LANES, width), F32) for _ in range(N_NSA_HEADS)]
    for bk in range(N_BUCKETS):
        hit = bkt == float(bk)
        for h in range(N_NSA_HEADS):
            vals[h] = jnp.where(hit, rb_ref[bk * N_NSA_HEADS + h] * LOG2E, vals[h])
    for h in range(N_NSA_HEADS):
        g = h % NSA_GROUP
        rows = slice(g * LANES, (g + 1) * LANES)
        if rolled_tiles:
            per_tile = LANES // key_stride
            for t in range(rolled_tiles):
                shift = (width - per_tile * (rolled_tiles - 1 - t)) % width
                o_ref[t, h // NSA_GROUP, rows, :] = vals[h] if shift == 0 else pltpu.roll(vals[h], shift, 1)
        else:
            o_ref[0, h // NSA_GROUP, rows, :] = vals[h]


def _bias_call(rb_flat, n_tiles, width, key_stride, key_offset, name, rolled=False):
    kern = functools.partial(_bias_kernel, width=width, key_stride=key_stride, key_offset=key_offset,
                             first_tile=n_tiles - 1 if rolled else 0, rolled_tiles=n_tiles if rolled else 0)
    block = (n_tiles if rolled else 1, N_NSA_KV_HEADS, GROUP_ROWS, width)
    return pl.pallas_call(
        kern,
        grid=(1 if rolled else n_tiles,),
        in_specs=[pl.BlockSpec(memory_space=pltpu.SMEM)],
        out_specs=pl.BlockSpec(block, lambda v: (v, 0, 0, 0)),
        out_shape=jax.ShapeDtypeStruct((n_tiles, N_NSA_KV_HEADS, GROUP_ROWS, width), F32),
        compiler_params=_params(1),
        name=name,
    )(rb_flat)


def _tile4(a):
    return jnp.concatenate([a] * NSA_GROUP, axis=0)


def _nsa_kernel(q_ref, ks_ref, kw_ref, vs_ref, vw_ref, g_ref, kcvc_ref, bc_ref, tz_ref, selmt_ref,
                wa_ref, wb_ref, wc_ref, wd_ref, o_ref, wa_out, wb_out, wc_out, wd_out,
                qg_scr, qw_scr, caug_scr, near_scr, edge_scr, m_scr, acc_scr, o_scr, s_scr,
                ksa_scr, vsa_scr, kwp_scr, vwa_scr, *, seq, ncp):
    qt = pl.program_id(1)
    q0 = qt * LANES
    n_slc = seq // SEL_BLOCK
    top_n = min(N_SEL, n_slc)
    nkv = N_NSA_KV_HEADS
    for src, dst in ((wa_ref, wa_out), (wb_ref, wb_out), (wc_ref, wc_out), (wd_ref, wd_out)):
        dst[...] = src[...].astype(BF16)
    ri = lax.broadcasted_iota(jnp.int32, (LANES, LANES), 0)
    ci = lax.broadcasted_iota(jnp.int32, (LANES, LANES), 1)
    eye = jnp.where(ri == ci, 1.0, 0.0).astype(BF16)
    gates = _sigmoid(g_ref[0])

    def gate_col(hk, br):
        cols = []
        for g in range(NSA_GROUP):
            c = COL_NG % LANES + (hk * NSA_GROUP + g) * N_BRANCH + br
            cols.append(gates[:, c:c + 1])
        return jnp.concatenate(cols, axis=0)

    @pl.when(qt == 0)
    def _():
        ones = jnp.ones((seq, HEAD_DIM), BF16)
        row = lax.broadcasted_iota(jnp.int32, (LANES, LANES), 0)
        is_far = jnp.logical_and(row >= AUG_FAR, row < AUG_FAR + 3)
        pad_aug = jnp.where(row == AUG_PAD, -MASK_BIG, jnp.where(is_far, 1.0, 0.0)).astype(BF16)
        win_aug = jnp.where(is_far, 1.0, 0.0).astype(BF16)
        lane512 = lax.broadcasted_iota(jnp.int32, (GROUP_ROWS, LANES), 1)
        causal = _tile4(jnp.where(ci <= ri, 0.0, -MASK_BIG))
        acc_scr[...] = jnp.zeros(acc_scr.shape, F32)
        edge_scr[...] = jnp.concatenate([jnp.full((GROUP_ROWS, LANES), -MASK_BIG, F32),
                                         _tile4(jnp.where(ri < ci, 0.0, -MASK_BIG))], axis=1)
        for hk in range(nkv):
            hc = slice(hk * HEAD_DIM, (hk + 1) * HEAD_DIM)
            far = tz_ref[2, hk]
            hi, mid, lo = _split3(far)
            caug = jnp.where(lane512 == AUG_FAR, hi, jnp.where(lane512 == AUG_FAR + 1, mid,
                             jnp.where(lane512 == AUG_FAR + 2, lo, jnp.where(lane512 == AUG_PAD, 1.0, 0.0))))
            caug_scr[hk] = caug.astype(BF16)
            qw_scr[hk, :, HEAD_DIM:] = caug.astype(BF16)
            near_scr[hk] = jnp.concatenate([tz_ref[1, hk] - far, (tz_ref[0, hk] - far) + causal], axis=1)
            ksa_scr[hk, 0, 0:HEAD_DIM, :] = jnp.zeros((HEAD_DIM, LANES), BF16)
            ksa_scr[hk, 0, HEAD_DIM:, :] = pad_aug
            for t in range(seq // LANES):
                rows = slice(t * LANES, (t + 1) * LANES)
                ksa_scr[hk, SEL_PAD_TILES + t, 0:HEAD_DIM, :] = ks_ref[0, rows, hc].T
                ksa_scr[hk, SEL_PAD_TILES + t, HEAD_DIM:, :] = jnp.where(
                    (t * LANES + ci) // SEL_BLOCK == ri, -MASK_BIG, jnp.where(is_far, 1.0, 0.0)).astype(BF16)
                kwp_scr[hk, WIN_PAD // LANES + t, 0:HEAD_DIM, :] = kw_ref[0, rows, hc].T
                kwp_scr[hk, WIN_PAD // LANES + t, HEAD_DIM:, :] = win_aug
            for t in range(WIN_PAD // LANES):
                kwp_scr[hk, t, 0:HEAD_DIM, :] = jnp.zeros((HEAD_DIM, LANES), BF16)
                kwp_scr[hk, t, HEAD_DIM:, :] = pad_aug
            vsa_scr[hk, 0:SEL_PAD_TILES * LANES, :] = jnp.zeros((SEL_PAD_TILES * LANES, 2 * HEAD_DIM), BF16)
            vsa_scr[hk, SEL_PAD_TILES * LANES:, 0:HEAD_DIM] = vs_ref[0, :, hc]
            vsa_scr[hk, SEL_PAD_TILES * LANES:, HEAD_DIM:] = ones
            vwa_scr[hk, 0:WIN_PAD, :] = jnp.zeros((WIN_PAD, 2 * HEAD_DIM), BF16)
            vwa_scr[hk, WIN_PAD:, 0:HEAD_DIM] = vw_ref[0, :, hc]
            vwa_scr[hk, WIN_PAD:, HEAD_DIM:] = ones

    def reset(br):
        m_scr[br] = jnp.full(m_scr.shape[1:], NEG, F32)

    def online_update(br, hk, s, vaug):
        m_prev = m_scr[br, hk]
        m_new = jnp.maximum(m_prev, jnp.max(s, axis=1, keepdims=True))
        alpha = jnp.exp2(m_prev - m_new)
        p = jnp.exp2(s - _lane_tile(m_new, s.shape[1] // LANES))
        pv = jnp.dot(p.astype(BF16), vaug, preferred_element_type=F32)
        acc_scr[br, hk] = _lane_tile(alpha, 2) * acc_scr[br, hk] + pv
        m_scr[br, hk] = m_new

    def finish(br, hk):
        acc = acc_scr[br, hk]
        return acc[:, 0:HEAD_DIM] / acc[:, HEAD_DIM:]

    for hk in range(nkv):
        for g in range(NSA_GROUP):
            h = hk * NSA_GROUP + g
            q_h = q_ref[0, :, h * HEAD_DIM:(h + 1) * HEAD_DIM]
            qg_scr[hk, g * LANES:(g + 1) * LANES, 0:HEAD_DIM] = q_h
            qw_scr[hk, g * LANES:(g + 1) * LANES, 0:HEAD_DIM] = q_h

    for hk in range(nkv):
        qg = qg_scr[hk, :, 0:HEAD_DIM]

        kc = kcvc_ref[0, 0, hk]
        vc = kcvc_ref[0, 1, hk]
        s = lax.dot_general(qg, kc, NT_DIMS, preferred_element_type=F32) + bc_ref[0, hk]
        rc = lax.broadcasted_iota(jnp.int32, (LANES, ncp), 0)
        cc = lax.broadcasted_iota(jnp.int32, (LANES, ncp), 1)
        valid_c = _tile4(jnp.where(q0 + rc - (CMP_STRIDE * cc + CMP_BLOCK - 1) >= 0, 1.0, 0.0)) > 0.5
        s = jnp.where(valid_c, s, NEG)
        p = jnp.where(valid_c, jnp.exp2(s - jnp.max(s, axis=1, keepdims=True)), 0.0)
        l = jnp.sum(p, axis=1, keepdims=True)
        p = p / jnp.where(l > 0.0, l, 1.0)
        o_scr[hk] = gate_col(hk, 0) * jnp.dot(p.astype(BF16), vc, preferred_element_type=F32)

        psum = p[0:LANES]
        for g in range(1, NSA_GROUP):
            psum = psum + p[g * LANES:(g + 1) * LANES]
        p_hi = psum.astype(BF16)
        p_lo = (psum - p_hi.astype(F32)).astype(BF16)
        selmt = selmt_ref[...]
        imp = (lax.dot_general(selmt, p_hi, NT_DIMS, preferred_element_type=F32)
               + lax.dot_general(selmt, p_lo, NT_DIMS, preferred_element_type=F32))
        imp = imp[0:n_slc]
        blk = lax.broadcasted_iota(jnp.int32, (n_slc, LANES), 0)
        cur = (q0 + lax.broadcasted_iota(jnp.int32, (n_slc, LANES), 1)) // SEL_BLOCK
        forced = (blk == 0) | (blk == cur) | (blk == cur - 1)
        imp = jnp.where(forced, FORCE, imp)
        imp = jnp.where(blk <= cur, imp, -jnp.inf)
        rank = jnp.zeros((n_slc, LANES), F32)
        for j in range(n_slc):
            row = imp[j:j + 1, :]
            beats = jnp.where(row > imp, 1.0, jnp.where(row == imp, jnp.where(blk > j, 1.0, 0.0), 0.0))
            rank = rank + beats
        sel_t = jnp.where(rank < top_n, jnp.where(imp > -jnp.inf, 1.0, 0.0), 0.0)
        if n_slc < LANES:
            sel_t = jnp.concatenate([sel_t, jnp.zeros((LANES - n_slc, LANES), F32)], axis=0)
        sel_q = lax.dot_general(eye, sel_t.astype(BF16), NT_DIMS, preferred_element_type=F32)
        not_sel = jnp.where(ci < n_slc, 1.0 - sel_q, 0.0).astype(BF16)
        for g in range(NSA_GROUP):
            rows = slice(g * LANES, (g + 1) * LANES)
            qg_scr[hk, rows, HEAD_DIM:] = jnp.where(ci < n_slc, not_sel, caug_scr[hk, rows, :])

    reset(WIN)
    for off, table in ((1, near_scr), (3, None), (5, edge_scr)):
        t0 = qt - off + WIN_PAD // LANES
        p0 = pl.multiple_of(t0 * LANES, LANES)
        for hk in range(nkv):
            k_t = jnp.concatenate([kwp_scr[hk, t0], kwp_scr[hk, t0 + 1]], axis=1)
            s = jnp.dot(qw_scr[hk], k_t, preferred_element_type=F32)
            if table is not None:
                s = s + (table[hk] if table is near_scr else table[...])
            online_update(WIN, hk, s, vwa_scr[hk, pl.ds(p0, KEY_CHUNK), :])

    reset(SEL)
    n_chunks = (qt + 2) // 2
    first_tile = SEL_PAD_TILES - (qt + 1) % 2

    def sel_scores(hk, c):
        t0 = first_tile + 2 * c
        k_t = jnp.concatenate([ksa_scr[hk, t0], ksa_scr[hk, t0 + 1]], axis=1)
        return jnp.dot(qg_scr[hk], k_t, preferred_element_type=F32)

    def sel_chunk(c, last):
        p0 = pl.multiple_of((first_tile + 2 * c) * LANES, LANES)
        for hk in range(nkv):
            s = s_scr[hk]
            if last:
                s = s + near_scr[hk]
            else:
                s_scr[hk] = sel_scores(hk, c + 1)
            online_update(SEL, hk, s, vsa_scr[hk, pl.ds(p0, KEY_CHUNK), :])

    def sel_pair(i, carry):
        sel_chunk(2 * i, False)
        sel_chunk(2 * i + 1, False)
        return carry

    for hk in range(nkv):
        s_scr[hk] = sel_scores(hk, 0)
    n_far = n_chunks - 1
    lax.fori_loop(0, n_far // 2, sel_pair, 0)
    pl.when(n_far % 2 == 1)(lambda: sel_chunk(n_far - 1, False))
    sel_chunk(n_chunks - 1, True)
    for hk in range(nkv):
        o = (o_scr[hk] + gate_col(hk, 1) * finish(SEL, hk)) + gate_col(hk, 2) * finish(WIN, hk)
        for g in range(NSA_GROUP):
            h = hk * NSA_GROUP + g
            o_ref[0, :, h * HEAD_DIM:(h + 1) * HEAD_DIM] = o[g * LANES:(g + 1) * LANES]


def _nsa_call(o16v, o32v, kcvc, bias_c, tz, selmt, weights_f32, batch, seq, ncp):
    nqt = seq // LANES
    wspecs = []
    for w in weights_f32:
        blk, span = _cast_block(w, batch * nqt)
        wspecs.append(pl.BlockSpec(blk, functools.partial(lambda b, t, span: ((b * nqt + t) // span, 0), span=span)))
    kvw = NSA_KV_WIDTH
    nkv = N_NSA_KV_HEADS
    base = (3 * FOX_WIDTH + NSA_WIDTH) // kvw
    return pl.pallas_call(
        functools.partial(_nsa_kernel, seq=seq, ncp=ncp),
        grid=(batch, nqt),
        in_specs=[pl.BlockSpec((1, LANES, NSA_WIDTH), lambda b, t: (b, t, 3 * FOX_WIDTH // NSA_WIDTH)),
                  pl.BlockSpec((1, seq, kvw), lambda b, t: (b, 0, base)),
                  pl.BlockSpec((1, seq, kvw), lambda b, t: (b, 0, base + 1)),
                  pl.BlockSpec((1, seq, kvw), lambda b, t: (b, 0, base + 2)),
                  pl.BlockSpec((1, seq, kvw), lambda b, t: (b, 0, base + 3)),
                  pl.BlockSpec((1, LANES, LANES), lambda b, t: (b, t, COL_NG // LANES)),
                  pl.BlockSpec((1, 2, N_NSA_KV_HEADS, ncp, HEAD_DIM), lambda b, t: (b, 0, 0, 0, 0)),
                  pl.BlockSpec((1, N_NSA_KV_HEADS, GROUP_ROWS, ncp), lambda b, t: (t, 0, 0, 0)),
                  pl.BlockSpec((3, N_NSA_KV_HEADS, GROUP_ROWS, LANES), lambda b, t: (0, 0, 0, 0)),
                  pl.BlockSpec((LANES, ncp), lambda b, t: (0, 0))] + wspecs,
        out_specs=[pl.BlockSpec((1, LANES, NSA_WIDTH), lambda b, t: (b, t, 0))] + wspecs,
        out_shape=[jax.ShapeDtypeStruct((batch, seq, NSA_WIDTH), F32)]
        + [jax.ShapeDtypeStruct(w.shape, BF16) for w in weights_f32],
        scratch_shapes=[pltpu.VMEM((nkv, GROUP_ROWS, 2 * HEAD_DIM), BF16),
                        pltpu.VMEM((nkv, GROUP_ROWS, 2 * HEAD_DIM), BF16),
                        pltpu.VMEM((nkv, GROUP_ROWS, LANES), BF16),
                        pltpu.VMEM((nkv, GROUP_ROWS, KEY_CHUNK), F32),
                        pltpu.VMEM((GROUP_ROWS, KEY_CHUNK), F32),
                        pltpu.VMEM((2, nkv, GROUP_ROWS, LANES), F32),
                        pltpu.VMEM((2, nkv, GROUP_ROWS, 2 * HEAD_DIM), F32),
                        pltpu.VMEM((nkv, GROUP_ROWS, HEAD_DIM), F32),
                        pltpu.VMEM((nkv, GROUP_ROWS, KEY_CHUNK), F32),
                        pltpu.VMEM((nkv, SEL_PAD_TILES + seq // LANES, 2 * HEAD_DIM, LANES), BF16),
                        pltpu.VMEM((nkv, SEL_PAD_TILES * LANES + seq, 2 * HEAD_DIM), BF16),
                        pltpu.VMEM((nkv, (seq + WIN_PAD) // LANES, 2 * HEAD_DIM, LANES), BF16),
                        pltpu.VMEM((nkv, seq + WIN_PAD, 2 * HEAD_DIM), BF16)],
        compiler_params=_params(2),
        name="nsa_attention",
    )(o16v, o16v, o16v, o16v, o16v, o32v, kcvc, bias_c, tz, selmt, *weights_f32)


def _out_kernel(of_ref, on_ref, gain_ref, w_ref, x_ref, g_ref, o_ref, y_scr, *, tm, n_chains):
    rows_per_chain = tm // n_chains
    gain_f = gain_ref[:, 0:FOX_WIDTH]
    gain_n = gain_ref[:, FOX_WIDTH:MIX_WIDTH]
    for c in range(n_chains):
        rows = slice(c * rows_per_chain, (c + 1) * rows_per_chain)
        for r0 in range(c * rows_per_chain, (c + 1) * rows_per_chain, NORM_ROWS):
            r = slice(r0, r0 + NORM_ROWS)
            y_scr[r, 0:FOX_WIDTH] = (_rms(of_ref[r, :]) * gain_f).astype(BF16)
            y_scr[r, FOX_WIDTH:MIX_WIDTH] = (_rms(on_ref[r, :]) * gain_n).astype(BF16)
        acc = jnp.dot(y_scr[rows, :], w_ref[...], preferred_element_type=F32)
        o_ref[rows, :] = x_ref[rows, :] + g_ref[0] * acc


def _out_call(o_fox, o_nsa, gain, w_out, x2, g1, seq):
    t, d = x2.shape
    tm = min(512, seq)
    rows_per_batch = seq // tm
    return pl.pallas_call(
        functools.partial(_out_kernel, tm=tm, n_chains=2),
        grid=(t // tm,),
        in_specs=[pl.BlockSpec((tm, FOX_WIDTH), lambda i: (i, 0)),
                  pl.BlockSpec((tm, NSA_WIDTH), lambda i: (i, 0)),
                  pl.BlockSpec((1, MIX_WIDTH), lambda i: (0, 0)),
                  pl.BlockSpec((MIX_WIDTH, d), lambda i: (0, 0)),
                  pl.BlockSpec((tm, d), lambda i: (i, 0)),
                  pl.BlockSpec((1, 1, d), lambda i: (i // rows_per_batch, 0, 0))],
        out_specs=pl.BlockSpec((tm, d), lambda i: (i, 0)),
        out_shape=jax.ShapeDtypeStruct((t, d), F32),
        scratch_shapes=[pltpu.VMEM((tm, MIX_WIDTH), BF16)],
        compiler_params=_params(1),
        name="out_proj",
    )(o_fox, o_nsa, gain, w_out, x2, g1)


def _ffn_kernel(x_ref, gain_ref, sc_ref, sh_ref, g_ref, wg_ref, wu_ref, wd_ref, o_ref, h_scr, *, n_f):
    f = pl.program_id(1)
    tm = h_scr.shape[0]

    def hidden_step(first_row, n_rows, first, last):
        rows = slice(first_row, first_row + n_rows)
        h = h_scr[rows, :]
        a = jnp.dot(h, wg_ref[...], preferred_element_type=F32)
        u = jnp.dot(h, wu_ref[...], preferred_element_type=F32)
        t = (a * _sigmoid(a)) * u
        part = jnp.dot(t.astype(BF16), wd_ref[...], preferred_element_type=F32)
        acc = part if first else o_ref[rows, :] + part
        o_ref[rows, :] = x_ref[rows, :] + g_ref[0] * acc if last else acc

    def edge_step(first, last):
        n_rows = tm // ROW_CHAINS
        for c in range(ROW_CHAINS):
            if first:
                _modulated_norm(x_ref, gain_ref, sc_ref, sh_ref, h_scr, c * n_rows, n_rows)
            hidden_step(c * n_rows, n_rows, first, last)

    if n_f == 1:
        edge_step(True, True)
    else:
        pl.when(f == 0)(functools.partial(edge_step, True, False))
        pl.when(jnp.logical_and(f > 0, f < n_f - 1))(functools.partial(hidden_step, 0, tm, False, False))
        pl.when(f == n_f - 1)(functools.partial(edge_step, False, True))


def _ffn_call(x1, gain2, sc, sh, g2, wg, wu, wd, seq):
    t, d = x1.shape
    dff = wg.shape[1]
    tm = min(1024, seq)
    tf = 512 if dff % 512 == 0 else dff
    rows_per_batch = seq // tm
    return pl.pallas_call(
        functools.partial(_ffn_kernel, n_f=dff // tf),
        grid=(t // tm, dff // tf),
        in_specs=[pl.BlockSpec((tm, d), lambda i, f: (i, 0)),
                  pl.BlockSpec((1, d), lambda i, f: (0, 0)),
                  pl.BlockSpec((1, 1, d), lambda i, f: (i // rows_per_batch, 0, 0)),
                  pl.BlockSpec((1, 1, d), lambda i, f: (i // rows_per_batch, 0, 0)),
                  pl.BlockSpec((1, 1, d), lambda i, f: (i // rows_per_batch, 0, 0)),
                  pl.BlockSpec((d, tf), lambda i, f: (0, f)),
                  pl.BlockSpec((d, tf), lambda i, f: (0, f)),
                  pl.BlockSpec((tf, d), lambda i, f: (f, 0))],
        out_specs=pl.BlockSpec((tm, d), lambda i, f: (i, 0)),
        out_shape=jax.ShapeDtypeStruct((t, d), F32),
        scratch_shapes=[pltpu.VMEM((tm, d), BF16)],
        compiler_params=_params(2),
        name="swiglu_ffn",
    )(x1, gain2, sc, sh, g2, wg, wu, wd)


def _selection_matrix_t(ncp, n_slc):
    r, q = SEL_BLOCK // CMP_STRIDE, CMP_BLOCK // CMP_STRIDE
    m = np.zeros((LANES, ncp), np.float32)
    for j in range(n_slc):
        for a in range(r):
            for b in range(q):
                c = r * j + a - b
                if 0 <= c < ncp:
                    m[j, c] += 1.0
    return m


def _w_in_block_sources():
    o = 0
    start = {}
    for name, width in (("fq", FOX_WIDTH), ("fk", FOX_WIDTH), ("fv", FOX_WIDTH), ("ff", N_FOX_HEADS),
                        ("nq", NSA_WIDTH), ("nk", N_BRANCH * NSA_KV_WIDTH), ("nv", N_BRANCH * NSA_KV_WIDTH),
                        ("ng", N_BRANCH * N_NSA_HEADS)):
        start[name] = o
        o += width
    kvw = NSA_KV_WIDTH
    groups = [(start["fq"], 3 * FOX_WIDTH), (start["nq"], NSA_WIDTH),
              (start["nk"] + kvw, 2 * kvw), (start["nv"] + kvw, 2 * kvw),
              (start["nk"], kvw), (start["nv"], kvw)]
    blocks = [s + LANES * b for s, width in groups for b in range(width // LANES)]
    return blocks, (start["ff"], N_FOX_HEADS), (start["ng"], N_BRANCH * N_NSA_HEADS)


def _repack_kernel(src_ref, wt_hbm, o16_ref, o32_ref, buf, sem, *, n16, n_whole, ff, ng):
    k = pl.program_id(0)
    n_slots = buf.shape[0]
    ahead = n_slots - 1
    slot = k % n_slots

    def whole_copy(kk, s):
        r0 = pl.multiple_of(src_ref[kk], 8)
        return pltpu.make_async_copy(wt_hbm.at[pl.ds(r0, LANES), :], buf.at[s], sem.at[s])

    def narrow_copies(s):
        return [pltpu.make_async_copy(wt_hbm.at[pl.ds(ff[0], ff[1]), :], buf.at[s, pl.ds(0, ff[1]), :], sem.at[s]),
                pltpu.make_async_copy(wt_hbm.at[pl.ds(ng[0], ng[1]), :], buf.at[s, pl.ds(ff[1], ng[1]), :],
                                      sem.at[s])]

    def start_block(kk):
        @pl.when(kk < n_whole)
        def _():
            whole_copy(kk, kk % n_slots).start()

        @pl.when(kk == n_whole)
        def _():
            for cp in narrow_copies(kk % n_slots):
                cp.start()

    @pl.when(k == 0)
    def _():
        for kk in range(ahead):
            start_block(kk)

    start_block(k + ahead)

    @pl.when(k < n_whole)
    def _():
        whole_copy(k, slot).wait()

    @pl.when(k == n_whole)
    def _():
        for cp in narrow_copies(slot):
            cp.wait()

    xt = buf[slot].T
    lane = lax.broadcasted_iota(jnp.int32, xt.shape, 1)
    xt = jnp.where(jnp.logical_or(k < n_whole, lane < ff[1] + ng[1]), xt, 0.0).astype(BF16)

    @pl.when(k < n16)
    def _():
        o16_ref[...] = xt

    @pl.when(k >= n16)
    def _():
        o32_ref[...] = xt


def _repack_w_in(wt):
    n, d = wt.shape
    blocks, ff, ng = _w_in_block_sources()
    n16 = W16 // LANES
    n_whole = len(blocks)
    assert n_whole + 1 == (W16 + W32) // LANES
    return pl.pallas_call(
        functools.partial(_repack_kernel, n16=n16, n_whole=n_whole, ff=ff, ng=ng),
        grid_spec=pltpu.PrefetchScalarGridSpec(
            num_scalar_prefetch=1,
            grid=(n_whole + 1,),
            in_specs=[pl.BlockSpec(memory_space=pl.ANY)],
            out_specs=[pl.BlockSpec((d, LANES), lambda k, src: (0, jnp.minimum(k, n16 - 1))),
                       pl.BlockSpec((d, LANES), lambda k, src: (0, jnp.maximum(k - n16, 0)))],
            scratch_shapes=[pltpu.VMEM((REPACK_SLOTS, LANES, d), F32), pltpu.SemaphoreType.DMA((REPACK_SLOTS,))]),
        out_shape=[jax.ShapeDtypeStruct((d, W16), BF16), jax.ShapeDtypeStruct((d, W32), BF16)],
        compiler_params=_params(1),
        name="w_in_repack",
    )(jnp.asarray(blocks, jnp.int32), wt)


def kernel(x, c, ada_w, ada_b, norm1_gain, norm2_gain, w_in, fox_f_bias, fox_q_gain, fox_k_gain, nsa_q_gain,
           nsa_k_gain, nsa_cmp_pos, nsa_cmp_w1, nsa_cmp_w2, rel_bias, mix_out_gain, w_out, ffn_w_gate, ffn_w_up,
           ffn_w_down):
    batch, seq, d = x.shape
    assert seq % KEY_CHUNK == 0 and seq >= WINDOW and d % LANES == 0 and seq // SEL_BLOCK <= AUG_FAR
    depth = ada_w.shape[0]
    nqt = seq // LANES
    ncp = -(-(seq // CMP_STRIDE) // LANES) * LANES

    selmt = jnp.asarray(_selection_matrix_t(ncp, seq // SEL_BLOCK), BF16)
    rb_flat = rel_bias.reshape(-1)
    bias_c = _bias_call(rb_flat, nqt, ncp, CMP_STRIDE, CMP_BLOCK - 1, "t5_bias_compressed", rolled=True)
    tz = _bias_call(rb_flat, 3, LANES, 1, 0, "t5_bias_toeplitz")

    ones_h = jnp.ones((HEAD_DIM,), F32)
    c_pad = jnp.pad(c, ((0, 8 - batch % 8 if batch % 8 else 0), (0, 0)))
    x2 = x.reshape(batch * seq, d)
    for layer in range(depth):
        mod = _ada_call(c_pad, ada_w[layer], ada_b[layer][None, :])[:batch]
        sh1, sc1, g1, sh2, sc2, g2 = [mod[:, i * d:(i + 1) * d][:, None, :] for i in range(N_MOD)]

        kg = nsa_k_gain[layer]
        col_gain = jnp.concatenate([
            jnp.tile(fox_q_gain[layer] * QSCALE, N_FOX_HEADS), jnp.tile(fox_k_gain[layer], N_FOX_HEADS),
            jnp.tile(ones_h, N_FOX_HEADS), jnp.tile(nsa_q_gain[layer] * QSCALE, N_NSA_HEADS),
            jnp.tile(kg[1], N_NSA_KV_HEADS), jnp.tile(kg[2], N_NSA_KV_HEADS),
            jnp.tile(ones_h, 2 * N_NSA_KV_HEADS)])[None, :]
        col_flag = jnp.concatenate([
            jnp.ones((2 * FOX_WIDTH,), F32), jnp.zeros((FOX_WIDTH,), F32), jnp.ones((NSA_WIDTH,), F32),
            jnp.ones((2 * NSA_KV_WIDTH,), F32), jnp.zeros((2 * NSA_KV_WIDTH,), F32)])[None, :]
        w16, w32 = _repack_w_in(jnp.swapaxes(w_in, 1, 2)[layer])
        o16, o32 = _proj_call(x2, sc1, sh1, norm1_gain[layer][None, :], w16, w32,
                              col_gain, col_flag, seq)
        o16v = o16.reshape(batch, seq, W16)
        o32v = o32.reshape(batch, seq, W32)

        fb_pad = jnp.pad(fox_f_bias[layer], (0, LANES - N_FOX_HEADS))[None, :]
        cs = _cum_call(o32, fb_pad, batch, seq, N_FOX_HEADS // FOX_HEADS_PER_STEP)
        o_fox = _fox_call(o16v, cs, batch, seq)

        w1 = nsa_cmp_w1[layer].reshape(2, CMP_BLOCK, HEAD_DIM, HEAD_DIM)
        kcvc = _cmp_call(o32, nsa_cmp_pos[layer], w1, nsa_cmp_w2[layer], kg[0][None, :],
                         batch, seq, ncp)
        o_nsa, wg16, wu16, wo16, wd16 = _nsa_call(
            o16v, o32v, kcvc, bias_c, tz, selmt,
            (ffn_w_gate[layer], ffn_w_up[layer], w_out[layer], ffn_w_down[layer]), batch, seq, ncp)

        x1 = _out_call(o_fox.reshape(batch * seq, FOX_WIDTH), o_nsa.reshape(batch * seq, NSA_WIDTH),
                       mix_out_gain[layer][None, :], wo16, x2, g1, seq)
        x2 = _ffn_call(x1, norm2_gain[layer][None, :], sc2, sh2, g2, wg16, wu16, wd16, seq)
    return x2.reshape(batch, seq, d)
```

```python
import functools
import math

import numpy as np
import jax
import jax.numpy as jnp
from jax import lax
from jax.experimental import pallas as pl
from jax.experimental.pallas import tpu as pltpu

HEAD_DIM = 128
N_FOX_HEADS = 8
N_NSA_HEADS = 8
N_NSA_KV_HEADS = 2
NSA_GROUP = N_NSA_HEADS // N_NSA_KV_HEADS
FOX_WIDTH = N_FOX_HEADS * HEAD_DIM
NSA_WIDTH = N_NSA_HEADS * HEAD_DIM
NSA_KV_WIDTH = N_NSA_KV_HEADS * HEAD_DIM
MIX_WIDTH = FOX_WIDTH + NSA_WIDTH
N_BRANCH = 3
CMP_BLOCK = 32
CMP_STRIDE = 16
SEL_BLOCK = 64
N_SEL = 8
WINDOW = 512
N_BUCKETS = 32
MAX_DISTANCE = 128
N_MOD = 6
SCALE = HEAD_DIM ** -0.5
LOG2E = math.log2(math.e)
LOG2E_HI = float(np.float32(LOG2E))
LOG2E_LO = LOG2E - LOG2E_HI
QSCALE = SCALE * LOG2E
EPS = 1e-6
NEG = -1e30
FORCE = 1e6

LANES = 128
GROUP_ROWS = NSA_GROUP * LANES
VMEM_LIMIT = 56 * 1024 * 1024
MXU_COLS = 256
KEY_CHUNK = MXU_COLS
WIN_PAD = WINDOW + LANES
MASK_BIG = 2.0 ** 100
SEL, WIN = 0, 1
REPACK_SLOTS = 4
AUG_FAR, AUG_PAD = 120, 127
FOX_HEADS_PER_STEP = 4
SEL_TRIP_CHUNKS = 4
SEL_PAD_TILES = 1

W16 = 3 * FOX_WIDTH + NSA_WIDTH + 4 * NSA_KV_WIDTH
W32 = 5 * LANES
COL_FF = 2 * NSA_KV_WIDTH
COL_NG = COL_FF + N_FOX_HEADS

F32 = jnp.float32
BF16 = jnp.bfloat16
NT_DIMS = (((1,), (1,)), ((), ()))


def _params(n_axes):
    return pltpu.CompilerParams(dimension_semantics=("arbitrary",) * n_axes,
                                vmem_limit_bytes=VMEM_LIMIT)


def _sigmoid(x):
    return 1.0 / (1.0 + jnp.exp(-x))


def _lane_tile(a, n):
    return jnp.concatenate([a] * n, axis=1)


BF16_SUBLANES = 16


def _cast_block(w, n_steps):
    rows, cols = w.shape
    assert rows % n_steps == 0, (w.shape, n_steps)
    per_step = rows // n_steps
    span = BF16_SUBLANES // math.gcd(BF16_SUBLANES, per_step)
    assert n_steps % span == 0, (w.shape, n_steps)
    return (per_step * span, cols), span


def _rms(x):
    return x * lax.rsqrt(jnp.mean(x * x, axis=-1, keepdims=True) + EPS)


NORM_ROWS = 16


def _modulated_norm(x_ref, gain_ref, sc_ref, sh_ref, h_ref, first_row, n_rows):
    gm = gain_ref[...] * (1.0 + sc_ref[0])
    sh = sh_ref[0]
    for r0 in range(first_row, first_row + n_rows, NORM_ROWS):
        rows = slice(r0, r0 + NORM_ROWS)
        h_ref[rows, :] = (_rms(x_ref[rows, :]) * gm + sh).astype(BF16)


ROW_CHAINS = 2


def _ada_kernel(c_ref, w_ref, b_ref, o_ref):
    c = c_ref[...]
    s = (c * _sigmoid(c)).astype(BF16)
    o_ref[...] = jnp.dot(s, w_ref[...].astype(BF16), preferred_element_type=F32) + b_ref[...]


def _ada_call(c_pad, w, b):
    rows, d = c_pad.shape
    n = w.shape[1]
    tn = next(t for t in (1024, 768, 512, 384, 256, 128) if n % t == 0)
    return pl.pallas_call(
        _ada_kernel,
        grid=(n // tn,),
        in_specs=[pl.BlockSpec((rows, d), lambda j: (0, 0)),
                  pl.BlockSpec((d, tn), lambda j: (0, j)),
                  pl.BlockSpec((1, tn), lambda j: (0, j))],
        out_specs=pl.BlockSpec((rows, tn), lambda j: (0, j)),
        out_shape=jax.ShapeDtypeStruct((rows, n), F32),
        compiler_params=_params(1),
        name="adaln",
    )(c_pad, w, b)


def _proj_kernel(x_ref, sc_ref, sh_ref, g_ref, w16_ref, w32_ref, gain_ref, flag_ref, o16_ref, o32_ref, h_scr, *, tn):
    j = pl.program_id(1)
    tm = h_scr.shape[0]

    def column_step(first_row, n_rows, with_side_outputs):
        rows = slice(first_row, first_row + n_rows)
        h = h_scr[rows, :]
        for c in range(tn // MXU_COLS):
            acc = jnp.dot(h, w16_ref[:, c * MXU_COLS:(c + 1) * MXU_COLS], preferred_element_type=F32)
            for g in range(MXU_COLS // LANES):
                cols = slice(c * MXU_COLS + g * LANES, c * MXU_COLS + (g + 1) * LANES)
                a = acc[:, g * LANES:(g + 1) * LANES]
                r = lax.rsqrt(jnp.mean(a * a, axis=-1, keepdims=True) + EPS)
                scale = jnp.where(flag_ref[:, cols] > 0.5, r, 1.0)
                o16_ref[rows, cols] = (a * scale * gain_ref[:, cols]).astype(BF16)
        if with_side_outputs:
            o32_ref[rows, :] = jnp.dot(h, w32_ref[...], preferred_element_type=F32)

    def step(first, with_side_outputs):
        n_rows = tm // ROW_CHAINS
        for c in range(ROW_CHAINS):
            if first:
                _modulated_norm(x_ref, g_ref, sc_ref, sh_ref, h_scr, c * n_rows, n_rows)
            column_step(c * n_rows, n_rows, with_side_outputs)

    last = pl.num_programs(1) - 1
    pl.when(j == 0)(functools.partial(step, True, False))
    pl.when(jnp.logical_and(j > 0, j < last))(functools.partial(step, False, False))
    pl.when(j == last)(functools.partial(step, False, True))


def _proj_call(x2, sc, sh, gain1, w16, w32, col_gain, col_flag, seq):
    t, d = x2.shape
    tm = min(1024, seq)
    tn = 1280 if W16 % 1280 == 0 else 1024
    assert W16 // tn >= 2
    rows_per_batch = seq // tm
    return pl.pallas_call(
        functools.partial(_proj_kernel, tn=tn),
        grid=(t // tm, W16 // tn),
        in_specs=[pl.BlockSpec((tm, d), lambda i, j: (i, 0)),
                  pl.BlockSpec((1, 1, d), lambda i, j: (i // rows_per_batch, 0, 0)),
                  pl.BlockSpec((1, 1, d), lambda i, j: (i // rows_per_batch, 0, 0)),
                  pl.BlockSpec((1, d), lambda i, j: (0, 0)),
                  pl.BlockSpec((d, tn), lambda i, j: (0, j)),
                  pl.BlockSpec((d, W32), lambda i, j: (0, 0)),
                  pl.BlockSpec((1, tn), lambda i, j: (0, j)),
                  pl.BlockSpec((1, tn), lambda i, j: (0, j))],
        out_specs=[pl.BlockSpec((tm, tn), lambda i, j: (i, j)),
                   pl.BlockSpec((tm, W32), lambda i, j: (i, 0))],
        out_shape=[jax.ShapeDtypeStruct((t, W16), BF16),
                   jax.ShapeDtypeStruct((t, W32), F32)],
        scratch_shapes=[pltpu.VMEM((tm, d), BF16)],
        compiler_params=_params(2),
        name="in_proj",
    )(x2, sc, sh, gain1, w16, w32, col_gain, col_flag)


def _split3(c):
    hi = c.astype(BF16).astype(F32)
    r1 = c - hi
    mid = r1.astype(BF16).astype(F32)
    return hi, mid, (r1 - mid).astype(BF16).astype(F32)


def _cum_kernel(ff_ref, fb_ref, o_ref, *, seq, groups):
    ri = lax.broadcasted_iota(jnp.int32, (LANES, LANES), 0)
    ci = lax.broadcasted_iota(jnp.int32, (LANES, LANES), 1)
    tri = jnp.where(ri >= ci, 1.0, 0.0).astype(BF16)
    carry = jnp.zeros((1, LANES), F32)
    for blk in range(seq // LANES):
        rows = slice(blk * LANES, (blk + 1) * LANES)
        x = ff_ref[0, rows, :] + fb_ref[...]
        lf = jnp.minimum(x, 0.0) - jnp.log(1.0 + jnp.exp(-jnp.abs(x)))
        hi = lf.astype(BF16)
        r1 = lf - hi.astype(F32)
        mid = r1.astype(BF16)
        lo = (r1 - mid.astype(F32)).astype(BF16)
        c = (jnp.dot(tri, hi, preferred_element_type=F32)
             + jnp.dot(tri, mid, preferred_element_type=F32)
             + jnp.dot(tri, lo, preferred_element_type=F32)) + carry
        carry = c[LANES - 1:LANES, :]
        terms = _split3(c * LOG2E_HI + c * LOG2E_LO)
        per = N_FOX_HEADS // groups
        for g in range(groups):
            packed = jnp.zeros((LANES, LANES), F32)
            for t, term in enumerate(terms):
                shift = (t * per - g * per) % LANES
                moved = term if shift == 0 else pltpu.roll(term, shift, 1)
                packed = jnp.where(jnp.logical_and(ci >= t * per, ci < (t + 1) * per), moved, packed)
            o_ref[0, g, rows, :] = packed


def _cum_call(o32, fb_pad, batch, seq, groups):
    o32v = o32.reshape(batch, seq, W32)
    return pl.pallas_call(
        functools.partial(_cum_kernel, seq=seq, groups=groups),
        grid=(batch,),
        in_specs=[pl.BlockSpec((1, seq, LANES), lambda b: (b, 0, COL_FF // LANES)),
                  pl.BlockSpec((1, LANES), lambda b: (0, 0))],
        out_specs=pl.BlockSpec((1, groups, seq, LANES), lambda b: (b, 0, 0, 0)),
        out_shape=jax.ShapeDtypeStruct((batch, groups, seq, LANES), F32),
        compiler_params=_params(1),
        name="fox_cumsum",
    )(o32v, fb_pad)


def _fox_kernel(q_ref, k_ref, v_ref, cs_ref, o_ref, m_scr, acc_scr, s_scr, kaug_scr, vaug_scr, *, seq, tq, tk, nh):
    n_diag = tq // tk
    lane_q = lax.broadcasted_iota(jnp.int32, (tq, LANES), 1)
    row8 = lax.broadcasted_iota(jnp.int32, (8, tk), 0)
    ri = lax.broadcasted_iota(jnp.int32, (tq, tk), 0)
    ci = lax.broadcasted_iota(jnp.int32, (tq, tk), 1)
    causal = [ci + d * tk <= ri for d in range(n_diag)]
    for c in range(seq // tk):
        rows = slice(c * tk, (c + 1) * tk)
        terms_t = cs_ref[0, 0, rows, :].T
        for j in range(nh):
            hi, mid, lo = [terms_t[t * nh + j:t * nh + j + 1] for t in range(3)]
            tail8 = jnp.where(row8 < 3, 1.0, jnp.where(row8 == 3, -hi, jnp.where(
                row8 == 4, -mid, jnp.where(row8 == 5, -lo, 0.0))))
            kaug_scr[j, c, 0:HEAD_DIM, :] = k_ref[0, rows, j * HEAD_DIM:(j + 1) * HEAD_DIM].T
            kaug_scr[j, c, HEAD_DIM:, :] = jnp.concatenate(
                [tail8, jnp.zeros((HEAD_DIM - 8, tk), F32)], axis=0).astype(BF16)
    for j in range(nh):
        vaug_scr[j, :, 0:HEAD_DIM] = v_ref[0, :, j * HEAD_DIM:(j + 1) * HEAD_DIM]
        vaug_scr[j, :, HEAD_DIM:] = jnp.ones((seq, HEAD_DIM), BF16)
    acc_scr[...] = jnp.zeros(acc_scr.shape, F32)

    def q_body(qi, carry):
        q0 = pl.multiple_of(qi * tq, tq)
        terms = cs_ref[0, 0, pl.ds(q0, tq), :]
        qs = []
        for j in range(nh):
            hi, mid, lo = [terms if t * nh + j == t else pltpu.roll(terms, (t - (t * nh + j)) % LANES, 1)
                           for t in range(3)]
            tail = jnp.where(lane_q == 0, hi, jnp.where(lane_q == 1, mid, jnp.where(lane_q == 2, lo,
                             jnp.where(lane_q < 6, 1.0, 0.0))))
            qs.append(jnp.concatenate([q_ref[0, pl.ds(q0, tq), j * HEAD_DIM:(j + 1) * HEAD_DIM],
                                       tail.astype(BF16)], axis=1))
            m_scr[j] = jnp.full((tq, LANES), NEG, F32)

        all_rows = slice(0, tq)

        def scores(j, ki, rows):
            return jnp.dot(qs[j][rows], kaug_scr[j, ki], preferred_element_type=F32)

        def tile(ki, mask, rows, next_rows):
            k0 = pl.multiple_of(ki * tk, tk)
            for j in range(nh):
                s = s_scr[j, rows, :]
                if mask is not None:
                    s = jnp.where(mask[rows], s, NEG)
                if next_rows is not None:
                    s_scr[j, next_rows, :] = scores(j, ki + 1, next_rows)
                m_prev = m_scr[j, rows, :]
                m_new = jnp.maximum(m_prev, jnp.max(s, axis=1, keepdims=True))
                alpha = jnp.exp2(m_prev - m_new)
                p = jnp.exp2(s - _lane_tile(m_new, tk // LANES))
                pv = jnp.dot(p.astype(BF16), vaug_scr[j, pl.ds(k0, tk), :], preferred_element_type=F32)
                acc_scr[j, rows, :] = _lane_tile(alpha, 2) * acc_scr[j, rows, :] + pv
                m_scr[j, rows, :] = m_new

        def k_body(ki, c2):
            tile(ki, None, all_rows, all_rows)
            return c2

        for j in range(nh):
            s_scr[j] = scores(j, 0, all_rows)
        n_off = qi * n_diag
        lax.fori_loop(0, n_off, k_body, 0)
        for d in range(n_diag):
            nxt = slice((d + 1) * tk, tq) if d + 1 < n_diag else None
            tile(n_off + d, causal[d], slice(d * tk, tq), nxt)
        for j in range(nh):
            acc = acc_scr[j]
            o_ref[0, pl.ds(q0, tq), j * HEAD_DIM:(j + 1) * HEAD_DIM] = acc[:, 0:HEAD_DIM] / acc[:, HEAD_DIM:]
        return carry

    lax.fori_loop(0, seq // tq, q_body, 0)


def _fox_call(o16v, cs, batch, seq):
    nh = FOX_HEADS_PER_STEP
    tq, tk = min(512, seq), min(512, seq)
    groups = N_FOX_HEADS // nh
    w = nh * HEAD_DIM
    return pl.pallas_call(
        functools.partial(_fox_kernel, seq=seq, tq=tq, tk=tk, nh=nh),
        grid=(batch, groups),
        in_specs=[pl.BlockSpec((1, seq, w), lambda b, h: (b, 0, h)),
                  pl.BlockSpec((1, seq, w), lambda b, h: (b, 0, groups + h)),
                  pl.BlockSpec((1, seq, w), lambda b, h: (b, 0, 2 * groups + h)),
                  pl.BlockSpec((1, 1, seq, LANES), lambda b, h: (b, h, 0, 0))],
        out_specs=pl.BlockSpec((1, seq, w), lambda b, h: (b, 0, h)),
        out_shape=jax.ShapeDtypeStruct((batch, seq, FOX_WIDTH), F32),
        scratch_shapes=[pltpu.VMEM((nh, tq, LANES), F32),
                        pltpu.VMEM((nh, tq, 2 * HEAD_DIM), F32),
                        pltpu.VMEM((nh, tq, tk), F32),
                        pltpu.VMEM((nh, seq // tk, 2 * HEAD_DIM, tk), BF16),
                        pltpu.VMEM((nh, seq, 2 * HEAD_DIM), BF16)],
        compiler_params=_params(2),
        name="fox_attention",
    )(o16v, o16v, o16v, cs)


def _cmp_kernel(x_ref, pos_ref, w1_ref, w2_ref, gain_ref, o_ref, xs_scr, *, seq, ncp):
    rows = xs_scr.shape[1]
    for kv in range(2):
        for h in range(N_NSA_KV_HEADS):
            n = kv * N_NSA_KV_HEADS + h
            xs_scr[n, 0:seq, :] = x_ref[0, :, n * HEAD_DIM:(n + 1) * HEAD_DIM]
            xs_scr[n, seq:rows, :] = jnp.zeros((rows - seq, LANES), F32)
            acc = jnp.zeros((ncp, HEAD_DIM), F32)
            for l in range(CMP_BLOCK):
                xl = xs_scr[n, pl.ds(l, ncp, stride=CMP_STRIDE), :] + pos_ref[kv, l:l + 1, :]
                acc = acc + jnp.dot(xl.astype(BF16), w1_ref[kv, l].astype(BF16), preferred_element_type=F32)
            hmid = acc * _sigmoid(acc)
            y = jnp.dot(hmid.astype(BF16), w2_ref[kv].astype(BF16), preferred_element_type=F32)
            if kv == 0:
                y = _rms(y) * gain_ref[...]
            o_ref[0, kv, h] = y.astype(BF16)


def _cmp_call(o32, pos, w1, w2, gain, batch, seq, ncp):
    o32v = o32.reshape(batch, seq, W32)
    hkv = N_NSA_KV_HEADS
    width = 2 * NSA_KV_WIDTH
    return pl.pallas_call(
        functools.partial(_cmp_kernel, seq=seq, ncp=ncp),
        grid=(batch,),
        in_specs=[pl.BlockSpec((1, seq, width), lambda b: (b, 0, 0)),
                  pl.BlockSpec((2, CMP_BLOCK, HEAD_DIM), lambda b: (0, 0, 0)),
                  pl.BlockSpec((2, CMP_BLOCK, HEAD_DIM, HEAD_DIM), lambda b: (0, 0, 0, 0)),
                  pl.BlockSpec((2, HEAD_DIM, HEAD_DIM), lambda b: (0, 0, 0)),
                  pl.BlockSpec((1, HEAD_DIM), lambda b: (0, 0))],
        out_specs=pl.BlockSpec((1, 2, hkv, ncp, HEAD_DIM), lambda b: (b, 0, 0, 0, 0)),
        out_shape=jax.ShapeDtypeStruct((batch, 2, hkv, ncp, HEAD_DIM), BF16),
        scratch_shapes=[pltpu.VMEM((2 * hkv, CMP_STRIDE * ncp + CMP_BLOCK, LANES), F32)],
        compiler_params=_params(1),
        name="nsa_compress",
    )(o32v, pos, w1, w2, gain)


def _bias_kernel(rb_ref, o_ref, *, width, key_stride, key_offset, first_tile, rolled_tiles):
    v = pl.program_id(0) + first_tile
    i = lax.broadcasted_iota(jnp.int32, (LANES, width), 0)
    j = lax.broadcasted_iota(jnp.int32, (LANES, width), 1)
    d = v * LANES + i - (key_stride * j + key_offset)
    n = jnp.maximum(d, 0)
    max_exact = N_BUCKETS // 2
    nf = jnp.maximum(n, 1).astype(F32)
    large = max_exact + jnp.trunc(jnp.log(nf / max_exact) / math.log(MAX_DISTANCE / max_exact)
                                  * (N_BUCKETS - max_exact))
    large = jnp.minimum(large, float(N_BUCKETS - 1))
    bkt = jnp.where(n < max_exact, n.astype(F32), large)
    vals = [jnp.zeros((LANES, width), F32) for _ in range(N_NSA_HEADS)]
    for bk in range(N_BUCKETS):
        hit = bkt == float(bk)
        for h in range(N_NSA_HEADS):
            vals[h] = jnp.where(hit, rb_ref[bk * N_NSA_HEADS + h] * LOG2E, vals[h])
    for h in range(N_NSA_HEADS):
        g = h % NSA_GROUP
        rows = slice(g * LANES, (g + 1) * LANES)
        if rolled_tiles:
            per_tile = LANES // key_stride
            for t in range(rolled_tiles):
                shift = (width - per_tile * (rolled_tiles - 1 - t)) % width
                o_ref[t, h // NSA_GROUP, rows, :] = vals[h] if shift == 0 else pltpu.roll(vals[h], shift, 1)
        else:
            o_ref[0, h // NSA_GROUP, rows, :] = vals[h]


def _bias_call(rb_flat, n_tiles, width, key_stride, key_offset, name, rolled=False):
    kern = functools.partial(_bias_kernel, width=width, key_stride=key_stride, key_offset=key_offset,
                             first_tile=n_tiles - 1 if rolled else 0, rolled_tiles=n_tiles if rolled else 0)
    block = (n_tiles if rolled else 1, N_NSA_KV_HEADS, GROUP_ROWS, width)
    return pl.pallas_call(
        kern,
        grid=(1 if rolled else n_tiles,),
        in_specs=[pl.BlockSpec(memory_space=pltpu.SMEM)],
        out_specs=pl.BlockSpec(block, lambda v: (v, 0, 0, 0)),
        out_shape=jax.ShapeDtypeStruct((n_tiles, N_NSA_KV_HEADS, GROUP_ROWS, width), F32),
        compiler_params=_params(1),
        name=name,
    )(rb_flat)


def _tile4(a):
    return jnp.concatenate([a] * NSA_GROUP, axis=0)


def _nsa_kernel(q_ref, ks_ref, kw_ref, vs_ref, vw_ref, g_ref, kcvc_ref, bc_ref, tz_ref, selmt_ref,
                wa_ref, wb_ref, wc_ref, wd_ref, o_ref, wa_out, wb_out, wc_out, wd_out,
                qg_scr, qw_scr, caug_scr, near_scr, edge_scr, m_scr, acc_scr, o_scr, s_scr,
                ksa_scr, vsa_scr, kwp_scr, vwa_scr, *, seq, ncp):
    qt = pl.program_id(1)
    q0 = qt * LANES
    n_slc = seq // SEL_BLOCK
    top_n = min(N_SEL, n_slc)
    nkv = N_NSA_KV_HEADS
    for src, dst in ((wa_ref, wa_out), (wb_ref, wb_out), (wc_ref, wc_out), (wd_ref, wd_out)):
        dst[...] = src[...].astype(BF16)
    ri = lax.broadcasted_iota(jnp.int32, (LANES, LANES), 0)
    ci = lax.broadcasted_iota(jnp.int32, (LANES, LANES), 1)
    eye = jnp.where(ri == ci, 1.0, 0.0).astype(BF16)
    gates = _sigmoid(g_ref[0])

    def gate_col(hk, br):
        cols = []
        for g in range(NSA_GROUP):
            c = COL_NG % LANES + (hk * NSA_GROUP + g) * N_BRANCH + br
            cols.append(gates[:, c:c + 1])
        return jnp.concatenate(cols, axis=0)

    @pl.when(qt == 0)
    def _():
        ones = jnp.ones((seq, HEAD_DIM), BF16)
        row = lax.broadcasted_iota(jnp.int32, (LANES, LANES), 0)
        is_far = jnp.logical_and(row >= AUG_FAR, row < AUG_FAR + 3)
        pad_aug = jnp.where(row == AUG_PAD, -MASK_BIG, jnp.where(is_far, 1.0, 0.0)).astype(BF16)
        win_aug = jnp.where(is_far, 1.0, 0.0).astype(BF16)
        lane512 = lax.broadcasted_iota(jnp.int32, (GROUP_ROWS, LANES), 1)
        causal = _tile4(jnp.where(ci <= ri, 0.0, -MASK_BIG))
        acc_scr[...] = jnp.zeros(acc_scr.shape, F32)
        edge_scr[...] = jnp.concatenate([jnp.full((GROUP_ROWS, LANES), -MASK_BIG, F32),
                                         _tile4(jnp.where(ri < ci, 0.0, -MASK_BIG))], axis=1)
        for hk in range(nkv):
            hc = slice(hk * HEAD_DIM, (hk + 1) * HEAD_DIM)
            far = tz_ref[2, hk]
            hi, mid, lo = _split3(far)
            caug = jnp.where(lane512 == AUG_FAR, hi, jnp.where(lane512 == AUG_FAR + 1, mid,
                             jnp.where(lane512 == AUG_FAR + 2, lo, jnp.where(lane512 == AUG_PAD, 1.0, 0.0))))
            caug_scr[hk] = caug.astype(BF16)
            qw_scr[hk, :, HEAD_DIM:] = caug.astype(BF16)
            near_scr[hk] = jnp.concatenate([tz_ref[1, hk] - far, (tz_ref[0, hk] - far) + causal], axis=1)
            ksa_scr[hk, 0, 0:HEAD_DIM, :] = jnp.zeros((HEAD_DIM, LANES), BF16)
            ksa_scr[hk, 0, HEAD_DIM:, :] = pad_aug
            for t in range(seq // LANES):
                rows = slice(t * LANES, (t + 1) * LANES)
                ksa_scr[hk, SEL_PAD_TILES + t, 0:HEAD_DIM, :] = ks_ref[0, rows, hc].T
                ksa_scr[hk, SEL_PAD_TILES + t, HEAD_DIM:, :] = jnp.where(
                    (t * LANES + ci) // SEL_BLOCK == ri, -MASK_BIG, jnp.where(is_far, 1.0, 0.0)).astype(BF16)
                kwp_scr[hk, WIN_PAD // LANES + t, 0:HEAD_DIM, :] = kw_ref[0, rows, hc].T
                kwp_scr[hk, WIN_PAD // LANES + t, HEAD_DIM:, :] = win_aug
            for t in range(WIN_PAD // LANES):
                kwp_scr[hk, t, 0:HEAD_DIM, :] = jnp.zeros((HEAD_DIM, LANES), BF16)
                kwp_scr[hk, t, HEAD_DIM:, :] = pad_aug
            vsa_scr[hk, 0:SEL_PAD_TILES * LANES, :] = jnp.zeros((SEL_PAD_TILES * LANES, 2 * HEAD_DIM), BF16)
            vsa_scr[hk, SEL_PAD_TILES * LANES:, 0:HEAD_DIM] = vs_ref[0, :, hc]
            vsa_scr[hk, SEL_PAD_TILES * LANES:, HEAD_DIM:] = ones
            vwa_scr[hk, 0:WIN_PAD, :] = jnp.zeros((WIN_PAD, 2 * HEAD_DIM), BF16)
            vwa_scr[hk, WIN_PAD:, 0:HEAD_DIM] = vw_ref[0, :, hc]
            vwa_scr[hk, WIN_PAD:, HEAD_DIM:] = ones

    def reset(br):
        m_scr[br] = jnp.full(m_scr.shape[1:], NEG, F32)

    def online_update(br, hk, s, vaug):
        m_prev = m_scr[br, hk]
        m_new = jnp.maximum(m_prev, jnp.max(s, axis=1, keepdims=True))
        alpha = jnp.exp2(m_prev - m_new)
        p = jnp.exp2(s - _lane_tile(m_new, s.shape[1] // LANES))
        pv = jnp.dot(p.astype(BF16), vaug, preferred_element_type=F32)
        acc_scr[br, hk] = _lane_tile(alpha, 2) * acc_scr[br, hk] + pv
        m_scr[br, hk] = m_new

    def finish(br, hk):
        acc = acc_scr[br, hk]
        return acc[:, 0:HEAD_DIM] / acc[:, HEAD_DIM:]

    for hk in range(nkv):
        for g in range(NSA_GROUP):
            h = hk * NSA_GROUP + g
            q_h = q_ref[0, :, h * HEAD_DIM:(h + 1) * HEAD_DIM]
            qg_scr[hk, g * LANES:(g + 1) * LANES, 0:HEAD_DIM] = q_h
            qw_scr[hk, g * LANES:(g + 1) * LANES, 0:HEAD_DIM] = q_h

    for hk in range(nkv):
        qg = qg_scr[hk, :, 0:HEAD_DIM]

        kc = kcvc_ref[0, 0, hk]
        vc = kcvc_ref[0, 1, hk]
        s = lax.dot_general(qg, kc, NT_DIMS, preferred_element_type=F32) + bc_ref[0, hk]
        rc = lax.broadcasted_iota(jnp.int32, (LANES, ncp), 0)
        cc = lax.broadcasted_iota(jnp.int32, (LANES, ncp), 1)
        valid_c = _tile4(jnp.where(q0 + rc - (CMP_STRIDE * cc + CMP_BLOCK - 1) >= 0, 1.0, 0.0)) > 0.5
        s = jnp.where(valid_c, s, NEG)
        p = jnp.where(valid_c, jnp.exp2(s - jnp.max(s, axis=1, keepdims=True)), 0.0)
        l = jnp.sum(p, axis=1, keepdims=True)
        p = p / jnp.where(l > 0.0, l, 1.0)
        o_scr[hk] = gate_col(hk, 0) * jnp.dot(p.astype(BF16), vc, preferred_element_type=F32)

        psum = p[0:LANES]
        for g in range(1, NSA_GROUP):
            psum = psum + p[g * LANES:(g + 1) * LANES]
        p_hi = psum.astype(BF16)
        p_lo = (psum - p_hi.astype(F32)).astype(BF16)
        selmt = selmt_ref[...]
        imp = (lax.dot_general(selmt, p_hi, NT_DIMS, preferred_element_type=F32)
               + lax.dot_general(selmt, p_lo, NT_DIMS, preferred_element_type=F32))
        imp = imp[0:n_slc]
        blk = lax.broadcasted_iota(jnp.int32, (n_slc, LANES), 0)
        cur = (q0 + lax.broadcasted_iota(jnp.int32, (n_slc, LANES), 1)) // SEL_BLOCK
        forced = (blk == 0) | (blk == cur) | (blk == cur - 1)
        imp = jnp.where(forced, FORCE, imp)
        imp = jnp.where(blk <= cur, imp, -jnp.inf)
        rank = jnp.zeros((n_slc, LANES), F32)
        for j in range(n_slc):
            row = imp[j:j + 1, :]
            beats = jnp.where(row > imp, 1.0, jnp.where(row == imp, jnp.where(blk > j, 1.0, 0.0), 0.0))
            rank = rank + beats
        sel_t = jnp.where(rank < top_n, jnp.where(imp > -jnp.inf, 1.0, 0.0), 0.0)
        if n_slc < LANES:
            sel_t = jnp.concatenate([sel_t, jnp.zeros((LANES - n_slc, LANES), F32)], axis=0)
        sel_q = lax.dot_general(eye, sel_t.astype(BF16), NT_DIMS, preferred_element_type=F32)
        not_sel = jnp.where(ci < n_slc, 1.0 - sel_q, 0.0).astype(BF16)
        for g in range(NSA_GROUP):
            rows = slice(g * LANES, (g + 1) * LANES)
            qg_scr[hk, rows, HEAD_DIM:] = jnp.where(ci < n_slc, not_sel, caug_scr[hk, rows, :])

    reset(WIN)
    for off, table in ((1, near_scr), (3, None), (5, edge_scr)):
        t0 = qt - off + WIN_PAD // LANES
        p0 = pl.multiple_of(t0 * LANES, LANES)
        for hk in range(nkv):
            k_t = jnp.concatenate([kwp_scr[hk, t0], kwp_scr[hk, t0 + 1]], axis=1)
            s = jnp.dot(qw_scr[hk], k_t, preferred_element_type=F32)
            if table is not None:
                s = s + (table[hk] if table is near_scr else table[...])
            online_update(WIN, hk, s, vwa_scr[hk, pl.ds(p0, KEY_CHUNK), :])

    reset(SEL)
    n_chunks = (qt + 2) // 2
    first_tile = SEL_PAD_TILES - (qt + 1) % 2

    def sel_scores(hk, c):
        t0 = first_tile + 2 * c
        k_t = jnp.concatenate([ksa_scr[hk, t0], ksa_scr[hk, t0 + 1]], axis=1)
        return jnp.dot(qg_scr[hk], k_t, preferred_element_type=F32)

    def sel_chunk(c, last):
        p0 = pl.multiple_of((first_tile + 2 * c) * LANES, LANES)
        for hk in range(nkv):
            s = s_scr[hk]
            if last:
                s = s + near_scr[hk]
            else:
                s_scr[hk] = sel_scores(hk, c + 1)
            online_update(SEL, hk, s, vsa_scr[hk, pl.ds(p0, KEY_CHUNK), :])

    def sel_trip(i, carry):
        for u in range(SEL_TRIP_CHUNKS):
            sel_chunk(SEL_TRIP_CHUNKS * i + u, False)
        return carry

    for hk in range(nkv):
        s_scr[hk] = sel_scores(hk, 0)
    n_far = n_chunks - 1
    n_trips = n_far // SEL_TRIP_CHUNKS
    lax.fori_loop(0, n_trips, sel_trip, 0)
    for rem in range(1, SEL_TRIP_CHUNKS):
        def leftover(rem=rem):
            for u in range(rem):
                sel_chunk(n_trips * SEL_TRIP_CHUNKS + u, False)
        pl.when(n_far % SEL_TRIP_CHUNKS == rem)(leftover)
    sel_chunk(n_chunks - 1, True)
    for hk in range(nkv):
        o = (o_scr[hk] + gate_col(hk, 1) * finish(SEL, hk)) + gate_col(hk, 2) * finish(WIN, hk)
        for g in range(NSA_GROUP):
            h = hk * NSA_GROUP + g
            o_ref[0, :, h * HEAD_DIM:(h + 1) * HEAD_DIM] = o[g * LANES:(g + 1) * LANES]


def _nsa_call(o16v, o32v, kcvc, bias_c, tz, selmt, weights_f32, batch, seq, ncp):
    nqt = seq // LANES
    wspecs = []
    for w in weights_f32:
        blk, span = _cast_block(w, batch * nqt)
        wspecs.append(pl.BlockSpec(blk, functools.partial(lambda b, t, span: ((b * nqt + t) // span, 0), span=span)))
    kvw = NSA_KV_WIDTH
    nkv = N_NSA_KV_HEADS
    base = (3 * FOX_WIDTH + NSA_WIDTH) // kvw
    return pl.pallas_call(
        functools.partial(_nsa_kernel, seq=seq, ncp=ncp),
        grid=(batch, nqt),
        in_specs=[pl.BlockSpec((1, LANES, NSA_WIDTH), lambda b, t: (b, t, 3 * FOX_WIDTH // NSA_WIDTH)),
                  pl.BlockSpec((1, seq, kvw), lambda b, t: (b, 0, base)),
                  pl.BlockSpec((1, seq, kvw), lambda b, t: (b, 0, base + 1)),
                  pl.BlockSpec((1, seq, kvw), lambda b, t: (b, 0, base + 2)),
                  pl.BlockSpec((1, seq, kvw), lambda b, t: (b, 0, base + 3)),
                  pl.BlockSpec((1, LANES, LANES), lambda b, t: (b, t, COL_NG // LANES)),
                  pl.BlockSpec((1, 2, N_NSA_KV_HEADS, ncp, HEAD_DIM), lambda b, t: (b, 0, 0, 0, 0)),
                  pl.BlockSpec((1, N_NSA_KV_HEADS, GROUP_ROWS, ncp), lambda b, t: (t, 0, 0, 0)),
                  pl.BlockSpec((3, N_NSA_KV_HEADS, GROUP_ROWS, LANES), lambda b, t: (0, 0, 0, 0)),
                  pl.BlockSpec((LANES, ncp), lambda b, t: (0, 0))] + wspecs,
        out_specs=[pl.BlockSpec((1, LANES, NSA_WIDTH), lambda b, t: (b, t, 0))] + wspecs,
        out_shape=[jax.ShapeDtypeStruct((batch, seq, NSA_WIDTH), F32)]
        + [jax.ShapeDtypeStruct(w.shape, BF16) for w in weights_f32],
        scratch_shapes=[pltpu.VMEM((nkv, GROUP_ROWS, 2 * HEAD_DIM), BF16),
                        pltpu.VMEM((nkv, GROUP_ROWS, 2 * HEAD_DIM), BF16),
                        pltpu.VMEM((nkv, GROUP_ROWS, LANES), BF16),
                        pltpu.VMEM((nkv, GROUP_ROWS, KEY_CHUNK), F32),
                        pltpu.VMEM((GROUP_ROWS, KEY_CHUNK), F32),
                        pltpu.VMEM((2, nkv, GROUP_ROWS, LANES), F32),
                        pltpu.VMEM((2, nkv, GROUP_ROWS, 2 * HEAD_DIM), F32),
                        pltpu.VMEM((nkv, GROUP_ROWS, HEAD_DIM), F32),
                        pltpu.VMEM((nkv, GROUP_ROWS, KEY_CHUNK), F32),
                        pltpu.VMEM((nkv, SEL_PAD_TILES + seq // LANES, 2 * HEAD_DIM, LANES), BF16),
                        pltpu.VMEM((nkv, SEL_PAD_TILES * LANES + seq, 2 * HEAD_DIM), BF16),
                        pltpu.VMEM((nkv, (seq + WIN_PAD) // LANES, 2 * HEAD_DIM, LANES), BF16),
                        pltpu.VMEM((nkv, seq + WIN_PAD, 2 * HEAD_DIM), BF16)],
        compiler_params=_params(2),
        name="nsa_attention",
    )(o16v, o16v, o16v, o16v, o16v, o32v, kcvc, bias_c, tz, selmt, *weights_f32)


def _out_kernel(of_ref, on_ref, gain_ref, w_ref, x_ref, g_ref, o_ref, y_scr, *, tm, n_chains):
    rows_per_chain = tm // n_chains
    gain_f = gain_ref[:, 0:FOX_WIDTH]
    gain_n = gain_ref[:, FOX_WIDTH:MIX_WIDTH]
    for c in range(n_chains):
        rows = slice(c * rows_per_chain, (c + 1) * rows_per_chain)
        for r0 in range(c * rows_per_chain, (c + 1) * rows_per_chain, NORM_ROWS):
            r = slice(r0, r0 + NORM_ROWS)
            y_scr[r, 0:FOX_WIDTH] = (_rms(of_ref[r, :]) * gain_f).astype(BF16)
            y_scr[r, FOX_WIDTH:MIX_WIDTH] = (_rms(on_ref[r, :]) * gain_n).astype(BF16)
        acc = jnp.dot(y_scr[rows, :], w_ref[...], preferred_element_type=F32)
        o_ref[rows, :] = x_ref[rows, :] + g_ref[0] * acc


def _out_call(o_fox, o_nsa, gain, w_out, x2, g1, seq):
    t, d = x2.shape
    tm = min(512, seq)
    rows_per_batch = seq // tm
    return pl.pallas_call(
        functools.partial(_out_kernel, tm=tm, n_chains=2),
        grid=(t // tm,),
        in_specs=[pl.BlockSpec((tm, FOX_WIDTH), lambda i: (i, 0)),
                  pl.BlockSpec((tm, NSA_WIDTH), lambda i: (i, 0)),
                  pl.BlockSpec((1, MIX_WIDTH), lambda i: (0, 0)),
                  pl.BlockSpec((MIX_WIDTH, d), lambda i: (0, 0)),
                  pl.BlockSpec((tm, d), lambda i: (i, 0)),
                  pl.BlockSpec((1, 1, d), lambda i: (i // rows_per_batch, 0, 0))],
        out_specs=pl.BlockSpec((tm, d), lambda i: (i, 0)),
        out_shape=jax.ShapeDtypeStruct((t, d), F32),
        scratch_shapes=[pltpu.VMEM((tm, MIX_WIDTH), BF16)],
        compiler_params=_params(1),
        name="out_proj",
    )(o_fox, o_nsa, gain, w_out, x2, g1)


def _ffn_kernel(x_ref, gain_ref, sc_ref, sh_ref, g_ref, wg_ref, wu_ref, wd_ref, o_ref, h_scr, *, n_f):
    f = pl.program_id(1)
    tm = h_scr.shape[0]

    def hidden_step(first_row, n_rows, first, last):
        rows = slice(first_row, first_row + n_rows)
        h = h_scr[rows, :]
        a = jnp.dot(h, wg_ref[...], preferred_element_type=F32)
        u = jnp.dot(h, wu_ref[...], preferred_element_type=F32)
        t = (a * _sigmoid(a)) * u
        part = jnp.dot(t.astype(BF16), wd_ref[...], preferred_element_type=F32)
        acc = part if first else o_ref[rows, :] + part
        o_ref[rows, :] = x_ref[rows, :] + g_ref[0] * acc if last else acc

    def edge_step(first, last):
        n_rows = tm // ROW_CHAINS
        for c in range(ROW_CHAINS):
            if first:
                _modulated_norm(x_ref, gain_ref, sc_ref, sh_ref, h_scr, c * n_rows, n_rows)
            hidden_step(c * n_rows, n_rows, first, last)

    if n_f == 1:
        edge_step(True, True)
    else:
        pl.when(f == 0)(functools.partial(edge_step, True, False))
        pl.when(jnp.logical_and(f > 0, f < n_f - 1))(functools.partial(hidden_step, 0, tm, False, False))
        pl.when(f == n_f - 1)(functools.partial(edge_step, False, True))


def _ffn_call(x1, gain2, sc, sh, g2, wg, wu, wd, seq):
    t, d = x1.shape
    dff = wg.shape[1]
    tm = min(1024, seq)
    tf = 512 if dff % 512 == 0 else dff
    rows_per_batch = seq // tm
    return pl.pallas_call(
        functools.partial(_ffn_kernel, n_f=dff // tf),
        grid=(t // tm, dff // tf),
        in_specs=[pl.BlockSpec((tm, d), lambda i, f: (i, 0)),
                  pl.BlockSpec((1, d), lambda i, f: (0, 0)),
                  pl.BlockSpec((1, 1, d), lambda i, f: (i // rows_per_batch, 0, 0)),
                  pl.BlockSpec((1, 1, d), lambda i, f: (i // rows_per_batch, 0, 0)),
                  pl.BlockSpec((1, 1, d), lambda i, f: (i // rows_per_batch, 0, 0)),
                  pl.BlockSpec((d, tf), lambda i, f: (0, f)),
                  pl.BlockSpec((d, tf), lambda i, f: (0, f)),
                  pl.BlockSpec((tf, d), lambda i, f: (f, 0))],
        out_specs=pl.BlockSpec((tm, d), lambda i, f: (i, 0)),
        out_shape=jax.ShapeDtypeStruct((t, d), F32),
        scratch_shapes=[pltpu.VMEM((tm, d), BF16)],
        compiler_params=_params(2),
        name="swiglu_ffn",
    )(x1, gain2, sc, sh, g2, wg, wu, wd)


def _selection_matrix_t(ncp, n_slc):
    r, q = SEL_BLOCK // CMP_STRIDE, CMP_BLOCK // CMP_STRIDE
    m = np.zeros((LANES, ncp), np.float32)
    for j in range(n_slc):
        for a in range(r):
            for b in range(q):
                c = r * j + a - b
                if 0 <= c < ncp:
                    m[j, c] += 1.0
    return m


def _w_in_block_sources():
    o = 0
    start = {}
    for name, width in (("fq", FOX_WIDTH), ("fk", FOX_WIDTH), ("fv", FOX_WIDTH), ("ff", N_FOX_HEADS),
                        ("nq", NSA_WIDTH), ("nk", N_BRANCH * NSA_KV_WIDTH), ("nv", N_BRANCH * NSA_KV_WIDTH),
                        ("ng", N_BRANCH * N_NSA_HEADS)):
        start[name] = o
        o += width
    kvw = NSA_KV_WIDTH
    groups = [(start["fq"], 3 * FOX_WIDTH), (start["nq"], NSA_WIDTH),
              (start["nk"] + kvw, 2 * kvw), (start["nv"] + kvw, 2 * kvw),
              (start["nk"], kvw), (start["nv"], kvw)]
    blocks = [s + LANES * b for s, width in groups for b in range(width // LANES)]
    return blocks, (start["ff"], N_FOX_HEADS), (start["ng"], N_BRANCH * N_NSA_HEADS)


def _repack_kernel(src_ref, wt_hbm, o16_ref, o32_ref, buf, sem, *, n16, n_whole, ff, ng):
    k = pl.program_id(0)
    n_slots = buf.shape[0]
    ahead = n_slots - 1
    slot = k % n_slots

    def whole_copy(kk, s):
        r0 = pl.multiple_of(src_ref[kk], 8)
        return pltpu.make_async_copy(wt_hbm.at[pl.ds(r0, LANES), :], buf.at[s], sem.at[s])

    def narrow_copies(s):
        return [pltpu.make_async_copy(wt_hbm.at[pl.ds(ff[0], ff[1]), :], buf.at[s, pl.ds(0, ff[1]), :], sem.at[s]),
                pltpu.make_async_copy(wt_hbm.at[pl.ds(ng[0], ng[1]), :], buf.at[s, pl.ds(ff[1], ng[1]), :],
                                      sem.at[s])]

    def start_block(kk):
        @pl.when(kk < n_whole)
        def _():
            whole_copy(kk, kk % n_slots).start()

        @pl.when(kk == n_whole)
        def _():
            for cp in narrow_copies(kk % n_slots):
                cp.start()

    @pl.when(k == 0)
    def _():
        for kk in range(ahead):
            start_block(kk)

    start_block(k + ahead)

    @pl.when(k < n_whole)
    def _():
        whole_copy(k, slot).wait()

    @pl.when(k == n_whole)
    def _():
        for cp in narrow_copies(slot):
            cp.wait()

    xt = buf[slot].T
    lane = lax.broadcasted_iota(jnp.int32, xt.shape, 1)
    xt = jnp.where(jnp.logical_or(k < n_whole, lane < ff[1] + ng[1]), xt, 0.0).astype(BF16)

    @pl.when(k < n16)
    def _():
        o16_ref[...] = xt

    @pl.when(k >= n16)
    def _():
        o32_ref[...] = xt


def _repack_w_in(wt):
    n, d = wt.shape
    blocks, ff, ng = _w_in_block_sources()
    n16 = W16 // LANES
    n_whole = len(blocks)
    assert n_whole + 1 == (W16 + W32) // LANES
    return pl.pallas_call(
        functools.partial(_repack_kernel, n16=n16, n_whole=n_whole, ff=ff, ng=ng),
        grid_spec=pltpu.PrefetchScalarGridSpec(
            num_scalar_prefetch=1,
            grid=(n_whole + 1,),
            in_specs=[pl.BlockSpec(memory_space=pl.ANY)],
            out_specs=[pl.BlockSpec((d, LANES), lambda k, src: (0, jnp.minimum(k, n16 - 1))),
                       pl.BlockSpec((d, LANES), lambda k, src: (0, jnp.maximum(k - n16, 0)))],
            scratch_shapes=[pltpu.VMEM((REPACK_SLOTS, LANES, d), F32), pltpu.SemaphoreType.DMA((REPACK_SLOTS,))]),
        out_shape=[jax.ShapeDtypeStruct((d, W16), BF16), jax.ShapeDtypeStruct((d, W32), BF16)],
        compiler_params=_params(1),
        name="w_in_repack",
    )(jnp.asarray(blocks, jnp.int32), wt)


def kernel(x, c, ada_w, ada_b, norm1_gain, norm2_gain, w_in, fox_f_bias, fox_q_gain, fox_k_gain, nsa_q_gain,
           nsa_k_gain, nsa_cmp_pos, nsa_cmp_w1, nsa_cmp_w2, rel_bias, mix_out_gain, w_out, ffn_w_gate, ffn_w_up,
           ffn_w_down):
    batch, seq, d = x.shape
    assert seq % KEY_CHUNK == 0 and seq >= WINDOW and d % LANES == 0 and seq // SEL_BLOCK <= AUG_FAR
    depth = ada_w.shape[0]
    nqt = seq // LANES
    ncp = -(-(seq // CMP_STRIDE) // LANES) * LANES

    selmt = jnp.asarray(_selection_matrix_t(ncp, seq // SEL_BLOCK), BF16)
    rb_flat = rel_bias.reshape(-1)
    bias_c = _bias_call(rb_flat, nqt, ncp, CMP_STRIDE, CMP_BLOCK - 1, "t5_bias_compressed", rolled=True)
    tz = _bias_call(rb_flat, 3, LANES, 1, 0, "t5_bias_toeplitz")

    ones_h = jnp.ones((HEAD_DIM,), F32)
    c_pad = jnp.pad(c, ((0, 8 - batch % 8 if batch % 8 else 0), (0, 0)))
    x2 = x.reshape(batch * seq, d)
    for layer in range(depth):
        mod = _ada_call(c_pad, ada_w[layer], ada_b[layer][None, :])[:batch]
        sh1, sc1, g1, sh2, sc2, g2 = [mod[:, i * d:(i + 1) * d][:, None, :] for i in range(N_MOD)]

        kg = nsa_k_gain[layer]
        col_gain = jnp.concatenate([
            jnp.tile(fox_q_gain[layer] * QSCALE, N_FOX_HEADS), jnp.tile(fox_k_gain[layer], N_FOX_HEADS),
            jnp.tile(ones_h, N_FOX_HEADS), jnp.tile(nsa_q_gain[layer] * QSCALE, N_NSA_HEADS),
            jnp.tile(kg[1], N_NSA_KV_HEADS), jnp.tile(kg[2], N_NSA_KV_HEADS),
            jnp.tile(ones_h, 2 * N_NSA_KV_HEADS)])[None, :]
        col_flag = jnp.concatenate([
            jnp.ones((2 * FOX_WIDTH,), F32), jnp.zeros((FOX_WIDTH,), F32), jnp.ones((NSA_WIDTH,), F32),
            jnp.ones((2 * NSA_KV_WIDTH,), F32), jnp.zeros((2 * NSA_KV_WIDTH,), F32)])[None, :]
        w16, w32 = _repack_w_in(jnp.swapaxes(w_in, 1, 2)[layer])
        o16, o32 = _proj_call(x2, sc1, sh1, norm1_gain[layer][None, :], w16, w32,
                              col_gain, col_flag, seq)
        o16v = o16.reshape(batch, seq, W16)
        o32v = o32.reshape(batch, seq, W32)

        fb_pad = jnp.pad(fox_f_bias[layer], (0, LANES - N_FOX_HEADS))[None, :]
        cs = _cum_call(o32, fb_pad, batch, seq, N_FOX_HEADS // FOX_HEADS_PER_STEP)
        o_fox = _fox_call(o16v, cs, batch, seq)

        w1 = nsa_cmp_w1[layer].reshape(2, CMP_BLOCK, HEAD_DIM, HEAD_DIM)
        kcvc = _cmp_call(o32, nsa_cmp_pos[layer], w1, nsa_cmp_w2[layer], kg[0][None, :],
                         batch, seq, ncp)
        o_nsa, wg16, wu16, wo16, wd16 = _nsa_call(
            o16v, o32v, kcvc, bias_c, tz, selmt,
            (ffn_w_gate[layer], ffn_w_up[layer], w_out[layer], ffn_w_down[layer]), batch, seq, ncp)

        x1 = _out_call(o_fox.reshape(batch * seq, FOX_WIDTH), o_nsa.reshape(batch * seq, NSA_WIDTH),
                       mix_out_gain[layer][None, :], wo16, x2, g1, seq)
        x2 = _ffn_call(x1, norm2_gain[layer][None, :], sc2, sh2, g2, wg16, wu16, wd16, seq)
    return x2.reshape(batch, seq, d)
```
